```python
import math
import jax
import jax.numpy as jnp
from jax import lax
import numpy as np

D_MODEL = 1024
BATCH = 8
SEQ = 2048
DEPTH = 1
DEC_BATCH = 128
DEC_SEQ = 8
PAST_LEN = 16384
PAGE_SIZE = 128

SSM_EXPAND = 2
SSM_INNER = SSM_EXPAND * D_MODEL
SSM_HEAD_DIM = 64
SSM_HEADS = SSM_INNER // SSM_HEAD_DIM
SSM_GROUPS = 2
SSM_HEADS_PER_GROUP = SSM_HEADS // SSM_GROUPS
SSM_STATE = 128
SSM_CONV_CH = SSM_INNER + 2 * SSM_GROUPS * SSM_STATE
DN_HEADS = 8
DN_HEAD_K = 128
DN_HEAD_V = 128
DN_QK = DN_HEADS * DN_HEAD_K
DN_V = DN_HEADS * DN_HEAD_V
DN_CONV_CH = 2 * DN_QK + DN_V
CONV_WIDTH = 4
CHUNK = 64
DT_MIN = 0.001
DT_MAX = 0.1
IN_SIZES = (SSM_INNER, SSM_CONV_CH, SSM_HEADS, DN_CONV_CH, DN_HEADS, DN_HEADS, DN_V, D_MODEL, D_MODEL)
IN_SPLITS = tuple(int(s) for s in np.cumsum(IN_SIZES)[:-1])
N_IN = sum(IN_SIZES)
N_EXPERTS = 64
TOP_K = 8
N_EXPERT_GROUPS = 8
TOPK_GROUPS = 4
EXPERT_FF = 256
SHARED_FF = 256
ROUTED_SCALE = 2.5
PLE_DIM = 256
LN_EPS = 1e-5
RMS_EPS = 1e-6
L2_EPS = 1e-6
DEEPNORM_ALPHA = (2 * DEPTH) ** 0.25
DEEPNORM_BETA = (8 * DEPTH) ** -0.25

kernel_name = "hybrid_ssd_gdn_moe_step"


def layer_norm(x, g, b):
    xf = x.astype(jnp.float32)
    mu = jnp.mean(xf, axis=-1, keepdims=True)
    var = jnp.mean(jnp.square(xf - mu), axis=-1, keepdims=True)
    y = (xf - mu) * lax.rsqrt(var + LN_EPS) * g.astype(jnp.float32) + b.astype(jnp.float32)
    return y.astype(x.dtype)


def rms_normalize(xf):
    return xf * lax.rsqrt(jnp.mean(jnp.square(xf), axis=-1, keepdims=True) + RMS_EPS)


def l2_normalize(xf):
    return xf * lax.rsqrt(jnp.sum(jnp.square(xf), axis=-1, keepdims=True) + L2_EPS)


def causal_conv(x, buf, w):
    seq_len = x.shape[1]
    xp = jnp.concatenate([buf.astype(x.dtype), x], axis=1)
    y = xp[:, 0:seq_len] * w[0]
    for tap in range(1, CONV_WIDTH):
        y = y + xp[:, tap:tap + seq_len] * w[tap]
    return y, xp[:, xp.shape[1] - (CONV_WIDTH - 1):]


def to_chunks(a, c):
    b, l = a.shape[:2]
    return jnp.moveaxis(a.reshape(b, l // c, c, *a.shape[2:]), 1, 0)


def from_chunks(a):
    nc, b, c = a.shape[:3]
    return jnp.moveaxis(a, 0, 1).reshape(b, nc * c, *a.shape[3:])


def ssd_scan(x, dt, a, bm, cm, h0):
    c = math.gcd(x.shape[1], CHUNK)
    tri = jnp.tril(jnp.ones((c, c), dtype=bool))[None, :, :, None, None]

    def step(h, inp):
        xc, dtc, bc, cc = inp
        cum = jnp.cumsum(dtc * a, axis=1)
        seg = cum[:, :, None] - cum[:, None]
        decay = jnp.exp(jnp.where(tri, seg, -jnp.inf))
        cb = jnp.einsum('bign,bjgn->bijg', cc, bc)
        wts = cb[..., None] * decay * dtc[:, None]
        y = jnp.einsum('bijgh,bjghp->bighp', wts, xc)
        y = y + jnp.einsum('bign,bghpn->bighp', cc, h) * jnp.exp(cum)[..., None]
        wend = jnp.exp(cum[:, -1:] - cum) * dtc
        h = jnp.exp(cum[:, -1])[..., None, None] * h + jnp.einsum('bjgh,bjgn,bjghp->bghpn', wend, bc, xc)
        return h, y

    h, ys = lax.scan(step, h0, (to_chunks(x, c), to_chunks(dt, c), to_chunks(bm, c), to_chunks(cm, c)))
    return from_chunks(ys), h


def gdn_scan(q, k, v, g, beta, s0):
    c = math.gcd(q.shape[1], CHUNK)
    dv = v.shape[-1]
    strict = jnp.tril(jnp.ones((c, c), dtype=bool), -1)
    incl = jnp.tril(jnp.ones((c, c), dtype=bool))

    def step(s, inp):
        qc, kc, vc, gc, bc = (jnp.swapaxes(t, 1, 2) for t in inp)
        gcum = jnp.cumsum(gc, axis=-1)
        seg = gcum[..., :, None] - gcum[..., None, :]
        kb = kc * bc[..., None]
        lmat = jnp.einsum('bhid,bhjd->bhij', kb, kc) * jnp.exp(jnp.where(strict, seg, -jnp.inf))
        rhs = jnp.concatenate([vc * bc[..., None], kb * jnp.exp(gcum)[..., None]], axis=-1)
        sol = lax.linalg.triangular_solve(lmat, rhs, left_side=True, lower=True, unit_diagonal=True)
        u = sol[..., :dv] - jnp.einsum('bhik,bhkv->bhiv', sol[..., dv:], s)
        qk = jnp.einsum('bhik,bhjk->bhij', qc, kc) * jnp.exp(jnp.where(incl, seg, -jnp.inf))
        o = jnp.einsum('bhik,bhkv->bhiv', qc * jnp.exp(gcum)[..., None], s) + jnp.einsum('bhij,bhjv->bhiv', qk, u)
        glast = gcum[..., -1]
        s = jnp.exp(glast)[..., None, None] * s + jnp.einsum(
            'bhjk,bhjv->bhkv', kc * jnp.exp(glast[..., None] - gcum)[..., None], u)
        return s, jnp.swapaxes(o, 1, 2)

    s, os_ = lax.scan(step, s0, tuple(to_chunks(t, c) for t in (q, k, v, g, beta)))
    return from_chunks(os_), s


def mixer(h, conv_a_buf, ssm_h, conv_b_buf, dn_s, lw):
    bsz, seq_len, _ = h.shape
    f32 = jnp.float32
    proj = h @ lw['w_in']
    z_a, xbc, dt_a, qkv, a_b, b_b, z_b, gate_a, gate_b = jnp.split(proj, IN_SPLITS, axis=-1)

    xbc, conv_a_new = causal_conv(xbc, conv_a_buf, lw['conv_a_w'])
    xbc = jax.nn.silu((xbc + lw['conv_a_b']).astype(f32))
    xs, bm, cm = jnp.split(xbc, (SSM_INNER, SSM_INNER + SSM_GROUPS * SSM_STATE), axis=-1)
    xs = xs.reshape(bsz, seq_len, SSM_GROUPS, SSM_HEADS_PER_GROUP, SSM_HEAD_DIM)
    bm = bm.reshape(bsz, seq_len, SSM_GROUPS, SSM_STATE)
    cm = cm.reshape(bsz, seq_len, SSM_GROUPS, SSM_STATE)
    dt = jax.nn.softplus(dt_a.astype(f32) + lw['ssm_dt_bias'].astype(f32))
    dt = dt.reshape(bsz, seq_len, SSM_GROUPS, SSM_HEADS_PER_GROUP)
    a_ssm = -jnp.exp(lw['ssm_a_log'].astype(f32)).reshape(SSM_GROUPS, SSM_HEADS_PER_GROUP)
    h0 = ssm_h.astype(f32).reshape(bsz, SSM_GROUPS, SSM_HEADS_PER_GROUP, SSM_HEAD_DIM, SSM_STATE)
    y, ssm_new = ssd_scan(xs, dt, a_ssm, bm, cm, h0)
    y = y + lw['ssm_d'].astype(f32).reshape(SSM_GROUPS, SSM_HEADS_PER_GROUP)[..., None] * xs
    y = y.reshape(bsz, seq_len, SSM_GROUPS, SSM_INNER // SSM_GROUPS)
    y = rms_normalize(y * jax.nn.silu(z_a.astype(f32)).reshape(y.shape))
    y = y.reshape(bsz, seq_len, SSM_INNER) * lw['ssm_norm_w'].astype(f32)
    out_a = y.astype(h.dtype) @ lw['w_a']

    qkv, conv_b_new = causal_conv(qkv, conv_b_buf, lw['conv_b_w'])
    qkv = jax.nn.silu(qkv.astype(f32))
    q, k, v = jnp.split(qkv, (DN_QK, 2 * DN_QK), axis=-1)
    q = l2_normalize(q.reshape(bsz, seq_len, DN_HEADS, DN_HEAD_K)) * (DN_HEAD_K ** -0.5)
    k = l2_normalize(k.reshape(bsz, seq_len, DN_HEADS, DN_HEAD_K))
    v = v.reshape(bsz, seq_len, DN_HEADS, DN_HEAD_V)
    beta = jax.nn.sigmoid(b_b.astype(f32))
    g = -jnp.exp(lw['dn_a_log'].astype(f32)) * jax.nn.softplus(a_b.astype(f32) + lw['dn_dt_bias'].astype(f32))
    o, dn_new = gdn_scan(q, k, v, g, beta, dn_s.astype(f32))
    o = rms_normalize(o) * lw['dn_norm_w'].astype(f32) * jax.nn.silu(
        z_b.astype(f32).reshape(bsz, seq_len, DN_HEADS, DN_HEAD_V))
    out_b = o.reshape(bsz, seq_len, DN_V).astype(h.dtype) @ lw['w_b']

    merged = jax.nn.sigmoid(gate_a) * out_a + jax.nn.sigmoid(gate_b) * out_b
    new_states = (conv_a_new,
                  ssm_new.reshape(bsz, SSM_HEADS, SSM_HEAD_DIM, SSM_STATE).astype(ssm_h.dtype),
                  conv_b_new,
                  dn_new.astype(dn_s.dtype))
    return merged @ lw['w_o'], new_states


def moe(h, lw):
    n_tok = h.shape[0]
    scores = jax.nn.sigmoid((h @ lw['router_w']).astype(jnp.float32))
    sel = scores + lw['router_bias'].astype(jnp.float32)
    grp = sel.reshape(n_tok, N_EXPERT_GROUPS, N_EXPERTS // N_EXPERT_GROUPS)
    grp_score = jnp.sum(lax.top_k(grp, 2)[0], axis=-1)
    _, gidx = lax.top_k(grp_score, TOPK_GROUPS)
    gmask = jnp.sum(jax.nn.one_hot(gidx, N_EXPERT_GROUPS, dtype=jnp.float32), axis=1) > 0
    emask = jnp.repeat(gmask, N_EXPERTS // N_EXPERT_GROUPS, axis=1)
    _, eidx = lax.top_k(jnp.where(emask, sel, -jnp.inf), TOP_K)
    gate = jnp.take_along_axis(scores, eidx, axis=1)
    gate = gate / jnp.sum(gate, axis=-1, keepdims=True) * ROUTED_SCALE
    flat_e = eidx.reshape(-1)
    order = jnp.argsort(flat_e)
    tok = order // TOP_K
    group_sizes = jnp.bincount(flat_e, length=N_EXPERTS).astype(jnp.int32)
    xs = h[tok]
    hid = jax.nn.silu(lax.ragged_dot(xs, lw['exp_w1'], group_sizes)) * lax.ragged_dot(xs, lw['exp_w3'], group_sizes)
    out = lax.ragged_dot(hid, lw['exp_w2'], group_sizes) * gate.reshape(-1)[order][:, None].astype(h.dtype)
    routed = jnp.zeros_like(h).at[tok].add(out)
    shared = (jax.nn.silu(h @ lw['sh_w1']) * (h @ lw['sh_w3'])) @ lw['sh_w2']
    return routed + shared


def decoder_layer(x, p, conv_a_buf, ssm_h, conv_b_buf, dn_s, lw):
    bsz, seq_len, dm = x.shape
    m, new_states = mixer(x, conv_a_buf, ssm_h, conv_b_buf, dn_s, lw)
    x = layer_norm(DEEPNORM_ALPHA * x + m, lw['ln1_g'], lw['ln1_b'])
    c = moe(x.reshape(bsz * seq_len, dm), lw).reshape(bsz, seq_len, dm)
    e = (p.astype(x.dtype) @ lw['ple_w']) * jax.nn.sigmoid(x @ lw['ple_gate_w'])
    x = layer_norm(DEEPNORM_ALPHA * x + c + e, lw['ln2_g'], lw['ln2_b'])
    return x, new_states


def setup_inputs(seed: int = 0) -> dict:
    key = jax.random.key(seed)
    ks = iter(jax.random.split(key, 64))

    def nrm(shape, scale):
        return jax.random.normal(next(ks), shape, jnp.float32) * scale

    def dt_bias(shape):
        u = jax.random.uniform(next(ks), shape, jnp.float32)
        dt = jnp.exp(u * (math.log(DT_MAX) - math.log(DT_MIN)) + math.log(DT_MIN))
        return dt + jnp.log(-jnp.expm1(-dt))

    def a_log(shape):
        return jnp.log(jax.random.uniform(next(ks), shape, jnp.float32, 1.0, 16.0))

    def gain(shape):
        return 1.0 + nrm(shape, 0.02)

    L = DEPTH
    return {
        'x_prompt': nrm((BATCH, SEQ, D_MODEL), 1.0),
        'x_sample': nrm((DEC_BATCH, DEC_SEQ, D_MODEL), 1.0),
        'p_prompt': nrm((DEPTH, BATCH, SEQ, PLE_DIM), 1.0),
        'p_sample': nrm((DEPTH, DEC_BATCH, DEC_SEQ, PLE_DIM), 1.0),
        'state_ssm_conv': nrm((DEPTH, DEC_BATCH, CONV_WIDTH - 1, SSM_CONV_CH), 1.0),
        'state_ssm': nrm((DEPTH, DEC_BATCH, SSM_HEADS, SSM_HEAD_DIM, SSM_STATE), 0.1),
        'state_dn_conv': nrm((DEPTH, DEC_BATCH, CONV_WIDTH - 1, DN_CONV_CH), 1.0),
        'state_dn': nrm((DEPTH, DEC_BATCH, DN_HEADS, DN_HEAD_K, DN_HEAD_V), 0.1),
        'emb_ln_g': gain((D_MODEL,)),
        'emb_ln_b': nrm((D_MODEL,), 0.02),
        'w_in': nrm((L, D_MODEL, N_IN), D_MODEL ** -0.5),
        'conv_a_w': nrm((L, CONV_WIDTH, SSM_CONV_CH), 0.5),
        'conv_a_b': nrm((L, SSM_CONV_CH), 0.02),
        'ssm_dt_bias': dt_bias((L, SSM_HEADS)),
        'ssm_a_log': a_log((L, SSM_HEADS)),
        'ssm_d': gain((L, SSM_HEADS)),
        'ssm_norm_w': gain((L, SSM_INNER)),
        'w_a': nrm((L, SSM_INNER, D_MODEL), SSM_INNER ** -0.5 * DEEPNORM_BETA),
        'conv_b_w': nrm((L, CONV_WIDTH, DN_CONV_CH), 0.5),
        'dn_dt_bias': dt_bias((L, DN_HEADS)),
        'dn_a_log': a_log((L, DN_HEADS)),
        'dn_norm_w': gain((L, DN_HEAD_V)),
        'w_b': nrm((L, DN_V, D_MODEL), DN_V ** -0.5 * DEEPNORM_BETA),
        'w_o': nrm((L, D_MODEL, D_MODEL), D_MODEL ** -0.5 * DEEPNORM_BETA),
        'ln1_g': gain((L, D_MODEL)),
        'ln1_b': nrm((L, D_MODEL), 0.02),
        'router_w': nrm((L, D_MODEL, N_EXPERTS), D_MODEL ** -0.5),
        'router_bias': nrm((L, N_EXPERTS), 0.01),
        'exp_w1': nrm((L, N_EXPERTS, D_MODEL, EXPERT_FF), D_MODEL ** -0.5),
        'exp_w3': nrm((L, N_EXPERTS, D_MODEL, EXPERT_FF), D_MODEL ** -0.5),
        'exp_w2': nrm((L, N_EXPERTS, EXPERT_FF, D_MODEL), EXPERT_FF ** -0.5 * DEEPNORM_BETA),
        'sh_w1': nrm((L, D_MODEL, SHARED_FF), D_MODEL ** -0.5),
        'sh_w3': nrm((L, D_MODEL, SHARED_FF), D_MODEL ** -0.5),
        'sh_w2': nrm((L, SHARED_FF, D_MODEL), SHARED_FF ** -0.5 * DEEPNORM_BETA),
        'ple_w': nrm((L, PLE_DIM, D_MODEL), PLE_DIM ** -0.5 * DEEPNORM_BETA),
        'ple_gate_w': nrm((L, D_MODEL, D_MODEL), D_MODEL ** -0.5),
        'ln2_g': gain((L, D_MODEL)),
        'ln2_b': nrm((L, D_MODEL), 0.02),
    }


def reference(x_prompt, x_sample, p_prompt, p_sample, state_ssm_conv, state_ssm, state_dn_conv, state_dn,
              emb_ln_g, emb_ln_b, w_in, conv_a_w, conv_a_b, ssm_dt_bias, ssm_a_log, ssm_d, ssm_norm_w, w_a,
              conv_b_w, dn_dt_bias, dn_a_log, dn_norm_w, w_b, w_o, ln1_g, ln1_b, router_w, router_bias,
              exp_w1, exp_w3, exp_w2, sh_w1, sh_w3, sh_w2, ple_w, ple_gate_w, ln2_g, ln2_b):
    bp = x_prompt.shape[0]
    dt_p = x_prompt.dtype
    xp = layer_norm(x_prompt, emb_ln_g, emb_ln_b)
    xs = layer_norm(x_sample, emb_ln_g, emb_ln_b)
    pa_conv, pa_ssm, pb_conv, pb_dn = [], [], [], []
    sa_conv, sa_ssm, sb_conv, sb_dn = [], [], [], []
    for i in range(DEPTH):
        lw = {
            'w_in': w_in[i], 'conv_a_w': conv_a_w[i], 'conv_a_b': conv_a_b[i],
            'ssm_dt_bias': ssm_dt_bias[i], 'ssm_a_log': ssm_a_log[i], 'ssm_d': ssm_d[i],
            'ssm_norm_w': ssm_norm_w[i], 'w_a': w_a[i], 'conv_b_w': conv_b_w[i],
            'dn_dt_bias': dn_dt_bias[i], 'dn_a_log': dn_a_log[i], 'dn_norm_w': dn_norm_w[i],
            'w_b': w_b[i], 'w_o': w_o[i], 'ln1_g': ln1_g[i], 'ln1_b': ln1_b[i],
            'router_w': router_w[i], 'router_bias': router_bias[i],
            'exp_w1': exp_w1[i], 'exp_w3': exp_w3[i], 'exp_w2': exp_w2[i],
            'sh_w1': sh_w1[i], 'sh_w3': sh_w3[i], 'sh_w2': sh_w2[i],
            'ple_w': ple_w[i], 'ple_gate_w': ple_gate_w[i], 'ln2_g': ln2_g[i], 'ln2_b': ln2_b[i],
        }
        xp, (c_a, h_a, c_b, s_b) = decoder_layer(
            xp, p_prompt[i],
            jnp.zeros((bp, CONV_WIDTH - 1, SSM_CONV_CH), dt_p),
            jnp.zeros((bp, SSM_HEADS, SSM_HEAD_DIM, SSM_STATE), dt_p),
            jnp.zeros((bp, CONV_WIDTH - 1, DN_CONV_CH), dt_p),
            jnp.zeros((bp, DN_HEADS, DN_HEAD_K, DN_HEAD_V), dt_p),
            lw)
        pa_conv.append(c_a); pa_ssm.append(h_a); pb_conv.append(c_b); pb_dn.append(s_b)
        xs, (c_a, h_a, c_b, s_b) = decoder_layer(
            xs, p_sample[i], state_ssm_conv[i], state_ssm[i], state_dn_conv[i], state_dn[i], lw)
        sa_conv.append(c_a); sa_ssm.append(h_a); sb_conv.append(c_b); sb_dn.append(s_b)
    return (xp, xs,
            jnp.stack(pa_conv), jnp.stack(pa_ssm), jnp.stack(pb_conv), jnp.stack(pb_dn),
            jnp.stack(sa_conv), jnp.stack(sa_ssm), jnp.stack(sb_conv), jnp.stack(sb_dn))
```

```python
import functools
import math

import jax
import jax.numpy as jnp
from jax import lax
from jax.experimental import pallas as pl
from jax.experimental.pallas import tpu as pltpu

F32 = jnp.float32
BF16 = jnp.bfloat16
HIGHEST = lax.Precision.HIGHEST

D_MODEL = 1024
SSM_INNER = 2048
SSM_HEAD_DIM = 64
SSM_HEADS = 32
SSM_GROUPS = 2
SSM_HEADS_PER_GROUP = 16
SSM_STATE = 128
SSM_CONV_CH = 2560
DN_HEADS = 8
DN_HEAD = 128
DN_QK = 1024
DN_V = 1024
DN_CONV_CH = 3072
CONV_WIDTH = 4
CHUNK = 64
N_EXPERTS = 64
TOP_K = 8
N_EXPERT_GROUPS = 8
EXPERTS_PER_GROUP = 8
TOPK_GROUPS = 4
EXPERT_FF = 256
SHARED_FF = 256
ROUTED_SCALE = 2.5
PLE_DIM = 256
LN_EPS = 1e-5
RMS_EPS = 1e-6
L2_EPS = 1e-6
DEEPNORM_ALPHA = 2.0 ** 0.25

LANES = 128
SUBLANES = 8
VMEM_LIMIT = 56 * 1024 * 1024
SMALL_DT = 0
SMALL_A = 32
SMALL_B = 40
CONV_PAD = 8


def _sigmoid(x):
    return 1.0 / (1.0 + jnp.exp(-x))


def _silu(x):
    return x * _sigmoid(x)


def _softplus(x):
    return jnp.maximum(x, 0.0) + jnp.log1p(jnp.exp(-jnp.abs(x)))


def _layer_norm(x, g, b):
    mu = jnp.mean(x, axis=-1, keepdims=True)
    xc = x - mu
    var = jnp.mean(xc * xc, axis=-1, keepdims=True)
    return xc * lax.rsqrt(var + LN_EPS) * g + b


def _dot(a, b):
    return jnp.dot(a, b, preferred_element_type=F32)


def _dot_nt(a, b):
    return lax.dot_general(a, b, (((1,), (1,)), ((), ())), preferred_element_type=F32)


def _dot_tn(a, b):
    return lax.dot_general(a, b, (((0,), (0,)), ((), ())), preferred_element_type=F32)


def _dot_f32(a, b):
    return jnp.dot(a, b, precision=HIGHEST, preferred_element_type=F32)


def _params(*sem):
    return pltpu.CompilerParams(dimension_semantics=sem, vmem_limit_bytes=VMEM_LIMIT)


def _row_tile(m, preferred):
    return max(d for d in range(SUBLANES, min(m, preferred) + 1, SUBLANES) if m % d == 0)


def _ln_matmul_kernel(x_ref, g_ref, b_ref, w_ref, o_ref, h_scr):
    @pl.when(pl.program_id(1) == 0)
    def _():
        h_scr[...] = _layer_norm(x_ref[...], g_ref[...], b_ref[...]).astype(BF16)

    o_ref[...] = _dot(h_scr[...], w_ref[...])


def _ln_matmul(x, g, b, w, tm, tn):
    m, k = x.shape
    n = w.shape[1]
    tm = _row_tile(m, tm)
    return pl.pallas_call(
        _ln_matmul_kernel,
        grid=(m // tm, n // tn),
        in_specs=[
            pl.BlockSpec((tm, k), lambda i, j: (i, 0)),
            pl.BlockSpec((1, k), lambda i, j: (0, 0)),
            pl.BlockSpec((1, k), lambda i, j: (0, 0)),
            pl.BlockSpec((k, tn), lambda i, j: (0, j)),
        ],
        out_specs=pl.BlockSpec((tm, tn), lambda i, j: (i, j)),
        out_shape=jax.ShapeDtypeStruct((m, n), F32),
        scratch_shapes=[pltpu.VMEM((tm, k), BF16)],
        compiler_params=_params("parallel", "arbitrary"),
        name="ln_matmul",
    )(x, g, b, w)


def _causal_conv(cbuf, xpre, cw, t):
    cbuf[CONV_PAD:CONV_PAD + t, :] = xpre
    lo = CONV_PAD - (CONV_WIDTH - 1)
    y = cbuf[lo:lo + t, :] * cw[0:1]
    y = y + cbuf[lo + 1:lo + 1 + t, :] * cw[1:2]
    y = y + cbuf[lo + 2:lo + 2 + t, :] * cw[2:3]
    y = y + xpre * cw[3:4]
    tail = cbuf[lo + t:CONV_PAD + t, :]
    cbuf[lo:CONV_PAD, :] = tail
    return y, tail


def _lane_expand(v, h0, count, width):
    t = v.shape[0]
    n = count * width
    out = jnp.broadcast_to(v[:, h0:h0 + 1], (t, n))
    if count > 1:
        lane = lax.broadcasted_iota(jnp.int32, (t, n), 1)
        for i in range(1, count):
            out = jnp.where(lane >= i * width, jnp.broadcast_to(v[:, h0 + i:h0 + i + 1], (t, n)), out)
    return out


def _grouped_transpose(v, per_tile):
    blocks = [v if r == 0 else pltpu.roll(v, LANES - r, axis=1) for r in range(per_tile)]
    return jnp.concatenate(blocks, axis=0).T


def _cumsum_rows(v, t):
    ii = lax.broadcasted_iota(jnp.int32, (t, t), 0)
    jj = lax.broadcasted_iota(jnp.int32, (t, t), 1)
    return _dot_f32((jj <= ii).astype(F32), v)


def _ssd_kernel(pa_ref, sm_ref, cw_ref, cb_ref, dtb_ref, alog_ref, dskip_ref, nw_ref, cst_ref, h0_ref,
                y_ref, cnew_ref, hnew_ref, cbuf, s_scr, *, t):
    per_tile = LANES // t
    width = per_tile * SSM_HEAD_DIM
    c = pl.program_id(1)
    last = pl.num_programs(1) - 1

    @pl.when(c == 0)
    def _():
        cbuf[CONV_PAD - (CONV_WIDTH - 1):CONV_PAD, :] = cst_ref[0]
        s_scr[...] = h0_ref[0]

    pa = pa_ref[...]
    z = pa[:, :SSM_INNER]
    conv, tail = _causal_conv(cbuf, pa[:, SSM_INNER:], cw_ref[...], t)

    @pl.when(c == last)
    def _():
        cnew_ref[0] = tail

    xbc = _silu(conv + cb_ref[...])
    xs = xbc[:, :SSM_INNER]
    bm = xbc[:, SSM_INNER:SSM_INNER + SSM_GROUPS * SSM_STATE]
    cm = xbc[:, SSM_INNER + SSM_GROUPS * SSM_STATE:]

    dt = _softplus(sm_ref[...] + dtb_ref[...])
    da = dt * (-jnp.exp(alog_ref[...]))
    cum = _cumsum_rows(da, t)
    cum_t = _grouped_transpose(cum, per_tile)
    dt_t = _grouped_transpose(dt, per_tile)

    row = lax.broadcasted_iota(jnp.int32, (t, LANES), 0)
    lane = lax.broadcasted_iota(jnp.int32, (t, LANES), 1)
    causal = (lane % t) <= row
    brow = lax.broadcasted_iota(jnp.int32, (LANES, width), 0)
    bcol = lax.broadcasted_iota(jnp.int32, (LANES, width), 1)
    blockdiag = (brow // t) == (bcol // SSM_HEAD_DIM)

    ys = []
    for g in range(SSM_GROUPS):
        bg = bm[:, g * SSM_STATE:(g + 1) * SSM_STATE]
        cg = cm[:, g * SSM_STATE:(g + 1) * SSM_STATE].astype(BF16)
        cb = _dot_nt(cg, jnp.concatenate([bg] * per_tile, axis=0).astype(BF16))
        bg = bg.astype(BF16)
        for q in range(SSM_HEADS_PER_GROUP // per_tile):
            h0 = g * SSM_HEADS_PER_GROUP + q * per_tile
            ccol = _lane_expand(cum, h0, per_tile, t)
            seg = ccol - cum_t[h0:h0 + 1, :]
            decay = jnp.exp(jnp.where(causal, seg, -jnp.inf))
            wts = (cb * decay * dt_t[h0:h0 + 1, :]).astype(BF16)
            xt = xs[:, h0 * SSM_HEAD_DIM:h0 * SSM_HEAD_DIM + width]
            xbd = jnp.where(blockdiag, jnp.concatenate([xt] * per_tile, axis=0), 0.0).astype(BF16)
            y_intra = _dot(wts, xbd)
            st = s_scr[h0:h0 + per_tile].reshape(width, SSM_STATE)
            cum_w = ccol if width == LANES else _lane_expand(cum, h0, per_tile, SSM_HEAD_DIM)
            y_state = _dot_nt(cg, st.astype(BF16)) * jnp.exp(cum_w)
            ys.append(y_intra + y_state)
            wend = jnp.exp(cum_w[t - 1:t, :] - cum_w) * _lane_expand(dt, h0, per_tile, SSM_HEAD_DIM)
            ds = _dot_tn((xt * wend).astype(BF16), bg)
            for i in range(per_tile):
                h = h0 + i
                s_scr[h] = jnp.exp(cum[t - 1:t, h:h + 1]) * s_scr[h] + ds[i * SSM_HEAD_DIM:(i + 1) * SSM_HEAD_DIM]

    y = jnp.concatenate(ys, axis=1) + dskip_ref[...] * xs
    y = y * _silu(z)
    half = SSM_INNER // SSM_GROUPS
    normed = []
    for g in range(SSM_GROUPS):
        yg = y[:, g * half:(g + 1) * half]
        normed.append(yg * lax.rsqrt(jnp.mean(yg * yg, axis=-1, keepdims=True) + RMS_EPS))
    y_ref[...] = (jnp.concatenate(normed, axis=1) * nw_ref[...]).astype(BF16)

    @pl.when(c == last)
    def _():
        hnew_ref[0] = s_scr[...]


def _alias_kernel(kernel, n_in):
    def wrapped(*refs):
        kernel(*refs[:n_in], *refs[n_in + 1:])
    return wrapped


def _scan_call(kernel, name, tok_in, const_in, state_in, y_all, y_width, scratch, *, batch, seq, row0, t):
    nchunk = seq // t
    blk0 = row0 // t
    rows = tok_in[0].shape[0]
    tok = lambda b, c: (blk0 + b * nchunk + c, 0)

    def per_batch(a):
        zeros = (0,) * (a.ndim - 1)
        return pl.BlockSpec((1,) + a.shape[1:], lambda b, c: (b,) + zeros)

    in_specs = ([pl.BlockSpec((t, a.shape[1]), tok) for a in tok_in]
                + [pl.BlockSpec(a.shape, lambda b, c: (0, 0)) for a in const_in]
                + [per_batch(a) for a in state_in])
    args = list(tok_in) + list(const_in) + list(state_in)
    aliases = {}
    if y_all is not None:
        in_specs.append(pl.BlockSpec(memory_space=pl.ANY))
        aliases = {len(args): 0}
        kernel = _alias_kernel(kernel, len(args))
        args.append(y_all)
    return pl.pallas_call(
        kernel,
        grid=(batch, nchunk),
        in_specs=in_specs,
        out_specs=[pl.BlockSpec((t, y_width), tok)] + [per_batch(a) for a in state_in],
        out_shape=[jax.ShapeDtypeStruct((rows, y_width), BF16)]
        + [jax.ShapeDtypeStruct(a.shape, F32) for a in state_in],
        scratch_shapes=scratch,
        input_output_aliases=aliases,
        compiler_params=_params("parallel", "arbitrary"),
        name=name,
    )(*args)


def _ssd(proj_a, small, consts, conv_state, h0, y_all, **where):
    t = where["t"]
    scratch = [pltpu.VMEM((CONV_PAD + t, SSM_CONV_CH), F32),
               pltpu.VMEM((SSM_HEADS, SSM_HEAD_DIM, SSM_STATE), F32)]
    return _scan_call(functools.partial(_ssd_kernel, t=t), f"ssd_t{t}", [proj_a, small], consts,
                      [conv_state, h0], y_all, SSM_INNER, scratch, **where)


def _gdn_kernel(pb_ref, sm_ref, cw_ref, alog_ref, dtb_ref, nw_ref, cst_ref, s0_ref,
                o_ref, cnew_ref, snew_ref, cbuf, s_scr, *, t):
    c = pl.program_id(1)
    last = pl.num_programs(1) - 1

    @pl.when(c == 0)
    def _():
        cbuf[CONV_PAD - (CONV_WIDTH - 1):CONV_PAD, :] = cst_ref[0]
        s_scr[...] = s0_ref[0]

    pb = pb_ref[...]
    zb = pb[:, DN_CONV_CH:]
    conv, tail = _causal_conv(cbuf, pb[:, :DN_CONV_CH], cw_ref[...], t)

    @pl.when(c == last)
    def _():
        cnew_ref[0] = tail

    qkv = _silu(conv)
    sm = sm_ref[...]
    beta = _sigmoid(sm)
    gate = -jnp.exp(alog_ref[...]) * _softplus(sm + dtb_ref[...])
    gcum = _cumsum_rows(gate, t)
    gcum_t = jnp.concatenate([gcum, jnp.zeros((LANES - t, LANES), F32)], axis=0).T

    ii = lax.broadcasted_iota(jnp.int32, (t, t), 0)
    jj = lax.broadcasted_iota(jnp.int32, (t, t), 1)
    strict = jj < ii
    incl = jj <= ii
    eye = (ii == jj).astype(F32)
    norm_w = nw_ref[...]

    outs = []
    for h in range(DN_HEADS):
        lo = h * DN_HEAD
        q = qkv[:, lo:lo + DN_HEAD]
        k = qkv[:, DN_QK + lo:DN_QK + lo + DN_HEAD]
        v = qkv[:, 2 * DN_QK + lo:2 * DN_QK + lo + DN_HEAD]
        q = q * lax.rsqrt(jnp.sum(q * q, axis=-1, keepdims=True) + L2_EPS) * (DN_HEAD ** -0.5)
        k = k * lax.rsqrt(jnp.sum(k * k, axis=-1, keepdims=True) + L2_EPS)
        gcol = gcum[:, SMALL_A + h:SMALL_A + h + 1]
        seg = gcol - gcum_t[SMALL_A + h:SMALL_A + h + 1, :t]
        bcol = beta[:, SMALL_B + h:SMALL_B + h + 1]
        kb = k * bcol
        k16 = k.astype(BF16)
        lmat = _dot_nt(kb.astype(BF16), k16) * jnp.exp(jnp.where(strict, seg, -jnp.inf))
        inv = eye - lmat
        power = lmat
        span = 2
        while span < t:
            power = _dot_f32(power, power)
            inv = inv + _dot_f32(inv, power)
            span *= 2
        egcol = jnp.exp(gcol)
        sol = _dot_f32(inv, jnp.concatenate([v * bcol, kb * egcol], axis=1))
        s = s_scr[h]
        s16 = s.astype(BF16)
        u = sol[:, :DN_HEAD] - _dot(sol[:, DN_HEAD:].astype(BF16), s16)
        u16 = u.astype(BF16)
        qk = _dot_nt(q.astype(BF16), k16) * jnp.exp(jnp.where(incl, seg, -jnp.inf))
        o = _dot((q * egcol).astype(BF16), s16) + _dot(qk.astype(BF16), u16)
        glast = gcum[t - 1:t, SMALL_A + h:SMALL_A + h + 1]
        s_scr[h] = jnp.exp(glast) * s + _dot_tn((k * jnp.exp(glast - gcol)).astype(BF16), u16)
        o = o * lax.rsqrt(jnp.mean(o * o, axis=-1, keepdims=True) + RMS_EPS)
        outs.append(o * norm_w * _silu(zb[:, lo:lo + DN_HEAD]))
    o_ref[...] = jnp.concatenate(outs, axis=1).astype(BF16)

    @pl.when(c == last)
    def _():
        snew_ref[0] = s_scr[...]


def _gdn(proj_b, small, consts, conv_state, s0, o_all, **where):
    t = where["t"]
    scratch = [pltpu.VMEM((CONV_PAD + t, DN_CONV_CH), F32),
               pltpu.VMEM((DN_HEADS, DN_HEAD, DN_HEAD), F32)]
    return _scan_call(functools.partial(_gdn_kernel, t=t), f"gdn_t{t}", [proj_b, small], consts,
                      [conv_state, s0], o_all, DN_V, scratch, **where)


def _merge_kernel(x_ref, ya_ref, ob_ref, gate_ref, eg_ref, eb_ref, wa_ref, wb_ref, wo_ref, g_ref, b_ref,
                  x1_ref, x1b_ref):
    h = _layer_norm(x_ref[...], eg_ref[...], eb_ref[...])
    out_a = _dot(ya_ref[...], wa_ref[...])
    out_b = _dot(ob_ref[...], wb_ref[...])
    gt = gate_ref[...]
    merged = _sigmoid(gt[:, :D_MODEL]) * out_a + _sigmoid(gt[:, D_MODEL:]) * out_b
    mix = _dot(merged.astype(BF16), wo_ref[...])
    x1 = _layer_norm(DEEPNORM_ALPHA * h + mix, g_ref[...], b_ref[...])
    x1_ref[...] = x1
    x1b_ref[...] = x1.astype(BF16)


def _rows_call(kernel, name, row_in, const_in, out_widths, out_dtypes, tm):
    m = row_in[0].shape[0]
    tm = _row_tile(m, tm)
    row = lambda i: (i, 0)
    const = lambda i: (0, 0)
    return pl.pallas_call(
        kernel,
        grid=(m // tm,),
        in_specs=[pl.BlockSpec((tm, a.shape[1]), row) for a in row_in]
        + [pl.BlockSpec(a.shape, const) for a in const_in],
        out_specs=[pl.BlockSpec((tm, w), row) for w in out_widths],
        out_shape=[jax.ShapeDtypeStruct((m, w), d) for w, d in zip(out_widths, out_dtypes)],
        compiler_params=_params("parallel"),
        name=name,
    )(*row_in, *const_in)


def _lane_max(x):
    return jnp.max(x, axis=-1, keepdims=True)


def _first_lane_of(x, value, lane):
    return jnp.min(jnp.where(x == value, lane, LANES), axis=-1, keepdims=True)


def _router_gates(x1, rw, rb):
    tm = x1.shape[0]
    lane = lax.broadcasted_iota(jnp.int32, (tm, LANES), 1)
    valid = lane < N_EXPERTS
    scores = _sigmoid(_dot_f32(x1, rw))
    sel = jnp.where(valid, scores + rb, -jnp.inf)
    group = lane // EXPERTS_PER_GROUP
    gscore = jnp.full((tm, LANES), -jnp.inf, F32)
    for g in range(N_EXPERT_GROUPS):
        vg = jnp.where(group == g, sel, -jnp.inf)
        m1 = _lane_max(vg)
        m2 = _lane_max(jnp.where(lane == _first_lane_of(vg, m1, lane), -jnp.inf, vg))
        gscore = jnp.where(group == g, m1 + m2, gscore)
    gkeep = jnp.zeros((tm, LANES), jnp.bool_)
    rest = gscore
    for _ in range(TOPK_GROUPS):
        m = _lane_max(rest)
        first = _first_lane_of(rest, m, lane)
        hit = group == first // EXPERTS_PER_GROUP
        gkeep = gkeep | hit
        rest = jnp.where(hit, -jnp.inf, rest)
    rest = jnp.where(gkeep, sel, -jnp.inf)
    keep = jnp.zeros((tm, LANES), jnp.bool_)
    for _ in range(TOP_K):
        m = _lane_max(rest)
        hit = lane == _first_lane_of(rest, m, lane)
        keep = keep | hit
        rest = jnp.where(hit, -jnp.inf, rest)
    picked = jnp.where(keep, scores, 0.0)
    return picked / jnp.sum(picked, axis=-1, keepdims=True) * ROUTED_SCALE


def _dense_router_kernel(x1_ref, x1b_ref, p_ref, sw1_ref, sw3_ref, sw2_ref, pw_ref, pg_ref, rw_ref, rb_ref,
                         dense_ref, gates_ref):
    xb = x1b_ref[...]
    hid = _silu(_dot(xb, sw1_ref[...])) * _dot(xb, sw3_ref[...])
    shared = _dot(hid.astype(BF16), sw2_ref[...])
    ple = _dot(p_ref[...].astype(BF16), pw_ref[...]) * _sigmoid(_dot(xb, pg_ref[...]))
    x1 = x1_ref[...]
    dense_ref[...] = DEEPNORM_ALPHA * x1 + shared + ple
    gates_ref[...] = _router_gates(x1, rw_ref[...], rb_ref[...])


def _moe_kernel(xb_ref, dense_ref, gates_ref, w1_ref, w3_ref, w2_ref, g_ref, b_ref, o_ref, acc):
    e = pl.program_id(1)

    @pl.when(e == 0)
    def _():
        acc[...] = jnp.zeros_like(acc)

    xb = xb_ref[...]
    hid = _silu(_dot(xb, w1_ref[0].astype(BF16))) * _dot(xb, w3_ref[0].astype(BF16))
    out = _dot(hid.astype(BF16), w2_ref[0].astype(BF16))
    gate = pltpu.roll(gates_ref[...], (LANES - e) % LANES, axis=1)[:, 0:1]
    acc[...] += out * gate

    @pl.when(e == pl.num_programs(1) - 1)
    def _():
        o_ref[...] = _layer_norm(dense_ref[...] + acc[...], g_ref[...], b_ref[...])


def _moe(x1b, dense, gates, w1, w3, w2, g, b, tm):
    m = x1b.shape[0]
    tm = _row_tile(m, tm)
    row = lambda i, e: (i, 0)
    const = lambda i, e: (0, 0)
    expert = lambda i, e: (e, 0, 0)
    return pl.pallas_call(
        _moe_kernel,
        grid=(m // tm, N_EXPERTS),
        in_specs=[
            pl.BlockSpec((tm, D_MODEL), row),
            pl.BlockSpec((tm, D_MODEL), row),
            pl.BlockSpec((tm, LANES), row),
            pl.BlockSpec((1, D_MODEL, EXPERT_FF), expert),
            pl.BlockSpec((1, D_MODEL, EXPERT_FF), expert),
            pl.BlockSpec((1, EXPERT_FF, D_MODEL), expert),
            pl.BlockSpec((1, D_MODEL), const),
            pl.BlockSpec((1, D_MODEL), const),
        ],
        out_specs=pl.BlockSpec((tm, D_MODEL), row),
        out_shape=jax.ShapeDtypeStruct((m, D_MODEL), F32),
        scratch_shapes=[pltpu.VMEM((tm, D_MODEL), F32)],
        compiler_params=_params("parallel", "arbitrary"),
        name="moe_dense",
    )(x1b, dense, gates, w1, w3, w2, g, b)


def _row(v):
    return v.reshape(1, -1).astype(F32)


def _lanes_at(v, start):
    return jnp.zeros((1, LANES), F32).at[0, start:start + v.shape[0]].set(v.astype(F32))


def kernel(x_prompt, x_sample, p_prompt, p_sample, state_ssm_conv, state_ssm, state_dn_conv, state_dn, emb_ln_g, emb_ln_b, w_in, conv_a_w, conv_a_b, ssm_dt_bias, ssm_a_log, ssm_d, ssm_norm_w, w_a, conv_b_w, dn_dt_bias, dn_a_log, dn_norm_w, w_b, w_o, ln1_g, ln1_b, router_w, router_bias, exp_w1, exp_w3, exp_w2, sh_w1, sh_w3, sh_w2, ple_w, ple_gate_w, ln2_g, ln2_b):
    bp, lp, _ = x_prompt.shape
    bs, ls, _ = x_sample.shape
    n_p = bp * lp
    n_s = bs * ls
    x = jnp.concatenate([x_prompt.reshape(n_p, D_MODEL), x_sample.reshape(n_s, D_MODEL)], axis=0)
    p = jnp.concatenate([p_prompt[0].reshape(n_p, PLE_DIM), p_sample[0].reshape(n_s, PLE_DIM)], axis=0)

    w = w_in[0]
    o_z, o_xbc, o_dt = 0, SSM_INNER, SSM_INNER + SSM_CONV_CH
    o_qkv = o_dt + SSM_HEADS
    o_a = o_qkv + DN_CONV_CH
    o_b = o_a + DN_HEADS
    o_zb = o_b + DN_HEADS
    o_ga = o_zb + DN_V
    w_pa = w[:, o_z:o_dt].astype(BF16)
    w_pb = jnp.concatenate([w[:, o_qkv:o_a], w[:, o_zb:o_ga]], axis=1).astype(BF16)
    w_pg = w[:, o_ga:].astype(BF16)
    w_ps = jnp.zeros((D_MODEL, LANES), F32)
    w_ps = w_ps.at[:, SMALL_DT:SMALL_DT + SSM_HEADS].set(w[:, o_dt:o_qkv])
    w_ps = w_ps.at[:, SMALL_A:SMALL_A + 2 * DN_HEADS].set(w[:, o_a:o_zb]).astype(BF16)

    eg, eb = _row(emb_ln_g), _row(emb_ln_b)
    proj_a = _ln_matmul(x, eg, eb, w_pa, 1024, 1536)
    proj_b = _ln_matmul(x, eg, eb, w_pb, 1024, 1024)
    proj_g = _ln_matmul(x, eg, eb, w_pg, 1024, 1024)
    small = _ln_matmul(x, eg, eb, w_ps, 1024, LANES)

    ssd_consts = [conv_a_w[0], _row(conv_a_b[0]), _lanes_at(ssm_dt_bias[0], SMALL_DT),
                  _lanes_at(ssm_a_log[0], SMALL_DT), _row(jnp.repeat(ssm_d[0], SSM_HEAD_DIM)),
                  _row(ssm_norm_w[0])]
    gdn_consts = [conv_b_w[0], _lanes_at(dn_a_log[0], SMALL_A), _lanes_at(dn_dt_bias[0], SMALL_A),
                  _row(dn_norm_w[0])]
    zeros = lambda *s: jnp.zeros(s, F32)
    prompt = dict(batch=bp, seq=lp, row0=0, t=math.gcd(lp, CHUNK))
    sample = dict(batch=bs, seq=ls, row0=n_p, t=math.gcd(ls, CHUNK))

    ya, pa_conv, pa_ssm = _ssd(proj_a, small, ssd_consts, zeros(bp, CONV_WIDTH - 1, SSM_CONV_CH),
                               zeros(bp, SSM_HEADS, SSM_HEAD_DIM, SSM_STATE), None, **prompt)
    ya, sa_conv, sa_ssm = _ssd(proj_a, small, ssd_consts, state_ssm_conv[0], state_ssm[0], ya, **sample)
    ob, pb_conv, pb_dn = _gdn(proj_b, small, gdn_consts, zeros(bp, CONV_WIDTH - 1, DN_CONV_CH),
                              zeros(bp, DN_HEADS, DN_HEAD, DN_HEAD), None, **prompt)
    ob, sb_conv, sb_dn = _gdn(proj_b, small, gdn_consts, state_dn_conv[0], state_dn[0], ob, **sample)

    x1, x1b = _rows_call(
        _merge_kernel, "merge", [x, ya, ob, proj_g],
        [eg, eb, w_a[0].astype(BF16), w_b[0].astype(BF16), w_o[0].astype(BF16), _row(ln1_g[0]), _row(ln1_b[0])],
        [D_MODEL, D_MODEL], [F32, BF16], 512)

    router_w_pad = jnp.zeros((D_MODEL, LANES), F32).at[:, :N_EXPERTS].set(router_w[0])
    dense, gates = _rows_call(
        _dense_router_kernel, "dense_router", [x1, x1b, p],
        [sh_w1[0].astype(BF16), sh_w3[0].astype(BF16), sh_w2[0].astype(BF16), ple_w[0].astype(BF16),
         ple_gate_w[0].astype(BF16), router_w_pad, _lanes_at(router_bias[0], 0)],
        [D_MODEL, LANES], [F32, F32], 512)

    out = _moe(x1b, dense, gates, exp_w1[0], exp_w3[0], exp_w2[0], _row(ln2_g[0]), _row(ln2_b[0]), 1024)

    return (out[:n_p].reshape(bp, lp, D_MODEL), out[n_p:].reshape(bs, ls, D_MODEL),
            pa_conv[None], pa_ssm[None], pb_conv[None], pb_dn[None],
            sa_conv[None], sa_ssm[None], sb_conv[None], sb_dn[None])
```

```python
import functools
import math

import jax
import jax.numpy as jnp
from jax import lax
from jax.experimental import pallas as pl
from jax.experimental.pallas import tpu as pltpu

F32 = jnp.float32
BF16 = jnp.bfloat16
HIGHEST = lax.Precision.HIGHEST

D_MODEL = 1024
SSM_INNER = 2048
SSM_HEAD_DIM = 64
SSM_HEADS = 32
SSM_GROUPS = 2
SSM_HEADS_PER_GROUP = 16
SSM_STATE = 128
SSM_CONV_CH = 2560
DN_HEADS = 8
DN_HEAD = 128
DN_QK = 1024
DN_V = 1024
DN_CONV_CH = 3072
CONV_WIDTH = 4
CHUNK = 64
N_EXPERTS = 64
TOP_K = 8
N_EXPERT_GROUPS = 8
EXPERTS_PER_GROUP = 8
TOPK_GROUPS = 4
EXPERT_FF = 256
SHARED_FF = 256
ROUTED_SCALE = 2.5
PLE_DIM = 256
LN_EPS = 1e-5
RMS_EPS = 1e-6
L2_EPS = 1e-6
DEEPNORM_ALPHA = 2.0 ** 0.25

LANES = 128
SUBLANES = 8
VMEM_LIMIT = 56 * 1024 * 1024
SMALL_DT = 0
SMALL_A = 32
SMALL_B = 40
CONV_PAD = 8


def _sigmoid(x):
    return 1.0 / (1.0 + jnp.exp(-x))


def _silu(x):
    return x * _sigmoid(x)


def _softplus(x):
    return jnp.maximum(x, 0.0) + jnp.log1p(jnp.exp(-jnp.abs(x)))


def _layer_norm(x, g, b):
    mu = jnp.mean(x, axis=-1, keepdims=True)
    xc = x - mu
    var = jnp.mean(xc * xc, axis=-1, keepdims=True)
    return xc * lax.rsqrt(var + LN_EPS) * g + b


def _dot(a, b):
    return jnp.dot(a, b, preferred_element_type=F32)


def _dot_nt(a, b):
    return lax.dot_general(a, b, (((1,), (1,)), ((), ())), preferred_element_type=F32)


def _dot_tn(a, b):
    return lax.dot_general(a, b, (((0,), (0,)), ((), ())), preferred_element_type=F32)


def _dot_f32(a, b):
    return jnp.dot(a, b, precision=HIGHEST, preferred_element_type=F32)


def _params(*sem):
    return pltpu.CompilerParams(dimension_semantics=sem, vmem_limit_bytes=VMEM_LIMIT)


def _row_tile(m, preferred):
    return max(d for d in range(SUBLANES, min(m, preferred) + 1, SUBLANES) if m % d == 0)


def _ln_matmul_kernel(x_ref, g_ref, b_ref, w_ref, o_ref, h_scr):
    @pl.when(pl.program_id(1) == 0)
    def _():
        h_scr[...] = _layer_norm(x_ref[...], g_ref[...], b_ref[...]).astype(BF16)

    o_ref[...] = _dot(h_scr[...], w_ref[...])


def _ln_matmul(x, g, b, w, tm, tn):
    m, k = x.shape
    n = w.shape[1]
    tm = _row_tile(m, tm)
    return pl.pallas_call(
        _ln_matmul_kernel,
        grid=(m // tm, n // tn),
        in_specs=[
            pl.BlockSpec((tm, k), lambda i, j: (i, 0)),
            pl.BlockSpec((1, k), lambda i, j: (0, 0)),
            pl.BlockSpec((1, k), lambda i, j: (0, 0)),
            pl.BlockSpec((k, tn), lambda i, j: (0, j)),
        ],
        out_specs=pl.BlockSpec((tm, tn), lambda i, j: (i, j)),
        out_shape=jax.ShapeDtypeStruct((m, n), F32),
        scratch_shapes=[pltpu.VMEM((tm, k), BF16)],
        compiler_params=_params("parallel", "arbitrary"),
        name="ln_matmul",
    )(x, g, b, w)


def _causal_conv(cbuf, xpre, cw, t):
    cbuf[CONV_PAD:CONV_PAD + t, :] = xpre
    lo = CONV_PAD - (CONV_WIDTH - 1)
    y = cbuf[lo:lo + t, :] * cw[0:1]
    y = y + cbuf[lo + 1:lo + 1 + t, :] * cw[1:2]
    y = y + cbuf[lo + 2:lo + 2 + t, :] * cw[2:3]
    y = y + xpre * cw[3:4]
    tail = cbuf[lo + t:CONV_PAD + t, :]
    cbuf[lo:CONV_PAD, :] = tail
    return y, tail


def _lane_expand(v, h0, count, width):
    t = v.shape[0]
    n = count * width
    out = jnp.broadcast_to(v[:, h0:h0 + 1], (t, n))
    if count > 1:
        lane = lax.broadcasted_iota(jnp.int32, (t, n), 1)
        for i in range(1, count):
            out = jnp.where(lane >= i * width, jnp.broadcast_to(v[:, h0 + i:h0 + i + 1], (t, n)), out)
    return out


def _grouped_transpose(v, per_tile):
    t = v.shape[0]
    blocks = [v if r == 0 else pltpu.roll(v, LANES - r, axis=1) for r in range(per_tile)]
    if per_tile * t < LANES:
        blocks.append(jnp.zeros((LANES - per_tile * t, LANES), F32))
    return jnp.concatenate(blocks, axis=0).T


def _same_block(i, j, size):
    shift = size.bit_length() - 1
    return lax.shift_right_logical(i, shift) == lax.shift_right_logical(j, shift)


def _unit_lower_inverses(lmats, ii, jj, t):
    mm = lambda a, b: _dot(a.astype(BF16), b.astype(BF16))
    base = min(16, t)
    in_base = _same_block(ii, jj, base)
    eye = (ii == jj).astype(F32)
    power = [jnp.where(in_base, l, 0.0) for l in lmats]
    inv = [eye - p for p in power]
    span = 2
    while span < base:
        power = [mm(p, p) for p in power]
        inv = [a + mm(a, p) for a, p in zip(inv, power)]
        span *= 2
    size = base
    while size < t:
        link = _same_block(ii, jj, 2 * size) & jnp.logical_not(_same_block(ii, jj, size))
        cross = [mm(a, jnp.where(link, l, 0.0)) for a, l in zip(inv, lmats)]
        inv = [a - mm(c, a) for a, c in zip(inv, cross)]
        size *= 2
    return inv


def _cumsum_rows(v, t):
    ii = lax.broadcasted_iota(jnp.int32, (t, t), 0)
    jj = lax.broadcasted_iota(jnp.int32, (t, t), 1)
    return _dot_f32((jj <= ii).astype(F32), v)


def _ssd_kernel(pa_ref, sm_ref, cw_ref, cb_ref, dtb_ref, alog_ref, dskip_ref, nw_ref, cst_ref, h0_ref,
                y_ref, cnew_ref, hnew_ref, cbuf, s_scr, *, t):
    per_tile = LANES // t
    width = per_tile * SSM_HEAD_DIM
    c = pl.program_id(1)
    last = pl.num_programs(1) - 1

    @pl.when(c == 0)
    def _():
        cbuf[CONV_PAD - (CONV_WIDTH - 1):CONV_PAD, :] = cst_ref[0]
        s_scr[...] = h0_ref[0]

    pa = pa_ref[...]
    z = pa[:, :SSM_INNER]
    conv, tail = _causal_conv(cbuf, pa[:, SSM_INNER:], cw_ref[...], t)

    @pl.when(c == last)
    def _():
        cnew_ref[0] = tail

    xbc = _silu(conv + cb_ref[...])
    xs = xbc[:, :SSM_INNER]
    bm = xbc[:, SSM_INNER:SSM_INNER + SSM_GROUPS * SSM_STATE]
    cm = xbc[:, SSM_INNER + SSM_GROUPS * SSM_STATE:]

    dt = _softplus(sm_ref[...] + dtb_ref[...])
    da = dt * (-jnp.exp(alog_ref[...]))
    cum = _cumsum_rows(da, t)
    cum_t = _grouped_transpose(cum, per_tile)
    dt_t = _grouped_transpose(dt, per_tile)

    row = lax.broadcasted_iota(jnp.int32, (t, LANES), 0)
    lane = lax.broadcasted_iota(jnp.int32, (t, LANES), 1)
    causal = (lane % t) <= row
    brow = lax.broadcasted_iota(jnp.int32, (LANES, width), 0)
    bcol = lax.broadcasted_iota(jnp.int32, (LANES, width), 1)
    blockdiag = (brow // t) == (bcol // SSM_HEAD_DIM)

    ys = []
    for g in range(SSM_GROUPS):
        bg = bm[:, g * SSM_STATE:(g + 1) * SSM_STATE]
        cg = cm[:, g * SSM_STATE:(g + 1) * SSM_STATE].astype(BF16)
        cb = _dot_nt(cg, jnp.concatenate([bg] * per_tile, axis=0).astype(BF16))
        bg = bg.astype(BF16)
        for q in range(SSM_HEADS_PER_GROUP // per_tile):
            h0 = g * SSM_HEADS_PER_GROUP + q * per_tile
            ccol = _lane_expand(cum, h0, per_tile, t)
            seg = ccol - cum_t[h0:h0 + 1, :]
            decay = jnp.exp(jnp.where(causal, seg, -jnp.inf))
            wts = (cb * decay * dt_t[h0:h0 + 1, :]).astype(BF16)
            xt = xs[:, h0 * SSM_HEAD_DIM:h0 * SSM_HEAD_DIM + width]
            xbd = jnp.where(blockdiag, jnp.concatenate([xt] * per_tile, axis=0), 0.0).astype(BF16)
            y_intra = _dot(wts, xbd)
            st = s_scr[h0:h0 + per_tile].reshape(width, SSM_STATE)
            cum_w = ccol if width == LANES else _lane_expand(cum, h0, per_tile, SSM_HEAD_DIM)
            y_state = _dot_nt(cg, st.astype(BF16)) * jnp.exp(cum_w)
            ys.append(y_intra + y_state)
            wend = jnp.exp(cum_w[t - 1:t, :] - cum_w) * _lane_expand(dt, h0, per_tile, SSM_HEAD_DIM)
            ds = _dot_tn((xt * wend).astype(BF16), bg)
            for i in range(per_tile):
                h = h0 + i
                s_scr[h] = jnp.exp(cum[t - 1:t, h:h + 1]) * s_scr[h] + ds[i * SSM_HEAD_DIM:(i + 1) * SSM_HEAD_DIM]

    y = jnp.concatenate(ys, axis=1) + dskip_ref[...] * xs
    y = y * _silu(z)
    half = SSM_INNER // SSM_GROUPS
    normed = []
    for g in range(SSM_GROUPS):
        yg = y[:, g * half:(g + 1) * half]
        normed.append(yg * lax.rsqrt(jnp.mean(yg * yg, axis=-1, keepdims=True) + RMS_EPS))
    y_ref[...] = (jnp.concatenate(normed, axis=1) * nw_ref[...]).astype(BF16)

    @pl.when(c == last)
    def _():
        hnew_ref[0] = s_scr[...]


def _alias_kernel(kernel, n_in):
    def wrapped(*refs):
        kernel(*refs[:n_in], *refs[n_in + 1:])
    return wrapped


def _scan_call(kernel, name, tok_in, const_in, state_in, y_all, y_width, scratch, *, batch, seq, row0, t):
    nchunk = seq // t
    blk0 = row0 // t
    rows = tok_in[0].shape[0]
    tok = lambda b, c: (blk0 + b * nchunk + c, 0)

    def per_batch(a):
        zeros = (0,) * (a.ndim - 1)
        return pl.BlockSpec((1,) + a.shape[1:], lambda b, c: (b,) + zeros)

    in_specs = ([pl.BlockSpec((t, a.shape[1]), tok) for a in tok_in]
                + [pl.BlockSpec(a.shape, lambda b, c: (0, 0)) for a in const_in]
                + [per_batch(a) for a in state_in])
    args = list(tok_in) + list(const_in) + list(state_in)
    aliases = {}
    if y_all is not None:
        in_specs.append(pl.BlockSpec(memory_space=pl.ANY))
        aliases = {len(args): 0}
        kernel = _alias_kernel(kernel, len(args))
        args.append(y_all)
    return pl.pallas_call(
        kernel,
        grid=(batch, nchunk),
        in_specs=in_specs,
        out_specs=[pl.BlockSpec((t, y_width), tok)] + [per_batch(a) for a in state_in],
        out_shape=[jax.ShapeDtypeStruct((rows, y_width), BF16)]
        + [jax.ShapeDtypeStruct(a.shape, F32) for a in state_in],
        scratch_shapes=scratch,
        input_output_aliases=aliases,
        compiler_params=_params("parallel", "arbitrary"),
        name=name,
    )(*args)


def _ssd(proj_a, small, consts, conv_state, h0, y_all, **where):
    t = where["t"]
    scratch = [pltpu.VMEM((CONV_PAD + t, SSM_CONV_CH), F32),
               pltpu.VMEM((SSM_HEADS, SSM_HEAD_DIM, SSM_STATE), F32)]
    return _scan_call(functools.partial(_ssd_kernel, t=t), f"ssd_t{t}", [proj_a, small], consts,
                      [conv_state, h0], y_all, SSM_INNER, scratch, **where)


def _gdn_kernel(pb_ref, sm_ref, cw_ref, alog_ref, dtb_ref, nw_ref, cst_ref, s0_ref,
                o_ref, cnew_ref, snew_ref, cbuf, s_scr, *, t):
    c = pl.program_id(1)
    last = pl.num_programs(1) - 1

    @pl.when(c == 0)
    def _():
        cbuf[CONV_PAD - (CONV_WIDTH - 1):CONV_PAD, :] = cst_ref[0]
        s_scr[...] = s0_ref[0]

    pb = pb_ref[...]
    zb = pb[:, DN_CONV_CH:]
    conv, tail = _causal_conv(cbuf, pb[:, :DN_CONV_CH], cw_ref[...], t)

    @pl.when(c == last)
    def _():
        cnew_ref[0] = tail

    qkv = _silu(conv)
    sm = sm_ref[...]
    beta = _sigmoid(sm)
    gate = -jnp.exp(alog_ref[...]) * _softplus(sm + dtb_ref[...])
    gcum = _cumsum_rows(gate, t)

    per_tile = min(LANES // t, DN_HEADS)
    n = per_tile * t
    tiles = range(DN_HEADS // per_tile)
    heads = range(DN_HEADS)
    gcum_t = _grouped_transpose(gcum, per_tile)
    ii = lax.broadcasted_iota(jnp.int32, (n, n), 0)
    jj = lax.broadcasted_iota(jnp.int32, (n, n), 1)
    same_head = _same_block(ii, jj, t)
    causal = same_head & (jj <= ii)
    diag = ii == jj
    stack = lambda xs, p: jnp.concatenate(xs[p * per_tile:(p + 1) * per_tile], axis=0)

    q, k, v, gcol, bcol, kb, egcol, s16 = [], [], [], [], [], [], [], []
    for h in heads:
        lo = h * DN_HEAD
        qh = qkv[:, lo:lo + DN_HEAD]
        kh = qkv[:, DN_QK + lo:DN_QK + lo + DN_HEAD]
        q.append(qh * lax.rsqrt(jnp.sum(qh * qh, axis=-1, keepdims=True) + L2_EPS) * (DN_HEAD ** -0.5))
        k.append(kh * lax.rsqrt(jnp.sum(kh * kh, axis=-1, keepdims=True) + L2_EPS))
        v.append(qkv[:, 2 * DN_QK + lo:2 * DN_QK + lo + DN_HEAD])
        gcol.append(gcum[:, SMALL_A + h:SMALL_A + h + 1])
        bcol.append(beta[:, SMALL_B + h:SMALL_B + h + 1])
        kb.append(k[h] * bcol[h])
        egcol.append(jnp.exp(gcol[h]))
        s16.append(s_scr[h].astype(BF16))

    k16 = [stack(k, p).astype(BF16) for p in tiles]
    decay, lmat, qk = [], [], []
    for p in tiles:
        h0 = p * per_tile
        seg = stack(gcol, p) - gcum_t[SMALL_A + h0:SMALL_A + h0 + 1, :n]
        decay.append(jnp.exp(jnp.where(causal, seg, -jnp.inf)))
    for p in tiles:
        lmat.append(_dot_nt(stack(kb, p).astype(BF16), k16[p]) * jnp.where(diag, 0.0, decay[p]))
    for p in tiles:
        qk.append((_dot_nt(stack(q, p).astype(BF16), k16[p]) * decay[p]).astype(BF16))
    inv = _unit_lower_inverses(lmat, ii, jj, t)

    from_state = [_dot(jnp.concatenate([kb[h] * egcol[h], q[h] * egcol[h]], axis=0).astype(BF16), s16[h])
                  for h in heads]
    rhs = [v[h] * bcol[h] - from_state[h][:t] for h in heads]
    u = [_dot(inv[p].astype(BF16), stack(rhs, p).astype(BF16)) for p in tiles]
    o_intra = [_dot(qk[p], u[p].astype(BF16)) for p in tiles]

    norm_w = nw_ref[...]
    outs = []
    for h in heads:
        p, i = divmod(h, per_tile)
        uh = u[p][i * t:(i + 1) * t].astype(BF16)
        glast = gcum[t - 1:t, SMALL_A + h:SMALL_A + h + 1]
        s_scr[h] = jnp.exp(glast) * s_scr[h] + _dot_tn((k[h] * jnp.exp(glast - gcol[h])).astype(BF16), uh)
        o = from_state[h][t:] + o_intra[p][i * t:(i + 1) * t]
        o = o * lax.rsqrt(jnp.mean(o * o, axis=-1, keepdims=True) + RMS_EPS)
        outs.append(o * norm_w * _silu(zb[:, h * DN_HEAD:(h + 1) * DN_HEAD]))
    o_ref[...] = jnp.concatenate(outs, axis=1).astype(BF16)

    @pl.when(c == last)
    def _():
        snew_ref[0] = s_scr[...]


def _gdn(proj_b, small, consts, conv_state, s0, o_all, **where):
    t = where["t"]
    scratch = [pltpu.VMEM((CONV_PAD + t, DN_CONV_CH), F32),
               pltpu.VMEM((DN_HEADS, DN_HEAD, DN_HEAD), F32)]
    return _scan_call(functools.partial(_gdn_kernel, t=t), f"gdn_t{t}", [proj_b, small], consts,
                      [conv_state, s0], o_all, DN_V, scratch, **where)


def _merge_kernel(x_ref, ya_ref, ob_ref, gate_ref, eg_ref, eb_ref, wa_ref, wb_ref, wo_ref, g_ref, b_ref,
                  x1_ref, x1b_ref):
    h = _layer_norm(x_ref[...], eg_ref[...], eb_ref[...])
    out_a = _dot(ya_ref[...], wa_ref[...])
    out_b = _dot(ob_ref[...], wb_ref[...])
    gt = gate_ref[...]
    merged = _sigmoid(gt[:, :D_MODEL]) * out_a + _sigmoid(gt[:, D_MODEL:]) * out_b
    mix = _dot(merged.astype(BF16), wo_ref[...])
    x1 = _layer_norm(DEEPNORM_ALPHA * h + mix, g_ref[...], b_ref[...])
    x1_ref[...] = x1
    x1b_ref[...] = x1.astype(BF16)


def _rows_call(kernel, name, row_in, const_in, out_widths, out_dtypes, tm):
    m = row_in[0].shape[0]
    tm = _row_tile(m, tm)
    row = lambda i: (i, 0)
    const = lambda i: (0, 0)
    return pl.pallas_call(
        kernel,
        grid=(m // tm,),
        in_specs=[pl.BlockSpec((tm, a.shape[1]), row) for a in row_in]
        + [pl.BlockSpec(a.shape, const) for a in const_in],
        out_specs=[pl.BlockSpec((tm, w), row) for w in out_widths],
        out_shape=[jax.ShapeDtypeStruct((m, w), d) for w, d in zip(out_widths, out_dtypes)],
        compiler_params=_params("parallel"),
        name=name,
    )(*row_in, *const_in)


def _lane_max(x):
    return jnp.max(x, axis=-1, keepdims=True)


def _first_lane_of(x, value, lane):
    return jnp.min(jnp.where(x == value, lane, LANES), axis=-1, keepdims=True)


def _router_gates(x1, rw, rb):
    tm = x1.shape[0]
    lane = lax.broadcasted_iota(jnp.int32, (tm, LANES), 1)
    valid = lane < N_EXPERTS
    scores = _sigmoid(_dot_f32(x1, rw))
    sel = jnp.where(valid, scores + rb, -jnp.inf)
    group = lane // EXPERTS_PER_GROUP
    gscore = jnp.full((tm, LANES), -jnp.inf, F32)
    for g in range(N_EXPERT_GROUPS):
        vg = jnp.where(group == g, sel, -jnp.inf)
        m1 = _lane_max(vg)
        m2 = _lane_max(jnp.where(lane == _first_lane_of(vg, m1, lane), -jnp.inf, vg))
        gscore = jnp.where(group == g, m1 + m2, gscore)
    gkeep = jnp.zeros((tm, LANES), jnp.bool_)
    rest = gscore
    for _ in range(TOPK_GROUPS):
        m = _lane_max(rest)
        first = _first_lane_of(rest, m, lane)
        hit = group == first // EXPERTS_PER_GROUP
        gkeep = gkeep | hit
        rest = jnp.where(hit, -jnp.inf, rest)
    rest = jnp.where(gkeep, sel, -jnp.inf)
    keep = jnp.zeros((tm, LANES), jnp.bool_)
    for _ in range(TOP_K):
        m = _lane_max(rest)
        hit = lane == _first_lane_of(rest, m, lane)
        keep = keep | hit
        rest = jnp.where(hit, -jnp.inf, rest)
    picked = jnp.where(keep, scores, 0.0)
    return picked / jnp.sum(picked, axis=-1, keepdims=True) * ROUTED_SCALE


def _dense_router_kernel(x1_ref, x1b_ref, p_ref, sw1_ref, sw3_ref, sw2_ref, pw_ref, pg_ref, rw_ref, rb_ref,
                         dense_ref, gates_ref):
    xb = x1b_ref[...]
    hid = _silu(_dot(xb, sw1_ref[...])) * _dot(xb, sw3_ref[...])
    shared = _dot(hid.astype(BF16), sw2_ref[...])
    ple = _dot(p_ref[...].astype(BF16), pw_ref[...]) * _sigmoid(_dot(xb, pg_ref[...]))
    x1 = x1_ref[...]
    dense_ref[...] = DEEPNORM_ALPHA * x1 + shared + ple
    gates_ref[...] = _router_gates(x1, rw_ref[...], rb_ref[...])


def _moe_kernel(xb_ref, dense_ref, gates_ref, w1_ref, w3_ref, w2_ref, g_ref, b_ref, o_ref, acc):
    e = pl.program_id(1)

    @pl.when(e == 0)
    def _():
        acc[...] = jnp.zeros_like(acc)

    xb = xb_ref[...]
    hid = _silu(_dot(xb, w1_ref[0].astype(BF16))) * _dot(xb, w3_ref[0].astype(BF16))
    out = _dot(hid.astype(BF16), w2_ref[0].astype(BF16))
    gate = pltpu.roll(gates_ref[...], (LANES - e) % LANES, axis=1)[:, 0:1]
    acc[...] += out * gate

    @pl.when(e == pl.num_programs(1) - 1)
    def _():
        o_ref[...] = _layer_norm(dense_ref[...] + acc[...], g_ref[...], b_ref[...])


def _moe(x1b, dense, gates, w1, w3, w2, g, b, tm):
    m = x1b.shape[0]
    tm = _row_tile(m, tm)
    row = lambda i, e: (i, 0)
    const = lambda i, e: (0, 0)
    expert = lambda i, e: (e, 0, 0)
    return pl.pallas_call(
        _moe_kernel,
        grid=(m // tm, N_EXPERTS),
        in_specs=[
            pl.BlockSpec((tm, D_MODEL), row),
            pl.BlockSpec((tm, D_MODEL), row),
            pl.BlockSpec((tm, LANES), row),
            pl.BlockSpec((1, D_MODEL, EXPERT_FF), expert),
            pl.BlockSpec((1, D_MODEL, EXPERT_FF), expert),
            pl.BlockSpec((1, EXPERT_FF, D_MODEL), expert),
            pl.BlockSpec((1, D_MODEL), const),
            pl.BlockSpec((1, D_MODEL), const),
        ],
        out_specs=pl.BlockSpec((tm, D_MODEL), row),
        out_shape=jax.ShapeDtypeStruct((m, D_MODEL), F32),
        scratch_shapes=[pltpu.VMEM((tm, D_MODEL), F32)],
        compiler_params=_params("parallel", "arbitrary"),
        name="moe_dense",
    )(x1b, dense, gates, w1, w3, w2, g, b)


def _row(v):
    return v.reshape(1, -1).astype(F32)


def _lanes_at(v, start):
    return jnp.zeros((1, LANES), F32).at[0, start:start + v.shape[0]].set(v.astype(F32))


def kernel(x_prompt, x_sample, p_prompt, p_sample, state_ssm_conv, state_ssm, state_dn_conv, state_dn, emb_ln_g, emb_ln_b, w_in, conv_a_w, conv_a_b, ssm_dt_bias, ssm_a_log, ssm_d, ssm_norm_w, w_a, conv_b_w, dn_dt_bias, dn_a_log, dn_norm_w, w_b, w_o, ln1_g, ln1_b, router_w, router_bias, exp_w1, exp_w3, exp_w2, sh_w1, sh_w3, sh_w2, ple_w, ple_gate_w, ln2_g, ln2_b):
    bp, lp, _ = x_prompt.shape
    bs, ls, _ = x_sample.shape
    n_p = bp * lp
    n_s = bs * ls
    x = jnp.concatenate([x_prompt.reshape(n_p, D_MODEL), x_sample.reshape(n_s, D_MODEL)], axis=0)
    p = jnp.concatenate([p_prompt[0].reshape(n_p, PLE_DIM), p_sample[0].reshape(n_s, PLE_DIM)], axis=0)

    w = w_in[0]
    o_z, o_xbc, o_dt = 0, SSM_INNER, SSM_INNER + SSM_CONV_CH
    o_qkv = o_dt + SSM_HEADS
    o_a = o_qkv + DN_CONV_CH
    o_b = o_a + DN_HEADS
    o_zb = o_b + DN_HEADS
    o_ga = o_zb + DN_V
    w_pa = w[:, o_z:o_dt].astype(BF16)
    w_pb = jnp.concatenate([w[:, o_qkv:o_a], w[:, o_zb:o_ga]], axis=1).astype(BF16)
    w_pg = w[:, o_ga:].astype(BF16)
    w_ps = jnp.zeros((D_MODEL, LANES), F32)
    w_ps = w_ps.at[:, SMALL_DT:SMALL_DT + SSM_HEADS].set(w[:, o_dt:o_qkv])
    w_ps = w_ps.at[:, SMALL_A:SMALL_A + 2 * DN_HEADS].set(w[:, o_a:o_zb]).astype(BF16)

    eg, eb = _row(emb_ln_g), _row(emb_ln_b)
    proj_a = _ln_matmul(x, eg, eb, w_pa, 1024, 1536)
    proj_b = _ln_matmul(x, eg, eb, w_pb, 1024, 1024)
    proj_g = _ln_matmul(x, eg, eb, w_pg, 1024, 1024)
    small = _ln_matmul(x, eg, eb, w_ps, 1024, LANES)

    ssd_consts = [conv_a_w[0], _row(conv_a_b[0]), _lanes_at(ssm_dt_bias[0], SMALL_DT),
                  _lanes_at(ssm_a_log[0], SMALL_DT), _row(jnp.repeat(ssm_d[0], SSM_HEAD_DIM)),
                  _row(ssm_norm_w[0])]
    gdn_consts = [conv_b_w[0], _lanes_at(dn_a_log[0], SMALL_A), _lanes_at(dn_dt_bias[0], SMALL_A),
                  _row(dn_norm_w[0])]
    zeros = lambda *s: jnp.zeros(s, F32)
    prompt = dict(batch=bp, seq=lp, row0=0, t=math.gcd(lp, CHUNK))
    sample = dict(batch=bs, seq=ls, row0=n_p, t=math.gcd(ls, CHUNK))

    ya, pa_conv, pa_ssm = _ssd(proj_a, small, ssd_consts, zeros(bp, CONV_WIDTH - 1, SSM_CONV_CH),
                               zeros(bp, SSM_HEADS, SSM_HEAD_DIM, SSM_STATE), None, **prompt)
    ya, sa_conv, sa_ssm = _ssd(proj_a, small, ssd_consts, state_ssm_conv[0], state_ssm[0], ya, **sample)
    ob, pb_conv, pb_dn = _gdn(proj_b, small, gdn_consts, zeros(bp, CONV_WIDTH - 1, DN_CONV_CH),
                              zeros(bp, DN_HEADS, DN_HEAD, DN_HEAD), None, **prompt)
    ob, sb_conv, sb_dn = _gdn(proj_b, small, gdn_consts, state_dn_conv[0], state_dn[0], ob, **sample)

    x1, x1b = _rows_call(
        _merge_kernel, "merge", [x, ya, ob, proj_g],
        [eg, eb, w_a[0].astype(BF16), w_b[0].astype(BF16), w_o[0].astype(BF16), _row(ln1_g[0]), _row(ln1_b[0])],
        [D_MODEL, D_MODEL], [F32, BF16], 512)

    router_w_pad = jnp.zeros((D_MODEL, LANES), F32).at[:, :N_EXPERTS].set(router_w[0])
    dense, gates = _rows_call(
        _dense_router_kernel, "dense_router", [x1, x1b, p],
        [sh_w1[0].astype(BF16), sh_w3[0].astype(BF16), sh_w2[0].astype(BF16), ple_w[0].astype(BF16),
         ple_gate_w[0].astype(BF16), router_w_pad, _lanes_at(router_bias[0], 0)],
        [D_MODEL, LANES], [F32, F32], 512)

    out = _moe(x1b, dense, gates, exp_w1[0], exp_w3[0], exp_w2[0], _row(ln2_g[0]), _row(ln2_b[0]), 1024)

    return (out[:n_p].reshape(bp, lp, D_MODEL), out[n_p:].reshape(bs, ls, D_MODEL),
            pa_conv[None], pa_ssm[None], pb_conv[None], pb_dn[None],
            sa_conv[None], sa_ssm[None], sb_conv[None], sb_dn[None])
```

```python
import functools
import math

import jax
import jax.numpy as jnp
from jax import lax
from jax.experimental import pallas as pl
from jax.experimental.pallas import tpu as pltpu

F32 = jnp.float32
BF16 = jnp.bfloat16
HIGHEST = lax.Precision.HIGHEST

D_MODEL = 1024
SSM_INNER = 2048
SSM_HEAD_DIM = 64
SSM_HEADS = 32
SSM_GROUPS = 2
SSM_HEADS_PER_GROUP = 16
SSM_STATE = 128
SSM_CONV_CH = 2560
DN_HEADS = 8
DN_HEAD = 128
DN_QK = 1024
DN_V = 1024
DN_CONV_CH = 3072
CONV_WIDTH = 4
CHUNK = 64
N_EXPERTS = 64
TOP_K = 8
N_EXPERT_GROUPS = 8
EXPERTS_PER_GROUP = 8
TOPK_GROUPS = 4
EXPERT_FF = 256
SHARED_FF = 256
ROUTED_SCALE = 2.5
PLE_DIM = 256
LN_EPS = 1e-5
RMS_EPS = 1e-6
L2_EPS = 1e-6
DEEPNORM_ALPHA = 2.0 ** 0.25

LANES = 128
SUBLANES = 8
VMEM_LIMIT = 56 * 1024 * 1024
SMALL_DT = 0
SMALL_A = 32
SMALL_B = 40
CONV_PAD = 8
MOE_BLOCK = 256
MOE_SEG = 16
MOE_SUB = 512
MOE_CAP = -(-(MOE_BLOCK * 8 + 64 * (MOE_SEG - 1)) // MOE_SUB) * MOE_SUB
MOE_TM = 512


def _sigmoid(x):
    return 1.0 / (1.0 + jnp.exp(-x))


def _silu(x):
    return x * _sigmoid(x)


def _softplus(x):
    return jnp.maximum(x, 0.0) + jnp.log1p(jnp.exp(-jnp.abs(x)))


def _layer_norm(x, g, b):
    mu = jnp.mean(x, axis=-1, keepdims=True)
    xc = x - mu
    var = jnp.mean(xc * xc, axis=-1, keepdims=True)
    return xc * lax.rsqrt(var + LN_EPS) * g + b


def _dot(a, b):
    return jnp.dot(a, b, preferred_element_type=F32)


def _dot_nt(a, b):
    return lax.dot_general(a, b, (((1,), (1,)), ((), ())), preferred_element_type=F32)


def _dot_tn(a, b):
    return lax.dot_general(a, b, (((0,), (0,)), ((), ())), preferred_element_type=F32)


def _dot_f32(a, b):
    return jnp.dot(a, b, precision=HIGHEST, preferred_element_type=F32)


def _params(*sem):
    return pltpu.CompilerParams(dimension_semantics=sem, vmem_limit_bytes=VMEM_LIMIT)


def _row_tile(m, preferred):
    return max(d for d in range(SUBLANES, min(m, preferred) + 1, SUBLANES) if m % d == 0)


def _ln_matmul_kernel(x_ref, g_ref, b_ref, w_ref, o_ref, h_scr):
    @pl.when(pl.program_id(1) == 0)
    def _():
        h_scr[...] = _layer_norm(x_ref[...], g_ref[...], b_ref[...]).astype(BF16)

    o_ref[...] = _dot(h_scr[...], w_ref[...])


def _ln_matmul(x, g, b, w, tm, tn):
    m, k = x.shape
    n = w.shape[1]
    tm = _row_tile(m, tm)
    return pl.pallas_call(
        _ln_matmul_kernel,
        grid=(m // tm, n // tn),
        in_specs=[
            pl.BlockSpec((tm, k), lambda i, j: (i, 0)),
            pl.BlockSpec((1, k), lambda i, j: (0, 0)),
            pl.BlockSpec((1, k), lambda i, j: (0, 0)),
            pl.BlockSpec((k, tn), lambda i, j: (0, j)),
        ],
        out_specs=pl.BlockSpec((tm, tn), lambda i, j: (i, j)),
        out_shape=jax.ShapeDtypeStruct((m, n), F32),
        scratch_shapes=[pltpu.VMEM((tm, k), BF16)],
        compiler_params=_params("parallel", "arbitrary"),
        name="ln_matmul",
    )(x, g, b, w)


def _causal_conv(cbuf, xpre, cw, t):
    cbuf[CONV_PAD:CONV_PAD + t, :] = xpre
    lo = CONV_PAD - (CONV_WIDTH - 1)
    y = cbuf[lo:lo + t, :] * cw[0:1]
    y = y + cbuf[lo + 1:lo + 1 + t, :] * cw[1:2]
    y = y + cbuf[lo + 2:lo + 2 + t, :] * cw[2:3]
    y = y + xpre * cw[3:4]
    tail = cbuf[lo + t:CONV_PAD + t, :]
    cbuf[lo:CONV_PAD, :] = tail
    return y, tail


def _lane_expand(v, h0, count, width):
    t = v.shape[0]
    n = count * width
    out = jnp.broadcast_to(v[:, h0:h0 + 1], (t, n))
    if count > 1:
        lane = lax.broadcasted_iota(jnp.int32, (t, n), 1)
        for i in range(1, count):
            out = jnp.where(lane >= i * width, jnp.broadcast_to(v[:, h0 + i:h0 + i + 1], (t, n)), out)
    return out


def _grouped_transpose(v, per_tile):
    t = v.shape[0]
    blocks = [v if r == 0 else pltpu.roll(v, LANES - r, axis=1) for r in range(per_tile)]
    if per_tile * t < LANES:
        blocks.append(jnp.zeros((LANES - per_tile * t, LANES), F32))
    return jnp.concatenate(blocks, axis=0).T


def _same_block(i, j, size):
    shift = size.bit_length() - 1
    return lax.shift_right_logical(i, shift) == lax.shift_right_logical(j, shift)


def _unit_lower_inverses(lmats, ii, jj, t):
    mm = lambda a, b: _dot(a.astype(BF16), b.astype(BF16))
    base = min(16, t)
    in_base = _same_block(ii, jj, base)
    eye = (ii == jj).astype(F32)
    power = [jnp.where(in_base, l, 0.0) for l in lmats]
    inv = [eye - p for p in power]
    span = 2
    while span < base:
        power = [mm(p, p) for p in power]
        inv = [a + mm(a, p) for a, p in zip(inv, power)]
        span *= 2
    size = base
    while size < t:
        link = _same_block(ii, jj, 2 * size) & jnp.logical_not(_same_block(ii, jj, size))
        cross = [mm(a, jnp.where(link, l, 0.0)) for a, l in zip(inv, lmats)]
        inv = [a - mm(c, a) for a, c in zip(inv, cross)]
        size *= 2
    return inv


def _cumsum_rows(v, t):
    ii = lax.broadcasted_iota(jnp.int32, (t, t), 0)
    jj = lax.broadcasted_iota(jnp.int32, (t, t), 1)
    return _dot_f32((jj <= ii).astype(F32), v)


def _ssd_kernel(pa_ref, sm_ref, cw_ref, cb_ref, dtb_ref, alog_ref, dskip_ref, nw_ref, cst_ref, h0_ref,
                y_ref, cnew_ref, hnew_ref, cbuf, s_scr, *, t):
    per_tile = LANES // t
    width = per_tile * SSM_HEAD_DIM
    c = pl.program_id(1)
    last = pl.num_programs(1) - 1

    @pl.when(c == 0)
    def _():
        cbuf[CONV_PAD - (CONV_WIDTH - 1):CONV_PAD, :] = cst_ref[0]
        s_scr[...] = h0_ref[0]

    pa = pa_ref[...]
    z = pa[:, :SSM_INNER]
    conv, tail = _causal_conv(cbuf, pa[:, SSM_INNER:], cw_ref[...], t)

    @pl.when(c == last)
    def _():
        cnew_ref[0] = tail

    xbc = _silu(conv + cb_ref[...])
    xs = xbc[:, :SSM_INNER]
    bm = xbc[:, SSM_INNER:SSM_INNER + SSM_GROUPS * SSM_STATE]
    cm = xbc[:, SSM_INNER + SSM_GROUPS * SSM_STATE:]

    dt = _softplus(sm_ref[...] + dtb_ref[...])
    da = dt * (-jnp.exp(alog_ref[...]))
    cum = _cumsum_rows(da, t)
    cum_t = _grouped_transpose(cum, per_tile)
    dt_t = _grouped_transpose(dt, per_tile)

    row = lax.broadcasted_iota(jnp.int32, (t, LANES), 0)
    lane = lax.broadcasted_iota(jnp.int32, (t, LANES), 1)
    causal = (lane % t) <= row
    brow = lax.broadcasted_iota(jnp.int32, (LANES, width), 0)
    bcol = lax.broadcasted_iota(jnp.int32, (LANES, width), 1)
    blockdiag = (brow // t) == (bcol // SSM_HEAD_DIM)

    ys = []
    for g in range(SSM_GROUPS):
        bg = bm[:, g * SSM_STATE:(g + 1) * SSM_STATE]
        cg = cm[:, g * SSM_STATE:(g + 1) * SSM_STATE].astype(BF16)
        cb = _dot_nt(cg, jnp.concatenate([bg] * per_tile, axis=0).astype(BF16))
        bg = bg.astype(BF16)
        for q in range(SSM_HEADS_PER_GROUP // per_tile):
            h0 = g * SSM_HEADS_PER_GROUP + q * per_tile
            ccol = _lane_expand(cum, h0, per_tile, t)
            seg = ccol - cum_t[h0:h0 + 1, :]
            decay = jnp.exp(jnp.where(causal, seg, -jnp.inf))
            wts = (cb * decay * dt_t[h0:h0 + 1, :]).astype(BF16)
            xt = xs[:, h0 * SSM_HEAD_DIM:h0 * SSM_HEAD_DIM + width]
            xbd = jnp.where(blockdiag, jnp.concatenate([xt] * per_tile, axis=0), 0.0).astype(BF16)
            y_intra = _dot(wts, xbd)
            st = s_scr[h0:h0 + per_tile].reshape(width, SSM_STATE)
            cum_w = ccol if width == LANES else _lane_expand(cum, h0, per_tile, SSM_HEAD_DIM)
            y_state = _dot_nt(cg, st.astype(BF16)) * jnp.exp(cum_w)
            ys.append(y_intra + y_state)
            wend = jnp.exp(cum_w[t - 1:t, :] - cum_w) * _lane_expand(dt, h0, per_tile, SSM_HEAD_DIM)
            ds = _dot_tn((xt * wend).astype(BF16), bg)
            for i in range(per_tile):
                h = h0 + i
                s_scr[h] = jnp.exp(cum[t - 1:t, h:h + 1]) * s_scr[h] + ds[i * SSM_HEAD_DIM:(i + 1) * SSM_HEAD_DIM]

    y = jnp.concatenate(ys, axis=1) + dskip_ref[...] * xs
    y = y * _silu(z)
    half = SSM_INNER // SSM_GROUPS
    normed = []
    for g in range(SSM_GROUPS):
        yg = y[:, g * half:(g + 1) * half]
        normed.append(yg * lax.rsqrt(jnp.mean(yg * yg, axis=-1, keepdims=True) + RMS_EPS))
    y_ref[...] = (jnp.concatenate(normed, axis=1) * nw_ref[...]).astype(BF16)

    @pl.when(c == last)
    def _():
        hnew_ref[0] = s_scr[...]


def _scan_call(kernel, name, tok_in, const_in, state_in, y_width, scratch, *, batch, seq, row0, t):
    nchunk = seq // t
    blk0 = row0 // t
    tok = lambda b, c: (blk0 + b * nchunk + c, 0)

    def per_batch(a):
        zeros = (0,) * (a.ndim - 1)
        return pl.BlockSpec((1,) + a.shape[1:], lambda b, c: (b,) + zeros)

    return pl.pallas_call(
        kernel,
        grid=(batch, nchunk),
        in_specs=[pl.BlockSpec((t, a.shape[1]), tok) for a in tok_in]
        + [pl.BlockSpec(a.shape, lambda b, c: (0, 0)) for a in const_in]
        + [per_batch(a) for a in state_in],
        out_specs=[pl.BlockSpec((t, y_width), lambda b, c: (b * nchunk + c, 0))]
        + [per_batch(a) for a in state_in],
        out_shape=[jax.ShapeDtypeStruct((batch * seq, y_width), BF16)]
        + [jax.ShapeDtypeStruct(a.shape, F32) for a in state_in],
        scratch_shapes=scratch,
        compiler_params=_params("parallel", "arbitrary"),
        name=name,
    )(*tok_in, *const_in, *state_in)


def _ssd(proj_a, small, consts, conv_state, h0, **where):
    t = where["t"]
    scratch = [pltpu.VMEM((CONV_PAD + t, SSM_CONV_CH), F32),
               pltpu.VMEM((SSM_HEADS, SSM_HEAD_DIM, SSM_STATE), F32)]
    return _scan_call(functools.partial(_ssd_kernel, t=t), f"ssd_t{t}", [proj_a, small], consts,
                      [conv_state, h0], SSM_INNER, scratch, **where)


def _gdn_kernel(pb_ref, sm_ref, cw_ref, alog_ref, dtb_ref, nw_ref, cst_ref, s0_ref,
                o_ref, cnew_ref, snew_ref, cbuf, s_scr, *, t):
    c = pl.program_id(1)
    last = pl.num_programs(1) - 1

    @pl.when(c == 0)
    def _():
        cbuf[CONV_PAD - (CONV_WIDTH - 1):CONV_PAD, :] = cst_ref[0]
        s_scr[...] = s0_ref[0]

    pb = pb_ref[...]
    zb = pb[:, DN_CONV_CH:]
    conv, tail = _causal_conv(cbuf, pb[:, :DN_CONV_CH], cw_ref[...], t)

    @pl.when(c == last)
    def _():
        cnew_ref[0] = tail

    qkv = _silu(conv)
    sm = sm_ref[...]
    beta = _sigmoid(sm)
    gate = -jnp.exp(alog_ref[...]) * _softplus(sm + dtb_ref[...])
    gcum = _cumsum_rows(gate, t)

    per_tile = min(LANES // t, DN_HEADS)
    n = per_tile * t
    tiles = range(DN_HEADS // per_tile)
    heads = range(DN_HEADS)
    gcum_t = _grouped_transpose(gcum, per_tile)
    ii = lax.broadcasted_iota(jnp.int32, (n, n), 0)
    jj = lax.broadcasted_iota(jnp.int32, (n, n), 1)
    same_head = _same_block(ii, jj, t)
    causal = same_head & (jj <= ii)
    diag = ii == jj
    stack = lambda xs, p: jnp.concatenate(xs[p * per_tile:(p + 1) * per_tile], axis=0)

    q, k, v, gcol, bcol, kb, egcol, s16 = [], [], [], [], [], [], [], []
    for h in heads:
        lo = h * DN_HEAD
        qh = qkv[:, lo:lo + DN_HEAD]
        kh = qkv[:, DN_QK + lo:DN_QK + lo + DN_HEAD]
        q.append(qh * lax.rsqrt(jnp.sum(qh * qh, axis=-1, keepdims=True) + L2_EPS) * (DN_HEAD ** -0.5))
        k.append(kh * lax.rsqrt(jnp.sum(kh * kh, axis=-1, keepdims=True) + L2_EPS))
        v.append(qkv[:, 2 * DN_QK + lo:2 * DN_QK + lo + DN_HEAD])
        gcol.append(gcum[:, SMALL_A + h:SMALL_A + h + 1])
        bcol.append(beta[:, SMALL_B + h:SMALL_B + h + 1])
        kb.append(k[h] * bcol[h])
        egcol.append(jnp.exp(gcol[h]))
        s16.append(s_scr[h].astype(BF16))

    k16 = [stack(k, p).astype(BF16) for p in tiles]
    decay, lmat, qk = [], [], []
    for p in tiles:
        h0 = p * per_tile
        seg = stack(gcol, p) - gcum_t[SMALL_A + h0:SMALL_A + h0 + 1, :n]
        decay.append(jnp.exp(jnp.where(causal, seg, -jnp.inf)))
    for p in tiles:
        lmat.append(_dot_nt(stack(kb, p).astype(BF16), k16[p]) * jnp.where(diag, 0.0, decay[p]))
    for p in tiles:
        qk.append((_dot_nt(stack(q, p).astype(BF16), k16[p]) * decay[p]).astype(BF16))
    inv = _unit_lower_inverses(lmat, ii, jj, t)

    from_state = [_dot(jnp.concatenate([kb[h] * egcol[h], q[h] * egcol[h]], axis=0).astype(BF16), s16[h])
                  for h in heads]
    rhs = [v[h] * bcol[h] - from_state[h][:t] for h in heads]
    u = [_dot(inv[p].astype(BF16), stack(rhs, p).astype(BF16)) for p in tiles]
    o_intra = [_dot(qk[p], u[p].astype(BF16)) for p in tiles]

    norm_w = nw_ref[...]
    outs = []
    for h in heads:
        p, i = divmod(h, per_tile)
        uh = u[p][i * t:(i + 1) * t].astype(BF16)
        glast = gcum[t - 1:t, SMALL_A + h:SMALL_A + h + 1]
        s_scr[h] = jnp.exp(glast) * s_scr[h] + _dot_tn((k[h] * jnp.exp(glast - gcol[h])).astype(BF16), uh)
        o = from_state[h][t:] + o_intra[p][i * t:(i + 1) * t]
        o = o * lax.rsqrt(jnp.mean(o * o, axis=-1, keepdims=True) + RMS_EPS)
        outs.append(o * norm_w * _silu(zb[:, h * DN_HEAD:(h + 1) * DN_HEAD]))
    o_ref[...] = jnp.concatenate(outs, axis=1).astype(BF16)

    @pl.when(c == last)
    def _():
        snew_ref[0] = s_scr[...]


def _gdn(proj_b, small, consts, conv_state, s0, **where):
    t = where["t"]
    scratch = [pltpu.VMEM((CONV_PAD + t, DN_CONV_CH), F32),
               pltpu.VMEM((DN_HEADS, DN_HEAD, DN_HEAD), F32)]
    return _scan_call(functools.partial(_gdn_kernel, t=t), f"gdn_t{t}", [proj_b, small], consts,
                      [conv_state, s0], DN_V, scratch, **where)


def _merge_kernel(x_ref, gate_ref, yap_ref, yas_ref, obp_ref, obs_ref, eg_ref, eb_ref, wa_ref, wb_ref, wo_ref,
                  g_ref, b_ref, x1_ref, x1b_ref, *, prompt_tiles):
    h = _layer_norm(x_ref[...], eg_ref[...], eb_ref[...])
    is_prompt = pl.program_id(0) < prompt_tiles
    out_a = _dot(jnp.where(is_prompt, yap_ref[...], yas_ref[...]), wa_ref[...])
    out_b = _dot(jnp.where(is_prompt, obp_ref[...], obs_ref[...]), wb_ref[...])
    gt = gate_ref[...]
    merged = _sigmoid(gt[:, :D_MODEL]) * out_a + _sigmoid(gt[:, D_MODEL:]) * out_b
    mix = _dot(merged.astype(BF16), wo_ref[...])
    x1 = _layer_norm(DEEPNORM_ALPHA * h + mix, g_ref[...], b_ref[...])
    x1_ref[...] = x1
    x1b_ref[...] = x1.astype(BF16)


def _merge(x, proj_g, ya_p, ya_s, ob_p, ob_s, consts, tm):
    m = x.shape[0]
    n_p = ya_p.shape[0]
    tm = _row_tile(math.gcd(n_p, m - n_p), tm)
    prompt_tiles = n_p // tm
    row = lambda i: (i, 0)
    in_prompt = lambda i: (jnp.minimum(i, prompt_tiles - 1), 0)
    in_sample = lambda i: (jnp.maximum(i - prompt_tiles, 0), 0)
    const = lambda i: (0, 0)
    return pl.pallas_call(
        functools.partial(_merge_kernel, prompt_tiles=prompt_tiles),
        grid=(m // tm,),
        in_specs=[pl.BlockSpec((tm, D_MODEL), row), pl.BlockSpec((tm, 2 * D_MODEL), row),
                  pl.BlockSpec((tm, SSM_INNER), in_prompt), pl.BlockSpec((tm, SSM_INNER), in_sample),
                  pl.BlockSpec((tm, DN_V), in_prompt), pl.BlockSpec((tm, DN_V), in_sample)]
        + [pl.BlockSpec(a.shape, const) for a in consts],
        out_specs=[pl.BlockSpec((tm, D_MODEL), row), pl.BlockSpec((tm, D_MODEL), row)],
        out_shape=[jax.ShapeDtypeStruct((m, D_MODEL), F32), jax.ShapeDtypeStruct((m, D_MODEL), BF16)],
        compiler_params=_params("parallel"),
        name="merge",
    )(x, proj_g, ya_p, ya_s, ob_p, ob_s, *consts)


def _lane_max(x):
    return jnp.max(x, axis=-1, keepdims=True)


def _first_lane_of(x, value, lane):
    return jnp.min(jnp.where(x == value, lane, LANES), axis=-1, keepdims=True)


def _router_gates(x1, rw, rb):
    tm = x1.shape[0]
    lane = lax.broadcasted_iota(jnp.int32, (tm, LANES), 1)
    valid = lane < N_EXPERTS
    scores = _sigmoid(_dot_f32(x1, rw))
    sel = jnp.where(valid, scores + rb, -jnp.inf)
    group = lane // EXPERTS_PER_GROUP
    gscore = jnp.full((tm, LANES), -jnp.inf, F32)
    for g in range(N_EXPERT_GROUPS):
        vg = jnp.where(group == g, sel, -jnp.inf)
        m1 = _lane_max(vg)
        m2 = _lane_max(jnp.where(lane == _first_lane_of(vg, m1, lane), -jnp.inf, vg))
        gscore = jnp.where(group == g, m1 + m2, gscore)
    gkeep = jnp.zeros((tm, LANES), jnp.bool_)
    rest = gscore
    for _ in range(TOPK_GROUPS):
        m = _lane_max(rest)
        first = _first_lane_of(rest, m, lane)
        hit = group == first // EXPERTS_PER_GROUP
        gkeep = gkeep | hit
        rest = jnp.where(hit, -jnp.inf, rest)
    rest = jnp.where(gkeep, sel, -jnp.inf)
    keep = jnp.zeros((tm, LANES), jnp.bool_)
    for _ in range(TOP_K):
        m = _lane_max(rest)
        hit = lane == _first_lane_of(rest, m, lane)
        keep = keep | hit
        rest = jnp.where(hit, -jnp.inf, rest)
    picked = jnp.where(keep, scores, 0.0)
    return picked / jnp.sum(picked, axis=-1, keepdims=True) * ROUTED_SCALE


def _dense_router_kernel(x1_ref, x1b_ref, p_ref, sw1_ref, sw3_ref, sw2_ref, pw_ref, pg_ref, rw_ref, rb_ref,
                         dense_ref, gates_ref, rank_ref, count_ref):
    xb = x1b_ref[...]
    hid = _silu(_dot(xb, sw1_ref[...])) * _dot(xb, sw3_ref[...])
    shared = _dot(hid.astype(BF16), sw2_ref[...])
    ple = _dot(p_ref[...].astype(BF16), pw_ref[...]) * _sigmoid(_dot(xb, pg_ref[...]))
    x1 = x1_ref[...]
    dense_ref[...] = DEEPNORM_ALPHA * x1 + shared + ple
    gates = _router_gates(x1, rw_ref[...], rb_ref[...])
    gates_ref[...] = gates
    tm = gates.shape[0]
    chosen = (gates > 0.0).astype(BF16)
    ii = lax.broadcasted_iota(jnp.int32, (tm, tm), 0)
    jj = lax.broadcasted_iota(jnp.int32, (tm, tm), 1)
    rank_ref[...] = _dot((jj < ii).astype(BF16), chosen)
    count_ref[0] = jnp.broadcast_to(jnp.sum(chosen.astype(F32), axis=0, keepdims=True), (SUBLANES, LANES))


def _dense_router(x1, x1b, p, consts):
    m = x1.shape[0]
    nblk = m // MOE_BLOCK
    row = lambda i: (i, 0)
    const = lambda i: (0, 0)
    return pl.pallas_call(
        _dense_router_kernel,
        grid=(nblk,),
        in_specs=[pl.BlockSpec((MOE_BLOCK, a.shape[1]), row) for a in (x1, x1b, p)]
        + [pl.BlockSpec(a.shape, const) for a in consts],
        out_specs=[pl.BlockSpec((MOE_BLOCK, D_MODEL), row), pl.BlockSpec((MOE_BLOCK, LANES), row),
                   pl.BlockSpec((MOE_BLOCK, LANES), row), pl.BlockSpec((1, SUBLANES, LANES), lambda i: (i, 0, 0))],
        out_shape=[jax.ShapeDtypeStruct((m, D_MODEL), F32), jax.ShapeDtypeStruct((m, LANES), F32),
                   jax.ShapeDtypeStruct((m, LANES), F32), jax.ShapeDtypeStruct((nblk, SUBLANES, LANES), F32)],
        compiler_params=_params("parallel"),
        name="dense_router",
    )(x1, x1b, p, *consts)


def _moe_tables(counts, tm):
    nblk = counts.shape[0]
    nchunk_max = MOE_CAP // MOE_SEG
    padded = (counts + MOE_SEG - 1) // MOE_SEG * MOE_SEG
    start = jnp.cumsum(padded, axis=1) - padded
    used = jnp.sum(padded, axis=1)
    rows_e = jnp.sum(padded, axis=0)
    rows_e_t = (rows_e + tm - 1) // tm * tm
    base_e = jnp.cumsum(rows_e_t) - rows_e_t
    seg_row = base_e[None, :] + jnp.cumsum(padded, axis=0) - padded
    chunk0 = jnp.arange(nchunk_max, dtype=jnp.int32) * MOE_SEG
    owner = jnp.sum((start + padded)[:, None, :] <= chunk0[None, :, None], axis=-1)
    onehot = owner[..., None] == jnp.arange(N_EXPERTS, dtype=jnp.int32)
    dst = chunk0[None, :] + jnp.sum(jnp.where(onehot, (seg_row - start)[:, None, :], 0), axis=-1)
    dst = jnp.where(chunk0[None, :] < used[:, None], dst, 0)
    n_tiles = _moe_max_tiles(nblk, tm)
    tile_row = jnp.arange(n_tiles, dtype=jnp.int32) * tm
    tiles_used = jnp.sum(rows_e_t) // tm
    tile_row = jnp.minimum(tile_row, (tiles_used - 1) * tm)
    tile_expert = jnp.minimum(jnp.sum((base_e + rows_e_t)[None, :] <= tile_row[:, None], axis=-1), N_EXPERTS - 1)
    bounds = jnp.zeros((nblk, SUBLANES, LANES), F32)
    bounds = bounds.at[:, 0, :N_EXPERTS].set(start.astype(F32)).at[:, 1, :N_EXPERTS].set(padded.astype(F32))
    i32 = lambda a: a.astype(jnp.int32)
    return dict(dst=i32(dst.reshape(-1)), nchunk=i32(used // MOE_SEG), pad_row=i32(base_e + rows_e),
                pad_n=i32((rows_e_t - rows_e) // MOE_SEG), tile_expert=i32(tile_expert),
                tiles_used=i32(tiles_used.reshape(1)), bounds=bounds)


def _moe_max_tiles(nblk, tm):
    return (nblk * (MOE_BLOCK * TOP_K + N_EXPERTS * (MOE_SEG - 1)) + N_EXPERTS * (tm - MOE_SEG) + tm - 1) // tm


def _slot_bounds(bounds):
    return bounds[0:1, :], bounds[1:2, :]


def _moe_gather_kernel(dst_ref, nchunk_ref, pad_row_ref, pad_n_ref,
                       x_ref, gates_ref, rank_ref, bounds_ref, rows_ref, buf, zbuf, sem, zsem):
    j = pl.program_id(0)
    nblk = pl.num_programs(0)
    slot = j % 2
    nchunk_max = MOE_CAP // MOE_SEG

    def chunk_copy(blk, q, sl):
        src = buf.at[sl, pl.ds(pl.multiple_of(q * MOE_SEG, MOE_SEG), MOE_SEG), :]
        row = pl.multiple_of(dst_ref[blk * nchunk_max + q], MOE_SEG)
        return pltpu.make_async_copy(src, rows_ref.at[pl.ds(row, MOE_SEG), :], sem.at[sl])

    def pad_copy(e, i):
        row = pl.multiple_of(pad_row_ref[e] + i * MOE_SEG, MOE_SEG)
        return pltpu.make_async_copy(zbuf, rows_ref.at[pl.ds(row, MOE_SEG), :], zsem)

    def for_chunks(blk, sl, action):
        def body(q, carry):
            action(chunk_copy(blk, q, sl))
            return carry
        lax.fori_loop(0, nchunk_ref[blk], body, 0)

    def for_pads(action):
        def per_expert(e, carry):
            def body(i, c):
                action(pad_copy(e, i))
                return c
            return lax.fori_loop(0, pad_n_ref[e], body, carry)
        lax.fori_loop(0, N_EXPERTS, per_expert, 0)

    @pl.when(j == 0)
    def _():
        zbuf[...] = jnp.zeros_like(zbuf)
        for_pads(lambda c: c.start())

    @pl.when(j >= 2)
    def _():
        for_chunks(j - 2, slot, lambda c: c.wait())

    start, length = _slot_bounds(bounds_ref[0])
    chosen_t = (gates_ref[...].T > 0.0).astype(BF16)
    rank_t = rank_ref[...].T.astype(BF16)
    x = x_ref[...]
    used = nchunk_ref[j] * MOE_SEG
    for sub in range(MOE_CAP // MOE_SUB):
        @pl.when(sub * MOE_SUB < used)
        def _():
            s = (sub * MOE_SUB + lax.broadcasted_iota(jnp.int32, (MOE_SUB, LANES), 0)).astype(F32)
            owner = (s >= start) & (s < start + length)
            within = s[:, 0:1] - jnp.sum(jnp.where(owner, start, 0.0), axis=1, keepdims=True)
            owner = owner.astype(BF16)
            pick = (_dot(owner, chosen_t) > 0.5) & (_dot(owner, rank_t) == within)
            buf[slot, sub * MOE_SUB:(sub + 1) * MOE_SUB, :] = _dot(pick.astype(BF16), x).astype(BF16)

    for_chunks(j, slot, lambda c: c.start())

    @pl.when(j == nblk - 1)
    def _():
        for_chunks(j, slot, lambda c: c.wait())

        @pl.when(j >= 1)
        def _():
            for_chunks(j - 1, 1 - slot, lambda c: c.wait())

        for_pads(lambda c: c.wait())


def _moe_gather(x1b, gates, rank, tables, tm):
    m = x1b.shape[0]
    nblk = m // MOE_BLOCK
    rows = _moe_max_tiles(nblk, tm) * tm
    blk = lambda j, *_: (j, 0)
    grid_spec = pltpu.PrefetchScalarGridSpec(
        num_scalar_prefetch=4,
        grid=(nblk,),
        in_specs=[pl.BlockSpec((MOE_BLOCK, D_MODEL), blk), pl.BlockSpec((MOE_BLOCK, LANES), blk),
                  pl.BlockSpec((MOE_BLOCK, LANES), blk),
                  pl.BlockSpec((1, SUBLANES, LANES), lambda j, *_: (j, 0, 0))],
        out_specs=pl.BlockSpec(memory_space=pl.ANY),
        scratch_shapes=[pltpu.VMEM((2, MOE_CAP, D_MODEL), BF16), pltpu.VMEM((MOE_SEG, D_MODEL), BF16),
                        pltpu.SemaphoreType.DMA((2,)), pltpu.SemaphoreType.DMA(())],
    )
    return pl.pallas_call(
        _moe_gather_kernel,
        grid_spec=grid_spec,
        out_shape=jax.ShapeDtypeStruct((rows, D_MODEL), BF16),
        compiler_params=_params("arbitrary"),
        name="moe_gather",
    )(tables["dst"], tables["nchunk"], tables["pad_row"], tables["pad_n"], x1b, gates, rank, tables["bounds"])


def _moe_ffn_kernel(tile_expert_ref, tiles_used_ref, x_ref, w1_ref, w3_ref, w2_ref, y_ref):
    @pl.when(pl.program_id(0) < tiles_used_ref[0])
    def _():
        x = x_ref[...]
        hid = _silu(_dot(x, w1_ref[0].astype(BF16))) * _dot(x, w3_ref[0].astype(BF16))
        y_ref[...] = _dot(hid.astype(BF16), w2_ref[0].astype(BF16)).astype(BF16)


def _moe_ffn(rows, w1, w3, w2, tables, tm):
    n_tiles = rows.shape[0] // tm
    tile = lambda i, tile_expert, tiles_used: (jnp.minimum(i, tiles_used[0] - 1), 0)
    expert = lambda i, tile_expert, tiles_used: (tile_expert[i], 0, 0)
    grid_spec = pltpu.PrefetchScalarGridSpec(
        num_scalar_prefetch=2,
        grid=(n_tiles,),
        in_specs=[pl.BlockSpec((tm, D_MODEL), tile), pl.BlockSpec((1, D_MODEL, EXPERT_FF), expert),
                  pl.BlockSpec((1, D_MODEL, EXPERT_FF), expert), pl.BlockSpec((1, EXPERT_FF, D_MODEL), expert)],
        out_specs=pl.BlockSpec((tm, D_MODEL), tile),
    )
    return pl.pallas_call(
        _moe_ffn_kernel,
        grid_spec=grid_spec,
        out_shape=jax.ShapeDtypeStruct(rows.shape, BF16),
        compiler_params=_params("arbitrary"),
        name="moe_ffn",
    )(tables["tile_expert"], tables["tiles_used"], rows, w1, w3, w2)


def _moe_combine_kernel(dst_ref, nchunk_ref, dense_ref, gates_ref, rank_ref, bounds_ref, g_ref, b_ref, rows_ref,
                        o_ref, buf, sem):
    j = pl.program_id(0)
    nblk = pl.num_programs(0)
    slot = j % 2
    nchunk_max = MOE_CAP // MOE_SEG

    def chunk_copy(blk, q, sl):
        row = pl.multiple_of(dst_ref[blk * nchunk_max + q], MOE_SEG)
        dst = buf.at[sl, pl.ds(pl.multiple_of(q * MOE_SEG, MOE_SEG), MOE_SEG), :]
        return pltpu.make_async_copy(rows_ref.at[pl.ds(row, MOE_SEG), :], dst, sem.at[sl])

    def for_chunks(blk, sl, action):
        def body(q, carry):
            action(chunk_copy(blk, q, sl))
            return carry
        lax.fori_loop(0, nchunk_ref[blk], body, 0)

    @pl.when(j == 0)
    def _():
        buf[...] = jnp.zeros_like(buf)
        for_chunks(0, 0, lambda c: c.start())

    @pl.when(j + 1 < nblk)
    def _():
        for_chunks(j + 1, 1 - slot, lambda c: c.start())

    for_chunks(j, slot, lambda c: c.wait())

    bounds_t = jnp.concatenate([bounds_ref[0], jnp.zeros((LANES - SUBLANES, LANES), F32)], axis=0).T
    start, length = bounds_t[:, 0:1], bounds_t[:, 1:2]
    gates = gates_ref[...].astype(BF16)
    rank = rank_ref[...].astype(BF16)
    used = nchunk_ref[j] * MOE_SEG
    o_ref[...] = dense_ref[...]
    for sub in range(MOE_CAP // MOE_SUB):
        @pl.when(sub * MOE_SUB < used)
        def _():
            s = (sub * MOE_SUB + lax.broadcasted_iota(jnp.int32, (LANES, MOE_SUB), 1)).astype(F32)
            owner = (s >= start) & (s < start + length)
            within = s[0:1, :] - jnp.sum(jnp.where(owner, start, 0.0), axis=0, keepdims=True)
            owner = owner.astype(BF16)
            weight = jnp.where(_dot(rank, owner) == within, _dot(gates, owner), 0.0)
            o_ref[...] += _dot(weight.astype(BF16), buf[slot, sub * MOE_SUB:(sub + 1) * MOE_SUB, :])
    o_ref[...] = _layer_norm(o_ref[...], g_ref[...], b_ref[...])


def _moe_combine(rows, dense, gates, rank, tables, g, b):
    m = dense.shape[0]
    nblk = m // MOE_BLOCK
    blk = lambda j, *_: (j, 0)
    const = lambda j, *_: (0, 0)
    grid_spec = pltpu.PrefetchScalarGridSpec(
        num_scalar_prefetch=2,
        grid=(nblk,),
        in_specs=[pl.BlockSpec((MOE_BLOCK, D_MODEL), blk), pl.BlockSpec((MOE_BLOCK, LANES), blk),
                  pl.BlockSpec((MOE_BLOCK, LANES), blk),
                  pl.BlockSpec((1, SUBLANES, LANES), lambda j, *_: (j, 0, 0)),
                  pl.BlockSpec((1, D_MODEL), const), pl.BlockSpec((1, D_MODEL), const),
                  pl.BlockSpec(memory_space=pl.ANY)],
        out_specs=pl.BlockSpec((MOE_BLOCK, D_MODEL), blk),
        scratch_shapes=[pltpu.VMEM((2, MOE_CAP, D_MODEL), BF16), pltpu.SemaphoreType.DMA((2,))],
    )
    return pl.pallas_call(
        _moe_combine_kernel,
        grid_spec=grid_spec,
        out_shape=jax.ShapeDtypeStruct((m, D_MODEL), F32),
        compiler_params=_params("arbitrary"),
        name="moe_combine",
    )(tables["dst"], tables["nchunk"], dense, gates, rank, tables["bounds"], g, b, rows)


def _row(v):
    return v.reshape(1, -1).astype(F32)


def _lanes_at(v, start):
    return jnp.zeros((1, LANES), F32).at[0, start:start + v.shape[0]].set(v.astype(F32))


def kernel(x_prompt, x_sample, p_prompt, p_sample, state_ssm_conv, state_ssm, state_dn_conv, state_dn, emb_ln_g, emb_ln_b, w_in, conv_a_w, conv_a_b, ssm_dt_bias, ssm_a_log, ssm_d, ssm_norm_w, w_a, conv_b_w, dn_dt_bias, dn_a_log, dn_norm_w, w_b, w_o, ln1_g, ln1_b, router_w, router_bias, exp_w1, exp_w3, exp_w2, sh_w1, sh_w3, sh_w2, ple_w, ple_gate_w, ln2_g, ln2_b):
    bp, lp, _ = x_prompt.shape
    bs, ls, _ = x_sample.shape
    n_p = bp * lp
    n_s = bs * ls
    x = jnp.concatenate([x_prompt.reshape(n_p, D_MODEL), x_sample.reshape(n_s, D_MODEL)], axis=0)
    p = jnp.concatenate([p_prompt[0].reshape(n_p, PLE_DIM), p_sample[0].reshape(n_s, PLE_DIM)], axis=0)

    w = w_in[0]
    o_z, o_xbc, o_dt = 0, SSM_INNER, SSM_INNER + SSM_CONV_CH
    o_qkv = o_dt + SSM_HEADS
    o_a = o_qkv + DN_CONV_CH
    o_b = o_a + DN_HEADS
    o_zb = o_b + DN_HEADS
    o_ga = o_zb + DN_V
    w_pa = w[:, o_z:o_dt].astype(BF16)
    w_pb = jnp.concatenate([w[:, o_qkv:o_a], w[:, o_zb:o_ga]], axis=1).astype(BF16)
    w_pg = w[:, o_ga:].astype(BF16)
    w_ps = jnp.zeros((D_MODEL, LANES), F32)
    w_ps = w_ps.at[:, SMALL_DT:SMALL_DT + SSM_HEADS].set(w[:, o_dt:o_qkv])
    w_ps = w_ps.at[:, SMALL_A:SMALL_A + 2 * DN_HEADS].set(w[:, o_a:o_zb]).astype(BF16)

    eg, eb = _row(emb_ln_g), _row(emb_ln_b)
    proj_a = _ln_matmul(x, eg, eb, w_pa, 1024, 1536)
    proj_b = _ln_matmul(x, eg, eb, w_pb, 1024, 1024)
    proj_g = _ln_matmul(x, eg, eb, w_pg, 1024, 1024)
    small = _ln_matmul(x, eg, eb, w_ps, 1024, LANES)

    ssd_consts = [conv_a_w[0], _row(conv_a_b[0]), _lanes_at(ssm_dt_bias[0], SMALL_DT),
                  _lanes_at(ssm_a_log[0], SMALL_DT), _row(jnp.repeat(ssm_d[0], SSM_HEAD_DIM)),
                  _row(ssm_norm_w[0])]
    gdn_consts = [conv_b_w[0], _lanes_at(dn_a_log[0], SMALL_A), _lanes_at(dn_dt_bias[0], SMALL_A),
                  _row(dn_norm_w[0])]
    zeros = lambda *s: jnp.zeros(s, F32)
    prompt = dict(batch=bp, seq=lp, row0=0, t=math.gcd(lp, CHUNK))
    sample = dict(batch=bs, seq=ls, row0=n_p, t=math.gcd(ls, CHUNK))

    ya_p, pa_conv, pa_ssm = _ssd(proj_a, small, ssd_consts, zeros(bp, CONV_WIDTH - 1, SSM_CONV_CH),
                                 zeros(bp, SSM_HEADS, SSM_HEAD_DIM, SSM_STATE), **prompt)
    ya_s, sa_conv, sa_ssm = _ssd(proj_a, small, ssd_consts, state_ssm_conv[0], state_ssm[0], **sample)
    ob_p, pb_conv, pb_dn = _gdn(proj_b, small, gdn_consts, zeros(bp, CONV_WIDTH - 1, DN_CONV_CH),
                                zeros(bp, DN_HEADS, DN_HEAD, DN_HEAD), **prompt)
    ob_s, sb_conv, sb_dn = _gdn(proj_b, small, gdn_consts, state_dn_conv[0], state_dn[0], **sample)

    x1, x1b = _merge(
        x, proj_g, ya_p, ya_s, ob_p, ob_s,
        [eg, eb, w_a[0].astype(BF16), w_b[0].astype(BF16), w_o[0].astype(BF16), _row(ln1_g[0]), _row(ln1_b[0])],
        512)

    router_w_pad = jnp.zeros((D_MODEL, LANES), F32).at[:, :N_EXPERTS].set(router_w[0])
    dense, gates, rank, counts = _dense_router(
        x1, x1b, p,
        [sh_w1[0].astype(BF16), sh_w3[0].astype(BF16), sh_w2[0].astype(BF16), ple_w[0].astype(BF16),
         ple_gate_w[0].astype(BF16), router_w_pad, _lanes_at(router_bias[0], 0)])

    tables = _moe_tables(counts[:, 0, :N_EXPERTS].astype(jnp.int32), MOE_TM)
    sorted_rows = _moe_gather(x1b, gates, rank, tables, MOE_TM)
    expert_out = _moe_ffn(sorted_rows, exp_w1[0], exp_w3[0], exp_w2[0], tables, MOE_TM)
    out = _moe_combine(expert_out, dense, gates, rank, tables, _row(ln2_g[0]), _row(ln2_b[0]))

    return (out[:n_p].reshape(bp, lp, D_MODEL), out[n_p:].reshape(bs, ls, D_MODEL),
            pa_conv[None], pa_ssm[None], pb_conv[None], pb_dn[None],
            sa_conv[None], sa_ssm[None], sb_conv[None], sb_dn[None])
```

```python
import functools
import math

import jax
import jax.numpy as jnp
from jax import lax
from jax.experimental import pallas as pl
from jax.experimental.pallas import tpu as pltpu

F32 = jnp.float32
BF16 = jnp.bfloat16
HIGHEST = lax.Precision.HIGHEST

D_MODEL = 1024
SSM_INNER = 2048
SSM_HEAD_DIM = 64
SSM_HEADS = 32
SSM_GROUPS = 2
SSM_HEADS_PER_GROUP = 16
SSM_STATE = 128
SSM_CONV_CH = 2560
DN_HEADS = 8
DN_HEAD = 128
DN_QK = 1024
DN_V = 1024
DN_CONV_CH = 3072
CONV_WIDTH = 4
CHUNK = 64
N_EXPERTS = 64
TOP_K = 8
N_EXPERT_GROUPS = 8
EXPERTS_PER_GROUP = 8
TOPK_GROUPS = 4
EXPERT_FF = 256
SHARED_FF = 256
ROUTED_SCALE = 2.5
PLE_DIM = 256
LN_EPS = 1e-5
RMS_EPS = 1e-6
L2_EPS = 1e-6
DEEPNORM_ALPHA = 2.0 ** 0.25

LANES = 128
SUBLANES = 8
VMEM_LIMIT = 56 * 1024 * 1024
SMALL_DT = 0
SMALL_A = 32
SMALL_B = 40
CONV_PAD = 8
MOE_BLOCK = 256
MOE_SEG = 16
MOE_SUB = 512
MOE_CAP = -(-(MOE_BLOCK * 8 + 64 * (MOE_SEG - 1)) // MOE_SUB) * MOE_SUB
MOE_TM = 512


def _sigmoid(x):
    return 1.0 / (1.0 + jnp.exp(-x))


def _silu(x):
    return x * _sigmoid(x)


def _softplus(x):
    return jnp.maximum(x, 0.0) + jnp.log(1.0 + jnp.exp(-jnp.abs(x)))


def _layer_norm(x, g, b):
    mu = jnp.mean(x, axis=-1, keepdims=True)
    xc = x - mu
    var = jnp.mean(xc * xc, axis=-1, keepdims=True)
    return xc * lax.rsqrt(var + LN_EPS) * g + b


def _dot(a, b):
    return jnp.dot(a, b, preferred_element_type=F32)


def _dot_nt(a, b):
    return lax.dot_general(a, b, (((1,), (1,)), ((), ())), preferred_element_type=F32)


def _dot_tn(a, b):
    return lax.dot_general(a, b, (((0,), (0,)), ((), ())), preferred_element_type=F32)


def _dot_f32(a, b):
    return jnp.dot(a, b, precision=HIGHEST, preferred_element_type=F32)


def _params(*sem):
    return pltpu.CompilerParams(dimension_semantics=sem, vmem_limit_bytes=VMEM_LIMIT)


def _row_tile(m, preferred):
    return max(d for d in range(SUBLANES, min(m, preferred) + 1, SUBLANES) if m % d == 0)


def _ln_matmul_kernel(x_ref, g_ref, b_ref, w_ref, o_ref, h_scr):
    @pl.when(pl.program_id(1) == 0)
    def _():
        h_scr[...] = _layer_norm(x_ref[...], g_ref[...], b_ref[...]).astype(BF16)

    o_ref[...] = _dot(h_scr[...], w_ref[...])


def _ln_matmul(x, g, b, w, tm, tn):
    m, k = x.shape
    n = w.shape[1]
    tm = _row_tile(m, tm)
    return pl.pallas_call(
        _ln_matmul_kernel,
        grid=(m // tm, n // tn),
        in_specs=[
            pl.BlockSpec((tm, k), lambda i, j: (i, 0)),
            pl.BlockSpec((1, k), lambda i, j: (0, 0)),
            pl.BlockSpec((1, k), lambda i, j: (0, 0)),
            pl.BlockSpec((k, tn), lambda i, j: (0, j)),
        ],
        out_specs=pl.BlockSpec((tm, tn), lambda i, j: (i, j)),
        out_shape=jax.ShapeDtypeStruct((m, n), F32),
        scratch_shapes=[pltpu.VMEM((tm, k), BF16)],
        compiler_params=_params("parallel", "arbitrary"),
        name="ln_matmul",
    )(x, g, b, w)


def _causal_conv(cbuf, xpre, cw, t):
    cbuf[CONV_PAD:CONV_PAD + t, :] = xpre
    lo = CONV_PAD - (CONV_WIDTH - 1)
    y = cbuf[lo:lo + t, :] * cw[0:1]
    y = y + cbuf[lo + 1:lo + 1 + t, :] * cw[1:2]
    y = y + cbuf[lo + 2:lo + 2 + t, :] * cw[2:3]
    y = y + xpre * cw[3:4]
    tail = cbuf[lo + t:CONV_PAD + t, :]
    cbuf[lo:CONV_PAD, :] = tail
    return y, tail


def _lane_expand(v, h0, count, width):
    t = v.shape[0]
    n = count * width
    out = jnp.broadcast_to(v[:, h0:h0 + 1], (t, n))
    if count > 1:
        lane = lax.broadcasted_iota(jnp.int32, (t, n), 1)
        for i in range(1, count):
            out = jnp.where(lane >= i * width, jnp.broadcast_to(v[:, h0 + i:h0 + i + 1], (t, n)), out)
    return out


def _grouped_transpose(v, per_tile):
    t = v.shape[0]
    blocks = [v if r == 0 else pltpu.roll(v, LANES - r, axis=1) for r in range(per_tile)]
    if per_tile * t < LANES:
        blocks.append(jnp.zeros((LANES - per_tile * t, LANES), F32))
    return jnp.concatenate(blocks, axis=0).T


def _same_block(i, j, size):
    shift = size.bit_length() - 1
    return lax.shift_right_logical(i, shift) == lax.shift_right_logical(j, shift)


def _unit_lower_inverses(lmats, ii, jj, t):
    mm = lambda a, b: _dot(a.astype(BF16), b.astype(BF16))
    base = min(16, t)
    in_base = _same_block(ii, jj, base)
    eye = (ii == jj).astype(F32)
    power = [jnp.where(in_base, l, 0.0) for l in lmats]
    inv = [eye - p for p in power]
    span = 2
    while span < base:
        power = [mm(p, p) for p in power]
        inv = [a + mm(a, p) for a, p in zip(inv, power)]
        span *= 2
    size = base
    while size < t:
        link = _same_block(ii, jj, 2 * size) & jnp.logical_not(_same_block(ii, jj, size))
        cross = [mm(a, jnp.where(link, l, 0.0)) for a, l in zip(inv, lmats)]
        inv = [a - mm(c, a) for a, c in zip(inv, cross)]
        size *= 2
    return inv


def _cumsum_rows(v, t):
    ii = lax.broadcasted_iota(jnp.int32, (t, t), 0)
    jj = lax.broadcasted_iota(jnp.int32, (t, t), 1)
    return _dot_f32((jj <= ii).astype(F32), v)


def _ssd_kernel(pa_ref, sm_ref, cw_ref, cb_ref, dtb_ref, alog_ref, dskip_ref, nw_ref, cst_ref, h0_ref,
                y_ref, cnew_ref, hnew_ref, cbuf, s_scr, *, t):
    per_tile = LANES // t
    width = per_tile * SSM_HEAD_DIM
    c = pl.program_id(1)
    last = pl.num_programs(1) - 1

    @pl.when(c == 0)
    def _():
        cbuf[CONV_PAD - (CONV_WIDTH - 1):CONV_PAD, :] = cst_ref[0]
        s_scr[...] = h0_ref[0]

    pa = pa_ref[...]
    z = pa[:, :SSM_INNER]
    conv, tail = _causal_conv(cbuf, pa[:, SSM_INNER:], cw_ref[...], t)

    @pl.when(c == last)
    def _():
        cnew_ref[0] = tail

    xbc = _silu(conv + cb_ref[...])
    xs = xbc[:, :SSM_INNER]
    bm = xbc[:, SSM_INNER:SSM_INNER + SSM_GROUPS * SSM_STATE]
    cm = xbc[:, SSM_INNER + SSM_GROUPS * SSM_STATE:]

    dt = _softplus(sm_ref[...] + dtb_ref[...])
    da = dt * (-jnp.exp(alog_ref[...]))
    cum = _cumsum_rows(da, t)
    cum_t = _grouped_transpose(cum, per_tile)
    dt_t = _grouped_transpose(dt, per_tile)

    row = lax.broadcasted_iota(jnp.int32, (t, LANES), 0)
    lane = lax.broadcasted_iota(jnp.int32, (t, LANES), 1)
    causal = (lane % t) <= row
    brow = lax.broadcasted_iota(jnp.int32, (LANES, width), 0)
    bcol = lax.broadcasted_iota(jnp.int32, (LANES, width), 1)
    blockdiag = (brow // t) == (bcol // SSM_HEAD_DIM)

    ys = []
    for g in range(SSM_GROUPS):
        bg = bm[:, g * SSM_STATE:(g + 1) * SSM_STATE]
        cg = cm[:, g * SSM_STATE:(g + 1) * SSM_STATE].astype(BF16)
        cb = _dot_nt(cg, jnp.concatenate([bg] * per_tile, axis=0).astype(BF16))
        bg = bg.astype(BF16)
        for q in range(SSM_HEADS_PER_GROUP // per_tile):
            h0 = g * SSM_HEADS_PER_GROUP + q * per_tile
            ccol = _lane_expand(cum, h0, per_tile, t)
            seg = ccol - cum_t[h0:h0 + 1, :]
            decay = jnp.exp(jnp.where(causal, seg, -jnp.inf))
            wts = (cb * decay * dt_t[h0:h0 + 1, :]).astype(BF16)
            xt = xs[:, h0 * SSM_HEAD_DIM:h0 * SSM_HEAD_DIM + width]
            xbd = jnp.where(blockdiag, jnp.concatenate([xt] * per_tile, axis=0), 0.0).astype(BF16)
            y_intra = _dot(wts, xbd)
            st = s_scr[h0:h0 + per_tile].reshape(width, SSM_STATE)
            cum_w = ccol if width == LANES else _lane_expand(cum, h0, per_tile, SSM_HEAD_DIM)
            y_state = _dot_nt(cg, st.astype(BF16)) * jnp.exp(cum_w)
            ys.append(y_intra + y_state)
            wend = jnp.exp(cum_w[t - 1:t, :] - cum_w) * _lane_expand(dt, h0, per_tile, SSM_HEAD_DIM)
            ds = _dot_tn((xt * wend).astype(BF16), bg)
            for i in range(per_tile):
                h = h0 + i
                s_scr[h] = jnp.exp(cum[t - 1:t, h:h + 1]) * s_scr[h] + ds[i * SSM_HEAD_DIM:(i + 1) * SSM_HEAD_DIM]

    y = jnp.concatenate(ys, axis=1) + dskip_ref[...] * xs
    y = y * _silu(z)
    half = SSM_INNER // SSM_GROUPS
    normed = []
    for g in range(SSM_GROUPS):
        yg = y[:, g * half:(g + 1) * half]
        normed.append(yg * lax.rsqrt(jnp.mean(yg * yg, axis=-1, keepdims=True) + RMS_EPS))
    y_ref[...] = (jnp.concatenate(normed, axis=1) * nw_ref[...]).astype(BF16)

    @pl.when(c == last)
    def _():
        hnew_ref[0] = s_scr[...]


def _scan_call(kernel, name, tok_in, const_in, state_in, y_width, scratch, *, batch, seq, row0, t):
    nchunk = seq // t
    blk0 = row0 // t
    tok = lambda b, c: (blk0 + b * nchunk + c, 0)

    def per_batch(a):
        zeros = (0,) * (a.ndim - 1)
        return pl.BlockSpec((1,) + a.shape[1:], lambda b, c: (b,) + zeros)

    return pl.pallas_call(
        kernel,
        grid=(batch, nchunk),
        in_specs=[pl.BlockSpec((t, a.shape[1]), tok) for a in tok_in]
        + [pl.BlockSpec(a.shape, lambda b, c: (0, 0)) for a in const_in]
        + [per_batch(a) for a in state_in],
        out_specs=[pl.BlockSpec((t, y_width), lambda b, c: (b * nchunk + c, 0))]
        + [per_batch(a) for a in state_in],
        out_shape=[jax.ShapeDtypeStruct((batch * seq, y_width), BF16)]
        + [jax.ShapeDtypeStruct(a.shape, F32) for a in state_in],
        scratch_shapes=scratch,
        compiler_params=_params("parallel", "arbitrary"),
        name=name,
    )(*tok_in, *const_in, *state_in)


def _ssd(proj_a, small, consts, conv_state, h0, **where):
    t = where["t"]
    scratch = [pltpu.VMEM((CONV_PAD + t, SSM_CONV_CH), F32),
               pltpu.VMEM((SSM_HEADS, SSM_HEAD_DIM, SSM_STATE), F32)]
    return _scan_call(functools.partial(_ssd_kernel, t=t), f"ssd_t{t}", [proj_a, small], consts,
                      [conv_state, h0], SSM_INNER, scratch, **where)


def _gdn_kernel(pb_ref, sm_ref, cw_ref, alog_ref, dtb_ref, nw_ref, cst_ref, s0_ref,
                o_ref, cnew_ref, snew_ref, cbuf, s_scr, *, t):
    c = pl.program_id(1)
    last = pl.num_programs(1) - 1

    @pl.when(c == 0)
    def _():
        cbuf[CONV_PAD - (CONV_WIDTH - 1):CONV_PAD, :] = cst_ref[0]
        s_scr[...] = s0_ref[0]

    pb = pb_ref[...]
    zb = pb[:, DN_CONV_CH:]
    conv, tail = _causal_conv(cbuf, pb[:, :DN_CONV_CH], cw_ref[...], t)

    @pl.when(c == last)
    def _():
        cnew_ref[0] = tail

    qkv = _silu(conv)
    sm = sm_ref[...]
    beta = _sigmoid(sm)
    gate = -jnp.exp(alog_ref[...]) * _softplus(sm + dtb_ref[...])
    gcum = _cumsum_rows(gate, t)

    per_tile = min(LANES // t, DN_HEADS)
    n = per_tile * t
    tiles = range(DN_HEADS // per_tile)
    heads = range(DN_HEADS)
    gcum_t = _grouped_transpose(gcum, per_tile)
    ii = lax.broadcasted_iota(jnp.int32, (n, n), 0)
    jj = lax.broadcasted_iota(jnp.int32, (n, n), 1)
    same_head = _same_block(ii, jj, t)
    causal = same_head & (jj <= ii)
    diag = ii == jj
    stack = lambda xs, p: jnp.concatenate(xs[p * per_tile:(p + 1) * per_tile], axis=0)

    q, k, v, gcol, bcol, kb, egcol, s16 = [], [], [], [], [], [], [], []
    for h in heads:
        lo = h * DN_HEAD
        qh = qkv[:, lo:lo + DN_HEAD]
        kh = qkv[:, DN_QK + lo:DN_QK + lo + DN_HEAD]
        q.append(qh * lax.rsqrt(jnp.sum(qh * qh, axis=-1, keepdims=True) + L2_EPS) * (DN_HEAD ** -0.5))
        k.append(kh * lax.rsqrt(jnp.sum(kh * kh, axis=-1, keepdims=True) + L2_EPS))
        v.append(qkv[:, 2 * DN_QK + lo:2 * DN_QK + lo + DN_HEAD])
        gcol.append(gcum[:, SMALL_A + h:SMALL_A + h + 1])
        bcol.append(beta[:, SMALL_B + h:SMALL_B + h + 1])
        kb.append(k[h] * bcol[h])
        egcol.append(jnp.exp(gcol[h]))
        s16.append(s_scr[h].astype(BF16))

    k16 = [stack(k, p).astype(BF16) for p in tiles]
    decay, lmat, qk = [], [], []
    for p in tiles:
        h0 = p * per_tile
        seg = stack(gcol, p) - gcum_t[SMALL_A + h0:SMALL_A + h0 + 1, :n]
        decay.append(jnp.exp(jnp.where(causal, seg, -jnp.inf)))
    for p in tiles:
        lmat.append(_dot_nt(stack(kb, p).astype(BF16), k16[p]) * jnp.where(diag, 0.0, decay[p]))
    for p in tiles:
        qk.append((_dot_nt(stack(q, p).astype(BF16), k16[p]) * decay[p]).astype(BF16))
    inv = _unit_lower_inverses(lmat, ii, jj, t)

    from_state = [_dot(jnp.concatenate([kb[h] * egcol[h], q[h] * egcol[h]], axis=0).astype(BF16), s16[h])
                  for h in heads]
    rhs = [v[h] * bcol[h] - from_state[h][:t] for h in heads]
    u = [_dot(inv[p].astype(BF16), stack(rhs, p).astype(BF16)) for p in tiles]
    o_intra = [_dot(qk[p], u[p].astype(BF16)) for p in tiles]

    norm_w = nw_ref[...]
    outs = []
    for h in heads:
        p, i = divmod(h, per_tile)
        uh = u[p][i * t:(i + 1) * t].astype(BF16)
        glast = gcum[t - 1:t, SMALL_A + h:SMALL_A + h + 1]
        s_scr[h] = jnp.exp(glast) * s_scr[h] + _dot_tn((k[h] * jnp.exp(glast - gcol[h])).astype(BF16), uh)
        o = from_state[h][t:] + o_intra[p][i * t:(i + 1) * t]
        o = o * lax.rsqrt(jnp.mean(o * o, axis=-1, keepdims=True) + RMS_EPS)
        outs.append(o * norm_w * _silu(zb[:, h * DN_HEAD:(h + 1) * DN_HEAD]))
    o_ref[...] = jnp.concatenate(outs, axis=1).astype(BF16)

    @pl.when(c == last)
    def _():
        snew_ref[0] = s_scr[...]


def _gdn(proj_b, small, consts, conv_state, s0, **where):
    t = where["t"]
    scratch = [pltpu.VMEM((CONV_PAD + t, DN_CONV_CH), F32),
               pltpu.VMEM((DN_HEADS, DN_HEAD, DN_HEAD), F32)]
    return _scan_call(functools.partial(_gdn_kernel, t=t), f"gdn_t{t}", [proj_b, small], consts,
                      [conv_state, s0], DN_V, scratch, **where)


def _merge_kernel(x_ref, gate_ref, yap_ref, yas_ref, obp_ref, obs_ref, eg_ref, eb_ref, wa_ref, wb_ref, wo_ref,
                  g_ref, b_ref, x1_ref, x1b_ref, *, prompt_tiles):
    h = _layer_norm(x_ref[...], eg_ref[...], eb_ref[...])
    is_prompt = pl.program_id(0) < prompt_tiles
    out_a = _dot(jnp.where(is_prompt, yap_ref[...], yas_ref[...]), wa_ref[...])
    out_b = _dot(jnp.where(is_prompt, obp_ref[...], obs_ref[...]), wb_ref[...])
    gt = gate_ref[...]
    merged = _sigmoid(gt[:, :D_MODEL]) * out_a + _sigmoid(gt[:, D_MODEL:]) * out_b
    mix = _dot(merged.astype(BF16), wo_ref[...])
    x1 = _layer_norm(DEEPNORM_ALPHA * h + mix, g_ref[...], b_ref[...])
    x1_ref[...] = x1
    x1b_ref[...] = x1.astype(BF16)


def _merge(x, proj_g, ya_p, ya_s, ob_p, ob_s, consts, tm):
    m = x.shape[0]
    n_p = ya_p.shape[0]
    tm = _row_tile(math.gcd(n_p, m - n_p), tm)
    prompt_tiles = n_p // tm
    row = lambda i: (i, 0)
    in_prompt = lambda i: (jnp.minimum(i, prompt_tiles - 1), 0)
    in_sample = lambda i: (jnp.maximum(i - prompt_tiles, 0), 0)
    const = lambda i: (0, 0)
    return pl.pallas_call(
        functools.partial(_merge_kernel, prompt_tiles=prompt_tiles),
        grid=(m // tm,),
        in_specs=[pl.BlockSpec((tm, D_MODEL), row), pl.BlockSpec((tm, 2 * D_MODEL), row),
                  pl.BlockSpec((tm, SSM_INNER), in_prompt), pl.BlockSpec((tm, SSM_INNER), in_sample),
                  pl.BlockSpec((tm, DN_V), in_prompt), pl.BlockSpec((tm, DN_V), in_sample)]
        + [pl.BlockSpec(a.shape, const) for a in consts],
        out_specs=[pl.BlockSpec((tm, D_MODEL), row), pl.BlockSpec((tm, D_MODEL), row)],
        out_shape=[jax.ShapeDtypeStruct((m, D_MODEL), F32), jax.ShapeDtypeStruct((m, D_MODEL), BF16)],
        compiler_params=_params("parallel"),
        name="merge",
    )(x, proj_g, ya_p, ya_s, ob_p, ob_s, *consts)


def _take_first_max(rest, index, limit, axis):
    best = jnp.max(rest, axis=axis, keepdims=True)
    first = jnp.min(jnp.where(rest == best, index, limit), axis=axis, keepdims=True)
    hit = index == first
    return hit, jnp.where(hit, -jnp.inf, rest)


def _router_gates_t(x1, xb, rw_hi, rw_lo, bias):
    tm = x1.shape[0]
    x_lo = (x1 - xb.astype(F32)).astype(BF16)
    logits = _dot_nt(rw_hi, xb) + (_dot_nt(rw_hi, x_lo) + _dot_nt(rw_lo, xb))
    scores = _sigmoid(logits)
    row = lax.broadcasted_iota(jnp.int32, (LANES, tm), 0)
    sel = jnp.where(row < N_EXPERTS, scores + bias, -jnp.inf)
    by_group = sel.reshape(LANES // EXPERTS_PER_GROUP, EXPERTS_PER_GROUP, tm)
    sub = lax.broadcasted_iota(jnp.int32, by_group.shape, 1)
    top1 = jnp.max(by_group, axis=1, keepdims=True)
    _, others = _take_first_max(by_group, sub, EXPERTS_PER_GROUP, 1)
    gscore = (top1 + jnp.max(others, axis=1, keepdims=True))[:N_EXPERT_GROUPS]
    gidx = lax.broadcasted_iota(jnp.int32, gscore.shape, 0)
    gkeep = jnp.zeros(gscore.shape, jnp.bool_)
    for _ in range(TOPK_GROUPS):
        hit, gscore = _take_first_max(gscore, gidx, N_EXPERT_GROUPS, 0)
        gkeep = gkeep | hit
    gkeep = jnp.broadcast_to(gkeep, (N_EXPERT_GROUPS, EXPERTS_PER_GROUP, tm)).reshape(N_EXPERTS, tm)
    rest = jnp.where(gkeep, sel[:N_EXPERTS], -jnp.inf)
    eidx = lax.broadcasted_iota(jnp.int32, (N_EXPERTS, tm), 0)
    keep = jnp.zeros((N_EXPERTS, tm), jnp.bool_)
    for _ in range(TOP_K):
        hit, rest = _take_first_max(rest, eidx, N_EXPERTS, 0)
        keep = keep | hit
    picked = jnp.where(keep, scores[:N_EXPERTS], 0.0)
    gates = picked / jnp.sum(picked, axis=0, keepdims=True) * ROUTED_SCALE
    return jnp.concatenate([gates, jnp.zeros((LANES - N_EXPERTS, tm), F32)], axis=0)


def _dense_router_kernel(x1_ref, x1b_ref, p_ref, sw1_ref, sw3_ref, sw2_ref, pw_ref, pg_ref, rwh_ref, rwl_ref,
                         rb_ref, dense_ref, gates_ref, rank_ref, chosen_t_ref, rank_t_ref, count_ref):
    xb = x1b_ref[...]
    hid = _silu(_dot(xb, sw1_ref[...])) * _dot(xb, sw3_ref[...])
    shared = _dot(hid.astype(BF16), sw2_ref[...])
    ple = _dot(p_ref[...].astype(BF16), pw_ref[...]) * _sigmoid(_dot(xb, pg_ref[...]))
    x1 = x1_ref[...]
    dense_ref[...] = DEEPNORM_ALPHA * x1 + shared + ple
    gates_t = _router_gates_t(x1, xb, rwh_ref[...], rwl_ref[...], rb_ref[...])
    tm = x1.shape[0]
    chosen_t = (gates_t > 0.0).astype(BF16)
    ii = lax.broadcasted_iota(jnp.int32, (tm, tm), 0)
    jj = lax.broadcasted_iota(jnp.int32, (tm, tm), 1)
    rank_t = _dot(chosen_t, (ii < jj).astype(BF16))
    chosen_t_ref[...] = chosen_t
    rank_t_ref[...] = rank_t.astype(BF16)
    gates = gates_t.T
    rank = rank_t.T
    gates_ref[...] = gates.astype(BF16)
    rank_ref[...] = rank.astype(BF16)
    count = rank[tm - 1:tm, :] + (gates[tm - 1:tm, :] > 0.0).astype(F32)
    count_ref[0] = jnp.broadcast_to(count, (SUBLANES, LANES))


def _dense_router(x1, x1b, p, consts):
    m = x1.shape[0]
    nblk = m // MOE_BLOCK
    row = lambda i: (i, 0)
    const = lambda i: (0, 0)
    return pl.pallas_call(
        _dense_router_kernel,
        grid=(nblk,),
        in_specs=[pl.BlockSpec((MOE_BLOCK, a.shape[1]), row) for a in (x1, x1b, p)]
        + [pl.BlockSpec(a.shape, const) for a in consts],
        out_specs=[pl.BlockSpec((MOE_BLOCK, D_MODEL), row), pl.BlockSpec((MOE_BLOCK, LANES), row),
                   pl.BlockSpec((MOE_BLOCK, LANES), row), pl.BlockSpec((LANES, MOE_BLOCK), row),
                   pl.BlockSpec((LANES, MOE_BLOCK), row), pl.BlockSpec((1, SUBLANES, LANES), lambda i: (i, 0, 0))],
        out_shape=[jax.ShapeDtypeStruct((m, D_MODEL), F32), jax.ShapeDtypeStruct((m, LANES), BF16),
                   jax.ShapeDtypeStruct((m, LANES), BF16), jax.ShapeDtypeStruct((nblk * LANES, MOE_BLOCK), BF16),
                   jax.ShapeDtypeStruct((nblk * LANES, MOE_BLOCK), BF16),
                   jax.ShapeDtypeStruct((nblk, SUBLANES, LANES), F32)],
        compiler_params=_params("parallel"),
        name="dense_router",
    )(x1, x1b, p, *consts)


def _moe_tables(counts, tm):
    nblk = counts.shape[0]
    nchunk_max = MOE_CAP // MOE_SEG
    padded = (counts + MOE_SEG - 1) // MOE_SEG * MOE_SEG
    start = jnp.cumsum(padded, axis=1) - padded
    used = jnp.sum(padded, axis=1)
    rows_e = jnp.sum(padded, axis=0)
    rows_e_t = (rows_e + tm - 1) // tm * tm
    base_e = jnp.cumsum(rows_e_t) - rows_e_t
    seg_row = base_e[None, :] + jnp.cumsum(padded, axis=0) - padded
    chunk0 = jnp.arange(nchunk_max, dtype=jnp.int32) * MOE_SEG
    owner = jnp.sum((start + padded)[:, None, :] <= chunk0[None, :, None], axis=-1)
    onehot = owner[..., None] == jnp.arange(N_EXPERTS, dtype=jnp.int32)
    dst = chunk0[None, :] + jnp.sum(jnp.where(onehot, (seg_row - start)[:, None, :], 0), axis=-1)
    dst = jnp.where(chunk0[None, :] < used[:, None], dst, 0)
    n_tiles = _moe_max_tiles(nblk, tm)
    tile_row = jnp.arange(n_tiles, dtype=jnp.int32) * tm
    tiles_used = jnp.sum(rows_e_t) // tm
    tile_row = jnp.minimum(tile_row, (tiles_used - 1) * tm)
    tile_expert = jnp.minimum(jnp.sum((base_e + rows_e_t)[None, :] <= tile_row[:, None], axis=-1), N_EXPERTS - 1)
    bounds = jnp.zeros((nblk, SUBLANES, LANES), F32)
    bounds = bounds.at[:, 0, :N_EXPERTS].set(start.astype(F32)).at[:, 1, :N_EXPERTS].set(padded.astype(F32))
    i32 = lambda a: a.astype(jnp.int32)
    return dict(dst=i32(dst.reshape(-1)), nchunk=i32(used // MOE_SEG), pad_row=i32(base_e + rows_e),
                pad_n=i32((rows_e_t - rows_e) // MOE_SEG), tile_expert=i32(tile_expert),
                tiles_used=i32(tiles_used.reshape(1)), bounds=bounds)


def _moe_max_tiles(nblk, tm):
    return (nblk * (MOE_BLOCK * TOP_K + N_EXPERTS * (MOE_SEG - 1)) + N_EXPERTS * (tm - MOE_SEG) + tm - 1) // tm


def _slot_bounds(bounds):
    return bounds[0:1, :], bounds[1:2, :]


def _moe_gather_kernel(dst_ref, nchunk_ref, pad_row_ref, pad_n_ref,
                       x_ref, chosen_t_ref, rank_t_ref, bounds_ref, rows_ref, buf, zbuf, sem, zsem):
    j = pl.program_id(0)
    nblk = pl.num_programs(0)
    slot = j % 2
    nchunk_max = MOE_CAP // MOE_SEG

    def chunk_copy(blk, q, sl):
        src = buf.at[sl, pl.ds(pl.multiple_of(q * MOE_SEG, MOE_SEG), MOE_SEG), :]
        row = pl.multiple_of(dst_ref[blk * nchunk_max + q], MOE_SEG)
        return pltpu.make_async_copy(src, rows_ref.at[pl.ds(row, MOE_SEG), :], sem.at[sl])

    def pad_copy(e, i):
        row = pl.multiple_of(pad_row_ref[e] + i * MOE_SEG, MOE_SEG)
        return pltpu.make_async_copy(zbuf, rows_ref.at[pl.ds(row, MOE_SEG), :], zsem)

    def for_chunks(blk, sl, action):
        def body(q, carry):
            action(chunk_copy(blk, q, sl))
            return carry
        lax.fori_loop(0, nchunk_ref[blk], body, 0)

    def for_pads(action):
        def per_expert(e, carry):
            def body(i, c):
                action(pad_copy(e, i))
                return c
            return lax.fori_loop(0, pad_n_ref[e], body, carry)
        lax.fori_loop(0, N_EXPERTS, per_expert, 0)

    @pl.when(j == 0)
    def _():
        zbuf[...] = jnp.zeros_like(zbuf)
        for_pads(lambda c: c.start())

    @pl.when(j >= 2)
    def _():
        for_chunks(j - 2, slot, lambda c: c.wait())

    start, length = _slot_bounds(bounds_ref[0])
    chosen_t = chosen_t_ref[...]
    rank_t = rank_t_ref[...]
    x = x_ref[...]
    used = nchunk_ref[j] * MOE_SEG
    for sub in range(MOE_CAP // MOE_SUB):
        @pl.when(sub * MOE_SUB < used)
        def _():
            s = (sub * MOE_SUB + lax.broadcasted_iota(jnp.int32, (MOE_SUB, LANES), 0)).astype(F32)
            owner = (s >= start) & (s < start + length)
            within = s[:, 0:1] - jnp.sum(jnp.where(owner, start, 0.0), axis=1, keepdims=True)
            owner = owner.astype(BF16)
            pick = (_dot(owner, chosen_t) > 0.5) & (_dot(owner, rank_t) == within)
            buf[slot, sub * MOE_SUB:(sub + 1) * MOE_SUB, :] = _dot(pick.astype(BF16), x).astype(BF16)

    for_chunks(j, slot, lambda c: c.start())

    @pl.when(j == nblk - 1)
    def _():
        for_chunks(j, slot, lambda c: c.wait())

        @pl.when(j >= 1)
        def _():
            for_chunks(j - 1, 1 - slot, lambda c: c.wait())

        for_pads(lambda c: c.wait())


def _moe_gather(x1b, chosen_t, rank_t, tables, tm):
    m = x1b.shape[0]
    nblk = m // MOE_BLOCK
    rows = _moe_max_tiles(nblk, tm) * tm
    blk = lambda j, *_: (j, 0)
    grid_spec = pltpu.PrefetchScalarGridSpec(
        num_scalar_prefetch=4,
        grid=(nblk,),
        in_specs=[pl.BlockSpec((MOE_BLOCK, D_MODEL), blk), pl.BlockSpec((LANES, MOE_BLOCK), blk),
                  pl.BlockSpec((LANES, MOE_BLOCK), blk),
                  pl.BlockSpec((1, SUBLANES, LANES), lambda j, *_: (j, 0, 0))],
        out_specs=pl.BlockSpec(memory_space=pl.ANY),
        scratch_shapes=[pltpu.VMEM((2, MOE_CAP, D_MODEL), BF16), pltpu.VMEM((MOE_SEG, D_MODEL), BF16),
                        pltpu.SemaphoreType.DMA((2,)), pltpu.SemaphoreType.DMA(())],
    )
    return pl.pallas_call(
        _moe_gather_kernel,
        grid_spec=grid_spec,
        out_shape=jax.ShapeDtypeStruct((rows, D_MODEL), BF16),
        compiler_params=_params("arbitrary"),
        name="moe_gather",
    )(tables["dst"], tables["nchunk"], tables["pad_row"], tables["pad_n"], x1b, chosen_t, rank_t, tables["bounds"])


def _moe_ffn_kernel(tile_expert_ref, tiles_used_ref, x_ref, w1_ref, w3_ref, w2_ref, y_ref):
    @pl.when(pl.program_id(0) < tiles_used_ref[0])
    def _():
        x = x_ref[...]
        hid = _silu(_dot(x, w1_ref[0].astype(BF16))) * _dot(x, w3_ref[0].astype(BF16))
        y_ref[...] = _dot(hid.astype(BF16), w2_ref[0].astype(BF16)).astype(BF16)


def _moe_ffn(rows, w1, w3, w2, tables, tm):
    n_tiles = rows.shape[0] // tm
    tile = lambda i, tile_expert, tiles_used: (jnp.minimum(i, tiles_used[0] - 1), 0)
    expert = lambda i, tile_expert, tiles_used: (tile_expert[i], 0, 0)
    grid_spec = pltpu.PrefetchScalarGridSpec(
        num_scalar_prefetch=2,
        grid=(n_tiles,),
        in_specs=[pl.BlockSpec((tm, D_MODEL), tile), pl.BlockSpec((1, D_MODEL, EXPERT_FF), expert),
                  pl.BlockSpec((1, D_MODEL, EXPERT_FF), expert), pl.BlockSpec((1, EXPERT_FF, D_MODEL), expert)],
        out_specs=pl.BlockSpec((tm, D_MODEL), tile),
    )
    return pl.pallas_call(
        _moe_ffn_kernel,
        grid_spec=grid_spec,
        out_shape=jax.ShapeDtypeStruct(rows.shape, BF16),
        compiler_params=_params("arbitrary"),
        name="moe_ffn",
    )(tables["tile_expert"], tables["tiles_used"], rows, w1, w3, w2)


def _moe_combine_kernel(dst_ref, nchunk_ref, dense_ref, gates_ref, rank_ref, bounds_ref, g_ref, b_ref, rows_ref,
                        o_ref, buf, sem):
    j = pl.program_id(0)
    nblk = pl.num_programs(0)
    slot = j % 2
    nchunk_max = MOE_CAP // MOE_SEG

    def chunk_copy(blk, q, sl):
        row = pl.multiple_of(dst_ref[blk * nchunk_max + q], MOE_SEG)
        dst = buf.at[sl, pl.ds(pl.multiple_of(q * MOE_SEG, MOE_SEG), MOE_SEG), :]
        return pltpu.make_async_copy(rows_ref.at[pl.ds(row, MOE_SEG), :], dst, sem.at[sl])

    def for_chunks(blk, sl, action):
        def body(q, carry):
            action(chunk_copy(blk, q, sl))
            return carry
        lax.fori_loop(0, nchunk_ref[blk], body, 0)

    @pl.when(j == 0)
    def _():
        buf[...] = jnp.zeros_like(buf)
        for_chunks(0, 0, lambda c: c.start())

    @pl.when(j + 1 < nblk)
    def _():
        for_chunks(j + 1, 1 - slot, lambda c: c.start())

    for_chunks(j, slot, lambda c: c.wait())

    bounds_t = jnp.concatenate([bounds_ref[0], jnp.zeros((LANES - SUBLANES, LANES), F32)], axis=0).T
    start, length = bounds_t[:, 0:1], bounds_t[:, 1:2]
    gates = gates_ref[...]
    rank = rank_ref[...]
    used = nchunk_ref[j] * MOE_SEG
    o_ref[...] = dense_ref[...]
    for sub in range(MOE_CAP // MOE_SUB):
        @pl.when(sub * MOE_SUB < used)
        def _():
            s = (sub * MOE_SUB + lax.broadcasted_iota(jnp.int32, (LANES, MOE_SUB), 1)).astype(F32)
            owner = (s >= start) & (s < start + length)
            within = s[0:1, :] - jnp.sum(jnp.where(owner, start, 0.0), axis=0, keepdims=True)
            owner = owner.astype(BF16)
            weight = jnp.where(_dot(rank, owner) == within, _dot(gates, owner), 0.0)
            o_ref[...] += _dot(weight.astype(BF16), buf[slot, sub * MOE_SUB:(sub + 1) * MOE_SUB, :])
    o_ref[...] = _layer_norm(o_ref[...], g_ref[...], b_ref[...])


def _moe_combine(rows, dense, gates, rank, tables, g, b):
    m = dense.shape[0]
    nblk = m // MOE_BLOCK
    blk = lambda j, *_: (j, 0)
    const = lambda j, *_: (0, 0)
    grid_spec = pltpu.PrefetchScalarGridSpec(
        num_scalar_prefetch=2,
        grid=(nblk,),
        in_specs=[pl.BlockSpec((MOE_BLOCK, D_MODEL), blk), pl.BlockSpec((MOE_BLOCK, LANES), blk),
                  pl.BlockSpec((MOE_BLOCK, LANES), blk),
                  pl.BlockSpec((1, SUBLANES, LANES), lambda j, *_: (j, 0, 0)),
                  pl.BlockSpec((1, D_MODEL), const), pl.BlockSpec((1, D_MODEL), const),
                  pl.BlockSpec(memory_space=pl.ANY)],
        out_specs=pl.BlockSpec((MOE_BLOCK, D_MODEL), blk),
        scratch_shapes=[pltpu.VMEM((2, MOE_CAP, D_MODEL), BF16), pltpu.SemaphoreType.DMA((2,))],
    )
    return pl.pallas_call(
        _moe_combine_kernel,
        grid_spec=grid_spec,
        out_shape=jax.ShapeDtypeStruct((m, D_MODEL), F32),
        compiler_params=_params("arbitrary"),
        name="moe_combine",
    )(tables["dst"], tables["nchunk"], dense, gates, rank, tables["bounds"], g, b, rows)


def _row(v):
    return v.reshape(1, -1).astype(F32)


def _lanes_at(v, start):
    return jnp.zeros((1, LANES), F32).at[0, start:start + v.shape[0]].set(v.astype(F32))


def kernel(x_prompt, x_sample, p_prompt, p_sample, state_ssm_conv, state_ssm, state_dn_conv, state_dn, emb_ln_g, emb_ln_b, w_in, conv_a_w, conv_a_b, ssm_dt_bias, ssm_a_log, ssm_d, ssm_norm_w, w_a, conv_b_w, dn_dt_bias, dn_a_log, dn_norm_w, w_b, w_o, ln1_g, ln1_b, router_w, router_bias, exp_w1, exp_w3, exp_w2, sh_w1, sh_w3, sh_w2, ple_w, ple_gate_w, ln2_g, ln2_b):
    bp, lp, _ = x_prompt.shape
    bs, ls, _ = x_sample.shape
    n_p = bp * lp
    n_s = bs * ls
    x = jnp.concatenate([x_prompt.reshape(n_p, D_MODEL), x_sample.reshape(n_s, D_MODEL)], axis=0)
    p = jnp.concatenate([p_prompt[0].reshape(n_p, PLE_DIM), p_sample[0].reshape(n_s, PLE_DIM)], axis=0)

    w = w_in[0]
    o_z, o_xbc, o_dt = 0, SSM_INNER, SSM_INNER + SSM_CONV_CH
    o_qkv = o_dt + SSM_HEADS
    o_a = o_qkv + DN_CONV_CH
    o_b = o_a + DN_HEADS
    o_zb = o_b + DN_HEADS
    o_ga = o_zb + DN_V
    w_pa = w[:, o_z:o_dt].astype(BF16)
    w_pb = jnp.concatenate([w[:, o_qkv:o_a], w[:, o_zb:o_ga]], axis=1).astype(BF16)
    w_pg = w[:, o_ga:].astype(BF16)
    w_ps = jnp.zeros((D_MODEL, LANES), F32)
    w_ps = w_ps.at[:, SMALL_DT:SMALL_DT + SSM_HEADS].set(w[:, o_dt:o_qkv])
    w_ps = w_ps.at[:, SMALL_A:SMALL_A + 2 * DN_HEADS].set(w[:, o_a:o_zb]).astype(BF16)

    eg, eb = _row(emb_ln_g), _row(emb_ln_b)
    proj_a = _ln_matmul(x, eg, eb, w_pa, 1024, 1536)
    proj_b = _ln_matmul(x, eg, eb, w_pb, 1024, 1024)
    proj_g = _ln_matmul(x, eg, eb, w_pg, 1024, 1024)
    small = _ln_matmul(x, eg, eb, w_ps, 1024, LANES)

    ssd_consts = [conv_a_w[0], _row(conv_a_b[0]), _lanes_at(ssm_dt_bias[0], SMALL_DT),
                  _lanes_at(ssm_a_log[0], SMALL_DT), _row(jnp.repeat(ssm_d[0], SSM_HEAD_DIM)),
                  _row(ssm_norm_w[0])]
    gdn_consts = [conv_b_w[0], _lanes_at(dn_a_log[0], SMALL_A), _lanes_at(dn_dt_bias[0], SMALL_A),
                  _row(dn_norm_w[0])]
    zeros = lambda *s: jnp.zeros(s, F32)
    prompt = dict(batch=bp, seq=lp, row0=0, t=math.gcd(lp, CHUNK))
    sample = dict(batch=bs, seq=ls, row0=n_p, t=math.gcd(ls, CHUNK))

    ya_p, pa_conv, pa_ssm = _ssd(proj_a, small, ssd_consts, zeros(bp, CONV_WIDTH - 1, SSM_CONV_CH),
                                 zeros(bp, SSM_HEADS, SSM_HEAD_DIM, SSM_STATE), **prompt)
    ya_s, sa_conv, sa_ssm = _ssd(proj_a, small, ssd_consts, state_ssm_conv[0], state_ssm[0], **sample)
    ob_p, pb_conv, pb_dn = _gdn(proj_b, small, gdn_consts, zeros(bp, CONV_WIDTH - 1, DN_CONV_CH),
                                zeros(bp, DN_HEADS, DN_HEAD, DN_HEAD), **prompt)
    ob_s, sb_conv, sb_dn = _gdn(proj_b, small, gdn_consts, state_dn_conv[0], state_dn[0], **sample)

    x1, x1b = _merge(
        x, proj_g, ya_p, ya_s, ob_p, ob_s,
        [eg, eb, w_a[0].astype(BF16), w_b[0].astype(BF16), w_o[0].astype(BF16), _row(ln1_g[0]), _row(ln1_b[0])],
        512)

    router_w_t = jnp.zeros((LANES, D_MODEL), F32).at[:N_EXPERTS].set(router_w[0].T)
    router_w_hi = router_w_t.astype(BF16)
    router_w_lo = (router_w_t - router_w_hi.astype(F32)).astype(BF16)
    router_b = jnp.broadcast_to(_lanes_at(router_bias[0], 0).reshape(LANES, 1), (LANES, MOE_BLOCK))
    dense, gates, rank, chosen_t, rank_t, counts = _dense_router(
        x1, x1b, p,
        [sh_w1[0].astype(BF16), sh_w3[0].astype(BF16), sh_w2[0].astype(BF16), ple_w[0].astype(BF16),
         ple_gate_w[0].astype(BF16), router_w_hi, router_w_lo, router_b])

    tables = _moe_tables(counts[:, 0, :N_EXPERTS].astype(jnp.int32), MOE_TM)
    sorted_rows = _moe_gather(x1b, chosen_t, rank_t, tables, MOE_TM)
    expert_out = _moe_ffn(sorted_rows, exp_w1[0], exp_w3[0], exp_w2[0], tables, MOE_TM)
    out = _moe_combine(expert_out, dense, gates, rank, tables, _row(ln2_g[0]), _row(ln2_b[0]))

    return (out[:n_p].reshape(bp, lp, D_MODEL), out[n_p:].reshape(bs, ls, D_MODEL),
            pa_conv[None], pa_ssm[None], pb_conv[None], pb_dn[None],
            sa_conv[None], sa_ssm[None], sb_conv[None], sb_dn[None])
```

```python
import functools
import math

import jax
import jax.numpy as jnp
from jax import lax
from jax.experimental import pallas as pl
from jax.experimental.pallas import tpu as pltpu

F32 = jnp.float32
BF16 = jnp.bfloat16
HIGHEST = lax.Precision.HIGHEST

D_MODEL = 1024
SSM_INNER = 2048
SSM_HEAD_DIM = 64
SSM_HEADS = 32
SSM_GROUPS = 2
SSM_HEADS_PER_GROUP = 16
SSM_STATE = 128
SSM_CONV_CH = 2560
DN_HEADS = 8
DN_HEAD = 128
DN_QK = 1024
DN_V = 1024
DN_CONV_CH = 3072
CONV_WIDTH = 4
CHUNK = 64
N_EXPERTS = 64
TOP_K = 8
N_EXPERT_GROUPS = 8
EXPERTS_PER_GROUP = 8
TOPK_GROUPS = 4
EXPERT_FF = 256
SHARED_FF = 256
ROUTED_SCALE = 2.5
PLE_DIM = 256
LN_EPS = 1e-5
RMS_EPS = 1e-6
L2_EPS = 1e-6
DEEPNORM_ALPHA = 2.0 ** 0.25

LANES = 128
SUBLANES = 8
VMEM_LIMIT = 56 * 1024 * 1024
SMALL_DT = 0
SMALL_A = 32
SMALL_B = 40
CONV_PAD = 8
MOE_BLOCK = 256
MOE_SEG = 16
MOE_SUB = 512
MOE_CAP = -(-(MOE_BLOCK * 8 + 64 * (MOE_SEG - 1)) // MOE_SUB) * MOE_SUB
MOE_TM = 512


def _sigmoid(x):
    return 1.0 / (1.0 + jnp.exp(-x))


def _silu(x):
    return x * _sigmoid(x)


def _softplus(x):
    return jnp.maximum(x, 0.0) + jnp.log(1.0 + jnp.exp(-jnp.abs(x)))


def _layer_norm(x, g, b):
    mu = jnp.mean(x, axis=-1, keepdims=True)
    xc = x - mu
    var = jnp.mean(xc * xc, axis=-1, keepdims=True)
    return xc * lax.rsqrt(var + LN_EPS) * g + b


def _dot(a, b):
    return jnp.dot(a, b, preferred_element_type=F32)


def _dot_nt(a, b):
    return lax.dot_general(a, b, (((1,), (1,)), ((), ())), preferred_element_type=F32)


def _dot_tn(a, b):
    return lax.dot_general(a, b, (((0,), (0,)), ((), ())), preferred_element_type=F32)


def _dot_f32(a, b):
    return jnp.dot(a, b, precision=HIGHEST, preferred_element_type=F32)


def _params(*sem):
    return pltpu.CompilerParams(dimension_semantics=sem, vmem_limit_bytes=VMEM_LIMIT)


def _row_tile(m, preferred):
    return max(d for d in range(SUBLANES, min(m, preferred) + 1, SUBLANES) if m % d == 0)


def _group_maps(prompt_tiles):
    in_prompt = lambda i, *_: (jnp.minimum(i, prompt_tiles - 1), 0)
    in_sample = lambda i, *_: (jnp.maximum(i - prompt_tiles, 0), 0)
    return in_prompt, in_sample


def _ln_matmul_kernel(xp_ref, xs_ref, g_ref, b_ref, w_ref, o_ref, h_scr, *, prompt_tiles):
    @pl.when(pl.program_id(1) == 0)
    def _():
        x = jnp.where(pl.program_id(0) < prompt_tiles, xp_ref[...], xs_ref[...])
        h_scr[...] = _layer_norm(x, g_ref[...], b_ref[...]).astype(BF16)

    o_ref[...] = _dot(h_scr[...], w_ref[...])


def _ln_matmul(xp, xs, g, b, w, tm, tn):
    k = xp.shape[1]
    m = xp.shape[0] + xs.shape[0]
    n = w.shape[1]
    tm = _row_tile(math.gcd(xp.shape[0], xs.shape[0]), tm)
    prompt_tiles = xp.shape[0] // tm
    in_prompt, in_sample = _group_maps(prompt_tiles)
    return pl.pallas_call(
        functools.partial(_ln_matmul_kernel, prompt_tiles=prompt_tiles),
        grid=(m // tm, n // tn),
        in_specs=[
            pl.BlockSpec((tm, k), in_prompt),
            pl.BlockSpec((tm, k), in_sample),
            pl.BlockSpec((1, k), lambda i, j: (0, 0)),
            pl.BlockSpec((1, k), lambda i, j: (0, 0)),
            pl.BlockSpec((k, tn), lambda i, j: (0, j)),
        ],
        out_specs=pl.BlockSpec((tm, tn), lambda i, j: (i, j)),
        out_shape=jax.ShapeDtypeStruct((m, n), F32),
        scratch_shapes=[pltpu.VMEM((tm, k), BF16)],
        compiler_params=_params("parallel", "arbitrary"),
        name="ln_matmul",
    )(xp, xs, g, b, w)


def _causal_conv(cbuf, xpre, cw, t):
    cbuf[CONV_PAD:CONV_PAD + t, :] = xpre
    lo = CONV_PAD - (CONV_WIDTH - 1)
    y = cbuf[lo:lo + t, :] * cw[0:1]
    y = y + cbuf[lo + 1:lo + 1 + t, :] * cw[1:2]
    y = y + cbuf[lo + 2:lo + 2 + t, :] * cw[2:3]
    y = y + xpre * cw[3:4]
    tail = cbuf[lo + t:CONV_PAD + t, :]
    cbuf[lo:CONV_PAD, :] = tail
    return y, tail


def _lane_expand(v, h0, count, width):
    t = v.shape[0]
    n = count * width
    out = jnp.broadcast_to(v[:, h0:h0 + 1], (t, n))
    if count > 1:
        lane = lax.broadcasted_iota(jnp.int32, (t, n), 1)
        for i in range(1, count):
            out = jnp.where(lane >= i * width, jnp.broadcast_to(v[:, h0 + i:h0 + i + 1], (t, n)), out)
    return out


def _grouped_transpose(v, per_tile):
    t = v.shape[0]
    blocks = [v if r == 0 else pltpu.roll(v, LANES - r, axis=1) for r in range(per_tile)]
    if per_tile * t < LANES:
        blocks.append(jnp.zeros((LANES - per_tile * t, LANES), F32))
    return jnp.concatenate(blocks, axis=0).T


def _same_block(i, j, size):
    shift = size.bit_length() - 1
    return lax.shift_right_logical(i, shift) == lax.shift_right_logical(j, shift)


def _unit_lower_inverses(lmats, ii, jj, t):
    mm = lambda a, b: _dot(a.astype(BF16), b.astype(BF16))
    base = min(16, t)
    in_base = _same_block(ii, jj, base)
    eye = (ii == jj).astype(F32)
    power = [jnp.where(in_base, l, 0.0) for l in lmats]
    inv = [eye - p for p in power]
    span = 2
    while span < base:
        power = [mm(p, p) for p in power]
        inv = [a + mm(a, p) for a, p in zip(inv, power)]
        span *= 2
    size = base
    while size < t:
        link = _same_block(ii, jj, 2 * size) & jnp.logical_not(_same_block(ii, jj, size))
        cross = [mm(a, jnp.where(link, l, 0.0)) for a, l in zip(inv, lmats)]
        inv = [a - mm(c, a) for a, c in zip(inv, cross)]
        size *= 2
    return inv


def _cumsum_rows(v, t):
    ii = lax.broadcasted_iota(jnp.int32, (t, t), 0)
    jj = lax.broadcasted_iota(jnp.int32, (t, t), 1)
    return _dot_f32((jj <= ii).astype(F32), v)


def _ssd_kernel(pa_ref, sm_ref, cw_ref, cb_ref, dtb_ref, alog_ref, dskip_ref, nw_ref, cst_ref, h0_ref,
                y_ref, cnew_ref, hnew_ref, cbuf, s_scr, *, t):
    per_tile = LANES // t
    width = per_tile * SSM_HEAD_DIM
    c = pl.program_id(1)
    last = pl.num_programs(1) - 1

    @pl.when(c == 0)
    def _():
        cbuf[CONV_PAD - (CONV_WIDTH - 1):CONV_PAD, :] = cst_ref[0]
        s_scr[...] = h0_ref[0]

    pa = pa_ref[...]
    z = pa[:, :SSM_INNER]
    conv, tail = _causal_conv(cbuf, pa[:, SSM_INNER:], cw_ref[...], t)

    @pl.when(c == last)
    def _():
        cnew_ref[0] = tail

    xbc = _silu(conv + cb_ref[...])
    xs = xbc[:, :SSM_INNER]
    bm = xbc[:, SSM_INNER:SSM_INNER + SSM_GROUPS * SSM_STATE]
    cm = xbc[:, SSM_INNER + SSM_GROUPS * SSM_STATE:]

    dt = _softplus(sm_ref[...] + dtb_ref[...])
    da = dt * (-jnp.exp(alog_ref[...]))
    cum = _cumsum_rows(da, t)
    cum_t = _grouped_transpose(cum, per_tile)
    dt_t = _grouped_transpose(dt, per_tile)

    row = lax.broadcasted_iota(jnp.int32, (t, LANES), 0)
    lane = lax.broadcasted_iota(jnp.int32, (t, LANES), 1)
    causal = (lane % t) <= row
    brow = lax.broadcasted_iota(jnp.int32, (LANES, width), 0)
    bcol = lax.broadcasted_iota(jnp.int32, (LANES, width), 1)
    blockdiag = (brow // t) == (bcol // SSM_HEAD_DIM)

    ys = []
    for g in range(SSM_GROUPS):
        bg = bm[:, g * SSM_STATE:(g + 1) * SSM_STATE]
        cg = cm[:, g * SSM_STATE:(g + 1) * SSM_STATE].astype(BF16)
        cb = _dot_nt(cg, jnp.concatenate([bg] * per_tile, axis=0).astype(BF16))
        bg = bg.astype(BF16)
        for q in range(SSM_HEADS_PER_GROUP // per_tile):
            h0 = g * SSM_HEADS_PER_GROUP + q * per_tile
            ccol = _lane_expand(cum, h0, per_tile, t)
            seg = ccol - cum_t[h0:h0 + 1, :]
            decay = jnp.exp(jnp.where(causal, seg, -jnp.inf))
            wts = (cb * decay * dt_t[h0:h0 + 1, :]).astype(BF16)
            xt = xs[:, h0 * SSM_HEAD_DIM:h0 * SSM_HEAD_DIM + width]
            xbd = jnp.where(blockdiag, jnp.concatenate([xt] * per_tile, axis=0), 0.0).astype(BF16)
            y_intra = _dot(wts, xbd)
            st = s_scr[h0:h0 + per_tile].reshape(width, SSM_STATE)
            cum_w = ccol if width == LANES else _lane_expand(cum, h0, per_tile, SSM_HEAD_DIM)
            y_state = _dot_nt(cg, st.astype(BF16)) * jnp.exp(cum_w)
            ys.append(y_intra + y_state)
            wend = jnp.exp(cum_w[t - 1:t, :] - cum_w) * _lane_expand(dt, h0, per_tile, SSM_HEAD_DIM)
            ds = _dot_tn((xt * wend).astype(BF16), bg)
            for i in range(per_tile):
                h = h0 + i
                s_scr[h] = jnp.exp(cum[t - 1:t, h:h + 1]) * s_scr[h] + ds[i * SSM_HEAD_DIM:(i + 1) * SSM_HEAD_DIM]

    y = jnp.concatenate(ys, axis=1) + dskip_ref[...] * xs
    y = y * _silu(z)
    half = SSM_INNER // SSM_GROUPS
    normed = []
    for g in range(SSM_GROUPS):
        yg = y[:, g * half:(g + 1) * half]
        normed.append(yg * lax.rsqrt(jnp.mean(yg * yg, axis=-1, keepdims=True) + RMS_EPS))
    y_ref[...] = (jnp.concatenate(normed, axis=1) * nw_ref[...]).astype(BF16)

    @pl.when(c == last)
    def _():
        hnew_ref[0] = s_scr[...]


def _scan_call(kernel, name, tok_in, const_in, state_in, y_width, scratch, *, batch, seq, row0, t):
    nchunk = seq // t
    blk0 = row0 // t
    tok = lambda b, c: (blk0 + b * nchunk + c, 0)

    def per_batch(a):
        zeros = (0,) * (a.ndim - 1)
        return pl.BlockSpec((1,) + a.shape[1:], lambda b, c: (b,) + zeros)

    return pl.pallas_call(
        kernel,
        grid=(batch, nchunk),
        in_specs=[pl.BlockSpec((t, a.shape[1]), tok) for a in tok_in]
        + [pl.BlockSpec(a.shape, lambda b, c: (0, 0)) for a in const_in]
        + [per_batch(a) for a in state_in],
        out_specs=[pl.BlockSpec((t, y_width), lambda b, c: (b * nchunk + c, 0))]
        + [per_batch(a) for a in state_in],
        out_shape=[jax.ShapeDtypeStruct((batch * seq, y_width), BF16)]
        + [jax.ShapeDtypeStruct(a.shape, F32) for a in state_in],
        scratch_shapes=scratch,
        compiler_params=_params("parallel", "arbitrary"),
        name=name,
    )(*tok_in, *const_in, *state_in)


def _ssd(proj_a, small, consts, conv_state, h0, **where):
    t = where["t"]
    scratch = [pltpu.VMEM((CONV_PAD + t, SSM_CONV_CH), F32),
               pltpu.VMEM((SSM_HEADS, SSM_HEAD_DIM, SSM_STATE), F32)]
    return _scan_call(functools.partial(_ssd_kernel, t=t), f"ssd_t{t}", [proj_a, small], consts,
                      [conv_state, h0], SSM_INNER, scratch, **where)


def _gdn_kernel(pb_ref, sm_ref, cw_ref, alog_ref, dtb_ref, nw_ref, cst_ref, s0_ref,
                o_ref, cnew_ref, snew_ref, cbuf, s_scr, *, t):
    c = pl.program_id(1)
    last = pl.num_programs(1) - 1

    @pl.when(c == 0)
    def _():
        cbuf[CONV_PAD - (CONV_WIDTH - 1):CONV_PAD, :] = cst_ref[0]
        s_scr[...] = s0_ref[0]

    pb = pb_ref[...]
    zb = pb[:, DN_CONV_CH:]
    conv, tail = _causal_conv(cbuf, pb[:, :DN_CONV_CH], cw_ref[...], t)

    @pl.when(c == last)
    def _():
        cnew_ref[0] = tail

    qkv = _silu(conv)
    sm = sm_ref[...]
    beta = _sigmoid(sm)
    gate = -jnp.exp(alog_ref[...]) * _softplus(sm + dtb_ref[...])
    gcum = _cumsum_rows(gate, t)

    per_tile = min(LANES // t, DN_HEADS)
    n = per_tile * t
    tiles = range(DN_HEADS // per_tile)
    heads = range(DN_HEADS)
    gcum_t = _grouped_transpose(gcum, per_tile)
    ii = lax.broadcasted_iota(jnp.int32, (n, n), 0)
    jj = lax.broadcasted_iota(jnp.int32, (n, n), 1)
    same_head = _same_block(ii, jj, t)
    causal = same_head & (jj <= ii)
    diag = ii == jj
    stack = lambda xs, p: jnp.concatenate(xs[p * per_tile:(p + 1) * per_tile], axis=0)

    q, k, v, gcol, bcol, kb, egcol, s16 = [], [], [], [], [], [], [], []
    for h in heads:
        lo = h * DN_HEAD
        qh = qkv[:, lo:lo + DN_HEAD]
        kh = qkv[:, DN_QK + lo:DN_QK + lo + DN_HEAD]
        q.append(qh * lax.rsqrt(jnp.sum(qh * qh, axis=-1, keepdims=True) + L2_EPS) * (DN_HEAD ** -0.5))
        k.append(kh * lax.rsqrt(jnp.sum(kh * kh, axis=-1, keepdims=True) + L2_EPS))
        v.append(qkv[:, 2 * DN_QK + lo:2 * DN_QK + lo + DN_HEAD])
        gcol.append(gcum[:, SMALL_A + h:SMALL_A + h + 1])
        bcol.append(beta[:, SMALL_B + h:SMALL_B + h + 1])
        kb.append(k[h] * bcol[h])
        egcol.append(jnp.exp(gcol[h]))
        s16.append(s_scr[h].astype(BF16))

    k16 = [stack(k, p).astype(BF16) for p in tiles]
    decay, lmat, qk = [], [], []
    for p in tiles:
        h0 = p * per_tile
        seg = stack(gcol, p) - gcum_t[SMALL_A + h0:SMALL_A + h0 + 1, :n]
        decay.append(jnp.exp(jnp.where(causal, seg, -jnp.inf)))
    for p in tiles:
        lmat.append(_dot_nt(stack(kb, p).astype(BF16), k16[p]) * jnp.where(diag, 0.0, decay[p]))
    for p in tiles:
        qk.append((_dot_nt(stack(q, p).astype(BF16), k16[p]) * decay[p]).astype(BF16))
    inv = _unit_lower_inverses(lmat, ii, jj, t)

    from_state = [_dot(jnp.concatenate([kb[h] * egcol[h], q[h] * egcol[h]], axis=0).astype(BF16), s16[h])
                  for h in heads]
    rhs = [v[h] * bcol[h] - from_state[h][:t] for h in heads]
    u = [_dot(inv[p].astype(BF16), stack(rhs, p).astype(BF16)) for p in tiles]
    o_intra = [_dot(qk[p], u[p].astype(BF16)) for p in tiles]

    norm_w = nw_ref[...]
    outs = []
    for h in heads:
        p, i = divmod(h, per_tile)
        uh = u[p][i * t:(i + 1) * t].astype(BF16)
        glast = gcum[t - 1:t, SMALL_A + h:SMALL_A + h + 1]
        s_scr[h] = jnp.exp(glast) * s_scr[h] + _dot_tn((k[h] * jnp.exp(glast - gcol[h])).astype(BF16), uh)
        o = from_state[h][t:] + o_intra[p][i * t:(i + 1) * t]
        o = o * lax.rsqrt(jnp.mean(o * o, axis=-1, keepdims=True) + RMS_EPS)
        outs.append(o * norm_w * _silu(zb[:, h * DN_HEAD:(h + 1) * DN_HEAD]))
    o_ref[...] = jnp.concatenate(outs, axis=1).astype(BF16)

    @pl.when(c == last)
    def _():
        snew_ref[0] = s_scr[...]


def _gdn(proj_b, small, consts, conv_state, s0, **where):
    t = where["t"]
    scratch = [pltpu.VMEM((CONV_PAD + t, DN_CONV_CH), F32),
               pltpu.VMEM((DN_HEADS, DN_HEAD, DN_HEAD), F32)]
    return _scan_call(functools.partial(_gdn_kernel, t=t), f"gdn_t{t}", [proj_b, small], consts,
                      [conv_state, s0], DN_V, scratch, **where)


def _merge_kernel(gate_ref, xp_ref, xs_ref, yap_ref, yas_ref, obp_ref, obs_ref, eg_ref, eb_ref, wa_ref, wb_ref,
                  wo_ref, g_ref, b_ref, x1_ref, x1b_ref, *, prompt_tiles):
    is_prompt = pl.program_id(0) < prompt_tiles
    h = _layer_norm(jnp.where(is_prompt, xp_ref[...], xs_ref[...]), eg_ref[...], eb_ref[...])
    out_a = _dot(jnp.where(is_prompt, yap_ref[...], yas_ref[...]), wa_ref[...])
    out_b = _dot(jnp.where(is_prompt, obp_ref[...], obs_ref[...]), wb_ref[...])
    gt = gate_ref[...]
    merged = _sigmoid(gt[:, :D_MODEL]) * out_a + _sigmoid(gt[:, D_MODEL:]) * out_b
    mix = _dot(merged.astype(BF16), wo_ref[...])
    x1 = _layer_norm(DEEPNORM_ALPHA * h + mix, g_ref[...], b_ref[...])
    x1_ref[...] = x1
    x1b_ref[...] = x1.astype(BF16)


def _merge(proj_g, xp, xs, ya_p, ya_s, ob_p, ob_s, consts, tm):
    m = proj_g.shape[0]
    n_p = xp.shape[0]
    tm = _row_tile(math.gcd(n_p, m - n_p), tm)
    in_prompt, in_sample = _group_maps(n_p // tm)
    row = lambda i: (i, 0)
    const = lambda i: (0, 0)
    return pl.pallas_call(
        functools.partial(_merge_kernel, prompt_tiles=n_p // tm),
        grid=(m // tm,),
        in_specs=[pl.BlockSpec((tm, 2 * D_MODEL), row),
                  pl.BlockSpec((tm, D_MODEL), in_prompt), pl.BlockSpec((tm, D_MODEL), in_sample),
                  pl.BlockSpec((tm, SSM_INNER), in_prompt), pl.BlockSpec((tm, SSM_INNER), in_sample),
                  pl.BlockSpec((tm, DN_V), in_prompt), pl.BlockSpec((tm, DN_V), in_sample)]
        + [pl.BlockSpec(a.shape, const) for a in consts],
        out_specs=[pl.BlockSpec((tm, D_MODEL), row), pl.BlockSpec((tm, D_MODEL), row)],
        out_shape=[jax.ShapeDtypeStruct((m, D_MODEL), F32), jax.ShapeDtypeStruct((m, D_MODEL), BF16)],
        compiler_params=_params("parallel"),
        name="merge",
    )(proj_g, xp, xs, ya_p, ya_s, ob_p, ob_s, *consts)


def _take_first_max(rest, index, limit, axis):
    best = jnp.max(rest, axis=axis, keepdims=True)
    first = jnp.min(jnp.where(rest == best, index, limit), axis=axis, keepdims=True)
    hit = index == first
    return hit, jnp.where(hit, -jnp.inf, rest)


def _router_gates_t(x1, xb, rw_hi, rw_lo, bias):
    tm = x1.shape[0]
    x_lo = (x1 - xb.astype(F32)).astype(BF16)
    logits = _dot_nt(rw_hi, xb) + (_dot_nt(rw_hi, x_lo) + _dot_nt(rw_lo, xb))
    scores = _sigmoid(logits)
    row = lax.broadcasted_iota(jnp.int32, (LANES, tm), 0)
    sel = jnp.where(row < N_EXPERTS, scores + bias, -jnp.inf)
    by_group = sel.reshape(LANES // EXPERTS_PER_GROUP, EXPERTS_PER_GROUP, tm)
    sub = lax.broadcasted_iota(jnp.int32, by_group.shape, 1)
    top1 = jnp.max(by_group, axis=1, keepdims=True)
    _, others = _take_first_max(by_group, sub, EXPERTS_PER_GROUP, 1)
    gscore = (top1 + jnp.max(others, axis=1, keepdims=True))[:N_EXPERT_GROUPS]
    gidx = lax.broadcasted_iota(jnp.int32, gscore.shape, 0)
    gkeep = jnp.zeros(gscore.shape, jnp.bool_)
    for _ in range(TOPK_GROUPS):
        hit, gscore = _take_first_max(gscore, gidx, N_EXPERT_GROUPS, 0)
        gkeep = gkeep | hit
    gkeep = jnp.broadcast_to(gkeep, (N_EXPERT_GROUPS, EXPERTS_PER_GROUP, tm)).reshape(N_EXPERTS, tm)
    rest = jnp.where(gkeep, sel[:N_EXPERTS], -jnp.inf)
    eidx = lax.broadcasted_iota(jnp.int32, (N_EXPERTS, tm), 0)
    keep = jnp.zeros((N_EXPERTS, tm), jnp.bool_)
    for _ in range(TOP_K):
        hit, rest = _take_first_max(rest, eidx, N_EXPERTS, 0)
        keep = keep | hit
    picked = jnp.where(keep, scores[:N_EXPERTS], 0.0)
    gates = picked / jnp.sum(picked, axis=0, keepdims=True) * ROUTED_SCALE
    return jnp.concatenate([gates, jnp.zeros((LANES - N_EXPERTS, tm), F32)], axis=0)


def _dense_router_kernel(x1_ref, x1b_ref, pp_ref, ps_ref, sw1_ref, sw3_ref, sw2_ref, pw_ref, pg_ref, rwh_ref,
                         rwl_ref, rb_ref, dense_ref, gates_ref, rank_ref, chosen_t_ref, rank_t_ref, count_ref,
                         *, prompt_tiles):
    xb = x1b_ref[...]
    hid = _silu(_dot(xb, sw1_ref[...])) * _dot(xb, sw3_ref[...])
    shared = _dot(hid.astype(BF16), sw2_ref[...])
    p = jnp.where(pl.program_id(0) < prompt_tiles, pp_ref[...], ps_ref[...])
    ple = _dot(p.astype(BF16), pw_ref[...]) * _sigmoid(_dot(xb, pg_ref[...]))
    x1 = x1_ref[...]
    dense_ref[...] = DEEPNORM_ALPHA * x1 + shared + ple
    gates_t = _router_gates_t(x1, xb, rwh_ref[...], rwl_ref[...], rb_ref[...])
    tm = x1.shape[0]
    chosen_t = (gates_t > 0.0).astype(BF16)
    ii = lax.broadcasted_iota(jnp.int32, (tm, tm), 0)
    jj = lax.broadcasted_iota(jnp.int32, (tm, tm), 1)
    rank_t = _dot(chosen_t, (ii < jj).astype(BF16))
    chosen_t_ref[...] = chosen_t
    rank_t_ref[...] = rank_t.astype(BF16)
    gates = gates_t.T
    rank = rank_t.T
    gates_ref[...] = gates.astype(BF16)
    rank_ref[...] = rank.astype(BF16)
    count = rank[tm - 1:tm, :] + (gates[tm - 1:tm, :] > 0.0).astype(F32)
    count_ref[0] = jnp.broadcast_to(count, (SUBLANES, LANES))


def _dense_router(x1, x1b, p_prompt, p_sample, consts):
    m = x1.shape[0]
    nblk = m // MOE_BLOCK
    prompt_tiles = p_prompt.shape[0] // MOE_BLOCK
    in_prompt, in_sample = _group_maps(prompt_tiles)
    row = lambda i: (i, 0)
    const = lambda i: (0, 0)
    return pl.pallas_call(
        functools.partial(_dense_router_kernel, prompt_tiles=prompt_tiles),
        grid=(nblk,),
        in_specs=[pl.BlockSpec((MOE_BLOCK, D_MODEL), row), pl.BlockSpec((MOE_BLOCK, D_MODEL), row),
                  pl.BlockSpec((MOE_BLOCK, PLE_DIM), in_prompt), pl.BlockSpec((MOE_BLOCK, PLE_DIM), in_sample)]
        + [pl.BlockSpec(a.shape, const) for a in consts],
        out_specs=[pl.BlockSpec((MOE_BLOCK, D_MODEL), row), pl.BlockSpec((MOE_BLOCK, LANES), row),
                   pl.BlockSpec((MOE_BLOCK, LANES), row), pl.BlockSpec((LANES, MOE_BLOCK), row),
                   pl.BlockSpec((LANES, MOE_BLOCK), row), pl.BlockSpec((1, SUBLANES, LANES), lambda i: (i, 0, 0))],
        out_shape=[jax.ShapeDtypeStruct((m, D_MODEL), F32), jax.ShapeDtypeStruct((m, LANES), BF16),
                   jax.ShapeDtypeStruct((m, LANES), BF16), jax.ShapeDtypeStruct((nblk * LANES, MOE_BLOCK), BF16),
                   jax.ShapeDtypeStruct((nblk * LANES, MOE_BLOCK), BF16),
                   jax.ShapeDtypeStruct((nblk, SUBLANES, LANES), F32)],
        compiler_params=_params("parallel"),
        name="dense_router",
    )(x1, x1b, p_prompt, p_sample, *consts)


def _moe_tables(counts, tm):
    nblk = counts.shape[0]
    nchunk_max = MOE_CAP // MOE_SEG
    padded = (counts + MOE_SEG - 1) // MOE_SEG * MOE_SEG
    start = jnp.cumsum(padded, axis=1) - padded
    used = jnp.sum(padded, axis=1)
    rows_e = jnp.sum(padded, axis=0)
    rows_e_t = (rows_e + tm - 1) // tm * tm
    base_e = jnp.cumsum(rows_e_t) - rows_e_t
    seg_row = base_e[None, :] + jnp.cumsum(padded, axis=0) - padded
    chunk0 = jnp.arange(nchunk_max, dtype=jnp.int32) * MOE_SEG
    owner = jnp.sum((start + padded)[:, None, :] <= chunk0[None, :, None], axis=-1)
    onehot = owner[..., None] == jnp.arange(N_EXPERTS, dtype=jnp.int32)
    dst = chunk0[None, :] + jnp.sum(jnp.where(onehot, (seg_row - start)[:, None, :], 0), axis=-1)
    n_tiles = _moe_max_tiles(nblk, tm)
    live = chunk0[None, :] < used[:, None]
    parity = (jnp.arange(nblk, dtype=jnp.int32) % 2)[:, None]
    spare = n_tiles * tm + parity * MOE_CAP + chunk0[None, :]
    src = jnp.where(live, dst, 0)
    dst = jnp.where(live, dst, spare)
    tile_row = jnp.arange(n_tiles, dtype=jnp.int32) * tm
    tiles_used = jnp.sum(rows_e_t) // tm
    tile_row = jnp.minimum(tile_row, (tiles_used - 1) * tm)
    tile_expert = jnp.minimum(jnp.sum((base_e + rows_e_t)[None, :] <= tile_row[:, None], axis=-1), N_EXPERTS - 1)
    bounds = jnp.zeros((nblk, SUBLANES, LANES), F32)
    bounds = bounds.at[:, 0, :N_EXPERTS].set(start.astype(F32)).at[:, 1, :N_EXPERTS].set(padded.astype(F32))
    i32 = lambda a: a.astype(jnp.int32)
    return dict(dst=i32(dst.reshape(-1)), src=i32(src.reshape(-1)), nchunk=i32(used // MOE_SEG),
                pad_row=i32(base_e + rows_e),
                pad_n=i32((rows_e_t - rows_e) // MOE_SEG), tile_expert=i32(tile_expert),
                tiles_used=i32(tiles_used.reshape(1)), bounds=bounds)


def _moe_max_tiles(nblk, tm):
    return (nblk * (MOE_BLOCK * TOP_K + N_EXPERTS * (MOE_SEG - 1)) + N_EXPERTS * (tm - MOE_SEG) + tm - 1) // tm


def _slot_bounds(bounds):
    return bounds[0:1, :], bounds[1:2, :]


def _moe_gather_kernel(dst_ref, nchunk_ref, pad_row_ref, pad_n_ref,
                       x_ref, chosen_t_ref, rank_t_ref, bounds_ref, rows_ref, buf, zbuf, sem, zsem):
    j = pl.program_id(0)
    nblk = pl.num_programs(0)
    slot = j % 2
    nchunk_max = MOE_CAP // MOE_SEG

    def chunk_copy(blk, q, sl):
        src = buf.at[sl, pl.ds(pl.multiple_of(q * MOE_SEG, MOE_SEG), MOE_SEG), :]
        row = pl.multiple_of(dst_ref[blk * nchunk_max + q], MOE_SEG)
        return pltpu.make_async_copy(src, rows_ref.at[pl.ds(row, MOE_SEG), :], sem.at[sl])

    def pad_copy(e, i):
        row = pl.multiple_of(pad_row_ref[e] + i * MOE_SEG, MOE_SEG)
        return pltpu.make_async_copy(zbuf, rows_ref.at[pl.ds(row, MOE_SEG), :], zsem)

    def sub_chunks(blk, sub, sl, action):
        for q in range(sub * (MOE_SUB // MOE_SEG), (sub + 1) * (MOE_SUB // MOE_SEG)):
            action(chunk_copy(blk, q, sl))

    def sub_used(blk, sub):
        return sub * MOE_SUB < nchunk_ref[blk] * MOE_SEG

    def for_pads(action):
        def per_expert(e, carry):
            def body(i, c):
                action(pad_copy(e, i))
                return c
            return lax.fori_loop(0, pad_n_ref[e], body, carry)
        lax.fori_loop(0, N_EXPERTS, per_expert, 0)

    @pl.when(j == 0)
    def _():
        zbuf[...] = jnp.zeros_like(zbuf)
        for_pads(lambda c: c.start())

    start, length = _slot_bounds(bounds_ref[0])
    chosen_t = chosen_t_ref[...]
    rank_t = rank_t_ref[...]
    x = x_ref[...]
    before = jnp.maximum(j - 2, 0)
    for sub in range(MOE_CAP // MOE_SUB):
        @pl.when((j >= 2) & sub_used(before, sub))
        def _():
            sub_chunks(before, sub, slot, lambda c: c.wait())

    for sub in range(MOE_CAP // MOE_SUB):
        @pl.when(sub_used(j, sub))
        def _():
            s = (sub * MOE_SUB + lax.broadcasted_iota(jnp.int32, (MOE_SUB, LANES), 0)).astype(F32)
            owner = (s >= start) & (s < start + length)
            within = s[:, 0:1] - jnp.sum(jnp.where(owner, start, 0.0), axis=1, keepdims=True)
            owner = owner.astype(BF16)
            pick = (_dot(owner, chosen_t) > 0.5) & (_dot(owner, rank_t) == within)
            buf[slot, sub * MOE_SUB:(sub + 1) * MOE_SUB, :] = _dot(pick.astype(BF16), x).astype(BF16)
            sub_chunks(j, sub, slot, lambda c: c.start())

    @pl.when(j == nblk - 1)
    def _():
        prev = jnp.maximum(j - 1, 0)
        for sub in range(MOE_CAP // MOE_SUB):
            @pl.when(sub_used(j, sub))
            def _():
                sub_chunks(j, sub, slot, lambda c: c.wait())

            @pl.when((j >= 1) & sub_used(prev, sub))
            def _():
                sub_chunks(prev, sub, 1 - slot, lambda c: c.wait())

        for_pads(lambda c: c.wait())


def _moe_gather(x1b, chosen_t, rank_t, tables, tm):
    m = x1b.shape[0]
    nblk = m // MOE_BLOCK
    rows = _moe_max_tiles(nblk, tm) * tm + 2 * MOE_CAP
    blk = lambda j, *_: (j, 0)
    grid_spec = pltpu.PrefetchScalarGridSpec(
        num_scalar_prefetch=4,
        grid=(nblk,),
        in_specs=[pl.BlockSpec((MOE_BLOCK, D_MODEL), blk), pl.BlockSpec((LANES, MOE_BLOCK), blk),
                  pl.BlockSpec((LANES, MOE_BLOCK), blk),
                  pl.BlockSpec((1, SUBLANES, LANES), lambda j, *_: (j, 0, 0))],
        out_specs=pl.BlockSpec(memory_space=pl.ANY),
        scratch_shapes=[pltpu.VMEM((2, MOE_CAP, D_MODEL), BF16), pltpu.VMEM((MOE_SEG, D_MODEL), BF16),
                        pltpu.SemaphoreType.DMA((2,)), pltpu.SemaphoreType.DMA(())],
    )
    return pl.pallas_call(
        _moe_gather_kernel,
        grid_spec=grid_spec,
        out_shape=jax.ShapeDtypeStruct((rows, D_MODEL), BF16),
        compiler_params=_params("arbitrary"),
        name="moe_gather",
    )(tables["dst"], tables["nchunk"], tables["pad_row"], tables["pad_n"], x1b, chosen_t, rank_t, tables["bounds"])


def _moe_ffn_kernel(tile_expert_ref, tiles_used_ref, x_ref, w1_ref, w3_ref, w2_ref, y_ref):
    @pl.when(pl.program_id(0) < tiles_used_ref[0])
    def _():
        x = x_ref[...]
        hid = _silu(_dot(x, w1_ref[0].astype(BF16))) * _dot(x, w3_ref[0].astype(BF16))
        y_ref[...] = _dot(hid.astype(BF16), w2_ref[0].astype(BF16)).astype(BF16)


def _moe_ffn(rows, w1, w3, w2, tables, tm):
    n_tiles = tables["tile_expert"].shape[0]
    tile = lambda i, tile_expert, tiles_used: (jnp.minimum(i, tiles_used[0] - 1), 0)
    expert = lambda i, tile_expert, tiles_used: (tile_expert[i], 0, 0)
    grid_spec = pltpu.PrefetchScalarGridSpec(
        num_scalar_prefetch=2,
        grid=(n_tiles,),
        in_specs=[pl.BlockSpec((tm, D_MODEL), tile), pl.BlockSpec((1, D_MODEL, EXPERT_FF), expert),
                  pl.BlockSpec((1, D_MODEL, EXPERT_FF), expert), pl.BlockSpec((1, EXPERT_FF, D_MODEL), expert)],
        out_specs=pl.BlockSpec((tm, D_MODEL), tile),
    )
    return pl.pallas_call(
        _moe_ffn_kernel,
        grid_spec=grid_spec,
        out_shape=jax.ShapeDtypeStruct(rows.shape, BF16),
        compiler_params=_params("arbitrary"),
        name="moe_ffn",
    )(tables["tile_expert"], tables["tiles_used"], rows, w1, w3, w2)


def _moe_combine_kernel(dst_ref, nchunk_ref, dense_ref, gates_ref, rank_ref, bounds_ref, g_ref, b_ref, rows_ref,
                        op_ref, os_ref, acc, buf, sem, *, prompt_tiles):
    j = pl.program_id(0)
    nblk = pl.num_programs(0)
    slot = j % 2
    nchunk_max = MOE_CAP // MOE_SEG

    def chunk_copy(blk, q, sl):
        row = pl.multiple_of(dst_ref[blk * nchunk_max + q], MOE_SEG)
        dst = buf.at[sl, pl.ds(pl.multiple_of(q * MOE_SEG, MOE_SEG), MOE_SEG), :]
        return pltpu.make_async_copy(rows_ref.at[pl.ds(row, MOE_SEG), :], dst, sem.at[sl])

    def for_chunks(blk, sl, action, also=True):
        for sub in range(MOE_CAP // MOE_SUB):
            @pl.when(also & (sub * MOE_SUB < nchunk_ref[blk] * MOE_SEG))
            def _():
                for q in range(sub * (MOE_SUB // MOE_SEG), (sub + 1) * (MOE_SUB // MOE_SEG)):
                    action(chunk_copy(blk, q, sl))

    @pl.when(j == 0)
    def _():
        buf[...] = jnp.zeros_like(buf)
        for_chunks(0, 0, lambda c: c.start())

    for_chunks(jnp.minimum(j + 1, nblk - 1), 1 - slot, lambda c: c.start(), also=j + 1 < nblk)
    for_chunks(j, slot, lambda c: c.wait())

    bounds_t = jnp.concatenate([bounds_ref[0], jnp.zeros((LANES - SUBLANES, LANES), F32)], axis=0).T
    start, length = bounds_t[:, 0:1], bounds_t[:, 1:2]
    gates = gates_ref[...]
    rank = rank_ref[...]
    used = nchunk_ref[j] * MOE_SEG
    acc[...] = dense_ref[...]
    for sub in range(MOE_CAP // MOE_SUB):
        @pl.when(sub * MOE_SUB < used)
        def _():
            s = (sub * MOE_SUB + lax.broadcasted_iota(jnp.int32, (LANES, MOE_SUB), 1)).astype(F32)
            owner = (s >= start) & (s < start + length)
            within = s[0:1, :] - jnp.sum(jnp.where(owner, start, 0.0), axis=0, keepdims=True)
            owner = owner.astype(BF16)
            weight = jnp.where(_dot(rank, owner) == within, _dot(gates, owner), 0.0)
            acc[...] += _dot(weight.astype(BF16), buf[slot, sub * MOE_SUB:(sub + 1) * MOE_SUB, :])

    @pl.when(j < prompt_tiles)
    def _():
        op_ref[...] = _layer_norm(acc[...], g_ref[...], b_ref[...])

    @pl.when(j >= prompt_tiles)
    def _():
        os_ref[...] = _layer_norm(acc[...], g_ref[...], b_ref[...])


def _moe_combine(rows, dense, gates, rank, tables, g, b, n_prompt):
    m = dense.shape[0]
    nblk = m // MOE_BLOCK
    prompt_tiles = n_prompt // MOE_BLOCK
    in_prompt, in_sample = _group_maps(prompt_tiles)
    blk = lambda j, *_: (j, 0)
    const = lambda j, *_: (0, 0)
    grid_spec = pltpu.PrefetchScalarGridSpec(
        num_scalar_prefetch=2,
        grid=(nblk,),
        in_specs=[pl.BlockSpec((MOE_BLOCK, D_MODEL), blk), pl.BlockSpec((MOE_BLOCK, LANES), blk),
                  pl.BlockSpec((MOE_BLOCK, LANES), blk),
                  pl.BlockSpec((1, SUBLANES, LANES), lambda j, *_: (j, 0, 0)),
                  pl.BlockSpec((1, D_MODEL), const), pl.BlockSpec((1, D_MODEL), const),
                  pl.BlockSpec(memory_space=pl.ANY)],
        out_specs=[pl.BlockSpec((MOE_BLOCK, D_MODEL), in_prompt), pl.BlockSpec((MOE_BLOCK, D_MODEL), in_sample)],
        scratch_shapes=[pltpu.VMEM((MOE_BLOCK, D_MODEL), F32), pltpu.VMEM((2, MOE_CAP, D_MODEL), BF16),
                        pltpu.SemaphoreType.DMA((2,))],
    )
    return pl.pallas_call(
        functools.partial(_moe_combine_kernel, prompt_tiles=prompt_tiles),
        grid_spec=grid_spec,
        out_shape=[jax.ShapeDtypeStruct((n_prompt, D_MODEL), F32),
                   jax.ShapeDtypeStruct((m - n_prompt, D_MODEL), F32)],
        compiler_params=_params("arbitrary"),
        name="moe_combine",
    )(tables["src"], tables["nchunk"], dense, gates, rank, tables["bounds"], g, b, rows)


def _row(v):
    return v.reshape(1, -1).astype(F32)


def _lanes_at(v, start):
    return jnp.zeros((1, LANES), F32).at[0, start:start + v.shape[0]].set(v.astype(F32))


def kernel(x_prompt, x_sample, p_prompt, p_sample, state_ssm_conv, state_ssm, state_dn_conv, state_dn, emb_ln_g, emb_ln_b, w_in, conv_a_w, conv_a_b, ssm_dt_bias, ssm_a_log, ssm_d, ssm_norm_w, w_a, conv_b_w, dn_dt_bias, dn_a_log, dn_norm_w, w_b, w_o, ln1_g, ln1_b, router_w, router_bias, exp_w1, exp_w3, exp_w2, sh_w1, sh_w3, sh_w2, ple_w, ple_gate_w, ln2_g, ln2_b):
    bp, lp, _ = x_prompt.shape
    bs, ls, _ = x_sample.shape
    n_p = bp * lp
    n_s = bs * ls
    xp = x_prompt.reshape(n_p, D_MODEL)
    xs = x_sample.reshape(n_s, D_MODEL)

    w = w_in[0]
    o_z, o_xbc, o_dt = 0, SSM_INNER, SSM_INNER + SSM_CONV_CH
    o_qkv = o_dt + SSM_HEADS
    o_a = o_qkv + DN_CONV_CH
    o_b = o_a + DN_HEADS
    o_zb = o_b + DN_HEADS
    o_ga = o_zb + DN_V
    w_pa = w[:, o_z:o_dt].astype(BF16)
    w_pb = jnp.concatenate([w[:, o_qkv:o_a], w[:, o_zb:o_ga]], axis=1).astype(BF16)
    w_pg = w[:, o_ga:].astype(BF16)
    w_ps = jnp.zeros((D_MODEL, LANES), F32)
    w_ps = w_ps.at[:, SMALL_DT:SMALL_DT + SSM_HEADS].set(w[:, o_dt:o_qkv])
    w_ps = w_ps.at[:, SMALL_A:SMALL_A + 2 * DN_HEADS].set(w[:, o_a:o_zb]).astype(BF16)

    eg, eb = _row(emb_ln_g), _row(emb_ln_b)
    proj_a = _ln_matmul(xp, xs, eg, eb, w_pa, 1024, 1536)
    proj_b = _ln_matmul(xp, xs, eg, eb, w_pb, 1024, 1024)
    proj_g = _ln_matmul(xp, xs, eg, eb, w_pg, 1024, 1024)
    small = _ln_matmul(xp, xs, eg, eb, w_ps, 1024, LANES)

    ssd_consts = [conv_a_w[0], _row(conv_a_b[0]), _lanes_at(ssm_dt_bias[0], SMALL_DT),
                  _lanes_at(ssm_a_log[0], SMALL_DT), _row(jnp.repeat(ssm_d[0], SSM_HEAD_DIM)),
                  _row(ssm_norm_w[0])]
    gdn_consts = [conv_b_w[0], _lanes_at(dn_a_log[0], SMALL_A), _lanes_at(dn_dt_bias[0], SMALL_A),
                  _row(dn_norm_w[0])]
    zeros = lambda *s: jnp.zeros(s, F32)
    prompt = dict(batch=bp, seq=lp, row0=0, t=math.gcd(lp, CHUNK))
    sample = dict(batch=bs, seq=ls, row0=n_p, t=math.gcd(ls, CHUNK))

    ya_p, pa_conv, pa_ssm = _ssd(proj_a, small, ssd_consts, zeros(bp, CONV_WIDTH - 1, SSM_CONV_CH),
                                 zeros(bp, SSM_HEADS, SSM_HEAD_DIM, SSM_STATE), **prompt)
    ya_s, sa_conv, sa_ssm = _ssd(proj_a, small, ssd_consts, state_ssm_conv[0], state_ssm[0], **sample)
    ob_p, pb_conv, pb_dn = _gdn(proj_b, small, gdn_consts, zeros(bp, CONV_WIDTH - 1, DN_CONV_CH),
                                zeros(bp, DN_HEADS, DN_HEAD, DN_HEAD), **prompt)
    ob_s, sb_conv, sb_dn = _gdn(proj_b, small, gdn_consts, state_dn_conv[0], state_dn[0], **sample)

    x1, x1b = _merge(
        proj_g, xp, xs, ya_p, ya_s, ob_p, ob_s,
        [eg, eb, w_a[0].astype(BF16), w_b[0].astype(BF16), w_o[0].astype(BF16), _row(ln1_g[0]), _row(ln1_b[0])],
        512)

    router_w_t = jnp.zeros((LANES, D_MODEL), F32).at[:N_EXPERTS].set(router_w[0].T)
    router_w_hi = router_w_t.astype(BF16)
    router_w_lo = (router_w_t - router_w_hi.astype(F32)).astype(BF16)
    router_b = jnp.broadcast_to(_lanes_at(router_bias[0], 0).reshape(LANES, 1), (LANES, MOE_BLOCK))
    dense, gates, rank, chosen_t, rank_t, counts = _dense_router(
        x1, x1b, p_prompt[0].reshape(n_p, PLE_DIM), p_sample[0].reshape(n_s, PLE_DIM),
        [sh_w1[0].astype(BF16), sh_w3[0].astype(BF16), sh_w2[0].astype(BF16), ple_w[0].astype(BF16),
         ple_gate_w[0].astype(BF16), router_w_hi, router_w_lo, router_b])

    tables = _moe_tables(counts[:, 0, :N_EXPERTS].astype(jnp.int32), MOE_TM)
    sorted_rows = _moe_gather(x1b, chosen_t, rank_t, tables, MOE_TM)
    expert_out = _moe_ffn(sorted_rows, exp_w1[0], exp_w3[0], exp_w2[0], tables, MOE_TM)
    out_p, out_s = _moe_combine(expert_out, dense, gates, rank, tables, _row(ln2_g[0]), _row(ln2_b[0]), n_p)

    return (out_p.reshape(bp, lp, D_MODEL), out_s.reshape(bs, ls, D_MODEL),
            pa_conv[None], pa_ssm[None], pb_conv[None], pb_dn[None],
            sa_conv[None], sa_ssm[None], sb_conv[None], sb_dn[None])
```

```python
import functools
import math

import jax
import jax.numpy as jnp
from jax import lax
from jax.experimental import pallas as pl
from jax.experimental.pallas import tpu as pltpu

F32 = jnp.float32
BF16 = jnp.bfloat16
HIGHEST = lax.Precision.HIGHEST

D_MODEL = 1024
SSM_INNER = 2048
SSM_HEAD_DIM = 64
SSM_HEADS = 32
SSM_GROUPS = 2
SSM_HEADS_PER_GROUP = 16
SSM_STATE = 128
SSM_CONV_CH = 2560
DN_HEADS = 8
DN_HEAD = 128
DN_QK = 1024
DN_V = 1024
DN_CONV_CH = 3072
CONV_WIDTH = 4
CHUNK = 64
N_EXPERTS = 64
TOP_K = 8
N_EXPERT_GROUPS = 8
EXPERTS_PER_GROUP = 8
TOPK_GROUPS = 4
EXPERT_FF = 256
SHARED_FF = 256
ROUTED_SCALE = 2.5
PLE_DIM = 256
LN_EPS = 1e-5
RMS_EPS = 1e-6
L2_EPS = 1e-6
DEEPNORM_ALPHA = 2.0 ** 0.25

LANES = 128
SUBLANES = 8
VMEM_LIMIT = 56 * 1024 * 1024
SMALL_DT = 0
SMALL_A = 32
SMALL_B = 40
CONV_PAD = 8
SCAN_CHUNKS = 2
SCAN_SEQS = 2
MOE_BLOCK = 256
MOE_SEG = 16
MOE_SUB = 512
MOE_CAP = -(-(MOE_BLOCK * 8 + 64 * (MOE_SEG - 1)) // MOE_SUB) * MOE_SUB
MOE_TM = 512


def _sigmoid(x):
    return 1.0 / (1.0 + jnp.exp(-x))


def _silu(x):
    return x * _sigmoid(x)


def _softplus(x):
    return jnp.maximum(x, 0.0) + jnp.log(1.0 + jnp.exp(-jnp.abs(x)))


def _layer_norm(x, g, b):
    mu = jnp.mean(x, axis=-1, keepdims=True)
    xc = x - mu
    var = jnp.mean(xc * xc, axis=-1, keepdims=True)
    return xc * lax.rsqrt(var + LN_EPS) * g + b


def _dot(a, b):
    return jnp.dot(a, b, preferred_element_type=F32)


def _dot_nt(a, b):
    return lax.dot_general(a, b, (((1,), (1,)), ((), ())), preferred_element_type=F32)


def _dot_tn(a, b):
    return lax.dot_general(a, b, (((0,), (0,)), ((), ())), preferred_element_type=F32)


def _dot_f32(a, b):
    return jnp.dot(a, b, precision=HIGHEST, preferred_element_type=F32)


def _params(*sem):
    return pltpu.CompilerParams(dimension_semantics=sem, vmem_limit_bytes=VMEM_LIMIT)


def _when(cond):
    if cond is True:
        return lambda fn: fn()
    if cond is False:
        return lambda fn: None
    return pl.when(cond)


def _row_tile(m, preferred):
    return max(d for d in range(SUBLANES, min(m, preferred) + 1, SUBLANES) if m % d == 0)


def _group_maps(prompt_tiles):
    in_prompt = lambda i, *_: (jnp.minimum(i, prompt_tiles - 1), 0)
    in_sample = lambda i, *_: (jnp.maximum(i - prompt_tiles, 0), 0)
    return in_prompt, in_sample


def _ln_matmul_kernel(xp_ref, xs_ref, g_ref, b_ref, w_ref, o_ref, h_scr, *, prompt_tiles):
    @pl.when(pl.program_id(1) == 0)
    def _():
        x = jnp.where(pl.program_id(0) < prompt_tiles, xp_ref[...], xs_ref[...])
        h_scr[...] = _layer_norm(x, g_ref[...], b_ref[...]).astype(BF16)

    o_ref[...] = _dot(h_scr[...], w_ref[...]).astype(o_ref.dtype)


def _ln_matmul(xp, xs, g, b, w, tm, tn, out_dtype):
    k = xp.shape[1]
    m = xp.shape[0] + xs.shape[0]
    n = w.shape[1]
    tm = _row_tile(math.gcd(xp.shape[0], xs.shape[0]), tm)
    prompt_tiles = xp.shape[0] // tm
    in_prompt, in_sample = _group_maps(prompt_tiles)
    return pl.pallas_call(
        functools.partial(_ln_matmul_kernel, prompt_tiles=prompt_tiles),
        grid=(m // tm, n // tn),
        in_specs=[
            pl.BlockSpec((tm, k), in_prompt),
            pl.BlockSpec((tm, k), in_sample),
            pl.BlockSpec((1, k), lambda i, j: (0, 0)),
            pl.BlockSpec((1, k), lambda i, j: (0, 0)),
            pl.BlockSpec((k, tn), lambda i, j: (0, j)),
        ],
        out_specs=pl.BlockSpec((tm, tn), lambda i, j: (i, j)),
        out_shape=jax.ShapeDtypeStruct((m, n), out_dtype),
        scratch_shapes=[pltpu.VMEM((tm, k), BF16)],
        compiler_params=_params("parallel", "arbitrary"),
        name="ln_matmul",
    )(xp, xs, g, b, w)


def _causal_conv(cbuf, xpre, cw, t):
    cbuf[CONV_PAD:CONV_PAD + t, :] = xpre
    lo = CONV_PAD - (CONV_WIDTH - 1)
    y = cbuf[lo:lo + t, :] * cw[0:1]
    y = y + cbuf[lo + 1:lo + 1 + t, :] * cw[1:2]
    y = y + cbuf[lo + 2:lo + 2 + t, :] * cw[2:3]
    y = y + xpre * cw[3:4]
    tail = cbuf[lo + t:CONV_PAD + t, :]
    cbuf[lo:CONV_PAD, :] = tail
    return y, tail


def _lane_expand(v, h0, count, width):
    t = v.shape[0]
    n = count * width
    out = jnp.broadcast_to(v[:, h0:h0 + 1], (t, n))
    if count > 1:
        lane = lax.broadcasted_iota(jnp.int32, (t, n), 1)
        for i in range(1, count):
            out = jnp.where(lane >= i * width, jnp.broadcast_to(v[:, h0 + i:h0 + i + 1], (t, n)), out)
    return out


def _grouped_transpose(v, per_tile):
    t = v.shape[0]
    blocks = [v if r == 0 else pltpu.roll(v, LANES - r, axis=1) for r in range(per_tile)]
    if per_tile * t < LANES:
        blocks.append(jnp.zeros((LANES - per_tile * t, LANES), F32))
    return jnp.concatenate(blocks, axis=0).T


def _same_block(i, j, size):
    shift = size.bit_length() - 1
    return lax.shift_right_logical(i, shift) == lax.shift_right_logical(j, shift)


def _unit_lower_inverses(lmats, ii, jj, t):
    mm = lambda a, b: _dot(a.astype(BF16), b.astype(BF16))
    base = min(16, t)
    in_base = _same_block(ii, jj, base)
    eye = (ii == jj).astype(F32)
    power = [jnp.where(in_base, l, 0.0) for l in lmats]
    inv = [eye - p for p in power]
    span = 2
    while span < base:
        power = [mm(p, p) for p in power]
        inv = [a + mm(a, p) for a, p in zip(inv, power)]
        span *= 2
    size = base
    while size < t:
        link = _same_block(ii, jj, 2 * size) & jnp.logical_not(_same_block(ii, jj, size))
        cross = [mm(a, jnp.where(link, l, 0.0)) for a, l in zip(inv, lmats)]
        inv = [a - mm(c, a) for a, c in zip(inv, cross)]
        size *= 2
    return inv


def _cumsum_rows(v, t):
    ii = lax.broadcasted_iota(jnp.int32, (t, t), 0)
    jj = lax.broadcasted_iota(jnp.int32, (t, t), 1)
    return _dot_f32((jj <= ii).astype(F32), v)


def _ssd_kernel(pa_ref, sm_ref, cw_ref, cb_ref, dtb_ref, alog_ref, dskip_ref, nw_ref, cst_ref, h0_ref,
                y_ref, cnew_ref, hnew_ref, cbuf, s_scr, *, t, first, last):
    per_tile = LANES // t
    width = per_tile * SSM_HEAD_DIM

    @_when(first)
    def _():
        cbuf[CONV_PAD - (CONV_WIDTH - 1):CONV_PAD, :] = cst_ref[0]
        s_scr[...] = h0_ref[0]

    pa = pa_ref[...].astype(F32)
    z = pa[:, :SSM_INNER]
    conv, tail = _causal_conv(cbuf, pa[:, SSM_INNER:], cw_ref[...], t)

    @_when(last)
    def _():
        cnew_ref[0] = tail

    xbc = _silu(conv + cb_ref[...])
    xs = xbc[:, :SSM_INNER]
    bm = xbc[:, SSM_INNER:SSM_INNER + SSM_GROUPS * SSM_STATE]
    cm = xbc[:, SSM_INNER + SSM_GROUPS * SSM_STATE:]

    dt = _softplus(sm_ref[...] + dtb_ref[...])
    da = dt * (-jnp.exp(alog_ref[...]))
    cum = _cumsum_rows(da, t)
    cum_t = _grouped_transpose(cum, per_tile)
    dt_t = _grouped_transpose(dt, per_tile)

    row = lax.broadcasted_iota(jnp.int32, (t, LANES), 0)
    lane = lax.broadcasted_iota(jnp.int32, (t, LANES), 1)
    causal = (lane % t) <= row
    brow = lax.broadcasted_iota(jnp.int32, (LANES, width), 0)
    bcol = lax.broadcasted_iota(jnp.int32, (LANES, width), 1)
    blockdiag = (brow // t) == (bcol // SSM_HEAD_DIM)

    ys = []
    for g in range(SSM_GROUPS):
        bg = bm[:, g * SSM_STATE:(g + 1) * SSM_STATE]
        cg = cm[:, g * SSM_STATE:(g + 1) * SSM_STATE].astype(BF16)
        cb = _dot_nt(cg, jnp.concatenate([bg] * per_tile, axis=0).astype(BF16))
        bg = bg.astype(BF16)
        for q in range(SSM_HEADS_PER_GROUP // per_tile):
            h0 = g * SSM_HEADS_PER_GROUP + q * per_tile
            ccol = _lane_expand(cum, h0, per_tile, t)
            seg = ccol - cum_t[h0:h0 + 1, :]
            decay = jnp.exp(jnp.where(causal, seg, -jnp.inf))
            wts = (cb * decay * dt_t[h0:h0 + 1, :]).astype(BF16)
            xt = xs[:, h0 * SSM_HEAD_DIM:h0 * SSM_HEAD_DIM + width]
            xbd = jnp.where(blockdiag, jnp.concatenate([xt] * per_tile, axis=0), 0.0).astype(BF16)
            y_intra = _dot(wts, xbd)
            st = s_scr[h0:h0 + per_tile].reshape(width, SSM_STATE)
            cum_w = ccol if width == LANES else _lane_expand(cum, h0, per_tile, SSM_HEAD_DIM)
            y_state = _dot_nt(cg, st.astype(BF16)) * jnp.exp(cum_w)
            ys.append(y_intra + y_state)
            wend = jnp.exp(cum_w[t - 1:t, :] - cum_w) * _lane_expand(dt, h0, per_tile, SSM_HEAD_DIM)
            ds = _dot_tn((xt * wend).astype(BF16), bg)
            for i in range(per_tile):
                h = h0 + i
                s_scr[h] = jnp.exp(cum[t - 1:t, h:h + 1]) * s_scr[h] + ds[i * SSM_HEAD_DIM:(i + 1) * SSM_HEAD_DIM]

    y = jnp.concatenate(ys, axis=1) + dskip_ref[...] * xs
    y = y * _silu(z)
    half = SSM_INNER // SSM_GROUPS
    normed = []
    for g in range(SSM_GROUPS):
        yg = y[:, g * half:(g + 1) * half]
        normed.append(yg * lax.rsqrt(jnp.mean(yg * yg, axis=-1, keepdims=True) + RMS_EPS))
    y_ref[...] = (jnp.concatenate(normed, axis=1) * nw_ref[...]).astype(BF16)

    @_when(last)
    def _():
        hnew_ref[0] = s_scr[...]


def _scan_step_kernel(chunk_kernel, n_tok, n_const, n_state, seqs, chunks, t):
    def step(*refs):
        tok = refs[:n_tok]
        const = refs[n_tok:n_tok + n_const]
        st_in = refs[n_tok + n_const:n_tok + n_const + n_state]
        y_ref = refs[n_tok + n_const + n_state]
        st_out = refs[n_tok + n_const + n_state + 1:n_tok + n_const + 2 * n_state + 1]
        scratch = refs[n_tok + n_const + 2 * n_state + 1:]
        first_step = pl.program_id(1) == 0
        last_step = pl.program_id(1) == pl.num_programs(1) - 1
        for s in range(seqs):
            for c in range(chunks):
                rows = pl.ds((s * chunks + c) * t, t)
                chunk_kernel(*[r.at[rows, :] for r in tok], *const, *[r.at[pl.ds(s, 1)] for r in st_in],
                             y_ref.at[rows, :], *[r.at[pl.ds(s, 1)] for r in st_out], *scratch,
                             first=first_step if c == 0 else False,
                             last=last_step if c == chunks - 1 else False)
    return step


def _scan_call(kernel, name, tok_in, const_in, state_in, y_width, scratch, *, batch, seq, row0, t, seqs, chunks):
    nstep = seq // (t * chunks)
    rows = seqs * chunks * t
    blk0 = row0 // rows

    def per_batch(a):
        zeros = (0,) * (a.ndim - 1)
        return pl.BlockSpec((seqs,) + a.shape[1:], lambda b, c: (b,) + zeros)

    return pl.pallas_call(
        _scan_step_kernel(kernel, len(tok_in), len(const_in), len(state_in), seqs, chunks, t),
        grid=(batch // seqs, nstep),
        in_specs=[pl.BlockSpec((rows, a.shape[1]), lambda b, c: (blk0 + b * nstep + c, 0)) for a in tok_in]
        + [pl.BlockSpec(a.shape, lambda b, c: (0, 0)) for a in const_in]
        + [per_batch(a) for a in state_in],
        out_specs=[pl.BlockSpec((rows, y_width), lambda b, c: (b * nstep + c, 0))]
        + [per_batch(a) for a in state_in],
        out_shape=[jax.ShapeDtypeStruct((batch * seq, y_width), BF16)]
        + [jax.ShapeDtypeStruct(a.shape, F32) for a in state_in],
        scratch_shapes=scratch,
        compiler_params=_params("parallel", "arbitrary"),
        name=name,
    )(*tok_in, *const_in, *state_in)


def _ssd(proj_a, small, consts, conv_state, h0, **where):
    t = where["t"]
    scratch = [pltpu.VMEM((CONV_PAD + t, SSM_CONV_CH), F32),
               pltpu.VMEM((SSM_HEADS, SSM_HEAD_DIM, SSM_STATE), F32)]
    return _scan_call(functools.partial(_ssd_kernel, t=t), f"ssd_t{t}", [proj_a, small], consts,
                      [conv_state, h0], SSM_INNER, scratch, **where)


def _gdn_kernel(pb_ref, sm_ref, cw_ref, alog_ref, dtb_ref, nw_ref, cst_ref, s0_ref,
                o_ref, cnew_ref, snew_ref, cbuf, s_scr, *, t, first, last):
    @_when(first)
    def _():
        cbuf[CONV_PAD - (CONV_WIDTH - 1):CONV_PAD, :] = cst_ref[0]
        s_scr[...] = s0_ref[0]

    pb = pb_ref[...].astype(F32)
    zb = pb[:, DN_CONV_CH:]
    conv, tail = _causal_conv(cbuf, pb[:, :DN_CONV_CH], cw_ref[...], t)

    @_when(last)
    def _():
        cnew_ref[0] = tail

    qkv = _silu(conv)
    sm = sm_ref[...]
    beta = _sigmoid(sm)
    gate = -jnp.exp(alog_ref[...]) * _softplus(sm + dtb_ref[...])
    gcum = _cumsum_rows(gate, t)

    per_tile = min(LANES // t, DN_HEADS)
    n = per_tile * t
    tiles = range(DN_HEADS // per_tile)
    heads = range(DN_HEADS)
    gcum_t = _grouped_transpose(gcum, per_tile)
    ii = lax.broadcasted_iota(jnp.int32, (n, n), 0)
    jj = lax.broadcasted_iota(jnp.int32, (n, n), 1)
    same_head = _same_block(ii, jj, t)
    causal = same_head & (jj <= ii)
    diag = ii == jj
    stack = lambda xs, p: jnp.concatenate(xs[p * per_tile:(p + 1) * per_tile], axis=0)

    q, k, v, gcol, bcol, kb, egcol, s16 = [], [], [], [], [], [], [], []
    for h in heads:
        lo = h * DN_HEAD
        qh = qkv[:, lo:lo + DN_HEAD]
        kh = qkv[:, DN_QK + lo:DN_QK + lo + DN_HEAD]
        q.append(qh * lax.rsqrt(jnp.sum(qh * qh, axis=-1, keepdims=True) + L2_EPS) * (DN_HEAD ** -0.5))
        k.append(kh * lax.rsqrt(jnp.sum(kh * kh, axis=-1, keepdims=True) + L2_EPS))
        v.append(qkv[:, 2 * DN_QK + lo:2 * DN_QK + lo + DN_HEAD])
        gcol.append(gcum[:, SMALL_A + h:SMALL_A + h + 1])
        bcol.append(beta[:, SMALL_B + h:SMALL_B + h + 1])
        kb.append(k[h] * bcol[h])
        egcol.append(jnp.exp(gcol[h]))
        s16.append(s_scr[h].astype(BF16))

    k16 = [stack(k, p).astype(BF16) for p in tiles]
    decay, lmat, qk = [], [], []
    for p in tiles:
        h0 = p * per_tile
        seg = stack(gcol, p) - gcum_t[SMALL_A + h0:SMALL_A + h0 + 1, :n]
        decay.append(jnp.exp(jnp.where(causal, seg, -jnp.inf)))
    for p in tiles:
        lmat.append(_dot_nt(stack(kb, p).astype(BF16), k16[p]) * jnp.where(diag, 0.0, decay[p]))
    for p in tiles:
        qk.append((_dot_nt(stack(q, p).astype(BF16), k16[p]) * decay[p]).astype(BF16))
    inv = _unit_lower_inverses(lmat, ii, jj, t)

    from_state = [_dot(jnp.concatenate([kb[h] * egcol[h], q[h] * egcol[h]], axis=0).astype(BF16), s16[h])
                  for h in heads]
    rhs = [v[h] * bcol[h] - from_state[h][:t] for h in heads]
    u = [_dot(inv[p].astype(BF16), stack(rhs, p).astype(BF16)) for p in tiles]
    o_intra = [_dot(qk[p], u[p].astype(BF16)) for p in tiles]

    norm_w = nw_ref[...]
    outs = []
    for h in heads:
        p, i = divmod(h, per_tile)
        uh = u[p][i * t:(i + 1) * t].astype(BF16)
        glast = gcum[t - 1:t, SMALL_A + h:SMALL_A + h + 1]
        s_scr[h] = jnp.exp(glast) * s_scr[h] + _dot_tn((k[h] * jnp.exp(glast - gcol[h])).astype(BF16), uh)
        o = from_state[h][t:] + o_intra[p][i * t:(i + 1) * t]
        o = o * lax.rsqrt(jnp.mean(o * o, axis=-1, keepdims=True) + RMS_EPS)
        outs.append(o * norm_w * _silu(zb[:, h * DN_HEAD:(h + 1) * DN_HEAD]))
    o_ref[...] = jnp.concatenate(outs, axis=1).astype(BF16)

    @_when(last)
    def _():
        snew_ref[0] = s_scr[...]


def _gdn(proj_b, small, consts, conv_state, s0, **where):
    t = where["t"]
    scratch = [pltpu.VMEM((CONV_PAD + t, DN_CONV_CH), F32),
               pltpu.VMEM((DN_HEADS, DN_HEAD, DN_HEAD), F32)]
    return _scan_call(functools.partial(_gdn_kernel, t=t), f"gdn_t{t}", [proj_b, small], consts,
                      [conv_state, s0], DN_V, scratch, **where)


def _merge_kernel(gate_ref, xp_ref, xs_ref, yap_ref, yas_ref, obp_ref, obs_ref, eg_ref, eb_ref, wa_ref, wb_ref,
                  wo_ref, g_ref, b_ref, x1_ref, x1b_ref, *, prompt_tiles):
    is_prompt = pl.program_id(0) < prompt_tiles
    h = _layer_norm(jnp.where(is_prompt, xp_ref[...], xs_ref[...]), eg_ref[...], eb_ref[...])
    out_a = _dot(jnp.where(is_prompt, yap_ref[...], yas_ref[...]), wa_ref[...])
    out_b = _dot(jnp.where(is_prompt, obp_ref[...], obs_ref[...]), wb_ref[...])
    gt = gate_ref[...].astype(F32)
    merged = _sigmoid(gt[:, :D_MODEL]) * out_a + _sigmoid(gt[:, D_MODEL:]) * out_b
    mix = _dot(merged.astype(BF16), wo_ref[...])
    x1 = _layer_norm(DEEPNORM_ALPHA * h + mix, g_ref[...], b_ref[...])
    x1_ref[...] = x1
    x1b_ref[...] = x1.astype(BF16)


def _merge(proj_g, xp, xs, ya_p, ya_s, ob_p, ob_s, consts, tm):
    m = proj_g.shape[0]
    n_p = xp.shape[0]
    tm = _row_tile(math.gcd(n_p, m - n_p), tm)
    in_prompt, in_sample = _group_maps(n_p // tm)
    row = lambda i: (i, 0)
    const = lambda i: (0, 0)
    return pl.pallas_call(
        functools.partial(_merge_kernel, prompt_tiles=n_p // tm),
        grid=(m // tm,),
        in_specs=[pl.BlockSpec((tm, 2 * D_MODEL), row),
                  pl.BlockSpec((tm, D_MODEL), in_prompt), pl.BlockSpec((tm, D_MODEL), in_sample),
                  pl.BlockSpec((tm, SSM_INNER), in_prompt), pl.BlockSpec((tm, SSM_INNER), in_sample),
                  pl.BlockSpec((tm, DN_V), in_prompt), pl.BlockSpec((tm, DN_V), in_sample)]
        + [pl.BlockSpec(a.shape, const) for a in consts],
        out_specs=[pl.BlockSpec((tm, D_MODEL), row), pl.BlockSpec((tm, D_MODEL), row)],
        out_shape=[jax.ShapeDtypeStruct((m, D_MODEL), F32), jax.ShapeDtypeStruct((m, D_MODEL), BF16)],
        compiler_params=_params("parallel"),
        name="merge",
    )(proj_g, xp, xs, ya_p, ya_s, ob_p, ob_s, *consts)


def _take_first_max(rest, index, limit, axis):
    best = jnp.max(rest, axis=axis, keepdims=True)
    first = jnp.min(jnp.where(rest == best, index, limit), axis=axis, keepdims=True)
    hit = index == first
    return hit, jnp.where(hit, -jnp.inf, rest)


def _router_gates_t(x1, xb, rw_hi, rw_lo, bias):
    tm = x1.shape[0]
    x_lo = (x1 - xb.astype(F32)).astype(BF16)
    logits = _dot_nt(rw_hi, xb) + (_dot_nt(rw_hi, x_lo) + _dot_nt(rw_lo, xb))
    scores = _sigmoid(logits)
    row = lax.broadcasted_iota(jnp.int32, (LANES, tm), 0)
    sel = jnp.where(row < N_EXPERTS, scores + bias, -jnp.inf)
    by_group = sel.reshape(LANES // EXPERTS_PER_GROUP, EXPERTS_PER_GROUP, tm)
    sub = lax.broadcasted_iota(jnp.int32, by_group.shape, 1)
    top1 = jnp.max(by_group, axis=1, keepdims=True)
    _, others = _take_first_max(by_group, sub, EXPERTS_PER_GROUP, 1)
    gscore = (top1 + jnp.max(others, axis=1, keepdims=True))[:N_EXPERT_GROUPS]
    gidx = lax.broadcasted_iota(jnp.int32, gscore.shape, 0)
    gkeep = jnp.zeros(gscore.shape, jnp.bool_)
    for _ in range(TOPK_GROUPS):
        hit, gscore = _take_first_max(gscore, gidx, N_EXPERT_GROUPS, 0)
        gkeep = gkeep | hit
    gkeep = jnp.broadcast_to(gkeep, (N_EXPERT_GROUPS, EXPERTS_PER_GROUP, tm)).reshape(N_EXPERTS, tm)
    rest = jnp.where(gkeep, sel[:N_EXPERTS], -jnp.inf)
    eidx = lax.broadcasted_iota(jnp.int32, (N_EXPERTS, tm), 0)
    keep = jnp.zeros((N_EXPERTS, tm), jnp.bool_)
    for _ in range(TOP_K):
        hit, rest = _take_first_max(rest, eidx, N_EXPERTS, 0)
        keep = keep | hit
    picked = jnp.where(keep, scores[:N_EXPERTS], 0.0)
    gates = picked / jnp.sum(picked, axis=0, keepdims=True) * ROUTED_SCALE
    return jnp.concatenate([gates, jnp.zeros((LANES - N_EXPERTS, tm), F32)], axis=0)


def _dense_router_kernel(x1_ref, x1b_ref, pp_ref, ps_ref, sw1_ref, sw3_ref, sw2_ref, pw_ref, pg_ref, rwh_ref,
                         rwl_ref, rb_ref, dense_ref, gates_ref, rank_ref, chosen_t_ref, rank_t_ref, count_ref,
                         *, prompt_tiles):
    xb = x1b_ref[...]
    hid = _silu(_dot(xb, sw1_ref[...])) * _dot(xb, sw3_ref[...])
    shared = _dot(hid.astype(BF16), sw2_ref[...])
    p = jnp.where(pl.program_id(0) < prompt_tiles, pp_ref[...], ps_ref[...])
    ple = _dot(p.astype(BF16), pw_ref[...]) * _sigmoid(_dot(xb, pg_ref[...]))
    x1 = x1_ref[...]
    dense_ref[...] = DEEPNORM_ALPHA * x1 + shared + ple
    gates_t = _router_gates_t(x1, xb, rwh_ref[...], rwl_ref[...], rb_ref[...])
    tm = x1.shape[0]
    chosen_t = (gates_t > 0.0).astype(BF16)
    ii = lax.broadcasted_iota(jnp.int32, (tm, tm), 0)
    jj = lax.broadcasted_iota(jnp.int32, (tm, tm), 1)
    rank_t = _dot(chosen_t, (ii < jj).astype(BF16))
    chosen_t_ref[...] = chosen_t
    rank_t_ref[...] = rank_t.astype(BF16)
    gates = gates_t.T
    rank = rank_t.T
    gates_ref[...] = gates.astype(BF16)
    rank_ref[...] = rank.astype(BF16)
    count = rank[tm - 1:tm, :] + (gates[tm - 1:tm, :] > 0.0).astype(F32)
    count_ref[0] = jnp.broadcast_to(count, (SUBLANES, LANES))


def _dense_router(x1, x1b, p_prompt, p_sample, consts):
    m = x1.shape[0]
    nblk = m // MOE_BLOCK
    prompt_tiles = p_prompt.shape[0] // MOE_BLOCK
    in_prompt, in_sample = _group_maps(prompt_tiles)
    row = lambda i: (i, 0)
    const = lambda i: (0, 0)
    return pl.pallas_call(
        functools.partial(_dense_router_kernel, prompt_tiles=prompt_tiles),
        grid=(nblk,),
        in_specs=[pl.BlockSpec((MOE_BLOCK, D_MODEL), row), pl.BlockSpec((MOE_BLOCK, D_MODEL), row),
                  pl.BlockSpec((MOE_BLOCK, PLE_DIM), in_prompt), pl.BlockSpec((MOE_BLOCK, PLE_DIM), in_sample)]
        + [pl.BlockSpec(a.shape, const) for a in consts],
        out_specs=[pl.BlockSpec((MOE_BLOCK, D_MODEL), row), pl.BlockSpec((MOE_BLOCK, LANES), row),
                   pl.BlockSpec((MOE_BLOCK, LANES), row), pl.BlockSpec((LANES, MOE_BLOCK), row),
                   pl.BlockSpec((LANES, MOE_BLOCK), row), pl.BlockSpec((1, SUBLANES, LANES), lambda i: (i, 0, 0))],
        out_shape=[jax.ShapeDtypeStruct((m, D_MODEL), F32), jax.ShapeDtypeStruct((m, LANES), BF16),
                   jax.ShapeDtypeStruct((m, LANES), BF16), jax.ShapeDtypeStruct((nblk * LANES, MOE_BLOCK), BF16),
                   jax.ShapeDtypeStruct((nblk * LANES, MOE_BLOCK), BF16),
                   jax.ShapeDtypeStruct((nblk, SUBLANES, LANES), F32)],
        compiler_params=_params("parallel"),
        name="dense_router",
    )(x1, x1b, p_prompt, p_sample, *consts)


def _moe_tables(counts, tm):
    nblk = counts.shape[0]
    nchunk_max = MOE_CAP // MOE_SEG
    padded = (counts + MOE_SEG - 1) // MOE_SEG * MOE_SEG
    start = jnp.cumsum(padded, axis=1) - padded
    used = jnp.sum(padded, axis=1)
    rows_e = jnp.sum(padded, axis=0)
    rows_e_t = (rows_e + tm - 1) // tm * tm
    base_e = jnp.cumsum(rows_e_t) - rows_e_t
    seg_row = base_e[None, :] + jnp.cumsum(padded, axis=0) - padded
    chunk0 = jnp.arange(nchunk_max, dtype=jnp.int32) * MOE_SEG
    owner = jnp.sum((start + padded)[:, None, :] <= chunk0[None, :, None], axis=-1)
    onehot = owner[..., None] == jnp.arange(N_EXPERTS, dtype=jnp.int32)
    dst = chunk0[None, :] + jnp.sum(jnp.where(onehot, (seg_row - start)[:, None, :], 0), axis=-1)
    n_tiles = _moe_max_tiles(nblk, tm)
    live = chunk0[None, :] < used[:, None]
    parity = (jnp.arange(nblk, dtype=jnp.int32) % 2)[:, None]
    spare = n_tiles * tm + parity * MOE_CAP + chunk0[None, :]
    src = jnp.where(live, dst, 0)
    dst = jnp.where(live, dst, spare)
    tile_row = jnp.arange(n_tiles, dtype=jnp.int32) * tm
    tiles_used = jnp.sum(rows_e_t) // tm
    tile_row = jnp.minimum(tile_row, (tiles_used - 1) * tm)
    tile_expert = jnp.minimum(jnp.sum((base_e + rows_e_t)[None, :] <= tile_row[:, None], axis=-1), N_EXPERTS - 1)
    bounds = jnp.zeros((nblk, SUBLANES, LANES), F32)
    bounds = bounds.at[:, 0, :N_EXPERTS].set(start.astype(F32)).at[:, 1, :N_EXPERTS].set(padded.astype(F32))
    i32 = lambda a: a.astype(jnp.int32)
    return dict(dst=i32(dst.reshape(-1)), src=i32(src.reshape(-1)), nchunk=i32(used // MOE_SEG),
                pad_row=i32(base_e + rows_e),
                pad_n=i32((rows_e_t - rows_e) // MOE_SEG), tile_expert=i32(tile_expert),
                tiles_used=i32(tiles_used.reshape(1)), bounds=bounds)


def _moe_max_tiles(nblk, tm):
    return (nblk * (MOE_BLOCK * TOP_K + N_EXPERTS * (MOE_SEG - 1)) + N_EXPERTS * (tm - MOE_SEG) + tm - 1) // tm


def _slot_bounds(bounds):
    return bounds[0:1, :], bounds[1:2, :]


def _moe_gather_kernel(dst_ref, nchunk_ref, pad_row_ref, pad_n_ref,
                       x_ref, chosen_t_ref, rank_t_ref, bounds_ref, rows_ref, buf, zbuf, sem, zsem):
    j = pl.program_id(0)
    nblk = pl.num_programs(0)
    slot = j % 2
    nchunk_max = MOE_CAP // MOE_SEG

    def chunk_copy(blk, q, sl):
        src = buf.at[sl, pl.ds(pl.multiple_of(q * MOE_SEG, MOE_SEG), MOE_SEG), :]
        row = pl.multiple_of(dst_ref[blk * nchunk_max + q], MOE_SEG)
        return pltpu.make_async_copy(src, rows_ref.at[pl.ds(row, MOE_SEG), :], sem.at[sl])

    def pad_copy(e, i):
        row = pl.multiple_of(pad_row_ref[e] + i * MOE_SEG, MOE_SEG)
        return pltpu.make_async_copy(zbuf, rows_ref.at[pl.ds(row, MOE_SEG), :], zsem)

    def sub_chunks(blk, sub, sl, action):
        for q in range(sub * (MOE_SUB // MOE_SEG), (sub + 1) * (MOE_SUB // MOE_SEG)):
            action(chunk_copy(blk, q, sl))

    def sub_used(blk, sub):
        return sub * MOE_SUB < nchunk_ref[blk] * MOE_SEG

    def for_pads(action):
        def per_expert(e, carry):
            def body(i, c):
                action(pad_copy(e, i))
                return c
            return lax.fori_loop(0, pad_n_ref[e], body, carry)
        lax.fori_loop(0, N_EXPERTS, per_expert, 0)

    @pl.when(j == 0)
    def _():
        zbuf[...] = jnp.zeros_like(zbuf)
        for_pads(lambda c: c.start())

    start, length = _slot_bounds(bounds_ref[0])
    chosen_t = chosen_t_ref[...]
    rank_t = rank_t_ref[...]
    x = x_ref[...]
    before = jnp.maximum(j - 2, 0)
    for sub in range(MOE_CAP // MOE_SUB):
        @pl.when((j >= 2) & sub_used(before, sub))
        def _():
            sub_chunks(before, sub, slot, lambda c: c.wait())

    for sub in range(MOE_CAP // MOE_SUB):
        @pl.when(sub_used(j, sub))
        def _():
            s = (sub * MOE_SUB + lax.broadcasted_iota(jnp.int32, (MOE_SUB, LANES), 0)).astype(F32)
            owner = (s >= start) & (s < start + length)
            within = s[:, 0:1] - jnp.sum(jnp.where(owner, start, 0.0), axis=1, keepdims=True)
            owner = owner.astype(BF16)
            pick = (_dot(owner, chosen_t) > 0.5) & (_dot(owner, rank_t) == within)
            buf[slot, sub * MOE_SUB:(sub + 1) * MOE_SUB, :] = _dot(pick.astype(BF16), x).astype(BF16)
            sub_chunks(j, sub, slot, lambda c: c.start())

    @pl.when(j == nblk - 1)
    def _():
        prev = jnp.maximum(j - 1, 0)
        for sub in range(MOE_CAP // MOE_SUB):
            @pl.when(sub_used(j, sub))
            def _():
                sub_chunks(j, sub, slot, lambda c: c.wait())

            @pl.when((j >= 1) & sub_used(prev, sub))
            def _():
                sub_chunks(prev, sub, 1 - slot, lambda c: c.wait())

        for_pads(lambda c: c.wait())


def _moe_gather(x1b, chosen_t, rank_t, tables, tm):
    m = x1b.shape[0]
    nblk = m // MOE_BLOCK
    rows = _moe_max_tiles(nblk, tm) * tm + 2 * MOE_CAP
    blk = lambda j, *_: (j, 0)
    grid_spec = pltpu.PrefetchScalarGridSpec(
        num_scalar_prefetch=4,
        grid=(nblk,),
        in_specs=[pl.BlockSpec((MOE_BLOCK, D_MODEL), blk), pl.BlockSpec((LANES, MOE_BLOCK), blk),
                  pl.BlockSpec((LANES, MOE_BLOCK), blk),
                  pl.BlockSpec((1, SUBLANES, LANES), lambda j, *_: (j, 0, 0))],
        out_specs=pl.BlockSpec(memory_space=pl.ANY),
        scratch_shapes=[pltpu.VMEM((2, MOE_CAP, D_MODEL), BF16), pltpu.VMEM((MOE_SEG, D_MODEL), BF16),
                        pltpu.SemaphoreType.DMA((2,)), pltpu.SemaphoreType.DMA(())],
    )
    return pl.pallas_call(
        _moe_gather_kernel,
        grid_spec=grid_spec,
        out_shape=jax.ShapeDtypeStruct((rows, D_MODEL), BF16),
        compiler_params=_params("arbitrary"),
        name="moe_gather",
    )(tables["dst"], tables["nchunk"], tables["pad_row"], tables["pad_n"], x1b, chosen_t, rank_t, tables["bounds"])


def _moe_ffn_kernel(tile_expert_ref, tiles_used_ref, x_ref, w1_ref, w3_ref, w2_ref, y_ref, w13_scr, w2_scr):
    i = pl.program_id(0)

    @pl.when(i < tiles_used_ref[0])
    def _():
        @pl.when((i == 0) | (tile_expert_ref[i] != tile_expert_ref[jnp.maximum(i - 1, 0)]))
        def _():
            w13_scr[:, :EXPERT_FF] = w1_ref[0].astype(BF16)
            w13_scr[:, EXPERT_FF:] = w3_ref[0].astype(BF16)
            w2_scr[...] = w2_ref[0].astype(BF16)

        up = _dot(x_ref[...], w13_scr[...])
        hid = _silu(up[:, :EXPERT_FF]) * up[:, EXPERT_FF:]
        y_ref[...] = _dot(hid.astype(BF16), w2_scr[...]).astype(BF16)


def _moe_ffn(rows, w1, w3, w2, tables, tm):
    n_tiles = tables["tile_expert"].shape[0]
    tile = lambda i, tile_expert, tiles_used: (jnp.minimum(i, tiles_used[0] - 1), 0)
    expert = lambda i, tile_expert, tiles_used: (tile_expert[i], 0, 0)
    grid_spec = pltpu.PrefetchScalarGridSpec(
        num_scalar_prefetch=2,
        grid=(n_tiles,),
        in_specs=[pl.BlockSpec((tm, D_MODEL), tile), pl.BlockSpec((1, D_MODEL, EXPERT_FF), expert),
                  pl.BlockSpec((1, D_MODEL, EXPERT_FF), expert), pl.BlockSpec((1, EXPERT_FF, D_MODEL), expert)],
        out_specs=pl.BlockSpec((tm, D_MODEL), tile),
        scratch_shapes=[pltpu.VMEM((D_MODEL, 2 * EXPERT_FF), BF16), pltpu.VMEM((EXPERT_FF, D_MODEL), BF16)],
    )
    return pl.pallas_call(
        _moe_ffn_kernel,
        grid_spec=grid_spec,
        out_shape=jax.ShapeDtypeStruct(rows.shape, BF16),
        compiler_params=_params("arbitrary"),
        name="moe_ffn",
    )(tables["tile_expert"], tables["tiles_used"], rows, w1, w3, w2)


def _moe_combine_kernel(dst_ref, nchunk_ref, dense_ref, gates_ref, rank_ref, bounds_ref, g_ref, b_ref, rows_ref,
                        op_ref, os_ref, acc, buf, sem, *, prompt_tiles):
    j = pl.program_id(0)
    nblk = pl.num_programs(0)
    slot = j % 2
    nchunk_max = MOE_CAP // MOE_SEG

    def chunk_copy(blk, q, sl):
        row = pl.multiple_of(dst_ref[blk * nchunk_max + q], MOE_SEG)
        dst = buf.at[sl, pl.ds(pl.multiple_of(q * MOE_SEG, MOE_SEG), MOE_SEG), :]
        return pltpu.make_async_copy(rows_ref.at[pl.ds(row, MOE_SEG), :], dst, sem.at[sl])

    def for_chunks(blk, sl, action, also=True):
        for sub in range(MOE_CAP // MOE_SUB):
            @pl.when(also & (sub * MOE_SUB < nchunk_ref[blk] * MOE_SEG))
            def _():
                for q in range(sub * (MOE_SUB // MOE_SEG), (sub + 1) * (MOE_SUB // MOE_SEG)):
                    action(chunk_copy(blk, q, sl))

    @pl.when(j == 0)
    def _():
        buf[...] = jnp.zeros_like(buf)
        for_chunks(0, 0, lambda c: c.start())

    for_chunks(jnp.minimum(j + 1, nblk - 1), 1 - slot, lambda c: c.start(), also=j + 1 < nblk)
    for_chunks(j, slot, lambda c: c.wait())

    bounds_t = jnp.concatenate([bounds_ref[0], jnp.zeros((LANES - SUBLANES, LANES), F32)], axis=0).T
    start, length = bounds_t[:, 0:1], bounds_t[:, 1:2]
    gates = gates_ref[...]
    rank = rank_ref[...]
    used = nchunk_ref[j] * MOE_SEG
    acc[...] = dense_ref[...]
    for sub in range(MOE_CAP // MOE_SUB):
        @pl.when(sub * MOE_SUB < used)
        def _():
            s = (sub * MOE_SUB + lax.broadcasted_iota(jnp.int32, (LANES, MOE_SUB), 1)).astype(F32)
            owner = (s >= start) & (s < start + length)
            within = s[0:1, :] - jnp.sum(jnp.where(owner, start, 0.0), axis=0, keepdims=True)
            owner = owner.astype(BF16)
            weight = jnp.where(_dot(rank, owner) == within, _dot(gates, owner), 0.0)
            acc[...] += _dot(weight.astype(BF16), buf[slot, sub * MOE_SUB:(sub + 1) * MOE_SUB, :])

    @pl.when(j < prompt_tiles)
    def _():
        op_ref[...] = _layer_norm(acc[...], g_ref[...], b_ref[...])

    @pl.when(j >= prompt_tiles)
    def _():
        os_ref[...] = _layer_norm(acc[...], g_ref[...], b_ref[...])


def _moe_combine(rows, dense, gates, rank, tables, g, b, n_prompt):
    m = dense.shape[0]
    nblk = m // MOE_BLOCK
    prompt_tiles = n_prompt // MOE_BLOCK
    in_prompt, in_sample = _group_maps(prompt_tiles)
    blk = lambda j, *_: (j, 0)
    const = lambda j, *_: (0, 0)
    grid_spec = pltpu.PrefetchScalarGridSpec(
        num_scalar_prefetch=2,
        grid=(nblk,),
        in_specs=[pl.BlockSpec((MOE_BLOCK, D_MODEL), blk), pl.BlockSpec((MOE_BLOCK, LANES), blk),
                  pl.BlockSpec((MOE_BLOCK, LANES), blk),
                  pl.BlockSpec((1, SUBLANES, LANES), lambda j, *_: (j, 0, 0)),
                  pl.BlockSpec((1, D_MODEL), const), pl.BlockSpec((1, D_MODEL), const),
                  pl.BlockSpec(memory_space=pl.ANY)],
        out_specs=[pl.BlockSpec((MOE_BLOCK, D_MODEL), in_prompt), pl.BlockSpec((MOE_BLOCK, D_MODEL), in_sample)],
        scratch_shapes=[pltpu.VMEM((MOE_BLOCK, D_MODEL), F32), pltpu.VMEM((2, MOE_CAP, D_MODEL), BF16),
                        pltpu.SemaphoreType.DMA((2,))],
    )
    return pl.pallas_call(
        functools.partial(_moe_combine_kernel, prompt_tiles=prompt_tiles),
        grid_spec=grid_spec,
        out_shape=[jax.ShapeDtypeStruct((n_prompt, D_MODEL), F32),
                   jax.ShapeDtypeStruct((m - n_prompt, D_MODEL), F32)],
        compiler_params=_params("arbitrary"),
        name="moe_combine",
    )(tables["src"], tables["nchunk"], dense, gates, rank, tables["bounds"], g, b, rows)


def _row(v):
    return v.reshape(1, -1).astype(F32)


def _scan_layout(batch, seq, row0):
    t = math.gcd(seq, CHUNK)
    nchunk = seq // t
    chunks = SCAN_CHUNKS if nchunk % SCAN_CHUNKS == 0 else 1
    seqs = SCAN_SEQS if (nchunk == 1 and batch % SCAN_SEQS == 0 and row0 % (SCAN_SEQS * t) == 0) else 1
    return dict(batch=batch, seq=seq, row0=row0, t=t, seqs=seqs, chunks=chunks)


def _lanes_at(v, start):
    return jnp.zeros((1, LANES), F32).at[0, start:start + v.shape[0]].set(v.astype(F32))


def kernel(x_prompt, x_sample, p_prompt, p_sample, state_ssm_conv, state_ssm, state_dn_conv, state_dn, emb_ln_g, emb_ln_b, w_in, conv_a_w, conv_a_b, ssm_dt_bias, ssm_a_log, ssm_d, ssm_norm_w, w_a, conv_b_w, dn_dt_bias, dn_a_log, dn_norm_w, w_b, w_o, ln1_g, ln1_b, router_w, router_bias, exp_w1, exp_w3, exp_w2, sh_w1, sh_w3, sh_w2, ple_w, ple_gate_w, ln2_g, ln2_b):
    bp, lp, _ = x_prompt.shape
    bs, ls, _ = x_sample.shape
    n_p = bp * lp
    n_s = bs * ls
    xp = x_prompt.reshape(n_p, D_MODEL)
    xs = x_sample.reshape(n_s, D_MODEL)

    w = w_in[0]
    o_z, o_xbc, o_dt = 0, SSM_INNER, SSM_INNER + SSM_CONV_CH
    o_qkv = o_dt + SSM_HEADS
    o_a = o_qkv + DN_CONV_CH
    o_b = o_a + DN_HEADS
    o_zb = o_b + DN_HEADS
    o_ga = o_zb + DN_V
    w_pa = w[:, o_z:o_dt].astype(BF16)
    w_pb = jnp.concatenate([w[:, o_qkv:o_a], w[:, o_zb:o_ga]], axis=1).astype(BF16)
    w_pg = w[:, o_ga:].astype(BF16)
    w_ps = jnp.zeros((D_MODEL, LANES), F32)
    w_ps = w_ps.at[:, SMALL_DT:SMALL_DT + SSM_HEADS].set(w[:, o_dt:o_qkv])
    w_ps = w_ps.at[:, SMALL_A:SMALL_A + 2 * DN_HEADS].set(w[:, o_a:o_zb]).astype(BF16)

    eg, eb = _row(emb_ln_g), _row(emb_ln_b)
    proj_a = _ln_matmul(xp, xs, eg, eb, w_pa, 1024, 1536, BF16)
    proj_b = _ln_matmul(xp, xs, eg, eb, w_pb, 1024, 1024, BF16)
    proj_g = _ln_matmul(xp, xs, eg, eb, w_pg, 1024, 1024, BF16)
    small = _ln_matmul(xp, xs, eg, eb, w_ps, 1024, LANES, F32)

    ssd_consts = [conv_a_w[0], _row(conv_a_b[0]), _lanes_at(ssm_dt_bias[0], SMALL_DT),
                  _lanes_at(ssm_a_log[0], SMALL_DT), _row(jnp.repeat(ssm_d[0], SSM_HEAD_DIM)),
                  _row(ssm_norm_w[0])]
    gdn_consts = [conv_b_w[0], _lanes_at(dn_a_log[0], SMALL_A), _lanes_at(dn_dt_bias[0], SMALL_A),
                  _row(dn_norm_w[0])]
    zeros = lambda *s: jnp.zeros(s, F32)
    prompt = _scan_layout(bp, lp, 0)
    sample = _scan_layout(bs, ls, n_p)

    ya_p, pa_conv, pa_ssm = _ssd(proj_a, small, ssd_consts, zeros(bp, CONV_WIDTH - 1, SSM_CONV_CH),
                                 zeros(bp, SSM_HEADS, SSM_HEAD_DIM, SSM_STATE), **prompt)
    ya_s, sa_conv, sa_ssm = _ssd(proj_a, small, ssd_consts, state_ssm_conv[0], state_ssm[0], **sample)
    ob_p, pb_conv, pb_dn = _gdn(proj_b, small, gdn_consts, zeros(bp, CONV_WIDTH - 1, DN_CONV_CH),
                                zeros(bp, DN_HEADS, DN_HEAD, DN_HEAD), **prompt)
    ob_s, sb_conv, sb_dn = _gdn(proj_b, small, gdn_consts, state_dn_conv[0], state_dn[0], **sample)

    x1, x1b = _merge(
        proj_g, xp, xs, ya_p, ya_s, ob_p, ob_s,
        [eg, eb, w_a[0].astype(BF16), w_b[0].astype(BF16), w_o[0].astype(BF16), _row(ln1_g[0]), _row(ln1_b[0])],
        512)

    router_w_t = jnp.zeros((LANES, D_MODEL), F32).at[:N_EXPERTS].set(router_w[0].T)
    router_w_hi = router_w_t.astype(BF16)
    router_w_lo = (router_w_t - router_w_hi.astype(F32)).astype(BF16)
    router_b = jnp.broadcast_to(_lanes_at(router_bias[0], 0).reshape(LANES, 1), (LANES, MOE_BLOCK))
    dense, gates, rank, chosen_t, rank_t, counts = _dense_router(
        x1, x1b, p_prompt[0].reshape(n_p, PLE_DIM), p_sample[0].reshape(n_s, PLE_DIM),
        [sh_w1[0].astype(BF16), sh_w3[0].astype(BF16), sh_w2[0].astype(BF16), ple_w[0].astype(BF16),
         ple_gate_w[0].astype(BF16), router_w_hi, router_w_lo, router_b])

    tables = _moe_tables(counts[:, 0, :N_EXPERTS].astype(jnp.int32), MOE_TM)
    sorted_rows = _moe_gather(x1b, chosen_t, rank_t, tables, MOE_TM)
    expert_out = _moe_ffn(sorted_rows, exp_w1[0], exp_w3[0], exp_w2[0], tables, MOE_TM)
    out_p, out_s = _moe_combine(expert_out, dense, gates, rank, tables, _row(ln2_g[0]), _row(ln2_b[0]), n_p)

    return (out_p.reshape(bp, lp, D_MODEL), out_s.reshape(bs, ls, D_MODEL),
            pa_conv[None], pa_ssm[None], pb_conv[None], pb_dn[None],
            sa_conv[None], sa_ssm[None], sb_conv[None], sb_dn[None])
```

```python
import functools
import math

import jax
import jax.numpy as jnp
from jax import lax
from jax.experimental import pallas as pl
from jax.experimental.pallas import tpu as pltpu

F32 = jnp.float32
BF16 = jnp.bfloat16
HIGHEST = lax.Precision.HIGHEST

D_MODEL = 1024
SSM_INNER = 2048
SSM_HEAD_DIM = 64
SSM_HEADS = 32
SSM_GROUPS = 2
SSM_HEADS_PER_GROUP = 16
SSM_STATE = 128
SSM_CONV_CH = 2560
DN_HEADS = 8
DN_HEAD = 128
DN_QK = 1024
DN_V = 1024
DN_CONV_CH = 3072
CONV_WIDTH = 4
CHUNK = 64
N_EXPERTS = 64
TOP_K = 8
N_EXPERT_GROUPS = 8
EXPERTS_PER_GROUP = 8
TOPK_GROUPS = 4
EXPERT_FF = 256
SHARED_FF = 256
ROUTED_SCALE = 2.5
PLE_DIM = 256
LN_EPS = 1e-5
RMS_EPS = 1e-6
L2_EPS = 1e-6
DEEPNORM_ALPHA = 2.0 ** 0.25

LANES = 128
SUBLANES = 8
VMEM_LIMIT = 56 * 1024 * 1024
SMALL_DT = 0
SMALL_A = 32
SMALL_B = 40
CONV_PAD = 8
SCAN_CHUNKS = 4
SCAN_SEQS = 2
MOE_BLOCK = 256
MOE_SEG = 16
MOE_SUB = 512
MOE_CAP = -(-(MOE_BLOCK * 8 + 64 * (MOE_SEG - 1)) // MOE_SUB) * MOE_SUB
MOE_TM = 512


def _sigmoid(x):
    return 1.0 / (1.0 + jnp.exp(-x))


def _silu(x):
    return x * _sigmoid(x)


def _softplus(x):
    return jnp.maximum(x, 0.0) + jnp.log(1.0 + jnp.exp(-jnp.abs(x)))


def _layer_norm(x, g, b):
    mu = jnp.mean(x, axis=-1, keepdims=True)
    xc = x - mu
    var = jnp.mean(xc * xc, axis=-1, keepdims=True)
    return xc * lax.rsqrt(var + LN_EPS) * g + b


def _dot(a, b):
    return jnp.dot(a, b, preferred_element_type=F32)


def _dot_nt(a, b):
    return lax.dot_general(a, b, (((1,), (1,)), ((), ())), preferred_element_type=F32)


def _dot_tn(a, b):
    return lax.dot_general(a, b, (((0,), (0,)), ((), ())), preferred_element_type=F32)


def _dot_f32(a, b):
    return jnp.dot(a, b, precision=HIGHEST, preferred_element_type=F32)


def _params(*sem):
    return pltpu.CompilerParams(dimension_semantics=sem, vmem_limit_bytes=VMEM_LIMIT)


def _when(cond):
    if cond is True:
        return lambda fn: fn()
    if cond is False:
        return lambda fn: None
    return pl.when(cond)


def _row_tile(m, preferred):
    return max(d for d in range(SUBLANES, min(m, preferred) + 1, SUBLANES) if m % d == 0)


def _group_maps(prompt_tiles):
    in_prompt = lambda i, *_: (jnp.minimum(i, prompt_tiles - 1), 0)
    in_sample = lambda i, *_: (jnp.maximum(i - prompt_tiles, 0), 0)
    return in_prompt, in_sample


def _ln_matmul_kernel(xp_ref, xs_ref, g_ref, b_ref, w_ref, *rest, prompt_tiles, narrow):
    if narrow:
        wn_ref, o_ref, on_ref, h_scr = rest
    else:
        o_ref, h_scr = rest

    @pl.when(pl.program_id(1) == 0)
    def _():
        x = jnp.where(pl.program_id(0) < prompt_tiles, xp_ref[...], xs_ref[...])
        h_scr[...] = _layer_norm(x, g_ref[...], b_ref[...]).astype(BF16)
        if narrow:
            on_ref[...] = _dot(h_scr[...], wn_ref[...])

    o_ref[...] = _dot(h_scr[...], w_ref[...]).astype(o_ref.dtype)


def _ln_matmul(xp, xs, g, b, w, tm, tn, w_narrow=None):
    k = xp.shape[1]
    m = xp.shape[0] + xs.shape[0]
    n = w.shape[1]
    tm = _row_tile(math.gcd(xp.shape[0], xs.shape[0]), tm)
    prompt_tiles = xp.shape[0] // tm
    in_prompt, in_sample = _group_maps(prompt_tiles)
    narrow = w_narrow is not None
    const = lambda i, j: (0, 0)
    row = lambda i, j: (i, 0)
    in_specs = [pl.BlockSpec((tm, k), in_prompt), pl.BlockSpec((tm, k), in_sample),
                pl.BlockSpec((1, k), const), pl.BlockSpec((1, k), const), pl.BlockSpec((k, tn), lambda i, j: (0, j))]
    out_specs = [pl.BlockSpec((tm, tn), lambda i, j: (i, j))]
    out_shape = [jax.ShapeDtypeStruct((m, n), BF16)]
    args = [xp, xs, g, b, w]
    if narrow:
        in_specs.append(pl.BlockSpec((k, LANES), const))
        out_specs.append(pl.BlockSpec((tm, LANES), row))
        out_shape.append(jax.ShapeDtypeStruct((m, LANES), F32))
        args.append(w_narrow)
    out = pl.pallas_call(
        functools.partial(_ln_matmul_kernel, prompt_tiles=prompt_tiles, narrow=narrow),
        grid=(m // tm, n // tn),
        in_specs=in_specs,
        out_specs=out_specs,
        out_shape=out_shape,
        scratch_shapes=[pltpu.VMEM((tm, k), BF16)],
        compiler_params=_params("parallel", "arbitrary"),
        name="ln_matmul",
    )(*args)
    return out if narrow else out[0]


def _load_conv_state(cbuf, state):
    taps = CONV_WIDTH - 1
    cbuf[0:SUBLANES - taps, :] = jnp.zeros((SUBLANES - taps, cbuf.shape[1]), F32)
    cbuf[SUBLANES - taps:SUBLANES, :] = state


def _causal_conv(cbuf, x16, cw, t):
    ch = x16.shape[1]
    prev = cbuf[...]
    hi = prev.astype(BF16).astype(F32)
    mid = (prev - hi).astype(BF16).astype(F32)
    low = ((prev - hi) - mid).astype(BF16).astype(F32)
    x = x16.astype(F32)
    pack = 2 * SUBLANES
    nhead = 4 * SUBLANES if t % pack == 0 else 5 * SUBLANES
    head = jnp.concatenate([hi, mid, low, jnp.zeros((nhead - 3 * SUBLANES, ch), F32)], axis=0)
    if t % pack == 0:
        ext = jnp.concatenate([head.astype(BF16), x16], axis=0)
    else:
        ext = jnp.concatenate([head, x], axis=0).astype(BF16)
    taps = CONV_WIDTH - 1
    out_row = lax.broadcasted_iota(jnp.int32, (taps * t, nhead + t), 0)
    col = lax.broadcasted_iota(jnp.int32, (taps * t, nhead + t), 1)
    log_t = t.bit_length() - 1
    shift = lax.shift_right_logical(out_row, log_t) + 1
    src = (out_row & (t - 1)) - shift
    in_x = (src >= 0) & (col == nhead + src)
    in_prev = (src < 0) & (col < 3 * SUBLANES) & ((col & (SUBLANES - 1)) == SUBLANES + src)
    shifted = _dot((in_x | in_prev).astype(BF16), ext)
    y = shifted[2 * t:3 * t] * cw[0:1]
    y = y + shifted[t:2 * t] * cw[1:2]
    y = y + shifted[0:t] * cw[2:3]
    y = y + x * cw[3:4]
    cbuf[...] = x[t - SUBLANES:t]
    return y, x[t - taps:t]


def _lane_expand(v, h0, count, width):
    t = v.shape[0]
    n = count * width
    out = jnp.broadcast_to(v[:, h0:h0 + 1], (t, n))
    if count > 1:
        lane = lax.broadcasted_iota(jnp.int32, (t, n), 1)
        for i in range(1, count):
            out = jnp.where(lane >= i * width, jnp.broadcast_to(v[:, h0 + i:h0 + i + 1], (t, n)), out)
    return out


def _grouped_transpose(v, per_tile):
    t = v.shape[0]
    blocks = [v if r == 0 else pltpu.roll(v, LANES - r, axis=1) for r in range(per_tile)]
    if per_tile * t < LANES:
        blocks.append(jnp.zeros((LANES - per_tile * t, LANES), F32))
    return jnp.concatenate(blocks, axis=0).T


def _same_block(i, j, size):
    shift = size.bit_length() - 1
    return lax.shift_right_logical(i, shift) == lax.shift_right_logical(j, shift)


def _unit_lower_inverses(lmats, ii, jj, t):
    mm = lambda a, b: _dot(a.astype(BF16), b.astype(BF16))
    base = min(16, t)
    in_base = _same_block(ii, jj, base)
    eye = (ii == jj).astype(F32)
    power = [jnp.where(in_base, l, 0.0) for l in lmats]
    inv = [eye - p for p in power]
    span = 2
    while span < base:
        power = [mm(p, p) for p in power]
        inv = [a + mm(a, p) for a, p in zip(inv, power)]
        span *= 2
    size = base
    while size < t:
        link = _same_block(ii, jj, 2 * size) & jnp.logical_not(_same_block(ii, jj, size))
        cross = [mm(a, jnp.where(link, l, 0.0)) for a, l in zip(inv, lmats)]
        inv = [a - mm(c, a) for a, c in zip(inv, cross)]
        size *= 2
    return inv


def _row_sums_of_squares(blocks):
    t = blocks[0].shape[0]
    sq = jnp.concatenate([b * b for b in blocks], axis=0)
    hi = sq.astype(BF16)
    lo = (sq - hi.astype(F32)).astype(BF16)
    ones = jnp.ones((LANES, LANES), BF16)
    sums = _dot(hi, ones) + _dot(lo, ones)
    return [sums[i * t:(i + 1) * t] for i in range(len(blocks))]


def _cumsum_rows(v, t):
    ii = lax.broadcasted_iota(jnp.int32, (t, t), 0)
    jj = lax.broadcasted_iota(jnp.int32, (t, t), 1)
    return _dot_f32((jj <= ii).astype(F32), v)


def _ssd_kernel(pa_ref, sm_ref, cw_ref, cb_ref, dtb_ref, alog_ref, dskip_ref, nw_ref, cst_ref, h0_ref,
                y_ref, cnew_ref, hnew_ref, cbuf, s_scr, *, t, first, last):
    per_tile = LANES // t
    width = per_tile * SSM_HEAD_DIM

    @_when(first)
    def _():
        _load_conv_state(cbuf, cst_ref[0])
        s_scr[...] = h0_ref[0]

    z = pa_ref[:, :SSM_INNER].astype(F32)
    conv, tail = _causal_conv(cbuf, pa_ref[:, SSM_INNER:], cw_ref[...], t)

    @_when(last)
    def _():
        cnew_ref[0] = tail

    xbc = _silu(conv + cb_ref[...])
    xs = xbc[:, :SSM_INNER]
    bm = xbc[:, SSM_INNER:SSM_INNER + SSM_GROUPS * SSM_STATE]
    cm = xbc[:, SSM_INNER + SSM_GROUPS * SSM_STATE:]

    dt = _softplus(sm_ref[...] + dtb_ref[...])
    da = dt * (-jnp.exp(alog_ref[...]))
    cum = _cumsum_rows(da, t)
    cum_t = _grouped_transpose(cum, per_tile)
    dt_t = _grouped_transpose(dt, per_tile)

    row = lax.broadcasted_iota(jnp.int32, (t, LANES), 0)
    lane = lax.broadcasted_iota(jnp.int32, (t, LANES), 1)
    causal = (lane % t) <= row
    brow = lax.broadcasted_iota(jnp.int32, (LANES, width), 0)
    bcol = lax.broadcasted_iota(jnp.int32, (LANES, width), 1)
    blockdiag = (brow // t) == (bcol // SSM_HEAD_DIM)

    ys = []
    for g in range(SSM_GROUPS):
        bg = bm[:, g * SSM_STATE:(g + 1) * SSM_STATE]
        cg = cm[:, g * SSM_STATE:(g + 1) * SSM_STATE].astype(BF16)
        cb = _dot_nt(cg, jnp.concatenate([bg] * per_tile, axis=0).astype(BF16))
        bg = bg.astype(BF16)
        for q in range(SSM_HEADS_PER_GROUP // per_tile):
            h0 = g * SSM_HEADS_PER_GROUP + q * per_tile
            ccol = _lane_expand(cum, h0, per_tile, t)
            seg = ccol - cum_t[h0:h0 + 1, :]
            decay = jnp.exp(jnp.where(causal, seg, -jnp.inf))
            wts = (cb * decay * dt_t[h0:h0 + 1, :]).astype(BF16)
            xt = xs[:, h0 * SSM_HEAD_DIM:h0 * SSM_HEAD_DIM + width]
            xbd = jnp.where(blockdiag, jnp.concatenate([xt] * per_tile, axis=0), 0.0).astype(BF16)
            y_intra = _dot(wts, xbd)
            st = s_scr[h0:h0 + per_tile].reshape(width, SSM_STATE)
            cum_w = ccol if width == LANES else _lane_expand(cum, h0, per_tile, SSM_HEAD_DIM)
            y_state = _dot_nt(cg, st.astype(BF16)) * jnp.exp(cum_w)
            ys.append(y_intra + y_state)
            wend = jnp.exp(cum_w[t - 1:t, :] - cum_w) * _lane_expand(dt, h0, per_tile, SSM_HEAD_DIM)
            ds = _dot_tn((xt * wend).astype(BF16), bg)
            for i in range(per_tile):
                h = h0 + i
                s_scr[h] = jnp.exp(cum[t - 1:t, h:h + 1]) * s_scr[h] + ds[i * SSM_HEAD_DIM:(i + 1) * SSM_HEAD_DIM]

    y = jnp.concatenate(ys, axis=1) + dskip_ref[...] * xs
    y = y * _silu(z)
    half = SSM_INNER // SSM_GROUPS
    normed = []
    for g in range(SSM_GROUPS):
        yg = y[:, g * half:(g + 1) * half]
        normed.append(yg * lax.rsqrt(jnp.mean(yg * yg, axis=-1, keepdims=True) + RMS_EPS))
    y_ref[...] = (jnp.concatenate(normed, axis=1) * nw_ref[...]).astype(BF16)

    @_when(last)
    def _():
        hnew_ref[0] = s_scr[...]


def _scan_step_kernel(chunk_kernel, n_tok, n_const, n_state, seqs, chunks, t):
    def step(*refs):
        tok = refs[:n_tok]
        const = refs[n_tok:n_tok + n_const]
        st_in = refs[n_tok + n_const:n_tok + n_const + n_state]
        y_ref = refs[n_tok + n_const + n_state]
        st_out = refs[n_tok + n_const + n_state + 1:n_tok + n_const + 2 * n_state + 1]
        scratch = refs[n_tok + n_const + 2 * n_state + 1:]
        first_step = pl.program_id(1) == 0
        last_step = pl.program_id(1) == pl.num_programs(1) - 1
        for s in range(seqs):
            for c in range(chunks):
                rows = pl.ds((s * chunks + c) * t, t)
                chunk_kernel(*[r.at[rows, :] for r in tok], *const, *[r.at[pl.ds(s, 1)] for r in st_in],
                             y_ref.at[rows, :], *[r.at[pl.ds(s, 1)] for r in st_out], *scratch,
                             first=first_step if c == 0 else False,
                             last=last_step if c == chunks - 1 else False)
    return step


def _scan_call(kernel, name, tok_in, const_in, state_in, y_width, scratch, *, batch, seq, row0, t, seqs, chunks):
    nstep = seq // (t * chunks)
    rows = seqs * chunks * t
    blk0 = row0 // rows

    def per_batch(a):
        zeros = (0,) * (a.ndim - 1)
        return pl.BlockSpec((seqs,) + a.shape[1:], lambda b, c: (b,) + zeros)

    return pl.pallas_call(
        _scan_step_kernel(kernel, len(tok_in), len(const_in), len(state_in), seqs, chunks, t),
        grid=(batch // seqs, nstep),
        in_specs=[pl.BlockSpec((rows, a.shape[1]), lambda b, c: (blk0 + b * nstep + c, 0)) for a in tok_in]
        + [pl.BlockSpec(a.shape, lambda b, c: (0, 0)) for a in const_in]
        + [per_batch(a) for a in state_in],
        out_specs=[pl.BlockSpec((rows, y_width), lambda b, c: (b * nstep + c, 0))]
        + [per_batch(a) for a in state_in],
        out_shape=[jax.ShapeDtypeStruct((batch * seq, y_width), BF16)]
        + [jax.ShapeDtypeStruct(a.shape, F32) for a in state_in],
        scratch_shapes=scratch,
        compiler_params=_params("parallel", "arbitrary"),
        name=name,
    )(*tok_in, *const_in, *state_in)


def _ssd(proj_a, small, consts, conv_state, h0, **where):
    t = where["t"]
    scratch = [pltpu.VMEM((SUBLANES, SSM_CONV_CH), F32),
               pltpu.VMEM((SSM_HEADS, SSM_HEAD_DIM, SSM_STATE), F32)]
    return _scan_call(functools.partial(_ssd_kernel, t=t), f"ssd_t{t}", [proj_a, small], consts,
                      [conv_state, h0], SSM_INNER, scratch, **where)


def _gdn_kernel(pb_ref, sm_ref, cw_ref, alog_ref, dtb_ref, nw_ref, cst_ref, s0_ref,
                o_ref, cnew_ref, snew_ref, cbuf, s_scr, *, t, first, last):
    @_when(first)
    def _():
        _load_conv_state(cbuf, cst_ref[0])
        s_scr[...] = s0_ref[0]

    zb = pb_ref[:, DN_CONV_CH:].astype(F32)
    conv, tail = _causal_conv(cbuf, pb_ref[:, :DN_CONV_CH], cw_ref[...], t)

    @_when(last)
    def _():
        cnew_ref[0] = tail

    qkv = _silu(conv)
    sm = sm_ref[...]
    beta = _sigmoid(sm)
    gate = -jnp.exp(alog_ref[...]) * _softplus(sm + dtb_ref[...])
    gcum = _cumsum_rows(gate, t)

    per_tile = min(LANES // t, DN_HEADS)
    n = per_tile * t
    tiles = range(DN_HEADS // per_tile)
    heads = range(DN_HEADS)
    gcum_t = _grouped_transpose(gcum, per_tile)
    ii = lax.broadcasted_iota(jnp.int32, (n, n), 0)
    jj = lax.broadcasted_iota(jnp.int32, (n, n), 1)
    same_head = _same_block(ii, jj, t)
    causal = same_head & (jj <= ii)
    diag = ii == jj
    stack = lambda xs, p: jnp.concatenate(xs[p * per_tile:(p + 1) * per_tile], axis=0)

    q, k, v, gcol, bcol, kb, egcol, s16 = [], [], [], [], [], [], [], []
    sumsq = _row_sums_of_squares([qkv[:, i * DN_HEAD:(i + 1) * DN_HEAD] for i in range(2 * DN_HEADS)])
    for h in heads:
        lo = h * DN_HEAD
        qh = qkv[:, lo:lo + DN_HEAD]
        kh = qkv[:, DN_QK + lo:DN_QK + lo + DN_HEAD]
        q.append(qh * lax.rsqrt(sumsq[h] + L2_EPS) * (DN_HEAD ** -0.5))
        k.append(kh * lax.rsqrt(sumsq[DN_HEADS + h] + L2_EPS))
        v.append(qkv[:, 2 * DN_QK + lo:2 * DN_QK + lo + DN_HEAD])
        gcol.append(gcum[:, SMALL_A + h:SMALL_A + h + 1])
        bcol.append(beta[:, SMALL_B + h:SMALL_B + h + 1])
        kb.append(k[h] * bcol[h])
        egcol.append(jnp.exp(gcol[h]))
        s16.append(s_scr[h].astype(BF16))

    k16 = [stack(k, p).astype(BF16) for p in tiles]
    decay, lmat, qk = [], [], []
    for p in tiles:
        h0 = p * per_tile
        seg = stack(gcol, p) - gcum_t[SMALL_A + h0:SMALL_A + h0 + 1, :n]
        decay.append(jnp.exp(jnp.where(causal, seg, -jnp.inf)))
    for p in tiles:
        lmat.append(_dot_nt(stack(kb, p).astype(BF16), k16[p]) * jnp.where(diag, 0.0, decay[p]))
    for p in tiles:
        qk.append((_dot_nt(stack(q, p).astype(BF16), k16[p]) * decay[p]).astype(BF16))
    inv = _unit_lower_inverses(lmat, ii, jj, t)

    from_state = [_dot(jnp.concatenate([kb[h] * egcol[h], q[h] * egcol[h]], axis=0).astype(BF16), s16[h])
                  for h in heads]
    rhs = [v[h] * bcol[h] - from_state[h][:t] for h in heads]
    u = [_dot(inv[p].astype(BF16), stack(rhs, p).astype(BF16)) for p in tiles]
    o_intra = [_dot(qk[p], u[p].astype(BF16)) for p in tiles]

    norm_w = nw_ref[...]
    outs = []
    for h in heads:
        p, i = divmod(h, per_tile)
        uh = u[p][i * t:(i + 1) * t].astype(BF16)
        glast = gcum[t - 1:t, SMALL_A + h:SMALL_A + h + 1]
        s_scr[h] = jnp.exp(glast) * s_scr[h] + _dot_tn((k[h] * jnp.exp(glast - gcol[h])).astype(BF16), uh)
        outs.append(from_state[h][t:] + o_intra[p][i * t:(i + 1) * t])
    osq = _row_sums_of_squares(outs)
    outs = [o * lax.rsqrt(osq[h] * (1.0 / DN_HEAD) + RMS_EPS) * norm_w * _silu(zb[:, h * DN_HEAD:(h + 1) * DN_HEAD])
            for h, o in enumerate(outs)]
    o_ref[...] = jnp.concatenate(outs, axis=1).astype(BF16)

    @_when(last)
    def _():
        snew_ref[0] = s_scr[...]


def _gdn(proj_b, small, consts, conv_state, s0, **where):
    t = where["t"]
    scratch = [pltpu.VMEM((SUBLANES, DN_CONV_CH), F32),
               pltpu.VMEM((DN_HEADS, DN_HEAD, DN_HEAD), F32)]
    return _scan_call(functools.partial(_gdn_kernel, t=t), f"gdn_t{t}", [proj_b, small], consts,
                      [conv_state, s0], DN_V, scratch, **where)


def _merge_kernel(gate_ref, xp_ref, xs_ref, yap_ref, yas_ref, obp_ref, obs_ref, eg_ref, eb_ref, wa_ref, wb_ref,
                  wo_ref, g_ref, b_ref, x1_ref, x1b_ref, *, prompt_tiles):
    is_prompt = pl.program_id(0) < prompt_tiles
    h = _layer_norm(jnp.where(is_prompt, xp_ref[...], xs_ref[...]), eg_ref[...], eb_ref[...])
    out_a = _dot(jnp.where(is_prompt, yap_ref[...], yas_ref[...]), wa_ref[...])
    out_b = _dot(jnp.where(is_prompt, obp_ref[...], obs_ref[...]), wb_ref[...])
    gt = gate_ref[...].astype(F32)
    merged = _sigmoid(gt[:, :D_MODEL]) * out_a + _sigmoid(gt[:, D_MODEL:]) * out_b
    mix = _dot(merged.astype(BF16), wo_ref[...])
    x1 = _layer_norm(DEEPNORM_ALPHA * h + mix, g_ref[...], b_ref[...])
    x1_ref[...] = x1
    x1b_ref[...] = x1.astype(BF16)


def _merge(proj_g, xp, xs, ya_p, ya_s, ob_p, ob_s, consts, tm):
    m = proj_g.shape[0]
    n_p = xp.shape[0]
    tm = _row_tile(math.gcd(n_p, m - n_p), tm)
    in_prompt, in_sample = _group_maps(n_p // tm)
    row = lambda i: (i, 0)
    const = lambda i: (0, 0)
    return pl.pallas_call(
        functools.partial(_merge_kernel, prompt_tiles=n_p // tm),
        grid=(m // tm,),
        in_specs=[pl.BlockSpec((tm, 2 * D_MODEL), row),
                  pl.BlockSpec((tm, D_MODEL), in_prompt), pl.BlockSpec((tm, D_MODEL), in_sample),
                  pl.BlockSpec((tm, SSM_INNER), in_prompt), pl.BlockSpec((tm, SSM_INNER), in_sample),
                  pl.BlockSpec((tm, DN_V), in_prompt), pl.BlockSpec((tm, DN_V), in_sample)]
        + [pl.BlockSpec(a.shape, const) for a in consts],
        out_specs=[pl.BlockSpec((tm, D_MODEL), row), pl.BlockSpec((tm, D_MODEL), row)],
        out_shape=[jax.ShapeDtypeStruct((m, D_MODEL), F32), jax.ShapeDtypeStruct((m, D_MODEL), BF16)],
        compiler_params=_params("parallel"),
        name="merge",
    )(proj_g, xp, xs, ya_p, ya_s, ob_p, ob_s, *consts)


def _take_first_max(rest, index, limit, axis):
    best = jnp.max(rest, axis=axis, keepdims=True)
    first = jnp.min(jnp.where(rest == best, index, limit), axis=axis, keepdims=True)
    hit = index == first
    return hit, jnp.where(hit, -jnp.inf, rest)


def _router_gates_t(x1, xb, rw_hi, rw_lo, bias):
    tm = x1.shape[0]
    x_lo = (x1 - xb.astype(F32)).astype(BF16)
    logits = _dot_nt(rw_hi, xb) + (_dot_nt(rw_hi, x_lo) + _dot_nt(rw_lo, xb))
    scores = _sigmoid(logits)
    row = lax.broadcasted_iota(jnp.int32, (LANES, tm), 0)
    sel = jnp.where(row < N_EXPERTS, scores + bias, -jnp.inf)
    by_group = sel.reshape(LANES // EXPERTS_PER_GROUP, EXPERTS_PER_GROUP, tm)
    sub = lax.broadcasted_iota(jnp.int32, by_group.shape, 1)
    top1 = jnp.max(by_group, axis=1, keepdims=True)
    _, others = _take_first_max(by_group, sub, EXPERTS_PER_GROUP, 1)
    gscore = (top1 + jnp.max(others, axis=1, keepdims=True))[:N_EXPERT_GROUPS]
    gidx = lax.broadcasted_iota(jnp.int32, gscore.shape, 0)
    gkeep = jnp.zeros(gscore.shape, jnp.bool_)
    for _ in range(TOPK_GROUPS):
        hit, gscore = _take_first_max(gscore, gidx, N_EXPERT_GROUPS, 0)
        gkeep = gkeep | hit
    gkeep = jnp.broadcast_to(gkeep, (N_EXPERT_GROUPS, EXPERTS_PER_GROUP, tm)).reshape(N_EXPERTS, tm)
    rest = jnp.where(gkeep, sel[:N_EXPERTS], -jnp.inf)
    eidx = lax.broadcasted_iota(jnp.int32, (N_EXPERTS, tm), 0)
    keep = jnp.zeros((N_EXPERTS, tm), jnp.bool_)
    for _ in range(TOP_K):
        hit, rest = _take_first_max(rest, eidx, N_EXPERTS, 0)
        keep = keep | hit
    picked = jnp.where(keep, scores[:N_EXPERTS], 0.0)
    gates = picked / jnp.sum(picked, axis=0, keepdims=True) * ROUTED_SCALE
    return jnp.concatenate([gates, jnp.zeros((LANES - N_EXPERTS, tm), F32)], axis=0)


def _dense_router_kernel(x1_ref, x1b_ref, pp_ref, ps_ref, sw1_ref, sw3_ref, sw2_ref, pw_ref, pg_ref, rwh_ref,
                         rwl_ref, rb_ref, dense_ref, gates_ref, rank_ref, chosen_t_ref, rank_t_ref, count_ref,
                         *, prompt_tiles):
    xb = x1b_ref[...]
    hid = _silu(_dot(xb, sw1_ref[...])) * _dot(xb, sw3_ref[...])
    shared = _dot(hid.astype(BF16), sw2_ref[...])
    p = jnp.where(pl.program_id(0) < prompt_tiles, pp_ref[...], ps_ref[...])
    ple = _dot(p.astype(BF16), pw_ref[...]) * _sigmoid(_dot(xb, pg_ref[...]))
    x1 = x1_ref[...]
    dense_ref[...] = DEEPNORM_ALPHA * x1 + shared + ple
    gates_t = _router_gates_t(x1, xb, rwh_ref[...], rwl_ref[...], rb_ref[...])
    tm = x1.shape[0]
    chosen_t = (gates_t > 0.0).astype(BF16)
    ii = lax.broadcasted_iota(jnp.int32, (tm, tm), 0)
    jj = lax.broadcasted_iota(jnp.int32, (tm, tm), 1)
    rank_t = _dot(chosen_t, (ii < jj).astype(BF16))
    chosen_t_ref[...] = chosen_t
    rank_t_ref[...] = rank_t.astype(BF16)
    gates = gates_t.T
    rank = rank_t.T
    gates_ref[...] = gates.astype(BF16)
    rank_ref[...] = rank.astype(BF16)
    count = rank[tm - 1:tm, :] + (gates[tm - 1:tm, :] > 0.0).astype(F32)
    count_ref[0] = jnp.broadcast_to(count, (SUBLANES, LANES))


def _dense_router(x1, x1b, p_prompt, p_sample, consts):
    m = x1.shape[0]
    nblk = m // MOE_BLOCK
    prompt_tiles = p_prompt.shape[0] // MOE_BLOCK
    in_prompt, in_sample = _group_maps(prompt_tiles)
    row = lambda i: (i, 0)
    const = lambda i: (0, 0)
    return pl.pallas_call(
        functools.partial(_dense_router_kernel, prompt_tiles=prompt_tiles),
        grid=(nblk,),
        in_specs=[pl.BlockSpec((MOE_BLOCK, D_MODEL), row), pl.BlockSpec((MOE_BLOCK, D_MODEL), row),
                  pl.BlockSpec((MOE_BLOCK, PLE_DIM), in_prompt), pl.BlockSpec((MOE_BLOCK, PLE_DIM), in_sample)]
        + [pl.BlockSpec(a.shape, const) for a in consts],
        out_specs=[pl.BlockSpec((MOE_BLOCK, D_MODEL), row), pl.BlockSpec((MOE_BLOCK, LANES), row),
                   pl.BlockSpec((MOE_BLOCK, LANES), row), pl.BlockSpec((LANES, MOE_BLOCK), row),
                   pl.BlockSpec((LANES, MOE_BLOCK), row), pl.BlockSpec((1, SUBLANES, LANES), lambda i: (i, 0, 0))],
        out_shape=[jax.ShapeDtypeStruct((m, D_MODEL), F32), jax.ShapeDtypeStruct((m, LANES), BF16),
                   jax.ShapeDtypeStruct((m, LANES), BF16), jax.ShapeDtypeStruct((nblk * LANES, MOE_BLOCK), BF16),
                   jax.ShapeDtypeStruct((nblk * LANES, MOE_BLOCK), BF16),
                   jax.ShapeDtypeStruct((nblk, SUBLANES, LANES), F32)],
        compiler_params=_params("parallel"),
        name="dense_router",
    )(x1, x1b, p_prompt, p_sample, *consts)


def _moe_tables(counts, tm):
    nblk = counts.shape[0]
    nchunk_max = MOE_CAP // MOE_SEG
    padded = (counts + MOE_SEG - 1) // MOE_SEG * MOE_SEG
    start = jnp.cumsum(padded, axis=1) - padded
    used = jnp.sum(padded, axis=1)
    rows_e = jnp.sum(padded, axis=0)
    rows_e_t = (rows_e + tm - 1) // tm * tm
    base_e = jnp.cumsum(rows_e_t) - rows_e_t
    seg_row = base_e[None, :] + jnp.cumsum(padded, axis=0) - padded
    chunk0 = jnp.arange(nchunk_max, dtype=jnp.int32) * MOE_SEG
    owner = jnp.sum((start + padded)[:, None, :] <= chunk0[None, :, None], axis=-1)
    onehot = owner[..., None] == jnp.arange(N_EXPERTS, dtype=jnp.int32)
    dst = chunk0[None, :] + jnp.sum(jnp.where(onehot, (seg_row - start)[:, None, :], 0), axis=-1)
    n_tiles = _moe_max_tiles(nblk, tm)
    live = chunk0[None, :] < used[:, None]
    parity = (jnp.arange(nblk, dtype=jnp.int32) % 2)[:, None]
    spare = n_tiles * tm + parity * MOE_CAP + chunk0[None, :]
    src = jnp.where(live, dst, 0)
    dst = jnp.where(live, dst, spare)
    tile_row = jnp.arange(n_tiles, dtype=jnp.int32) * tm
    tiles_used = jnp.sum(rows_e_t) // tm
    tile_row = jnp.minimum(tile_row, (tiles_used - 1) * tm)
    tile_expert = jnp.minimum(jnp.sum((base_e + rows_e_t)[None, :] <= tile_row[:, None], axis=-1), N_EXPERTS - 1)
    bounds = jnp.zeros((nblk, SUBLANES, LANES), F32)
    bounds = bounds.at[:, 0, :N_EXPERTS].set(start.astype(F32)).at[:, 1, :N_EXPERTS].set(padded.astype(F32))
    i32 = lambda a: a.astype(jnp.int32)
    return dict(dst=i32(dst.reshape(-1)), src=i32(src.reshape(-1)), nchunk=i32(used // MOE_SEG),
                pad_row=i32(base_e + rows_e),
                pad_n=i32((rows_e_t - rows_e) // MOE_SEG), tile_expert=i32(tile_expert),
                tiles_used=i32(tiles_used.reshape(1)), bounds=bounds)


def _moe_max_tiles(nblk, tm):
    return (nblk * (MOE_BLOCK * TOP_K + N_EXPERTS * (MOE_SEG - 1)) + N_EXPERTS * (tm - MOE_SEG) + tm - 1) // tm


def _slot_bounds(bounds):
    return bounds[0:1, :], bounds[1:2, :]


def _moe_gather_kernel(dst_ref, nchunk_ref, pad_row_ref, pad_n_ref,
                       x_ref, chosen_t_ref, rank_t_ref, bounds_ref, rows_ref, buf, zbuf, sem, zsem):
    j = pl.program_id(0)
    nblk = pl.num_programs(0)
    slot = j % 2
    nchunk_max = MOE_CAP // MOE_SEG

    def chunk_copy(blk, q, sl):
        src = buf.at[sl, pl.ds(pl.multiple_of(q * MOE_SEG, MOE_SEG), MOE_SEG), :]
        row = pl.multiple_of(dst_ref[blk * nchunk_max + q], MOE_SEG)
        return pltpu.make_async_copy(src, rows_ref.at[pl.ds(row, MOE_SEG), :], sem.at[sl])

    def pad_copy(e, i):
        row = pl.multiple_of(pad_row_ref[e] + i * MOE_SEG, MOE_SEG)
        return pltpu.make_async_copy(zbuf, rows_ref.at[pl.ds(row, MOE_SEG), :], zsem)

    def sub_chunks(blk, sub, sl, action):
        for q in range(sub * (MOE_SUB // MOE_SEG), (sub + 1) * (MOE_SUB // MOE_SEG)):
            action(chunk_copy(blk, q, sl))

    def sub_used(blk, sub):
        return sub * MOE_SUB < nchunk_ref[blk] * MOE_SEG

    def for_pads(action):
        def per_expert(e, carry):
            def body(i, c):
                action(pad_copy(e, i))
                return c
            return lax.fori_loop(0, pad_n_ref[e], body, carry)
        lax.fori_loop(0, N_EXPERTS, per_expert, 0)

    @pl.when(j == 0)
    def _():
        zbuf[...] = jnp.zeros_like(zbuf)
        for_pads(lambda c: c.start())

    start, length = _slot_bounds(bounds_ref[0])
    chosen_t = chosen_t_ref[...]
    rank_t = rank_t_ref[...]
    x = x_ref[...]
    before = jnp.maximum(j - 2, 0)
    for sub in range(MOE_CAP // MOE_SUB):
        @pl.when((j >= 2) & sub_used(before, sub))
        def _():
            sub_chunks(before, sub, slot, lambda c: c.wait())

    for sub in range(MOE_CAP // MOE_SUB):
        @pl.when(sub_used(j, sub))
        def _():
            s = (sub * MOE_SUB + lax.broadcasted_iota(jnp.int32, (MOE_SUB, LANES), 0)).astype(F32)
            owner = (s >= start) & (s < start + length)
            within = s[:, 0:1] - jnp.sum(jnp.where(owner, start, 0.0), axis=1, keepdims=True)
            owner = owner.astype(BF16)
            pick = (_dot(owner, chosen_t) > 0.5) & (_dot(owner, rank_t) == within)
            buf[slot, sub * MOE_SUB:(sub + 1) * MOE_SUB, :] = _dot(pick.astype(BF16), x).astype(BF16)
            sub_chunks(j, sub, slot, lambda c: c.start())

    @pl.when(j == nblk - 1)
    def _():
        prev = jnp.maximum(j - 1, 0)
        for sub in range(MOE_CAP // MOE_SUB):
            @pl.when(sub_used(j, sub))
            def _():
                sub_chunks(j, sub, slot, lambda c: c.wait())

            @pl.when((j >= 1) & sub_used(prev, sub))
            def _():
                sub_chunks(prev, sub, 1 - slot, lambda c: c.wait())

        for_pads(lambda c: c.wait())


def _moe_gather(x1b, chosen_t, rank_t, tables, tm):
    m = x1b.shape[0]
    nblk = m // MOE_BLOCK
    rows = _moe_max_tiles(nblk, tm) * tm + 2 * MOE_CAP
    blk = lambda j, *_: (j, 0)
    grid_spec = pltpu.PrefetchScalarGridSpec(
        num_scalar_prefetch=4,
        grid=(nblk,),
        in_specs=[pl.BlockSpec((MOE_BLOCK, D_MODEL), blk), pl.BlockSpec((LANES, MOE_BLOCK), blk),
                  pl.BlockSpec((LANES, MOE_BLOCK), blk),
                  pl.BlockSpec((1, SUBLANES, LANES), lambda j, *_: (j, 0, 0))],
        out_specs=pl.BlockSpec(memory_space=pl.ANY),
        scratch_shapes=[pltpu.VMEM((2, MOE_CAP, D_MODEL), BF16), pltpu.VMEM((MOE_SEG, D_MODEL), BF16),
                        pltpu.SemaphoreType.DMA((2,)), pltpu.SemaphoreType.DMA(())],
    )
    return pl.pallas_call(
        _moe_gather_kernel,
        grid_spec=grid_spec,
        out_shape=jax.ShapeDtypeStruct((rows, D_MODEL), BF16),
        compiler_params=_params("arbitrary"),
        name="moe_gather",
    )(tables["dst"], tables["nchunk"], tables["pad_row"], tables["pad_n"], x1b, chosen_t, rank_t, tables["bounds"])


def _moe_ffn_kernel(tile_expert_ref, tiles_used_ref, x_ref, w1_ref, w3_ref, w2_ref, y_ref, w13_scr, w2_scr):
    i = pl.program_id(0)

    @pl.when(i < tiles_used_ref[0])
    def _():
        @pl.when((i == 0) | (tile_expert_ref[i] != tile_expert_ref[jnp.maximum(i - 1, 0)]))
        def _():
            w13_scr[:, :EXPERT_FF] = w1_ref[0].astype(BF16)
            w13_scr[:, EXPERT_FF:] = w3_ref[0].astype(BF16)
            w2_scr[...] = w2_ref[0].astype(BF16)

        up = _dot(x_ref[...], w13_scr[...])
        hid = _silu(up[:, :EXPERT_FF]) * up[:, EXPERT_FF:]
        y_ref[...] = _dot(hid.astype(BF16), w2_scr[...]).astype(BF16)


def _moe_ffn(rows, w1, w3, w2, tables, tm):
    n_tiles = tables["tile_expert"].shape[0]
    tile = lambda i, tile_expert, tiles_used: (jnp.minimum(i, tiles_used[0] - 1), 0)
    expert = lambda i, tile_expert, tiles_used: (tile_expert[i], 0, 0)
    grid_spec = pltpu.PrefetchScalarGridSpec(
        num_scalar_prefetch=2,
        grid=(n_tiles,),
        in_specs=[pl.BlockSpec((tm, D_MODEL), tile), pl.BlockSpec((1, D_MODEL, EXPERT_FF), expert),
                  pl.BlockSpec((1, D_MODEL, EXPERT_FF), expert), pl.BlockSpec((1, EXPERT_FF, D_MODEL), expert)],
        out_specs=pl.BlockSpec((tm, D_MODEL), tile),
        scratch_shapes=[pltpu.VMEM((D_MODEL, 2 * EXPERT_FF), BF16), pltpu.VMEM((EXPERT_FF, D_MODEL), BF16)],
    )
    return pl.pallas_call(
        _moe_ffn_kernel,
        grid_spec=grid_spec,
        out_shape=jax.ShapeDtypeStruct(rows.shape, BF16),
        compiler_params=_params("arbitrary"),
        name="moe_ffn",
    )(tables["tile_expert"], tables["tiles_used"], rows, w1, w3, w2)


def _moe_combine_kernel(dst_ref, nchunk_ref, dense_ref, gates_ref, rank_ref, bounds_ref, g_ref, b_ref, rows_ref,
                        op_ref, os_ref, acc, buf, sem, *, prompt_tiles):
    j = pl.program_id(0)
    nblk = pl.num_programs(0)
    slot = j % 2
    nchunk_max = MOE_CAP // MOE_SEG

    def chunk_copy(blk, q, sl):
        row = pl.multiple_of(dst_ref[blk * nchunk_max + q], MOE_SEG)
        dst = buf.at[sl, pl.ds(pl.multiple_of(q * MOE_SEG, MOE_SEG), MOE_SEG), :]
        return pltpu.make_async_copy(rows_ref.at[pl.ds(row, MOE_SEG), :], dst, sem.at[sl])

    def for_chunks(blk, sl, action, also=True):
        for sub in range(MOE_CAP // MOE_SUB):
            @pl.when(also & (sub * MOE_SUB < nchunk_ref[blk] * MOE_SEG))
            def _():
                for q in range(sub * (MOE_SUB // MOE_SEG), (sub + 1) * (MOE_SUB // MOE_SEG)):
                    action(chunk_copy(blk, q, sl))

    @pl.when(j == 0)
    def _():
        buf[...] = jnp.zeros_like(buf)
        for_chunks(0, 0, lambda c: c.start())

    for_chunks(jnp.minimum(j + 1, nblk - 1), 1 - slot, lambda c: c.start(), also=j + 1 < nblk)
    for_chunks(j, slot, lambda c: c.wait())

    bounds_t = jnp.concatenate([bounds_ref[0], jnp.zeros((LANES - SUBLANES, LANES), F32)], axis=0).T
    start, length = bounds_t[:, 0:1], bounds_t[:, 1:2]
    gates = gates_ref[...]
    rank = rank_ref[...]
    used = nchunk_ref[j] * MOE_SEG
    acc[...] = dense_ref[...]
    for sub in range(MOE_CAP // MOE_SUB):
        @pl.when(sub * MOE_SUB < used)
        def _():
            s = (sub * MOE_SUB + lax.broadcasted_iota(jnp.int32, (LANES, MOE_SUB), 1)).astype(F32)
            owner = (s >= start) & (s < start + length)
            within = s[0:1, :] - jnp.sum(jnp.where(owner, start, 0.0), axis=0, keepdims=True)
            owner = owner.astype(BF16)
            weight = jnp.where(_dot(rank, owner) == within, _dot(gates, owner), 0.0)
            acc[...] += _dot(weight.astype(BF16), buf[slot, sub * MOE_SUB:(sub + 1) * MOE_SUB, :])

    @pl.when(j < prompt_tiles)
    def _():
        op_ref[...] = _layer_norm(acc[...], g_ref[...], b_ref[...])

    @pl.when(j >= prompt_tiles)
    def _():
        os_ref[...] = _layer_norm(acc[...], g_ref[...], b_ref[...])


def _moe_combine(rows, dense, gates, rank, tables, g, b, n_prompt):
    m = dense.shape[0]
    nblk = m // MOE_BLOCK
    prompt_tiles = n_prompt // MOE_BLOCK
    in_prompt, in_sample = _group_maps(prompt_tiles)
    blk = lambda j, *_: (j, 0)
    const = lambda j, *_: (0, 0)
    grid_spec = pltpu.PrefetchScalarGridSpec(
        num_scalar_prefetch=2,
        grid=(nblk,),
        in_specs=[pl.BlockSpec((MOE_BLOCK, D_MODEL), blk), pl.BlockSpec((MOE_BLOCK, LANES), blk),
                  pl.BlockSpec((MOE_BLOCK, LANES), blk),
                  pl.BlockSpec((1, SUBLANES, LANES), lambda j, *_: (j, 0, 0)),
                  pl.BlockSpec((1, D_MODEL), const), pl.BlockSpec((1, D_MODEL), const),
                  pl.BlockSpec(memory_space=pl.ANY)],
        out_specs=[pl.BlockSpec((MOE_BLOCK, D_MODEL), in_prompt), pl.BlockSpec((MOE_BLOCK, D_MODEL), in_sample)],
        scratch_shapes=[pltpu.VMEM((MOE_BLOCK, D_MODEL), F32), pltpu.VMEM((2, MOE_CAP, D_MODEL), BF16),
                        pltpu.SemaphoreType.DMA((2,))],
    )
    return pl.pallas_call(
        functools.partial(_moe_combine_kernel, prompt_tiles=prompt_tiles),
        grid_spec=grid_spec,
        out_shape=[jax.ShapeDtypeStruct((n_prompt, D_MODEL), F32),
                   jax.ShapeDtypeStruct((m - n_prompt, D_MODEL), F32)],
        compiler_params=_params("arbitrary"),
        name="moe_combine",
    )(tables["src"], tables["nchunk"], dense, gates, rank, tables["bounds"], g, b, rows)


def _row(v):
    return v.reshape(1, -1).astype(F32)


def _scan_layout(batch, seq, row0):
    t = math.gcd(seq, CHUNK)
    nchunk = seq // t
    chunks = SCAN_CHUNKS if nchunk % SCAN_CHUNKS == 0 else 1
    seqs = SCAN_SEQS if (nchunk == 1 and batch % SCAN_SEQS == 0 and row0 % (SCAN_SEQS * t) == 0) else 1
    return dict(batch=batch, seq=seq, row0=row0, t=t, seqs=seqs, chunks=chunks)


def _lanes_at(v, start):
    return jnp.zeros((1, LANES), F32).at[0, start:start + v.shape[0]].set(v.astype(F32))


def kernel(x_prompt, x_sample, p_prompt, p_sample, state_ssm_conv, state_ssm, state_dn_conv, state_dn, emb_ln_g, emb_ln_b, w_in, conv_a_w, conv_a_b, ssm_dt_bias, ssm_a_log, ssm_d, ssm_norm_w, w_a, conv_b_w, dn_dt_bias, dn_a_log, dn_norm_w, w_b, w_o, ln1_g, ln1_b, router_w, router_bias, exp_w1, exp_w3, exp_w2, sh_w1, sh_w3, sh_w2, ple_w, ple_gate_w, ln2_g, ln2_b):
    bp, lp, _ = x_prompt.shape
    bs, ls, _ = x_sample.shape
    n_p = bp * lp
    n_s = bs * ls
    xp = x_prompt.reshape(n_p, D_MODEL)
    xs = x_sample.reshape(n_s, D_MODEL)

    w = w_in[0]
    o_z, o_xbc, o_dt = 0, SSM_INNER, SSM_INNER + SSM_CONV_CH
    o_qkv = o_dt + SSM_HEADS
    o_a = o_qkv + DN_CONV_CH
    o_b = o_a + DN_HEADS
    o_zb = o_b + DN_HEADS
    o_ga = o_zb + DN_V
    w_pa = w[:, o_z:o_dt].astype(BF16)
    w_pb = jnp.concatenate([w[:, o_qkv:o_a], w[:, o_zb:o_ga]], axis=1).astype(BF16)
    w_pg = w[:, o_ga:].astype(BF16)
    w_ps = jnp.zeros((D_MODEL, LANES), F32)
    w_ps = w_ps.at[:, SMALL_DT:SMALL_DT + SSM_HEADS].set(w[:, o_dt:o_qkv])
    w_ps = w_ps.at[:, SMALL_A:SMALL_A + 2 * DN_HEADS].set(w[:, o_a:o_zb]).astype(BF16)

    eg, eb = _row(emb_ln_g), _row(emb_ln_b)
    proj_a, small = _ln_matmul(xp, xs, eg, eb, w_pa, 1024, 2304, w_ps)
    proj_b = _ln_matmul(xp, xs, eg, eb, w_pb, 1024, 2048)
    proj_g = _ln_matmul(xp, xs, eg, eb, w_pg, 1024, 2048)

    ssd_consts = [conv_a_w[0], _row(conv_a_b[0]), _lanes_at(ssm_dt_bias[0], SMALL_DT),
                  _lanes_at(ssm_a_log[0], SMALL_DT), _row(jnp.repeat(ssm_d[0], SSM_HEAD_DIM)),
                  _row(ssm_norm_w[0])]
    gdn_consts = [conv_b_w[0], _lanes_at(dn_a_log[0], SMALL_A), _lanes_at(dn_dt_bias[0], SMALL_A),
                  _row(dn_norm_w[0])]
    zeros = lambda *s: jnp.zeros(s, F32)
    prompt = _scan_layout(bp, lp, 0)
    sample = _scan_layout(bs, ls, n_p)

    ya_p, pa_conv, pa_ssm = _ssd(proj_a, small, ssd_consts, zeros(bp, CONV_WIDTH - 1, SSM_CONV_CH),
                                 zeros(bp, SSM_HEADS, SSM_HEAD_DIM, SSM_STATE), **prompt)
    ya_s, sa_conv, sa_ssm = _ssd(proj_a, small, ssd_consts, state_ssm_conv[0], state_ssm[0], **sample)
    ob_p, pb_conv, pb_dn = _gdn(proj_b, small, gdn_consts, zeros(bp, CONV_WIDTH - 1, DN_CONV_CH),
                                zeros(bp, DN_HEADS, DN_HEAD, DN_HEAD), **prompt)
    ob_s, sb_conv, sb_dn = _gdn(proj_b, small, gdn_consts, state_dn_conv[0], state_dn[0], **sample)

    x1, x1b = _merge(
        proj_g, xp, xs, ya_p, ya_s, ob_p, ob_s,
        [eg, eb, w_a[0].astype(BF16), w_b[0].astype(BF16), w_o[0].astype(BF16), _row(ln1_g[0]), _row(ln1_b[0])],
        512)

    router_w_t = jnp.zeros((LANES, D_MODEL), F32).at[:N_EXPERTS].set(router_w[0].T)
    router_w_hi = router_w_t.astype(BF16)
    router_w_lo = (router_w_t - router_w_hi.astype(F32)).astype(BF16)
    router_b = jnp.broadcast_to(_lanes_at(router_bias[0], 0).reshape(LANES, 1), (LANES, MOE_BLOCK))
    dense, gates, rank, chosen_t, rank_t, counts = _dense_router(
        x1, x1b, p_prompt[0].reshape(n_p, PLE_DIM), p_sample[0].reshape(n_s, PLE_DIM),
        [sh_w1[0].astype(BF16), sh_w3[0].astype(BF16), sh_w2[0].astype(BF16), ple_w[0].astype(BF16),
         ple_gate_w[0].astype(BF16), router_w_hi, router_w_lo, router_b])

    tables = _moe_tables(counts[:, 0, :N_EXPERTS].astype(jnp.int32), MOE_TM)
    sorted_rows = _moe_gather(x1b, chosen_t, rank_t, tables, MOE_TM)
    expert_out = _moe_ffn(sorted_rows, exp_w1[0], exp_w3[0], exp_w2[0], tables, MOE_TM)
    out_p, out_s = _moe_combine(expert_out, dense, gates, rank, tables, _row(ln2_g[0]), _row(ln2_b[0]), n_p)

    return (out_p.reshape(bp, lp, D_MODEL), out_s.reshape(bs, ls, D_MODEL),
            pa_conv[None], pa_ssm[None], pb_conv[None], pb_dn[None],
            sa_conv[None], sa_ssm[None], sb_conv[None], sb_dn[None])
```

```python
import functools
import math

import jax
import jax.numpy as jnp
from jax import lax
from jax.experimental import pallas as pl
from jax.experimental.pallas import tpu as pltpu

F32 = jnp.float32
BF16 = jnp.bfloat16
HIGHEST = lax.Precision.HIGHEST

D_MODEL = 1024
SSM_INNER = 2048
SSM_HEAD_DIM = 64
SSM_HEADS = 32
SSM_GROUPS = 2
SSM_HEADS_PER_GROUP = 16
SSM_STATE = 128
SSM_CONV_CH = 2560
DN_HEADS = 8
DN_HEAD = 128
DN_QK = 1024
DN_V = 1024
DN_CONV_CH = 3072
CONV_WIDTH = 4
CHUNK = 64
N_EXPERTS = 64
TOP_K = 8
N_EXPERT_GROUPS = 8
EXPERTS_PER_GROUP = 8
TOPK_GROUPS = 4
EXPERT_FF = 256
SHARED_FF = 256
ROUTED_SCALE = 2.5
PLE_DIM = 256
LN_EPS = 1e-5
RMS_EPS = 1e-6
L2_EPS = 1e-6
DEEPNORM_ALPHA = 2.0 ** 0.25

LANES = 128
SUBLANES = 8
VMEM_LIMIT = 56 * 1024 * 1024
SMALL_DT = 0
SMALL_A = 32
SMALL_B = 40
CONV_PAD = 8
SCAN_CHUNKS = 4
SCAN_SEQS = 4
MOE_BLOCK = 256
MOE_SEG = 16
MOE_SUB = 512
MOE_CAP = -(-(MOE_BLOCK * 8 + 64 * (MOE_SEG - 1)) // MOE_SUB) * MOE_SUB
MOE_TM = 512


def _sigmoid(x):
    return 1.0 / (1.0 + jnp.exp(-x))


def _silu(x):
    return x * _sigmoid(x)


def _softplus(x):
    return jnp.maximum(x, 0.0) + jnp.log(1.0 + jnp.exp(-jnp.abs(x)))


def _layer_norm(x, g, b):
    mu = jnp.mean(x, axis=-1, keepdims=True)
    xc = x - mu
    var = jnp.mean(xc * xc, axis=-1, keepdims=True)
    return xc * lax.rsqrt(var + LN_EPS) * g + b


def _dot(a, b):
    return jnp.dot(a, b, preferred_element_type=F32)


def _dot_nt(a, b):
    return lax.dot_general(a, b, (((1,), (1,)), ((), ())), preferred_element_type=F32)


def _dot_tn(a, b):
    return lax.dot_general(a, b, (((0,), (0,)), ((), ())), preferred_element_type=F32)


def _dot_f32(a, b):
    return jnp.dot(a, b, precision=HIGHEST, preferred_element_type=F32)


def _params(*sem):
    return pltpu.CompilerParams(dimension_semantics=sem, vmem_limit_bytes=VMEM_LIMIT)


def _when(cond):
    if cond is True:
        return lambda fn: fn()
    if cond is False:
        return lambda fn: None
    return pl.when(cond)


def _row_tile(m, preferred):
    return max(d for d in range(SUBLANES, min(m, preferred) + 1, SUBLANES) if m % d == 0)


def _group_maps(prompt_tiles):
    in_prompt = lambda i, *_: (jnp.minimum(i, prompt_tiles - 1), 0)
    in_sample = lambda i, *_: (jnp.maximum(i - prompt_tiles, 0), 0)
    return in_prompt, in_sample


def _ln_matmul_kernel(xp_ref, xs_ref, g_ref, b_ref, w_ref, *rest, prompt_tiles, narrow):
    if narrow:
        wn_ref, o_ref, on_ref, h_scr = rest
    else:
        o_ref, h_scr = rest

    @pl.when(pl.program_id(1) == 0)
    def _():
        x = jnp.where(pl.program_id(0) < prompt_tiles, xp_ref[...], xs_ref[...])
        h_scr[...] = _layer_norm(x, g_ref[...], b_ref[...]).astype(BF16)
        if narrow:
            on_ref[...] = _dot(h_scr[...], wn_ref[...])

    o_ref[...] = _dot(h_scr[...], w_ref[...]).astype(o_ref.dtype)


def _ln_matmul(xp, xs, g, b, w, tm, tn, w_narrow=None):
    k = xp.shape[1]
    m = xp.shape[0] + xs.shape[0]
    n = w.shape[1]
    tm = _row_tile(math.gcd(xp.shape[0], xs.shape[0]), tm)
    prompt_tiles = xp.shape[0] // tm
    in_prompt, in_sample = _group_maps(prompt_tiles)
    narrow = w_narrow is not None
    const = lambda i, j: (0, 0)
    row = lambda i, j: (i, 0)
    in_specs = [pl.BlockSpec((tm, k), in_prompt), pl.BlockSpec((tm, k), in_sample),
                pl.BlockSpec((1, k), const), pl.BlockSpec((1, k), const), pl.BlockSpec((k, tn), lambda i, j: (0, j))]
    out_specs = [pl.BlockSpec((tm, tn), lambda i, j: (i, j))]
    out_shape = [jax.ShapeDtypeStruct((m, n), BF16)]
    args = [xp, xs, g, b, w]
    if narrow:
        in_specs.append(pl.BlockSpec((k, LANES), const))
        out_specs.append(pl.BlockSpec((tm, LANES), row))
        out_shape.append(jax.ShapeDtypeStruct((m, LANES), F32))
        args.append(w_narrow)
    out = pl.pallas_call(
        functools.partial(_ln_matmul_kernel, prompt_tiles=prompt_tiles, narrow=narrow),
        grid=(m // tm, n // tn),
        in_specs=in_specs,
        out_specs=out_specs,
        out_shape=out_shape,
        scratch_shapes=[pltpu.VMEM((tm, k), BF16)],
        compiler_params=_params("parallel", "arbitrary"),
        name="ln_matmul",
    )(*args)
    return out if narrow else out[0]


def _load_conv_state(cbuf, state):
    taps = CONV_WIDTH - 1
    cbuf[0:SUBLANES - taps, :] = jnp.zeros((SUBLANES - taps, cbuf.shape[1]), F32)
    cbuf[SUBLANES - taps:SUBLANES, :] = state


def _causal_conv(cbuf, x16, cw, t):
    ch = x16.shape[1]
    prev = cbuf[...]
    taps = CONV_WIDTH - 1
    if t <= SUBLANES:
        x = x16.astype(F32)
        ext = jnp.concatenate([prev, x], axis=0)
        lo = SUBLANES - taps
        y = ext[lo:lo + t] * cw[0:1]
        y = y + ext[lo + 1:lo + 1 + t] * cw[1:2]
        y = y + ext[lo + 2:lo + 2 + t] * cw[2:3]
        y = y + x * cw[3:4]
        cbuf[...] = ext[t:t + SUBLANES]
        return y, ext[SUBLANES + t - taps:SUBLANES + t]
    hi = prev.astype(BF16).astype(F32)
    mid = (prev - hi).astype(BF16).astype(F32)
    low = ((prev - hi) - mid).astype(BF16).astype(F32)
    x = x16.astype(F32)
    pack = 2 * SUBLANES
    nhead = 4 * SUBLANES if t % pack == 0 else 5 * SUBLANES
    head = jnp.concatenate([hi, mid, low, jnp.zeros((nhead - 3 * SUBLANES, ch), F32)], axis=0)
    if t % pack == 0:
        ext = jnp.concatenate([head.astype(BF16), x16], axis=0)
    else:
        ext = jnp.concatenate([head, x], axis=0).astype(BF16)
    out_row = lax.broadcasted_iota(jnp.int32, (taps * t, nhead + t), 0)
    col = lax.broadcasted_iota(jnp.int32, (taps * t, nhead + t), 1)
    log_t = t.bit_length() - 1
    shift = lax.shift_right_logical(out_row, log_t) + 1
    src = (out_row & (t - 1)) - shift
    in_x = (src >= 0) & (col == nhead + src)
    in_prev = (src < 0) & (col < 3 * SUBLANES) & ((col & (SUBLANES - 1)) == SUBLANES + src)
    shifted = _dot((in_x | in_prev).astype(BF16), ext)
    y = shifted[2 * t:3 * t] * cw[0:1]
    y = y + shifted[t:2 * t] * cw[1:2]
    y = y + shifted[0:t] * cw[2:3]
    y = y + x * cw[3:4]
    cbuf[...] = x[t - SUBLANES:t]
    return y, x[t - taps:t]


def _lane_expand(v, h0, count, width):
    t = v.shape[0]
    n = count * width
    out = jnp.broadcast_to(v[:, h0:h0 + 1], (t, n))
    if count > 1:
        lane = lax.broadcasted_iota(jnp.int32, (t, n), 1)
        for i in range(1, count):
            out = jnp.where(lane >= i * width, jnp.broadcast_to(v[:, h0 + i:h0 + i + 1], (t, n)), out)
    return out


def _grouped_transpose(v, per_tile):
    t = v.shape[0]
    blocks = [v if r == 0 else pltpu.roll(v, LANES - r, axis=1) for r in range(per_tile)]
    if per_tile * t < LANES:
        blocks.append(jnp.zeros((LANES - per_tile * t, LANES), F32))
    return jnp.concatenate(blocks, axis=0).T


def _same_block(i, j, size):
    shift = size.bit_length() - 1
    return lax.shift_right_logical(i, shift) == lax.shift_right_logical(j, shift)


def _unit_lower_inverses(lmats, ii, jj, t):
    mm = lambda a, b: _dot(a.astype(BF16), b.astype(BF16))
    base = min(16, t)
    in_base = _same_block(ii, jj, base)
    eye = (ii == jj).astype(F32)
    power = [jnp.where(in_base, l, 0.0) for l in lmats]
    inv = [eye - p for p in power]
    span = 2
    while span < base:
        power = [mm(p, p) for p in power]
        inv = [a + mm(a, p) for a, p in zip(inv, power)]
        span *= 2
    size = base
    while size < t:
        link = _same_block(ii, jj, 2 * size) & jnp.logical_not(_same_block(ii, jj, size))
        cross = [mm(a, jnp.where(link, l, 0.0)) for a, l in zip(inv, lmats)]
        inv = [a - mm(c, a) for a, c in zip(inv, cross)]
        size *= 2
    return inv


def _row_sums_of_squares(blocks):
    t = blocks[0].shape[0]
    sq = jnp.concatenate([b * b for b in blocks], axis=0)
    hi = sq.astype(BF16)
    lo = (sq - hi.astype(F32)).astype(BF16)
    ones = jnp.ones((LANES, LANES), BF16)
    sums = _dot(hi, ones) + _dot(lo, ones)
    return [sums[i * t:(i + 1) * t] for i in range(len(blocks))]


def _cumsum_rows(v, t):
    ii = lax.broadcasted_iota(jnp.int32, (t, t), 0)
    jj = lax.broadcasted_iota(jnp.int32, (t, t), 1)
    return _dot_f32((jj <= ii).astype(F32), v)


def _ssd_kernel(pa_ref, sm_ref, cw_ref, cb_ref, dtb_ref, alog_ref, dskip_ref, nw_ref, cst_ref, h0_ref,
                y_ref, cnew_ref, hnew_ref, cbuf, s_scr, *, t, first, last):
    per_tile = LANES // t
    width = per_tile * SSM_HEAD_DIM

    @_when(first)
    def _():
        _load_conv_state(cbuf, cst_ref[0])
        s_scr[...] = h0_ref[0]

    z = pa_ref[:, :SSM_INNER].astype(F32)
    conv, tail = _causal_conv(cbuf, pa_ref[:, SSM_INNER:], cw_ref[...], t)

    @_when(last)
    def _():
        cnew_ref[0] = tail

    xbc = _silu(conv + cb_ref[...])
    xs = xbc[:, :SSM_INNER]
    bm = xbc[:, SSM_INNER:SSM_INNER + SSM_GROUPS * SSM_STATE]
    cm = xbc[:, SSM_INNER + SSM_GROUPS * SSM_STATE:]

    dt = _softplus(sm_ref[...] + dtb_ref[...])
    da = dt * (-jnp.exp(alog_ref[...]))
    cum = _cumsum_rows(da, t)
    cum_t = _grouped_transpose(cum, per_tile)
    dt_t = _grouped_transpose(dt, per_tile)

    row = lax.broadcasted_iota(jnp.int32, (t, LANES), 0)
    lane = lax.broadcasted_iota(jnp.int32, (t, LANES), 1)
    causal = (lane % t) <= row
    brow = lax.broadcasted_iota(jnp.int32, (LANES, width), 0)
    bcol = lax.broadcasted_iota(jnp.int32, (LANES, width), 1)
    blockdiag = (brow // t) == (bcol // SSM_HEAD_DIM)

    ys = []
    for g in range(SSM_GROUPS):
        bg = bm[:, g * SSM_STATE:(g + 1) * SSM_STATE]
        cg = cm[:, g * SSM_STATE:(g + 1) * SSM_STATE].astype(BF16)
        cb = _dot_nt(cg, jnp.concatenate([bg] * per_tile, axis=0).astype(BF16))
        bg = bg.astype(BF16)
        for q in range(SSM_HEADS_PER_GROUP // per_tile):
            h0 = g * SSM_HEADS_PER_GROUP + q * per_tile
            ccol = _lane_expand(cum, h0, per_tile, t)
            seg = ccol - cum_t[h0:h0 + 1, :]
            decay = jnp.exp(jnp.where(causal, seg, -jnp.inf))
            wts = (cb * decay * dt_t[h0:h0 + 1, :]).astype(BF16)
            xt = xs[:, h0 * SSM_HEAD_DIM:h0 * SSM_HEAD_DIM + width]
            xbd = jnp.where(blockdiag, jnp.concatenate([xt] * per_tile, axis=0), 0.0).astype(BF16)
            y_intra = _dot(wts, xbd)
            st = s_scr[h0:h0 + per_tile].reshape(width, SSM_STATE)
            cum_w = ccol if width == LANES else _lane_expand(cum, h0, per_tile, SSM_HEAD_DIM)
            y_state = _dot_nt(cg, st.astype(BF16)) * jnp.exp(cum_w)
            ys.append(y_intra + y_state)
            wend = jnp.exp(cum_w[t - 1:t, :] - cum_w) * _lane_expand(dt, h0, per_tile, SSM_HEAD_DIM)
            ds = _dot_tn((xt * wend).astype(BF16), bg)
            for i in range(per_tile):
                h = h0 + i
                s_scr[h] = jnp.exp(cum[t - 1:t, h:h + 1]) * s_scr[h] + ds[i * SSM_HEAD_DIM:(i + 1) * SSM_HEAD_DIM]

    y = jnp.concatenate(ys, axis=1) + dskip_ref[...] * xs
    y = y * _silu(z)
    half = SSM_INNER // SSM_GROUPS
    normed = []
    for g in range(SSM_GROUPS):
        yg = y[:, g * half:(g + 1) * half]
        normed.append(yg * lax.rsqrt(jnp.mean(yg * yg, axis=-1, keepdims=True) + RMS_EPS))
    y_ref[...] = (jnp.concatenate(normed, axis=1) * nw_ref[...]).astype(BF16)

    @_when(last)
    def _():
        hnew_ref[0] = s_scr[...]


def _scan_step_kernel(chunk_kernel, n_tok, n_const, n_state, seqs, chunks, t):
    def step(*refs):
        tok = refs[:n_tok]
        const = refs[n_tok:n_tok + n_const]
        st_in = refs[n_tok + n_const:n_tok + n_const + n_state]
        y_ref = refs[n_tok + n_const + n_state]
        st_out = refs[n_tok + n_const + n_state + 1:n_tok + n_const + 2 * n_state + 1]
        scratch = refs[n_tok + n_const + 2 * n_state + 1:]
        first_step = pl.program_id(1) == 0
        last_step = pl.program_id(1) == pl.num_programs(1) - 1
        for s in range(seqs):
            for c in range(chunks):
                rows = pl.ds((s * chunks + c) * t, t)
                chunk_kernel(*[r.at[rows, :] for r in tok], *const, *[r.at[pl.ds(s, 1)] for r in st_in],
                             y_ref.at[rows, :], *[r.at[pl.ds(s, 1)] for r in st_out], *scratch,
                             first=first_step if c == 0 else False,
                             last=last_step if c == chunks - 1 else False)
    return step


def _scan_call(kernel, name, tok_in, const_in, state_in, y_width, scratch, *, batch, seq, row0, t, seqs, chunks,
               step_kernel=None):
    nstep = seq // (t * chunks)
    rows = seqs * chunks * t
    blk0 = row0 // rows

    def per_batch(a):
        zeros = (0,) * (a.ndim - 1)
        return pl.BlockSpec((seqs,) + a.shape[1:], lambda b, c: (b,) + zeros)

    if step_kernel is None:
        step_kernel = _scan_step_kernel(kernel, len(tok_in), len(const_in), len(state_in), seqs, chunks, t)
    return pl.pallas_call(
        step_kernel,
        grid=(batch // seqs, nstep),
        in_specs=[pl.BlockSpec((rows, a.shape[1]), lambda b, c: (blk0 + b * nstep + c, 0)) for a in tok_in]
        + [pl.BlockSpec(a.shape, lambda b, c: (0, 0)) for a in const_in]
        + [per_batch(a) for a in state_in],
        out_specs=[pl.BlockSpec((rows, y_width), lambda b, c: (b * nstep + c, 0))]
        + [per_batch(a) for a in state_in],
        out_shape=[jax.ShapeDtypeStruct((batch * seq, y_width), BF16)]
        + [jax.ShapeDtypeStruct(a.shape, F32) for a in state_in],
        scratch_shapes=scratch,
        compiler_params=_params("parallel", "arbitrary"),
        name=name,
    )(*tok_in, *const_in, *state_in)


def _ssd(proj_a, small, consts, conv_state, h0, **where):
    t = where["t"]
    scratch = [pltpu.VMEM((SUBLANES, SSM_CONV_CH), F32),
               pltpu.VMEM((SSM_HEADS, SSM_HEAD_DIM, SSM_STATE), F32)]
    return _scan_call(functools.partial(_ssd_kernel, t=t), f"ssd_t{t}", [proj_a, small], consts,
                      [conv_state, h0], SSM_INNER, scratch, **where)


def _gdn_step_kernel(pb_ref, sm_ref, cw_ref, alog_ref, dtb_ref, nw_ref, cst_ref, s0_ref,
                     o_ref, cnew_ref, snew_ref, cbuf, s_scr, *, t, seqs, chunks):
    first_step = pl.program_id(1) == 0
    last_step = pl.program_id(1) == pl.num_programs(1) - 1
    units = [(s, c) for s in range(seqs) for c in range(chunks)]
    rows = lambda u: pl.ds(u * t, t)
    per_tile = min(LANES // t, DN_HEADS)
    n = per_tile * t
    ntile = DN_HEADS // per_tile
    heads = range(DN_HEADS)
    ii = lax.broadcasted_iota(jnp.int32, (n, n), 0)
    jj = lax.broadcasted_iota(jnp.int32, (n, n), 1)
    causal = _same_block(ii, jj, t) & (jj <= ii)
    diag = ii == jj
    stack = lambda xs, p: jnp.concatenate(xs[p * per_tile:(p + 1) * per_tile], axis=0)

    qkv, zb, gcum, beta = [], [], [], []
    for u, (s, c) in enumerate(units):
        @_when(first_step if c == 0 else False)
        def _():
            _load_conv_state(cbuf, cst_ref[s])

        conv, tail = _causal_conv(cbuf, pb_ref[rows(u), :DN_CONV_CH], cw_ref[...], t)

        @_when(last_step if c == chunks - 1 else False)
        def _():
            cnew_ref[s] = tail

        qkv.append(_silu(conv))
        zb.append(pb_ref[rows(u), DN_CONV_CH:].astype(F32))
        sm = sm_ref[rows(u), :]
        beta.append(_sigmoid(sm))
        gate = -jnp.exp(alog_ref[...]) * _softplus(sm + dtb_ref[...])
        gcum.append(_cumsum_rows(gate, t))
    gcum_t = [_grouped_transpose(g, per_tile) for g in gcum]

    sumsq = _row_sums_of_squares([x[:, i * DN_HEAD:(i + 1) * DN_HEAD] for x in qkv for i in range(2 * DN_HEADS)])
    q, k, v, gcol, bcol, kb, egcol = ([[] for _ in units] for _ in range(7))
    for u in range(len(units)):
        for h in heads:
            lo = h * DN_HEAD
            ss = sumsq[u * 2 * DN_HEADS:(u + 1) * 2 * DN_HEADS]
            q[u].append(qkv[u][:, lo:lo + DN_HEAD] * lax.rsqrt(ss[h] + L2_EPS) * (DN_HEAD ** -0.5))
            k[u].append(qkv[u][:, DN_QK + lo:DN_QK + lo + DN_HEAD] * lax.rsqrt(ss[DN_HEADS + h] + L2_EPS))
            v[u].append(qkv[u][:, 2 * DN_QK + lo:2 * DN_QK + lo + DN_HEAD])
            gcol[u].append(gcum[u][:, SMALL_A + h:SMALL_A + h + 1])
            bcol[u].append(beta[u][:, SMALL_B + h:SMALL_B + h + 1])
            kb[u].append(k[u][h] * bcol[u][h])
            egcol[u].append(jnp.exp(gcol[u][h]))

    tiles = [(u, p) for u in range(len(units)) for p in range(ntile)]
    k16 = [stack(k[u], p).astype(BF16) for u, p in tiles]
    decay = []
    for u, p in tiles:
        h0 = p * per_tile
        seg = stack(gcol[u], p) - gcum_t[u][SMALL_A + h0:SMALL_A + h0 + 1, :n]
        decay.append(jnp.exp(jnp.where(causal, seg, -jnp.inf)))
    lmat = [_dot_nt(stack(kb[u], p).astype(BF16), k16[i]) * jnp.where(diag, 0.0, decay[i])
            for i, (u, p) in enumerate(tiles)]
    qk = [(_dot_nt(stack(q[u], p).astype(BF16), k16[i]) * decay[i]).astype(BF16) for i, (u, p) in enumerate(tiles)]
    inv = [a.astype(BF16) for a in _unit_lower_inverses(lmat, ii, jj, t)]

    outs = []
    for u, (s, c) in enumerate(units):
        @_when(first_step if c == 0 else False)
        def _():
            s_scr[...] = s0_ref[s]

        s16 = [s_scr[h].astype(BF16) for h in heads]
        from_state = [_dot(jnp.concatenate([kb[u][h] * egcol[u][h], q[u][h] * egcol[u][h]], axis=0).astype(BF16),
                           s16[h]) for h in heads]
        rhs = [v[u][h] * bcol[u][h] - from_state[h][:t] for h in heads]
        upd = [_dot(inv[u * ntile + p], stack(rhs, p).astype(BF16)) for p in range(ntile)]
        o_intra = [_dot(qk[u * ntile + p], upd[p].astype(BF16)) for p in range(ntile)]
        for h in heads:
            p, i = divmod(h, per_tile)
            uh = upd[p][i * t:(i + 1) * t].astype(BF16)
            glast = gcum[u][t - 1:t, SMALL_A + h:SMALL_A + h + 1]
            s_scr[h] = jnp.exp(glast) * s_scr[h] + _dot_tn((k[u][h] * jnp.exp(glast - gcol[u][h])).astype(BF16), uh)
            outs.append(from_state[h][t:] + o_intra[p][i * t:(i + 1) * t])

        @_when(last_step if c == chunks - 1 else False)
        def _():
            snew_ref[s] = s_scr[...]

    norm_w = nw_ref[...]
    osq = _row_sums_of_squares(outs)
    for u in range(len(units)):
        normed = []
        for h in heads:
            i = u * DN_HEADS + h
            normed.append(outs[i] * lax.rsqrt(osq[i] * (1.0 / DN_HEAD) + RMS_EPS) * norm_w
                          * _silu(zb[u][:, h * DN_HEAD:(h + 1) * DN_HEAD]))
        o_ref[rows(u), :] = jnp.concatenate(normed, axis=1).astype(BF16)


def _gdn(proj_b, small, consts, conv_state, s0, **where):
    t = where["t"]
    scratch = [pltpu.VMEM((SUBLANES, DN_CONV_CH), F32),
               pltpu.VMEM((DN_HEADS, DN_HEAD, DN_HEAD), F32)]
    step = functools.partial(_gdn_step_kernel, t=t, seqs=where["seqs"], chunks=where["chunks"])
    return _scan_call(None, f"gdn_t{t}", [proj_b, small], consts, [conv_state, s0], DN_V, scratch,
                      step_kernel=step, **where)


def _merge_kernel(gate_ref, xp_ref, xs_ref, yap_ref, yas_ref, obp_ref, obs_ref, eg_ref, eb_ref, wa_ref, wb_ref,
                  wo_ref, g_ref, b_ref, x1_ref, x1b_ref, *, prompt_tiles):
    is_prompt = pl.program_id(0) < prompt_tiles
    h = _layer_norm(jnp.where(is_prompt, xp_ref[...], xs_ref[...]), eg_ref[...], eb_ref[...])
    out_a = _dot(jnp.where(is_prompt, yap_ref[...], yas_ref[...]), wa_ref[...])
    out_b = _dot(jnp.where(is_prompt, obp_ref[...], obs_ref[...]), wb_ref[...])
    gt = gate_ref[...].astype(F32)
    merged = _sigmoid(gt[:, :D_MODEL]) * out_a + _sigmoid(gt[:, D_MODEL:]) * out_b
    mix = _dot(merged.astype(BF16), wo_ref[...])
    x1 = _layer_norm(DEEPNORM_ALPHA * h + mix, g_ref[...], b_ref[...])
    x1_ref[...] = x1
    x1b_ref[...] = x1.astype(BF16)


def _merge(proj_g, xp, xs, ya_p, ya_s, ob_p, ob_s, consts, tm):
    m = proj_g.shape[0]
    n_p = xp.shape[0]
    tm = _row_tile(math.gcd(n_p, m - n_p), tm)
    in_prompt, in_sample = _group_maps(n_p // tm)
    row = lambda i: (i, 0)
    const = lambda i: (0, 0)
    return pl.pallas_call(
        functools.partial(_merge_kernel, prompt_tiles=n_p // tm),
        grid=(m // tm,),
        in_specs=[pl.BlockSpec((tm, 2 * D_MODEL), row),
                  pl.BlockSpec((tm, D_MODEL), in_prompt), pl.BlockSpec((tm, D_MODEL), in_sample),
                  pl.BlockSpec((tm, SSM_INNER), in_prompt), pl.BlockSpec((tm, SSM_INNER), in_sample),
                  pl.BlockSpec((tm, DN_V), in_prompt), pl.BlockSpec((tm, DN_V), in_sample)]
        + [pl.BlockSpec(a.shape, const) for a in consts],
        out_specs=[pl.BlockSpec((tm, D_MODEL), row), pl.BlockSpec((tm, D_MODEL), row)],
        out_shape=[jax.ShapeDtypeStruct((m, D_MODEL), F32), jax.ShapeDtypeStruct((m, D_MODEL), BF16)],
        compiler_params=_params("parallel"),
        name="merge",
    )(proj_g, xp, xs, ya_p, ya_s, ob_p, ob_s, *consts)


def _take_first_max(rest, index, limit, axis):
    best = jnp.max(rest, axis=axis, keepdims=True)
    first = jnp.min(jnp.where(rest == best, index, limit), axis=axis, keepdims=True)
    hit = index == first
    return hit, jnp.where(hit, -jnp.inf, rest)


def _router_gates_t(x1, xb, rw_hi, rw_lo, bias):
    tm = x1.shape[0]
    x_lo = (x1 - xb.astype(F32)).astype(BF16)
    logits = _dot_nt(rw_hi, xb) + (_dot_nt(rw_hi, x_lo) + _dot_nt(rw_lo, xb))
    scores = _sigmoid(logits)
    row = lax.broadcasted_iota(jnp.int32, (LANES, tm), 0)
    sel = jnp.where(row < N_EXPERTS, scores + bias, -jnp.inf)
    by_group = sel.reshape(LANES // EXPERTS_PER_GROUP, EXPERTS_PER_GROUP, tm)
    sub = lax.broadcasted_iota(jnp.int32, by_group.shape, 1)
    top1 = jnp.max(by_group, axis=1, keepdims=True)
    _, others = _take_first_max(by_group, sub, EXPERTS_PER_GROUP, 1)
    gscore = (top1 + jnp.max(others, axis=1, keepdims=True))[:N_EXPERT_GROUPS]
    gidx = lax.broadcasted_iota(jnp.int32, gscore.shape, 0)
    gkeep = jnp.zeros(gscore.shape, jnp.bool_)
    for _ in range(TOPK_GROUPS):
        hit, gscore = _take_first_max(gscore, gidx, N_EXPERT_GROUPS, 0)
        gkeep = gkeep | hit
    gkeep = jnp.broadcast_to(gkeep, (N_EXPERT_GROUPS, EXPERTS_PER_GROUP, tm)).reshape(N_EXPERTS, tm)
    rest = jnp.where(gkeep, sel[:N_EXPERTS], -jnp.inf)
    eidx = lax.broadcasted_iota(jnp.int32, (N_EXPERTS, tm), 0)
    keep = jnp.zeros((N_EXPERTS, tm), jnp.bool_)
    for _ in range(TOP_K):
        hit, rest = _take_first_max(rest, eidx, N_EXPERTS, 0)
        keep = keep | hit
    picked = jnp.where(keep, scores[:N_EXPERTS], 0.0)
    gates = picked / jnp.sum(picked, axis=0, keepdims=True) * ROUTED_SCALE
    return jnp.concatenate([gates, jnp.zeros((LANES - N_EXPERTS, tm), F32)], axis=0)


def _dense_router_kernel(x1_ref, x1b_ref, pp_ref, ps_ref, sw1_ref, sw3_ref, sw2_ref, pw_ref, pg_ref, rwh_ref,
                         rwl_ref, rb_ref, dense_ref, gates_ref, rank_ref, chosen_t_ref, rank_t_ref, count_ref,
                         *, prompt_tiles):
    xb = x1b_ref[...]
    hid = _silu(_dot(xb, sw1_ref[...])) * _dot(xb, sw3_ref[...])
    shared = _dot(hid.astype(BF16), sw2_ref[...])
    p = jnp.where(pl.program_id(0) < prompt_tiles, pp_ref[...], ps_ref[...])
    ple = _dot(p.astype(BF16), pw_ref[...]) * _sigmoid(_dot(xb, pg_ref[...]))
    x1 = x1_ref[...]
    dense_ref[...] = DEEPNORM_ALPHA * x1 + shared + ple
    gates_t = _router_gates_t(x1, xb, rwh_ref[...], rwl_ref[...], rb_ref[...])
    tm = x1.shape[0]
    chosen_t = (gates_t > 0.0).astype(BF16)
    ii = lax.broadcasted_iota(jnp.int32, (tm, tm), 0)
    jj = lax.broadcasted_iota(jnp.int32, (tm, tm), 1)
    rank_t = _dot(chosen_t, (ii < jj).astype(BF16))
    chosen_t_ref[...] = chosen_t
    rank_t_ref[...] = rank_t.astype(BF16)
    gates = gates_t.T
    rank = rank_t.T
    gates_ref[...] = gates.astype(BF16)
    rank_ref[...] = rank.astype(BF16)
    count = rank[tm - 1:tm, :] + (gates[tm - 1:tm, :] > 0.0).astype(F32)
    count_ref[0] = jnp.broadcast_to(count, (SUBLANES, LANES))


def _dense_router(x1, x1b, p_prompt, p_sample, consts):
    m = x1.shape[0]
    nblk = m // MOE_BLOCK
    prompt_tiles = p_prompt.shape[0] // MOE_BLOCK
    in_prompt, in_sample = _group_maps(prompt_tiles)
    row = lambda i: (i, 0)
    const = lambda i: (0, 0)
    return pl.pallas_call(
        functools.partial(_dense_router_kernel, prompt_tiles=prompt_tiles),
        grid=(nblk,),
        in_specs=[pl.BlockSpec((MOE_BLOCK, D_MODEL), row), pl.BlockSpec((MOE_BLOCK, D_MODEL), row),
                  pl.BlockSpec((MOE_BLOCK, PLE_DIM), in_prompt), pl.BlockSpec((MOE_BLOCK, PLE_DIM), in_sample)]
        + [pl.BlockSpec(a.shape, const) for a in consts],
        out_specs=[pl.BlockSpec((MOE_BLOCK, D_MODEL), row), pl.BlockSpec((MOE_BLOCK, LANES), row),
                   pl.BlockSpec((MOE_BLOCK, LANES), row), pl.BlockSpec((LANES, MOE_BLOCK), row),
                   pl.BlockSpec((LANES, MOE_BLOCK), row), pl.BlockSpec((1, SUBLANES, LANES), lambda i: (i, 0, 0))],
        out_shape=[jax.ShapeDtypeStruct((m, D_MODEL), F32), jax.ShapeDtypeStruct((m, LANES), BF16),
                   jax.ShapeDtypeStruct((m, LANES), BF16), jax.ShapeDtypeStruct((nblk * LANES, MOE_BLOCK), BF16),
                   jax.ShapeDtypeStruct((nblk * LANES, MOE_BLOCK), BF16),
                   jax.ShapeDtypeStruct((nblk, SUBLANES, LANES), F32)],
        compiler_params=_params("parallel"),
        name="dense_router",
    )(x1, x1b, p_prompt, p_sample, *consts)


def _moe_tables(counts, tm):
    nblk = counts.shape[0]
    nchunk_max = MOE_CAP // MOE_SEG
    padded = (counts + MOE_SEG - 1) // MOE_SEG * MOE_SEG
    start = jnp.cumsum(padded, axis=1) - padded
    used = jnp.sum(padded, axis=1)
    rows_e = jnp.sum(padded, axis=0)
    rows_e_t = (rows_e + tm - 1) // tm * tm
    base_e = jnp.cumsum(rows_e_t) - rows_e_t
    seg_row = base_e[None, :] + jnp.cumsum(padded, axis=0) - padded
    chunk0 = jnp.arange(nchunk_max, dtype=jnp.int32) * MOE_SEG
    owner = jnp.sum((start + padded)[:, None, :] <= chunk0[None, :, None], axis=-1)
    onehot = owner[..., None] == jnp.arange(N_EXPERTS, dtype=jnp.int32)
    dst = chunk0[None, :] + jnp.sum(jnp.where(onehot, (seg_row - start)[:, None, :], 0), axis=-1)
    n_tiles = _moe_max_tiles(nblk, tm)
    live = chunk0[None, :] < used[:, None]
    parity = (jnp.arange(nblk, dtype=jnp.int32) % 2)[:, None]
    spare = n_tiles * tm + parity * MOE_CAP + chunk0[None, :]
    src = jnp.where(live, dst, 0)
    dst = jnp.where(live, dst, spare)
    tile_row = jnp.arange(n_tiles, dtype=jnp.int32) * tm
    tiles_used = jnp.sum(rows_e_t) // tm
    tile_row = jnp.minimum(tile_row, (tiles_used - 1) * tm)
    tile_expert = jnp.minimum(jnp.sum((base_e + rows_e_t)[None, :] <= tile_row[:, None], axis=-1), N_EXPERTS - 1)
    bounds = jnp.zeros((nblk, SUBLANES, LANES), F32)
    bounds = bounds.at[:, 0, :N_EXPERTS].set(start.astype(F32)).at[:, 1, :N_EXPERTS].set(padded.astype(F32))
    i32 = lambda a: a.astype(jnp.int32)
    return dict(dst=i32(dst.reshape(-1)), src=i32(src.reshape(-1)), nchunk=i32(used // MOE_SEG),
                pad_row=i32(base_e + rows_e),
                pad_n=i32((rows_e_t - rows_e) // MOE_SEG), tile_expert=i32(tile_expert),
                tiles_used=i32(tiles_used.reshape(1)), bounds=bounds)


def _moe_max_tiles(nblk, tm):
    return (nblk * (MOE_BLOCK * TOP_K + N_EXPERTS * (MOE_SEG - 1)) + N_EXPERTS * (tm - MOE_SEG) + tm - 1) // tm


def _slot_bounds(bounds):
    return bounds[0:1, :], bounds[1:2, :]


def _moe_gather_kernel(dst_ref, nchunk_ref, pad_row_ref, pad_n_ref,
                       x_ref, chosen_t_ref, rank_t_ref, bounds_ref, rows_ref, buf, zbuf, sem, zsem):
    j = pl.program_id(0)
    nblk = pl.num_programs(0)
    slot = j % 2
    nchunk_max = MOE_CAP // MOE_SEG

    def chunk_copy(blk, q, sl):
        src = buf.at[sl, pl.ds(pl.multiple_of(q * MOE_SEG, MOE_SEG), MOE_SEG), :]
        row = pl.multiple_of(dst_ref[blk * nchunk_max + q], MOE_SEG)
        return pltpu.make_async_copy(src, rows_ref.at[pl.ds(row, MOE_SEG), :], sem.at[sl])

    def pad_copy(e, i):
        row = pl.multiple_of(pad_row_ref[e] + i * MOE_SEG, MOE_SEG)
        return pltpu.make_async_copy(zbuf, rows_ref.at[pl.ds(row, MOE_SEG), :], zsem)

    def sub_chunks(blk, sub, sl, action):
        for q in range(sub * (MOE_SUB // MOE_SEG), (sub + 1) * (MOE_SUB // MOE_SEG)):
            action(chunk_copy(blk, q, sl))

    def sub_used(blk, sub):
        return sub * MOE_SUB < nchunk_ref[blk] * MOE_SEG

    def for_pads(action):
        def per_expert(e, carry):
            def body(i, c):
                action(pad_copy(e, i))
                return c
            return lax.fori_loop(0, pad_n_ref[e], body, carry)
        lax.fori_loop(0, N_EXPERTS, per_expert, 0)

    @pl.when(j == 0)
    def _():
        zbuf[...] = jnp.zeros_like(zbuf)
        for_pads(lambda c: c.start())

    start, length = _slot_bounds(bounds_ref[0])
    chosen_t = chosen_t_ref[...]
    rank_t = rank_t_ref[...]
    x = x_ref[...]
    before = jnp.maximum(j - 2, 0)
    for sub in range(MOE_CAP // MOE_SUB):
        @pl.when((j >= 2) & sub_used(before, sub))
        def _():
            sub_chunks(before, sub, slot, lambda c: c.wait())

    for sub in range(MOE_CAP // MOE_SUB):
        @pl.when(sub_used(j, sub))
        def _():
            s = (sub * MOE_SUB + lax.broadcasted_iota(jnp.int32, (MOE_SUB, LANES), 0)).astype(F32)
            owner = (s >= start) & (s < start + length)
            within = s[:, 0:1] - jnp.sum(jnp.where(owner, start, 0.0), axis=1, keepdims=True)
            owner = owner.astype(BF16)
            pick = (_dot(owner, chosen_t) > 0.5) & (_dot(owner, rank_t) == within)
            buf[slot, sub * MOE_SUB:(sub + 1) * MOE_SUB, :] = _dot(pick.astype(BF16), x).astype(BF16)
            sub_chunks(j, sub, slot, lambda c: c.start())

    @pl.when(j == nblk - 1)
    def _():
        prev = jnp.maximum(j - 1, 0)
        for sub in range(MOE_CAP // MOE_SUB):
            @pl.when(sub_used(j, sub))
            def _():
                sub_chunks(j, sub, slot, lambda c: c.wait())

            @pl.when((j >= 1) & sub_used(prev, sub))
            def _():
                sub_chunks(prev, sub, 1 - slot, lambda c: c.wait())

        for_pads(lambda c: c.wait())


def _moe_gather(x1b, chosen_t, rank_t, tables, tm):
    m = x1b.shape[0]
    nblk = m // MOE_BLOCK
    rows = _moe_max_tiles(nblk, tm) * tm + 2 * MOE_CAP
    blk = lambda j, *_: (j, 0)
    grid_spec = pltpu.PrefetchScalarGridSpec(
        num_scalar_prefetch=4,
        grid=(nblk,),
        in_specs=[pl.BlockSpec((MOE_BLOCK, D_MODEL), blk), pl.BlockSpec((LANES, MOE_BLOCK), blk),
                  pl.BlockSpec((LANES, MOE_BLOCK), blk),
                  pl.BlockSpec((1, SUBLANES, LANES), lambda j, *_: (j, 0, 0))],
        out_specs=pl.BlockSpec(memory_space=pl.ANY),
        scratch_shapes=[pltpu.VMEM((2, MOE_CAP, D_MODEL), BF16), pltpu.VMEM((MOE_SEG, D_MODEL), BF16),
                        pltpu.SemaphoreType.DMA((2,)), pltpu.SemaphoreType.DMA(())],
    )
    return pl.pallas_call(
        _moe_gather_kernel,
        grid_spec=grid_spec,
        out_shape=jax.ShapeDtypeStruct((rows, D_MODEL), BF16),
        compiler_params=_params("arbitrary"),
        name="moe_gather",
    )(tables["dst"], tables["nchunk"], tables["pad_row"], tables["pad_n"], x1b, chosen_t, rank_t, tables["bounds"])


def _moe_ffn_kernel(tile_expert_ref, tiles_used_ref, x_ref, w1_ref, w3_ref, w2_ref, y_ref, w13_scr, w2_scr):
    i = pl.program_id(0)

    @pl.when(i < tiles_used_ref[0])
    def _():
        @pl.when((i == 0) | (tile_expert_ref[i] != tile_expert_ref[jnp.maximum(i - 1, 0)]))
        def _():
            w13_scr[:, :EXPERT_FF] = w1_ref[0].astype(BF16)
            w13_scr[:, EXPERT_FF:] = w3_ref[0].astype(BF16)
            w2_scr[...] = w2_ref[0].astype(BF16)

        up = _dot(x_ref[...], w13_scr[...])
        hid = _silu(up[:, :EXPERT_FF]) * up[:, EXPERT_FF:]
        y_ref[...] = _dot(hid.astype(BF16), w2_scr[...]).astype(BF16)


def _moe_ffn(rows, w1, w3, w2, tables, tm):
    n_tiles = tables["tile_expert"].shape[0]
    tile = lambda i, tile_expert, tiles_used: (jnp.minimum(i, tiles_used[0] - 1), 0)
    expert = lambda i, tile_expert, tiles_used: (tile_expert[i], 0, 0)
    grid_spec = pltpu.PrefetchScalarGridSpec(
        num_scalar_prefetch=2,
        grid=(n_tiles,),
        in_specs=[pl.BlockSpec((tm, D_MODEL), tile), pl.BlockSpec((1, D_MODEL, EXPERT_FF), expert),
                  pl.BlockSpec((1, D_MODEL, EXPERT_FF), expert), pl.BlockSpec((1, EXPERT_FF, D_MODEL), expert)],
        out_specs=pl.BlockSpec((tm, D_MODEL), tile),
        scratch_shapes=[pltpu.VMEM((D_MODEL, 2 * EXPERT_FF), BF16), pltpu.VMEM((EXPERT_FF, D_MODEL), BF16)],
    )
    return pl.pallas_call(
        _moe_ffn_kernel,
        grid_spec=grid_spec,
        out_shape=jax.ShapeDtypeStruct(rows.shape, BF16),
        compiler_params=_params("arbitrary"),
        name="moe_ffn",
    )(tables["tile_expert"], tables["tiles_used"], rows, w1, w3, w2)


def _moe_combine_kernel(dst_ref, nchunk_ref, dense_ref, gates_ref, rank_ref, bounds_ref, g_ref, b_ref, rows_ref,
                        op_ref, os_ref, acc, buf, sem, *, prompt_tiles):
    j = pl.program_id(0)
    nblk = pl.num_programs(0)
    slot = j % 2
    nchunk_max = MOE_CAP // MOE_SEG

    def chunk_copy(blk, q, sl):
        row = pl.multiple_of(dst_ref[blk * nchunk_max + q], MOE_SEG)
        dst = buf.at[sl, pl.ds(pl.multiple_of(q * MOE_SEG, MOE_SEG), MOE_SEG), :]
        return pltpu.make_async_copy(rows_ref.at[pl.ds(row, MOE_SEG), :], dst, sem.at[sl])

    def for_chunks(blk, sl, action, also=True):
        for sub in range(MOE_CAP // MOE_SUB):
            @pl.when(also & (sub * MOE_SUB < nchunk_ref[blk] * MOE_SEG))
            def _():
                for q in range(sub * (MOE_SUB // MOE_SEG), (sub + 1) * (MOE_SUB // MOE_SEG)):
                    action(chunk_copy(blk, q, sl))

    @pl.when(j == 0)
    def _():
        buf[...] = jnp.zeros_like(buf)
        for_chunks(0, 0, lambda c: c.start())

    for_chunks(jnp.minimum(j + 1, nblk - 1), 1 - slot, lambda c: c.start(), also=j + 1 < nblk)
    for_chunks(j, slot, lambda c: c.wait())

    bounds_t = jnp.concatenate([bounds_ref[0], jnp.zeros((LANES - SUBLANES, LANES), F32)], axis=0).T
    start, length = bounds_t[:, 0:1], bounds_t[:, 1:2]
    gates = gates_ref[...]
    rank = rank_ref[...]
    used = nchunk_ref[j] * MOE_SEG
    acc[...] = dense_ref[...]
    for sub in range(MOE_CAP // MOE_SUB):
        @pl.when(sub * MOE_SUB < used)
        def _():
            s = (sub * MOE_SUB + lax.broadcasted_iota(jnp.int32, (LANES, MOE_SUB), 1)).astype(F32)
            owner = (s >= start) & (s < start + length)
            within = s[0:1, :] - jnp.sum(jnp.where(owner, start, 0.0), axis=0, keepdims=True)
            owner = owner.astype(BF16)
            weight = jnp.where(_dot(rank, owner) == within, _dot(gates, owner), 0.0)
            acc[...] += _dot(weight.astype(BF16), buf[slot, sub * MOE_SUB:(sub + 1) * MOE_SUB, :])

    @pl.when(j < prompt_tiles)
    def _():
        op_ref[...] = _layer_norm(acc[...], g_ref[...], b_ref[...])

    @pl.when(j >= prompt_tiles)
    def _():
        os_ref[...] = _layer_norm(acc[...], g_ref[...], b_ref[...])


def _moe_combine(rows, dense, gates, rank, tables, g, b, n_prompt):
    m = dense.shape[0]
    nblk = m // MOE_BLOCK
    prompt_tiles = n_prompt // MOE_BLOCK
    in_prompt, in_sample = _group_maps(prompt_tiles)
    blk = lambda j, *_: (j, 0)
    const = lambda j, *_: (0, 0)
    grid_spec = pltpu.PrefetchScalarGridSpec(
        num_scalar_prefetch=2,
        grid=(nblk,),
        in_specs=[pl.BlockSpec((MOE_BLOCK, D_MODEL), blk), pl.BlockSpec((MOE_BLOCK, LANES), blk),
                  pl.BlockSpec((MOE_BLOCK, LANES), blk),
                  pl.BlockSpec((1, SUBLANES, LANES), lambda j, *_: (j, 0, 0)),
                  pl.BlockSpec((1, D_MODEL), const), pl.BlockSpec((1, D_MODEL), const),
                  pl.BlockSpec(memory_space=pl.ANY)],
        out_specs=[pl.BlockSpec((MOE_BLOCK, D_MODEL), in_prompt), pl.BlockSpec((MOE_BLOCK, D_MODEL), in_sample)],
        scratch_shapes=[pltpu.VMEM((MOE_BLOCK, D_MODEL), F32), pltpu.VMEM((2, MOE_CAP, D_MODEL), BF16),
                        pltpu.SemaphoreType.DMA((2,))],
    )
    return pl.pallas_call(
        functools.partial(_moe_combine_kernel, prompt_tiles=prompt_tiles),
        grid_spec=grid_spec,
        out_shape=[jax.ShapeDtypeStruct((n_prompt, D_MODEL), F32),
                   jax.ShapeDtypeStruct((m - n_prompt, D_MODEL), F32)],
        compiler_params=_params("arbitrary"),
        name="moe_combine",
    )(tables["src"], tables["nchunk"], dense, gates, rank, tables["bounds"], g, b, rows)


def _row(v):
    return v.reshape(1, -1).astype(F32)


def _scan_layout(batch, seq, row0):
    t = math.gcd(seq, CHUNK)
    nchunk = seq // t
    chunks = SCAN_CHUNKS if nchunk % SCAN_CHUNKS == 0 else 1
    seqs = SCAN_SEQS if (nchunk == 1 and batch % SCAN_SEQS == 0 and row0 % (SCAN_SEQS * t) == 0) else 1
    return dict(batch=batch, seq=seq, row0=row0, t=t, seqs=seqs, chunks=chunks)


def _lanes_at(v, start):
    return jnp.zeros((1, LANES), F32).at[0, start:start + v.shape[0]].set(v.astype(F32))


def kernel(x_prompt, x_sample, p_prompt, p_sample, state_ssm_conv, state_ssm, state_dn_conv, state_dn, emb_ln_g, emb_ln_b, w_in, conv_a_w, conv_a_b, ssm_dt_bias, ssm_a_log, ssm_d, ssm_norm_w, w_a, conv_b_w, dn_dt_bias, dn_a_log, dn_norm_w, w_b, w_o, ln1_g, ln1_b, router_w, router_bias, exp_w1, exp_w3, exp_w2, sh_w1, sh_w3, sh_w2, ple_w, ple_gate_w, ln2_g, ln2_b):
    bp, lp, _ = x_prompt.shape
    bs, ls, _ = x_sample.shape
    n_p = bp * lp
    n_s = bs * ls
    xp = x_prompt.reshape(n_p, D_MODEL)
    xs = x_sample.reshape(n_s, D_MODEL)

    w = w_in[0]
    o_z, o_xbc, o_dt = 0, SSM_INNER, SSM_INNER + SSM_CONV_CH
    o_qkv = o_dt + SSM_HEADS
    o_a = o_qkv + DN_CONV_CH
    o_b = o_a + DN_HEADS
    o_zb = o_b + DN_HEADS
    o_ga = o_zb + DN_V
    w_pa = w[:, o_z:o_dt].astype(BF16)
    w_pb = jnp.concatenate([w[:, o_qkv:o_a], w[:, o_zb:o_ga]], axis=1).astype(BF16)
    w_pg = w[:, o_ga:].astype(BF16)
    w_ps = jnp.zeros((D_MODEL, LANES), F32)
    w_ps = w_ps.at[:, SMALL_DT:SMALL_DT + SSM_HEADS].set(w[:, o_dt:o_qkv])
    w_ps = w_ps.at[:, SMALL_A:SMALL_A + 2 * DN_HEADS].set(w[:, o_a:o_zb]).astype(BF16)

    eg, eb = _row(emb_ln_g), _row(emb_ln_b)
    proj_a, small = _ln_matmul(xp, xs, eg, eb, w_pa, 1024, 2304, w_ps)
    proj_b = _ln_matmul(xp, xs, eg, eb, w_pb, 1024, 2048)
    proj_g = _ln_matmul(xp, xs, eg, eb, w_pg, 1024, 2048)

    ssd_consts = [conv_a_w[0], _row(conv_a_b[0]), _lanes_at(ssm_dt_bias[0], SMALL_DT),
                  _lanes_at(ssm_a_log[0], SMALL_DT), _row(jnp.repeat(ssm_d[0], SSM_HEAD_DIM)),
                  _row(ssm_norm_w[0])]
    gdn_consts = [conv_b_w[0], _lanes_at(dn_a_log[0], SMALL_A), _lanes_at(dn_dt_bias[0], SMALL_A),
                  _row(dn_norm_w[0])]
    zeros = lambda *s: jnp.zeros(s, F32)
    prompt = _scan_layout(bp, lp, 0)
    sample = _scan_layout(bs, ls, n_p)

    ya_p, pa_conv, pa_ssm = _ssd(proj_a, small, ssd_consts, zeros(bp, CONV_WIDTH - 1, SSM_CONV_CH),
                                 zeros(bp, SSM_HEADS, SSM_HEAD_DIM, SSM_STATE), **prompt)
    ya_s, sa_conv, sa_ssm = _ssd(proj_a, small, ssd_consts, state_ssm_conv[0], state_ssm[0], **sample)
    ob_p, pb_conv, pb_dn = _gdn(proj_b, small, gdn_consts, zeros(bp, CONV_WIDTH - 1, DN_CONV_CH),
                                zeros(bp, DN_HEADS, DN_HEAD, DN_HEAD), **prompt)
    ob_s, sb_conv, sb_dn = _gdn(proj_b, small, gdn_consts, state_dn_conv[0], state_dn[0], **sample)

    x1, x1b = _merge(
        proj_g, xp, xs, ya_p, ya_s, ob_p, ob_s,
        [eg, eb, w_a[0].astype(BF16), w_b[0].astype(BF16), w_o[0].astype(BF16), _row(ln1_g[0]), _row(ln1_b[0])],
        512)

    router_w_t = jnp.zeros((LANES, D_MODEL), F32).at[:N_EXPERTS].set(router_w[0].T)
    router_w_hi = router_w_t.astype(BF16)
    router_w_lo = (router_w_t - router_w_hi.astype(F32)).astype(BF16)
    router_b = jnp.broadcast_to(_lanes_at(router_bias[0], 0).reshape(LANES, 1), (LANES, MOE_BLOCK))
    dense, gates, rank, chosen_t, rank_t, counts = _dense_router(
        x1, x1b, p_prompt[0].reshape(n_p, PLE_DIM), p_sample[0].reshape(n_s, PLE_DIM),
        [sh_w1[0].astype(BF16), sh_w3[0].astype(BF16), sh_w2[0].astype(BF16), ple_w[0].astype(BF16),
         ple_gate_w[0].astype(BF16), router_w_hi, router_w_lo, router_b])

    tables = _moe_tables(counts[:, 0, :N_EXPERTS].astype(jnp.int32), MOE_TM)
    sorted_rows = _moe_gather(x1b, chosen_t, rank_t, tables, MOE_TM)
    expert_out = _moe_ffn(sorted_rows, exp_w1[0], exp_w3[0], exp_w2[0], tables, MOE_TM)
    out_p, out_s = _moe_combine(expert_out, dense, gates, rank, tables, _row(ln2_g[0]), _row(ln2_b[0]), n_p)

    return (out_p.reshape(bp, lp, D_MODEL), out_s.reshape(bs, ls, D_MODEL),
            pa_conv[None], pa_ssm[None], pb_conv[None], pb_dn[None],
            sa_conv[None], sa_ssm[None], sb_conv[None], sb_dn[None])
```

```python
import functools
import math

import jax
import jax.numpy as jnp
from jax import lax
from jax.experimental import pallas as pl
from jax.experimental.pallas import tpu as pltpu

F32 = jnp.float32
BF16 = jnp.bfloat16
HIGHEST = lax.Precision.HIGHEST

D_MODEL = 1024
SSM_INNER = 2048
SSM_HEAD_DIM = 64
SSM_HEADS = 32
SSM_GROUPS = 2
SSM_HEADS_PER_GROUP = 16
SSM_STATE = 128
SSM_CONV_CH = 2560
DN_HEADS = 8
DN_HEAD = 128
DN_QK = 1024
DN_V = 1024
DN_CONV_CH = 3072
CONV_WIDTH = 4
CHUNK = 64
N_EXPERTS = 64
TOP_K = 8
N_EXPERT_GROUPS = 8
EXPERTS_PER_GROUP = 8
TOPK_GROUPS = 4
EXPERT_FF = 256
SHARED_FF = 256
ROUTED_SCALE = 2.5
PLE_DIM = 256
LN_EPS = 1e-5
RMS_EPS = 1e-6
L2_EPS = 1e-6
DEEPNORM_ALPHA = 2.0 ** 0.25

LANES = 128
SUBLANES = 8
VMEM_LIMIT = 56 * 1024 * 1024
SMALL_DT = 0
SMALL_A = 32
SMALL_B = 40
CONV_PAD = 8
SCAN_CHUNKS = 8
SCAN_SEQS = 4
MOE_BLOCK = 256
MOE_SEG = 16
MOE_SUB = 512
MOE_CAP = -(-(MOE_BLOCK * 8 + 64 * (MOE_SEG - 1)) // MOE_SUB) * MOE_SUB
MOE_TM = 512


def _sigmoid(x):
    return 1.0 / (1.0 + jnp.exp(-x))


def _silu(x):
    return x * _sigmoid(x)


def _softplus(x):
    return jnp.maximum(x, 0.0) + jnp.log(1.0 + jnp.exp(-jnp.abs(x)))


def _layer_norm(x, g, b):
    mu = jnp.mean(x, axis=-1, keepdims=True)
    xc = x - mu
    var = jnp.mean(xc * xc, axis=-1, keepdims=True)
    return xc * lax.rsqrt(var + LN_EPS) * g + b


def _dot(a, b):
    return jnp.dot(a, b, preferred_element_type=F32)


def _dot_nt(a, b):
    return lax.dot_general(a, b, (((1,), (1,)), ((), ())), preferred_element_type=F32)


def _dot_tn(a, b):
    return lax.dot_general(a, b, (((0,), (0,)), ((), ())), preferred_element_type=F32)


def _dot_f32(a, b):
    return jnp.dot(a, b, precision=HIGHEST, preferred_element_type=F32)


def _params(*sem):
    return pltpu.CompilerParams(dimension_semantics=sem, vmem_limit_bytes=VMEM_LIMIT)


def _when(cond):
    if cond is True:
        return lambda fn: fn()
    if cond is False:
        return lambda fn: None
    return pl.when(cond)


def _row_tile(m, preferred):
    return max(d for d in range(SUBLANES, min(m, preferred) + 1, SUBLANES) if m % d == 0)


def _group_maps(prompt_tiles):
    in_prompt = lambda i, *_: (jnp.minimum(i, prompt_tiles - 1), 0)
    in_sample = lambda i, *_: (jnp.maximum(i - prompt_tiles, 0), 0)
    return in_prompt, in_sample


def _ln_matmul_kernel(xp_ref, xs_ref, g_ref, b_ref, w_ref, *rest, prompt_tiles, narrow):
    if narrow:
        wn_ref, o_ref, on_ref, h_scr = rest
    else:
        o_ref, h_scr = rest

    @pl.when(pl.program_id(1) == 0)
    def _():
        x = jnp.where(pl.program_id(0) < prompt_tiles, xp_ref[...], xs_ref[...])
        h_scr[...] = _layer_norm(x, g_ref[...], b_ref[...]).astype(BF16)
        if narrow:
            on_ref[...] = _dot(h_scr[...], wn_ref[...])

    o_ref[...] = _dot(h_scr[...], w_ref[...]).astype(o_ref.dtype)


def _ln_matmul(xp, xs, g, b, w, tm, tn, w_narrow=None):
    k = xp.shape[1]
    m = xp.shape[0] + xs.shape[0]
    n = w.shape[1]
    tm = _row_tile(math.gcd(xp.shape[0], xs.shape[0]), tm)
    prompt_tiles = xp.shape[0] // tm
    in_prompt, in_sample = _group_maps(prompt_tiles)
    narrow = w_narrow is not None
    const = lambda i, j: (0, 0)
    row = lambda i, j: (i, 0)
    in_specs = [pl.BlockSpec((tm, k), in_prompt), pl.BlockSpec((tm, k), in_sample),
                pl.BlockSpec((1, k), const), pl.BlockSpec((1, k), const), pl.BlockSpec((k, tn), lambda i, j: (0, j))]
    out_specs = [pl.BlockSpec((tm, tn), lambda i, j: (i, j))]
    out_shape = [jax.ShapeDtypeStruct((m, n), BF16)]
    args = [xp, xs, g, b, w]
    if narrow:
        in_specs.append(pl.BlockSpec((k, LANES), const))
        out_specs.append(pl.BlockSpec((tm, LANES), row))
        out_shape.append(jax.ShapeDtypeStruct((m, LANES), F32))
        args.append(w_narrow)
    out = pl.pallas_call(
        functools.partial(_ln_matmul_kernel, prompt_tiles=prompt_tiles, narrow=narrow),
        grid=(m // tm, n // tn),
        in_specs=in_specs,
        out_specs=out_specs,
        out_shape=out_shape,
        scratch_shapes=[pltpu.VMEM((tm, k), BF16)],
        compiler_params=_params("parallel", "arbitrary"),
        name="ln_matmul",
    )(*args)
    return out if narrow else out[0]


def _load_conv_state(cbuf, state):
    taps = CONV_WIDTH - 1
    cbuf[0:SUBLANES - taps, :] = jnp.zeros((SUBLANES - taps, cbuf.shape[1]), F32)
    cbuf[SUBLANES - taps:SUBLANES, :] = state


def _causal_conv(cbuf, x16, cw, t):
    ch = x16.shape[1]
    prev = cbuf[...]
    taps = CONV_WIDTH - 1
    if t <= SUBLANES:
        x = x16.astype(F32)
        ext = jnp.concatenate([prev, x], axis=0)
        lo = SUBLANES - taps
        y = ext[lo:lo + t] * cw[0:1]
        y = y + ext[lo + 1:lo + 1 + t] * cw[1:2]
        y = y + ext[lo + 2:lo + 2 + t] * cw[2:3]
        y = y + x * cw[3:4]
        cbuf[...] = ext[t:t + SUBLANES]
        return y, ext[SUBLANES + t - taps:SUBLANES + t]
    hi = prev.astype(BF16).astype(F32)
    mid = (prev - hi).astype(BF16).astype(F32)
    low = ((prev - hi) - mid).astype(BF16).astype(F32)
    x = x16.astype(F32)
    pack = 2 * SUBLANES
    nhead = 4 * SUBLANES if t % pack == 0 else 5 * SUBLANES
    head = jnp.concatenate([hi, mid, low, jnp.zeros((nhead - 3 * SUBLANES, ch), F32)], axis=0)
    if t % pack == 0:
        ext = jnp.concatenate([head.astype(BF16), x16], axis=0)
    else:
        ext = jnp.concatenate([head, x], axis=0).astype(BF16)
    out_row = lax.broadcasted_iota(jnp.int32, (taps * t, nhead + t), 0)
    col = lax.broadcasted_iota(jnp.int32, (taps * t, nhead + t), 1)
    log_t = t.bit_length() - 1
    shift = lax.shift_right_logical(out_row, log_t) + 1
    src = (out_row & (t - 1)) - shift
    in_x = (src >= 0) & (col == nhead + src)
    in_prev = (src < 0) & (col < 3 * SUBLANES) & ((col & (SUBLANES - 1)) == SUBLANES + src)
    shifted = _dot((in_x | in_prev).astype(BF16), ext)
    y = shifted[2 * t:3 * t] * cw[0:1]
    y = y + shifted[t:2 * t] * cw[1:2]
    y = y + shifted[0:t] * cw[2:3]
    y = y + x * cw[3:4]
    cbuf[...] = x[t - SUBLANES:t]
    return y, x[t - taps:t]


def _lane_expand(v, h0, count, width):
    t = v.shape[0]
    n = count * width
    out = jnp.broadcast_to(v[:, h0:h0 + 1], (t, n))
    if count > 1:
        lane = lax.broadcasted_iota(jnp.int32, (t, n), 1)
        for i in range(1, count):
            out = jnp.where(lane >= i * width, jnp.broadcast_to(v[:, h0 + i:h0 + i + 1], (t, n)), out)
    return out


def _grouped_transpose(v, per_tile):
    t = v.shape[0]
    blocks = [v if r == 0 else pltpu.roll(v, LANES - r, axis=1) for r in range(per_tile)]
    if per_tile * t < LANES:
        blocks.append(jnp.zeros((LANES - per_tile * t, LANES), F32))
    return jnp.concatenate(blocks, axis=0).T


def _same_block(i, j, size):
    shift = size.bit_length() - 1
    return lax.shift_right_logical(i, shift) == lax.shift_right_logical(j, shift)


def _unit_lower_inverses(lmats, ii, jj, t):
    mm = lambda a, b: _dot(a.astype(BF16), b.astype(BF16))
    base = min(16, t)
    in_base = _same_block(ii, jj, base)
    eye = (ii == jj).astype(F32)
    power = [jnp.where(in_base, l, 0.0) for l in lmats]
    inv = [eye - p for p in power]
    span = 2
    while span < base:
        power = [mm(p, p) for p in power]
        inv = [a + mm(a, p) for a, p in zip(inv, power)]
        span *= 2
    size = base
    while size < t:
        link = _same_block(ii, jj, 2 * size) & jnp.logical_not(_same_block(ii, jj, size))
        cross = [mm(a, jnp.where(link, l, 0.0)) for a, l in zip(inv, lmats)]
        inv = [a - mm(c, a) for a, c in zip(inv, cross)]
        size *= 2
    return inv


def _row_sums_of_squares(blocks):
    t = blocks[0].shape[0]
    sq = jnp.concatenate([b * b for b in blocks], axis=0)
    hi = sq.astype(BF16)
    lo = (sq - hi.astype(F32)).astype(BF16)
    ones = jnp.ones((LANES, LANES), BF16)
    sums = _dot(hi, ones) + _dot(lo, ones)
    return [sums[i * t:(i + 1) * t] for i in range(len(blocks))]


def _cumsum_rows(v, t):
    ii = lax.broadcasted_iota(jnp.int32, (t, t), 0)
    jj = lax.broadcasted_iota(jnp.int32, (t, t), 1)
    return _dot_f32((jj <= ii).astype(F32), v)


def _ssd_kernel(pa_ref, sm_ref, cw_ref, cb_ref, dtb_ref, alog_ref, dskip_ref, nw_ref, cst_ref, h0_ref,
                y_ref, cnew_ref, hnew_ref, cbuf, s_scr, *, t, first, last):
    per_tile = LANES // t
    width = per_tile * SSM_HEAD_DIM

    @_when(first)
    def _():
        _load_conv_state(cbuf, cst_ref[0])
        s_scr[...] = h0_ref[0]

    z = pa_ref[:, :SSM_INNER].astype(F32)
    conv, tail = _causal_conv(cbuf, pa_ref[:, SSM_INNER:], cw_ref[...], t)

    @_when(last)
    def _():
        cnew_ref[0] = tail

    xbc = _silu(conv + cb_ref[...])
    xs = xbc[:, :SSM_INNER]
    bm = xbc[:, SSM_INNER:SSM_INNER + SSM_GROUPS * SSM_STATE]
    cm = xbc[:, SSM_INNER + SSM_GROUPS * SSM_STATE:]

    dt = _softplus(sm_ref[...] + dtb_ref[...])
    da = dt * (-jnp.exp(alog_ref[...]))
    cum = _cumsum_rows(da, t)
    cum_t = _grouped_transpose(cum, per_tile)
    dt_t = _grouped_transpose(dt, per_tile)

    row = lax.broadcasted_iota(jnp.int32, (t, LANES), 0)
    lane = lax.broadcasted_iota(jnp.int32, (t, LANES), 1)
    causal = (lane % t) <= row
    brow = lax.broadcasted_iota(jnp.int32, (LANES, width), 0)
    bcol = lax.broadcasted_iota(jnp.int32, (LANES, width), 1)
    blockdiag = (brow // t) == (bcol // SSM_HEAD_DIM)

    ys = []
    for g in range(SSM_GROUPS):
        bg = bm[:, g * SSM_STATE:(g + 1) * SSM_STATE]
        cg = cm[:, g * SSM_STATE:(g + 1) * SSM_STATE].astype(BF16)
        cb = _dot_nt(cg, jnp.concatenate([bg] * per_tile, axis=0).astype(BF16))
        bg = bg.astype(BF16)
        for q in range(SSM_HEADS_PER_GROUP // per_tile):
            h0 = g * SSM_HEADS_PER_GROUP + q * per_tile
            ccol = _lane_expand(cum, h0, per_tile, t)
            seg = ccol - cum_t[h0:h0 + 1, :]
            decay = jnp.exp(jnp.where(causal, seg, -jnp.inf))
            wts = (cb * decay * dt_t[h0:h0 + 1, :]).astype(BF16)
            xt = xs[:, h0 * SSM_HEAD_DIM:h0 * SSM_HEAD_DIM + width]
            xbd = jnp.where(blockdiag, jnp.concatenate([xt] * per_tile, axis=0), 0.0).astype(BF16)
            y_intra = _dot(wts, xbd)
            st = s_scr[h0:h0 + per_tile].reshape(width, SSM_STATE)
            cum_w = ccol if width == LANES else _lane_expand(cum, h0, per_tile, SSM_HEAD_DIM)
            y_state = _dot_nt(cg, st.astype(BF16)) * jnp.exp(cum_w)
            ys.append(y_intra + y_state)
            wend = jnp.exp(cum_w[t - 1:t, :] - cum_w) * _lane_expand(dt, h0, per_tile, SSM_HEAD_DIM)
            ds = _dot_tn((xt * wend).astype(BF16), bg)
            for i in range(per_tile):
                h = h0 + i
                s_scr[h] = jnp.exp(cum[t - 1:t, h:h + 1]) * s_scr[h] + ds[i * SSM_HEAD_DIM:(i + 1) * SSM_HEAD_DIM]

    y = jnp.concatenate(ys, axis=1) + dskip_ref[...] * xs
    y = y * _silu(z)
    half = SSM_INNER // SSM_GROUPS
    normed = []
    for g in range(SSM_GROUPS):
        yg = y[:, g * half:(g + 1) * half]
        normed.append(yg * lax.rsqrt(jnp.mean(yg * yg, axis=-1, keepdims=True) + RMS_EPS))
    y_ref[...] = (jnp.concatenate(normed, axis=1) * nw_ref[...]).astype(BF16)

    @_when(last)
    def _():
        hnew_ref[0] = s_scr[...]


def _scan_step_kernel(chunk_kernel, n_tok, n_const, n_state, seqs, chunks, t):
    def step(*refs):
        tok = refs[:n_tok]
        const = refs[n_tok:n_tok + n_const]
        st_in = refs[n_tok + n_const:n_tok + n_const + n_state]
        y_ref = refs[n_tok + n_const + n_state]
        st_out = refs[n_tok + n_const + n_state + 1:n_tok + n_const + 2 * n_state + 1]
        scratch = refs[n_tok + n_const + 2 * n_state + 1:]
        first_step = pl.program_id(1) == 0
        last_step = pl.program_id(1) == pl.num_programs(1) - 1
        for s in range(seqs):
            for c in range(chunks):
                rows = pl.ds((s * chunks + c) * t, t)
                chunk_kernel(*[r.at[rows, :] for r in tok], *const, *[r.at[pl.ds(s, 1)] for r in st_in],
                             y_ref.at[rows, :], *[r.at[pl.ds(s, 1)] for r in st_out], *scratch,
                             first=first_step if c == 0 else False,
                             last=last_step if c == chunks - 1 else False)
    return step


def _scan_call(kernel, name, tok_in, const_in, state_in, y_width, scratch, *, batch, seq, row0, t, seqs, chunks,
               step_kernel=None):
    nstep = seq // (t * chunks)
    rows = seqs * chunks * t
    blk0 = row0 // rows

    def per_batch(a):
        zeros = (0,) * (a.ndim - 1)
        return pl.BlockSpec((seqs,) + a.shape[1:], lambda b, c: (b,) + zeros)

    if step_kernel is None:
        step_kernel = _scan_step_kernel(kernel, len(tok_in), len(const_in), len(state_in), seqs, chunks, t)
    return pl.pallas_call(
        step_kernel,
        grid=(batch // seqs, nstep),
        in_specs=[pl.BlockSpec((rows, a.shape[1]), lambda b, c: (blk0 + b * nstep + c, 0)) for a in tok_in]
        + [pl.BlockSpec(a.shape, lambda b, c: (0, 0)) for a in const_in]
        + [per_batch(a) for a in state_in],
        out_specs=[pl.BlockSpec((rows, y_width), lambda b, c: (b * nstep + c, 0))]
        + [per_batch(a) for a in state_in],
        out_shape=[jax.ShapeDtypeStruct((batch * seq, y_width), BF16)]
        + [jax.ShapeDtypeStruct(a.shape, F32) for a in state_in],
        scratch_shapes=scratch,
        compiler_params=_params("parallel", "arbitrary"),
        name=name,
    )(*tok_in, *const_in, *state_in)


def _ssd(proj_a, small, consts, conv_state, h0, **where):
    t = where["t"]
    scratch = [pltpu.VMEM((SUBLANES, SSM_CONV_CH), F32),
               pltpu.VMEM((SSM_HEADS, SSM_HEAD_DIM, SSM_STATE), F32)]
    return _scan_call(functools.partial(_ssd_kernel, t=t), f"ssd_t{t}", [proj_a, small], consts,
                      [conv_state, h0], SSM_INNER, scratch, **where)


def _gdn_step_kernel(pb_ref, sm_ref, cw_ref, alog_ref, dtb_ref, nw_ref, cst_ref, s0_ref,
                     o_ref, cnew_ref, snew_ref, cbuf, s_scr, *, t, seqs, chunks):
    first_step = pl.program_id(1) == 0
    last_step = pl.program_id(1) == pl.num_programs(1) - 1
    units = [(s, c) for s in range(seqs) for c in range(chunks)]
    rows = lambda u: pl.ds(u * t, t)
    per_tile = min(LANES // t, DN_HEADS)
    n = per_tile * t
    ntile = DN_HEADS // per_tile
    heads = range(DN_HEADS)
    ii = lax.broadcasted_iota(jnp.int32, (n, n), 0)
    jj = lax.broadcasted_iota(jnp.int32, (n, n), 1)
    causal = _same_block(ii, jj, t) & (jj <= ii)
    diag = ii == jj
    stack = lambda xs, p: jnp.concatenate(xs[p * per_tile:(p + 1) * per_tile], axis=0)

    qkv, zb, gcum, beta = [], [], [], []
    for u, (s, c) in enumerate(units):
        @_when(first_step if c == 0 else False)
        def _():
            _load_conv_state(cbuf, cst_ref[s])

        conv, tail = _causal_conv(cbuf, pb_ref[rows(u), :DN_CONV_CH], cw_ref[...], t)

        @_when(last_step if c == chunks - 1 else False)
        def _():
            cnew_ref[s] = tail

        qkv.append(_silu(conv))
        zb.append(pb_ref[rows(u), DN_CONV_CH:].astype(F32))
        sm = sm_ref[rows(u), :]
        beta.append(_sigmoid(sm))
        gate = -jnp.exp(alog_ref[...]) * _softplus(sm + dtb_ref[...])
        gcum.append(_cumsum_rows(gate, t))
    gcum_t = [_grouped_transpose(g, per_tile) for g in gcum]

    sumsq = _row_sums_of_squares([x[:, i * DN_HEAD:(i + 1) * DN_HEAD] for x in qkv for i in range(2 * DN_HEADS)])
    q, k, v, gcol, bcol, kb, egcol = ([[] for _ in units] for _ in range(7))
    for u in range(len(units)):
        for h in heads:
            lo = h * DN_HEAD
            ss = sumsq[u * 2 * DN_HEADS:(u + 1) * 2 * DN_HEADS]
            q[u].append(qkv[u][:, lo:lo + DN_HEAD] * lax.rsqrt(ss[h] + L2_EPS) * (DN_HEAD ** -0.5))
            k[u].append(qkv[u][:, DN_QK + lo:DN_QK + lo + DN_HEAD] * lax.rsqrt(ss[DN_HEADS + h] + L2_EPS))
            v[u].append(qkv[u][:, 2 * DN_QK + lo:2 * DN_QK + lo + DN_HEAD])
            gcol[u].append(gcum[u][:, SMALL_A + h:SMALL_A + h + 1])
            bcol[u].append(beta[u][:, SMALL_B + h:SMALL_B + h + 1])
            kb[u].append(k[u][h] * bcol[u][h])
            egcol[u].append(jnp.exp(gcol[u][h]))

    tiles = [(u, p) for u in range(len(units)) for p in range(ntile)]
    k16 = [stack(k[u], p).astype(BF16) for u, p in tiles]
    decay = []
    for u, p in tiles:
        h0 = p * per_tile
        seg = stack(gcol[u], p) - gcum_t[u][SMALL_A + h0:SMALL_A + h0 + 1, :n]
        decay.append(jnp.exp(jnp.where(causal, seg, -jnp.inf)))
    lmat = [_dot_nt(stack(kb[u], p).astype(BF16), k16[i]) * jnp.where(diag, 0.0, decay[i])
            for i, (u, p) in enumerate(tiles)]
    qk = [(_dot_nt(stack(q[u], p).astype(BF16), k16[i]) * decay[i]).astype(BF16) for i, (u, p) in enumerate(tiles)]
    inv = [a.astype(BF16) for a in _unit_lower_inverses(lmat, ii, jj, t)]

    split = lambda x: (x[:, :DN_HEAD], x[:, DN_HEAD:])
    o_from, o_own, s_from, s_own, glast = ([[] for _ in units] for _ in range(5))
    aw = [_dot(inv[i], jnp.concatenate(
        [stack([kb[u][h] * egcol[u][h] for h in heads], p), stack([v[u][h] * bcol[u][h] for h in heads], p)],
        axis=1).astype(BF16)) for i, (u, p) in enumerate(tiles)]
    qaw = [_dot(qk[i], aw[i].astype(BF16)) for i in range(len(tiles))]
    pad_rows = -t % (2 * SUBLANES)
    pad = [jnp.zeros((pad_rows, DN_HEAD), F32)] if pad_rows else []
    for u in range(len(units)):
        for h in heads:
            p, j = divmod(h, per_tile)
            i = u * ntile + p
            sel = slice(j * t, (j + 1) * t)
            qkw, qku0 = split(qaw[i][sel])
            o_from[u].append(q[u][h] * egcol[u][h] - qkw)
            o_own[u].append(qku0)
            glast[u].append(gcum[u][t - 1:t, SMALL_A + h:SMALL_A + h + 1])
            kd = (k[u][h] * jnp.exp(glast[u][h] - gcol[u][h])).astype(BF16)
            kdw, kdu0 = split(_dot_tn(kd, aw[i][sel].astype(BF16)))
            s_from[u].append(kdw)
            s_own[u].append(kdu0)

    outs = []
    for u, (s, c) in enumerate(units):
        @_when(first_step if c == 0 else False)
        def _():
            s_scr[...] = s0_ref[s]

        for h in heads:
            state = s_scr[h]
            both = _dot(jnp.concatenate([s_from[u][h], o_from[u][h]] + pad, axis=0).astype(BF16), state.astype(BF16))
            outs.append(both[DN_HEAD:DN_HEAD + t] + o_own[u][h])
            s_scr[h] = jnp.exp(glast[u][h]) * state - both[:DN_HEAD] + s_own[u][h]

        @_when(last_step if c == chunks - 1 else False)
        def _():
            snew_ref[s] = s_scr[...]

    norm_w = nw_ref[...]
    osq = _row_sums_of_squares(outs)
    for u in range(len(units)):
        normed = []
        for h in heads:
            i = u * DN_HEADS + h
            normed.append(outs[i] * lax.rsqrt(osq[i] * (1.0 / DN_HEAD) + RMS_EPS) * norm_w
                          * _silu(zb[u][:, h * DN_HEAD:(h + 1) * DN_HEAD]))
        o_ref[rows(u), :] = jnp.concatenate(normed, axis=1).astype(BF16)


def _gdn(proj_b, small, consts, conv_state, s0, **where):
    t = where["t"]
    scratch = [pltpu.VMEM((SUBLANES, DN_CONV_CH), F32),
               pltpu.VMEM((DN_HEADS, DN_HEAD, DN_HEAD), F32)]
    step = functools.partial(_gdn_step_kernel, t=t, seqs=where["seqs"], chunks=where["chunks"])
    return _scan_call(None, f"gdn_t{t}", [proj_b, small], consts, [conv_state, s0], DN_V, scratch,
                      step_kernel=step, **where)


def _merge_kernel(gate_ref, xp_ref, xs_ref, yap_ref, yas_ref, obp_ref, obs_ref, eg_ref, eb_ref, wa_ref, wb_ref,
                  wo_ref, g_ref, b_ref, x1_ref, x1b_ref, *, prompt_tiles):
    is_prompt = pl.program_id(0) < prompt_tiles
    h = _layer_norm(jnp.where(is_prompt, xp_ref[...], xs_ref[...]), eg_ref[...], eb_ref[...])
    out_a = _dot(jnp.where(is_prompt, yap_ref[...], yas_ref[...]), wa_ref[...])
    out_b = _dot(jnp.where(is_prompt, obp_ref[...], obs_ref[...]), wb_ref[...])
    gt = gate_ref[...].astype(F32)
    merged = _sigmoid(gt[:, :D_MODEL]) * out_a + _sigmoid(gt[:, D_MODEL:]) * out_b
    mix = _dot(merged.astype(BF16), wo_ref[...])
    x1 = _layer_norm(DEEPNORM_ALPHA * h + mix, g_ref[...], b_ref[...])
    x1_ref[...] = x1
    x1b_ref[...] = x1.astype(BF16)


def _merge(proj_g, xp, xs, ya_p, ya_s, ob_p, ob_s, consts, tm):
    m = proj_g.shape[0]
    n_p = xp.shape[0]
    tm = _row_tile(math.gcd(n_p, m - n_p), tm)
    in_prompt, in_sample = _group_maps(n_p // tm)
    row = lambda i: (i, 0)
    const = lambda i: (0, 0)
    return pl.pallas_call(
        functools.partial(_merge_kernel, prompt_tiles=n_p // tm),
        grid=(m // tm,),
        in_specs=[pl.BlockSpec((tm, 2 * D_MODEL), row),
                  pl.BlockSpec((tm, D_MODEL), in_prompt), pl.BlockSpec((tm, D_MODEL), in_sample),
                  pl.BlockSpec((tm, SSM_INNER), in_prompt), pl.BlockSpec((tm, SSM_INNER), in_sample),
                  pl.BlockSpec((tm, DN_V), in_prompt), pl.BlockSpec((tm, DN_V), in_sample)]
        + [pl.BlockSpec(a.shape, const) for a in consts],
        out_specs=[pl.BlockSpec((tm, D_MODEL), row), pl.BlockSpec((tm, D_MODEL), row)],
        out_shape=[jax.ShapeDtypeStruct((m, D_MODEL), F32), jax.ShapeDtypeStruct((m, D_MODEL), BF16)],
        compiler_params=_params("parallel"),
        name="merge",
    )(proj_g, xp, xs, ya_p, ya_s, ob_p, ob_s, *consts)


def _take_first_max(rest, index, limit, axis):
    best = jnp.max(rest, axis=axis, keepdims=True)
    first = jnp.min(jnp.where(rest == best, index, limit), axis=axis, keepdims=True)
    hit = index == first
    return hit, jnp.where(hit, -jnp.inf, rest)


def _router_gates_t(x1, xb, rw_hi, rw_lo, bias):
    tm = x1.shape[0]
    x_lo = (x1 - xb.astype(F32)).astype(BF16)
    logits = _dot_nt(rw_hi, xb) + (_dot_nt(rw_hi, x_lo) + _dot_nt(rw_lo, xb))
    scores = _sigmoid(logits)
    row = lax.broadcasted_iota(jnp.int32, (LANES, tm), 0)
    sel = jnp.where(row < N_EXPERTS, scores + bias, -jnp.inf)
    by_group = sel.reshape(LANES // EXPERTS_PER_GROUP, EXPERTS_PER_GROUP, tm)
    sub = lax.broadcasted_iota(jnp.int32, by_group.shape, 1)
    top1 = jnp.max(by_group, axis=1, keepdims=True)
    _, others = _take_first_max(by_group, sub, EXPERTS_PER_GROUP, 1)
    gscore = (top1 + jnp.max(others, axis=1, keepdims=True))[:N_EXPERT_GROUPS]
    gidx = lax.broadcasted_iota(jnp.int32, gscore.shape, 0)
    gkeep = jnp.zeros(gscore.shape, jnp.bool_)
    for _ in range(TOPK_GROUPS):
        hit, gscore = _take_first_max(gscore, gidx, N_EXPERT_GROUPS, 0)
        gkeep = gkeep | hit
    gkeep = jnp.broadcast_to(gkeep, (N_EXPERT_GROUPS, EXPERTS_PER_GROUP, tm)).reshape(N_EXPERTS, tm)
    rest = jnp.where(gkeep, sel[:N_EXPERTS], -jnp.inf)
    eidx = lax.broadcasted_iota(jnp.int32, (N_EXPERTS, tm), 0)
    keep = jnp.zeros((N_EXPERTS, tm), jnp.bool_)
    for _ in range(TOP_K):
        hit, rest = _take_first_max(rest, eidx, N_EXPERTS, 0)
        keep = keep | hit
    picked = jnp.where(keep, scores[:N_EXPERTS], 0.0)
    gates = picked / jnp.sum(picked, axis=0, keepdims=True) * ROUTED_SCALE
    return jnp.concatenate([gates, jnp.zeros((LANES - N_EXPERTS, tm), F32)], axis=0)


def _dense_router_kernel(x1_ref, x1b_ref, pp_ref, ps_ref, sw1_ref, sw3_ref, sw2_ref, pw_ref, pg_ref, rwh_ref,
                         rwl_ref, rb_ref, dense_ref, gates_ref, rank_ref, chosen_t_ref, rank_t_ref, count_ref,
                         *, prompt_tiles):
    xb = x1b_ref[...]
    hid = _silu(_dot(xb, sw1_ref[...])) * _dot(xb, sw3_ref[...])
    shared = _dot(hid.astype(BF16), sw2_ref[...])
    p = jnp.where(pl.program_id(0) < prompt_tiles, pp_ref[...], ps_ref[...])
    ple = _dot(p.astype(BF16), pw_ref[...]) * _sigmoid(_dot(xb, pg_ref[...]))
    x1 = x1_ref[...]
    dense_ref[...] = DEEPNORM_ALPHA * x1 + shared + ple
    gates_t = _router_gates_t(x1, xb, rwh_ref[...], rwl_ref[...], rb_ref[...])
    tm = x1.shape[0]
    chosen_t = (gates_t > 0.0).astype(BF16)
    ii = lax.broadcasted_iota(jnp.int32, (tm, tm), 0)
    jj = lax.broadcasted_iota(jnp.int32, (tm, tm), 1)
    rank_t = _dot(chosen_t, (ii < jj).astype(BF16))
    chosen_t_ref[...] = chosen_t
    rank_t_ref[...] = rank_t.astype(BF16)
    gates = gates_t.T
    rank = rank_t.T
    gates_ref[...] = gates.astype(BF16)
    rank_ref[...] = rank.astype(BF16)
    count = rank[tm - 1:tm, :] + (gates[tm - 1:tm, :] > 0.0).astype(F32)
    count_ref[0] = jnp.broadcast_to(count, (SUBLANES, LANES))


def _dense_router(x1, x1b, p_prompt, p_sample, consts):
    m = x1.shape[0]
    nblk = m // MOE_BLOCK
    prompt_tiles = p_prompt.shape[0] // MOE_BLOCK
    in_prompt, in_sample = _group_maps(prompt_tiles)
    row = lambda i: (i, 0)
    const = lambda i: (0, 0)
    return pl.pallas_call(
        functools.partial(_dense_router_kernel, prompt_tiles=prompt_tiles),
        grid=(nblk,),
        in_specs=[pl.BlockSpec((MOE_BLOCK, D_MODEL), row), pl.BlockSpec((MOE_BLOCK, D_MODEL), row),
                  pl.BlockSpec((MOE_BLOCK, PLE_DIM), in_prompt), pl.BlockSpec((MOE_BLOCK, PLE_DIM), in_sample)]
        + [pl.BlockSpec(a.shape, const) for a in consts],
        out_specs=[pl.BlockSpec((MOE_BLOCK, D_MODEL), row), pl.BlockSpec((MOE_BLOCK, LANES), row),
                   pl.BlockSpec((MOE_BLOCK, LANES), row), pl.BlockSpec((LANES, MOE_BLOCK), row),
                   pl.BlockSpec((LANES, MOE_BLOCK), row), pl.BlockSpec((1, SUBLANES, LANES), lambda i: (i, 0, 0))],
        out_shape=[jax.ShapeDtypeStruct((m, D_MODEL), F32), jax.ShapeDtypeStruct((m, LANES), BF16),
                   jax.ShapeDtypeStruct((m, LANES), BF16), jax.ShapeDtypeStruct((nblk * LANES, MOE_BLOCK), BF16),
                   jax.ShapeDtypeStruct((nblk * LANES, MOE_BLOCK), BF16),
                   jax.ShapeDtypeStruct((nblk, SUBLANES, LANES), F32)],
        compiler_params=_params("parallel"),
        name="dense_router",
    )(x1, x1b, p_prompt, p_sample, *consts)


def _moe_tables(counts, tm):
    nblk = counts.shape[0]
    nchunk_max = MOE_CAP // MOE_SEG
    padded = (counts + MOE_SEG - 1) // MOE_SEG * MOE_SEG
    start = jnp.cumsum(padded, axis=1) - padded
    used = jnp.sum(padded, axis=1)
    rows_e = jnp.sum(padded, axis=0)
    rows_e_t = (rows_e + tm - 1) // tm * tm
    base_e = jnp.cumsum(rows_e_t) - rows_e_t
    seg_row = base_e[None, :] + jnp.cumsum(padded, axis=0) - padded
    chunk0 = jnp.arange(nchunk_max, dtype=jnp.int32) * MOE_SEG
    owner = jnp.sum((start + padded)[:, None, :] <= chunk0[None, :, None], axis=-1)
    onehot = owner[..., None] == jnp.arange(N_EXPERTS, dtype=jnp.int32)
    dst = chunk0[None, :] + jnp.sum(jnp.where(onehot, (seg_row - start)[:, None, :], 0), axis=-1)
    n_tiles = _moe_max_tiles(nblk, tm)
    live = chunk0[None, :] < used[:, None]
    parity = (jnp.arange(nblk, dtype=jnp.int32) % 2)[:, None]
    spare = n_tiles * tm + parity * MOE_CAP + chunk0[None, :]
    src = jnp.where(live, dst, 0)
    dst = jnp.where(live, dst, spare)
    tile_row = jnp.arange(n_tiles, dtype=jnp.int32) * tm
    tiles_used = jnp.sum(rows_e_t) // tm
    tile_row = jnp.minimum(tile_row, (tiles_used - 1) * tm)
    tile_expert = jnp.minimum(jnp.sum((base_e + rows_e_t)[None, :] <= tile_row[:, None], axis=-1), N_EXPERTS - 1)
    bounds = jnp.zeros((nblk, SUBLANES, LANES), F32)
    bounds = bounds.at[:, 0, :N_EXPERTS].set(start.astype(F32)).at[:, 1, :N_EXPERTS].set(padded.astype(F32))
    i32 = lambda a: a.astype(jnp.int32)
    return dict(dst=i32(dst.reshape(-1)), src=i32(src.reshape(-1)), nchunk=i32(used // MOE_SEG),
                pad_row=i32(base_e + rows_e),
                pad_n=i32((rows_e_t - rows_e) // MOE_SEG), tile_expert=i32(tile_expert),
                tiles_used=i32(tiles_used.reshape(1)), bounds=bounds)


def _moe_max_tiles(nblk, tm):
    return (nblk * (MOE_BLOCK * TOP_K + N_EXPERTS * (MOE_SEG - 1)) + N_EXPERTS * (tm - MOE_SEG) + tm - 1) // tm


def _slot_bounds(bounds):
    return bounds[0:1, :], bounds[1:2, :]


def _moe_gather_kernel(dst_ref, nchunk_ref, pad_row_ref, pad_n_ref,
                       x_ref, chosen_t_ref, rank_t_ref, bounds_ref, rows_ref, buf, zbuf, sem, zsem):
    j = pl.program_id(0)
    nblk = pl.num_programs(0)
    slot = j % 2
    nchunk_max = MOE_CAP // MOE_SEG

    def chunk_copy(blk, q, sl):
        src = buf.at[sl, pl.ds(pl.multiple_of(q * MOE_SEG, MOE_SEG), MOE_SEG), :]
        row = pl.multiple_of(dst_ref[blk * nchunk_max + q], MOE_SEG)
        return pltpu.make_async_copy(src, rows_ref.at[pl.ds(row, MOE_SEG), :], sem.at[sl])

    def pad_copy(e, i):
        row = pl.multiple_of(pad_row_ref[e] + i * MOE_SEG, MOE_SEG)
        return pltpu.make_async_copy(zbuf, rows_ref.at[pl.ds(row, MOE_SEG), :], zsem)

    def sub_chunks(blk, sub, sl, action):
        for q in range(sub * (MOE_SUB // MOE_SEG), (sub + 1) * (MOE_SUB // MOE_SEG)):
            action(chunk_copy(blk, q, sl))

    def sub_used(blk, sub):
        return sub * MOE_SUB < nchunk_ref[blk] * MOE_SEG

    def for_pads(action):
        def per_expert(e, carry):
            def body(i, c):
                action(pad_copy(e, i))
                return c
            return lax.fori_loop(0, pad_n_ref[e], body, carry)
        lax.fori_loop(0, N_EXPERTS, per_expert, 0)

    @pl.when(j == 0)
    def _():
        zbuf[...] = jnp.zeros_like(zbuf)
        for_pads(lambda c: c.start())

    start, length = _slot_bounds(bounds_ref[0])
    chosen_t = chosen_t_ref[...]
    rank_t = rank_t_ref[...]
    x = x_ref[...]
    before = jnp.maximum(j - 2, 0)
    for sub in range(MOE_CAP // MOE_SUB):
        @pl.when((j >= 2) & sub_used(before, sub))
        def _():
            sub_chunks(before, sub, slot, lambda c: c.wait())

    for sub in range(MOE_CAP // MOE_SUB):
        @pl.when(sub_used(j, sub))
        def _():
            s = (sub * MOE_SUB + lax.broadcasted_iota(jnp.int32, (MOE_SUB, LANES), 0)).astype(F32)
            owner = (s >= start) & (s < start + length)
            within = s[:, 0:1] - jnp.sum(jnp.where(owner, start, 0.0), axis=1, keepdims=True)
            owner = owner.astype(BF16)
            pick = (_dot(owner, chosen_t) > 0.5) & (_dot(owner, rank_t) == within)
            buf[slot, sub * MOE_SUB:(sub + 1) * MOE_SUB, :] = _dot(pick.astype(BF16), x).astype(BF16)
            sub_chunks(j, sub, slot, lambda c: c.start())

    @pl.when(j == nblk - 1)
    def _():
        prev = jnp.maximum(j - 1, 0)
        for sub in range(MOE_CAP // MOE_SUB):
            @pl.when(sub_used(j, sub))
            def _():
                sub_chunks(j, sub, slot, lambda c: c.wait())

            @pl.when((j >= 1) & sub_used(prev, sub))
            def _():
                sub_chunks(prev, sub, 1 - slot, lambda c: c.wait())

        for_pads(lambda c: c.wait())


def _moe_gather(x1b, chosen_t, rank_t, tables, tm):
    m = x1b.shape[0]
    nblk = m // MOE_BLOCK
    rows = _moe_max_tiles(nblk, tm) * tm + 2 * MOE_CAP
    blk = lambda j, *_: (j, 0)
    grid_spec = pltpu.PrefetchScalarGridSpec(
        num_scalar_prefetch=4,
        grid=(nblk,),
        in_specs=[pl.BlockSpec((MOE_BLOCK, D_MODEL), blk), pl.BlockSpec((LANES, MOE_BLOCK), blk),
                  pl.BlockSpec((LANES, MOE_BLOCK), blk),
                  pl.BlockSpec((1, SUBLANES, LANES), lambda j, *_: (j, 0, 0))],
        out_specs=pl.BlockSpec(memory_space=pl.ANY),
        scratch_shapes=[pltpu.VMEM((2, MOE_CAP, D_MODEL), BF16), pltpu.VMEM((MOE_SEG, D_MODEL), BF16),
                        pltpu.SemaphoreType.DMA((2,)), pltpu.SemaphoreType.DMA(())],
    )
    return pl.pallas_call(
        _moe_gather_kernel,
        grid_spec=grid_spec,
        out_shape=jax.ShapeDtypeStruct((rows, D_MODEL), BF16),
        compiler_params=_params("arbitrary"),
        name="moe_gather",
    )(tables["dst"], tables["nchunk"], tables["pad_row"], tables["pad_n"], x1b, chosen_t, rank_t, tables["bounds"])


def _moe_ffn_kernel(tile_expert_ref, tiles_used_ref, x_ref, w1_ref, w3_ref, w2_ref, y_ref, w13_scr, w2_scr):
    i = pl.program_id(0)

    @pl.when(i < tiles_used_ref[0])
    def _():
        @pl.when((i == 0) | (tile_expert_ref[i] != tile_expert_ref[jnp.maximum(i - 1, 0)]))
        def _():
            w13_scr[:, :EXPERT_FF] = w1_ref[0].astype(BF16)
            w13_scr[:, EXPERT_FF:] = w3_ref[0].astype(BF16)
            w2_scr[...] = w2_ref[0].astype(BF16)

        up = _dot(x_ref[...], w13_scr[...])
        hid = _silu(up[:, :EXPERT_FF]) * up[:, EXPERT_FF:]
        y_ref[...] = _dot(hid.astype(BF16), w2_scr[...]).astype(BF16)


def _moe_ffn(rows, w1, w3, w2, tables, tm):
    n_tiles = tables["tile_expert"].shape[0]
    tile = lambda i, tile_expert, tiles_used: (jnp.minimum(i, tiles_used[0] - 1), 0)
    expert = lambda i, tile_expert, tiles_used: (tile_expert[i], 0, 0)
    grid_spec = pltpu.PrefetchScalarGridSpec(
        num_scalar_prefetch=2,
        grid=(n_tiles,),
        in_specs=[pl.BlockSpec((tm, D_MODEL), tile), pl.BlockSpec((1, D_MODEL, EXPERT_FF), expert),
                  pl.BlockSpec((1, D_MODEL, EXPERT_FF), expert), pl.BlockSpec((1, EXPERT_FF, D_MODEL), expert)],
        out_specs=pl.BlockSpec((tm, D_MODEL), tile),
        scratch_shapes=[pltpu.VMEM((D_MODEL, 2 * EXPERT_FF), BF16), pltpu.VMEM((EXPERT_FF, D_MODEL), BF16)],
    )
    return pl.pallas_call(
        _moe_ffn_kernel,
        grid_spec=grid_spec,
        out_shape=jax.ShapeDtypeStruct(rows.shape, BF16),
        compiler_params=_params("arbitrary"),
        name="moe_ffn",
    )(tables["tile_expert"], tables["tiles_used"], rows, w1, w3, w2)


def _moe_combine_kernel(dst_ref, nchunk_ref, dense_ref, gates_ref, rank_ref, bounds_ref, g_ref, b_ref, rows_ref,
                        op_ref, os_ref, acc, buf, sem, *, prompt_tiles):
    j = pl.program_id(0)
    nblk = pl.num_programs(0)
    slot = j % 2
    nchunk_max = MOE_CAP // MOE_SEG

    def chunk_copy(blk, q, sl):
        row = pl.multiple_of(dst_ref[blk * nchunk_max + q], MOE_SEG)
        dst = buf.at[sl, pl.ds(pl.multiple_of(q * MOE_SEG, MOE_SEG), MOE_SEG), :]
        return pltpu.make_async_copy(rows_ref.at[pl.ds(row, MOE_SEG), :], dst, sem.at[sl])

    def for_chunks(blk, sl, action, also=True):
        for sub in range(MOE_CAP // MOE_SUB):
            @pl.when(also & (sub * MOE_SUB < nchunk_ref[blk] * MOE_SEG))
            def _():
                for q in range(sub * (MOE_SUB // MOE_SEG), (sub + 1) * (MOE_SUB // MOE_SEG)):
                    action(chunk_copy(blk, q, sl))

    @pl.when(j == 0)
    def _():
        buf[...] = jnp.zeros_like(buf)
        for_chunks(0, 0, lambda c: c.start())

    for_chunks(jnp.minimum(j + 1, nblk - 1), 1 - slot, lambda c: c.start(), also=j + 1 < nblk)
    for_chunks(j, slot, lambda c: c.wait())

    bounds_t = jnp.concatenate([bounds_ref[0], jnp.zeros((LANES - SUBLANES, LANES), F32)], axis=0).T
    start, length = bounds_t[:, 0:1], bounds_t[:, 1:2]
    gates = gates_ref[...]
    rank = rank_ref[...]
    used = nchunk_ref[j] * MOE_SEG
    acc[...] = dense_ref[...]
    for sub in range(MOE_CAP // MOE_SUB):
        @pl.when(sub * MOE_SUB < used)
        def _():
            s = (sub * MOE_SUB + lax.broadcasted_iota(jnp.int32, (LANES, MOE_SUB), 1)).astype(F32)
            owner = (s >= start) & (s < start + length)
            within = s[0:1, :] - jnp.sum(jnp.where(owner, start, 0.0), axis=0, keepdims=True)
            owner = owner.astype(BF16)
            weight = jnp.where(_dot(rank, owner) == within, _dot(gates, owner), 0.0)
            acc[...] += _dot(weight.astype(BF16), buf[slot, sub * MOE_SUB:(sub + 1) * MOE_SUB, :])

    @pl.when(j < prompt_tiles)
    def _():
        op_ref[...] = _layer_norm(acc[...], g_ref[...], b_ref[...])

    @pl.when(j >= prompt_tiles)
    def _():
        os_ref[...] = _layer_norm(acc[...], g_ref[...], b_ref[...])


def _moe_combine(rows, dense, gates, rank, tables, g, b, n_prompt):
    m = dense.shape[0]
    nblk = m // MOE_BLOCK
    prompt_tiles = n_prompt // MOE_BLOCK
    in_prompt, in_sample = _group_maps(prompt_tiles)
    blk = lambda j, *_: (j, 0)
    const = lambda j, *_: (0, 0)
    grid_spec = pltpu.PrefetchScalarGridSpec(
        num_scalar_prefetch=2,
        grid=(nblk,),
        in_specs=[pl.BlockSpec((MOE_BLOCK, D_MODEL), blk), pl.BlockSpec((MOE_BLOCK, LANES), blk),
                  pl.BlockSpec((MOE_BLOCK, LANES), blk),
                  pl.BlockSpec((1, SUBLANES, LANES), lambda j, *_: (j, 0, 0)),
                  pl.BlockSpec((1, D_MODEL), const), pl.BlockSpec((1, D_MODEL), const),
                  pl.BlockSpec(memory_space=pl.ANY)],
        out_specs=[pl.BlockSpec((MOE_BLOCK, D_MODEL), in_prompt), pl.BlockSpec((MOE_BLOCK, D_MODEL), in_sample)],
        scratch_shapes=[pltpu.VMEM((MOE_BLOCK, D_MODEL), F32), pltpu.VMEM((2, MOE_CAP, D_MODEL), BF16),
                        pltpu.SemaphoreType.DMA((2,))],
    )
    return pl.pallas_call(
        functools.partial(_moe_combine_kernel, prompt_tiles=prompt_tiles),
        grid_spec=grid_spec,
        out_shape=[jax.ShapeDtypeStruct((n_prompt, D_MODEL), F32),
                   jax.ShapeDtypeStruct((m - n_prompt, D_MODEL), F32)],
        compiler_params=_params("arbitrary"),
        name="moe_combine",
    )(tables["src"], tables["nchunk"], dense, gates, rank, tables["bounds"], g, b, rows)


def _row(v):
    return v.reshape(1, -1).astype(F32)


def _scan_layout(batch, seq, row0):
    t = math.gcd(seq, CHUNK)
    nchunk = seq // t
    chunks = SCAN_CHUNKS if nchunk % SCAN_CHUNKS == 0 else 1
    seqs = SCAN_SEQS if (nchunk == 1 and batch % SCAN_SEQS == 0 and row0 % (SCAN_SEQS * t) == 0) else 1
    return dict(batch=batch, seq=seq, row0=row0, t=t, seqs=seqs, chunks=chunks)


def _lanes_at(v, start):
    return jnp.zeros((1, LANES), F32).at[0, start:start + v.shape[0]].set(v.astype(F32))


def kernel(x_prompt, x_sample, p_prompt, p_sample, state_ssm_conv, state_ssm, state_dn_conv, state_dn, emb_ln_g, emb_ln_b, w_in, conv_a_w, conv_a_b, ssm_dt_bias, ssm_a_log, ssm_d, ssm_norm_w, w_a, conv_b_w, dn_dt_bias, dn_a_log, dn_norm_w, w_b, w_o, ln1_g, ln1_b, router_w, router_bias, exp_w1, exp_w3, exp_w2, sh_w1, sh_w3, sh_w2, ple_w, ple_gate_w, ln2_g, ln2_b):
    bp, lp, _ = x_prompt.shape
    bs, ls, _ = x_sample.shape
    n_p = bp * lp
    n_s = bs * ls
    xp = x_prompt.reshape(n_p, D_MODEL)
    xs = x_sample.reshape(n_s, D_MODEL)

    w = w_in[0]
    o_z, o_xbc, o_dt = 0, SSM_INNER, SSM_INNER + SSM_CONV_CH
    o_qkv = o_dt + SSM_HEADS
    o_a = o_qkv + DN_CONV_CH
    o_b = o_a + DN_HEADS
    o_zb = o_b + DN_HEADS
    o_ga = o_zb + DN_V
    w_pa = w[:, o_z:o_dt].astype(BF16)
    w_pb = jnp.concatenate([w[:, o_qkv:o_a], w[:, o_zb:o_ga]], axis=1).astype(BF16)
    w_pg = w[:, o_ga:].astype(BF16)
    w_ps = jnp.zeros((D_MODEL, LANES), F32)
    w_ps = w_ps.at[:, SMALL_DT:SMALL_DT + SSM_HEADS].set(w[:, o_dt:o_qkv])
    w_ps = w_ps.at[:, SMALL_A:SMALL_A + 2 * DN_HEADS].set(w[:, o_a:o_zb]).astype(BF16)

    eg, eb = _row(emb_ln_g), _row(emb_ln_b)
    proj_a, small = _ln_matmul(xp, xs, eg, eb, w_pa, 1024, 2304, w_ps)
    proj_b = _ln_matmul(xp, xs, eg, eb, w_pb, 1024, 2048)
    proj_g = _ln_matmul(xp, xs, eg, eb, w_pg, 1024, 2048)

    ssd_consts = [conv_a_w[0], _row(conv_a_b[0]), _lanes_at(ssm_dt_bias[0], SMALL_DT),
                  _lanes_at(ssm_a_log[0], SMALL_DT), _row(jnp.repeat(ssm_d[0], SSM_HEAD_DIM)),
                  _row(ssm_norm_w[0])]
    gdn_consts = [conv_b_w[0], _lanes_at(dn_a_log[0], SMALL_A), _lanes_at(dn_dt_bias[0], SMALL_A),
                  _row(dn_norm_w[0])]
    zeros = lambda *s: jnp.zeros(s, F32)
    prompt = _scan_layout(bp, lp, 0)
    sample = _scan_layout(bs, ls, n_p)

    ya_p, pa_conv, pa_ssm = _ssd(proj_a, small, ssd_consts, zeros(bp, CONV_WIDTH - 1, SSM_CONV_CH),
                                 zeros(bp, SSM_HEADS, SSM_HEAD_DIM, SSM_STATE), **prompt)
    ya_s, sa_conv, sa_ssm = _ssd(proj_a, small, ssd_consts, state_ssm_conv[0], state_ssm[0], **sample)
    ob_p, pb_conv, pb_dn = _gdn(proj_b, small, gdn_consts, zeros(bp, CONV_WIDTH - 1, DN_CONV_CH),
                                zeros(bp, DN_HEADS, DN_HEAD, DN_HEAD), **prompt)
    ob_s, sb_conv, sb_dn = _gdn(proj_b, small, gdn_consts, state_dn_conv[0], state_dn[0], **sample)

    x1, x1b = _merge(
        proj_g, xp, xs, ya_p, ya_s, ob_p, ob_s,
        [eg, eb, w_a[0].astype(BF16), w_b[0].astype(BF16), w_o[0].astype(BF16), _row(ln1_g[0]), _row(ln1_b[0])],
        512)

    router_w_t = jnp.zeros((LANES, D_MODEL), F32).at[:N_EXPERTS].set(router_w[0].T)
    router_w_hi = router_w_t.astype(BF16)
    router_w_lo = (router_w_t - router_w_hi.astype(F32)).astype(BF16)
    router_b = jnp.broadcast_to(_lanes_at(router_bias[0], 0).reshape(LANES, 1), (LANES, MOE_BLOCK))
    dense, gates, rank, chosen_t, rank_t, counts = _dense_router(
        x1, x1b, p_prompt[0].reshape(n_p, PLE_DIM), p_sample[0].reshape(n_s, PLE_DIM),
        [sh_w1[0].astype(BF16), sh_w3[0].astype(BF16), sh_w2[0].astype(BF16), ple_w[0].astype(BF16),
         ple_gate_w[0].astype(BF16), router_w_hi, router_w_lo, router_b])

    tables = _moe_tables(counts[:, 0, :N_EXPERTS].astype(jnp.int32), MOE_TM)
    sorted_rows = _moe_gather(x1b, chosen_t, rank_t, tables, MOE_TM)
    expert_out = _moe_ffn(sorted_rows, exp_w1[0], exp_w3[0], exp_w2[0], tables, MOE_TM)
    out_p, out_s = _moe_combine(expert_out, dense, gates, rank, tables, _row(ln2_g[0]), _row(ln2_b[0]), n_p)

    return (out_p.reshape(bp, lp, D_MODEL), out_s.reshape(bs, ls, D_MODEL),
            pa_conv[None], pa_ssm[None], pb_conv[None], pb_dn[None],
            sa_conv[None], sa_ssm[None], sb_conv[None], sb_dn[None])
```

```python
import functools
import math

import jax
import jax.numpy as jnp
from jax import lax
from jax.experimental import pallas as pl
from jax.experimental.pallas import tpu as pltpu

F32 = jnp.float32
BF16 = jnp.bfloat16
HIGHEST = lax.Precision.HIGHEST

D_MODEL = 1024
SSM_INNER = 2048
SSM_HEAD_DIM = 64
SSM_HEADS = 32
SSM_GROUPS = 2
SSM_HEADS_PER_GROUP = 16
SSM_STATE = 128
SSM_CONV_CH = 2560
DN_HEADS = 8
DN_HEAD = 128
DN_QK = 1024
DN_V = 1024
DN_CONV_CH = 3072
CONV_WIDTH = 4
CHUNK = 64
N_EXPERTS = 64
TOP_K = 8
N_EXPERT_GROUPS = 8
EXPERTS_PER_GROUP = 8
TOPK_GROUPS = 4
EXPERT_FF = 256
SHARED_FF = 256
ROUTED_SCALE = 2.5
PLE_DIM = 256
LN_EPS = 1e-5
RMS_EPS = 1e-6
L2_EPS = 1e-6
DEEPNORM_ALPHA = 2.0 ** 0.25

LANES = 128
SUBLANES = 8
VMEM_LIMIT = 56 * 1024 * 1024
SMALL_DT = 0
SMALL_A = 32
SMALL_B = 40
CONV_PAD = 8
SCAN_CHUNKS = 8
SCAN_SEQS = 4
MOE_BLOCK = 256
MOE_SEG = 16
MOE_SUB = 512
MOE_CAP = -(-(MOE_BLOCK * 8 + 64 * (MOE_SEG - 1)) // MOE_SUB) * MOE_SUB
MOE_TM = 512


def _sigmoid(x):
    return 1.0 / (1.0 + jnp.exp(-x))


def _silu(x):
    return x * _sigmoid(x)


def _softplus(x):
    return jnp.maximum(x, 0.0) + jnp.log(1.0 + jnp.exp(-jnp.abs(x)))


def _layer_norm(x, g, b):
    mu = jnp.mean(x, axis=-1, keepdims=True)
    xc = x - mu
    var = jnp.mean(xc * xc, axis=-1, keepdims=True)
    return xc * lax.rsqrt(var + LN_EPS) * g + b


def _dot(a, b):
    return jnp.dot(a, b, preferred_element_type=F32)


def _dot_nt(a, b):
    return lax.dot_general(a, b, (((1,), (1,)), ((), ())), preferred_element_type=F32)


def _dot_tn(a, b):
    return lax.dot_general(a, b, (((0,), (0,)), ((), ())), preferred_element_type=F32)


def _dot_f32(a, b):
    return jnp.dot(a, b, precision=HIGHEST, preferred_element_type=F32)


def _params(*sem):
    return pltpu.CompilerParams(dimension_semantics=sem, vmem_limit_bytes=VMEM_LIMIT)


def _when(cond):
    if cond is True:
        return lambda fn: fn()
    if cond is False:
        return lambda fn: None
    return pl.when(cond)


def _row_tile(m, preferred):
    return max(d for d in range(SUBLANES, min(m, preferred) + 1, SUBLANES) if m % d == 0)


def _group_maps(prompt_tiles):
    in_prompt = lambda i, *_: (jnp.minimum(i, prompt_tiles - 1), 0)
    in_sample = lambda i, *_: (jnp.maximum(i - prompt_tiles, 0), 0)
    return in_prompt, in_sample


def _ln_matmul_kernel(xp_ref, xs_ref, g_ref, b_ref, w_ref, *rest, prompt_tiles, narrow):
    if narrow:
        wn_ref, o_ref, on_ref, h_scr = rest
    else:
        o_ref, h_scr = rest

    @pl.when(pl.program_id(1) == 0)
    def _():
        x = jnp.where(pl.program_id(0) < prompt_tiles, xp_ref[...], xs_ref[...])
        h_scr[...] = _layer_norm(x, g_ref[...], b_ref[...]).astype(BF16)
        if narrow:
            on_ref[...] = _dot(h_scr[...], wn_ref[...])

    o_ref[...] = _dot(h_scr[...], w_ref[...]).astype(o_ref.dtype)


def _ln_matmul(xp, xs, g, b, w, tm, tn, w_narrow=None):
    k = xp.shape[1]
    m = xp.shape[0] + xs.shape[0]
    n = w.shape[1]
    tm = _row_tile(math.gcd(xp.shape[0], xs.shape[0]), tm)
    prompt_tiles = xp.shape[0] // tm
    in_prompt, in_sample = _group_maps(prompt_tiles)
    narrow = w_narrow is not None
    const = lambda i, j: (0, 0)
    row = lambda i, j: (i, 0)
    in_specs = [pl.BlockSpec((tm, k), in_prompt), pl.BlockSpec((tm, k), in_sample),
                pl.BlockSpec((1, k), const), pl.BlockSpec((1, k), const), pl.BlockSpec((k, tn), lambda i, j: (0, j))]
    out_specs = [pl.BlockSpec((tm, tn), lambda i, j: (i, j))]
    out_shape = [jax.ShapeDtypeStruct((m, n), BF16)]
    args = [xp, xs, g, b, w]
    if narrow:
        in_specs.append(pl.BlockSpec((k, LANES), const))
        out_specs.append(pl.BlockSpec((tm, LANES), row))
        out_shape.append(jax.ShapeDtypeStruct((m, LANES), F32))
        args.append(w_narrow)
    out = pl.pallas_call(
        functools.partial(_ln_matmul_kernel, prompt_tiles=prompt_tiles, narrow=narrow),
        grid=(m // tm, n // tn),
        in_specs=in_specs,
        out_specs=out_specs,
        out_shape=out_shape,
        scratch_shapes=[pltpu.VMEM((tm, k), BF16)],
        compiler_params=_params("parallel", "arbitrary"),
        name="ln_matmul",
    )(*args)
    return out if narrow else out[0]


def _load_conv_state(cbuf, state):
    taps = CONV_WIDTH - 1
    cbuf[0:SUBLANES - taps, :] = jnp.zeros((SUBLANES - taps, cbuf.shape[1]), F32)
    cbuf[SUBLANES - taps:SUBLANES, :] = state


def _causal_conv(cbuf, x16, cw, t):
    ch = x16.shape[1]
    prev = cbuf[...]
    taps = CONV_WIDTH - 1
    if t <= SUBLANES:
        x = x16.astype(F32)
        ext = jnp.concatenate([prev, x], axis=0)
        lo = SUBLANES - taps
        y = ext[lo:lo + t] * cw[0:1]
        y = y + ext[lo + 1:lo + 1 + t] * cw[1:2]
        y = y + ext[lo + 2:lo + 2 + t] * cw[2:3]
        y = y + x * cw[3:4]
        cbuf[...] = ext[t:t + SUBLANES]
        return y, ext[SUBLANES + t - taps:SUBLANES + t]
    hi = prev.astype(BF16).astype(F32)
    mid = (prev - hi).astype(BF16).astype(F32)
    low = ((prev - hi) - mid).astype(BF16).astype(F32)
    x = x16.astype(F32)
    pack = 2 * SUBLANES
    nhead = 4 * SUBLANES if t % pack == 0 else 5 * SUBLANES
    head = jnp.concatenate([hi, mid, low, jnp.zeros((nhead - 3 * SUBLANES, ch), F32)], axis=0)
    if t % pack == 0:
        ext = jnp.concatenate([head.astype(BF16), x16], axis=0)
    else:
        ext = jnp.concatenate([head, x], axis=0).astype(BF16)
    out_row = lax.broadcasted_iota(jnp.int32, (taps * t, nhead + t), 0)
    col = lax.broadcasted_iota(jnp.int32, (taps * t, nhead + t), 1)
    log_t = t.bit_length() - 1
    shift = lax.shift_right_logical(out_row, log_t) + 1
    src = (out_row & (t - 1)) - shift
    in_x = (src >= 0) & (col == nhead + src)
    in_prev = (src < 0) & (col < 3 * SUBLANES) & ((col & (SUBLANES - 1)) == SUBLANES + src)
    shifted = _dot((in_x | in_prev).astype(BF16), ext)
    y = shifted[2 * t:3 * t] * cw[0:1]
    y = y + shifted[t:2 * t] * cw[1:2]
    y = y + shifted[0:t] * cw[2:3]
    y = y + x * cw[3:4]
    cbuf[...] = x[t - SUBLANES:t]
    return y, x[t - taps:t]


def _lane_expand(v, h0, count, width):
    t = v.shape[0]
    n = count * width
    out = jnp.broadcast_to(v[:, h0:h0 + 1], (t, n))
    if count > 1:
        lane = lax.broadcasted_iota(jnp.int32, (t, n), 1)
        for i in range(1, count):
            out = jnp.where(lane >= i * width, jnp.broadcast_to(v[:, h0 + i:h0 + i + 1], (t, n)), out)
    return out


def _grouped_transpose(v, per_tile):
    t = v.shape[0]
    blocks = [v if r == 0 else pltpu.roll(v, LANES - r, axis=1) for r in range(per_tile)]
    if per_tile * t < LANES:
        blocks.append(jnp.zeros((LANES - per_tile * t, LANES), F32))
    return jnp.concatenate(blocks, axis=0).T


def _same_block(i, j, size):
    shift = size.bit_length() - 1
    return lax.shift_right_logical(i, shift) == lax.shift_right_logical(j, shift)


def _unit_lower_inverses(lmats, ii, jj, t):
    mm = lambda a, b: _dot(a.astype(BF16), b.astype(BF16))
    base = min(16, t)
    in_base = _same_block(ii, jj, base)
    eye = (ii == jj).astype(F32)
    power = [jnp.where(in_base, l, 0.0) for l in lmats]
    inv = [eye - p for p in power]
    span = 2
    while span < base:
        power = [mm(p, p) for p in power]
        inv = [a + mm(a, p) for a, p in zip(inv, power)]
        span *= 2
    size = base
    while size < t:
        link = _same_block(ii, jj, 2 * size) & jnp.logical_not(_same_block(ii, jj, size))
        cross = [mm(a, jnp.where(link, l, 0.0)) for a, l in zip(inv, lmats)]
        inv = [a - mm(c, a) for a, c in zip(inv, cross)]
        size *= 2
    return inv


def _row_sums_of_squares(blocks):
    t = blocks[0].shape[0]
    sq = jnp.concatenate([b * b for b in blocks], axis=0)
    hi = sq.astype(BF16)
    lo = (sq - hi.astype(F32)).astype(BF16)
    ones = jnp.ones((LANES, LANES), BF16)
    sums = _dot(hi, ones) + _dot(lo, ones)
    return [sums[i * t:(i + 1) * t] for i in range(len(blocks))]


def _cumsum_rows(v, t):
    ii = lax.broadcasted_iota(jnp.int32, (t, t), 0)
    jj = lax.broadcasted_iota(jnp.int32, (t, t), 1)
    return _dot_f32((jj <= ii).astype(F32), v)


def _ssd_kernel(pa_ref, sm_ref, cw_ref, cb_ref, dtb_ref, alog_ref, dskip_ref, nw_ref, cst_ref, h0_ref,
                y_ref, cnew_ref, hnew_ref, cbuf, s_scr, *, t, first, last):
    per_tile = LANES // t
    width = per_tile * SSM_HEAD_DIM

    @_when(first)
    def _():
        _load_conv_state(cbuf, cst_ref[0])
        s_scr[...] = h0_ref[0]

    z = pa_ref[:, :SSM_INNER].astype(F32)
    conv, tail = _causal_conv(cbuf, pa_ref[:, SSM_INNER:], cw_ref[...], t)

    @_when(last)
    def _():
        cnew_ref[0] = tail

    xbc = _silu(conv + cb_ref[...])
    xs = xbc[:, :SSM_INNER]
    bm = xbc[:, SSM_INNER:SSM_INNER + SSM_GROUPS * SSM_STATE]
    cm = xbc[:, SSM_INNER + SSM_GROUPS * SSM_STATE:]

    dt = _softplus(sm_ref[...] + dtb_ref[...])
    da = dt * (-jnp.exp(alog_ref[...]))
    cum = _cumsum_rows(da, t)
    cum_t = _grouped_transpose(cum, per_tile)
    dt_t = _grouped_transpose(dt, per_tile)

    row = lax.broadcasted_iota(jnp.int32, (t, LANES), 0)
    lane = lax.broadcasted_iota(jnp.int32, (t, LANES), 1)
    causal = (lane % t) <= row
    brow = lax.broadcasted_iota(jnp.int32, (LANES, width), 0)
    bcol = lax.broadcasted_iota(jnp.int32, (LANES, width), 1)
    blockdiag = (brow // t) == (bcol // SSM_HEAD_DIM)

    ys = []
    for g in range(SSM_GROUPS):
        bg = bm[:, g * SSM_STATE:(g + 1) * SSM_STATE]
        cg = cm[:, g * SSM_STATE:(g + 1) * SSM_STATE].astype(BF16)
        cb = _dot_nt(cg, jnp.concatenate([bg] * per_tile, axis=0).astype(BF16))
        bg = bg.astype(BF16)
        for q in range(SSM_HEADS_PER_GROUP // per_tile):
            h0 = g * SSM_HEADS_PER_GROUP + q * per_tile
            ccol = _lane_expand(cum, h0, per_tile, t)
            seg = ccol - cum_t[h0:h0 + 1, :]
            decay = jnp.exp(jnp.where(causal, seg, -jnp.inf))
            wts = (cb * decay * dt_t[h0:h0 + 1, :]).astype(BF16)
            xt = xs[:, h0 * SSM_HEAD_DIM:h0 * SSM_HEAD_DIM + width]
            xbd = jnp.where(blockdiag, jnp.concatenate([xt] * per_tile, axis=0), 0.0).astype(BF16)
            y_intra = _dot(wts, xbd)
            st = s_scr[h0:h0 + per_tile].reshape(width, SSM_STATE)
            cum_w = ccol if width == LANES else _lane_expand(cum, h0, per_tile, SSM_HEAD_DIM)
            y_state = _dot_nt(cg, st.astype(BF16)) * jnp.exp(cum_w)
            ys.append(y_intra + y_state)
            wend = jnp.exp(cum_w[t - 1:t, :] - cum_w) * _lane_expand(dt, h0, per_tile, SSM_HEAD_DIM)
            ds = _dot_tn((xt * wend).astype(BF16), bg)
            for i in range(per_tile):
                h = h0 + i
                s_scr[h] = jnp.exp(cum[t - 1:t, h:h + 1]) * s_scr[h] + ds[i * SSM_HEAD_DIM:(i + 1) * SSM_HEAD_DIM]

    y = jnp.concatenate(ys, axis=1) + dskip_ref[...] * xs
    y = y * _silu(z)
    half = SSM_INNER // SSM_GROUPS
    normed = []
    for g in range(SSM_GROUPS):
        yg = y[:, g * half:(g + 1) * half]
        normed.append(yg * lax.rsqrt(jnp.mean(yg * yg, axis=-1, keepdims=True) + RMS_EPS))
    y_ref[...] = (jnp.concatenate(normed, axis=1) * nw_ref[...]).astype(BF16)

    @_when(last)
    def _():
        hnew_ref[0] = s_scr[...]


def _scan_step_kernel(chunk_kernel, n_tok, n_const, n_state, seqs, chunks, t):
    def step(*refs):
        tok = refs[:n_tok]
        const = refs[n_tok:n_tok + n_const]
        st_in = refs[n_tok + n_const:n_tok + n_const + n_state]
        y_ref = refs[n_tok + n_const + n_state]
        st_out = refs[n_tok + n_const + n_state + 1:n_tok + n_const + 2 * n_state + 1]
        scratch = refs[n_tok + n_const + 2 * n_state + 1:]
        first_step = pl.program_id(1) == 0
        last_step = pl.program_id(1) == pl.num_programs(1) - 1
        for s in range(seqs):
            for c in range(chunks):
                rows = pl.ds((s * chunks + c) * t, t)
                chunk_kernel(*[r.at[rows, :] for r in tok], *const, *[r.at[pl.ds(s, 1)] for r in st_in],
                             y_ref.at[rows, :], *[r.at[pl.ds(s, 1)] for r in st_out], *scratch,
                             first=first_step if c == 0 else False,
                             last=last_step if c == chunks - 1 else False)
    return step


def _scan_call(kernel, name, tok_in, const_in, state_in, y_width, scratch, *, batch, seq, row0, t, seqs, chunks,
               step_kernel=None):
    nstep = seq // (t * chunks)
    rows = seqs * chunks * t
    blk0 = row0 // rows

    def per_batch(a):
        zeros = (0,) * (a.ndim - 1)
        return pl.BlockSpec((seqs,) + a.shape[1:], lambda b, c: (b,) + zeros)

    if step_kernel is None:
        step_kernel = _scan_step_kernel(kernel, len(tok_in), len(const_in), len(state_in), seqs, chunks, t)
    return pl.pallas_call(
        step_kernel,
        grid=(batch // seqs, nstep),
        in_specs=[pl.BlockSpec((rows, a.shape[1]), lambda b, c: (blk0 + b * nstep + c, 0)) for a in tok_in]
        + [pl.BlockSpec(a.shape, lambda b, c: (0, 0)) for a in const_in]
        + [per_batch(a) for a in state_in],
        out_specs=[pl.BlockSpec((rows, y_width), lambda b, c: (b * nstep + c, 0))]
        + [per_batch(a) for a in state_in],
        out_shape=[jax.ShapeDtypeStruct((batch * seq, y_width), BF16)]
        + [jax.ShapeDtypeStruct(a.shape, F32) for a in state_in],
        scratch_shapes=scratch,
        compiler_params=_params("parallel", "arbitrary"),
        name=name,
    )(*tok_in, *const_in, *state_in)


def _ssd(proj_a, small, consts, conv_state, h0, **where):
    t = where["t"]
    scratch = [pltpu.VMEM((SUBLANES, SSM_CONV_CH), F32),
               pltpu.VMEM((SSM_HEADS, SSM_HEAD_DIM, SSM_STATE), F32)]
    return _scan_call(functools.partial(_ssd_kernel, t=t), f"ssd_t{t}", [proj_a, small], consts,
                      [conv_state, h0], SSM_INNER, scratch, **where)


def _gdn_step_kernel(pb_ref, sm_ref, cw_ref, alog_ref, dtb_ref, nw_ref, cst_ref, s0_ref,
                     o_ref, cnew_ref, snew_ref, cbuf, s_scr, *, t, seqs, chunks):
    first_step = pl.program_id(1) == 0
    last_step = pl.program_id(1) == pl.num_programs(1) - 1
    units = [(s, c) for s in range(seqs) for c in range(chunks)]
    rows = lambda u: pl.ds(u * t, t)
    per_tile = min(LANES // t, DN_HEADS)
    n = per_tile * t
    ntile = DN_HEADS // per_tile
    heads = range(DN_HEADS)
    ii = lax.broadcasted_iota(jnp.int32, (n, n), 0)
    jj = lax.broadcasted_iota(jnp.int32, (n, n), 1)
    causal = _same_block(ii, jj, t) & (jj <= ii)
    diag = ii == jj
    stack = lambda xs, p: jnp.concatenate(xs[p * per_tile:(p + 1) * per_tile], axis=0)

    qkv, zb, gcum, beta = [], [], [], []
    for u, (s, c) in enumerate(units):
        @_when(first_step if c == 0 else False)
        def _():
            _load_conv_state(cbuf, cst_ref[s])

        conv, tail = _causal_conv(cbuf, pb_ref[rows(u), :DN_CONV_CH], cw_ref[...], t)

        @_when(last_step if c == chunks - 1 else False)
        def _():
            cnew_ref[s] = tail

        qkv.append(_silu(conv))
        zb.append(pb_ref[rows(u), DN_CONV_CH:].astype(F32))
        sm = sm_ref[rows(u), :]
        beta.append(_sigmoid(sm))
        gate = -jnp.exp(alog_ref[...]) * _softplus(sm + dtb_ref[...])
        gcum.append(_cumsum_rows(gate, t))
    gcum_t = [_grouped_transpose(g, per_tile) for g in gcum]

    sumsq = _row_sums_of_squares([x[:, i * DN_HEAD:(i + 1) * DN_HEAD] for x in qkv for i in range(2 * DN_HEADS)])
    q, k, v, gcol, bcol, kb, egcol = ([[] for _ in units] for _ in range(7))
    for u in range(len(units)):
        for h in heads:
            lo = h * DN_HEAD
            ss = sumsq[u * 2 * DN_HEADS:(u + 1) * 2 * DN_HEADS]
            q[u].append(qkv[u][:, lo:lo + DN_HEAD] * lax.rsqrt(ss[h] + L2_EPS) * (DN_HEAD ** -0.5))
            k[u].append(qkv[u][:, DN_QK + lo:DN_QK + lo + DN_HEAD] * lax.rsqrt(ss[DN_HEADS + h] + L2_EPS))
            v[u].append(qkv[u][:, 2 * DN_QK + lo:2 * DN_QK + lo + DN_HEAD])
            gcol[u].append(gcum[u][:, SMALL_A + h:SMALL_A + h + 1])
            bcol[u].append(beta[u][:, SMALL_B + h:SMALL_B + h + 1])
            kb[u].append(k[u][h] * bcol[u][h])
            egcol[u].append(jnp.exp(gcol[u][h]))

    tiles = [(u, p) for u in range(len(units)) for p in range(ntile)]
    k16 = [stack(k[u], p).astype(BF16) for u, p in tiles]
    decay = []
    for u, p in tiles:
        h0 = p * per_tile
        seg = stack(gcol[u], p) - gcum_t[u][SMALL_A + h0:SMALL_A + h0 + 1, :n]
        decay.append(jnp.exp(jnp.where(causal, seg, -jnp.inf)))
    lmat = [_dot_nt(stack(kb[u], p).astype(BF16), k16[i]) * jnp.where(diag, 0.0, decay[i])
            for i, (u, p) in enumerate(tiles)]
    qk = [(_dot_nt(stack(q[u], p).astype(BF16), k16[i]) * decay[i]).astype(BF16) for i, (u, p) in enumerate(tiles)]
    inv = [a.astype(BF16) for a in _unit_lower_inverses(lmat, ii, jj, t)]

    split = lambda x: (x[:, :DN_HEAD], x[:, DN_HEAD:])
    o_from, o_own, s_from, s_own, glast = ([[] for _ in units] for _ in range(5))
    aw = [_dot(inv[i], jnp.concatenate(
        [stack([kb[u][h] * egcol[u][h] for h in heads], p), stack([v[u][h] * bcol[u][h] for h in heads], p)],
        axis=1).astype(BF16)) for i, (u, p) in enumerate(tiles)]
    qaw = [_dot(qk[i], aw[i].astype(BF16)) for i in range(len(tiles))]
    pad_rows = -t % (2 * SUBLANES)
    pad = [jnp.zeros((pad_rows, DN_HEAD), F32)] if pad_rows else []
    for u in range(len(units)):
        for h in heads:
            p, j = divmod(h, per_tile)
            i = u * ntile + p
            sel = slice(j * t, (j + 1) * t)
            qkw, qku0 = split(qaw[i][sel])
            o_from[u].append(q[u][h] * egcol[u][h] - qkw)
            o_own[u].append(qku0)
            glast[u].append(gcum[u][t - 1:t, SMALL_A + h:SMALL_A + h + 1])
            kd = (k[u][h] * jnp.exp(glast[u][h] - gcol[u][h])).astype(BF16)
            kdw, kdu0 = split(_dot_tn(kd, aw[i][sel].astype(BF16)))
            s_from[u].append(kdw)
            s_own[u].append(kdu0)

    outs = []
    for u, (s, c) in enumerate(units):
        @_when(first_step if c == 0 else False)
        def _():
            s_scr[...] = s0_ref[s]

        for h in heads:
            state = s_scr[h]
            both = _dot(jnp.concatenate([s_from[u][h], o_from[u][h]] + pad, axis=0).astype(BF16), state.astype(BF16))
            outs.append(both[DN_HEAD:DN_HEAD + t] + o_own[u][h])
            s_scr[h] = jnp.exp(glast[u][h]) * state - both[:DN_HEAD] + s_own[u][h]

        @_when(last_step if c == chunks - 1 else False)
        def _():
            snew_ref[s] = s_scr[...]

    norm_w = nw_ref[...]
    osq = _row_sums_of_squares(outs)
    for u in range(len(units)):
        normed = []
        for h in heads:
            i = u * DN_HEADS + h
            normed.append(outs[i] * lax.rsqrt(osq[i] * (1.0 / DN_HEAD) + RMS_EPS) * norm_w
                          * _silu(zb[u][:, h * DN_HEAD:(h + 1) * DN_HEAD]))
        o_ref[rows(u), :] = jnp.concatenate(normed, axis=1).astype(BF16)


def _gdn(proj_b, small, consts, conv_state, s0, **where):
    t = where["t"]
    scratch = [pltpu.VMEM((SUBLANES, DN_CONV_CH), F32),
               pltpu.VMEM((DN_HEADS, DN_HEAD, DN_HEAD), F32)]
    step = functools.partial(_gdn_step_kernel, t=t, seqs=where["seqs"], chunks=where["chunks"])
    return _scan_call(None, f"gdn_t{t}", [proj_b, small], consts, [conv_state, s0], DN_V, scratch,
                      step_kernel=step, **where)


def _merge_kernel(xp_ref, xs_ref, yap_ref, yas_ref, obp_ref, obs_ref, eg_ref, eb_ref, wg_ref, wa_ref, wb_ref,
                  wo_ref, g_ref, b_ref, x1_ref, x1b_ref, *, prompt_tiles):
    is_prompt = pl.program_id(0) < prompt_tiles
    h = _layer_norm(jnp.where(is_prompt, xp_ref[...], xs_ref[...]), eg_ref[...], eb_ref[...])
    out_a = _dot(jnp.where(is_prompt, yap_ref[...], yas_ref[...]), wa_ref[...])
    out_b = _dot(jnp.where(is_prompt, obp_ref[...], obs_ref[...]), wb_ref[...])
    gt = _dot(h.astype(BF16), wg_ref[...])
    merged = _sigmoid(gt[:, :D_MODEL]) * out_a + _sigmoid(gt[:, D_MODEL:]) * out_b
    mix = _dot(merged.astype(BF16), wo_ref[...])
    x1 = _layer_norm(DEEPNORM_ALPHA * h + mix, g_ref[...], b_ref[...])
    x1_ref[...] = x1
    x1b_ref[...] = x1.astype(BF16)


def _merge(xp, xs, ya_p, ya_s, ob_p, ob_s, consts, tm):
    m = xp.shape[0] + xs.shape[0]
    n_p = xp.shape[0]
    tm = _row_tile(math.gcd(n_p, m - n_p), tm)
    in_prompt, in_sample = _group_maps(n_p // tm)
    row = lambda i: (i, 0)
    const = lambda i: (0, 0)
    return pl.pallas_call(
        functools.partial(_merge_kernel, prompt_tiles=n_p // tm),
        grid=(m // tm,),
        in_specs=[pl.BlockSpec((tm, D_MODEL), in_prompt), pl.BlockSpec((tm, D_MODEL), in_sample),
                  pl.BlockSpec((tm, SSM_INNER), in_prompt), pl.BlockSpec((tm, SSM_INNER), in_sample),
                  pl.BlockSpec((tm, DN_V), in_prompt), pl.BlockSpec((tm, DN_V), in_sample)]
        + [pl.BlockSpec(a.shape, const) for a in consts],
        out_specs=[pl.BlockSpec((tm, D_MODEL), row), pl.BlockSpec((tm, D_MODEL), row)],
        out_shape=[jax.ShapeDtypeStruct((m, D_MODEL), F32), jax.ShapeDtypeStruct((m, D_MODEL), BF16)],
        compiler_params=_params("parallel"),
        name="merge",
    )(xp, xs, ya_p, ya_s, ob_p, ob_s, *consts)


def _take_first_max(rest, index, limit, axis):
    best = jnp.max(rest, axis=axis, keepdims=True)
    first = jnp.min(jnp.where(rest == best, index, limit), axis=axis, keepdims=True)
    hit = index == first
    return hit, jnp.where(hit, -jnp.inf, rest)


def _router_gates_t(x1, xb, rw_hi, rw_lo, bias):
    tm = x1.shape[0]
    x_lo = (x1 - xb.astype(F32)).astype(BF16)
    logits = _dot_nt(rw_hi, xb) + (_dot_nt(rw_hi, x_lo) + _dot_nt(rw_lo, xb))
    scores = _sigmoid(logits)
    row = lax.broadcasted_iota(jnp.int32, (LANES, tm), 0)
    sel = jnp.where(row < N_EXPERTS, scores + bias, -jnp.inf)
    by_group = sel.reshape(LANES // EXPERTS_PER_GROUP, EXPERTS_PER_GROUP, tm)
    sub = lax.broadcasted_iota(jnp.int32, by_group.shape, 1)
    top1 = jnp.max(by_group, axis=1, keepdims=True)
    _, others = _take_first_max(by_group, sub, EXPERTS_PER_GROUP, 1)
    gscore = (top1 + jnp.max(others, axis=1, keepdims=True))[:N_EXPERT_GROUPS]
    gidx = lax.broadcasted_iota(jnp.int32, gscore.shape, 0)
    gkeep = jnp.zeros(gscore.shape, jnp.bool_)
    for _ in range(TOPK_GROUPS):
        hit, gscore = _take_first_max(gscore, gidx, N_EXPERT_GROUPS, 0)
        gkeep = gkeep | hit
    gkeep = jnp.broadcast_to(gkeep, (N_EXPERT_GROUPS, EXPERTS_PER_GROUP, tm)).reshape(N_EXPERTS, tm)
    rest = jnp.where(gkeep, sel[:N_EXPERTS], -jnp.inf)
    eidx = lax.broadcasted_iota(jnp.int32, (N_EXPERTS, tm), 0)
    keep = jnp.zeros((N_EXPERTS, tm), jnp.bool_)
    for _ in range(TOP_K):
        hit, rest = _take_first_max(rest, eidx, N_EXPERTS, 0)
        keep = keep | hit
    picked = jnp.where(keep, scores[:N_EXPERTS], 0.0)
    gates = picked / jnp.sum(picked, axis=0, keepdims=True) * ROUTED_SCALE
    return jnp.concatenate([gates, jnp.zeros((LANES - N_EXPERTS, tm), F32)], axis=0)


def _dense_router_kernel(x1_ref, x1b_ref, pp_ref, ps_ref, sw1_ref, sw3_ref, sw2_ref, pw_ref, pg_ref, rwh_ref,
                         rwl_ref, rb_ref, dense_ref, gates_ref, rank_ref, chosen_t_ref, rank_t_ref, count_ref,
                         *, prompt_tiles):
    xb = x1b_ref[...]
    hid = _silu(_dot(xb, sw1_ref[...])) * _dot(xb, sw3_ref[...])
    shared = _dot(hid.astype(BF16), sw2_ref[...])
    p = jnp.where(pl.program_id(0) < prompt_tiles, pp_ref[...], ps_ref[...])
    ple = _dot(p.astype(BF16), pw_ref[...]) * _sigmoid(_dot(xb, pg_ref[...]))
    x1 = x1_ref[...]
    dense_ref[...] = DEEPNORM_ALPHA * x1 + shared + ple
    gates_t = _router_gates_t(x1, xb, rwh_ref[...], rwl_ref[...], rb_ref[...])
    tm = x1.shape[0]
    chosen_t = (gates_t > 0.0).astype(BF16)
    ii = lax.broadcasted_iota(jnp.int32, (tm, tm), 0)
    jj = lax.broadcasted_iota(jnp.int32, (tm, tm), 1)
    rank_t = _dot(chosen_t, (ii < jj).astype(BF16))
    chosen_t_ref[...] = chosen_t
    rank_t_ref[...] = rank_t.astype(BF16)
    gates = gates_t.T
    rank = rank_t.T
    gates_ref[...] = gates.astype(BF16)
    rank_ref[...] = rank.astype(BF16)
    count = rank[tm - 1:tm, :] + (gates[tm - 1:tm, :] > 0.0).astype(F32)
    count_ref[0] = jnp.broadcast_to(count, (SUBLANES, LANES))


def _dense_router(x1, x1b, p_prompt, p_sample, consts):
    m = x1.shape[0]
    nblk = m // MOE_BLOCK
    prompt_tiles = p_prompt.shape[0] // MOE_BLOCK
    in_prompt, in_sample = _group_maps(prompt_tiles)
    row = lambda i: (i, 0)
    const = lambda i: (0, 0)
    return pl.pallas_call(
        functools.partial(_dense_router_kernel, prompt_tiles=prompt_tiles),
        grid=(nblk,),
        in_specs=[pl.BlockSpec((MOE_BLOCK, D_MODEL), row), pl.BlockSpec((MOE_BLOCK, D_MODEL), row),
                  pl.BlockSpec((MOE_BLOCK, PLE_DIM), in_prompt), pl.BlockSpec((MOE_BLOCK, PLE_DIM), in_sample)]
        + [pl.BlockSpec(a.shape, const) for a in consts],
        out_specs=[pl.BlockSpec((MOE_BLOCK, D_MODEL), row), pl.BlockSpec((MOE_BLOCK, LANES), row),
                   pl.BlockSpec((MOE_BLOCK, LANES), row), pl.BlockSpec((LANES, MOE_BLOCK), row),
                   pl.BlockSpec((LANES, MOE_BLOCK), row), pl.BlockSpec((1, SUBLANES, LANES), lambda i: (i, 0, 0))],
        out_shape=[jax.ShapeDtypeStruct((m, D_MODEL), F32), jax.ShapeDtypeStruct((m, LANES), BF16),
                   jax.ShapeDtypeStruct((m, LANES), BF16), jax.ShapeDtypeStruct((nblk * LANES, MOE_BLOCK), BF16),
                   jax.ShapeDtypeStruct((nblk * LANES, MOE_BLOCK), BF16),
                   jax.ShapeDtypeStruct((nblk, SUBLANES, LANES), F32)],
        compiler_params=_params("parallel"),
        name="dense_router",
    )(x1, x1b, p_prompt, p_sample, *consts)


def _moe_tables(counts, tm):
    nblk = counts.shape[0]
    nchunk_max = MOE_CAP // MOE_SEG
    padded = (counts + MOE_SEG - 1) // MOE_SEG * MOE_SEG
    start = jnp.cumsum(padded, axis=1) - padded
    used = jnp.sum(padded, axis=1)
    rows_e = jnp.sum(padded, axis=0)
    rows_e_t = (rows_e + tm - 1) // tm * tm
    base_e = jnp.cumsum(rows_e_t) - rows_e_t
    seg_row = base_e[None, :] + jnp.cumsum(padded, axis=0) - padded
    chunk0 = jnp.arange(nchunk_max, dtype=jnp.int32) * MOE_SEG
    owner = jnp.sum((start + padded)[:, None, :] <= chunk0[None, :, None], axis=-1)
    onehot = owner[..., None] == jnp.arange(N_EXPERTS, dtype=jnp.int32)
    dst = chunk0[None, :] + jnp.sum(jnp.where(onehot, (seg_row - start)[:, None, :], 0), axis=-1)
    n_tiles = _moe_max_tiles(nblk, tm)
    live = chunk0[None, :] < used[:, None]
    parity = (jnp.arange(nblk, dtype=jnp.int32) % 2)[:, None]
    spare = n_tiles * tm + parity * MOE_CAP + chunk0[None, :]
    src = jnp.where(live, dst, 0)
    dst = jnp.where(live, dst, spare)
    tile_row = jnp.arange(n_tiles, dtype=jnp.int32) * tm
    tiles_used = jnp.sum(rows_e_t) // tm
    tile_row = jnp.minimum(tile_row, (tiles_used - 1) * tm)
    tile_expert = jnp.minimum(jnp.sum((base_e + rows_e_t)[None, :] <= tile_row[:, None], axis=-1), N_EXPERTS - 1)
    bounds = jnp.zeros((nblk, SUBLANES, LANES), F32)
    bounds = bounds.at[:, 0, :N_EXPERTS].set(start.astype(F32)).at[:, 1, :N_EXPERTS].set(padded.astype(F32))
    i32 = lambda a: a.astype(jnp.int32)
    return dict(dst=i32(dst.reshape(-1)), src=i32(src.reshape(-1)), nchunk=i32(used // MOE_SEG),
                pad_row=i32(base_e + rows_e),
                pad_n=i32((rows_e_t - rows_e) // MOE_SEG), tile_expert=i32(tile_expert),
                tiles_used=i32(tiles_used.reshape(1)), bounds=bounds)


def _moe_max_tiles(nblk, tm):
    return (nblk * (MOE_BLOCK * TOP_K + N_EXPERTS * (MOE_SEG - 1)) + N_EXPERTS * (tm - MOE_SEG) + tm - 1) // tm


def _slot_bounds(bounds):
    return bounds[0:1, :], bounds[1:2, :]


def _moe_gather_kernel(dst_ref, nchunk_ref, pad_row_ref, pad_n_ref,
                       x_ref, chosen_t_ref, rank_t_ref, bounds_ref, rows_ref, buf, zbuf, sem, zsem):
    j = pl.program_id(0)
    nblk = pl.num_programs(0)
    slot = j % 2
    nchunk_max = MOE_CAP // MOE_SEG

    def chunk_copy(blk, q, sl):
        src = buf.at[sl, pl.ds(pl.multiple_of(q * MOE_SEG, MOE_SEG), MOE_SEG), :]
        row = pl.multiple_of(dst_ref[blk * nchunk_max + q], MOE_SEG)
        return pltpu.make_async_copy(src, rows_ref.at[pl.ds(row, MOE_SEG), :], sem.at[sl])

    def pad_copy(e, i):
        row = pl.multiple_of(pad_row_ref[e] + i * MOE_SEG, MOE_SEG)
        return pltpu.make_async_copy(zbuf, rows_ref.at[pl.ds(row, MOE_SEG), :], zsem)

    def sub_chunks(blk, sub, sl, action):
        for q in range(sub * (MOE_SUB // MOE_SEG), (sub + 1) * (MOE_SUB // MOE_SEG)):
            action(chunk_copy(blk, q, sl))

    def sub_used(blk, sub):
        return sub * MOE_SUB < nchunk_ref[blk] * MOE_SEG

    def for_pads(action):
        def per_expert(e, carry):
            def body(i, c):
                action(pad_copy(e, i))
                return c
            return lax.fori_loop(0, pad_n_ref[e], body, carry)
        lax.fori_loop(0, N_EXPERTS, per_expert, 0)

    @pl.when(j == 0)
    def _():
        zbuf[...] = jnp.zeros_like(zbuf)
        for_pads(lambda c: c.start())

    start, length = _slot_bounds(bounds_ref[0])
    chosen_t = chosen_t_ref[...]
    rank_t = rank_t_ref[...]
    x = x_ref[...]
    before = jnp.maximum(j - 2, 0)
    for sub in range(MOE_CAP // MOE_SUB):
        @pl.when((j >= 2) & sub_used(before, sub))
        def _():
            sub_chunks(before, sub, slot, lambda c: c.wait())

    for sub in range(MOE_CAP // MOE_SUB):
        @pl.when(sub_used(j, sub))
        def _():
            s = (sub * MOE_SUB + lax.broadcasted_iota(jnp.int32, (MOE_SUB, LANES), 0)).astype(F32)
            owner = (s >= start) & (s < start + length)
            within = s[:, 0:1] - jnp.sum(jnp.where(owner, start, 0.0), axis=1, keepdims=True)
            owner = owner.astype(BF16)
            pick = (_dot(owner, chosen_t) > 0.5) & (_dot(owner, rank_t) == within)
            buf[slot, sub * MOE_SUB:(sub + 1) * MOE_SUB, :] = _dot(pick.astype(BF16), x).astype(BF16)
            sub_chunks(j, sub, slot, lambda c: c.start())

    @pl.when(j == nblk - 1)
    def _():
        prev = jnp.maximum(j - 1, 0)
        for sub in range(MOE_CAP // MOE_SUB):
            @pl.when(sub_used(j, sub))
            def _():
                sub_chunks(j, sub, slot, lambda c: c.wait())

            @pl.when((j >= 1) & sub_used(prev, sub))
            def _():
                sub_chunks(prev, sub, 1 - slot, lambda c: c.wait())

        for_pads(lambda c: c.wait())


def _moe_gather(x1b, chosen_t, rank_t, tables, tm):
    m = x1b.shape[0]
    nblk = m // MOE_BLOCK
    rows = _moe_max_tiles(nblk, tm) * tm + 2 * MOE_CAP
    blk = lambda j, *_: (j, 0)
    grid_spec = pltpu.PrefetchScalarGridSpec(
        num_scalar_prefetch=4,
        grid=(nblk,),
        in_specs=[pl.BlockSpec((MOE_BLOCK, D_MODEL), blk), pl.BlockSpec((LANES, MOE_BLOCK), blk),
                  pl.BlockSpec((LANES, MOE_BLOCK), blk),
                  pl.BlockSpec((1, SUBLANES, LANES), lambda j, *_: (j, 0, 0))],
        out_specs=pl.BlockSpec(memory_space=pl.ANY),
        scratch_shapes=[pltpu.VMEM((2, MOE_CAP, D_MODEL), BF16), pltpu.VMEM((MOE_SEG, D_MODEL), BF16),
                        pltpu.SemaphoreType.DMA((2,)), pltpu.SemaphoreType.DMA(())],
    )
    return pl.pallas_call(
        _moe_gather_kernel,
        grid_spec=grid_spec,
        out_shape=jax.ShapeDtypeStruct((rows, D_MODEL), BF16),
        compiler_params=_params("arbitrary"),
        name="moe_gather",
    )(tables["dst"], tables["nchunk"], tables["pad_row"], tables["pad_n"], x1b, chosen_t, rank_t, tables["bounds"])


def _moe_ffn_kernel(tile_expert_ref, tiles_used_ref, x_ref, w1_ref, w3_ref, w2_ref, y_ref, w13_scr, w2_scr):
    i = pl.program_id(0)

    @pl.when(i < tiles_used_ref[0])
    def _():
        @pl.when((i == 0) | (tile_expert_ref[i] != tile_expert_ref[jnp.maximum(i - 1, 0)]))
        def _():
            w13_scr[:, :EXPERT_FF] = w1_ref[0].astype(BF16)
            w13_scr[:, EXPERT_FF:] = w3_ref[0].astype(BF16)
            w2_scr[...] = w2_ref[0].astype(BF16)

        up = _dot(x_ref[...], w13_scr[...])
        hid = _silu(up[:, :EXPERT_FF]) * up[:, EXPERT_FF:]
        y_ref[...] = _dot(hid.astype(BF16), w2_scr[...]).astype(BF16)


def _moe_ffn(rows, w1, w3, w2, tables, tm):
    n_tiles = tables["tile_expert"].shape[0]
    tile = lambda i, tile_expert, tiles_used: (jnp.minimum(i, tiles_used[0] - 1), 0)
    expert = lambda i, tile_expert, tiles_used: (tile_expert[i], 0, 0)
    grid_spec = pltpu.PrefetchScalarGridSpec(
        num_scalar_prefetch=2,
        grid=(n_tiles,),
        in_specs=[pl.BlockSpec((tm, D_MODEL), tile), pl.BlockSpec((1, D_MODEL, EXPERT_FF), expert),
                  pl.BlockSpec((1, D_MODEL, EXPERT_FF), expert), pl.BlockSpec((1, EXPERT_FF, D_MODEL), expert)],
        out_specs=pl.BlockSpec((tm, D_MODEL), tile),
        scratch_shapes=[pltpu.VMEM((D_MODEL, 2 * EXPERT_FF), BF16), pltpu.VMEM((EXPERT_FF, D_MODEL), BF16)],
    )
    return pl.pallas_call(
        _moe_ffn_kernel,
        grid_spec=grid_spec,
        out_shape=jax.ShapeDtypeStruct(rows.shape, BF16),
        compiler_params=_params("arbitrary"),
        name="moe_ffn",
    )(tables["tile_expert"], tables["tiles_used"], rows, w1, w3, w2)


def _moe_combine_kernel(dst_ref, nchunk_ref, dense_ref, gates_ref, rank_ref, bounds_ref, g_ref, b_ref, rows_ref,
                        op_ref, os_ref, acc, buf, sem, *, prompt_tiles):
    j = pl.program_id(0)
    nblk = pl.num_programs(0)
    slot = j % 2
    nchunk_max = MOE_CAP // MOE_SEG

    def chunk_copy(blk, q, sl):
        row = pl.multiple_of(dst_ref[blk * nchunk_max + q], MOE_SEG)
        dst = buf.at[sl, pl.ds(pl.multiple_of(q * MOE_SEG, MOE_SEG), MOE_SEG), :]
        return pltpu.make_async_copy(rows_ref.at[pl.ds(row, MOE_SEG), :], dst, sem.at[sl])

    def for_chunks(blk, sl, action, also=True):
        for sub in range(MOE_CAP // MOE_SUB):
            @pl.when(also & (sub * MOE_SUB < nchunk_ref[blk] * MOE_SEG))
            def _():
                for q in range(sub * (MOE_SUB // MOE_SEG), (sub + 1) * (MOE_SUB // MOE_SEG)):
                    action(chunk_copy(blk, q, sl))

    @pl.when(j == 0)
    def _():
        buf[...] = jnp.zeros_like(buf)
        for_chunks(0, 0, lambda c: c.start())

    for_chunks(jnp.minimum(j + 1, nblk - 1), 1 - slot, lambda c: c.start(), also=j + 1 < nblk)
    for_chunks(j, slot, lambda c: c.wait())

    bounds_t = jnp.concatenate([bounds_ref[0], jnp.zeros((LANES - SUBLANES, LANES), F32)], axis=0).T
    start, length = bounds_t[:, 0:1], bounds_t[:, 1:2]
    gates = gates_ref[...]
    rank = rank_ref[...]
    used = nchunk_ref[j] * MOE_SEG
    acc[...] = dense_ref[...]
    for sub in range(MOE_CAP // MOE_SUB):
        @pl.when(sub * MOE_SUB < used)
        def _():
            s = (sub * MOE_SUB + lax.broadcasted_iota(jnp.int32, (LANES, MOE_SUB), 1)).astype(F32)
            owner = (s >= start) & (s < start + length)
            within = s[0:1, :] - jnp.sum(jnp.where(owner, start, 0.0), axis=0, keepdims=True)
            owner = owner.astype(BF16)
            weight = jnp.where(_dot(rank, owner) == within, _dot(gates, owner), 0.0)
            acc[...] += _dot(weight.astype(BF16), buf[slot, sub * MOE_SUB:(sub + 1) * MOE_SUB, :])

    @pl.when(j < prompt_tiles)
    def _():
        op_ref[...] = _layer_norm(acc[...], g_ref[...], b_ref[...])

    @pl.when(j >= prompt_tiles)
    def _():
        os_ref[...] = _layer_norm(acc[...], g_ref[...], b_ref[...])


def _moe_combine(rows, dense, gates, rank, tables, g, b, n_prompt):
    m = dense.shape[0]
    nblk = m // MOE_BLOCK
    prompt_tiles = n_prompt // MOE_BLOCK
    in_prompt, in_sample = _group_maps(prompt_tiles)
    blk = lambda j, *_: (j, 0)
    const = lambda j, *_: (0, 0)
    grid_spec = pltpu.PrefetchScalarGridSpec(
        num_scalar_prefetch=2,
        grid=(nblk,),
        in_specs=[pl.BlockSpec((MOE_BLOCK, D_MODEL), blk), pl.BlockSpec((MOE_BLOCK, LANES), blk),
                  pl.BlockSpec((MOE_BLOCK, LANES), blk),
                  pl.BlockSpec((1, SUBLANES, LANES), lambda j, *_: (j, 0, 0)),
                  pl.BlockSpec((1, D_MODEL), const), pl.BlockSpec((1, D_MODEL), const),
                  pl.BlockSpec(memory_space=pl.ANY)],
        out_specs=[pl.BlockSpec((MOE_BLOCK, D_MODEL), in_prompt), pl.BlockSpec((MOE_BLOCK, D_MODEL), in_sample)],
        scratch_shapes=[pltpu.VMEM((MOE_BLOCK, D_MODEL), F32), pltpu.VMEM((2, MOE_CAP, D_MODEL), BF16),
                        pltpu.SemaphoreType.DMA((2,))],
    )
    return pl.pallas_call(
        functools.partial(_moe_combine_kernel, prompt_tiles=prompt_tiles),
        grid_spec=grid_spec,
        out_shape=[jax.ShapeDtypeStruct((n_prompt, D_MODEL), F32),
                   jax.ShapeDtypeStruct((m - n_prompt, D_MODEL), F32)],
        compiler_params=_params("arbitrary"),
        name="moe_combine",
    )(tables["src"], tables["nchunk"], dense, gates, rank, tables["bounds"], g, b, rows)


def _row(v):
    return v.reshape(1, -1).astype(F32)


def _scan_layout(batch, seq, row0):
    t = math.gcd(seq, CHUNK)
    nchunk = seq // t
    chunks = SCAN_CHUNKS if nchunk % SCAN_CHUNKS == 0 else 1
    seqs = SCAN_SEQS if (nchunk == 1 and batch % SCAN_SEQS == 0 and row0 % (SCAN_SEQS * t) == 0) else 1
    return dict(batch=batch, seq=seq, row0=row0, t=t, seqs=seqs, chunks=chunks)


def _lanes_at(v, start):
    return jnp.zeros((1, LANES), F32).at[0, start:start + v.shape[0]].set(v.astype(F32))


def kernel(x_prompt, x_sample, p_prompt, p_sample, state_ssm_conv, state_ssm, state_dn_conv, state_dn, emb_ln_g, emb_ln_b, w_in, conv_a_w, conv_a_b, ssm_dt_bias, ssm_a_log, ssm_d, ssm_norm_w, w_a, conv_b_w, dn_dt_bias, dn_a_log, dn_norm_w, w_b, w_o, ln1_g, ln1_b, router_w, router_bias, exp_w1, exp_w3, exp_w2, sh_w1, sh_w3, sh_w2, ple_w, ple_gate_w, ln2_g, ln2_b):
    bp, lp, _ = x_prompt.shape
    bs, ls, _ = x_sample.shape
    n_p = bp * lp
    n_s = bs * ls
    xp = x_prompt.reshape(n_p, D_MODEL)
    xs = x_sample.reshape(n_s, D_MODEL)

    w = w_in[0]
    o_z, o_xbc, o_dt = 0, SSM_INNER, SSM_INNER + SSM_CONV_CH
    o_qkv = o_dt + SSM_HEADS
    o_a = o_qkv + DN_CONV_CH
    o_b = o_a + DN_HEADS
    o_zb = o_b + DN_HEADS
    o_ga = o_zb + DN_V
    w_pa = w[:, o_z:o_dt].astype(BF16)
    w_pb = jnp.concatenate([w[:, o_qkv:o_a], w[:, o_zb:o_ga]], axis=1).astype(BF16)
    w_pg = w[:, o_ga:].astype(BF16)
    w_ps = jnp.zeros((D_MODEL, LANES), F32)
    w_ps = w_ps.at[:, SMALL_DT:SMALL_DT + SSM_HEADS].set(w[:, o_dt:o_qkv])
    w_ps = w_ps.at[:, SMALL_A:SMALL_A + 2 * DN_HEADS].set(w[:, o_a:o_zb]).astype(BF16)

    eg, eb = _row(emb_ln_g), _row(emb_ln_b)
    proj_a, small = _ln_matmul(xp, xs, eg, eb, w_pa, 1024, 2304, w_ps)
    proj_b = _ln_matmul(xp, xs, eg, eb, w_pb, 1024, 2048)

    ssd_consts = [conv_a_w[0], _row(conv_a_b[0]), _lanes_at(ssm_dt_bias[0], SMALL_DT),
                  _lanes_at(ssm_a_log[0], SMALL_DT), _row(jnp.repeat(ssm_d[0], SSM_HEAD_DIM)),
                  _row(ssm_norm_w[0])]
    gdn_consts = [conv_b_w[0], _lanes_at(dn_a_log[0], SMALL_A), _lanes_at(dn_dt_bias[0], SMALL_A),
                  _row(dn_norm_w[0])]
    zeros = lambda *s: jnp.zeros(s, F32)
    prompt = _scan_layout(bp, lp, 0)
    sample = _scan_layout(bs, ls, n_p)

    ya_p, pa_conv, pa_ssm = _ssd(proj_a, small, ssd_consts, zeros(bp, CONV_WIDTH - 1, SSM_CONV_CH),
                                 zeros(bp, SSM_HEADS, SSM_HEAD_DIM, SSM_STATE), **prompt)
    ya_s, sa_conv, sa_ssm = _ssd(proj_a, small, ssd_consts, state_ssm_conv[0], state_ssm[0], **sample)
    ob_p, pb_conv, pb_dn = _gdn(proj_b, small, gdn_consts, zeros(bp, CONV_WIDTH - 1, DN_CONV_CH),
                                zeros(bp, DN_HEADS, DN_HEAD, DN_HEAD), **prompt)
    ob_s, sb_conv, sb_dn = _gdn(proj_b, small, gdn_consts, state_dn_conv[0], state_dn[0], **sample)

    x1, x1b = _merge(
        xp, xs, ya_p, ya_s, ob_p, ob_s,
        [eg, eb, w_pg, w_a[0].astype(BF16), w_b[0].astype(BF16), w_o[0].astype(BF16), _row(ln1_g[0]), _row(ln1_b[0])],
        512)

    router_w_t = jnp.zeros((LANES, D_MODEL), F32).at[:N_EXPERTS].set(router_w[0].T)
    router_w_hi = router_w_t.astype(BF16)
    router_w_lo = (router_w_t - router_w_hi.astype(F32)).astype(BF16)
    router_b = jnp.broadcast_to(_lanes_at(router_bias[0], 0).reshape(LANES, 1), (LANES, MOE_BLOCK))
    dense, gates, rank, chosen_t, rank_t, counts = _dense_router(
        x1, x1b, p_prompt[0].reshape(n_p, PLE_DIM), p_sample[0].reshape(n_s, PLE_DIM),
        [sh_w1[0].astype(BF16), sh_w3[0].astype(BF16), sh_w2[0].astype(BF16), ple_w[0].astype(BF16),
         ple_gate_w[0].astype(BF16), router_w_hi, router_w_lo, router_b])

    tables = _moe_tables(counts[:, 0, :N_EXPERTS].astype(jnp.int32), MOE_TM)
    sorted_rows = _moe_gather(x1b, chosen_t, rank_t, tables, MOE_TM)
    expert_out = _moe_ffn(sorted_rows, exp_w1[0], exp_w3[0], exp_w2[0], tables, MOE_TM)
    out_p, out_s = _moe_combine(expert_out, dense, gates, rank, tables, _row(ln2_g[0]), _row(ln2_b[0]), n_p)

    return (out_p.reshape(bp, lp, D_MODEL), out_s.reshape(bs, ls, D_MODEL),
            pa_conv[None], pa_ssm[None], pb_conv[None], pb_dn[None],
            sa_conv[None], sa_ssm[None], sb_conv[None], sb_dn[None])
```

```python
import functools
import math

import jax
import jax.numpy as jnp
from jax import lax
from jax.experimental import pallas as pl
from jax.experimental.pallas import tpu as pltpu

F32 = jnp.float32
BF16 = jnp.bfloat16
HIGHEST = lax.Precision.HIGHEST

D_MODEL = 1024
SSM_INNER = 2048
SSM_HEAD_DIM = 64
SSM_HEADS = 32
SSM_GROUPS = 2
SSM_HEADS_PER_GROUP = 16
SSM_STATE = 128
SSM_CONV_CH = 2560
DN_HEADS = 8
DN_HEAD = 128
DN_QK = 1024
DN_V = 1024
DN_CONV_CH = 3072
CONV_WIDTH = 4
CHUNK = 64
N_EXPERTS = 64
TOP_K = 8
N_EXPERT_GROUPS = 8
EXPERTS_PER_GROUP = 8
TOPK_GROUPS = 4
EXPERT_FF = 256
SHARED_FF = 256
ROUTED_SCALE = 2.5
PLE_DIM = 256
LN_EPS = 1e-5
RMS_EPS = 1e-6
L2_EPS = 1e-6
DEEPNORM_ALPHA = 2.0 ** 0.25

LANES = 128
SUBLANES = 8
VMEM_LIMIT = 56 * 1024 * 1024
SMALL_DT = 0
SMALL_A = 32
SMALL_B = 40
CONV_PAD = 8
SCAN_CHUNKS = 8
SCAN_SEQS = 4
MOE_BLOCK = 256
MOE_SEG = 16
MOE_SUB = 512
MOE_CAP = -(-(MOE_BLOCK * 8 + 64 * (MOE_SEG - 1)) // MOE_SUB) * MOE_SUB
MOE_TM = 512
MOE_SUB_ALWAYS = -(-(MOE_BLOCK * 8 + 64 * MOE_SEG // 2) // MOE_SUB)


def _sigmoid(x):
    return 1.0 / (1.0 + jnp.exp(-x))


def _silu(x):
    return x * _sigmoid(x)


def _softplus(x):
    return jnp.maximum(x, 0.0) + jnp.log(1.0 + jnp.exp(-jnp.abs(x)))


def _layer_norm(x, g, b):
    mu = jnp.mean(x, axis=-1, keepdims=True)
    xc = x - mu
    var = jnp.mean(xc * xc, axis=-1, keepdims=True)
    return xc * lax.rsqrt(var + LN_EPS) * g + b


def _dot(a, b):
    return jnp.dot(a, b, preferred_element_type=F32)


def _dot_nt(a, b):
    return lax.dot_general(a, b, (((1,), (1,)), ((), ())), preferred_element_type=F32)


def _dot_tn(a, b):
    return lax.dot_general(a, b, (((0,), (0,)), ((), ())), preferred_element_type=F32)


def _dot_f32(a, b):
    return jnp.dot(a, b, precision=HIGHEST, preferred_element_type=F32)


def _params(*sem):
    return pltpu.CompilerParams(dimension_semantics=sem, vmem_limit_bytes=VMEM_LIMIT)


def _when(cond):
    if cond is True:
        return lambda fn: fn()
    if cond is False:
        return lambda fn: None
    return pl.when(cond)


def _row_tile(m, preferred):
    return max(d for d in range(SUBLANES, min(m, preferred) + 1, SUBLANES) if m % d == 0)


def _group_maps(prompt_tiles):
    in_prompt = lambda i, *_: (jnp.minimum(i, prompt_tiles - 1), 0)
    in_sample = lambda i, *_: (jnp.maximum(i - prompt_tiles, 0), 0)
    return in_prompt, in_sample


def _ln_matmul_kernel(xp_ref, xs_ref, g_ref, b_ref, w_ref, *rest, prompt_tiles, narrow):
    if narrow:
        wn_ref, o_ref, on_ref, h_scr = rest
    else:
        o_ref, h_scr = rest

    @pl.when(pl.program_id(1) == 0)
    def _():
        x = jnp.where(pl.program_id(0) < prompt_tiles, xp_ref[...], xs_ref[...])
        h_scr[...] = _layer_norm(x, g_ref[...], b_ref[...]).astype(BF16)
        if narrow:
            on_ref[...] = _dot(h_scr[...], wn_ref[...])

    o_ref[...] = _dot(h_scr[...], w_ref[...]).astype(o_ref.dtype)


def _ln_matmul(xp, xs, g, b, w, tm, tn, w_narrow=None):
    k = xp.shape[1]
    m = xp.shape[0] + xs.shape[0]
    n = w.shape[1]
    tm = _row_tile(math.gcd(xp.shape[0], xs.shape[0]), tm)
    prompt_tiles = xp.shape[0] // tm
    in_prompt, in_sample = _group_maps(prompt_tiles)
    narrow = w_narrow is not None
    const = lambda i, j: (0, 0)
    row = lambda i, j: (i, 0)
    in_specs = [pl.BlockSpec((tm, k), in_prompt), pl.BlockSpec((tm, k), in_sample),
                pl.BlockSpec((1, k), const), pl.BlockSpec((1, k), const), pl.BlockSpec((k, tn), lambda i, j: (0, j))]
    out_specs = [pl.BlockSpec((tm, tn), lambda i, j: (i, j))]
    out_shape = [jax.ShapeDtypeStruct((m, n), BF16)]
    args = [xp, xs, g, b, w]
    if narrow:
        in_specs.append(pl.BlockSpec((k, LANES), const))
        out_specs.append(pl.BlockSpec((tm, LANES), row))
        out_shape.append(jax.ShapeDtypeStruct((m, LANES), F32))
        args.append(w_narrow)
    out = pl.pallas_call(
        functools.partial(_ln_matmul_kernel, prompt_tiles=prompt_tiles, narrow=narrow),
        grid=(m // tm, n // tn),
        in_specs=in_specs,
        out_specs=out_specs,
        out_shape=out_shape,
        scratch_shapes=[pltpu.VMEM((tm, k), BF16)],
        compiler_params=_params("parallel", "arbitrary"),
        name="ln_matmul",
    )(*args)
    return out if narrow else out[0]


def _load_conv_state(cbuf, state):
    taps = CONV_WIDTH - 1
    cbuf[0:SUBLANES - taps, :] = jnp.zeros((SUBLANES - taps, cbuf.shape[1]), F32)
    cbuf[SUBLANES - taps:SUBLANES, :] = state


def _causal_conv(cbuf, x16, cw, t):
    ch = x16.shape[1]
    prev = cbuf[...]
    taps = CONV_WIDTH - 1
    if t <= SUBLANES:
        x = x16.astype(F32)
        ext = jnp.concatenate([prev, x], axis=0)
        lo = SUBLANES - taps
        y = ext[lo:lo + t] * cw[0:1]
        y = y + ext[lo + 1:lo + 1 + t] * cw[1:2]
        y = y + ext[lo + 2:lo + 2 + t] * cw[2:3]
        y = y + x * cw[3:4]
        cbuf[...] = ext[t:t + SUBLANES]
        return y, ext[SUBLANES + t - taps:SUBLANES + t]
    hi = prev.astype(BF16).astype(F32)
    mid = (prev - hi).astype(BF16).astype(F32)
    low = ((prev - hi) - mid).astype(BF16).astype(F32)
    x = x16.astype(F32)
    pack = 2 * SUBLANES
    nhead = 4 * SUBLANES if t % pack == 0 else 5 * SUBLANES
    head = jnp.concatenate([hi, mid, low, jnp.zeros((nhead - 3 * SUBLANES, ch), F32)], axis=0)
    if t % pack == 0:
        ext = jnp.concatenate([head.astype(BF16), x16], axis=0)
    else:
        ext = jnp.concatenate([head, x], axis=0).astype(BF16)
    out_row = lax.broadcasted_iota(jnp.int32, (taps * t, nhead + t), 0)
    col = lax.broadcasted_iota(jnp.int32, (taps * t, nhead + t), 1)
    log_t = t.bit_length() - 1
    shift = lax.shift_right_logical(out_row, log_t) + 1
    src = (out_row & (t - 1)) - shift
    in_x = (src >= 0) & (col == nhead + src)
    in_prev = (src < 0) & (col < 3 * SUBLANES) & ((col & (SUBLANES - 1)) == SUBLANES + src)
    shifted = _dot((in_x | in_prev).astype(BF16), ext)
    y = shifted[2 * t:3 * t] * cw[0:1]
    y = y + shifted[t:2 * t] * cw[1:2]
    y = y + shifted[0:t] * cw[2:3]
    y = y + x * cw[3:4]
    cbuf[...] = x[t - SUBLANES:t]
    return y, x[t - taps:t]


def _lane_expand(v, h0, count, width):
    t = v.shape[0]
    n = count * width
    out = jnp.broadcast_to(v[:, h0:h0 + 1], (t, n))
    if count > 1:
        lane = lax.broadcasted_iota(jnp.int32, (t, n), 1)
        for i in range(1, count):
            out = jnp.where(lane >= i * width, jnp.broadcast_to(v[:, h0 + i:h0 + i + 1], (t, n)), out)
    return out


def _grouped_transpose(v, per_tile):
    t = v.shape[0]
    blocks = [v if r == 0 else pltpu.roll(v, LANES - r, axis=1) for r in range(per_tile)]
    if per_tile * t < LANES:
        blocks.append(jnp.zeros((LANES - per_tile * t, LANES), F32))
    return jnp.concatenate(blocks, axis=0).T


def _same_block(i, j, size):
    shift = size.bit_length() - 1
    return lax.shift_right_logical(i, shift) == lax.shift_right_logical(j, shift)


def _unit_lower_inverses(lmats, ii, jj, t):
    mm = lambda a, b: _dot(a.astype(BF16), b.astype(BF16))
    base = min(16, t)
    in_base = _same_block(ii, jj, base)
    eye = (ii == jj).astype(F32)
    power = [jnp.where(in_base, l, 0.0) for l in lmats]
    inv = [eye - p for p in power]
    span = 2
    while span < base:
        power = [mm(p, p) for p in power]
        inv = [a + mm(a, p) for a, p in zip(inv, power)]
        span *= 2
    size = base
    while size < t:
        link = _same_block(ii, jj, 2 * size) & jnp.logical_not(_same_block(ii, jj, size))
        cross = [mm(a, jnp.where(link, l, 0.0)) for a, l in zip(inv, lmats)]
        inv = [a - mm(c, a) for a, c in zip(inv, cross)]
        size *= 2
    return inv


def _row_sums_of_squares(blocks):
    t = blocks[0].shape[0]
    sq = jnp.concatenate([b * b for b in blocks], axis=0)
    hi = sq.astype(BF16)
    lo = (sq - hi.astype(F32)).astype(BF16)
    ones = jnp.ones((LANES, LANES), BF16)
    sums = _dot(hi, ones) + _dot(lo, ones)
    return [sums[i * t:(i + 1) * t] for i in range(len(blocks))]


def _cumsum_rows(v, t):
    ii = lax.broadcasted_iota(jnp.int32, (t, t), 0)
    jj = lax.broadcasted_iota(jnp.int32, (t, t), 1)
    return _dot_f32((jj <= ii).astype(F32), v)


def _ssd_kernel(pa_ref, sm_ref, cw_ref, cb_ref, dtb_ref, alog_ref, dskip_ref, nw_ref, cst_ref, h0_ref,
                y_ref, cnew_ref, hnew_ref, cbuf, s_scr, *, t, first, last):
    per_tile = LANES // t
    width = per_tile * SSM_HEAD_DIM

    @_when(first)
    def _():
        _load_conv_state(cbuf, cst_ref[0])
        s_scr[...] = h0_ref[0]

    z = pa_ref[:, :SSM_INNER].astype(F32)
    conv, tail = _causal_conv(cbuf, pa_ref[:, SSM_INNER:], cw_ref[...], t)

    @_when(last)
    def _():
        cnew_ref[0] = tail

    xbc = _silu(conv + cb_ref[...])
    xs = xbc[:, :SSM_INNER]
    bm = xbc[:, SSM_INNER:SSM_INNER + SSM_GROUPS * SSM_STATE]
    cm = xbc[:, SSM_INNER + SSM_GROUPS * SSM_STATE:]

    dt = _softplus(sm_ref[...] + dtb_ref[...])
    da = dt * (-jnp.exp(alog_ref[...]))
    cum = _cumsum_rows(da, t)
    cum_t = _grouped_transpose(cum, per_tile)
    dt_t = _grouped_transpose(dt, per_tile)

    row = lax.broadcasted_iota(jnp.int32, (t, LANES), 0)
    lane = lax.broadcasted_iota(jnp.int32, (t, LANES), 1)
    causal = (lane % t) <= row
    brow = lax.broadcasted_iota(jnp.int32, (LANES, width), 0)
    bcol = lax.broadcasted_iota(jnp.int32, (LANES, width), 1)
    blockdiag = (brow // t) == (bcol // SSM_HEAD_DIM)

    ys = []
    for g in range(SSM_GROUPS):
        bg = bm[:, g * SSM_STATE:(g + 1) * SSM_STATE]
        cg = cm[:, g * SSM_STATE:(g + 1) * SSM_STATE].astype(BF16)
        cb = _dot_nt(cg, jnp.concatenate([bg] * per_tile, axis=0).astype(BF16))
        bg = bg.astype(BF16)
        for q in range(SSM_HEADS_PER_GROUP // per_tile):
            h0 = g * SSM_HEADS_PER_GROUP + q * per_tile
            ccol = _lane_expand(cum, h0, per_tile, t)
            seg = ccol - cum_t[h0:h0 + 1, :]
            decay = jnp.exp(jnp.where(causal, seg, -jnp.inf))
            wts = (cb * decay * dt_t[h0:h0 + 1, :]).astype(BF16)
            xt = xs[:, h0 * SSM_HEAD_DIM:h0 * SSM_HEAD_DIM + width]
            xbd = jnp.where(blockdiag, jnp.concatenate([xt] * per_tile, axis=0), 0.0).astype(BF16)
            y_intra = _dot(wts, xbd)
            st = s_scr[h0:h0 + per_tile].reshape(width, SSM_STATE)
            cum_w = ccol if width == LANES else _lane_expand(cum, h0, per_tile, SSM_HEAD_DIM)
            y_state = _dot_nt(cg, st.astype(BF16)) * jnp.exp(cum_w)
            ys.append(y_intra + y_state)
            wend = jnp.exp(cum_w[t - 1:t, :] - cum_w) * _lane_expand(dt, h0, per_tile, SSM_HEAD_DIM)
            ds = _dot_tn((xt * wend).astype(BF16), bg)
            for i in range(per_tile):
                h = h0 + i
                s_scr[h] = jnp.exp(cum[t - 1:t, h:h + 1]) * s_scr[h] + ds[i * SSM_HEAD_DIM:(i + 1) * SSM_HEAD_DIM]

    y = jnp.concatenate(ys, axis=1) + dskip_ref[...] * xs
    y = y * _silu(z)
    half = SSM_INNER // SSM_GROUPS
    normed = []
    for g in range(SSM_GROUPS):
        yg = y[:, g * half:(g + 1) * half]
        normed.append(yg * lax.rsqrt(jnp.mean(yg * yg, axis=-1, keepdims=True) + RMS_EPS))
    y_ref[...] = (jnp.concatenate(normed, axis=1) * nw_ref[...]).astype(BF16)

    @_when(last)
    def _():
        hnew_ref[0] = s_scr[...]


def _scan_step_kernel(chunk_kernel, n_tok, n_const, n_state, seqs, chunks, t):
    def step(*refs):
        tok = refs[:n_tok]
        const = refs[n_tok:n_tok + n_const]
        st_in = refs[n_tok + n_const:n_tok + n_const + n_state]
        y_ref = refs[n_tok + n_const + n_state]
        st_out = refs[n_tok + n_const + n_state + 1:n_tok + n_const + 2 * n_state + 1]
        scratch = refs[n_tok + n_const + 2 * n_state + 1:]
        first_step = pl.program_id(1) == 0
        last_step = pl.program_id(1) == pl.num_programs(1) - 1
        for s in range(seqs):
            for c in range(chunks):
                rows = pl.ds((s * chunks + c) * t, t)
                chunk_kernel(*[r.at[rows, :] for r in tok], *const, *[r.at[pl.ds(s, 1)] for r in st_in],
                             y_ref.at[rows, :], *[r.at[pl.ds(s, 1)] for r in st_out], *scratch,
                             first=first_step if c == 0 else False,
                             last=last_step if c == chunks - 1 else False)
    return step


def _scan_call(kernel, name, tok_in, const_in, state_in, y_width, scratch, *, batch, seq, row0, t, seqs, chunks,
               step_kernel=None):
    nstep = seq // (t * chunks)
    rows = seqs * chunks * t
    blk0 = row0 // rows

    def per_batch(a):
        zeros = (0,) * (a.ndim - 1)
        return pl.BlockSpec((seqs,) + a.shape[1:], lambda b, c: (b,) + zeros)

    if step_kernel is None:
        step_kernel = _scan_step_kernel(kernel, len(tok_in), len(const_in), len(state_in), seqs, chunks, t)
    return pl.pallas_call(
        step_kernel,
        grid=(batch // seqs, nstep),
        in_specs=[pl.BlockSpec((rows, a.shape[1]), lambda b, c: (blk0 + b * nstep + c, 0)) for a in tok_in]
        + [pl.BlockSpec(a.shape, lambda b, c: (0, 0)) for a in const_in]
        + [per_batch(a) for a in state_in],
        out_specs=[pl.BlockSpec((rows, y_width), lambda b, c: (b * nstep + c, 0))]
        + [per_batch(a) for a in state_in],
        out_shape=[jax.ShapeDtypeStruct((batch * seq, y_width), BF16)]
        + [jax.ShapeDtypeStruct(a.shape, F32) for a in state_in],
        scratch_shapes=scratch,
        compiler_params=_params("parallel", "arbitrary"),
        name=name,
    )(*tok_in, *const_in, *state_in)


def _ssd(proj_a, small, consts, conv_state, h0, **where):
    t = where["t"]
    scratch = [pltpu.VMEM((SUBLANES, SSM_CONV_CH), F32),
               pltpu.VMEM((SSM_HEADS, SSM_HEAD_DIM, SSM_STATE), F32)]
    return _scan_call(functools.partial(_ssd_kernel, t=t), f"ssd_t{t}", [proj_a, small], consts,
                      [conv_state, h0], SSM_INNER, scratch, **where)


def _gdn_step_kernel(pb_ref, sm_ref, cw_ref, alog_ref, dtb_ref, nw_ref, cst_ref, s0_ref,
                     o_ref, cnew_ref, snew_ref, cbuf, s_scr, *, t, seqs, chunks):
    first_step = pl.program_id(1) == 0
    last_step = pl.program_id(1) == pl.num_programs(1) - 1
    units = [(s, c) for s in range(seqs) for c in range(chunks)]
    rows = lambda u: pl.ds(u * t, t)
    per_tile = min(LANES // t, DN_HEADS)
    n = per_tile * t
    ntile = DN_HEADS // per_tile
    heads = range(DN_HEADS)
    ii = lax.broadcasted_iota(jnp.int32, (n, n), 0)
    jj = lax.broadcasted_iota(jnp.int32, (n, n), 1)
    causal = _same_block(ii, jj, t) & (jj <= ii)
    diag = ii == jj
    stack = lambda xs, p: jnp.concatenate(xs[p * per_tile:(p + 1) * per_tile], axis=0)

    qkv, zb, gcum, beta = [], [], [], []
    for u, (s, c) in enumerate(units):
        @_when(first_step if c == 0 else False)
        def _():
            _load_conv_state(cbuf, cst_ref[s])

        conv, tail = _causal_conv(cbuf, pb_ref[rows(u), :DN_CONV_CH], cw_ref[...], t)

        @_when(last_step if c == chunks - 1 else False)
        def _():
            cnew_ref[s] = tail

        qkv.append(_silu(conv))
        zb.append(pb_ref[rows(u), DN_CONV_CH:].astype(F32))
        sm = sm_ref[rows(u), :]
        beta.append(_sigmoid(sm))
        gate = -jnp.exp(alog_ref[...]) * _softplus(sm + dtb_ref[...])
        gcum.append(_cumsum_rows(gate, t))
    gcum_t = [_grouped_transpose(g, per_tile) for g in gcum]

    sumsq = _row_sums_of_squares([x[:, i * DN_HEAD:(i + 1) * DN_HEAD] for x in qkv for i in range(2 * DN_HEADS)])
    q, k, v, gcol, bcol, kb, egcol = ([[] for _ in units] for _ in range(7))
    for u in range(len(units)):
        for h in heads:
            lo = h * DN_HEAD
            ss = sumsq[u * 2 * DN_HEADS:(u + 1) * 2 * DN_HEADS]
            q[u].append(qkv[u][:, lo:lo + DN_HEAD] * lax.rsqrt(ss[h] + L2_EPS) * (DN_HEAD ** -0.5))
            k[u].append(qkv[u][:, DN_QK + lo:DN_QK + lo + DN_HEAD] * lax.rsqrt(ss[DN_HEADS + h] + L2_EPS))
            v[u].append(qkv[u][:, 2 * DN_QK + lo:2 * DN_QK + lo + DN_HEAD])
            gcol[u].append(gcum[u][:, SMALL_A + h:SMALL_A + h + 1])
            bcol[u].append(beta[u][:, SMALL_B + h:SMALL_B + h + 1])
            kb[u].append(k[u][h] * bcol[u][h])
            egcol[u].append(jnp.exp(gcol[u][h]))

    tiles = [(u, p) for u in range(len(units)) for p in range(ntile)]
    k16 = [stack(k[u], p).astype(BF16) for u, p in tiles]
    decay = []
    for u, p in tiles:
        h0 = p * per_tile
        seg = stack(gcol[u], p) - gcum_t[u][SMALL_A + h0:SMALL_A + h0 + 1, :n]
        decay.append(jnp.exp(jnp.where(causal, seg, -jnp.inf)))
    lmat = [_dot_nt(stack(kb[u], p).astype(BF16), k16[i]) * jnp.where(diag, 0.0, decay[i])
            for i, (u, p) in enumerate(tiles)]
    qk = [(_dot_nt(stack(q[u], p).astype(BF16), k16[i]) * decay[i]).astype(BF16) for i, (u, p) in enumerate(tiles)]
    inv = [a.astype(BF16) for a in _unit_lower_inverses(lmat, ii, jj, t)]

    split = lambda x: (x[:, :DN_HEAD], x[:, DN_HEAD:])
    o_from, o_own, s_from, s_own, glast = ([[] for _ in units] for _ in range(5))
    aw = [_dot(inv[i], jnp.concatenate(
        [stack([kb[u][h] * egcol[u][h] for h in heads], p), stack([v[u][h] * bcol[u][h] for h in heads], p)],
        axis=1).astype(BF16)) for i, (u, p) in enumerate(tiles)]
    qaw = [_dot(qk[i], aw[i].astype(BF16)) for i in range(len(tiles))]
    pad_rows = -t % (2 * SUBLANES)
    pad = [jnp.zeros((pad_rows, DN_HEAD), F32)] if pad_rows else []
    for u in range(len(units)):
        for h in heads:
            p, j = divmod(h, per_tile)
            i = u * ntile + p
            sel = slice(j * t, (j + 1) * t)
            qkw, qku0 = split(qaw[i][sel])
            o_from[u].append(q[u][h] * egcol[u][h] - qkw)
            o_own[u].append(qku0)
            glast[u].append(gcum[u][t - 1:t, SMALL_A + h:SMALL_A + h + 1])
            kd = (k[u][h] * jnp.exp(glast[u][h] - gcol[u][h])).astype(BF16)
            kdw, kdu0 = split(_dot_tn(kd, aw[i][sel].astype(BF16)))
            s_from[u].append(kdw)
            s_own[u].append(kdu0)

    outs = []
    for u, (s, c) in enumerate(units):
        @_when(first_step if c == 0 else False)
        def _():
            s_scr[...] = s0_ref[s]

        for h in heads:
            state = s_scr[h]
            both = _dot(jnp.concatenate([s_from[u][h], o_from[u][h]] + pad, axis=0).astype(BF16), state.astype(BF16))
            outs.append(both[DN_HEAD:DN_HEAD + t] + o_own[u][h])
            s_scr[h] = jnp.exp(glast[u][h]) * state - both[:DN_HEAD] + s_own[u][h]

        @_when(last_step if c == chunks - 1 else False)
        def _():
            snew_ref[s] = s_scr[...]

    norm_w = nw_ref[...]
    osq = _row_sums_of_squares(outs)
    for u in range(len(units)):
        normed = []
        for h in heads:
            i = u * DN_HEADS + h
            normed.append(outs[i] * lax.rsqrt(osq[i] * (1.0 / DN_HEAD) + RMS_EPS) * norm_w
                          * _silu(zb[u][:, h * DN_HEAD:(h + 1) * DN_HEAD]))
        o_ref[rows(u), :] = jnp.concatenate(normed, axis=1).astype(BF16)


def _gdn(proj_b, small, consts, conv_state, s0, **where):
    t = where["t"]
    scratch = [pltpu.VMEM((SUBLANES, DN_CONV_CH), F32),
               pltpu.VMEM((DN_HEADS, DN_HEAD, DN_HEAD), F32)]
    step = functools.partial(_gdn_step_kernel, t=t, seqs=where["seqs"], chunks=where["chunks"])
    return _scan_call(None, f"gdn_t{t}", [proj_b, small], consts, [conv_state, s0], DN_V, scratch,
                      step_kernel=step, **where)


def _merge_kernel(xp_ref, xs_ref, yap_ref, yas_ref, obp_ref, obs_ref, eg_ref, eb_ref, wg_ref, wa_ref, wb_ref,
                  wo_ref, g_ref, b_ref, x1_ref, x1b_ref, *, prompt_tiles):
    is_prompt = pl.program_id(0) < prompt_tiles
    h = _layer_norm(jnp.where(is_prompt, xp_ref[...], xs_ref[...]), eg_ref[...], eb_ref[...])
    out_a = _dot(jnp.where(is_prompt, yap_ref[...], yas_ref[...]), wa_ref[...])
    out_b = _dot(jnp.where(is_prompt, obp_ref[...], obs_ref[...]), wb_ref[...])
    gt = _dot(h.astype(BF16), wg_ref[...])
    merged = _sigmoid(gt[:, :D_MODEL]) * out_a + _sigmoid(gt[:, D_MODEL:]) * out_b
    mix = _dot(merged.astype(BF16), wo_ref[...])
    x1 = _layer_norm(DEEPNORM_ALPHA * h + mix, g_ref[...], b_ref[...])
    x1_ref[...] = x1
    x1b_ref[...] = x1.astype(BF16)


def _merge(xp, xs, ya_p, ya_s, ob_p, ob_s, consts, tm):
    m = xp.shape[0] + xs.shape[0]
    n_p = xp.shape[0]
    tm = _row_tile(math.gcd(n_p, m - n_p), tm)
    in_prompt, in_sample = _group_maps(n_p // tm)
    row = lambda i: (i, 0)
    const = lambda i: (0, 0)
    return pl.pallas_call(
        functools.partial(_merge_kernel, prompt_tiles=n_p // tm),
        grid=(m // tm,),
        in_specs=[pl.BlockSpec((tm, D_MODEL), in_prompt), pl.BlockSpec((tm, D_MODEL), in_sample),
                  pl.BlockSpec((tm, SSM_INNER), in_prompt), pl.BlockSpec((tm, SSM_INNER), in_sample),
                  pl.BlockSpec((tm, DN_V), in_prompt), pl.BlockSpec((tm, DN_V), in_sample)]
        + [pl.BlockSpec(a.shape, const) for a in consts],
        out_specs=[pl.BlockSpec((tm, D_MODEL), row), pl.BlockSpec((tm, D_MODEL), row)],
        out_shape=[jax.ShapeDtypeStruct((m, D_MODEL), F32), jax.ShapeDtypeStruct((m, D_MODEL), BF16)],
        compiler_params=_params("parallel"),
        name="merge",
    )(xp, xs, ya_p, ya_s, ob_p, ob_s, *consts)


def _take_first_max(rest, index, limit, axis):
    best = jnp.max(rest, axis=axis, keepdims=True)
    first = jnp.min(jnp.where(rest == best, index, limit), axis=axis, keepdims=True)
    hit = index == first
    return hit, jnp.where(hit, -jnp.inf, rest)


def _router_gates_t(x1, xb, rw_hi, rw_lo, bias):
    tm = x1.shape[0]
    x_lo = (x1 - xb.astype(F32)).astype(BF16)
    logits = _dot_nt(rw_hi, xb) + (_dot_nt(rw_hi, x_lo) + _dot_nt(rw_lo, xb))
    scores = _sigmoid(logits)
    row = lax.broadcasted_iota(jnp.int32, (LANES, tm), 0)
    sel = jnp.where(row < N_EXPERTS, scores + bias, -jnp.inf)
    by_group = sel.reshape(LANES // EXPERTS_PER_GROUP, EXPERTS_PER_GROUP, tm)
    sub = lax.broadcasted_iota(jnp.int32, by_group.shape, 1)
    top1 = jnp.max(by_group, axis=1, keepdims=True)
    _, others = _take_first_max(by_group, sub, EXPERTS_PER_GROUP, 1)
    gscore = (top1 + jnp.max(others, axis=1, keepdims=True))[:N_EXPERT_GROUPS]
    gidx = lax.broadcasted_iota(jnp.int32, gscore.shape, 0)
    gkeep = jnp.zeros(gscore.shape, jnp.bool_)
    for _ in range(TOPK_GROUPS):
        hit, gscore = _take_first_max(gscore, gidx, N_EXPERT_GROUPS, 0)
        gkeep = gkeep | hit
    gkeep = jnp.broadcast_to(gkeep, (N_EXPERT_GROUPS, EXPERTS_PER_GROUP, tm)).reshape(N_EXPERTS, tm)
    rest = jnp.where(gkeep, sel[:N_EXPERTS], -jnp.inf)
    eidx = lax.broadcasted_iota(jnp.int32, (N_EXPERTS, tm), 0)
    keep = jnp.zeros((N_EXPERTS, tm), jnp.bool_)
    for _ in range(TOP_K):
        hit, rest = _take_first_max(rest, eidx, N_EXPERTS, 0)
        keep = keep | hit
    picked = jnp.where(keep, scores[:N_EXPERTS], 0.0)
    gates = picked / jnp.sum(picked, axis=0, keepdims=True) * ROUTED_SCALE
    return jnp.concatenate([gates, jnp.zeros((LANES - N_EXPERTS, tm), F32)], axis=0)


def _dense_router_kernel(x1_ref, x1b_ref, pp_ref, ps_ref, sw1_ref, sw3_ref, sw2_ref, pw_ref, pg_ref, rwh_ref,
                         rwl_ref, rb_ref, dense_ref, gates_ref, rank_ref, chosen_t_ref, rank_t_ref, count_ref,
                         *, prompt_tiles):
    xb = x1b_ref[...]
    hid = _silu(_dot(xb, sw1_ref[...])) * _dot(xb, sw3_ref[...])
    shared = _dot(hid.astype(BF16), sw2_ref[...])
    p = jnp.where(pl.program_id(0) < prompt_tiles, pp_ref[...], ps_ref[...])
    ple = _dot(p.astype(BF16), pw_ref[...]) * _sigmoid(_dot(xb, pg_ref[...]))
    x1 = x1_ref[...]
    dense_ref[...] = DEEPNORM_ALPHA * x1 + shared + ple
    gates_t = _router_gates_t(x1, xb, rwh_ref[...], rwl_ref[...], rb_ref[...])
    tm = x1.shape[0]
    chosen_t = (gates_t > 0.0).astype(BF16)
    ii = lax.broadcasted_iota(jnp.int32, (tm, tm), 0)
    jj = lax.broadcasted_iota(jnp.int32, (tm, tm), 1)
    rank_t = _dot(chosen_t, (ii < jj).astype(BF16))
    chosen_t_ref[...] = chosen_t
    rank_t_ref[...] = rank_t.astype(BF16)
    gates = gates_t.T
    rank = rank_t.T
    gates_ref[...] = gates.astype(BF16)
    rank_ref[...] = rank.astype(BF16)
    count = rank[tm - 1:tm, :] + (gates[tm - 1:tm, :] > 0.0).astype(F32)
    count_ref[0] = jnp.broadcast_to(count, (SUBLANES, LANES))


def _dense_router(x1, x1b, p_prompt, p_sample, consts):
    m = x1.shape[0]
    nblk = m // MOE_BLOCK
    prompt_tiles = p_prompt.shape[0] // MOE_BLOCK
    in_prompt, in_sample = _group_maps(prompt_tiles)
    row = lambda i: (i, 0)
    const = lambda i: (0, 0)
    return pl.pallas_call(
        functools.partial(_dense_router_kernel, prompt_tiles=prompt_tiles),
        grid=(nblk,),
        in_specs=[pl.BlockSpec((MOE_BLOCK, D_MODEL), row), pl.BlockSpec((MOE_BLOCK, D_MODEL), row),
                  pl.BlockSpec((MOE_BLOCK, PLE_DIM), in_prompt), pl.BlockSpec((MOE_BLOCK, PLE_DIM), in_sample)]
        + [pl.BlockSpec(a.shape, const) for a in consts],
        out_specs=[pl.BlockSpec((MOE_BLOCK, D_MODEL), row), pl.BlockSpec((MOE_BLOCK, LANES), row),
                   pl.BlockSpec((MOE_BLOCK, LANES), row), pl.BlockSpec((LANES, MOE_BLOCK), row),
                   pl.BlockSpec((LANES, MOE_BLOCK), row), pl.BlockSpec((1, SUBLANES, LANES), lambda i: (i, 0, 0))],
        out_shape=[jax.ShapeDtypeStruct((m, D_MODEL), F32), jax.ShapeDtypeStruct((m, LANES), BF16),
                   jax.ShapeDtypeStruct((m, LANES), BF16), jax.ShapeDtypeStruct((nblk * LANES, MOE_BLOCK), BF16),
                   jax.ShapeDtypeStruct((nblk * LANES, MOE_BLOCK), BF16),
                   jax.ShapeDtypeStruct((nblk, SUBLANES, LANES), F32)],
        compiler_params=_params("parallel"),
        name="dense_router",
    )(x1, x1b, p_prompt, p_sample, *consts)


def _moe_tables(counts, tm):
    nblk = counts.shape[0]
    nchunk_max = MOE_CAP // MOE_SEG
    padded = (counts + MOE_SEG - 1) // MOE_SEG * MOE_SEG
    start = jnp.cumsum(padded, axis=1) - padded
    used = jnp.sum(padded, axis=1)
    rows_e = jnp.sum(padded, axis=0)
    rows_e_t = (rows_e + tm - 1) // tm * tm
    base_e = jnp.cumsum(rows_e_t) - rows_e_t
    seg_row = base_e[None, :] + jnp.cumsum(padded, axis=0) - padded
    chunk0 = jnp.arange(nchunk_max, dtype=jnp.int32) * MOE_SEG
    owner = jnp.sum((start + padded)[:, None, :] <= chunk0[None, :, None], axis=-1)
    onehot = owner[..., None] == jnp.arange(N_EXPERTS, dtype=jnp.int32)
    dst = chunk0[None, :] + jnp.sum(jnp.where(onehot, (seg_row - start)[:, None, :], 0), axis=-1)
    n_tiles = _moe_max_tiles(nblk, tm)
    live = chunk0[None, :] < used[:, None]
    parity = (jnp.arange(nblk, dtype=jnp.int32) % 2)[:, None]
    spare = n_tiles * tm + parity * MOE_CAP + chunk0[None, :]
    src = jnp.where(live, dst, 0)
    dst = jnp.where(live, dst, spare)
    tile_row = jnp.arange(n_tiles, dtype=jnp.int32) * tm
    tiles_used = jnp.sum(rows_e_t) // tm
    tile_row = jnp.minimum(tile_row, (tiles_used - 1) * tm)
    tile_expert = jnp.minimum(jnp.sum((base_e + rows_e_t)[None, :] <= tile_row[:, None], axis=-1), N_EXPERTS - 1)
    bounds = jnp.zeros((nblk, SUBLANES, LANES), F32)
    bounds = bounds.at[:, 0, :N_EXPERTS].set(start.astype(F32)).at[:, 1, :N_EXPERTS].set(padded.astype(F32))
    i32 = lambda a: a.astype(jnp.int32)
    return dict(dst=i32(dst.reshape(-1)), src=i32(src.reshape(-1)), nchunk=i32(used // MOE_SEG),
                pad_row=i32(base_e + rows_e),
                pad_n=i32((rows_e_t - rows_e) // MOE_SEG), tile_expert=i32(tile_expert),
                tiles_used=i32(tiles_used.reshape(1)), bounds=bounds)


def _moe_max_tiles(nblk, tm):
    return (nblk * (MOE_BLOCK * TOP_K + N_EXPERTS * (MOE_SEG - 1)) + N_EXPERTS * (tm - MOE_SEG) + tm - 1) // tm


def _slot_bounds(bounds):
    return bounds[0:1, :], bounds[1:2, :]


def _moe_gather_kernel(dst_ref, nchunk_ref, pad_row_ref, pad_n_ref,
                       x_ref, chosen_t_ref, rank_t_ref, bounds_ref, rows_ref, buf, zbuf, sem, zsem):
    j = pl.program_id(0)
    nblk = pl.num_programs(0)
    slot = j % 2
    nchunk_max = MOE_CAP // MOE_SEG

    def chunk_copy(blk, q, sl):
        src = buf.at[sl, pl.ds(pl.multiple_of(q * MOE_SEG, MOE_SEG), MOE_SEG), :]
        row = pl.multiple_of(dst_ref[blk * nchunk_max + q], MOE_SEG)
        return pltpu.make_async_copy(src, rows_ref.at[pl.ds(row, MOE_SEG), :], sem.at[sl])

    def pad_copy(e, i):
        row = pl.multiple_of(pad_row_ref[e] + i * MOE_SEG, MOE_SEG)
        return pltpu.make_async_copy(zbuf, rows_ref.at[pl.ds(row, MOE_SEG), :], zsem)

    def sub_chunks(blk, sub, sl, action):
        for q in range(sub * (MOE_SUB // MOE_SEG), (sub + 1) * (MOE_SUB // MOE_SEG)):
            action(chunk_copy(blk, q, sl))

    def sub_used(blk, sub):
        return True if sub < MOE_SUB_ALWAYS else sub * MOE_SUB < nchunk_ref[blk] * MOE_SEG

    def for_pads(action):
        def per_expert(e, carry):
            def body(i, c):
                action(pad_copy(e, i))
                return c
            return lax.fori_loop(0, pad_n_ref[e], body, carry)
        lax.fori_loop(0, N_EXPERTS, per_expert, 0)

    @pl.when(j == 0)
    def _():
        zbuf[...] = jnp.zeros_like(zbuf)
        for_pads(lambda c: c.start())

    start, length = _slot_bounds(bounds_ref[0])
    x = x_ref[...]
    chosen_rank_t = jnp.concatenate([chosen_t_ref[...], rank_t_ref[...]], axis=1)
    before = jnp.maximum(j - 2, 0)
    subs = range(MOE_CAP // MOE_SUB)

    @pl.when(j >= 2)
    def _():
        for sub in subs:
            @_when(sub_used(before, sub))
            def _():
                sub_chunks(before, sub, slot, lambda c: c.wait())

    def owners(sub):
        s = (sub * MOE_SUB + lax.broadcasted_iota(jnp.int32, (MOE_SUB, LANES), 0)).astype(F32)
        owner = (s >= start) & (s < start + length)
        within = s[:, 0:1] - jnp.sum(jnp.where(owner, start, 0.0), axis=1, keepdims=True)
        return owner.astype(BF16), within

    def picks(owner_within):
        owner, within = owner_within
        hit = _dot(owner, chosen_rank_t)
        return ((hit[:, :MOE_BLOCK] > 0.5) & (hit[:, MOE_BLOCK:] == within)).astype(BF16)

    def emit(sub, pick):
        buf[slot, sub * MOE_SUB:(sub + 1) * MOE_SUB, :] = _dot(pick, x).astype(BF16)
        sub_chunks(j, sub, slot, lambda c: c.start())

    staged = [picks(ow) for ow in [owners(sub) for sub in subs[:MOE_SUB_ALWAYS]]]
    for sub, pick in enumerate(staged):
        emit(sub, pick)
    for sub in subs[MOE_SUB_ALWAYS:]:
        @pl.when(sub_used(j, sub))
        def _():
            emit(sub, picks(owners(sub)))

    @pl.when(j == nblk - 1)
    def _():
        prev = jnp.maximum(j - 1, 0)
        for sub in subs:
            @_when(sub_used(j, sub))
            def _():
                sub_chunks(j, sub, slot, lambda c: c.wait())

            @_when((j >= 1) & sub_used(prev, sub))
            def _():
                sub_chunks(prev, sub, 1 - slot, lambda c: c.wait())

        for_pads(lambda c: c.wait())


def _moe_gather(x1b, chosen_t, rank_t, tables, tm):
    m = x1b.shape[0]
    nblk = m // MOE_BLOCK
    rows = _moe_max_tiles(nblk, tm) * tm + 2 * MOE_CAP
    blk = lambda j, *_: (j, 0)
    grid_spec = pltpu.PrefetchScalarGridSpec(
        num_scalar_prefetch=4,
        grid=(nblk,),
        in_specs=[pl.BlockSpec((MOE_BLOCK, D_MODEL), blk), pl.BlockSpec((LANES, MOE_BLOCK), blk),
                  pl.BlockSpec((LANES, MOE_BLOCK), blk),
                  pl.BlockSpec((1, SUBLANES, LANES), lambda j, *_: (j, 0, 0))],
        out_specs=pl.BlockSpec(memory_space=pl.ANY),
        scratch_shapes=[pltpu.VMEM((2, MOE_CAP, D_MODEL), BF16), pltpu.VMEM((MOE_SEG, D_MODEL), BF16),
                        pltpu.SemaphoreType.DMA((2,)), pltpu.SemaphoreType.DMA(())],
    )
    return pl.pallas_call(
        _moe_gather_kernel,
        grid_spec=grid_spec,
        out_shape=jax.ShapeDtypeStruct((rows, D_MODEL), BF16),
        compiler_params=_params("arbitrary"),
        name="moe_gather",
    )(tables["dst"], tables["nchunk"], tables["pad_row"], tables["pad_n"], x1b, chosen_t, rank_t, tables["bounds"])


def _moe_ffn_kernel(tile_expert_ref, tiles_used_ref, x_ref, w1_ref, w3_ref, w2_ref, y_ref, w13_scr, w2_scr):
    i = pl.program_id(0)

    @pl.when(i < tiles_used_ref[0])
    def _():
        @pl.when((i == 0) | (tile_expert_ref[i] != tile_expert_ref[jnp.maximum(i - 1, 0)]))
        def _():
            w13_scr[:, :EXPERT_FF] = w1_ref[0].astype(BF16)
            w13_scr[:, EXPERT_FF:] = w3_ref[0].astype(BF16)
            w2_scr[...] = w2_ref[0].astype(BF16)

        up = _dot(x_ref[...], w13_scr[...])
        hid = _silu(up[:, :EXPERT_FF]) * up[:, EXPERT_FF:]
        y_ref[...] = _dot(hid.astype(BF16), w2_scr[...]).astype(BF16)


def _moe_ffn(rows, w1, w3, w2, tables, tm):
    n_tiles = tables["tile_expert"].shape[0]
    tile = lambda i, tile_expert, tiles_used: (jnp.minimum(i, tiles_used[0] - 1), 0)
    expert = lambda i, tile_expert, tiles_used: (tile_expert[i], 0, 0)
    grid_spec = pltpu.PrefetchScalarGridSpec(
        num_scalar_prefetch=2,
        grid=(n_tiles,),
        in_specs=[pl.BlockSpec((tm, D_MODEL), tile), pl.BlockSpec((1, D_MODEL, EXPERT_FF), expert),
                  pl.BlockSpec((1, D_MODEL, EXPERT_FF), expert), pl.BlockSpec((1, EXPERT_FF, D_MODEL), expert)],
        out_specs=pl.BlockSpec((tm, D_MODEL), tile),
        scratch_shapes=[pltpu.VMEM((D_MODEL, 2 * EXPERT_FF), BF16), pltpu.VMEM((EXPERT_FF, D_MODEL), BF16)],
    )
    return pl.pallas_call(
        _moe_ffn_kernel,
        grid_spec=grid_spec,
        out_shape=jax.ShapeDtypeStruct(rows.shape, BF16),
        compiler_params=_params("arbitrary"),
        name="moe_ffn",
    )(tables["tile_expert"], tables["tiles_used"], rows, w1, w3, w2)


def _moe_combine_kernel(dst_ref, nchunk_ref, dense_ref, gates_ref, rank_ref, bounds_ref, g_ref, b_ref, rows_ref,
                        op_ref, os_ref, acc, buf, sem, *, prompt_tiles):
    j = pl.program_id(0)
    nblk = pl.num_programs(0)
    slot = j % 2
    nchunk_max = MOE_CAP // MOE_SEG

    def chunk_copy(blk, q, sl):
        row = pl.multiple_of(dst_ref[blk * nchunk_max + q], MOE_SEG)
        dst = buf.at[sl, pl.ds(pl.multiple_of(q * MOE_SEG, MOE_SEG), MOE_SEG), :]
        return pltpu.make_async_copy(rows_ref.at[pl.ds(row, MOE_SEG), :], dst, sem.at[sl])

    subs = range(MOE_CAP // MOE_SUB)
    always, rest = subs[:MOE_SUB_ALWAYS], subs[MOE_SUB_ALWAYS:]

    def sub_used(blk, sub):
        return True if sub < MOE_SUB_ALWAYS else sub * MOE_SUB < nchunk_ref[blk] * MOE_SEG

    def group(blk, sub, sl, action):
        for q in range(sub * (MOE_SUB // MOE_SEG), (sub + 1) * (MOE_SUB // MOE_SEG)):
            action(chunk_copy(blk, q, sl))

    @pl.when(j == 0)
    def _():
        buf[...] = jnp.zeros_like(buf)
        for sub in subs:
            @_when(sub_used(0, sub))
            def _():
                group(0, sub, 0, lambda c: c.start())

    for sub in subs:
        @_when(sub_used(j, sub))
        def _():
            group(j, sub, slot, lambda c: c.wait())

    bounds_t = jnp.concatenate([bounds_ref[0], jnp.zeros((LANES - SUBLANES, LANES), F32)], axis=0).T
    start, length = bounds_t[:, 0:1], bounds_t[:, 1:2]
    gates_rank = jnp.concatenate([gates_ref[...], rank_ref[...]], axis=0)

    def owners(sub):
        s = (sub * MOE_SUB + lax.broadcasted_iota(jnp.int32, (LANES, MOE_SUB), 1)).astype(F32)
        owner = (s >= start) & (s < start + length)
        within = s[0:1, :] - jnp.sum(jnp.where(owner, start, 0.0), axis=0, keepdims=True)
        return owner.astype(BF16), within

    def weights(owner_within):
        owner, within = owner_within
        hit = _dot(gates_rank, owner)
        return jnp.where(hit[MOE_BLOCK:] == within, hit[:MOE_BLOCK], 0.0).astype(BF16)

    def apply(sub, weight):
        return _dot(weight, buf[slot, sub * MOE_SUB:(sub + 1) * MOE_SUB, :])

    nxt = jnp.minimum(j + 1, nblk - 1)
    staged = [owners(sub) for sub in always]
    for sub in always:
        group(nxt, sub, 1 - slot, lambda c: c.start())
    staged = [weights(ow) for ow in staged]
    total = dense_ref[...]
    for sub, weight in zip(always, staged):
        total = total + apply(sub, weight)
    acc[...] = total
    for sub in rest:
        @pl.when((j + 1 < nblk) & sub_used(nxt, sub))
        def _():
            group(nxt, sub, 1 - slot, lambda c: c.start())

        @pl.when(sub_used(j, sub))
        def _():
            acc[...] += apply(sub, weights(owners(sub)))

    @pl.when(j == nblk - 1)
    def _():
        for sub in always:
            group(nxt, sub, 1 - slot, lambda c: c.wait())

    @pl.when(j < prompt_tiles)
    def _():
        op_ref[...] = _layer_norm(acc[...], g_ref[...], b_ref[...])

    @pl.when(j >= prompt_tiles)
    def _():
        os_ref[...] = _layer_norm(acc[...], g_ref[...], b_ref[...])


def _moe_combine(rows, dense, gates, rank, tables, g, b, n_prompt):
    m = dense.shape[0]
    nblk = m // MOE_BLOCK
    prompt_tiles = n_prompt // MOE_BLOCK
    in_prompt, in_sample = _group_maps(prompt_tiles)
    blk = lambda j, *_: (j, 0)
    const = lambda j, *_: (0, 0)
    grid_spec = pltpu.PrefetchScalarGridSpec(
        num_scalar_prefetch=2,
        grid=(nblk,),
        in_specs=[pl.BlockSpec((MOE_BLOCK, D_MODEL), blk), pl.BlockSpec((MOE_BLOCK, LANES), blk),
                  pl.BlockSpec((MOE_BLOCK, LANES), blk),
                  pl.BlockSpec((1, SUBLANES, LANES), lambda j, *_: (j, 0, 0)),
                  pl.BlockSpec((1, D_MODEL), const), pl.BlockSpec((1, D_MODEL), const),
                  pl.BlockSpec(memory_space=pl.ANY)],
        out_specs=[pl.BlockSpec((MOE_BLOCK, D_MODEL), in_prompt), pl.BlockSpec((MOE_BLOCK, D_MODEL), in_sample)],
        scratch_shapes=[pltpu.VMEM((MOE_BLOCK, D_MODEL), F32), pltpu.VMEM((2, MOE_CAP, D_MODEL), BF16),
                        pltpu.SemaphoreType.DMA((2,))],
    )
    return pl.pallas_call(
        functools.partial(_moe_combine_kernel, prompt_tiles=prompt_tiles),
        grid_spec=grid_spec,
        out_shape=[jax.ShapeDtypeStruct((n_prompt, D_MODEL), F32),
                   jax.ShapeDtypeStruct((m - n_prompt, D_MODEL), F32)],
        compiler_params=_params("arbitrary"),
        name="moe_combine",
    )(tables["src"], tables["nchunk"], dense, gates, rank, tables["bounds"], g, b, rows)


def _row(v):
    return v.reshape(1, -1).astype(F32)


def _scan_layout(batch, seq, row0):
    t = math.gcd(seq, CHUNK)
    nchunk = seq // t
    chunks = SCAN_CHUNKS if nchunk % SCAN_CHUNKS == 0 else 1
    seqs = SCAN_SEQS if (nchunk == 1 and batch % SCAN_SEQS == 0 and row0 % (SCAN_SEQS * t) == 0) else 1
    return dict(batch=batch, seq=seq, row0=row0, t=t, seqs=seqs, chunks=chunks)


def _lanes_at(v, start):
    return jnp.zeros((1, LANES), F32).at[0, start:start + v.shape[0]].set(v.astype(F32))


def kernel(x_prompt, x_sample, p_prompt, p_sample, state_ssm_conv, state_ssm, state_dn_conv, state_dn, emb_ln_g, emb_ln_b, w_in, conv_a_w, conv_a_b, ssm_dt_bias, ssm_a_log, ssm_d, ssm_norm_w, w_a, conv_b_w, dn_dt_bias, dn_a_log, dn_norm_w, w_b, w_o, ln1_g, ln1_b, router_w, router_bias, exp_w1, exp_w3, exp_w2, sh_w1, sh_w3, sh_w2, ple_w, ple_gate_w, ln2_g, ln2_b):
    bp, lp, _ = x_prompt.shape
    bs, ls, _ = x_sample.shape
    n_p = bp * lp
    n_s = bs * ls
    xp = x_prompt.reshape(n_p, D_MODEL)
    xs = x_sample.reshape(n_s, D_MODEL)

    w = w_in[0]
    o_z, o_xbc, o_dt = 0, SSM_INNER, SSM_INNER + SSM_CONV_CH
    o_qkv = o_dt + SSM_HEADS
    o_a = o_qkv + DN_CONV_CH
    o_b = o_a + DN_HEADS
    o_zb = o_b + DN_HEADS
    o_ga = o_zb + DN_V
    w_pa = w[:, o_z:o_dt].astype(BF16)
    w_pb = jnp.concatenate([w[:, o_qkv:o_a], w[:, o_zb:o_ga]], axis=1).astype(BF16)
    w_pg = w[:, o_ga:].astype(BF16)
    w_ps = jnp.zeros((D_MODEL, LANES), F32)
    w_ps = w_ps.at[:, SMALL_DT:SMALL_DT + SSM_HEADS].set(w[:, o_dt:o_qkv])
    w_ps = w_ps.at[:, SMALL_A:SMALL_A + 2 * DN_HEADS].set(w[:, o_a:o_zb]).astype(BF16)

    eg, eb = _row(emb_ln_g), _row(emb_ln_b)
    proj_a, small = _ln_matmul(xp, xs, eg, eb, w_pa, 1024, 2304, w_ps)
    proj_b = _ln_matmul(xp, xs, eg, eb, w_pb, 1024, 2048)

    ssd_consts = [conv_a_w[0], _row(conv_a_b[0]), _lanes_at(ssm_dt_bias[0], SMALL_DT),
                  _lanes_at(ssm_a_log[0], SMALL_DT), _row(jnp.repeat(ssm_d[0], SSM_HEAD_DIM)),
                  _row(ssm_norm_w[0])]
    gdn_consts = [conv_b_w[0], _lanes_at(dn_a_log[0], SMALL_A), _lanes_at(dn_dt_bias[0], SMALL_A),
                  _row(dn_norm_w[0])]
    zeros = lambda *s: jnp.zeros(s, F32)
    prompt = _scan_layout(bp, lp, 0)
    sample = _scan_layout(bs, ls, n_p)

    ya_p, pa_conv, pa_ssm = _ssd(proj_a, small, ssd_consts, zeros(bp, CONV_WIDTH - 1, SSM_CONV_CH),
                                 zeros(bp, SSM_HEADS, SSM_HEAD_DIM, SSM_STATE), **prompt)
    ya_s, sa_conv, sa_ssm = _ssd(proj_a, small, ssd_consts, state_ssm_conv[0], state_ssm[0], **sample)
    ob_p, pb_conv, pb_dn = _gdn(proj_b, small, gdn_consts, zeros(bp, CONV_WIDTH - 1, DN_CONV_CH),
                                zeros(bp, DN_HEADS, DN_HEAD, DN_HEAD), **prompt)
    ob_s, sb_conv, sb_dn = _gdn(proj_b, small, gdn_consts, state_dn_conv[0], state_dn[0], **sample)

    x1, x1b = _merge(
        xp, xs, ya_p, ya_s, ob_p, ob_s,
        [eg, eb, w_pg, w_a[0].astype(BF16), w_b[0].astype(BF16), w_o[0].astype(BF16), _row(ln1_g[0]), _row(ln1_b[0])],
        512)

    router_w_t = jnp.zeros((LANES, D_MODEL), F32).at[:N_EXPERTS].set(router_w[0].T)
    router_w_hi = router_w_t.astype(BF16)
    router_w_lo = (router_w_t - router_w_hi.astype(F32)).astype(BF16)
    router_b = jnp.broadcast_to(_lanes_at(router_bias[0], 0).reshape(LANES, 1), (LANES, MOE_BLOCK))
    dense, gates, rank, chosen_t, rank_t, counts = _dense_router(
        x1, x1b, p_prompt[0].reshape(n_p, PLE_DIM), p_sample[0].reshape(n_s, PLE_DIM),
        [sh_w1[0].astype(BF16), sh_w3[0].astype(BF16), sh_w2[0].astype(BF16), ple_w[0].astype(BF16),
         ple_gate_w[0].astype(BF16), router_w_hi, router_w_lo, router_b])

    tables = _moe_tables(counts[:, 0, :N_EXPERTS].astype(jnp.int32), MOE_TM)
    sorted_rows = _moe_gather(x1b, chosen_t, rank_t, tables, MOE_TM)
    expert_out = _moe_ffn(sorted_rows, exp_w1[0], exp_w3[0], exp_w2[0], tables, MOE_TM)
    out_p, out_s = _moe_combine(expert_out, dense, gates, rank, tables, _row(ln2_g[0]), _row(ln2_b[0]), n_p)

    return (out_p.reshape(bp, lp, D_MODEL), out_s.reshape(bs, ls, D_MODEL),
            pa_conv[None], pa_ssm[None], pb_conv[None], pb_dn[None],
            sa_conv[None], sa_ssm[None], sb_conv[None], sb_dn[None])
```

```python
import functools
import math

import jax
import jax.numpy as jnp
from jax import lax
from jax.experimental import pallas as pl
from jax.experimental.pallas import tpu as pltpu

F32 = jnp.float32
BF16 = jnp.bfloat16
HIGHEST = lax.Precision.HIGHEST

D_MODEL = 1024
SSM_INNER = 2048
SSM_HEAD_DIM = 64
SSM_HEADS = 32
SSM_GROUPS = 2
SSM_HEADS_PER_GROUP = 16
SSM_STATE = 128
SSM_CONV_CH = 2560
DN_HEADS = 8
DN_HEAD = 128
DN_QK = 1024
DN_V = 1024
DN_CONV_CH = 3072
CONV_WIDTH = 4
CHUNK = 64
N_EXPERTS = 64
TOP_K = 8
N_EXPERT_GROUPS = 8
EXPERTS_PER_GROUP = 8
TOPK_GROUPS = 4
EXPERT_FF = 256
SHARED_FF = 256
ROUTED_SCALE = 2.5
PLE_DIM = 256
LN_EPS = 1e-5
RMS_EPS = 1e-6
L2_EPS = 1e-6
DEEPNORM_ALPHA = 2.0 ** 0.25

LANES = 128
SUBLANES = 8
VMEM_LIMIT = 56 * 1024 * 1024
SMALL_DT = 0
SMALL_A = 32
SMALL_B = 40
PROJ_TM = 1024
PROJ_TN_A = 2304
PROJ_TN_B = 2048
MERGE_TM = 512
SCAN_CHUNKS = 8
SCAN_SEQS = 8
MOE_BLOCK = 256
MOE_SEG = 16
MOE_SUB = 512
MOE_CAP = -(-(MOE_BLOCK * TOP_K + N_EXPERTS * (MOE_SEG - 1)) // MOE_SUB) * MOE_SUB
MOE_TM = 512
MOE_SUB_ALWAYS = -(-(MOE_BLOCK * TOP_K + N_EXPERTS * MOE_SEG // 2) // MOE_SUB)


def _sigmoid(x):
    return 1.0 / (1.0 + jnp.exp(-x))


def _silu(x):
    return x * _sigmoid(x)


def _softplus(x):
    return jnp.maximum(x, 0.0) + jnp.log(1.0 + jnp.exp(-jnp.abs(x)))


def _layer_norm(x, g, b):
    mu = jnp.mean(x, axis=-1, keepdims=True)
    xc = x - mu
    var = jnp.mean(xc * xc, axis=-1, keepdims=True)
    return xc * lax.rsqrt(var + LN_EPS) * g + b


def _dot(a, b):
    return jnp.dot(a, b, preferred_element_type=F32)


def _dot_nt(a, b):
    return lax.dot_general(a, b, (((1,), (1,)), ((), ())), preferred_element_type=F32)


def _dot_tn(a, b):
    return lax.dot_general(a, b, (((0,), (0,)), ((), ())), preferred_element_type=F32)


def _dot_f32(a, b):
    return jnp.dot(a, b, precision=HIGHEST, preferred_element_type=F32)


def _params(*sem):
    return pltpu.CompilerParams(dimension_semantics=sem, vmem_limit_bytes=VMEM_LIMIT)


def _when(cond):
    if cond is True:
        return lambda fn: fn()
    if cond is False:
        return lambda fn: None
    return pl.when(cond)


def _row_tile(m, preferred):
    return max(d for d in range(SUBLANES, min(m, preferred) + 1, SUBLANES) if m % d == 0)


def _group_maps(prompt_tiles):
    in_prompt = lambda i, *_: (jnp.minimum(i, prompt_tiles - 1), 0)
    in_sample = lambda i, *_: (jnp.maximum(i - prompt_tiles, 0), 0)
    return in_prompt, in_sample


def _ln_matmul_kernel(xp_ref, xs_ref, g_ref, b_ref, w_ref, *rest, prompt_tiles, narrow):
    if narrow:
        wn_ref, o_ref, on_ref, h_scr = rest
    else:
        o_ref, h_scr = rest

    @pl.when(pl.program_id(1) == 0)
    def _():
        x = jnp.where(pl.program_id(0) < prompt_tiles, xp_ref[...], xs_ref[...])
        h_scr[...] = _layer_norm(x, g_ref[...], b_ref[...]).astype(BF16)
        if narrow:
            on_ref[...] = _dot(h_scr[...], wn_ref[...])

    o_ref[...] = _dot(h_scr[...], w_ref[...]).astype(o_ref.dtype)


def _ln_matmul(xp, xs, g, b, w, tm, tn, w_narrow=None):
    k = xp.shape[1]
    m = xp.shape[0] + xs.shape[0]
    n = w.shape[1]
    tm = _row_tile(math.gcd(xp.shape[0], xs.shape[0]), tm)
    prompt_tiles = xp.shape[0] // tm
    in_prompt, in_sample = _group_maps(prompt_tiles)
    narrow = w_narrow is not None
    const = lambda i, j: (0, 0)
    row = lambda i, j: (i, 0)
    in_specs = [pl.BlockSpec((tm, k), in_prompt), pl.BlockSpec((tm, k), in_sample),
                pl.BlockSpec((1, k), const), pl.BlockSpec((1, k), const), pl.BlockSpec((k, tn), lambda i, j: (0, j))]
    out_specs = [pl.BlockSpec((tm, tn), lambda i, j: (i, j))]
    out_shape = [jax.ShapeDtypeStruct((m, n), BF16)]
    args = [xp, xs, g, b, w]
    if narrow:
        in_specs.append(pl.BlockSpec((k, LANES), const))
        out_specs.append(pl.BlockSpec((tm, LANES), row))
        out_shape.append(jax.ShapeDtypeStruct((m, LANES), F32))
        args.append(w_narrow)
    out = pl.pallas_call(
        functools.partial(_ln_matmul_kernel, prompt_tiles=prompt_tiles, narrow=narrow),
        grid=(m // tm, n // tn),
        in_specs=in_specs,
        out_specs=out_specs,
        out_shape=out_shape,
        scratch_shapes=[pltpu.VMEM((tm, k), BF16)],
        compiler_params=_params("parallel", "arbitrary"),
        name="ln_matmul",
    )(*args)
    return out if narrow else out[0]


def _load_conv_state(cbuf, state):
    taps = CONV_WIDTH - 1
    cbuf[0:SUBLANES - taps, :] = jnp.zeros((SUBLANES - taps, cbuf.shape[1]), F32)
    cbuf[SUBLANES - taps:SUBLANES, :] = state


def _causal_conv(cbuf, x16, cw, t):
    ch = x16.shape[1]
    prev = cbuf[...]
    taps = CONV_WIDTH - 1
    if t <= SUBLANES:
        x = x16.astype(F32)
        ext = jnp.concatenate([prev, x], axis=0)
        lo = SUBLANES - taps
        y = ext[lo:lo + t] * cw[0:1]
        y = y + ext[lo + 1:lo + 1 + t] * cw[1:2]
        y = y + ext[lo + 2:lo + 2 + t] * cw[2:3]
        y = y + x * cw[3:4]
        cbuf[...] = ext[t:t + SUBLANES]
        return y, ext[SUBLANES + t - taps:SUBLANES + t]
    hi = prev.astype(BF16).astype(F32)
    mid = (prev - hi).astype(BF16).astype(F32)
    low = ((prev - hi) - mid).astype(BF16).astype(F32)
    x = x16.astype(F32)
    pack = 2 * SUBLANES
    nhead = 4 * SUBLANES if t % pack == 0 else 5 * SUBLANES
    head = jnp.concatenate([hi, mid, low, jnp.zeros((nhead - 3 * SUBLANES, ch), F32)], axis=0)
    if t % pack == 0:
        ext = jnp.concatenate([head.astype(BF16), x16], axis=0)
    else:
        ext = jnp.concatenate([head, x], axis=0).astype(BF16)
    out_row = lax.broadcasted_iota(jnp.int32, (taps * t, nhead + t), 0)
    col = lax.broadcasted_iota(jnp.int32, (taps * t, nhead + t), 1)
    log_t = t.bit_length() - 1
    shift = lax.shift_right_logical(out_row, log_t) + 1
    src = (out_row & (t - 1)) - shift
    in_x = (src >= 0) & (col == nhead + src)
    in_prev = (src < 0) & (col < 3 * SUBLANES) & ((col & (SUBLANES - 1)) == SUBLANES + src)
    shifted = _dot((in_x | in_prev).astype(BF16), ext)
    y = shifted[2 * t:3 * t] * cw[0:1]
    y = y + shifted[t:2 * t] * cw[1:2]
    y = y + shifted[0:t] * cw[2:3]
    y = y + x * cw[3:4]
    cbuf[...] = x[t - SUBLANES:t]
    return y, x[t - taps:t]


def _lane_expand(v, h0, count, width):
    t = v.shape[0]
    n = count * width
    out = jnp.broadcast_to(v[:, h0:h0 + 1], (t, n))
    if count > 1:
        lane = lax.broadcasted_iota(jnp.int32, (t, n), 1)
        for i in range(1, count):
            out = jnp.where(lane >= i * width, jnp.broadcast_to(v[:, h0 + i:h0 + i + 1], (t, n)), out)
    return out


def _grouped_transpose(v, per_tile):
    t = v.shape[0]
    blocks = [v if r == 0 else pltpu.roll(v, LANES - r, axis=1) for r in range(per_tile)]
    if per_tile * t < LANES:
        blocks.append(jnp.zeros((LANES - per_tile * t, LANES), F32))
    return jnp.concatenate(blocks, axis=0).T


def _same_block(i, j, size):
    shift = size.bit_length() - 1
    return lax.shift_right_logical(i, shift) == lax.shift_right_logical(j, shift)


def _unit_lower_inverses(lmats, ii, jj, t):
    mm = lambda a, b: _dot(a.astype(BF16), b.astype(BF16))
    base = min(16, t)
    in_base = _same_block(ii, jj, base)
    eye = (ii == jj).astype(F32)
    power = [jnp.where(in_base, l, 0.0) for l in lmats]
    inv = [eye - p for p in power]
    span = 2
    while span < base:
        power = [mm(p, p) for p in power]
        inv = [a + mm(a, p) for a, p in zip(inv, power)]
        span *= 2
    size = base
    while size < t:
        link = _same_block(ii, jj, 2 * size) & jnp.logical_not(_same_block(ii, jj, size))
        cross = [mm(a, jnp.where(link, l, 0.0)) for a, l in zip(inv, lmats)]
        inv = [a - mm(c, a) for a, c in zip(inv, cross)]
        size *= 2
    return inv


def _row_sums_of_squares(blocks):
    t = blocks[0].shape[0]
    sq = jnp.concatenate([b * b for b in blocks], axis=0)
    hi = sq.astype(BF16)
    lo = (sq - hi.astype(F32)).astype(BF16)
    ones = jnp.ones((LANES, LANES), BF16)
    sums = _dot(hi, ones) + _dot(lo, ones)
    return [sums[i * t:(i + 1) * t] for i in range(len(blocks))]


def _cumsum_rows(v, t):
    ii = lax.broadcasted_iota(jnp.int32, (t, t), 0)
    jj = lax.broadcasted_iota(jnp.int32, (t, t), 1)
    return _dot_f32((jj <= ii).astype(F32), v)


def _ssd_kernel(pa_ref, sm_ref, cw_ref, cb_ref, dtb_ref, alog_ref, dskip_ref, nw_ref, cst_ref, h0_ref,
                y_ref, cnew_ref, hnew_ref, cbuf, s_scr, *, t, first, last):
    per_tile = LANES // t
    width = per_tile * SSM_HEAD_DIM

    @_when(first)
    def _():
        _load_conv_state(cbuf, cst_ref[0])
        s_scr[...] = h0_ref[0]

    z = pa_ref[:, :SSM_INNER].astype(F32)
    conv, tail = _causal_conv(cbuf, pa_ref[:, SSM_INNER:], cw_ref[...], t)

    @_when(last)
    def _():
        cnew_ref[0] = tail

    xbc = _silu(conv + cb_ref[...])
    xs = xbc[:, :SSM_INNER]
    bm = xbc[:, SSM_INNER:SSM_INNER + SSM_GROUPS * SSM_STATE]
    cm = xbc[:, SSM_INNER + SSM_GROUPS * SSM_STATE:]

    dt = _softplus(sm_ref[...] + dtb_ref[...])
    da = dt * (-jnp.exp(alog_ref[...]))
    cum = _cumsum_rows(da, t)
    cum_t = _grouped_transpose(cum, per_tile)
    dt_t = _grouped_transpose(dt, per_tile)

    row = lax.broadcasted_iota(jnp.int32, (t, LANES), 0)
    lane = lax.broadcasted_iota(jnp.int32, (t, LANES), 1)
    causal = (lane % t) <= row
    brow = lax.broadcasted_iota(jnp.int32, (LANES, width), 0)
    bcol = lax.broadcasted_iota(jnp.int32, (LANES, width), 1)
    blockdiag = (brow // t) == (bcol // SSM_HEAD_DIM)

    ys = []
    for g in range(SSM_GROUPS):
        bg = bm[:, g * SSM_STATE:(g + 1) * SSM_STATE]
        cg = cm[:, g * SSM_STATE:(g + 1) * SSM_STATE].astype(BF16)
        cb = _dot_nt(cg, jnp.concatenate([bg] * per_tile, axis=0).astype(BF16))
        bg = bg.astype(BF16)
        for q in range(SSM_HEADS_PER_GROUP // per_tile):
            h0 = g * SSM_HEADS_PER_GROUP + q * per_tile
            ccol = _lane_expand(cum, h0, per_tile, t)
            seg = ccol - cum_t[h0:h0 + 1, :]
            decay = jnp.exp(jnp.where(causal, seg, -jnp.inf))
            wts = (cb * decay * dt_t[h0:h0 + 1, :]).astype(BF16)
            xt = xs[:, h0 * SSM_HEAD_DIM:h0 * SSM_HEAD_DIM + width]
            xbd = jnp.where(blockdiag, jnp.concatenate([xt] * per_tile, axis=0), 0.0).astype(BF16)
            y_intra = _dot(wts, xbd)
            st = s_scr[h0:h0 + per_tile].reshape(width, SSM_STATE)
            cum_w = ccol if width == LANES else _lane_expand(cum, h0, per_tile, SSM_HEAD_DIM)
            y_state = _dot_nt(cg, st.astype(BF16)) * jnp.exp(cum_w)
            ys.append(y_intra + y_state)
            wend = jnp.exp(cum_w[t - 1:t, :] - cum_w) * _lane_expand(dt, h0, per_tile, SSM_HEAD_DIM)
            ds = _dot_tn((xt * wend).astype(BF16), bg)
            for i in range(per_tile):
                h = h0 + i
                s_scr[h] = jnp.exp(cum[t - 1:t, h:h + 1]) * s_scr[h] + ds[i * SSM_HEAD_DIM:(i + 1) * SSM_HEAD_DIM]

    y = jnp.concatenate(ys, axis=1) + dskip_ref[...] * xs
    y = y * _silu(z)
    half = SSM_INNER // SSM_GROUPS
    normed = []
    for g in range(SSM_GROUPS):
        yg = y[:, g * half:(g + 1) * half]
        normed.append(yg * lax.rsqrt(jnp.mean(yg * yg, axis=-1, keepdims=True) + RMS_EPS))
    y_ref[...] = (jnp.concatenate(normed, axis=1) * nw_ref[...]).astype(BF16)

    @_when(last)
    def _():
        hnew_ref[0] = s_scr[...]


def _scan_step_kernel(chunk_kernel, n_tok, n_const, n_state, seqs, chunks, t):
    def step(*refs):
        tok = refs[:n_tok]
        const = refs[n_tok:n_tok + n_const]
        st_in = refs[n_tok + n_const:n_tok + n_const + n_state]
        y_ref = refs[n_tok + n_const + n_state]
        st_out = refs[n_tok + n_const + n_state + 1:n_tok + n_const + 2 * n_state + 1]
        scratch = refs[n_tok + n_const + 2 * n_state + 1:]
        first_step = pl.program_id(1) == 0
        last_step = pl.program_id(1) == pl.num_programs(1) - 1
        for s in range(seqs):
            for c in range(chunks):
                rows = pl.ds((s * chunks + c) * t, t)
                chunk_kernel(*[r.at[rows, :] for r in tok], *const, *[r.at[pl.ds(s, 1)] for r in st_in],
                             y_ref.at[rows, :], *[r.at[pl.ds(s, 1)] for r in st_out], *scratch,
                             first=first_step if c == 0 else False,
                             last=last_step if c == chunks - 1 else False)
    return step


def _scan_call(kernel, name, tok_in, const_in, state_in, y_width, scratch, *, batch, seq, row0, t, seqs, chunks,
               step_kernel=None):
    nstep = seq // (t * chunks)
    rows = seqs * chunks * t
    blk0 = row0 // rows

    def per_batch(a):
        zeros = (0,) * (a.ndim - 1)
        return pl.BlockSpec((seqs,) + a.shape[1:], lambda b, c: (b,) + zeros)

    if step_kernel is None:
        step_kernel = _scan_step_kernel(kernel, len(tok_in), len(const_in), len(state_in), seqs, chunks, t)
    return pl.pallas_call(
        step_kernel,
        grid=(batch // seqs, nstep),
        in_specs=[pl.BlockSpec((rows, a.shape[1]), lambda b, c: (blk0 + b * nstep + c, 0)) for a in tok_in]
        + [pl.BlockSpec(a.shape, lambda b, c: (0, 0)) for a in const_in]
        + [per_batch(a) for a in state_in],
        out_specs=[pl.BlockSpec((rows, y_width), lambda b, c: (b * nstep + c, 0))]
        + [per_batch(a) for a in state_in],
        out_shape=[jax.ShapeDtypeStruct((batch * seq, y_width), BF16)]
        + [jax.ShapeDtypeStruct(a.shape, F32) for a in state_in],
        scratch_shapes=scratch,
        compiler_params=_params("parallel", "arbitrary"),
        name=name,
    )(*tok_in, *const_in, *state_in)


def _ssd(proj_a, small, consts, conv_state, h0, **where):
    t = where["t"]
    scratch = [pltpu.VMEM((SUBLANES, SSM_CONV_CH), F32),
               pltpu.VMEM((SSM_HEADS, SSM_HEAD_DIM, SSM_STATE), F32)]
    return _scan_call(functools.partial(_ssd_kernel, t=t), f"ssd_t{t}", [proj_a, small], consts,
                      [conv_state, h0], SSM_INNER, scratch, **where)


def _gdn_step_kernel(pb_ref, sm_ref, cw_ref, alog_ref, dtb_ref, nw_ref, cst_ref, s0_ref,
                     o_ref, cnew_ref, snew_ref, cbuf, s_scr, *, t, seqs, chunks):
    first_step = pl.program_id(1) == 0
    last_step = pl.program_id(1) == pl.num_programs(1) - 1
    units = [(s, c) for s in range(seqs) for c in range(chunks)]
    rows = lambda u: pl.ds(u * t, t)
    per_tile = min(LANES // t, DN_HEADS)
    n = per_tile * t
    ntile = DN_HEADS // per_tile
    heads = range(DN_HEADS)
    ii = lax.broadcasted_iota(jnp.int32, (n, n), 0)
    jj = lax.broadcasted_iota(jnp.int32, (n, n), 1)
    causal = _same_block(ii, jj, t) & (jj <= ii)
    diag = ii == jj
    stack = lambda xs, p: jnp.concatenate(xs[p * per_tile:(p + 1) * per_tile], axis=0)

    qkv, zb, gcum, beta = [], [], [], []
    for u, (s, c) in enumerate(units):
        @_when(first_step if c == 0 else False)
        def _():
            _load_conv_state(cbuf, cst_ref[s])

        conv, tail = _causal_conv(cbuf, pb_ref[rows(u), :DN_CONV_CH], cw_ref[...], t)

        @_when(last_step if c == chunks - 1 else False)
        def _():
            cnew_ref[s] = tail

        qkv.append(_silu(conv))
        zb.append(pb_ref[rows(u), DN_CONV_CH:].astype(F32))
        sm = sm_ref[rows(u), :]
        beta.append(_sigmoid(sm))
        gate = -jnp.exp(alog_ref[...]) * _softplus(sm + dtb_ref[...])
        gcum.append(_cumsum_rows(gate, t))
    gcum_t = [_grouped_transpose(g, per_tile) for g in gcum]

    sumsq = _row_sums_of_squares([x[:, i * DN_HEAD:(i + 1) * DN_HEAD] for x in qkv for i in range(2 * DN_HEADS)])
    q, k, v, gcol, bcol, kb, egcol = ([[] for _ in units] for _ in range(7))
    for u in range(len(units)):
        for h in heads:
            lo = h * DN_HEAD
            ss = sumsq[u * 2 * DN_HEADS:(u + 1) * 2 * DN_HEADS]
            q[u].append(qkv[u][:, lo:lo + DN_HEAD] * lax.rsqrt(ss[h] + L2_EPS) * (DN_HEAD ** -0.5))
            k[u].append(qkv[u][:, DN_QK + lo:DN_QK + lo + DN_HEAD] * lax.rsqrt(ss[DN_HEADS + h] + L2_EPS))
            v[u].append(qkv[u][:, 2 * DN_QK + lo:2 * DN_QK + lo + DN_HEAD])
            gcol[u].append(gcum[u][:, SMALL_A + h:SMALL_A + h + 1])
            bcol[u].append(beta[u][:, SMALL_B + h:SMALL_B + h + 1])
            kb[u].append(k[u][h] * bcol[u][h])
            egcol[u].append(jnp.exp(gcol[u][h]))

    tiles = [(u, p) for u in range(len(units)) for p in range(ntile)]
    k16 = [stack(k[u], p).astype(BF16) for u, p in tiles]
    decay = []
    for u, p in tiles:
        h0 = p * per_tile
        seg = stack(gcol[u], p) - gcum_t[u][SMALL_A + h0:SMALL_A + h0 + 1, :n]
        decay.append(jnp.exp(jnp.where(causal, seg, -jnp.inf)))
    lmat = [_dot_nt(stack(kb[u], p).astype(BF16), k16[i]) * jnp.where(diag, 0.0, decay[i])
            for i, (u, p) in enumerate(tiles)]
    qk = [(_dot_nt(stack(q[u], p).astype(BF16), k16[i]) * decay[i]).astype(BF16) for i, (u, p) in enumerate(tiles)]
    inv = [a.astype(BF16) for a in _unit_lower_inverses(lmat, ii, jj, t)]

    split = lambda x: (x[:, :DN_HEAD], x[:, DN_HEAD:])
    o_from, o_own, s_from, s_own, glast = ([[] for _ in units] for _ in range(5))
    aw = [_dot(inv[i], jnp.concatenate(
        [stack([kb[u][h] * egcol[u][h] for h in heads], p), stack([v[u][h] * bcol[u][h] for h in heads], p)],
        axis=1).astype(BF16)) for i, (u, p) in enumerate(tiles)]
    qaw = [_dot(qk[i], aw[i].astype(BF16)) for i in range(len(tiles))]
    pad_rows = -t % (2 * SUBLANES)
    pad = [jnp.zeros((pad_rows, DN_HEAD), F32)] if pad_rows else []
    for u in range(len(units)):
        for h in heads:
            p, j = divmod(h, per_tile)
            i = u * ntile + p
            sel = slice(j * t, (j + 1) * t)
            qkw, qku0 = split(qaw[i][sel])
            o_from[u].append(q[u][h] * egcol[u][h] - qkw)
            o_own[u].append(qku0)
            glast[u].append(gcum[u][t - 1:t, SMALL_A + h:SMALL_A + h + 1])
            kd = (k[u][h] * jnp.exp(glast[u][h] - gcol[u][h])).astype(BF16)
            kdw, kdu0 = split(_dot_tn(kd, aw[i][sel].astype(BF16)))
            s_from[u].append(kdw)
            s_own[u].append(kdu0)

    outs = []
    for u, (s, c) in enumerate(units):
        @_when(first_step if c == 0 else False)
        def _():
            s_scr[...] = s0_ref[s]

        for h in heads:
            state = s_scr[h]
            both = _dot(jnp.concatenate([s_from[u][h], o_from[u][h]] + pad, axis=0).astype(BF16), state.astype(BF16))
            outs.append(both[DN_HEAD:DN_HEAD + t] + o_own[u][h])
            s_scr[h] = jnp.exp(glast[u][h]) * state - both[:DN_HEAD] + s_own[u][h]

        @_when(last_step if c == chunks - 1 else False)
        def _():
            snew_ref[s] = s_scr[...]

    norm_w = nw_ref[...]
    osq = _row_sums_of_squares(outs)
    for u in range(len(units)):
        normed = []
        for h in heads:
            i = u * DN_HEADS + h
            normed.append(outs[i] * lax.rsqrt(osq[i] * (1.0 / DN_HEAD) + RMS_EPS) * norm_w
                          * _silu(zb[u][:, h * DN_HEAD:(h + 1) * DN_HEAD]))
        o_ref[rows(u), :] = jnp.concatenate(normed, axis=1).astype(BF16)


def _gdn(proj_b, small, consts, conv_state, s0, **where):
    t = where["t"]
    scratch = [pltpu.VMEM((SUBLANES, DN_CONV_CH), F32),
               pltpu.VMEM((DN_HEADS, DN_HEAD, DN_HEAD), F32)]
    step = functools.partial(_gdn_step_kernel, t=t, seqs=where["seqs"], chunks=where["chunks"])
    return _scan_call(None, f"gdn_t{t}", [proj_b, small], consts, [conv_state, s0], DN_V, scratch,
                      step_kernel=step, **where)


def _merge_kernel(xp_ref, xs_ref, yap_ref, yas_ref, obp_ref, obs_ref, eg_ref, eb_ref, wg_ref, wa_ref, wb_ref,
                  wo_ref, g_ref, b_ref, x1_ref, x1b_ref, *, prompt_tiles):
    is_prompt = pl.program_id(0) < prompt_tiles
    h = _layer_norm(jnp.where(is_prompt, xp_ref[...], xs_ref[...]), eg_ref[...], eb_ref[...])
    out_a = _dot(jnp.where(is_prompt, yap_ref[...], yas_ref[...]), wa_ref[...])
    out_b = _dot(jnp.where(is_prompt, obp_ref[...], obs_ref[...]), wb_ref[...])
    gt = _dot(h.astype(BF16), wg_ref[...])
    merged = _sigmoid(gt[:, :D_MODEL]) * out_a + _sigmoid(gt[:, D_MODEL:]) * out_b
    mix = _dot(merged.astype(BF16), wo_ref[...])
    x1 = _layer_norm(DEEPNORM_ALPHA * h + mix, g_ref[...], b_ref[...])
    x1_ref[...] = x1
    x1b_ref[...] = x1.astype(BF16)


def _merge(xp, xs, ya_p, ya_s, ob_p, ob_s, consts, tm):
    m = xp.shape[0] + xs.shape[0]
    n_p = xp.shape[0]
    tm = _row_tile(math.gcd(n_p, m - n_p), tm)
    in_prompt, in_sample = _group_maps(n_p // tm)
    row = lambda i: (i, 0)
    const = lambda i: (0, 0)
    return pl.pallas_call(
        functools.partial(_merge_kernel, prompt_tiles=n_p // tm),
        grid=(m // tm,),
        in_specs=[pl.BlockSpec((tm, D_MODEL), in_prompt), pl.BlockSpec((tm, D_MODEL), in_sample),
                  pl.BlockSpec((tm, SSM_INNER), in_prompt), pl.BlockSpec((tm, SSM_INNER), in_sample),
                  pl.BlockSpec((tm, DN_V), in_prompt), pl.BlockSpec((tm, DN_V), in_sample)]
        + [pl.BlockSpec(a.shape, const) for a in consts],
        out_specs=[pl.BlockSpec((tm, D_MODEL), row), pl.BlockSpec((tm, D_MODEL), row)],
        out_shape=[jax.ShapeDtypeStruct((m, D_MODEL), F32), jax.ShapeDtypeStruct((m, D_MODEL), BF16)],
        compiler_params=_params("parallel"),
        name="merge",
    )(xp, xs, ya_p, ya_s, ob_p, ob_s, *consts)


def _take_first_max(rest, index, limit, axis):
    best = jnp.max(rest, axis=axis, keepdims=True)
    first = jnp.min(jnp.where(rest == best, index, limit), axis=axis, keepdims=True)
    hit = index == first
    return hit, jnp.where(hit, -jnp.inf, rest)


def _router_gates_t(x1, xb, rw_hi, rw_lo, bias):
    tm = x1.shape[0]
    x_lo = (x1 - xb.astype(F32)).astype(BF16)
    logits = _dot_nt(rw_hi, xb) + (_dot_nt(rw_hi, x_lo) + _dot_nt(rw_lo, xb))
    scores = _sigmoid(logits)
    row = lax.broadcasted_iota(jnp.int32, (LANES, tm), 0)
    sel = jnp.where(row < N_EXPERTS, scores + bias, -jnp.inf)
    by_group = sel.reshape(LANES // EXPERTS_PER_GROUP, EXPERTS_PER_GROUP, tm)
    sub = lax.broadcasted_iota(jnp.int32, by_group.shape, 1)
    top1 = jnp.max(by_group, axis=1, keepdims=True)
    _, others = _take_first_max(by_group, sub, EXPERTS_PER_GROUP, 1)
    gscore = (top1 + jnp.max(others, axis=1, keepdims=True))[:N_EXPERT_GROUPS]
    gidx = lax.broadcasted_iota(jnp.int32, gscore.shape, 0)
    gkeep = jnp.zeros(gscore.shape, jnp.bool_)
    for _ in range(TOPK_GROUPS):
        hit, gscore = _take_first_max(gscore, gidx, N_EXPERT_GROUPS, 0)
        gkeep = gkeep | hit
    gkeep = jnp.broadcast_to(gkeep, (N_EXPERT_GROUPS, EXPERTS_PER_GROUP, tm)).reshape(N_EXPERTS, tm)
    rest = jnp.where(gkeep, sel[:N_EXPERTS], -jnp.inf)
    eidx = lax.broadcasted_iota(jnp.int32, (N_EXPERTS, tm), 0)
    keep = jnp.zeros((N_EXPERTS, tm), jnp.bool_)
    for _ in range(TOP_K):
        hit, rest = _take_first_max(rest, eidx, N_EXPERTS, 0)
        keep = keep | hit
    picked = jnp.where(keep, scores[:N_EXPERTS], 0.0)
    gates = picked / jnp.sum(picked, axis=0, keepdims=True) * ROUTED_SCALE
    return jnp.concatenate([gates, jnp.zeros((LANES - N_EXPERTS, tm), F32)], axis=0)


def _dense_router_kernel(x1_ref, x1b_ref, pp_ref, ps_ref, sw1_ref, sw3_ref, sw2_ref, pw_ref, pg_ref, rwh_ref,
                         rwl_ref, rb_ref, dense_ref, gates_ref, rank_ref, chosen_t_ref, rank_t_ref, count_ref,
                         *, prompt_tiles):
    xb = x1b_ref[...]
    hid = _silu(_dot(xb, sw1_ref[...])) * _dot(xb, sw3_ref[...])
    shared = _dot(hid.astype(BF16), sw2_ref[...])
    p = jnp.where(pl.program_id(0) < prompt_tiles, pp_ref[...], ps_ref[...])
    ple = _dot(p.astype(BF16), pw_ref[...]) * _sigmoid(_dot(xb, pg_ref[...]))
    x1 = x1_ref[...]
    dense_ref[...] = DEEPNORM_ALPHA * x1 + shared + ple
    gates_t = _router_gates_t(x1, xb, rwh_ref[...], rwl_ref[...], rb_ref[...])
    tm = x1.shape[0]
    chosen_t = (gates_t > 0.0).astype(BF16)
    ii = lax.broadcasted_iota(jnp.int32, (tm, tm), 0)
    jj = lax.broadcasted_iota(jnp.int32, (tm, tm), 1)
    rank_t = _dot(chosen_t, (ii < jj).astype(BF16))
    chosen_t_ref[...] = chosen_t
    rank_t_ref[...] = rank_t.astype(BF16)
    gates = gates_t.T
    rank = rank_t.T
    gates_ref[...] = gates.astype(BF16)
    rank_ref[...] = rank.astype(BF16)
    count = rank[tm - 1:tm, :] + (gates[tm - 1:tm, :] > 0.0).astype(F32)
    count_ref[0] = jnp.broadcast_to(count, (SUBLANES, LANES))


def _dense_router(x1, x1b, p_prompt, p_sample, consts):
    m = x1.shape[0]
    nblk = m // MOE_BLOCK
    prompt_tiles = p_prompt.shape[0] // MOE_BLOCK
    in_prompt, in_sample = _group_maps(prompt_tiles)
    row = lambda i: (i, 0)
    const = lambda i: (0, 0)
    return pl.pallas_call(
        functools.partial(_dense_router_kernel, prompt_tiles=prompt_tiles),
        grid=(nblk,),
        in_specs=[pl.BlockSpec((MOE_BLOCK, D_MODEL), row), pl.BlockSpec((MOE_BLOCK, D_MODEL), row),
                  pl.BlockSpec((MOE_BLOCK, PLE_DIM), in_prompt), pl.BlockSpec((MOE_BLOCK, PLE_DIM), in_sample)]
        + [pl.BlockSpec(a.shape, const) for a in consts],
        out_specs=[pl.BlockSpec((MOE_BLOCK, D_MODEL), row), pl.BlockSpec((MOE_BLOCK, LANES), row),
                   pl.BlockSpec((MOE_BLOCK, LANES), row), pl.BlockSpec((LANES, MOE_BLOCK), row),
                   pl.BlockSpec((LANES, MOE_BLOCK), row), pl.BlockSpec((1, SUBLANES, LANES), lambda i: (i, 0, 0))],
        out_shape=[jax.ShapeDtypeStruct((m, D_MODEL), F32), jax.ShapeDtypeStruct((m, LANES), BF16),
                   jax.ShapeDtypeStruct((m, LANES), BF16), jax.ShapeDtypeStruct((nblk * LANES, MOE_BLOCK), BF16),
                   jax.ShapeDtypeStruct((nblk * LANES, MOE_BLOCK), BF16),
                   jax.ShapeDtypeStruct((nblk, SUBLANES, LANES), F32)],
        compiler_params=_params("parallel"),
        name="dense_router",
    )(x1, x1b, p_prompt, p_sample, *consts)


def _moe_tables(counts, tm):
    nblk = counts.shape[0]
    nchunk_max = MOE_CAP // MOE_SEG
    padded = (counts + MOE_SEG - 1) // MOE_SEG * MOE_SEG
    start = jnp.cumsum(padded, axis=1) - padded
    used = jnp.sum(padded, axis=1)
    rows_e = jnp.sum(padded, axis=0)
    rows_e_t = (rows_e + tm - 1) // tm * tm
    base_e = jnp.cumsum(rows_e_t) - rows_e_t
    seg_row = base_e[None, :] + jnp.cumsum(padded, axis=0) - padded
    chunk0 = jnp.arange(nchunk_max, dtype=jnp.int32) * MOE_SEG
    owner = jnp.sum((start + padded)[:, None, :] <= chunk0[None, :, None], axis=-1)
    onehot = owner[..., None] == jnp.arange(N_EXPERTS, dtype=jnp.int32)
    dst = chunk0[None, :] + jnp.sum(jnp.where(onehot, (seg_row - start)[:, None, :], 0), axis=-1)
    n_tiles = _moe_max_tiles(nblk, tm)
    live = chunk0[None, :] < used[:, None]
    parity = (jnp.arange(nblk, dtype=jnp.int32) % 2)[:, None]
    spare = n_tiles * tm + parity * MOE_CAP + chunk0[None, :]
    src = jnp.where(live, dst, 0)
    dst = jnp.where(live, dst, spare)
    tile_row = jnp.arange(n_tiles, dtype=jnp.int32) * tm
    tiles_used = jnp.sum(rows_e_t) // tm
    tile_row = jnp.minimum(tile_row, (tiles_used - 1) * tm)
    tile_expert = jnp.minimum(jnp.sum((base_e + rows_e_t)[None, :] <= tile_row[:, None], axis=-1), N_EXPERTS - 1)
    bounds = jnp.zeros((nblk, SUBLANES, LANES), F32)
    bounds = bounds.at[:, 0, :N_EXPERTS].set(start.astype(F32)).at[:, 1, :N_EXPERTS].set(padded.astype(F32))
    i32 = lambda a: a.astype(jnp.int32)
    return dict(dst=i32(dst.reshape(-1)), src=i32(src.reshape(-1)), nchunk=i32(used // MOE_SEG),
                pad_row=i32(base_e + rows_e),
                pad_n=i32((rows_e_t - rows_e) // MOE_SEG), tile_expert=i32(tile_expert),
                tiles_used=i32(tiles_used.reshape(1)), bounds=bounds)


def _moe_max_tiles(nblk, tm):
    return (nblk * (MOE_BLOCK * TOP_K + N_EXPERTS * (MOE_SEG - 1)) + N_EXPERTS * (tm - MOE_SEG) + tm - 1) // tm


def _slot_bounds(bounds):
    return bounds[0:1, :], bounds[1:2, :]


def _moe_gather_kernel(dst_ref, nchunk_ref, pad_row_ref, pad_n_ref,
                       x_ref, chosen_t_ref, rank_t_ref, bounds_ref, rows_ref, buf, zbuf, sem, zsem):
    j = pl.program_id(0)
    nblk = pl.num_programs(0)
    slot = j % 2
    nchunk_max = MOE_CAP // MOE_SEG

    def chunk_copy(blk, q, sl):
        src = buf.at[sl, pl.ds(pl.multiple_of(q * MOE_SEG, MOE_SEG), MOE_SEG), :]
        row = pl.multiple_of(dst_ref[blk * nchunk_max + q], MOE_SEG)
        return pltpu.make_async_copy(src, rows_ref.at[pl.ds(row, MOE_SEG), :], sem.at[sl])

    def pad_copy(e, i):
        row = pl.multiple_of(pad_row_ref[e] + i * MOE_SEG, MOE_SEG)
        return pltpu.make_async_copy(zbuf, rows_ref.at[pl.ds(row, MOE_SEG), :], zsem)

    def sub_chunks(blk, sub, sl, action):
        for q in range(sub * (MOE_SUB // MOE_SEG), (sub + 1) * (MOE_SUB // MOE_SEG)):
            action(chunk_copy(blk, q, sl))

    def sub_used(blk, sub):
        return True if sub < MOE_SUB_ALWAYS else sub * MOE_SUB < nchunk_ref[blk] * MOE_SEG

    def for_pads(action):
        def per_expert(e, carry):
            def body(i, c):
                action(pad_copy(e, i))
                return c
            return lax.fori_loop(0, pad_n_ref[e], body, carry)
        lax.fori_loop(0, N_EXPERTS, per_expert, 0)

    @pl.when(j == 0)
    def _():
        zbuf[...] = jnp.zeros_like(zbuf)
        for_pads(lambda c: c.start())

    start, length = _slot_bounds(bounds_ref[0])
    x = x_ref[...]
    chosen_rank_t = jnp.concatenate([chosen_t_ref[...], rank_t_ref[...]], axis=1)
    before = jnp.maximum(j - 2, 0)
    subs = range(MOE_CAP // MOE_SUB)

    @pl.when(j >= 2)
    def _():
        for sub in subs:
            @_when(sub_used(before, sub))
            def _():
                sub_chunks(before, sub, slot, lambda c: c.wait())

    def owners(sub):
        s = (sub * MOE_SUB + lax.broadcasted_iota(jnp.int32, (MOE_SUB, LANES), 0)).astype(F32)
        owner = (s >= start) & (s < start + length)
        within = s[:, 0:1] - jnp.sum(jnp.where(owner, start, 0.0), axis=1, keepdims=True)
        return owner.astype(BF16), within

    def picks(owner_within):
        owner, within = owner_within
        hit = _dot(owner, chosen_rank_t)
        return ((hit[:, :MOE_BLOCK] > 0.5) & (hit[:, MOE_BLOCK:] == within)).astype(BF16)

    def emit(sub, pick):
        buf[slot, sub * MOE_SUB:(sub + 1) * MOE_SUB, :] = _dot(pick, x).astype(BF16)
        sub_chunks(j, sub, slot, lambda c: c.start())

    staged = [picks(ow) for ow in [owners(sub) for sub in subs[:MOE_SUB_ALWAYS]]]
    for sub, pick in enumerate(staged):
        emit(sub, pick)
    for sub in subs[MOE_SUB_ALWAYS:]:
        @pl.when(sub_used(j, sub))
        def _():
            emit(sub, picks(owners(sub)))

    @pl.when(j == nblk - 1)
    def _():
        prev = jnp.maximum(j - 1, 0)
        for sub in subs:
            @_when(sub_used(j, sub))
            def _():
                sub_chunks(j, sub, slot, lambda c: c.wait())

            @_when((j >= 1) & sub_used(prev, sub))
            def _():
                sub_chunks(prev, sub, 1 - slot, lambda c: c.wait())

        for_pads(lambda c: c.wait())


def _moe_gather(x1b, chosen_t, rank_t, tables, tm):
    m = x1b.shape[0]
    nblk = m // MOE_BLOCK
    rows = _moe_max_tiles(nblk, tm) * tm + 2 * MOE_CAP
    blk = lambda j, *_: (j, 0)
    grid_spec = pltpu.PrefetchScalarGridSpec(
        num_scalar_prefetch=4,
        grid=(nblk,),
        in_specs=[pl.BlockSpec((MOE_BLOCK, D_MODEL), blk), pl.BlockSpec((LANES, MOE_BLOCK), blk),
                  pl.BlockSpec((LANES, MOE_BLOCK), blk),
                  pl.BlockSpec((1, SUBLANES, LANES), lambda j, *_: (j, 0, 0))],
        out_specs=pl.BlockSpec(memory_space=pl.ANY),
        scratch_shapes=[pltpu.VMEM((2, MOE_CAP, D_MODEL), BF16), pltpu.VMEM((MOE_SEG, D_MODEL), BF16),
                        pltpu.SemaphoreType.DMA((2,)), pltpu.SemaphoreType.DMA(())],
    )
    return pl.pallas_call(
        _moe_gather_kernel,
        grid_spec=grid_spec,
        out_shape=jax.ShapeDtypeStruct((rows, D_MODEL), BF16),
        compiler_params=_params("arbitrary"),
        name="moe_gather",
    )(tables["dst"], tables["nchunk"], tables["pad_row"], tables["pad_n"], x1b, chosen_t, rank_t, tables["bounds"])


def _moe_ffn_kernel(tile_expert_ref, tiles_used_ref, x_ref, w1_ref, w3_ref, w2_ref, y_ref, w13_scr, w2_scr):
    i = pl.program_id(0)

    @pl.when(i < tiles_used_ref[0])
    def _():
        @pl.when((i == 0) | (tile_expert_ref[i] != tile_expert_ref[jnp.maximum(i - 1, 0)]))
        def _():
            w13_scr[:, :EXPERT_FF] = w1_ref[0].astype(BF16)
            w13_scr[:, EXPERT_FF:] = w3_ref[0].astype(BF16)
            w2_scr[...] = w2_ref[0].astype(BF16)

        up = _dot(x_ref[...], w13_scr[...])
        hid = _silu(up[:, :EXPERT_FF]) * up[:, EXPERT_FF:]
        y_ref[...] = _dot(hid.astype(BF16), w2_scr[...]).astype(BF16)


def _moe_ffn(rows, w1, w3, w2, tables, tm):
    n_tiles = tables["tile_expert"].shape[0]
    tile = lambda i, tile_expert, tiles_used: (jnp.minimum(i, tiles_used[0] - 1), 0)
    expert = lambda i, tile_expert, tiles_used: (tile_expert[i], 0, 0)
    grid_spec = pltpu.PrefetchScalarGridSpec(
        num_scalar_prefetch=2,
        grid=(n_tiles,),
        in_specs=[pl.BlockSpec((tm, D_MODEL), tile), pl.BlockSpec((1, D_MODEL, EXPERT_FF), expert),
                  pl.BlockSpec((1, D_MODEL, EXPERT_FF), expert), pl.BlockSpec((1, EXPERT_FF, D_MODEL), expert)],
        out_specs=pl.BlockSpec((tm, D_MODEL), tile),
        scratch_shapes=[pltpu.VMEM((D_MODEL, 2 * EXPERT_FF), BF16), pltpu.VMEM((EXPERT_FF, D_MODEL), BF16)],
    )
    return pl.pallas_call(
        _moe_ffn_kernel,
        grid_spec=grid_spec,
        out_shape=jax.ShapeDtypeStruct(rows.shape, BF16),
        compiler_params=_params("arbitrary"),
        name="moe_ffn",
    )(tables["tile_expert"], tables["tiles_used"], rows, w1, w3, w2)


def _moe_combine_kernel(dst_ref, nchunk_ref, dense_ref, gates_ref, rank_ref, bounds_ref, g_ref, b_ref, rows_ref,
                        op_ref, os_ref, acc, buf, sem, *, prompt_tiles):
    j = pl.program_id(0)
    nblk = pl.num_programs(0)
    slot = j % 2
    nchunk_max = MOE_CAP // MOE_SEG

    def chunk_copy(blk, q, sl):
        row = pl.multiple_of(dst_ref[blk * nchunk_max + q], MOE_SEG)
        dst = buf.at[sl, pl.ds(pl.multiple_of(q * MOE_SEG, MOE_SEG), MOE_SEG), :]
        return pltpu.make_async_copy(rows_ref.at[pl.ds(row, MOE_SEG), :], dst, sem.at[sl])

    subs = range(MOE_CAP // MOE_SUB)
    always, rest = subs[:MOE_SUB_ALWAYS], subs[MOE_SUB_ALWAYS:]

    def sub_used(blk, sub):
        return True if sub < MOE_SUB_ALWAYS else sub * MOE_SUB < nchunk_ref[blk] * MOE_SEG

    def group(blk, sub, sl, action):
        for q in range(sub * (MOE_SUB // MOE_SEG), (sub + 1) * (MOE_SUB // MOE_SEG)):
            action(chunk_copy(blk, q, sl))

    @pl.when(j == 0)
    def _():
        buf[...] = jnp.zeros_like(buf)
        for sub in subs:
            @_when(sub_used(0, sub))
            def _():
                group(0, sub, 0, lambda c: c.start())

    for sub in subs:
        @_when(sub_used(j, sub))
        def _():
            group(j, sub, slot, lambda c: c.wait())

    bounds_t = jnp.concatenate([bounds_ref[0], jnp.zeros((LANES - SUBLANES, LANES), F32)], axis=0).T
    start, length = bounds_t[:, 0:1], bounds_t[:, 1:2]
    gates_rank = jnp.concatenate([gates_ref[...], rank_ref[...]], axis=0)

    def owners(sub):
        s = (sub * MOE_SUB + lax.broadcasted_iota(jnp.int32, (LANES, MOE_SUB), 1)).astype(F32)
        owner = (s >= start) & (s < start + length)
        within = s[0:1, :] - jnp.sum(jnp.where(owner, start, 0.0), axis=0, keepdims=True)
        return owner.astype(BF16), within

    def weights(owner_within):
        owner, within = owner_within
        hit = _dot(gates_rank, owner)
        return jnp.where(hit[MOE_BLOCK:] == within, hit[:MOE_BLOCK], 0.0).astype(BF16)

    def apply(sub, weight):
        return _dot(weight, buf[slot, sub * MOE_SUB:(sub + 1) * MOE_SUB, :])

    nxt = jnp.minimum(j + 1, nblk - 1)
    staged = [owners(sub) for sub in always]
    for sub in always:
        group(nxt, sub, 1 - slot, lambda c: c.start())
    staged = [weights(ow) for ow in staged]
    total = dense_ref[...]
    for sub, weight in zip(always, staged):
        total = total + apply(sub, weight)
    acc[...] = total
    for sub in rest:
        @pl.when((j + 1 < nblk) & sub_used(nxt, sub))
        def _():
            group(nxt, sub, 1 - slot, lambda c: c.start())

        @pl.when(sub_used(j, sub))
        def _():
            acc[...] += apply(sub, weights(owners(sub)))

    @pl.when(j == nblk - 1)
    def _():
        for sub in always:
            group(nxt, sub, 1 - slot, lambda c: c.wait())

    @pl.when(j < prompt_tiles)
    def _():
        op_ref[...] = _layer_norm(acc[...], g_ref[...], b_ref[...])

    @pl.when(j >= prompt_tiles)
    def _():
        os_ref[...] = _layer_norm(acc[...], g_ref[...], b_ref[...])


def _moe_combine(rows, dense, gates, rank, tables, g, b, n_prompt):
    m = dense.shape[0]
    nblk = m // MOE_BLOCK
    prompt_tiles = n_prompt // MOE_BLOCK
    in_prompt, in_sample = _group_maps(prompt_tiles)
    blk = lambda j, *_: (j, 0)
    const = lambda j, *_: (0, 0)
    grid_spec = pltpu.PrefetchScalarGridSpec(
        num_scalar_prefetch=2,
        grid=(nblk,),
        in_specs=[pl.BlockSpec((MOE_BLOCK, D_MODEL), blk), pl.BlockSpec((MOE_BLOCK, LANES), blk),
                  pl.BlockSpec((MOE_BLOCK, LANES), blk),
                  pl.BlockSpec((1, SUBLANES, LANES), lambda j, *_: (j, 0, 0)),
                  pl.BlockSpec((1, D_MODEL), const), pl.BlockSpec((1, D_MODEL), const),
                  pl.BlockSpec(memory_space=pl.ANY)],
        out_specs=[pl.BlockSpec((MOE_BLOCK, D_MODEL), in_prompt), pl.BlockSpec((MOE_BLOCK, D_MODEL), in_sample)],
        scratch_shapes=[pltpu.VMEM((MOE_BLOCK, D_MODEL), F32), pltpu.VMEM((2, MOE_CAP, D_MODEL), BF16),
                        pltpu.SemaphoreType.DMA((2,))],
    )
    return pl.pallas_call(
        functools.partial(_moe_combine_kernel, prompt_tiles=prompt_tiles),
        grid_spec=grid_spec,
        out_shape=[jax.ShapeDtypeStruct((n_prompt, D_MODEL), F32),
                   jax.ShapeDtypeStruct((m - n_prompt, D_MODEL), F32)],
        compiler_params=_params("arbitrary"),
        name="moe_combine",
    )(tables["src"], tables["nchunk"], dense, gates, rank, tables["bounds"], g, b, rows)


def _row(v):
    return v.reshape(1, -1).astype(F32)


def _scan_layout(batch, seq, row0):
    t = math.gcd(seq, CHUNK)
    nchunk = seq // t
    chunks = SCAN_CHUNKS if nchunk % SCAN_CHUNKS == 0 else 1
    seqs = SCAN_SEQS if (nchunk == 1 and batch % SCAN_SEQS == 0 and row0 % (SCAN_SEQS * t) == 0) else 1
    return dict(batch=batch, seq=seq, row0=row0, t=t, seqs=seqs, chunks=chunks)


def _lanes_at(v, start):
    return jnp.zeros((1, LANES), F32).at[0, start:start + v.shape[0]].set(v.astype(F32))


def kernel(x_prompt, x_sample, p_prompt, p_sample, state_ssm_conv, state_ssm, state_dn_conv, state_dn, emb_ln_g, emb_ln_b, w_in, conv_a_w, conv_a_b, ssm_dt_bias, ssm_a_log, ssm_d, ssm_norm_w, w_a, conv_b_w, dn_dt_bias, dn_a_log, dn_norm_w, w_b, w_o, ln1_g, ln1_b, router_w, router_bias, exp_w1, exp_w3, exp_w2, sh_w1, sh_w3, sh_w2, ple_w, ple_gate_w, ln2_g, ln2_b):
    bp, lp, _ = x_prompt.shape
    bs, ls, _ = x_sample.shape
    n_p = bp * lp
    n_s = bs * ls
    xp = x_prompt.reshape(n_p, D_MODEL)
    xs = x_sample.reshape(n_s, D_MODEL)

    w = w_in[0]
    o_z, o_xbc, o_dt = 0, SSM_INNER, SSM_INNER + SSM_CONV_CH
    o_qkv = o_dt + SSM_HEADS
    o_a = o_qkv + DN_CONV_CH
    o_b = o_a + DN_HEADS
    o_zb = o_b + DN_HEADS
    o_ga = o_zb + DN_V
    w_pa = w[:, o_z:o_dt].astype(BF16)
    w_pb = jnp.concatenate([w[:, o_qkv:o_a], w[:, o_zb:o_ga]], axis=1).astype(BF16)
    w_pg = w[:, o_ga:].astype(BF16)
    w_ps = jnp.zeros((D_MODEL, LANES), F32)
    w_ps = w_ps.at[:, SMALL_DT:SMALL_DT + SSM_HEADS].set(w[:, o_dt:o_qkv])
    w_ps = w_ps.at[:, SMALL_A:SMALL_A + 2 * DN_HEADS].set(w[:, o_a:o_zb]).astype(BF16)

    eg, eb = _row(emb_ln_g), _row(emb_ln_b)
    proj_a, small = _ln_matmul(xp, xs, eg, eb, w_pa, PROJ_TM, PROJ_TN_A, w_ps)
    proj_b = _ln_matmul(xp, xs, eg, eb, w_pb, PROJ_TM, PROJ_TN_B)

    ssd_consts = [conv_a_w[0], _row(conv_a_b[0]), _lanes_at(ssm_dt_bias[0], SMALL_DT),
                  _lanes_at(ssm_a_log[0], SMALL_DT), _row(jnp.repeat(ssm_d[0], SSM_HEAD_DIM)),
                  _row(ssm_norm_w[0])]
    gdn_consts = [conv_b_w[0], _lanes_at(dn_a_log[0], SMALL_A), _lanes_at(dn_dt_bias[0], SMALL_A),
                  _row(dn_norm_w[0])]
    zeros = lambda *s: jnp.zeros(s, F32)
    prompt = _scan_layout(bp, lp, 0)
    sample = _scan_layout(bs, ls, n_p)

    ya_p, pa_conv, pa_ssm = _ssd(proj_a, small, ssd_consts, zeros(bp, CONV_WIDTH - 1, SSM_CONV_CH),
                                 zeros(bp, SSM_HEADS, SSM_HEAD_DIM, SSM_STATE), **prompt)
    ya_s, sa_conv, sa_ssm = _ssd(proj_a, small, ssd_consts, state_ssm_conv[0], state_ssm[0], **sample)
    ob_p, pb_conv, pb_dn = _gdn(proj_b, small, gdn_consts, zeros(bp, CONV_WIDTH - 1, DN_CONV_CH),
                                zeros(bp, DN_HEADS, DN_HEAD, DN_HEAD), **prompt)
    ob_s, sb_conv, sb_dn = _gdn(proj_b, small, gdn_consts, state_dn_conv[0], state_dn[0], **sample)

    x1, x1b = _merge(
        xp, xs, ya_p, ya_s, ob_p, ob_s,
        [eg, eb, w_pg, w_a[0].astype(BF16), w_b[0].astype(BF16), w_o[0].astype(BF16), _row(ln1_g[0]), _row(ln1_b[0])],
        MERGE_TM)

    router_w_t = jnp.zeros((LANES, D_MODEL), F32).at[:N_EXPERTS].set(router_w[0].T)
    router_w_hi = router_w_t.astype(BF16)
    router_w_lo = (router_w_t - router_w_hi.astype(F32)).astype(BF16)
    router_b = jnp.broadcast_to(_lanes_at(router_bias[0], 0).reshape(LANES, 1), (LANES, MOE_BLOCK))
    dense, gates, rank, chosen_t, rank_t, counts = _dense_router(
        x1, x1b, p_prompt[0].reshape(n_p, PLE_DIM), p_sample[0].reshape(n_s, PLE_DIM),
        [sh_w1[0].astype(BF16), sh_w3[0].astype(BF16), sh_w2[0].astype(BF16), ple_w[0].astype(BF16),
         ple_gate_w[0].astype(BF16), router_w_hi, router_w_lo, router_b])

    tables = _moe_tables(counts[:, 0, :N_EXPERTS].astype(jnp.int32), MOE_TM)
    sorted_rows = _moe_gather(x1b, chosen_t, rank_t, tables, MOE_TM)
    expert_out = _moe_ffn(sorted_rows, exp_w1[0], exp_w3[0], exp_w2[0], tables, MOE_TM)
    out_p, out_s = _moe_combine(expert_out, dense, gates, rank, tables, _row(ln2_g[0]), _row(ln2_b[0]), n_p)

    return (out_p.reshape(bp, lp, D_MODEL), out_s.reshape(bs, ls, D_MODEL),
            pa_conv[None], pa_ssm[None], pb_conv[None], pb_dn[None],
            sa_conv[None], sa_ssm[None], sb_conv[None], sb_dn[None])
```

```python
import functools
import math

import jax
import jax.numpy as jnp
from jax import lax
from jax.experimental import pallas as pl
from jax.experimental.pallas import tpu as pltpu

F32 = jnp.float32
BF16 = jnp.bfloat16
HIGHEST = lax.Precision.HIGHEST

D_MODEL = 1024
SSM_INNER = 2048
SSM_HEAD_DIM = 64
SSM_HEADS = 32
SSM_GROUPS = 2
SSM_HEADS_PER_GROUP = 16
SSM_STATE = 128
SSM_CONV_CH = 2560
DN_HEADS = 8
DN_HEAD = 128
DN_QK = 1024
DN_V = 1024
DN_CONV_CH = 3072
CONV_WIDTH = 4
CHUNK = 64
N_EXPERTS = 64
TOP_K = 8
N_EXPERT_GROUPS = 8
EXPERTS_PER_GROUP = 8
TOPK_GROUPS = 4
EXPERT_FF = 256
SHARED_FF = 256
ROUTED_SCALE = 2.5
PLE_DIM = 256
LN_EPS = 1e-5
RMS_EPS = 1e-6
L2_EPS = 1e-6
DEEPNORM_ALPHA = 2.0 ** 0.25

LANES = 128
SUBLANES = 8
VMEM_LIMIT = 56 * 1024 * 1024
SMALL_DT = 0
SMALL_A = 32
SMALL_B = 40
PROJ_TM = 1024
PROJ_TN_A = 2304
PROJ_TN_B = 2048
MERGE_TM = 512
SCAN_CHUNKS = 8
SCAN_SEQS = 8
MOE_BLOCK = 256
MOE_SEG = 16
MOE_SUB = 512
MOE_CAP = -(-(MOE_BLOCK * TOP_K + N_EXPERTS * (MOE_SEG - 1)) // MOE_SUB) * MOE_SUB
MOE_TM = 512
ROUTER_TM = 512
MOE_SUB_ALWAYS = -(-(MOE_BLOCK * TOP_K + N_EXPERTS * MOE_SEG // 2) // MOE_SUB)


def _sigmoid(x):
    return 1.0 / (1.0 + jnp.exp(-x))


def _silu(x):
    return x * _sigmoid(x)


def _softplus(x):
    return jnp.maximum(x, 0.0) + jnp.log(1.0 + jnp.exp(-jnp.abs(x)))


def _layer_norm(x, g, b):
    mu = jnp.mean(x, axis=-1, keepdims=True)
    xc = x - mu
    var = jnp.mean(xc * xc, axis=-1, keepdims=True)
    return xc * lax.rsqrt(var + LN_EPS) * g + b


def _dot(a, b):
    return jnp.dot(a, b, preferred_element_type=F32)


def _dot_nt(a, b):
    return lax.dot_general(a, b, (((1,), (1,)), ((), ())), preferred_element_type=F32)


def _dot_tn(a, b):
    return lax.dot_general(a, b, (((0,), (0,)), ((), ())), preferred_element_type=F32)


def _dot_f32(a, b):
    return jnp.dot(a, b, precision=HIGHEST, preferred_element_type=F32)


def _params(*sem):
    return pltpu.CompilerParams(dimension_semantics=sem, vmem_limit_bytes=VMEM_LIMIT)


def _when(cond):
    if cond is True:
        return lambda fn: fn()
    if cond is False:
        return lambda fn: None
    return pl.when(cond)


def _row_tile(m, preferred):
    return max(d for d in range(SUBLANES, min(m, preferred) + 1, SUBLANES) if m % d == 0)


def _group_maps(prompt_tiles):
    in_prompt = lambda i, *_: (jnp.minimum(i, prompt_tiles - 1), 0)
    in_sample = lambda i, *_: (jnp.maximum(i - prompt_tiles, 0), 0)
    return in_prompt, in_sample


def _ln_matmul_kernel(xp_ref, xs_ref, g_ref, b_ref, w_ref, *rest, prompt_tiles, narrow):
    if narrow:
        wn_ref, o_ref, on_ref, h_scr = rest
    else:
        o_ref, h_scr = rest

    @pl.when(pl.program_id(1) == 0)
    def _():
        x = jnp.where(pl.program_id(0) < prompt_tiles, xp_ref[...], xs_ref[...])
        h_scr[...] = _layer_norm(x, g_ref[...], b_ref[...]).astype(BF16)
        if narrow:
            on_ref[...] = _dot(h_scr[...], wn_ref[...])

    o_ref[...] = _dot(h_scr[...], w_ref[...]).astype(o_ref.dtype)


def _ln_matmul(xp, xs, g, b, w, tm, tn, w_narrow=None):
    k = xp.shape[1]
    m = xp.shape[0] + xs.shape[0]
    n = w.shape[1]
    tm = _row_tile(math.gcd(xp.shape[0], xs.shape[0]), tm)
    prompt_tiles = xp.shape[0] // tm
    in_prompt, in_sample = _group_maps(prompt_tiles)
    narrow = w_narrow is not None
    const = lambda i, j: (0, 0)
    row = lambda i, j: (i, 0)
    in_specs = [pl.BlockSpec((tm, k), in_prompt), pl.BlockSpec((tm, k), in_sample),
                pl.BlockSpec((1, k), const), pl.BlockSpec((1, k), const), pl.BlockSpec((k, tn), lambda i, j: (0, j))]
    out_specs = [pl.BlockSpec((tm, tn), lambda i, j: (i, j))]
    out_shape = [jax.ShapeDtypeStruct((m, n), BF16)]
    args = [xp, xs, g, b, w]
    if narrow:
        in_specs.append(pl.BlockSpec((k, LANES), const))
        out_specs.append(pl.BlockSpec((tm, LANES), row))
        out_shape.append(jax.ShapeDtypeStruct((m, LANES), F32))
        args.append(w_narrow)
    out = pl.pallas_call(
        functools.partial(_ln_matmul_kernel, prompt_tiles=prompt_tiles, narrow=narrow),
        grid=(m // tm, n // tn),
        in_specs=in_specs,
        out_specs=out_specs,
        out_shape=out_shape,
        scratch_shapes=[pltpu.VMEM((tm, k), BF16)],
        compiler_params=_params("parallel", "arbitrary"),
        name="ln_matmul",
    )(*args)
    return out if narrow else out[0]


def _load_conv_state(cbuf, state):
    taps = CONV_WIDTH - 1
    cbuf[0:SUBLANES - taps, :] = jnp.zeros((SUBLANES - taps, cbuf.shape[1]), F32)
    cbuf[SUBLANES - taps:SUBLANES, :] = state


def _causal_conv(cbuf, x16, cw, t):
    ch = x16.shape[1]
    prev = cbuf[...]
    taps = CONV_WIDTH - 1
    if t <= SUBLANES:
        x = x16.astype(F32)
        ext = jnp.concatenate([prev, x], axis=0)
        lo = SUBLANES - taps
        y = ext[lo:lo + t] * cw[0:1]
        y = y + ext[lo + 1:lo + 1 + t] * cw[1:2]
        y = y + ext[lo + 2:lo + 2 + t] * cw[2:3]
        y = y + x * cw[3:4]
        cbuf[...] = ext[t:t + SUBLANES]
        return y, ext[SUBLANES + t - taps:SUBLANES + t]
    hi = prev.astype(BF16).astype(F32)
    mid = (prev - hi).astype(BF16).astype(F32)
    low = ((prev - hi) - mid).astype(BF16).astype(F32)
    x = x16.astype(F32)
    pack = 2 * SUBLANES
    nhead = 4 * SUBLANES if t % pack == 0 else 5 * SUBLANES
    head = jnp.concatenate([hi, mid, low, jnp.zeros((nhead - 3 * SUBLANES, ch), F32)], axis=0)
    if t % pack == 0:
        ext = jnp.concatenate([head.astype(BF16), x16], axis=0)
    else:
        ext = jnp.concatenate([head, x], axis=0).astype(BF16)
    out_row = lax.broadcasted_iota(jnp.int32, (taps * t, nhead + t), 0)
    col = lax.broadcasted_iota(jnp.int32, (taps * t, nhead + t), 1)
    log_t = t.bit_length() - 1
    shift = lax.shift_right_logical(out_row, log_t) + 1
    src = (out_row & (t - 1)) - shift
    in_x = (src >= 0) & (col == nhead + src)
    in_prev = (src < 0) & (col < 3 * SUBLANES) & ((col & (SUBLANES - 1)) == SUBLANES + src)
    shifted = _dot((in_x | in_prev).astype(BF16), ext)
    y = shifted[2 * t:3 * t] * cw[0:1]
    y = y + shifted[t:2 * t] * cw[1:2]
    y = y + shifted[0:t] * cw[2:3]
    y = y + x * cw[3:4]
    cbuf[...] = x[t - SUBLANES:t]
    return y, x[t - taps:t]


def _lane_expand(v, h0, count, width):
    t = v.shape[0]
    n = count * width
    out = jnp.broadcast_to(v[:, h0:h0 + 1], (t, n))
    if count > 1:
        lane = lax.broadcasted_iota(jnp.int32, (t, n), 1)
        for i in range(1, count):
            out = jnp.where(lane >= i * width, jnp.broadcast_to(v[:, h0 + i:h0 + i + 1], (t, n)), out)
    return out


def _grouped_transpose(v, per_tile):
    t = v.shape[0]
    blocks = [v if r == 0 else pltpu.roll(v, LANES - r, axis=1) for r in range(per_tile)]
    if per_tile * t < LANES:
        blocks.append(jnp.zeros((LANES - per_tile * t, LANES), F32))
    return jnp.concatenate(blocks, axis=0).T


def _same_block(i, j, size):
    shift = size.bit_length() - 1
    return lax.shift_right_logical(i, shift) == lax.shift_right_logical(j, shift)


def _unit_lower_inverses(lmats, ii, jj, t):
    mm = lambda a, b: _dot(a.astype(BF16), b.astype(BF16))
    base = min(16, t)
    in_base = _same_block(ii, jj, base)
    eye = (ii == jj).astype(F32)
    power = [jnp.where(in_base, l, 0.0) for l in lmats]
    inv = [eye - p for p in power]
    span = 2
    while span < base:
        power = [mm(p, p) for p in power]
        inv = [a + mm(a, p) for a, p in zip(inv, power)]
        span *= 2
    size = base
    while size < t:
        link = _same_block(ii, jj, 2 * size) & jnp.logical_not(_same_block(ii, jj, size))
        cross = [mm(a, jnp.where(link, l, 0.0)) for a, l in zip(inv, lmats)]
        inv = [a - mm(c, a) for a, c in zip(inv, cross)]
        size *= 2
    return inv


def _row_sums_of_squares(blocks):
    t = blocks[0].shape[0]
    sq = jnp.concatenate([b * b for b in blocks], axis=0)
    hi = sq.astype(BF16)
    lo = (sq - hi.astype(F32)).astype(BF16)
    ones = jnp.ones((LANES, LANES), BF16)
    sums = _dot(hi, ones) + _dot(lo, ones)
    return [sums[i * t:(i + 1) * t] for i in range(len(blocks))]


def _cumsum_rows(v, t):
    ii = lax.broadcasted_iota(jnp.int32, (t, t), 0)
    jj = lax.broadcasted_iota(jnp.int32, (t, t), 1)
    return _dot_f32((jj <= ii).astype(F32), v)


def _ssd_kernel(pa_ref, sm_ref, cw_ref, cb_ref, dtb_ref, alog_ref, dskip_ref, nw_ref, cst_ref, h0_ref,
                y_ref, cnew_ref, hnew_ref, cbuf, s_scr, *, t, first, last):
    per_tile = LANES // t
    width = per_tile * SSM_HEAD_DIM

    @_when(first)
    def _():
        _load_conv_state(cbuf, cst_ref[0])
        s_scr[...] = h0_ref[0]

    z = pa_ref[:, :SSM_INNER].astype(F32)
    conv, tail = _causal_conv(cbuf, pa_ref[:, SSM_INNER:], cw_ref[...], t)

    @_when(last)
    def _():
        cnew_ref[0] = tail

    xbc = _silu(conv + cb_ref[...])
    xs = xbc[:, :SSM_INNER]
    bm = xbc[:, SSM_INNER:SSM_INNER + SSM_GROUPS * SSM_STATE]
    cm = xbc[:, SSM_INNER + SSM_GROUPS * SSM_STATE:]

    dt = _softplus(sm_ref[...] + dtb_ref[...])
    da = dt * (-jnp.exp(alog_ref[...]))
    cum = _cumsum_rows(da, t)
    cum_t = _grouped_transpose(cum, per_tile)
    dt_t = _grouped_transpose(dt, per_tile)

    row = lax.broadcasted_iota(jnp.int32, (t, LANES), 0)
    lane = lax.broadcasted_iota(jnp.int32, (t, LANES), 1)
    causal = (lane % t) <= row
    brow = lax.broadcasted_iota(jnp.int32, (LANES, width), 0)
    bcol = lax.broadcasted_iota(jnp.int32, (LANES, width), 1)
    blockdiag = (brow // t) == (bcol // SSM_HEAD_DIM)

    ys = []
    for g in range(SSM_GROUPS):
        bg = bm[:, g * SSM_STATE:(g + 1) * SSM_STATE]
        cg = cm[:, g * SSM_STATE:(g + 1) * SSM_STATE].astype(BF16)
        cb = _dot_nt(cg, jnp.concatenate([bg] * per_tile, axis=0).astype(BF16))
        bg = bg.astype(BF16)
        for q in range(SSM_HEADS_PER_GROUP // per_tile):
            h0 = g * SSM_HEADS_PER_GROUP + q * per_tile
            ccol = _lane_expand(cum, h0, per_tile, t)
            seg = ccol - cum_t[h0:h0 + 1, :]
            decay = jnp.exp(jnp.where(causal, seg, -jnp.inf))
            wts = (cb * decay * dt_t[h0:h0 + 1, :]).astype(BF16)
            xt = xs[:, h0 * SSM_HEAD_DIM:h0 * SSM_HEAD_DIM + width]
            xbd = jnp.where(blockdiag, jnp.concatenate([xt] * per_tile, axis=0), 0.0).astype(BF16)
            y_intra = _dot(wts, xbd)
            st = s_scr[h0:h0 + per_tile].reshape(width, SSM_STATE)
            cum_w = ccol if width == LANES else _lane_expand(cum, h0, per_tile, SSM_HEAD_DIM)
            y_state = _dot_nt(cg, st.astype(BF16)) * jnp.exp(cum_w)
            ys.append(y_intra + y_state)
            wend = jnp.exp(cum_w[t - 1:t, :] - cum_w) * _lane_expand(dt, h0, per_tile, SSM_HEAD_DIM)
            ds = _dot_tn((xt * wend).astype(BF16), bg)
            for i in range(per_tile):
                h = h0 + i
                s_scr[h] = jnp.exp(cum[t - 1:t, h:h + 1]) * s_scr[h] + ds[i * SSM_HEAD_DIM:(i + 1) * SSM_HEAD_DIM]

    y = jnp.concatenate(ys, axis=1) + dskip_ref[...] * xs
    y = y * _silu(z)
    half = SSM_INNER // SSM_GROUPS
    normed = []
    for g in range(SSM_GROUPS):
        yg = y[:, g * half:(g + 1) * half]
        normed.append(yg * lax.rsqrt(jnp.mean(yg * yg, axis=-1, keepdims=True) + RMS_EPS))
    y_ref[...] = (jnp.concatenate(normed, axis=1) * nw_ref[...]).astype(BF16)

    @_when(last)
    def _():
        hnew_ref[0] = s_scr[...]


def _scan_step_kernel(chunk_kernel, n_tok, n_const, n_state, seqs, chunks, t):
    def step(*refs):
        tok = refs[:n_tok]
        const = refs[n_tok:n_tok + n_const]
        st_in = refs[n_tok + n_const:n_tok + n_const + n_state]
        y_ref = refs[n_tok + n_const + n_state]
        st_out = refs[n_tok + n_const + n_state + 1:n_tok + n_const + 2 * n_state + 1]
        scratch = refs[n_tok + n_const + 2 * n_state + 1:]
        first_step = pl.program_id(1) == 0
        last_step = pl.program_id(1) == pl.num_programs(1) - 1
        for s in range(seqs):
            for c in range(chunks):
                rows = pl.ds((s * chunks + c) * t, t)
                chunk_kernel(*[r.at[rows, :] for r in tok], *const, *[r.at[pl.ds(s, 1)] for r in st_in],
                             y_ref.at[rows, :], *[r.at[pl.ds(s, 1)] for r in st_out], *scratch,
                             first=first_step if c == 0 else False,
                             last=last_step if c == chunks - 1 else False)
    return step


def _scan_call(kernel, name, tok_in, const_in, state_in, y_width, scratch, *, batch, seq, row0, t, seqs, chunks,
               step_kernel=None):
    nstep = seq // (t * chunks)
    rows = seqs * chunks * t
    blk0 = row0 // rows

    def per_batch(a):
        zeros = (0,) * (a.ndim - 1)
        return pl.BlockSpec((seqs,) + a.shape[1:], lambda b, c: (b,) + zeros)

    if step_kernel is None:
        step_kernel = _scan_step_kernel(kernel, len(tok_in), len(const_in), len(state_in), seqs, chunks, t)
    return pl.pallas_call(
        step_kernel,
        grid=(batch // seqs, nstep),
        in_specs=[pl.BlockSpec((rows, a.shape[1]), lambda b, c: (blk0 + b * nstep + c, 0)) for a in tok_in]
        + [pl.BlockSpec(a.shape, lambda b, c: (0, 0)) for a in const_in]
        + [per_batch(a) for a in state_in],
        out_specs=[pl.BlockSpec((rows, y_width), lambda b, c: (b * nstep + c, 0))]
        + [per_batch(a) for a in state_in],
        out_shape=[jax.ShapeDtypeStruct((batch * seq, y_width), BF16)]
        + [jax.ShapeDtypeStruct(a.shape, F32) for a in state_in],
        scratch_shapes=scratch,
        compiler_params=_params("parallel", "arbitrary"),
        name=name,
    )(*tok_in, *const_in, *state_in)


def _ssd(proj_a, small, consts, conv_state, h0, **where):
    t = where["t"]
    scratch = [pltpu.VMEM((SUBLANES, SSM_CONV_CH), F32),
               pltpu.VMEM((SSM_HEADS, SSM_HEAD_DIM, SSM_STATE), F32)]
    return _scan_call(functools.partial(_ssd_kernel, t=t), f"ssd_t{t}", [proj_a, small], consts,
                      [conv_state, h0], SSM_INNER, scratch, **where)


def _gdn_step_kernel(pb_ref, sm_ref, cw_ref, alog_ref, dtb_ref, nw_ref, cst_ref, s0_ref,
                     o_ref, cnew_ref, snew_ref, cbuf, s_scr, *, t, seqs, chunks):
    first_step = pl.program_id(1) == 0
    last_step = pl.program_id(1) == pl.num_programs(1) - 1
    units = [(s, c) for s in range(seqs) for c in range(chunks)]
    rows = lambda u: pl.ds(u * t, t)
    per_tile = min(LANES // t, DN_HEADS)
    n = per_tile * t
    ntile = DN_HEADS // per_tile
    heads = range(DN_HEADS)
    ii = lax.broadcasted_iota(jnp.int32, (n, n), 0)
    jj = lax.broadcasted_iota(jnp.int32, (n, n), 1)
    causal = _same_block(ii, jj, t) & (jj <= ii)
    diag = ii == jj
    stack = lambda xs, p: jnp.concatenate(xs[p * per_tile:(p + 1) * per_tile], axis=0)

    qkv, zb, gcum, beta = [], [], [], []
    for u, (s, c) in enumerate(units):
        @_when(first_step if c == 0 else False)
        def _():
            _load_conv_state(cbuf, cst_ref[s])

        conv, tail = _causal_conv(cbuf, pb_ref[rows(u), :DN_CONV_CH], cw_ref[...], t)

        @_when(last_step if c == chunks - 1 else False)
        def _():
            cnew_ref[s] = tail

        qkv.append(_silu(conv))
        zb.append(pb_ref[rows(u), DN_CONV_CH:].astype(F32))
        sm = sm_ref[rows(u), :]
        beta.append(_sigmoid(sm))
        gate = -jnp.exp(alog_ref[...]) * _softplus(sm + dtb_ref[...])
        gcum.append(_cumsum_rows(gate, t))
    gcum_t = [_grouped_transpose(g, per_tile) for g in gcum]

    sumsq = _row_sums_of_squares([x[:, i * DN_HEAD:(i + 1) * DN_HEAD] for x in qkv for i in range(2 * DN_HEADS)])
    q, k, v, gcol, bcol, kb, egcol = ([[] for _ in units] for _ in range(7))
    for u in range(len(units)):
        for h in heads:
            lo = h * DN_HEAD
            ss = sumsq[u * 2 * DN_HEADS:(u + 1) * 2 * DN_HEADS]
            q[u].append(qkv[u][:, lo:lo + DN_HEAD] * lax.rsqrt(ss[h] + L2_EPS) * (DN_HEAD ** -0.5))
            k[u].append(qkv[u][:, DN_QK + lo:DN_QK + lo + DN_HEAD] * lax.rsqrt(ss[DN_HEADS + h] + L2_EPS))
            v[u].append(qkv[u][:, 2 * DN_QK + lo:2 * DN_QK + lo + DN_HEAD])
            gcol[u].append(gcum[u][:, SMALL_A + h:SMALL_A + h + 1])
            bcol[u].append(beta[u][:, SMALL_B + h:SMALL_B + h + 1])
            kb[u].append(k[u][h] * bcol[u][h])
            egcol[u].append(jnp.exp(gcol[u][h]))

    tiles = [(u, p) for u in range(len(units)) for p in range(ntile)]
    k16 = [stack(k[u], p).astype(BF16) for u, p in tiles]
    decay = []
    for u, p in tiles:
        h0 = p * per_tile
        seg = stack(gcol[u], p) - gcum_t[u][SMALL_A + h0:SMALL_A + h0 + 1, :n]
        decay.append(jnp.exp(jnp.where(causal, seg, -jnp.inf)))
    lmat = [_dot_nt(stack(kb[u], p).astype(BF16), k16[i]) * jnp.where(diag, 0.0, decay[i])
            for i, (u, p) in enumerate(tiles)]
    qk = [(_dot_nt(stack(q[u], p).astype(BF16), k16[i]) * decay[i]).astype(BF16) for i, (u, p) in enumerate(tiles)]
    inv = [a.astype(BF16) for a in _unit_lower_inverses(lmat, ii, jj, t)]

    split = lambda x: (x[:, :DN_HEAD], x[:, DN_HEAD:])
    o_from, o_own, s_from, s_own, glast = ([[] for _ in units] for _ in range(5))
    aw = [_dot(inv[i], jnp.concatenate(
        [stack([kb[u][h] * egcol[u][h] for h in heads], p), stack([v[u][h] * bcol[u][h] for h in heads], p)],
        axis=1).astype(BF16)) for i, (u, p) in enumerate(tiles)]
    qaw = [_dot(qk[i], aw[i].astype(BF16)) for i in range(len(tiles))]
    pad_rows = -t % (2 * SUBLANES)
    pad = [jnp.zeros((pad_rows, DN_HEAD), F32)] if pad_rows else []
    for u in range(len(units)):
        for h in heads:
            p, j = divmod(h, per_tile)
            i = u * ntile + p
            sel = slice(j * t, (j + 1) * t)
            qkw, qku0 = split(qaw[i][sel])
            o_from[u].append(q[u][h] * egcol[u][h] - qkw)
            o_own[u].append(qku0)
            glast[u].append(gcum[u][t - 1:t, SMALL_A + h:SMALL_A + h + 1])
            kd = (k[u][h] * jnp.exp(glast[u][h] - gcol[u][h])).astype(BF16)
            kdw, kdu0 = split(_dot_tn(kd, aw[i][sel].astype(BF16)))
            s_from[u].append(kdw)
            s_own[u].append(kdu0)

    outs = []
    for u, (s, c) in enumerate(units):
        @_when(first_step if c == 0 else False)
        def _():
            s_scr[...] = s0_ref[s]

        for h in heads:
            state = s_scr[h]
            both = _dot(jnp.concatenate([s_from[u][h], o_from[u][h]] + pad, axis=0).astype(BF16), state.astype(BF16))
            outs.append(both[DN_HEAD:DN_HEAD + t] + o_own[u][h])
            s_scr[h] = jnp.exp(glast[u][h]) * state - both[:DN_HEAD] + s_own[u][h]

        @_when(last_step if c == chunks - 1 else False)
        def _():
            snew_ref[s] = s_scr[...]

    norm_w = nw_ref[...]
    osq = _row_sums_of_squares(outs)
    for u in range(len(units)):
        normed = []
        for h in heads:
            i = u * DN_HEADS + h
            normed.append(outs[i] * lax.rsqrt(osq[i] * (1.0 / DN_HEAD) + RMS_EPS) * norm_w
                          * _silu(zb[u][:, h * DN_HEAD:(h + 1) * DN_HEAD]))
        o_ref[rows(u), :] = jnp.concatenate(normed, axis=1).astype(BF16)


def _gdn(proj_b, small, consts, conv_state, s0, **where):
    t = where["t"]
    scratch = [pltpu.VMEM((SUBLANES, DN_CONV_CH), F32),
               pltpu.VMEM((DN_HEADS, DN_HEAD, DN_HEAD), F32)]
    step = functools.partial(_gdn_step_kernel, t=t, seqs=where["seqs"], chunks=where["chunks"])
    return _scan_call(None, f"gdn_t{t}", [proj_b, small], consts, [conv_state, s0], DN_V, scratch,
                      step_kernel=step, **where)


def _merge_kernel(xp_ref, xs_ref, yap_ref, yas_ref, obp_ref, obs_ref, eg_ref, eb_ref, wg_ref, wa_ref, wb_ref,
                  wo_ref, g_ref, b_ref, x1_ref, x1b_ref, *, prompt_tiles):
    is_prompt = pl.program_id(0) < prompt_tiles
    h = _layer_norm(jnp.where(is_prompt, xp_ref[...], xs_ref[...]), eg_ref[...], eb_ref[...])
    out_a = _dot(jnp.where(is_prompt, yap_ref[...], yas_ref[...]), wa_ref[...])
    out_b = _dot(jnp.where(is_prompt, obp_ref[...], obs_ref[...]), wb_ref[...])
    gt = _dot(h.astype(BF16), wg_ref[...])
    merged = _sigmoid(gt[:, :D_MODEL]) * out_a + _sigmoid(gt[:, D_MODEL:]) * out_b
    mix = _dot(merged.astype(BF16), wo_ref[...])
    x1 = _layer_norm(DEEPNORM_ALPHA * h + mix, g_ref[...], b_ref[...])
    x1_ref[...] = x1
    x1b_ref[...] = x1.astype(BF16)


def _merge(xp, xs, ya_p, ya_s, ob_p, ob_s, consts, tm):
    m = xp.shape[0] + xs.shape[0]
    n_p = xp.shape[0]
    tm = _row_tile(math.gcd(n_p, m - n_p), tm)
    in_prompt, in_sample = _group_maps(n_p // tm)
    row = lambda i: (i, 0)
    const = lambda i: (0, 0)
    return pl.pallas_call(
        functools.partial(_merge_kernel, prompt_tiles=n_p // tm),
        grid=(m // tm,),
        in_specs=[pl.BlockSpec((tm, D_MODEL), in_prompt), pl.BlockSpec((tm, D_MODEL), in_sample),
                  pl.BlockSpec((tm, SSM_INNER), in_prompt), pl.BlockSpec((tm, SSM_INNER), in_sample),
                  pl.BlockSpec((tm, DN_V), in_prompt), pl.BlockSpec((tm, DN_V), in_sample)]
        + [pl.BlockSpec(a.shape, const) for a in consts],
        out_specs=[pl.BlockSpec((tm, D_MODEL), row), pl.BlockSpec((tm, D_MODEL), row)],
        out_shape=[jax.ShapeDtypeStruct((m, D_MODEL), F32), jax.ShapeDtypeStruct((m, D_MODEL), BF16)],
        compiler_params=_params("parallel"),
        name="merge",
    )(xp, xs, ya_p, ya_s, ob_p, ob_s, *consts)


def _take_first_max(rest, index, limit, axis):
    best = jnp.max(rest, axis=axis, keepdims=True)
    first = jnp.min(jnp.where(rest == best, index, limit), axis=axis, keepdims=True)
    hit = index == first
    return hit, jnp.where(hit, -jnp.inf, rest)


def _router_gates_t(x1, xb, rw_hi, rw_lo, bias):
    tm = x1.shape[0]
    x_lo = (x1 - xb.astype(F32)).astype(BF16)
    logits = _dot_nt(rw_hi, xb) + (_dot_nt(rw_hi, x_lo) + _dot_nt(rw_lo, xb))
    scores = _sigmoid(logits)
    row = lax.broadcasted_iota(jnp.int32, (LANES, tm), 0)
    sel = jnp.where(row < N_EXPERTS, scores + bias, -jnp.inf)
    by_group = sel.reshape(LANES // EXPERTS_PER_GROUP, EXPERTS_PER_GROUP, tm)
    sub = lax.broadcasted_iota(jnp.int32, by_group.shape, 1)
    top1 = jnp.max(by_group, axis=1, keepdims=True)
    _, others = _take_first_max(by_group, sub, EXPERTS_PER_GROUP, 1)
    gscore = (top1 + jnp.max(others, axis=1, keepdims=True))[:N_EXPERT_GROUPS]
    gidx = lax.broadcasted_iota(jnp.int32, gscore.shape, 0)
    gkeep = jnp.zeros(gscore.shape, jnp.bool_)
    for _ in range(TOPK_GROUPS):
        hit, gscore = _take_first_max(gscore, gidx, N_EXPERT_GROUPS, 0)
        gkeep = gkeep | hit
    gkeep = jnp.broadcast_to(gkeep, (N_EXPERT_GROUPS, EXPERTS_PER_GROUP, tm)).reshape(N_EXPERTS, tm)
    rest = jnp.where(gkeep, sel[:N_EXPERTS], -jnp.inf)
    eidx = lax.broadcasted_iota(jnp.int32, (N_EXPERTS, tm), 0)
    keep = jnp.zeros((N_EXPERTS, tm), jnp.bool_)
    for _ in range(TOP_K):
        hit, rest = _take_first_max(rest, eidx, N_EXPERTS, 0)
        keep = keep | hit
    picked = jnp.where(keep, scores[:N_EXPERTS], 0.0)
    gates = picked / jnp.sum(picked, axis=0, keepdims=True) * ROUTED_SCALE
    return jnp.concatenate([gates, jnp.zeros((LANES - N_EXPERTS, tm), F32)], axis=0)


def _dense_router_kernel(x1_ref, x1b_ref, pp_ref, ps_ref, sw1_ref, sw3_ref, sw2_ref, pw_ref, pg_ref, rwh_ref,
                         rwl_ref, rb_ref, dense_ref, gates_ref, rank_ref, chosen_t_ref, rank_t_ref, count_ref,
                         *, prompt_tiles):
    xb = x1b_ref[...]
    hid = _silu(_dot(xb, sw1_ref[...])) * _dot(xb, sw3_ref[...])
    shared = _dot(hid.astype(BF16), sw2_ref[...])
    p = jnp.where(pl.program_id(0) < prompt_tiles, pp_ref[...], ps_ref[...])
    ple = _dot(p.astype(BF16), pw_ref[...]) * _sigmoid(_dot(xb, pg_ref[...]))
    x1 = x1_ref[...]
    dense_ref[...] = DEEPNORM_ALPHA * x1 + shared + ple
    ii = lax.broadcasted_iota(jnp.int32, (MOE_BLOCK, MOE_BLOCK), 0)
    jj = lax.broadcasted_iota(jnp.int32, (MOE_BLOCK, MOE_BLOCK), 1)
    earlier = (ii < jj).astype(BF16)
    for blk in range(x1.shape[0] // MOE_BLOCK):
        tok = slice(blk * MOE_BLOCK, (blk + 1) * MOE_BLOCK)
        exp = slice(blk * LANES, (blk + 1) * LANES)
        gates_t = _router_gates_t(x1[tok], xb[tok], rwh_ref[...], rwl_ref[...], rb_ref[...])
        chosen_t = (gates_t > 0.0).astype(BF16)
        rank_t = _dot(chosen_t, earlier)
        chosen_t_ref[exp, :] = chosen_t
        rank_t_ref[exp, :] = rank_t.astype(BF16)
        gates = gates_t.T
        rank = rank_t.T
        gates_ref[tok, :] = gates.astype(BF16)
        rank_ref[tok, :] = rank.astype(BF16)
        count = rank[MOE_BLOCK - 1:, :] + (gates[MOE_BLOCK - 1:, :] > 0.0).astype(F32)
        count_ref[blk] = jnp.broadcast_to(count, (SUBLANES, LANES))


def _dense_router(x1, x1b, p_prompt, p_sample, consts):
    m = x1.shape[0]
    tm = _row_tile(math.gcd(p_prompt.shape[0], p_sample.shape[0]), ROUTER_TM)
    assert tm % MOE_BLOCK == 0
    nblk = m // MOE_BLOCK
    per_step = tm // MOE_BLOCK
    prompt_tiles = p_prompt.shape[0] // tm
    in_prompt, in_sample = _group_maps(prompt_tiles)
    row = lambda i: (i, 0)
    const = lambda i: (0, 0)
    return pl.pallas_call(
        functools.partial(_dense_router_kernel, prompt_tiles=prompt_tiles),
        grid=(m // tm,),
        in_specs=[pl.BlockSpec((tm, D_MODEL), row), pl.BlockSpec((tm, D_MODEL), row),
                  pl.BlockSpec((tm, PLE_DIM), in_prompt), pl.BlockSpec((tm, PLE_DIM), in_sample)]
        + [pl.BlockSpec(a.shape, const) for a in consts],
        out_specs=[pl.BlockSpec((tm, D_MODEL), row), pl.BlockSpec((tm, LANES), row),
                   pl.BlockSpec((tm, LANES), row), pl.BlockSpec((per_step * LANES, MOE_BLOCK), row),
                   pl.BlockSpec((per_step * LANES, MOE_BLOCK), row),
                   pl.BlockSpec((per_step, SUBLANES, LANES), lambda i: (i, 0, 0))],
        out_shape=[jax.ShapeDtypeStruct((m, D_MODEL), F32), jax.ShapeDtypeStruct((m, LANES), BF16),
                   jax.ShapeDtypeStruct((m, LANES), BF16), jax.ShapeDtypeStruct((nblk * LANES, MOE_BLOCK), BF16),
                   jax.ShapeDtypeStruct((nblk * LANES, MOE_BLOCK), BF16),
                   jax.ShapeDtypeStruct((nblk, SUBLANES, LANES), F32)],
        compiler_params=_params("parallel"),
        name="dense_router",
    )(x1, x1b, p_prompt, p_sample, *consts)


def _moe_tables(counts, tm):
    nblk = counts.shape[0]
    nchunk_max = MOE_CAP // MOE_SEG
    padded = (counts + MOE_SEG - 1) // MOE_SEG * MOE_SEG
    start = jnp.cumsum(padded, axis=1) - padded
    used = jnp.sum(padded, axis=1)
    rows_e = jnp.sum(padded, axis=0)
    rows_e_t = (rows_e + tm - 1) // tm * tm
    base_e = jnp.cumsum(rows_e_t) - rows_e_t
    seg_row = base_e[None, :] + jnp.cumsum(padded, axis=0) - padded
    chunk0 = jnp.arange(nchunk_max, dtype=jnp.int32) * MOE_SEG
    owner = jnp.sum((start + padded)[:, None, :] <= chunk0[None, :, None], axis=-1)
    onehot = owner[..., None] == jnp.arange(N_EXPERTS, dtype=jnp.int32)
    dst = chunk0[None, :] + jnp.sum(jnp.where(onehot, (seg_row - start)[:, None, :], 0), axis=-1)
    n_tiles = _moe_max_tiles(nblk, tm)
    live = chunk0[None, :] < used[:, None]
    parity = (jnp.arange(nblk, dtype=jnp.int32) % 2)[:, None]
    spare = n_tiles * tm + parity * MOE_CAP + chunk0[None, :]
    src = jnp.where(live, dst, 0)
    dst = jnp.where(live, dst, spare)
    tile_row = jnp.arange(n_tiles, dtype=jnp.int32) * tm
    tiles_used = jnp.sum(rows_e_t) // tm
    tile_row = jnp.minimum(tile_row, (tiles_used - 1) * tm)
    tile_expert = jnp.minimum(jnp.sum((base_e + rows_e_t)[None, :] <= tile_row[:, None], axis=-1), N_EXPERTS - 1)
    bounds = jnp.zeros((nblk, SUBLANES, LANES), F32)
    bounds = bounds.at[:, 0, :N_EXPERTS].set(start.astype(F32)).at[:, 1, :N_EXPERTS].set(padded.astype(F32))
    i32 = lambda a: a.astype(jnp.int32)
    return dict(dst=i32(dst.reshape(-1)), src=i32(src.reshape(-1)), nchunk=i32(used // MOE_SEG),
                pad_row=i32(base_e + rows_e),
                pad_n=i32((rows_e_t - rows_e) // MOE_SEG), tile_expert=i32(tile_expert),
                tiles_used=i32(tiles_used.reshape(1)), bounds=bounds)


def _moe_max_tiles(nblk, tm):
    return (nblk * (MOE_BLOCK * TOP_K + N_EXPERTS * (MOE_SEG - 1)) + N_EXPERTS * (tm - MOE_SEG) + tm - 1) // tm


def _slot_bounds(bounds):
    return bounds[0:1, :], bounds[1:2, :]


def _moe_gather_kernel(dst_ref, nchunk_ref, pad_row_ref, pad_n_ref,
                       x_ref, chosen_t_ref, rank_t_ref, bounds_ref, rows_ref, buf, zbuf, sem, zsem):
    j = pl.program_id(0)
    nblk = pl.num_programs(0)
    slot = j % 2
    nchunk_max = MOE_CAP // MOE_SEG

    def chunk_copy(blk, q, sl):
        src = buf.at[sl, pl.ds(pl.multiple_of(q * MOE_SEG, MOE_SEG), MOE_SEG), :]
        row = pl.multiple_of(dst_ref[blk * nchunk_max + q], MOE_SEG)
        return pltpu.make_async_copy(src, rows_ref.at[pl.ds(row, MOE_SEG), :], sem.at[sl])

    def pad_copy(e, i):
        row = pl.multiple_of(pad_row_ref[e] + i * MOE_SEG, MOE_SEG)
        return pltpu.make_async_copy(zbuf, rows_ref.at[pl.ds(row, MOE_SEG), :], zsem)

    def sub_chunks(blk, sub, sl, action):
        for q in range(sub * (MOE_SUB // MOE_SEG), (sub + 1) * (MOE_SUB // MOE_SEG)):
            action(chunk_copy(blk, q, sl))

    def sub_used(blk, sub):
        return True if sub < MOE_SUB_ALWAYS else sub * MOE_SUB < nchunk_ref[blk] * MOE_SEG

    def for_pads(action):
        def per_expert(e, carry):
            def body(i, c):
                action(pad_copy(e, i))
                return c
            return lax.fori_loop(0, pad_n_ref[e], body, carry)
        lax.fori_loop(0, N_EXPERTS, per_expert, 0)

    @pl.when(j == 0)
    def _():
        zbuf[...] = jnp.zeros_like(zbuf)
        for_pads(lambda c: c.start())

    start, length = _slot_bounds(bounds_ref[0])
    x = x_ref[...]
    chosen_rank_t = jnp.concatenate([chosen_t_ref[...], rank_t_ref[...]], axis=1)
    before = jnp.maximum(j - 2, 0)
    subs = range(MOE_CAP // MOE_SUB)

    @pl.when(j >= 2)
    def _():
        for sub in subs:
            @_when(sub_used(before, sub))
            def _():
                sub_chunks(before, sub, slot, lambda c: c.wait())

    def owners(sub):
        s = (sub * MOE_SUB + lax.broadcasted_iota(jnp.int32, (MOE_SUB, LANES), 0)).astype(F32)
        owner = (s >= start) & (s < start + length)
        within = s[:, 0:1] - jnp.sum(jnp.where(owner, start, 0.0), axis=1, keepdims=True)
        return owner.astype(BF16), within

    def picks(owner_within):
        owner, within = owner_within
        hit = _dot(owner, chosen_rank_t)
        return ((hit[:, :MOE_BLOCK] > 0.5) & (hit[:, MOE_BLOCK:] == within)).astype(BF16)

    def emit(sub, pick):
        buf[slot, sub * MOE_SUB:(sub + 1) * MOE_SUB, :] = _dot(pick, x).astype(BF16)
        sub_chunks(j, sub, slot, lambda c: c.start())

    staged = [picks(ow) for ow in [owners(sub) for sub in subs[:MOE_SUB_ALWAYS]]]
    for sub, pick in enumerate(staged):
        emit(sub, pick)
    for sub in subs[MOE_SUB_ALWAYS:]:
        @pl.when(sub_used(j, sub))
        def _():
            emit(sub, picks(owners(sub)))

    @pl.when(j == nblk - 1)
    def _():
        prev = jnp.maximum(j - 1, 0)
        for sub in subs:
            @_when(sub_used(j, sub))
            def _():
                sub_chunks(j, sub, slot, lambda c: c.wait())

            @_when((j >= 1) & sub_used(prev, sub))
            def _():
                sub_chunks(prev, sub, 1 - slot, lambda c: c.wait())

        for_pads(lambda c: c.wait())


def _moe_gather(x1b, chosen_t, rank_t, tables, tm):
    m = x1b.shape[0]
    nblk = m // MOE_BLOCK
    rows = _moe_max_tiles(nblk, tm) * tm + 2 * MOE_CAP
    blk = lambda j, *_: (j, 0)
    grid_spec = pltpu.PrefetchScalarGridSpec(
        num_scalar_prefetch=4,
        grid=(nblk,),
        in_specs=[pl.BlockSpec((MOE_BLOCK, D_MODEL), blk), pl.BlockSpec((LANES, MOE_BLOCK), blk),
                  pl.BlockSpec((LANES, MOE_BLOCK), blk),
                  pl.BlockSpec((1, SUBLANES, LANES), lambda j, *_: (j, 0, 0))],
        out_specs=pl.BlockSpec(memory_space=pl.ANY),
        scratch_shapes=[pltpu.VMEM((2, MOE_CAP, D_MODEL), BF16), pltpu.VMEM((MOE_SEG, D_MODEL), BF16),
                        pltpu.SemaphoreType.DMA((2,)), pltpu.SemaphoreType.DMA(())],
    )
    return pl.pallas_call(
        _moe_gather_kernel,
        grid_spec=grid_spec,
        out_shape=jax.ShapeDtypeStruct((rows, D_MODEL), BF16),
        compiler_params=_params("arbitrary"),
        name="moe_gather",
    )(tables["dst"], tables["nchunk"], tables["pad_row"], tables["pad_n"], x1b, chosen_t, rank_t, tables["bounds"])


def _moe_ffn_kernel(tile_expert_ref, x_ref, w1_ref, w3_ref, w2_ref, y_ref, w13_scr, w2_scr):
    i = pl.program_id(0)

    @pl.when((i == 0) | (tile_expert_ref[i] != tile_expert_ref[jnp.maximum(i - 1, 0)]))
    def _():
        w13_scr[:, :EXPERT_FF] = w1_ref[0].astype(BF16)
        w13_scr[:, EXPERT_FF:] = w3_ref[0].astype(BF16)
        w2_scr[...] = w2_ref[0].astype(BF16)

    up = _dot(x_ref[...], w13_scr[...])
    hid = _silu(up[:, :EXPERT_FF]) * up[:, EXPERT_FF:]
    y_ref[...] = _dot(hid.astype(BF16), w2_scr[...]).astype(BF16)


def _moe_ffn(rows, w1, w3, w2, tables, tm):
    tile = lambda i, tile_expert: (i, 0)
    expert = lambda i, tile_expert: (tile_expert[i], 0, 0)
    grid_spec = pltpu.PrefetchScalarGridSpec(
        num_scalar_prefetch=1,
        grid=(tables["tiles_used"][0],),
        in_specs=[pl.BlockSpec((tm, D_MODEL), tile), pl.BlockSpec((1, D_MODEL, EXPERT_FF), expert),
                  pl.BlockSpec((1, D_MODEL, EXPERT_FF), expert), pl.BlockSpec((1, EXPERT_FF, D_MODEL), expert)],
        out_specs=pl.BlockSpec((tm, D_MODEL), tile),
        scratch_shapes=[pltpu.VMEM((D_MODEL, 2 * EXPERT_FF), BF16), pltpu.VMEM((EXPERT_FF, D_MODEL), BF16)],
    )
    return pl.pallas_call(
        _moe_ffn_kernel,
        grid_spec=grid_spec,
        out_shape=jax.ShapeDtypeStruct(rows.shape, BF16),
        compiler_params=_params("arbitrary"),
        name="moe_ffn",
    )(tables["tile_expert"], rows, w1, w3, w2)


def _moe_combine_kernel(dst_ref, nchunk_ref, dense_ref, gates_ref, rank_ref, bounds_ref, g_ref, b_ref, rows_ref,
                        op_ref, os_ref, acc, buf, sem, *, prompt_tiles):
    j = pl.program_id(0)
    nblk = pl.num_programs(0)
    slot = j % 2
    nchunk_max = MOE_CAP // MOE_SEG

    def chunk_copy(blk, q, sl):
        row = pl.multiple_of(dst_ref[blk * nchunk_max + q], MOE_SEG)
        dst = buf.at[sl, pl.ds(pl.multiple_of(q * MOE_SEG, MOE_SEG), MOE_SEG), :]
        return pltpu.make_async_copy(rows_ref.at[pl.ds(row, MOE_SEG), :], dst, sem.at[sl])

    subs = range(MOE_CAP // MOE_SUB)
    always, rest = subs[:MOE_SUB_ALWAYS], subs[MOE_SUB_ALWAYS:]

    def sub_used(blk, sub):
        return True if sub < MOE_SUB_ALWAYS else sub * MOE_SUB < nchunk_ref[blk] * MOE_SEG

    def group(blk, sub, sl, action):
        for q in range(sub * (MOE_SUB // MOE_SEG), (sub + 1) * (MOE_SUB // MOE_SEG)):
            action(chunk_copy(blk, q, sl))

    @pl.when(j == 0)
    def _():
        buf[...] = jnp.zeros_like(buf)
        for sub in subs:
            @_when(sub_used(0, sub))
            def _():
                group(0, sub, 0, lambda c: c.start())

    for sub in subs:
        @_when(sub_used(j, sub))
        def _():
            group(j, sub, slot, lambda c: c.wait())

    bounds_t = jnp.concatenate([bounds_ref[0], jnp.zeros((LANES - SUBLANES, LANES), F32)], axis=0).T
    start, length = bounds_t[:, 0:1], bounds_t[:, 1:2]
    gates_rank = jnp.concatenate([gates_ref[...], rank_ref[...]], axis=0)

    def owners(sub):
        s = (sub * MOE_SUB + lax.broadcasted_iota(jnp.int32, (LANES, MOE_SUB), 1)).astype(F32)
        owner = (s >= start) & (s < start + length)
        within = s[0:1, :] - jnp.sum(jnp.where(owner, start, 0.0), axis=0, keepdims=True)
        return owner.astype(BF16), within

    def weights(owner_within):
        owner, within = owner_within
        hit = _dot(gates_rank, owner)
        return jnp.where(hit[MOE_BLOCK:] == within, hit[:MOE_BLOCK], 0.0).astype(BF16)

    def apply(sub, weight):
        return _dot(weight, buf[slot, sub * MOE_SUB:(sub + 1) * MOE_SUB, :])

    nxt = jnp.minimum(j + 1, nblk - 1)
    staged = [owners(sub) for sub in always]
    for sub in always:
        group(nxt, sub, 1 - slot, lambda c: c.start())
    staged = [weights(ow) for ow in staged]
    total = dense_ref[...]
    for sub, weight in zip(always, staged):
        total = total + apply(sub, weight)
    acc[...] = total
    for sub in rest:
        @pl.when((j + 1 < nblk) & sub_used(nxt, sub))
        def _():
            group(nxt, sub, 1 - slot, lambda c: c.start())

        @pl.when(sub_used(j, sub))
        def _():
            acc[...] += apply(sub, weights(owners(sub)))

    @pl.when(j == nblk - 1)
    def _():
        for sub in always:
            group(nxt, sub, 1 - slot, lambda c: c.wait())

    @pl.when(j < prompt_tiles)
    def _():
        op_ref[...] = _layer_norm(acc[...], g_ref[...], b_ref[...])

    @pl.when(j >= prompt_tiles)
    def _():
        os_ref[...] = _layer_norm(acc[...], g_ref[...], b_ref[...])


def _moe_combine(rows, dense, gates, rank, tables, g, b, n_prompt):
    m = dense.shape[0]
    nblk = m // MOE_BLOCK
    prompt_tiles = n_prompt // MOE_BLOCK
    in_prompt, in_sample = _group_maps(prompt_tiles)
    blk = lambda j, *_: (j, 0)
    const = lambda j, *_: (0, 0)
    grid_spec = pltpu.PrefetchScalarGridSpec(
        num_scalar_prefetch=2,
        grid=(nblk,),
        in_specs=[pl.BlockSpec((MOE_BLOCK, D_MODEL), blk), pl.BlockSpec((MOE_BLOCK, LANES), blk),
                  pl.BlockSpec((MOE_BLOCK, LANES), blk),
                  pl.BlockSpec((1, SUBLANES, LANES), lambda j, *_: (j, 0, 0)),
                  pl.BlockSpec((1, D_MODEL), const), pl.BlockSpec((1, D_MODEL), const),
                  pl.BlockSpec(memory_space=pl.ANY)],
        out_specs=[pl.BlockSpec((MOE_BLOCK, D_MODEL), in_prompt), pl.BlockSpec((MOE_BLOCK, D_MODEL), in_sample)],
        scratch_shapes=[pltpu.VMEM((MOE_BLOCK, D_MODEL), F32), pltpu.VMEM((2, MOE_CAP, D_MODEL), BF16),
                        pltpu.SemaphoreType.DMA((2,))],
    )
    return pl.pallas_call(
        functools.partial(_moe_combine_kernel, prompt_tiles=prompt_tiles),
        grid_spec=grid_spec,
        out_shape=[jax.ShapeDtypeStruct((n_prompt, D_MODEL), F32),
                   jax.ShapeDtypeStruct((m - n_prompt, D_MODEL), F32)],
        compiler_params=_params("arbitrary"),
        name="moe_combine",
    )(tables["src"], tables["nchunk"], dense, gates, rank, tables["bounds"], g, b, rows)


def _row(v):
    return v.reshape(1, -1).astype(F32)


def _scan_layout(batch, seq, row0):
    t = math.gcd(seq, CHUNK)
    nchunk = seq // t
    chunks = SCAN_CHUNKS if nchunk % SCAN_CHUNKS == 0 else 1
    seqs = SCAN_SEQS if (nchunk == 1 and batch % SCAN_SEQS == 0 and row0 % (SCAN_SEQS * t) == 0) else 1
    return dict(batch=batch, seq=seq, row0=row0, t=t, seqs=seqs, chunks=chunks)


def _lanes_at(v, start):
    return jnp.zeros((1, LANES), F32).at[0, start:start + v.shape[0]].set(v.astype(F32))


def kernel(x_prompt, x_sample, p_prompt, p_sample, state_ssm_conv, state_ssm, state_dn_conv, state_dn, emb_ln_g, emb_ln_b, w_in, conv_a_w, conv_a_b, ssm_dt_bias, ssm_a_log, ssm_d, ssm_norm_w, w_a, conv_b_w, dn_dt_bias, dn_a_log, dn_norm_w, w_b, w_o, ln1_g, ln1_b, router_w, router_bias, exp_w1, exp_w3, exp_w2, sh_w1, sh_w3, sh_w2, ple_w, ple_gate_w, ln2_g, ln2_b):
    bp, lp, _ = x_prompt.shape
    bs, ls, _ = x_sample.shape
    n_p = bp * lp
    n_s = bs * ls
    xp = x_prompt.reshape(n_p, D_MODEL)
    xs = x_sample.reshape(n_s, D_MODEL)

    w = w_in[0]
    o_z, o_xbc, o_dt = 0, SSM_INNER, SSM_INNER + SSM_CONV_CH
    o_qkv = o_dt + SSM_HEADS
    o_a = o_qkv + DN_CONV_CH
    o_b = o_a + DN_HEADS
    o_zb = o_b + DN_HEADS
    o_ga = o_zb + DN_V
    w_pa = w[:, o_z:o_dt].astype(BF16)
    w_pb = jnp.concatenate([w[:, o_qkv:o_a], w[:, o_zb:o_ga]], axis=1).astype(BF16)
    w_pg = w[:, o_ga:].astype(BF16)
    w_ps = jnp.zeros((D_MODEL, LANES), F32)
    w_ps = w_ps.at[:, SMALL_DT:SMALL_DT + SSM_HEADS].set(w[:, o_dt:o_qkv])
    w_ps = w_ps.at[:, SMALL_A:SMALL_A + 2 * DN_HEADS].set(w[:, o_a:o_zb]).astype(BF16)

    eg, eb = _row(emb_ln_g), _row(emb_ln_b)
    proj_a, small = _ln_matmul(xp, xs, eg, eb, w_pa, PROJ_TM, PROJ_TN_A, w_ps)
    proj_b = _ln_matmul(xp, xs, eg, eb, w_pb, PROJ_TM, PROJ_TN_B)

    ssd_consts = [conv_a_w[0], _row(conv_a_b[0]), _lanes_at(ssm_dt_bias[0], SMALL_DT),
                  _lanes_at(ssm_a_log[0], SMALL_DT), _row(jnp.repeat(ssm_d[0], SSM_HEAD_DIM)),
                  _row(ssm_norm_w[0])]
    gdn_consts = [conv_b_w[0], _lanes_at(dn_a_log[0], SMALL_A), _lanes_at(dn_dt_bias[0], SMALL_A),
                  _row(dn_norm_w[0])]
    zeros = lambda *s: jnp.zeros(s, F32)
    prompt = _scan_layout(bp, lp, 0)
    sample = _scan_layout(bs, ls, n_p)

    ya_p, pa_conv, pa_ssm = _ssd(proj_a, small, ssd_consts, zeros(bp, CONV_WIDTH - 1, SSM_CONV_CH),
                                 zeros(bp, SSM_HEADS, SSM_HEAD_DIM, SSM_STATE), **prompt)
    ya_s, sa_conv, sa_ssm = _ssd(proj_a, small, ssd_consts, state_ssm_conv[0], state_ssm[0], **sample)
    ob_p, pb_conv, pb_dn = _gdn(proj_b, small, gdn_consts, zeros(bp, CONV_WIDTH - 1, DN_CONV_CH),
                                zeros(bp, DN_HEADS, DN_HEAD, DN_HEAD), **prompt)
    ob_s, sb_conv, sb_dn = _gdn(proj_b, small, gdn_consts, state_dn_conv[0], state_dn[0], **sample)

    x1, x1b = _merge(
        xp, xs, ya_p, ya_s, ob_p, ob_s,
        [eg, eb, w_pg, w_a[0].astype(BF16), w_b[0].astype(BF16), w_o[0].astype(BF16), _row(ln1_g[0]), _row(ln1_b[0])],
        MERGE_TM)

    router_w_t = jnp.zeros((LANES, D_MODEL), F32).at[:N_EXPERTS].set(router_w[0].T)
    router_w_hi = router_w_t.astype(BF16)
    router_w_lo = (router_w_t - router_w_hi.astype(F32)).astype(BF16)
    router_b = jnp.broadcast_to(_lanes_at(router_bias[0], 0).reshape(LANES, 1), (LANES, MOE_BLOCK))
    dense, gates, rank, chosen_t, rank_t, counts = _dense_router(
        x1, x1b, p_prompt[0].reshape(n_p, PLE_DIM), p_sample[0].reshape(n_s, PLE_DIM),
        [sh_w1[0].astype(BF16), sh_w3[0].astype(BF16), sh_w2[0].astype(BF16), ple_w[0].astype(BF16),
         ple_gate_w[0].astype(BF16), router_w_hi, router_w_lo, router_b])

    tables = _moe_tables(counts[:, 0, :N_EXPERTS].astype(jnp.int32), MOE_TM)
    sorted_rows = _moe_gather(x1b, chosen_t, rank_t, tables, MOE_TM)
    expert_out = _moe_ffn(sorted_rows, exp_w1[0], exp_w3[0], exp_w2[0], tables, MOE_TM)
    out_p, out_s = _moe_combine(expert_out, dense, gates, rank, tables, _row(ln2_g[0]), _row(ln2_b[0]), n_p)

    return (out_p.reshape(bp, lp, D_MODEL), out_s.reshape(bs, ls, D_MODEL),
            pa_conv[None], pa_ssm[None], pb_conv[None], pb_dn[None],
            sa_conv[None], sa_ssm[None], sb_conv[None], sb_dn[None])
```

```python
import functools
import math

import jax
import jax.numpy as jnp
from jax import lax
from jax.experimental import pallas as pl
from jax.experimental.pallas import tpu as pltpu

F32 = jnp.float32
BF16 = jnp.bfloat16
HIGHEST = lax.Precision.HIGHEST

D_MODEL = 1024
SSM_INNER = 2048
SSM_HEAD_DIM = 64
SSM_HEADS = 32
SSM_GROUPS = 2
SSM_HEADS_PER_GROUP = 16
SSM_STATE = 128
SSM_CONV_CH = 2560
DN_HEADS = 8
DN_HEAD = 128
DN_QK = 1024
DN_V = 1024
DN_CONV_CH = 3072
CONV_WIDTH = 4
CHUNK = 64
N_EXPERTS = 64
TOP_K = 8
N_EXPERT_GROUPS = 8
EXPERTS_PER_GROUP = 8
TOPK_GROUPS = 4
EXPERT_FF = 256
SHARED_FF = 256
ROUTED_SCALE = 2.5
PLE_DIM = 256
LN_EPS = 1e-5
RMS_EPS = 1e-6
L2_EPS = 1e-6
DEEPNORM_ALPHA = 2.0 ** 0.25

LANES = 128
SUBLANES = 8
VMEM_LIMIT = 56 * 1024 * 1024
SMALL_DT = 0
SMALL_A = 32
SMALL_B = 40
PROJ_TM = 1024
PROJ_TN_A = 2304
PROJ_TN_B = 2048
MERGE_TM = 512
SCAN_CHUNKS = 8
SCAN_SEQS = 8
MOE_BLOCK = 256
MOE_SEG = 16
MOE_SUB = 512
MOE_CAP = -(-(MOE_BLOCK * TOP_K + N_EXPERTS * (MOE_SEG - 1)) // MOE_SUB) * MOE_SUB
MOE_TM = 512
ROUTER_TM = 512
MOE_SUB_ALWAYS = -(-(MOE_BLOCK * TOP_K + N_EXPERTS * MOE_SEG // 2) // MOE_SUB)


def _sigmoid(x):
    return 1.0 / (1.0 + jnp.exp(-x))


def _silu(x):
    return x * _sigmoid(x)


def _softplus(x):
    return jnp.maximum(x, 0.0) + jnp.log(1.0 + jnp.exp(-jnp.abs(x)))


def _layer_norm(x, g, b):
    mu = jnp.mean(x, axis=-1, keepdims=True)
    xc = x - mu
    var = jnp.mean(xc * xc, axis=-1, keepdims=True)
    return xc * lax.rsqrt(var + LN_EPS) * g + b


def _dot(a, b):
    return jnp.dot(a, b, preferred_element_type=F32)


def _dot_nt(a, b):
    return lax.dot_general(a, b, (((1,), (1,)), ((), ())), preferred_element_type=F32)


def _dot_tn(a, b):
    return lax.dot_general(a, b, (((0,), (0,)), ((), ())), preferred_element_type=F32)


def _dot_f32(a, b):
    return jnp.dot(a, b, precision=HIGHEST, preferred_element_type=F32)


def _params(*sem):
    return pltpu.CompilerParams(dimension_semantics=sem, vmem_limit_bytes=VMEM_LIMIT)


def _when(cond):
    if cond is True:
        return lambda fn: fn()
    if cond is False:
        return lambda fn: None
    return pl.when(cond)


def _row_tile(m, preferred):
    return max(d for d in range(SUBLANES, min(m, preferred) + 1, SUBLANES) if m % d == 0)


def _group_maps(prompt_tiles):
    in_prompt = lambda i, *_: (jnp.minimum(i, prompt_tiles - 1), 0)
    in_sample = lambda i, *_: (jnp.maximum(i - prompt_tiles, 0), 0)
    return in_prompt, in_sample


def _ln_matmul_kernel(xp_ref, xs_ref, g_ref, b_ref, w_ref, *rest, prompt_tiles, narrow):
    if narrow:
        wn_ref, o_ref, on_ref, h_scr = rest
    else:
        o_ref, h_scr = rest

    @pl.when(pl.program_id(1) == 0)
    def _():
        x = jnp.where(pl.program_id(0) < prompt_tiles, xp_ref[...], xs_ref[...])
        h_scr[...] = _layer_norm(x, g_ref[...], b_ref[...]).astype(BF16)
        if narrow:
            on_ref[...] = _dot(h_scr[...], wn_ref[...])

    o_ref[...] = _dot(h_scr[...], w_ref[...]).astype(o_ref.dtype)


def _ln_matmul(xp, xs, g, b, w, tm, tn, w_narrow=None):
    k = xp.shape[1]
    m = xp.shape[0] + xs.shape[0]
    n = w.shape[1]
    tm = _row_tile(math.gcd(xp.shape[0], xs.shape[0]), tm)
    prompt_tiles = xp.shape[0] // tm
    in_prompt, in_sample = _group_maps(prompt_tiles)
    narrow = w_narrow is not None
    const = lambda i, j: (0, 0)
    row = lambda i, j: (i, 0)
    in_specs = [pl.BlockSpec((tm, k), in_prompt), pl.BlockSpec((tm, k), in_sample),
                pl.BlockSpec((1, k), const), pl.BlockSpec((1, k), const), pl.BlockSpec((k, tn), lambda i, j: (0, j))]
    out_specs = [pl.BlockSpec((tm, tn), lambda i, j: (i, j))]
    out_shape = [jax.ShapeDtypeStruct((m, n), BF16)]
    args = [xp, xs, g, b, w]
    if narrow:
        in_specs.append(pl.BlockSpec((k, LANES), const))
        out_specs.append(pl.BlockSpec((tm, LANES), row))
        out_shape.append(jax.ShapeDtypeStruct((m, LANES), F32))
        args.append(w_narrow)
    out = pl.pallas_call(
        functools.partial(_ln_matmul_kernel, prompt_tiles=prompt_tiles, narrow=narrow),
        grid=(m // tm, n // tn),
        in_specs=in_specs,
        out_specs=out_specs,
        out_shape=out_shape,
        scratch_shapes=[pltpu.VMEM((tm, k), BF16)],
        compiler_params=_params("parallel", "arbitrary"),
        name="ln_matmul",
    )(*args)
    return out if narrow else out[0]


def _load_conv_state(cbuf, state):
    taps = CONV_WIDTH - 1
    cbuf[0:SUBLANES - taps, :] = jnp.zeros((SUBLANES - taps, cbuf.shape[1]), F32)
    cbuf[SUBLANES - taps:SUBLANES, :] = state


def _causal_conv(cbuf, x16, cw, t):
    ch = x16.shape[1]
    prev = cbuf[...]
    taps = CONV_WIDTH - 1
    if t <= SUBLANES:
        x = x16.astype(F32)
        ext = jnp.concatenate([prev, x], axis=0)
        lo = SUBLANES - taps
        y = ext[lo:lo + t] * cw[0:1]
        y = y + ext[lo + 1:lo + 1 + t] * cw[1:2]
        y = y + ext[lo + 2:lo + 2 + t] * cw[2:3]
        y = y + x * cw[3:4]
        cbuf[...] = ext[t:t + SUBLANES]
        return y, ext[SUBLANES + t - taps:SUBLANES + t]
    hi = prev.astype(BF16).astype(F32)
    mid = (prev - hi).astype(BF16).astype(F32)
    low = ((prev - hi) - mid).astype(BF16).astype(F32)
    x = x16.astype(F32)
    pack = 2 * SUBLANES
    nhead = 4 * SUBLANES if t % pack == 0 else 5 * SUBLANES
    head = jnp.concatenate([hi, mid, low, jnp.zeros((nhead - 3 * SUBLANES, ch), F32)], axis=0)
    if t % pack == 0:
        ext = jnp.concatenate([head.astype(BF16), x16], axis=0)
    else:
        ext = jnp.concatenate([head, x], axis=0).astype(BF16)
    out_row = lax.broadcasted_iota(jnp.int32, (taps * t, nhead + t), 0)
    col = lax.broadcasted_iota(jnp.int32, (taps * t, nhead + t), 1)
    log_t = t.bit_length() - 1
    shift = lax.shift_right_logical(out_row, log_t) + 1
    src = (out_row & (t - 1)) - shift
    in_x = (src >= 0) & (col == nhead + src)
    in_prev = (src < 0) & (col < 3 * SUBLANES) & ((col & (SUBLANES - 1)) == SUBLANES + src)
    shifted = _dot((in_x | in_prev).astype(BF16), ext)
    y = shifted[2 * t:3 * t] * cw[0:1]
    y = y + shifted[t:2 * t] * cw[1:2]
    y = y + shifted[0:t] * cw[2:3]
    y = y + x * cw[3:4]
    cbuf[...] = x[t - SUBLANES:t]
    return y, x[t - taps:t]


def _lane_expand(v, h0, count, width):
    t = v.shape[0]
    n = count * width
    out = jnp.broadcast_to(v[:, h0:h0 + 1], (t, n))
    if count > 1:
        lane = lax.broadcasted_iota(jnp.int32, (t, n), 1)
        for i in range(1, count):
            out = jnp.where(lane >= i * width, jnp.broadcast_to(v[:, h0 + i:h0 + i + 1], (t, n)), out)
    return out


def _grouped_transpose(v, per_tile):
    t = v.shape[0]
    blocks = [v if r == 0 else pltpu.roll(v, LANES - r, axis=1) for r in range(per_tile)]
    if per_tile * t < LANES:
        blocks.append(jnp.zeros((LANES - per_tile * t, LANES), F32))
    return jnp.concatenate(blocks, axis=0).T


def _same_block(i, j, size):
    shift = size.bit_length() - 1
    return lax.shift_right_logical(i, shift) == lax.shift_right_logical(j, shift)


def _unit_lower_inverses(lmats, ii, jj, t):
    mm = lambda a, b: _dot(a.astype(BF16), b.astype(BF16))
    base = min(16, t)
    in_base = _same_block(ii, jj, base)
    eye = (ii == jj).astype(F32)
    power = [jnp.where(in_base, l, 0.0) for l in lmats]
    inv = [eye - p for p in power]
    span = 2
    while span < base:
        power = [mm(p, p) for p in power]
        inv = [a + mm(a, p) for a, p in zip(inv, power)]
        span *= 2
    size = base
    while size < t:
        link = _same_block(ii, jj, 2 * size) & jnp.logical_not(_same_block(ii, jj, size))
        cross = [mm(a, jnp.where(link, l, 0.0)) for a, l in zip(inv, lmats)]
        inv = [a - mm(c, a) for a, c in zip(inv, cross)]
        size *= 2
    return inv


def _row_sums_of_squares(blocks):
    t = blocks[0].shape[0]
    sq = jnp.concatenate([b * b for b in blocks], axis=0)
    hi = sq.astype(BF16)
    lo = (sq - hi.astype(F32)).astype(BF16)
    ones = jnp.ones((LANES, LANES), BF16)
    sums = _dot(hi, ones) + _dot(lo, ones)
    return [sums[i * t:(i + 1) * t] for i in range(len(blocks))]


def _cumsum_rows(v, t):
    ii = lax.broadcasted_iota(jnp.int32, (t, t), 0)
    jj = lax.broadcasted_iota(jnp.int32, (t, t), 1)
    return _dot_f32((jj <= ii).astype(F32), v)


def _ssd_kernel(pa_ref, sm_ref, cw_ref, cb_ref, dtb_ref, alog_ref, dskip_ref, nw_ref, cst_ref, h0_ref,
                y_ref, cnew_ref, hnew_ref, cbuf, s_scr, *, t, first, last):
    per_tile = LANES // t
    width = per_tile * SSM_HEAD_DIM

    @_when(first)
    def _():
        _load_conv_state(cbuf, cst_ref[0])
        s_scr[...] = h0_ref[0]

    z = pa_ref[:, :SSM_INNER].astype(F32)
    conv, tail = _causal_conv(cbuf, pa_ref[:, SSM_INNER:], cw_ref[...], t)

    @_when(last)
    def _():
        cnew_ref[0] = tail

    xbc = _silu(conv + cb_ref[...])
    xs = xbc[:, :SSM_INNER]
    bm = xbc[:, SSM_INNER:SSM_INNER + SSM_GROUPS * SSM_STATE]
    cm = xbc[:, SSM_INNER + SSM_GROUPS * SSM_STATE:]

    dt = _softplus(sm_ref[...] + dtb_ref[...])
    da = dt * (-jnp.exp(alog_ref[...]))
    cum = _cumsum_rows(da, t)
    cum_t = _grouped_transpose(cum, per_tile)
    dt_t = _grouped_transpose(dt, per_tile)

    row = lax.broadcasted_iota(jnp.int32, (t, LANES), 0)
    lane = lax.broadcasted_iota(jnp.int32, (t, LANES), 1)
    causal = (lane % t) <= row
    brow = lax.broadcasted_iota(jnp.int32, (LANES, width), 0)
    bcol = lax.broadcasted_iota(jnp.int32, (LANES, width), 1)
    blockdiag = (brow // t) == (bcol // SSM_HEAD_DIM)

    ys = []
    for g in range(SSM_GROUPS):
        bg = bm[:, g * SSM_STATE:(g + 1) * SSM_STATE]
        cg = cm[:, g * SSM_STATE:(g + 1) * SSM_STATE].astype(BF16)
        cb = _dot_nt(cg, jnp.concatenate([bg] * per_tile, axis=0).astype(BF16))
        bg = bg.astype(BF16)
        for q in range(SSM_HEADS_PER_GROUP // per_tile):
            h0 = g * SSM_HEADS_PER_GROUP + q * per_tile
            ccol = _lane_expand(cum, h0, per_tile, t)
            seg = ccol - cum_t[h0:h0 + 1, :]
            decay = jnp.exp(jnp.where(causal, seg, -jnp.inf))
            wts = (cb * decay * dt_t[h0:h0 + 1, :]).astype(BF16)
            xt = xs[:, h0 * SSM_HEAD_DIM:h0 * SSM_HEAD_DIM + width]
            xbd = jnp.where(blockdiag, jnp.concatenate([xt] * per_tile, axis=0), 0.0).astype(BF16)
            y_intra = _dot(wts, xbd)
            st = s_scr[h0:h0 + per_tile].reshape(width, SSM_STATE)
            cum_w = ccol if width == LANES else _lane_expand(cum, h0, per_tile, SSM_HEAD_DIM)
            y_state = _dot_nt(cg, st.astype(BF16)) * jnp.exp(cum_w)
            ys.append(y_intra + y_state)
            wend = jnp.exp(cum_w[t - 1:t, :] - cum_w) * _lane_expand(dt, h0, per_tile, SSM_HEAD_DIM)
            ds = _dot_tn((xt * wend).astype(BF16), bg)
            for i in range(per_tile):
                h = h0 + i
                s_scr[h] = jnp.exp(cum[t - 1:t, h:h + 1]) * s_scr[h] + ds[i * SSM_HEAD_DIM:(i + 1) * SSM_HEAD_DIM]

    y = jnp.concatenate(ys, axis=1) + dskip_ref[...] * xs
    y = y * _silu(z)
    half = SSM_INNER // SSM_GROUPS
    normed = []
    for g in range(SSM_GROUPS):
        yg = y[:, g * half:(g + 1) * half]
        normed.append(yg * lax.rsqrt(jnp.mean(yg * yg, axis=-1, keepdims=True) + RMS_EPS))
    y_ref[...] = (jnp.concatenate(normed, axis=1) * nw_ref[...]).astype(BF16)

    @_when(last)
    def _():
        hnew_ref[0] = s_scr[...]


def _scan_step_kernel(chunk_kernel, n_tok, n_const, n_state, seqs, chunks, t):
    def step(*refs):
        tok = refs[:n_tok]
        const = refs[n_tok:n_tok + n_const]
        st_in = refs[n_tok + n_const:n_tok + n_const + n_state]
        y_ref = refs[n_tok + n_const + n_state]
        st_out = refs[n_tok + n_const + n_state + 1:n_tok + n_const + 2 * n_state + 1]
        scratch = refs[n_tok + n_const + 2 * n_state + 1:]
        first_step = pl.program_id(1) == 0
        last_step = pl.program_id(1) == pl.num_programs(1) - 1
        for s in range(seqs):
            for c in range(chunks):
                rows = pl.ds((s * chunks + c) * t, t)
                chunk_kernel(*[r.at[rows, :] for r in tok], *const, *[r.at[pl.ds(s, 1)] for r in st_in],
                             y_ref.at[rows, :], *[r.at[pl.ds(s, 1)] for r in st_out], *scratch,
                             first=first_step if c == 0 else False,
                             last=last_step if c == chunks - 1 else False)
    return step


def _scan_call(kernel, name, tok_in, const_in, state_in, y_width, scratch, *, batch, seq, row0, t, seqs, chunks,
               step_kernel=None):
    nstep = seq // (t * chunks)
    rows = seqs * chunks * t
    blk0 = row0 // rows

    def per_batch(a):
        zeros = (0,) * (a.ndim - 1)
        return pl.BlockSpec((seqs,) + a.shape[1:], lambda b, c: (b,) + zeros)

    if step_kernel is None:
        step_kernel = _scan_step_kernel(kernel, len(tok_in), len(const_in), len(state_in), seqs, chunks, t)
    return pl.pallas_call(
        step_kernel,
        grid=(batch // seqs, nstep),
        in_specs=[pl.BlockSpec((rows, a.shape[1]), lambda b, c: (blk0 + b * nstep + c, 0)) for a in tok_in]
        + [pl.BlockSpec(a.shape, lambda b, c: (0, 0)) for a in const_in]
        + [per_batch(a) for a in state_in],
        out_specs=[pl.BlockSpec((rows, y_width), lambda b, c: (b * nstep + c, 0))]
        + [per_batch(a) for a in state_in],
        out_shape=[jax.ShapeDtypeStruct((batch * seq, y_width), BF16)]
        + [jax.ShapeDtypeStruct(a.shape, F32) for a in state_in],
        scratch_shapes=scratch,
        compiler_params=_params("parallel", "arbitrary"),
        name=name,
    )(*tok_in, *const_in, *state_in)


def _ssd(proj_a, small, consts, conv_state, h0, **where):
    t = where["t"]
    scratch = [pltpu.VMEM((SUBLANES, SSM_CONV_CH), F32),
               pltpu.VMEM((SSM_HEADS, SSM_HEAD_DIM, SSM_STATE), F32)]
    return _scan_call(functools.partial(_ssd_kernel, t=t), f"ssd_t{t}", [proj_a, small], consts,
                      [conv_state, h0], SSM_INNER, scratch, **where)


def _gdn_step_kernel(pb_ref, sm_ref, cw_ref, alog_ref, dtb_ref, nw_ref, cst_ref, s0_ref,
                     o_ref, cnew_ref, snew_ref, cbuf, s_scr, *, t, seqs, chunks):
    first_step = pl.program_id(1) == 0
    last_step = pl.program_id(1) == pl.num_programs(1) - 1
    units = [(s, c) for s in range(seqs) for c in range(chunks)]
    rows = lambda u: pl.ds(u * t, t)
    per_tile = min(LANES // t, DN_HEADS)
    n = per_tile * t
    ntile = DN_HEADS // per_tile
    heads = range(DN_HEADS)
    ii = lax.broadcasted_iota(jnp.int32, (n, n), 0)
    jj = lax.broadcasted_iota(jnp.int32, (n, n), 1)
    causal = _same_block(ii, jj, t) & (jj <= ii)
    diag = ii == jj
    stack = lambda xs, p: jnp.concatenate(xs[p * per_tile:(p + 1) * per_tile], axis=0)

    qkv, zb, gcum, beta = [], [], [], []
    for u, (s, c) in enumerate(units):
        @_when(first_step if c == 0 else False)
        def _():
            _load_conv_state(cbuf, cst_ref[s])

        conv, tail = _causal_conv(cbuf, pb_ref[rows(u), :DN_CONV_CH], cw_ref[...], t)

        @_when(last_step if c == chunks - 1 else False)
        def _():
            cnew_ref[s] = tail

        qkv.append(_silu(conv))
        zb.append(pb_ref[rows(u), DN_CONV_CH:].astype(F32))
        sm = sm_ref[rows(u), :]
        beta.append(_sigmoid(sm))
        gate = -jnp.exp(alog_ref[...]) * _softplus(sm + dtb_ref[...])
        gcum.append(_cumsum_rows(gate, t))
    gcum_t = [_grouped_transpose(g, per_tile) for g in gcum]

    sumsq = _row_sums_of_squares([x[:, i * DN_HEAD:(i + 1) * DN_HEAD] for x in qkv for i in range(2 * DN_HEADS)])
    q, k, v, gcol, bcol, kb, egcol = ([[] for _ in units] for _ in range(7))
    for u in range(len(units)):
        for h in heads:
            lo = h * DN_HEAD
            ss = sumsq[u * 2 * DN_HEADS:(u + 1) * 2 * DN_HEADS]
            q[u].append(qkv[u][:, lo:lo + DN_HEAD] * lax.rsqrt(ss[h] + L2_EPS) * (DN_HEAD ** -0.5))
            k[u].append(qkv[u][:, DN_QK + lo:DN_QK + lo + DN_HEAD] * lax.rsqrt(ss[DN_HEADS + h] + L2_EPS))
            v[u].append(qkv[u][:, 2 * DN_QK + lo:2 * DN_QK + lo + DN_HEAD])
            gcol[u].append(gcum[u][:, SMALL_A + h:SMALL_A + h + 1])
            bcol[u].append(beta[u][:, SMALL_B + h:SMALL_B + h + 1])
            kb[u].append(k[u][h] * bcol[u][h])
            egcol[u].append(jnp.exp(gcol[u][h]))

    tiles = [(u, p) for u in range(len(units)) for p in range(ntile)]
    k16 = [stack(k[u], p).astype(BF16) for u, p in tiles]
    decay = []
    for u, p in tiles:
        h0 = p * per_tile
        seg = stack(gcol[u], p) - gcum_t[u][SMALL_A + h0:SMALL_A + h0 + 1, :n]
        decay.append(jnp.exp(jnp.where(causal, seg, -jnp.inf)))
    lmat = [_dot_nt(stack(kb[u], p).astype(BF16), k16[i]) * jnp.where(diag, 0.0, decay[i])
            for i, (u, p) in enumerate(tiles)]
    qk = [(_dot_nt(stack(q[u], p).astype(BF16), k16[i]) * decay[i]).astype(BF16) for i, (u, p) in enumerate(tiles)]
    inv = [a.astype(BF16) for a in _unit_lower_inverses(lmat, ii, jj, t)]

    split = lambda x: (x[:, :DN_HEAD], x[:, DN_HEAD:])
    o_from, o_own, s_from, s_own, glast = ([[] for _ in units] for _ in range(5))
    aw = [_dot(inv[i], jnp.concatenate(
        [stack([kb[u][h] * egcol[u][h] for h in heads], p), stack([v[u][h] * bcol[u][h] for h in heads], p)],
        axis=1).astype(BF16)) for i, (u, p) in enumerate(tiles)]
    qaw = [_dot(qk[i], aw[i].astype(BF16)) for i in range(len(tiles))]
    pad_rows = -t % (2 * SUBLANES)
    pad = [jnp.zeros((pad_rows, DN_HEAD), F32)] if pad_rows else []
    for u in range(len(units)):
        for h in heads:
            p, j = divmod(h, per_tile)
            i = u * ntile + p
            sel = slice(j * t, (j + 1) * t)
            qkw, qku0 = split(qaw[i][sel])
            o_from[u].append(q[u][h] * egcol[u][h] - qkw)
            o_own[u].append(qku0)
            glast[u].append(gcum[u][t - 1:t, SMALL_A + h:SMALL_A + h + 1])
            kd = (k[u][h] * jnp.exp(glast[u][h] - gcol[u][h])).astype(BF16)
            kdw, kdu0 = split(_dot_tn(kd, aw[i][sel].astype(BF16)))
            s_from[u].append(kdw)
            s_own[u].append(kdu0)

    outs = []
    for u, (s, c) in enumerate(units):
        @_when(first_step if c == 0 else False)
        def _():
            s_scr[...] = s0_ref[s]

        for h in heads:
            state = s_scr[h]
            both = _dot(jnp.concatenate([s_from[u][h], o_from[u][h]] + pad, axis=0).astype(BF16), state.astype(BF16))
            outs.append(both[DN_HEAD:DN_HEAD + t] + o_own[u][h])
            s_scr[h] = jnp.exp(glast[u][h]) * state - both[:DN_HEAD] + s_own[u][h]

        @_when(last_step if c == chunks - 1 else False)
        def _():
            snew_ref[s] = s_scr[...]

    norm_w = nw_ref[...]
    osq = _row_sums_of_squares(outs)
    for u in range(len(units)):
        normed = []
        for h in heads:
            i = u * DN_HEADS + h
            normed.append(outs[i] * lax.rsqrt(osq[i] * (1.0 / DN_HEAD) + RMS_EPS) * norm_w
                          * _silu(zb[u][:, h * DN_HEAD:(h + 1) * DN_HEAD]))
        o_ref[rows(u), :] = jnp.concatenate(normed, axis=1).astype(BF16)


def _gdn(proj_b, small, consts, conv_state, s0, **where):
    t = where["t"]
    scratch = [pltpu.VMEM((SUBLANES, DN_CONV_CH), F32),
               pltpu.VMEM((DN_HEADS, DN_HEAD, DN_HEAD), F32)]
    step = functools.partial(_gdn_step_kernel, t=t, seqs=where["seqs"], chunks=where["chunks"])
    return _scan_call(None, f"gdn_t{t}", [proj_b, small], consts, [conv_state, s0], DN_V, scratch,
                      step_kernel=step, **where)


def _merge_kernel(xp_ref, xs_ref, yap_ref, yas_ref, obp_ref, obs_ref, eg_ref, eb_ref, wg_ref, wa_ref, wb_ref,
                  wo_ref, g_ref, b_ref, x1_ref, x1b_ref, *, prompt_tiles):
    is_prompt = pl.program_id(0) < prompt_tiles
    h = _layer_norm(jnp.where(is_prompt, xp_ref[...], xs_ref[...]), eg_ref[...], eb_ref[...])
    out_a = _dot(jnp.where(is_prompt, yap_ref[...], yas_ref[...]), wa_ref[...])
    out_b = _dot(jnp.where(is_prompt, obp_ref[...], obs_ref[...]), wb_ref[...])
    gt = _dot(h.astype(BF16), wg_ref[...])
    merged = _sigmoid(gt[:, :D_MODEL]) * out_a + _sigmoid(gt[:, D_MODEL:]) * out_b
    mix = _dot(merged.astype(BF16), wo_ref[...])
    x1 = _layer_norm(DEEPNORM_ALPHA * h + mix, g_ref[...], b_ref[...])
    x1_ref[...] = x1
    x1b_ref[...] = x1.astype(BF16)


def _merge(xp, xs, ya_p, ya_s, ob_p, ob_s, consts, tm):
    m = xp.shape[0] + xs.shape[0]
    n_p = xp.shape[0]
    tm = _row_tile(math.gcd(n_p, m - n_p), tm)
    in_prompt, in_sample = _group_maps(n_p // tm)
    row = lambda i: (i, 0)
    const = lambda i: (0, 0)
    return pl.pallas_call(
        functools.partial(_merge_kernel, prompt_tiles=n_p // tm),
        grid=(m // tm,),
        in_specs=[pl.BlockSpec((tm, D_MODEL), in_prompt), pl.BlockSpec((tm, D_MODEL), in_sample),
                  pl.BlockSpec((tm, SSM_INNER), in_prompt), pl.BlockSpec((tm, SSM_INNER), in_sample),
                  pl.BlockSpec((tm, DN_V), in_prompt), pl.BlockSpec((tm, DN_V), in_sample)]
        + [pl.BlockSpec(a.shape, const) for a in consts],
        out_specs=[pl.BlockSpec((tm, D_MODEL), row), pl.BlockSpec((tm, D_MODEL), row)],
        out_shape=[jax.ShapeDtypeStruct((m, D_MODEL), F32), jax.ShapeDtypeStruct((m, D_MODEL), BF16)],
        compiler_params=_params("parallel"),
        name="merge",
    )(xp, xs, ya_p, ya_s, ob_p, ob_s, *consts)


def _take_first_max(rest, index, limit, axis):
    best = jnp.max(rest, axis=axis, keepdims=True)
    first = jnp.min(jnp.where(rest == best, index, limit), axis=axis, keepdims=True)
    hit = index == first
    return hit, jnp.where(hit, -jnp.inf, rest)


def _router_gates_t(x1, xb, rw_hi, rw_lo, bias):
    tm = x1.shape[0]
    x_lo = (x1 - xb.astype(F32)).astype(BF16)
    logits = _dot_nt(rw_hi, xb) + (_dot_nt(rw_hi, x_lo) + _dot_nt(rw_lo, xb))
    scores = _sigmoid(logits)
    row = lax.broadcasted_iota(jnp.int32, (LANES, tm), 0)
    sel = jnp.where(row < N_EXPERTS, scores + bias, -jnp.inf)
    by_group = sel.reshape(LANES // EXPERTS_PER_GROUP, EXPERTS_PER_GROUP, tm)
    sub = lax.broadcasted_iota(jnp.int32, by_group.shape, 1)
    top1 = jnp.max(by_group, axis=1, keepdims=True)
    _, others = _take_first_max(by_group, sub, EXPERTS_PER_GROUP, 1)
    gscore = (top1 + jnp.max(others, axis=1, keepdims=True))[:N_EXPERT_GROUPS]
    gidx = lax.broadcasted_iota(jnp.int32, gscore.shape, 0)
    gkeep = jnp.zeros(gscore.shape, jnp.bool_)
    for _ in range(TOPK_GROUPS):
        hit, gscore = _take_first_max(gscore, gidx, N_EXPERT_GROUPS, 0)
        gkeep = gkeep | hit
    gkeep = jnp.broadcast_to(gkeep, (N_EXPERT_GROUPS, EXPERTS_PER_GROUP, tm)).reshape(N_EXPERTS, tm)
    rest = jnp.where(gkeep, sel[:N_EXPERTS], -jnp.inf)
    eidx = lax.broadcasted_iota(jnp.int32, (N_EXPERTS, tm), 0)
    keep = jnp.zeros((N_EXPERTS, tm), jnp.bool_)
    for _ in range(TOP_K):
        hit, rest = _take_first_max(rest, eidx, N_EXPERTS, 0)
        keep = keep | hit
    picked = jnp.where(keep, scores[:N_EXPERTS], 0.0)
    gates = picked / jnp.sum(picked, axis=0, keepdims=True) * ROUTED_SCALE
    return jnp.concatenate([gates, jnp.zeros((LANES - N_EXPERTS, tm), F32)], axis=0)


def _dense_router_kernel(x1_ref, x1b_ref, pp_ref, ps_ref, sw1_ref, sw3_ref, sw2_ref, pw_ref, pg_ref, rwh_ref,
                         rwl_ref, rb_ref, dense_ref, gates_ref, rank_ref, chosen_t_ref, rank_t_ref, count_ref,
                         *, prompt_tiles):
    xb = x1b_ref[...]
    hid = _silu(_dot(xb, sw1_ref[...])) * _dot(xb, sw3_ref[...])
    shared = _dot(hid.astype(BF16), sw2_ref[...])
    p = jnp.where(pl.program_id(0) < prompt_tiles, pp_ref[...], ps_ref[...])
    ple = _dot(p.astype(BF16), pw_ref[...]) * _sigmoid(_dot(xb, pg_ref[...]))
    x1 = x1_ref[...]
    dense_ref[...] = DEEPNORM_ALPHA * x1 + shared + ple
    ii = lax.broadcasted_iota(jnp.int32, (MOE_BLOCK, MOE_BLOCK), 0)
    jj = lax.broadcasted_iota(jnp.int32, (MOE_BLOCK, MOE_BLOCK), 1)
    earlier = (ii < jj).astype(BF16)
    for blk in range(x1.shape[0] // MOE_BLOCK):
        tok = slice(blk * MOE_BLOCK, (blk + 1) * MOE_BLOCK)
        exp = slice(blk * LANES, (blk + 1) * LANES)
        gates_t = _router_gates_t(x1[tok], xb[tok], rwh_ref[...], rwl_ref[...], rb_ref[...])
        chosen_t = (gates_t > 0.0).astype(BF16)
        rank_t = _dot(chosen_t, earlier)
        chosen_t_ref[exp, :] = chosen_t
        rank_t_ref[exp, :] = rank_t.astype(BF16)
        gates = gates_t.T
        rank = rank_t.T
        gates_ref[tok, :] = gates.astype(BF16)
        rank_ref[tok, :] = rank.astype(BF16)
        count = rank[MOE_BLOCK - 1:, :] + (gates[MOE_BLOCK - 1:, :] > 0.0).astype(F32)
        count_ref[blk] = jnp.broadcast_to(count, (SUBLANES, LANES))


def _dense_router(x1, x1b, p_prompt, p_sample, consts):
    m = x1.shape[0]
    tm = _row_tile(math.gcd(p_prompt.shape[0], p_sample.shape[0]), ROUTER_TM)
    assert tm % MOE_BLOCK == 0
    nblk = m // MOE_BLOCK
    per_step = tm // MOE_BLOCK
    prompt_tiles = p_prompt.shape[0] // tm
    in_prompt, in_sample = _group_maps(prompt_tiles)
    row = lambda i: (i, 0)
    const = lambda i: (0, 0)
    return pl.pallas_call(
        functools.partial(_dense_router_kernel, prompt_tiles=prompt_tiles),
        grid=(m // tm,),
        in_specs=[pl.BlockSpec((tm, D_MODEL), row), pl.BlockSpec((tm, D_MODEL), row),
                  pl.BlockSpec((tm, PLE_DIM), in_prompt), pl.BlockSpec((tm, PLE_DIM), in_sample)]
        + [pl.BlockSpec(a.shape, const) for a in consts],
        out_specs=[pl.BlockSpec((tm, D_MODEL), row), pl.BlockSpec((tm, LANES), row),
                   pl.BlockSpec((tm, LANES), row), pl.BlockSpec((per_step * LANES, MOE_BLOCK), row),
                   pl.BlockSpec((per_step * LANES, MOE_BLOCK), row),
                   pl.BlockSpec((per_step, SUBLANES, LANES), lambda i: (i, 0, 0))],
        out_shape=[jax.ShapeDtypeStruct((m, D_MODEL), F32), jax.ShapeDtypeStruct((m, LANES), BF16),
                   jax.ShapeDtypeStruct((m, LANES), BF16), jax.ShapeDtypeStruct((nblk * LANES, MOE_BLOCK), BF16),
                   jax.ShapeDtypeStruct((nblk * LANES, MOE_BLOCK), BF16),
                   jax.ShapeDtypeStruct((nblk, SUBLANES, LANES), F32)],
        compiler_params=_params("parallel"),
        name="dense_router",
    )(x1, x1b, p_prompt, p_sample, *consts)


def _moe_tables(counts, tm):
    nblk = counts.shape[0]
    nchunk_max = MOE_CAP // MOE_SEG
    padded = (counts + MOE_SEG - 1) // MOE_SEG * MOE_SEG
    start = jnp.cumsum(padded, axis=1) - padded
    used = jnp.sum(padded, axis=1)
    rows_e = jnp.sum(padded, axis=0)
    rows_e_t = (rows_e + tm - 1) // tm * tm
    base_e = jnp.cumsum(rows_e_t) - rows_e_t
    seg_row = base_e[None, :] + jnp.cumsum(padded, axis=0) - padded
    chunk0 = jnp.arange(nchunk_max, dtype=jnp.int32) * MOE_SEG
    owner = jnp.sum((start + padded)[:, None, :] <= chunk0[None, :, None], axis=-1)
    onehot = owner[..., None] == jnp.arange(N_EXPERTS, dtype=jnp.int32)
    dst = chunk0[None, :] + jnp.sum(jnp.where(onehot, (seg_row - start)[:, None, :], 0), axis=-1)
    n_tiles = _moe_max_tiles(nblk, tm)
    live = chunk0[None, :] < used[:, None]
    parity = (jnp.arange(nblk, dtype=jnp.int32) % 2)[:, None]
    spare = n_tiles * tm + parity * MOE_CAP + chunk0[None, :]
    src = jnp.where(live, dst, 0)
    dst = jnp.where(live, dst, spare)
    tile_row = jnp.arange(n_tiles, dtype=jnp.int32) * tm
    tiles_used = jnp.sum(rows_e_t) // tm
    tile_row = jnp.minimum(tile_row, (tiles_used - 1) * tm)
    tile_expert = jnp.minimum(jnp.sum((base_e + rows_e_t)[None, :] <= tile_row[:, None], axis=-1), N_EXPERTS - 1)
    in_use = rows_e > 0
    experts = jnp.arange(N_EXPERTS, dtype=jnp.int32)
    later = jnp.where(in_use[None, :] & (experts[None, :] > experts[:, None]), experts[None, :], N_EXPERTS)
    next_in_use = jnp.min(later, axis=1)
    next_in_use = jnp.where(next_in_use == N_EXPERTS, -1, next_in_use)
    order = jnp.cumsum(in_use) - 1
    tile_slot = order[tile_expert] % 2
    tile_next = next_in_use[tile_expert]
    bounds = jnp.zeros((nblk, SUBLANES, LANES), F32)
    bounds = bounds.at[:, 0, :N_EXPERTS].set(start.astype(F32)).at[:, 1, :N_EXPERTS].set(padded.astype(F32))
    i32 = lambda a: a.astype(jnp.int32)
    return dict(dst=i32(dst.reshape(-1)), src=i32(src.reshape(-1)), nchunk=i32(used // MOE_SEG),
                pad_row=i32(base_e + rows_e),
                pad_n=i32((rows_e_t - rows_e) // MOE_SEG), tile_expert=i32(tile_expert),
                tile_slot=i32(tile_slot), tile_next=i32(tile_next),
                tiles_used=i32(tiles_used.reshape(1)), bounds=bounds)


def _moe_max_tiles(nblk, tm):
    return (nblk * (MOE_BLOCK * TOP_K + N_EXPERTS * (MOE_SEG - 1)) + N_EXPERTS * (tm - MOE_SEG) + tm - 1) // tm


def _slot_bounds(bounds):
    return bounds[0:1, :], bounds[1:2, :]


def _moe_gather_kernel(dst_ref, nchunk_ref, pad_row_ref, pad_n_ref,
                       x_ref, chosen_t_ref, rank_t_ref, bounds_ref, rows_ref, buf, zbuf, sem, zsem):
    j = pl.program_id(0)
    nblk = pl.num_programs(0)
    slot = j % 2
    nchunk_max = MOE_CAP // MOE_SEG

    def chunk_copy(blk, q, sl):
        src = buf.at[sl, pl.ds(pl.multiple_of(q * MOE_SEG, MOE_SEG), MOE_SEG), :]
        row = pl.multiple_of(dst_ref[blk * nchunk_max + q], MOE_SEG)
        return pltpu.make_async_copy(src, rows_ref.at[pl.ds(row, MOE_SEG), :], sem.at[sl])

    def pad_copy(e, i):
        row = pl.multiple_of(pad_row_ref[e] + i * MOE_SEG, MOE_SEG)
        return pltpu.make_async_copy(zbuf, rows_ref.at[pl.ds(row, MOE_SEG), :], zsem)

    def sub_chunks(blk, sub, sl, action):
        for q in range(sub * (MOE_SUB // MOE_SEG), (sub + 1) * (MOE_SUB // MOE_SEG)):
            action(chunk_copy(blk, q, sl))

    def sub_used(blk, sub):
        return True if sub < MOE_SUB_ALWAYS else sub * MOE_SUB < nchunk_ref[blk] * MOE_SEG

    def for_pads(action):
        def per_expert(e, carry):
            def body(i, c):
                action(pad_copy(e, i))
                return c
            return lax.fori_loop(0, pad_n_ref[e], body, carry)
        lax.fori_loop(0, N_EXPERTS, per_expert, 0)

    @pl.when(j == 0)
    def _():
        zbuf[...] = jnp.zeros_like(zbuf)
        for_pads(lambda c: c.start())

    start, length = _slot_bounds(bounds_ref[0])
    x = x_ref[...]
    chosen_rank_t = jnp.concatenate([chosen_t_ref[...], rank_t_ref[...]], axis=1)
    before = jnp.maximum(j - 2, 0)
    subs = range(MOE_CAP // MOE_SUB)

    @pl.when(j >= 2)
    def _():
        for sub in subs:
            @_when(sub_used(before, sub))
            def _():
                sub_chunks(before, sub, slot, lambda c: c.wait())

    def owners(sub):
        s = (sub * MOE_SUB + lax.broadcasted_iota(jnp.int32, (MOE_SUB, LANES), 0)).astype(F32)
        owner = (s >= start) & (s < start + length)
        within = s[:, 0:1] - jnp.sum(jnp.where(owner, start, 0.0), axis=1, keepdims=True)
        return owner.astype(BF16), within

    def picks(owner_within):
        owner, within = owner_within
        hit = _dot(owner, chosen_rank_t)
        return ((hit[:, :MOE_BLOCK] > 0.5) & (hit[:, MOE_BLOCK:] == within)).astype(BF16)

    def emit(sub, pick):
        buf[slot, sub * MOE_SUB:(sub + 1) * MOE_SUB, :] = _dot(pick, x).astype(BF16)
        sub_chunks(j, sub, slot, lambda c: c.start())

    staged = [picks(ow) for ow in [owners(sub) for sub in subs[:MOE_SUB_ALWAYS]]]
    for sub, pick in enumerate(staged):
        emit(sub, pick)
    for sub in subs[MOE_SUB_ALWAYS:]:
        @pl.when(sub_used(j, sub))
        def _():
            emit(sub, picks(owners(sub)))

    @pl.when(j == nblk - 1)
    def _():
        prev = jnp.maximum(j - 1, 0)
        for sub in subs:
            @_when(sub_used(j, sub))
            def _():
                sub_chunks(j, sub, slot, lambda c: c.wait())

            @_when((j >= 1) & sub_used(prev, sub))
            def _():
                sub_chunks(prev, sub, 1 - slot, lambda c: c.wait())

        for_pads(lambda c: c.wait())


def _moe_gather(x1b, chosen_t, rank_t, tables, tm):
    m = x1b.shape[0]
    nblk = m // MOE_BLOCK
    rows = _moe_max_tiles(nblk, tm) * tm + 2 * MOE_CAP
    blk = lambda j, *_: (j, 0)
    grid_spec = pltpu.PrefetchScalarGridSpec(
        num_scalar_prefetch=4,
        grid=(nblk,),
        in_specs=[pl.BlockSpec((MOE_BLOCK, D_MODEL), blk), pl.BlockSpec((LANES, MOE_BLOCK), blk),
                  pl.BlockSpec((LANES, MOE_BLOCK), blk),
                  pl.BlockSpec((1, SUBLANES, LANES), lambda j, *_: (j, 0, 0))],
        out_specs=pl.BlockSpec(memory_space=pl.ANY),
        scratch_shapes=[pltpu.VMEM((2, MOE_CAP, D_MODEL), BF16), pltpu.VMEM((MOE_SEG, D_MODEL), BF16),
                        pltpu.SemaphoreType.DMA((2,)), pltpu.SemaphoreType.DMA(())],
    )
    return pl.pallas_call(
        _moe_gather_kernel,
        grid_spec=grid_spec,
        out_shape=jax.ShapeDtypeStruct((rows, D_MODEL), BF16),
        compiler_params=_params("arbitrary"),
        name="moe_gather",
    )(tables["dst"], tables["nchunk"], tables["pad_row"], tables["pad_n"], x1b, chosen_t, rank_t, tables["bounds"])


def _moe_ffn_kernel(tile_expert_ref, tile_slot_ref, tile_next_ref, x_ref, w1_hbm, w3_hbm, w2_hbm, y_ref,
                    w1_buf, w3_buf, w2_buf, w13_scr, w2_scr, sem):
    i = pl.program_id(0)
    expert = tile_expert_ref[i]
    slot = tile_slot_ref[i]

    def weight_copies(e, s):
        pairs = ((w1_hbm, w1_buf), (w3_hbm, w3_buf), (w2_hbm, w2_buf))
        return [pltpu.make_async_copy(w.at[e], buf.at[s], sem.at[s, n]) for n, (w, buf) in enumerate(pairs)]

    @pl.when(i == 0)
    def _():
        for copy in weight_copies(expert, slot):
            copy.start()

    @pl.when((i == 0) | (expert != tile_expert_ref[jnp.maximum(i - 1, 0)]))
    def _():
        for copy in weight_copies(expert, slot):
            copy.wait()

        @pl.when(tile_next_ref[i] >= 0)
        def _():
            for copy in weight_copies(tile_next_ref[i], 1 - slot):
                copy.start()

        w13_scr[:, :EXPERT_FF] = w1_buf[slot].astype(BF16)
        w13_scr[:, EXPERT_FF:] = w3_buf[slot].astype(BF16)
        w2_scr[...] = w2_buf[slot].astype(BF16)

    up = _dot(x_ref[...], w13_scr[...])
    hid = _silu(up[:, :EXPERT_FF]) * up[:, EXPERT_FF:]
    y_ref[...] = _dot(hid.astype(BF16), w2_scr[...]).astype(BF16)


def _moe_ffn(rows, w1, w3, w2, tables, tm):
    tile = lambda i, *_: (i, 0)
    grid_spec = pltpu.PrefetchScalarGridSpec(
        num_scalar_prefetch=3,
        grid=(tables["tiles_used"][0],),
        in_specs=[pl.BlockSpec((tm, D_MODEL), tile)] + [pl.BlockSpec(memory_space=pl.ANY)] * 3,
        out_specs=pl.BlockSpec((tm, D_MODEL), tile),
        scratch_shapes=[pltpu.VMEM((2, D_MODEL, EXPERT_FF), F32), pltpu.VMEM((2, D_MODEL, EXPERT_FF), F32),
                        pltpu.VMEM((2, EXPERT_FF, D_MODEL), F32),
                        pltpu.VMEM((D_MODEL, 2 * EXPERT_FF), BF16), pltpu.VMEM((EXPERT_FF, D_MODEL), BF16),
                        pltpu.SemaphoreType.DMA((2, 3))],
    )
    return pl.pallas_call(
        _moe_ffn_kernel,
        grid_spec=grid_spec,
        out_shape=jax.ShapeDtypeStruct(rows.shape, BF16),
        compiler_params=_params("arbitrary"),
        name="moe_ffn",
    )(tables["tile_expert"], tables["tile_slot"], tables["tile_next"], rows, w1, w3, w2)


def _moe_combine_kernel(dst_ref, nchunk_ref, dense_ref, gates_ref, rank_ref, bounds_ref, g_ref, b_ref, rows_ref,
                        op_ref, os_ref, acc, buf, sem, *, prompt_tiles):
    j = pl.program_id(0)
    nblk = pl.num_programs(0)
    slot = j % 2
    nchunk_max = MOE_CAP // MOE_SEG

    def chunk_copy(blk, q, sl):
        row = pl.multiple_of(dst_ref[blk * nchunk_max + q], MOE_SEG)
        dst = buf.at[sl, pl.ds(pl.multiple_of(q * MOE_SEG, MOE_SEG), MOE_SEG), :]
        return pltpu.make_async_copy(rows_ref.at[pl.ds(row, MOE_SEG), :], dst, sem.at[sl])

    subs = range(MOE_CAP // MOE_SUB)
    always, rest = subs[:MOE_SUB_ALWAYS], subs[MOE_SUB_ALWAYS:]

    def sub_used(blk, sub):
        return True if sub < MOE_SUB_ALWAYS else sub * MOE_SUB < nchunk_ref[blk] * MOE_SEG

    def group(blk, sub, sl, action):
        for q in range(sub * (MOE_SUB // MOE_SEG), (sub + 1) * (MOE_SUB // MOE_SEG)):
            action(chunk_copy(blk, q, sl))

    @pl.when(j == 0)
    def _():
        buf[...] = jnp.zeros_like(buf)
        for sub in subs:
            @_when(sub_used(0, sub))
            def _():
                group(0, sub, 0, lambda c: c.start())

    for sub in subs:
        @_when(sub_used(j, sub))
        def _():
            group(j, sub, slot, lambda c: c.wait())

    bounds_t = jnp.concatenate([bounds_ref[0], jnp.zeros((LANES - SUBLANES, LANES), F32)], axis=0).T
    start, length = bounds_t[:, 0:1], bounds_t[:, 1:2]
    gates_rank = jnp.concatenate([gates_ref[...], rank_ref[...]], axis=0)

    def owners(sub):
        s = (sub * MOE_SUB + lax.broadcasted_iota(jnp.int32, (LANES, MOE_SUB), 1)).astype(F32)
        owner = (s >= start) & (s < start + length)
        within = s[0:1, :] - jnp.sum(jnp.where(owner, start, 0.0), axis=0, keepdims=True)
        return owner.astype(BF16), within

    def weights(owner_within):
        owner, within = owner_within
        hit = _dot(gates_rank, owner)
        return jnp.where(hit[MOE_BLOCK:] == within, hit[:MOE_BLOCK], 0.0).astype(BF16)

    def apply(sub, weight):
        return _dot(weight, buf[slot, sub * MOE_SUB:(sub + 1) * MOE_SUB, :])

    nxt = jnp.minimum(j + 1, nblk - 1)
    staged = [owners(sub) for sub in always]
    for sub in always:
        group(nxt, sub, 1 - slot, lambda c: c.start())
    staged = [weights(ow) for ow in staged]
    total = dense_ref[...]
    for sub, weight in zip(always, staged):
        total = total + apply(sub, weight)
    acc[...] = total
    for sub in rest:
        @pl.when((j + 1 < nblk) & sub_used(nxt, sub))
        def _():
            group(nxt, sub, 1 - slot, lambda c: c.start())

        @pl.when(sub_used(j, sub))
        def _():
            acc[...] += apply(sub, weights(owners(sub)))

    @pl.when(j == nblk - 1)
    def _():
        for sub in always:
            group(nxt, sub, 1 - slot, lambda c: c.wait())

    @pl.when(j < prompt_tiles)
    def _():
        op_ref[...] = _layer_norm(acc[...], g_ref[...], b_ref[...])

    @pl.when(j >= prompt_tiles)
    def _():
        os_ref[...] = _layer_norm(acc[...], g_ref[...], b_ref[...])


def _moe_combine(rows, dense, gates, rank, tables, g, b, n_prompt):
    m = dense.shape[0]
    nblk = m // MOE_BLOCK
    prompt_tiles = n_prompt // MOE_BLOCK
    in_prompt, in_sample = _group_maps(prompt_tiles)
    blk = lambda j, *_: (j, 0)
    const = lambda j, *_: (0, 0)
    grid_spec = pltpu.PrefetchScalarGridSpec(
        num_scalar_prefetch=2,
        grid=(nblk,),
        in_specs=[pl.BlockSpec((MOE_BLOCK, D_MODEL), blk), pl.BlockSpec((MOE_BLOCK, LANES), blk),
                  pl.BlockSpec((MOE_BLOCK, LANES), blk),
                  pl.BlockSpec((1, SUBLANES, LANES), lambda j, *_: (j, 0, 0)),
                  pl.BlockSpec((1, D_MODEL), const), pl.BlockSpec((1, D_MODEL), const),
                  pl.BlockSpec(memory_space=pl.ANY)],
        out_specs=[pl.BlockSpec((MOE_BLOCK, D_MODEL), in_prompt), pl.BlockSpec((MOE_BLOCK, D_MODEL), in_sample)],
        scratch_shapes=[pltpu.VMEM((MOE_BLOCK, D_MODEL), F32), pltpu.VMEM((2, MOE_CAP, D_MODEL), BF16),
                        pltpu.SemaphoreType.DMA((2,))],
    )
    return pl.pallas_call(
        functools.partial(_moe_combine_kernel, prompt_tiles=prompt_tiles),
        grid_spec=grid_spec,
        out_shape=[jax.ShapeDtypeStruct((n_prompt, D_MODEL), F32),
                   jax.ShapeDtypeStruct((m - n_prompt, D_MODEL), F32)],
        compiler_params=_params("arbitrary"),
        name="moe_combine",
    )(tables["src"], tables["nchunk"], dense, gates, rank, tables["bounds"], g, b, rows)


def _row(v):
    return v.reshape(1, -1).astype(F32)


def _scan_layout(batch, seq, row0):
    t = math.gcd(seq, CHUNK)
    nchunk = seq // t
    chunks = SCAN_CHUNKS if nchunk % SCAN_CHUNKS == 0 else 1
    seqs = SCAN_SEQS if (nchunk == 1 and batch % SCAN_SEQS == 0 and row0 % (SCAN_SEQS * t) == 0) else 1
    return dict(batch=batch, seq=seq, row0=row0, t=t, seqs=seqs, chunks=chunks)


def _lanes_at(v, start):
    return jnp.zeros((1, LANES), F32).at[0, start:start + v.shape[0]].set(v.astype(F32))


def kernel(x_prompt, x_sample, p_prompt, p_sample, state_ssm_conv, state_ssm, state_dn_conv, state_dn, emb_ln_g, emb_ln_b, w_in, conv_a_w, conv_a_b, ssm_dt_bias, ssm_a_log, ssm_d, ssm_norm_w, w_a, conv_b_w, dn_dt_bias, dn_a_log, dn_norm_w, w_b, w_o, ln1_g, ln1_b, router_w, router_bias, exp_w1, exp_w3, exp_w2, sh_w1, sh_w3, sh_w2, ple_w, ple_gate_w, ln2_g, ln2_b):
    bp, lp, _ = x_prompt.shape
    bs, ls, _ = x_sample.shape
    n_p = bp * lp
    n_s = bs * ls
    xp = x_prompt.reshape(n_p, D_MODEL)
    xs = x_sample.reshape(n_s, D_MODEL)

    w = w_in[0]
    o_z, o_xbc, o_dt = 0, SSM_INNER, SSM_INNER + SSM_CONV_CH
    o_qkv = o_dt + SSM_HEADS
    o_a = o_qkv + DN_CONV_CH
    o_b = o_a + DN_HEADS
    o_zb = o_b + DN_HEADS
    o_ga = o_zb + DN_V
    w_pa = w[:, o_z:o_dt].astype(BF16)
    w_pb = jnp.concatenate([w[:, o_qkv:o_a], w[:, o_zb:o_ga]], axis=1).astype(BF16)
    w_pg = w[:, o_ga:].astype(BF16)
    w_ps = jnp.zeros((D_MODEL, LANES), F32)
    w_ps = w_ps.at[:, SMALL_DT:SMALL_DT + SSM_HEADS].set(w[:, o_dt:o_qkv])
    w_ps = w_ps.at[:, SMALL_A:SMALL_A + 2 * DN_HEADS].set(w[:, o_a:o_zb]).astype(BF16)

    eg, eb = _row(emb_ln_g), _row(emb_ln_b)
    proj_a, small = _ln_matmul(xp, xs, eg, eb, w_pa, PROJ_TM, PROJ_TN_A, w_ps)
    proj_b = _ln_matmul(xp, xs, eg, eb, w_pb, PROJ_TM, PROJ_TN_B)

    ssd_consts = [conv_a_w[0], _row(conv_a_b[0]), _lanes_at(ssm_dt_bias[0], SMALL_DT),
                  _lanes_at(ssm_a_log[0], SMALL_DT), _row(jnp.repeat(ssm_d[0], SSM_HEAD_DIM)),
                  _row(ssm_norm_w[0])]
    gdn_consts = [conv_b_w[0], _lanes_at(dn_a_log[0], SMALL_A), _lanes_at(dn_dt_bias[0], SMALL_A),
                  _row(dn_norm_w[0])]
    zeros = lambda *s: jnp.zeros(s, F32)
    prompt = _scan_layout(bp, lp, 0)
    sample = _scan_layout(bs, ls, n_p)

    ya_p, pa_conv, pa_ssm = _ssd(proj_a, small, ssd_consts, zeros(bp, CONV_WIDTH - 1, SSM_CONV_CH),
                                 zeros(bp, SSM_HEADS, SSM_HEAD_DIM, SSM_STATE), **prompt)
    ya_s, sa_conv, sa_ssm = _ssd(proj_a, small, ssd_consts, state_ssm_conv[0], state_ssm[0], **sample)
    ob_p, pb_conv, pb_dn = _gdn(proj_b, small, gdn_consts, zeros(bp, CONV_WIDTH - 1, DN_CONV_CH),
                                zeros(bp, DN_HEADS, DN_HEAD, DN_HEAD), **prompt)
    ob_s, sb_conv, sb_dn = _gdn(proj_b, small, gdn_consts, state_dn_conv[0], state_dn[0], **sample)

    x1, x1b = _merge(
        xp, xs, ya_p, ya_s, ob_p, ob_s,
        [eg, eb, w_pg, w_a[0].astype(BF16), w_b[0].astype(BF16), w_o[0].astype(BF16), _row(ln1_g[0]), _row(ln1_b[0])],
        MERGE_TM)

    router_w_t = jnp.zeros((LANES, D_MODEL), F32).at[:N_EXPERTS].set(router_w[0].T)
    router_w_hi = router_w_t.astype(BF16)
    router_w_lo = (router_w_t - router_w_hi.astype(F32)).astype(BF16)
    router_b = jnp.broadcast_to(_lanes_at(router_bias[0], 0).reshape(LANES, 1), (LANES, MOE_BLOCK))
    dense, gates, rank, chosen_t, rank_t, counts = _dense_router(
        x1, x1b, p_prompt[0].reshape(n_p, PLE_DIM), p_sample[0].reshape(n_s, PLE_DIM),
        [sh_w1[0].astype(BF16), sh_w3[0].astype(BF16), sh_w2[0].astype(BF16), ple_w[0].astype(BF16),
         ple_gate_w[0].astype(BF16), router_w_hi, router_w_lo, router_b])

    tables = _moe_tables(counts[:, 0, :N_EXPERTS].astype(jnp.int32), MOE_TM)
    sorted_rows = _moe_gather(x1b, chosen_t, rank_t, tables, MOE_TM)
    expert_out = _moe_ffn(sorted_rows, exp_w1[0], exp_w3[0], exp_w2[0], tables, MOE_TM)
    out_p, out_s = _moe_combine(expert_out, dense, gates, rank, tables, _row(ln2_g[0]), _row(ln2_b[0]), n_p)

    return (out_p.reshape(bp, lp, D_MODEL), out_s.reshape(bs, ls, D_MODEL),
            pa_conv[None], pa_ssm[None], pb_conv[None], pb_dn[None],
            sa_conv[None], sa_ssm[None], sb_conv[None], sb_dn[None])
```

```python
import functools
import math

import jax
import jax.numpy as jnp
from jax import lax
from jax.experimental import pallas as pl
from jax.experimental.pallas import tpu as pltpu

F32 = jnp.float32
BF16 = jnp.bfloat16
HIGHEST = lax.Precision.HIGHEST

D_MODEL = 1024
SSM_INNER = 2048
SSM_HEAD_DIM = 64
SSM_HEADS = 32
SSM_GROUPS = 2
SSM_HEADS_PER_GROUP = 16
SSM_STATE = 128
SSM_CONV_CH = 2560
DN_HEADS = 8
DN_HEAD = 128
DN_QK = 1024
DN_V = 1024
DN_CONV_CH = 3072
CONV_WIDTH = 4
CHUNK = 64
N_EXPERTS = 64
TOP_K = 8
N_EXPERT_GROUPS = 8
EXPERTS_PER_GROUP = 8
TOPK_GROUPS = 4
EXPERT_FF = 256
SHARED_FF = 256
ROUTED_SCALE = 2.5
PLE_DIM = 256
LN_EPS = 1e-5
RMS_EPS = 1e-6
L2_EPS = 1e-6
DEEPNORM_ALPHA = 2.0 ** 0.25

LANES = 128
SUBLANES = 8
VMEM_LIMIT = 56 * 1024 * 1024
SMALL_DT = 0
SMALL_A = 32
SMALL_B = 40
PROJ_TM = 1024
PROJ_TN_A = 2304
PROJ_TN_B = 2048
MERGE_TM = 512
SCAN_CHUNKS = 8
SCAN_SEQS = 8
MOE_BLOCK = 256
MOE_SEG = 16
MOE_SUB = 512
MOE_CAP = -(-(MOE_BLOCK * TOP_K + N_EXPERTS * (MOE_SEG - 1)) // MOE_SUB) * MOE_SUB
MOE_TM = 512
ROUTER_TM = 512
MOE_SUB_ALWAYS = -(-(MOE_BLOCK * TOP_K + N_EXPERTS * MOE_SEG // 2) // MOE_SUB)


def _sigmoid(x):
    return 1.0 / (1.0 + jnp.exp(-x))


def _silu(x):
    return x * _sigmoid(x)


def _softplus(x):
    return jnp.maximum(x, 0.0) + jnp.log(1.0 + jnp.exp(-jnp.abs(x)))


def _layer_norm(x, g, b):
    mu = jnp.mean(x, axis=-1, keepdims=True)
    xc = x - mu
    var = jnp.mean(xc * xc, axis=-1, keepdims=True)
    return xc * lax.rsqrt(var + LN_EPS) * g + b


def _dot(a, b):
    return jnp.dot(a, b, preferred_element_type=F32)


def _dot_nt(a, b):
    return lax.dot_general(a, b, (((1,), (1,)), ((), ())), preferred_element_type=F32)


def _dot_tn(a, b):
    return lax.dot_general(a, b, (((0,), (0,)), ((), ())), preferred_element_type=F32)


def _dot_f32(a, b):
    return jnp.dot(a, b, precision=HIGHEST, preferred_element_type=F32)


def _params(*sem):
    return pltpu.CompilerParams(dimension_semantics=sem, vmem_limit_bytes=VMEM_LIMIT)


def _when(cond):
    if cond is True:
        return lambda fn: fn()
    if cond is False:
        return lambda fn: None
    return pl.when(cond)


def _row_tile(m, preferred):
    return max(d for d in range(SUBLANES, min(m, preferred) + 1, SUBLANES) if m % d == 0)


def _group_maps(prompt_tiles):
    in_prompt = lambda i, *_: (jnp.minimum(i, prompt_tiles - 1), 0)
    in_sample = lambda i, *_: (jnp.maximum(i - prompt_tiles, 0), 0)
    return in_prompt, in_sample


def _ln_matmul_kernel(xp_ref, xs_ref, g_ref, b_ref, w_ref, *rest, prompt_tiles, narrow):
    if narrow:
        wn_ref, o_ref, on_ref, h_scr = rest
    else:
        o_ref, h_scr = rest

    @pl.when(pl.program_id(1) == 0)
    def _():
        x = jnp.where(pl.program_id(0) < prompt_tiles, xp_ref[...], xs_ref[...])
        h_scr[...] = _layer_norm(x, g_ref[...], b_ref[...]).astype(BF16)
        if narrow:
            on_ref[...] = _dot(h_scr[...], wn_ref[...])

    o_ref[...] = _dot(h_scr[...], w_ref[...]).astype(o_ref.dtype)


def _ln_matmul(xp, xs, g, b, w, tm, tn, w_narrow=None):
    k = xp.shape[1]
    m = xp.shape[0] + xs.shape[0]
    n = w.shape[1]
    tm = _row_tile(math.gcd(xp.shape[0], xs.shape[0]), tm)
    prompt_tiles = xp.shape[0] // tm
    in_prompt, in_sample = _group_maps(prompt_tiles)
    narrow = w_narrow is not None
    const = lambda i, j: (0, 0)
    row = lambda i, j: (i, 0)
    in_specs = [pl.BlockSpec((tm, k), in_prompt), pl.BlockSpec((tm, k), in_sample),
                pl.BlockSpec((1, k), const), pl.BlockSpec((1, k), const), pl.BlockSpec((k, tn), lambda i, j: (0, j))]
    out_specs = [pl.BlockSpec((tm, tn), lambda i, j: (i, j))]
    out_shape = [jax.ShapeDtypeStruct((m, n), BF16)]
    args = [xp, xs, g, b, w]
    if narrow:
        in_specs.append(pl.BlockSpec((k, LANES), const))
        out_specs.append(pl.BlockSpec((tm, LANES), row))
        out_shape.append(jax.ShapeDtypeStruct((m, LANES), F32))
        args.append(w_narrow)
    out = pl.pallas_call(
        functools.partial(_ln_matmul_kernel, prompt_tiles=prompt_tiles, narrow=narrow),
        grid=(m // tm, n // tn),
        in_specs=in_specs,
        out_specs=out_specs,
        out_shape=out_shape,
        scratch_shapes=[pltpu.VMEM((tm, k), BF16)],
        compiler_params=_params("parallel", "arbitrary"),
        name="ln_matmul",
    )(*args)
    return out if narrow else out[0]


def _load_conv_state(cbuf, state):
    taps = CONV_WIDTH - 1
    cbuf[0:SUBLANES - taps, :] = jnp.zeros((SUBLANES - taps, cbuf.shape[1]), F32)
    cbuf[SUBLANES - taps:SUBLANES, :] = state


def _causal_conv(cbuf, x16, cw, t):
    ch = x16.shape[1]
    prev = cbuf[...]
    taps = CONV_WIDTH - 1
    if t <= SUBLANES:
        x = x16.astype(F32)
        ext = jnp.concatenate([prev, x], axis=0)
        lo = SUBLANES - taps
        y = ext[lo:lo + t] * cw[0:1]
        y = y + ext[lo + 1:lo + 1 + t] * cw[1:2]
        y = y + ext[lo + 2:lo + 2 + t] * cw[2:3]
        y = y + x * cw[3:4]
        cbuf[...] = ext[t:t + SUBLANES]
        return y, ext[SUBLANES + t - taps:SUBLANES + t]
    hi = prev.astype(BF16).astype(F32)
    mid = (prev - hi).astype(BF16).astype(F32)
    low = ((prev - hi) - mid).astype(BF16).astype(F32)
    x = x16.astype(F32)
    pack = 2 * SUBLANES
    nhead = 4 * SUBLANES if t % pack == 0 else 5 * SUBLANES
    head = jnp.concatenate([hi, mid, low, jnp.zeros((nhead - 3 * SUBLANES, ch), F32)], axis=0)
    if t % pack == 0:
        ext = jnp.concatenate([head.astype(BF16), x16], axis=0)
    else:
        ext = jnp.concatenate([head, x], axis=0).astype(BF16)
    out_row = lax.broadcasted_iota(jnp.int32, (taps * t, nhead + t), 0)
    col = lax.broadcasted_iota(jnp.int32, (taps * t, nhead + t), 1)
    log_t = t.bit_length() - 1
    shift = lax.shift_right_logical(out_row, log_t) + 1
    src = (out_row & (t - 1)) - shift
    in_x = (src >= 0) & (col == nhead + src)
    in_prev = (src < 0) & (col < 3 * SUBLANES) & ((col & (SUBLANES - 1)) == SUBLANES + src)
    shifted = _dot((in_x | in_prev).astype(BF16), ext)
    y = shifted[2 * t:3 * t] * cw[0:1]
    y = y + shifted[t:2 * t] * cw[1:2]
    y = y + shifted[0:t] * cw[2:3]
    y = y + x * cw[3:4]
    cbuf[...] = x[t - SUBLANES:t]
    return y, x[t - taps:t]


def _lane_expand(v, h0, count, width):
    t = v.shape[0]
    n = count * width
    out = jnp.broadcast_to(v[:, h0:h0 + 1], (t, n))
    if count > 1:
        lane = lax.broadcasted_iota(jnp.int32, (t, n), 1)
        for i in range(1, count):
            out = jnp.where(lane >= i * width, jnp.broadcast_to(v[:, h0 + i:h0 + i + 1], (t, n)), out)
    return out


def _grouped_transpose(v, per_tile):
    t = v.shape[0]
    blocks = [v if r == 0 else pltpu.roll(v, LANES - r, axis=1) for r in range(per_tile)]
    if per_tile * t < LANES:
        blocks.append(jnp.zeros((LANES - per_tile * t, LANES), F32))
    return jnp.concatenate(blocks, axis=0).T


def _same_block(i, j, size):
    shift = size.bit_length() - 1
    return lax.shift_right_logical(i, shift) == lax.shift_right_logical(j, shift)


def _unit_lower_inverses(lmats, ii, jj, t):
    mm = lambda a, b: _dot(a.astype(BF16), b.astype(BF16))
    base = min(16, t)
    in_base = _same_block(ii, jj, base)
    eye = (ii == jj).astype(F32)
    power = [jnp.where(in_base, l, 0.0) for l in lmats]
    inv = [eye - p for p in power]
    span = 2
    while span < base:
        power = [mm(p, p) for p in power]
        inv = [a + mm(a, p) for a, p in zip(inv, power)]
        span *= 2
    size = base
    while size < t:
        link = _same_block(ii, jj, 2 * size) & jnp.logical_not(_same_block(ii, jj, size))
        cross = [mm(a, jnp.where(link, l, 0.0)) for a, l in zip(inv, lmats)]
        inv = [a - mm(c, a) for a, c in zip(inv, cross)]
        size *= 2
    return inv


def _row_sums_of_squares(blocks):
    t = blocks[0].shape[0]
    sq = jnp.concatenate([b * b for b in blocks], axis=0)
    hi = sq.astype(BF16)
    lo = (sq - hi.astype(F32)).astype(BF16)
    ones = jnp.ones((LANES, LANES), BF16)
    sums = _dot(hi, ones) + _dot(lo, ones)
    return [sums[i * t:(i + 1) * t] for i in range(len(blocks))]


def _cumsum_rows(v, t):
    ii = lax.broadcasted_iota(jnp.int32, (t, t), 0)
    jj = lax.broadcasted_iota(jnp.int32, (t, t), 1)
    return _dot_f32((jj <= ii).astype(F32), v)


def _ssd_kernel(pa_ref, sm_ref, cw_ref, cb_ref, dtb_ref, alog_ref, dskip_ref, nw_ref, cst_ref, h0_ref,
                y_ref, cnew_ref, hnew_ref, cbuf, s_scr, *, t, first, last):
    per_tile = LANES // t
    width = per_tile * SSM_HEAD_DIM

    @_when(first)
    def _():
        _load_conv_state(cbuf, cst_ref[0])
        s_scr[...] = h0_ref[0]

    z = pa_ref[:, :SSM_INNER].astype(F32)
    conv, tail = _causal_conv(cbuf, pa_ref[:, SSM_INNER:], cw_ref[...], t)

    @_when(last)
    def _():
        cnew_ref[0] = tail

    xbc = _silu(conv + cb_ref[...])
    xs = xbc[:, :SSM_INNER]
    bm = xbc[:, SSM_INNER:SSM_INNER + SSM_GROUPS * SSM_STATE]
    cm = xbc[:, SSM_INNER + SSM_GROUPS * SSM_STATE:]

    dt = _softplus(sm_ref[...] + dtb_ref[...])
    da = dt * (-jnp.exp(alog_ref[...]))
    cum = _cumsum_rows(da, t)
    cum_t = _grouped_transpose(cum, per_tile)
    dt_t = _grouped_transpose(dt, per_tile)

    row = lax.broadcasted_iota(jnp.int32, (t, LANES), 0)
    lane = lax.broadcasted_iota(jnp.int32, (t, LANES), 1)
    causal = (lane % t) <= row
    brow = lax.broadcasted_iota(jnp.int32, (LANES, width), 0)
    bcol = lax.broadcasted_iota(jnp.int32, (LANES, width), 1)
    blockdiag = (brow // t) == (bcol // SSM_HEAD_DIM)

    ys = []
    for g in range(SSM_GROUPS):
        bg = bm[:, g * SSM_STATE:(g + 1) * SSM_STATE]
        cg = cm[:, g * SSM_STATE:(g + 1) * SSM_STATE].astype(BF16)
        cb = _dot_nt(cg, jnp.concatenate([bg] * per_tile, axis=0).astype(BF16))
        bg = bg.astype(BF16)
        for q in range(SSM_HEADS_PER_GROUP // per_tile):
            h0 = g * SSM_HEADS_PER_GROUP + q * per_tile
            ccol = _lane_expand(cum, h0, per_tile, t)
            seg = ccol - cum_t[h0:h0 + 1, :]
            decay = jnp.exp(jnp.where(causal, seg, -jnp.inf))
            wts = (cb * decay * dt_t[h0:h0 + 1, :]).astype(BF16)
            xt = xs[:, h0 * SSM_HEAD_DIM:h0 * SSM_HEAD_DIM + width]
            xbd = jnp.where(blockdiag, jnp.concatenate([xt] * per_tile, axis=0), 0.0).astype(BF16)
            y_intra = _dot(wts, xbd)
            st = s_scr[h0:h0 + per_tile].reshape(width, SSM_STATE)
            cum_w = ccol if width == LANES else _lane_expand(cum, h0, per_tile, SSM_HEAD_DIM)
            y_state = _dot_nt(cg, st.astype(BF16)) * jnp.exp(cum_w)
            ys.append(y_intra + y_state)
            wend = jnp.exp(cum_w[t - 1:t, :] - cum_w) * _lane_expand(dt, h0, per_tile, SSM_HEAD_DIM)
            ds = _dot_tn((xt * wend).astype(BF16), bg)
            for i in range(per_tile):
                h = h0 + i
                s_scr[h] = jnp.exp(cum[t - 1:t, h:h + 1]) * s_scr[h] + ds[i * SSM_HEAD_DIM:(i + 1) * SSM_HEAD_DIM]

    y = jnp.concatenate(ys, axis=1) + dskip_ref[...] * xs
    y = y * _silu(z)
    half = SSM_INNER // SSM_GROUPS
    normed = []
    for g in range(SSM_GROUPS):
        yg = y[:, g * half:(g + 1) * half]
        normed.append(yg * lax.rsqrt(jnp.mean(yg * yg, axis=-1, keepdims=True) + RMS_EPS))
    y_ref[...] = (jnp.concatenate(normed, axis=1) * nw_ref[...]).astype(BF16)

    @_when(last)
    def _():
        hnew_ref[0] = s_scr[...]


def _scan_step_kernel(chunk_kernel, n_tok, n_const, n_state, seqs, chunks, t):
    def step(*refs):
        tok = refs[:n_tok]
        const = refs[n_tok:n_tok + n_const]
        st_in = refs[n_tok + n_const:n_tok + n_const + n_state]
        y_ref = refs[n_tok + n_const + n_state]
        st_out = refs[n_tok + n_const + n_state + 1:n_tok + n_const + 2 * n_state + 1]
        scratch = refs[n_tok + n_const + 2 * n_state + 1:]
        first_step = pl.program_id(1) == 0
        last_step = pl.program_id(1) == pl.num_programs(1) - 1
        for s in range(seqs):
            for c in range(chunks):
                rows = pl.ds((s * chunks + c) * t, t)
                chunk_kernel(*[r.at[rows, :] for r in tok], *const, *[r.at[pl.ds(s, 1)] for r in st_in],
                             y_ref.at[rows, :], *[r.at[pl.ds(s, 1)] for r in st_out], *scratch,
                             first=first_step if c == 0 else False,
                             last=last_step if c == chunks - 1 else False)
    return step


def _scan_call(kernel, name, tok_in, const_in, state_in, y_width, scratch, *, batch, seq, row0, t, seqs, chunks,
               step_kernel=None):
    nstep = seq // (t * chunks)
    rows = seqs * chunks * t
    blk0 = row0 // rows

    def per_batch(a):
        zeros = (0,) * (a.ndim - 1)
        return pl.BlockSpec((seqs,) + a.shape[1:], lambda b, c: (b,) + zeros)

    if step_kernel is None:
        step_kernel = _scan_step_kernel(kernel, len(tok_in), len(const_in), len(state_in), seqs, chunks, t)
    return pl.pallas_call(
        step_kernel,
        grid=(batch // seqs, nstep),
        in_specs=[pl.BlockSpec((rows, a.shape[1]), lambda b, c: (blk0 + b * nstep + c, 0)) for a in tok_in]
        + [pl.BlockSpec(a.shape, lambda b, c: (0, 0)) for a in const_in]
        + [per_batch(a) for a in state_in],
        out_specs=[pl.BlockSpec((rows, y_width), lambda b, c: (b * nstep + c, 0))]
        + [per_batch(a) for a in state_in],
        out_shape=[jax.ShapeDtypeStruct((batch * seq, y_width), BF16)]
        + [jax.ShapeDtypeStruct(a.shape, F32) for a in state_in],
        scratch_shapes=scratch,
        compiler_params=_params("parallel", "arbitrary"),
        name=name,
    )(*tok_in, *const_in, *state_in)


def _ssd(proj_a, small, consts, conv_state, h0, **where):
    t = where["t"]
    scratch = [pltpu.VMEM((SUBLANES, SSM_CONV_CH), F32),
               pltpu.VMEM((SSM_HEADS, SSM_HEAD_DIM, SSM_STATE), F32)]
    return _scan_call(functools.partial(_ssd_kernel, t=t), f"ssd_t{t}", [proj_a, small], consts,
                      [conv_state, h0], SSM_INNER, scratch, **where)


def _gdn_step_kernel(pb_ref, sm_ref, cw_ref, alog_ref, dtb_ref, nw_ref, cst_ref, s0_ref,
                     o_ref, cnew_ref, snew_ref, cbuf, s_scr, *, t, seqs, chunks):
    first_step = pl.program_id(1) == 0
    last_step = pl.program_id(1) == pl.num_programs(1) - 1
    units = [(s, c) for s in range(seqs) for c in range(chunks)]
    rows = lambda u: pl.ds(u * t, t)
    per_tile = min(LANES // t, DN_HEADS)
    n = per_tile * t
    ntile = DN_HEADS // per_tile
    heads = range(DN_HEADS)
    ii = lax.broadcasted_iota(jnp.int32, (n, n), 0)
    jj = lax.broadcasted_iota(jnp.int32, (n, n), 1)
    causal = _same_block(ii, jj, t) & (jj <= ii)
    diag = ii == jj
    stack = lambda xs, p: jnp.concatenate(xs[p * per_tile:(p + 1) * per_tile], axis=0)

    qkv, zb, gcum, beta = [], [], [], []
    for u, (s, c) in enumerate(units):
        @_when(first_step if c == 0 else False)
        def _():
            _load_conv_state(cbuf, cst_ref[s])

        conv, tail = _causal_conv(cbuf, pb_ref[rows(u), :DN_CONV_CH], cw_ref[...], t)

        @_when(last_step if c == chunks - 1 else False)
        def _():
            cnew_ref[s] = tail

        qkv.append(_silu(conv))
        zb.append(pb_ref[rows(u), DN_CONV_CH:].astype(F32))
        sm = sm_ref[rows(u), :]
        beta.append(_sigmoid(sm))
        gate = -jnp.exp(alog_ref[...]) * _softplus(sm + dtb_ref[...])
        gcum.append(_cumsum_rows(gate, t))
    gcum_t = [_grouped_transpose(g, per_tile) for g in gcum]

    sumsq = _row_sums_of_squares([x[:, i * DN_HEAD:(i + 1) * DN_HEAD] for x in qkv for i in range(2 * DN_HEADS)])
    q, k, v, gcol, bcol, kb, egcol = ([[] for _ in units] for _ in range(7))
    for u in range(len(units)):
        for h in heads:
            lo = h * DN_HEAD
            ss = sumsq[u * 2 * DN_HEADS:(u + 1) * 2 * DN_HEADS]
            q[u].append(qkv[u][:, lo:lo + DN_HEAD] * lax.rsqrt(ss[h] + L2_EPS) * (DN_HEAD ** -0.5))
            k[u].append(qkv[u][:, DN_QK + lo:DN_QK + lo + DN_HEAD] * lax.rsqrt(ss[DN_HEADS + h] + L2_EPS))
            v[u].append(qkv[u][:, 2 * DN_QK + lo:2 * DN_QK + lo + DN_HEAD])
            gcol[u].append(gcum[u][:, SMALL_A + h:SMALL_A + h + 1])
            bcol[u].append(beta[u][:, SMALL_B + h:SMALL_B + h + 1])
            kb[u].append(k[u][h] * bcol[u][h])
            egcol[u].append(jnp.exp(gcol[u][h]))

    tiles = [(u, p) for u in range(len(units)) for p in range(ntile)]
    k16 = [stack(k[u], p).astype(BF16) for u, p in tiles]
    decay = []
    for u, p in tiles:
        h0 = p * per_tile
        seg = stack(gcol[u], p) - gcum_t[u][SMALL_A + h0:SMALL_A + h0 + 1, :n]
        decay.append(jnp.exp(jnp.where(causal, seg, -jnp.inf)))
    lmat = [_dot_nt(stack(kb[u], p).astype(BF16), k16[i]) * jnp.where(diag, 0.0, decay[i])
            for i, (u, p) in enumerate(tiles)]
    qk = [(_dot_nt(stack(q[u], p).astype(BF16), k16[i]) * decay[i]).astype(BF16) for i, (u, p) in enumerate(tiles)]
    inv = [a.astype(BF16) for a in _unit_lower_inverses(lmat, ii, jj, t)]

    split = lambda x: (x[:, :DN_HEAD], x[:, DN_HEAD:])
    o_from, o_own, s_from, s_own, glast = ([[] for _ in units] for _ in range(5))
    aw = [_dot(inv[i], jnp.concatenate(
        [stack([kb[u][h] * egcol[u][h] for h in heads], p), stack([v[u][h] * bcol[u][h] for h in heads], p)],
        axis=1).astype(BF16)) for i, (u, p) in enumerate(tiles)]
    qaw = [_dot(qk[i], aw[i].astype(BF16)) for i in range(len(tiles))]
    pad_rows = -t % (2 * SUBLANES)
    pad = [jnp.zeros((pad_rows, DN_HEAD), F32)] if pad_rows else []
    for u in range(len(units)):
        for h in heads:
            p, j = divmod(h, per_tile)
            i = u * ntile + p
            sel = slice(j * t, (j + 1) * t)
            qkw, qku0 = split(qaw[i][sel])
            o_from[u].append(q[u][h] * egcol[u][h] - qkw)
            o_own[u].append(qku0)
            glast[u].append(gcum[u][t - 1:t, SMALL_A + h:SMALL_A + h + 1])
            kd = (k[u][h] * jnp.exp(glast[u][h] - gcol[u][h])).astype(BF16)
            kdw, kdu0 = split(_dot_tn(kd, aw[i][sel].astype(BF16)))
            s_from[u].append(kdw)
            s_own[u].append(kdu0)

    outs = []
    for u, (s, c) in enumerate(units):
        @_when(first_step if c == 0 else False)
        def _():
            s_scr[...] = s0_ref[s]

        for h in heads:
            state = s_scr[h]
            both = _dot(jnp.concatenate([s_from[u][h], o_from[u][h]] + pad, axis=0).astype(BF16), state.astype(BF16))
            outs.append(both[DN_HEAD:DN_HEAD + t] + o_own[u][h])
            s_scr[h] = jnp.exp(glast[u][h]) * state - both[:DN_HEAD] + s_own[u][h]

        @_when(last_step if c == chunks - 1 else False)
        def _():
            snew_ref[s] = s_scr[...]

    norm_w = nw_ref[...]
    osq = _row_sums_of_squares(outs)
    for u in range(len(units)):
        normed = []
        for h in heads:
            i = u * DN_HEADS + h
            normed.append(outs[i] * lax.rsqrt(osq[i] * (1.0 / DN_HEAD) + RMS_EPS) * norm_w
                          * _silu(zb[u][:, h * DN_HEAD:(h + 1) * DN_HEAD]))
        o_ref[rows(u), :] = jnp.concatenate(normed, axis=1).astype(BF16)


def _gdn(proj_b, small, consts, conv_state, s0, **where):
    t = where["t"]
    scratch = [pltpu.VMEM((SUBLANES, DN_CONV_CH), F32),
               pltpu.VMEM((DN_HEADS, DN_HEAD, DN_HEAD), F32)]
    step = functools.partial(_gdn_step_kernel, t=t, seqs=where["seqs"], chunks=where["chunks"])
    return _scan_call(None, f"gdn_t{t}", [proj_b, small], consts, [conv_state, s0], DN_V, scratch,
                      step_kernel=step, **where)


def _merge_kernel(xp_ref, xs_ref, yap_ref, yas_ref, obp_ref, obs_ref, eg_ref, eb_ref, wg_ref, wa_ref, wb_ref,
                  wo_ref, g_ref, b_ref, x1_ref, x1b_ref, *, prompt_tiles):
    is_prompt = pl.program_id(0) < prompt_tiles
    h = _layer_norm(jnp.where(is_prompt, xp_ref[...], xs_ref[...]), eg_ref[...], eb_ref[...])
    out_a = _dot(jnp.where(is_prompt, yap_ref[...], yas_ref[...]), wa_ref[...])
    out_b = _dot(jnp.where(is_prompt, obp_ref[...], obs_ref[...]), wb_ref[...])
    gt = _dot(h.astype(BF16), wg_ref[...])
    merged = _sigmoid(gt[:, :D_MODEL]) * out_a + _sigmoid(gt[:, D_MODEL:]) * out_b
    mix = _dot(merged.astype(BF16), wo_ref[...])
    x1 = _layer_norm(DEEPNORM_ALPHA * h + mix, g_ref[...], b_ref[...])
    x1_ref[...] = x1
    x1b_ref[...] = x1.astype(BF16)


def _merge(xp, xs, ya_p, ya_s, ob_p, ob_s, consts, tm):
    m = xp.shape[0] + xs.shape[0]
    n_p = xp.shape[0]
    tm = _row_tile(math.gcd(n_p, m - n_p), tm)
    in_prompt, in_sample = _group_maps(n_p // tm)
    row = lambda i: (i, 0)
    const = lambda i: (0, 0)
    return pl.pallas_call(
        functools.partial(_merge_kernel, prompt_tiles=n_p // tm),
        grid=(m // tm,),
        in_specs=[pl.BlockSpec((tm, D_MODEL), in_prompt), pl.BlockSpec((tm, D_MODEL), in_sample),
                  pl.BlockSpec((tm, SSM_INNER), in_prompt), pl.BlockSpec((tm, SSM_INNER), in_sample),
                  pl.BlockSpec((tm, DN_V), in_prompt), pl.BlockSpec((tm, DN_V), in_sample)]
        + [pl.BlockSpec(a.shape, const) for a in consts],
        out_specs=[pl.BlockSpec((tm, D_MODEL), row), pl.BlockSpec((tm, D_MODEL), row)],
        out_shape=[jax.ShapeDtypeStruct((m, D_MODEL), F32), jax.ShapeDtypeStruct((m, D_MODEL), BF16)],
        compiler_params=_params("parallel"),
        name="merge",
    )(xp, xs, ya_p, ya_s, ob_p, ob_s, *consts)


def _take_first_max(rest, index, limit, axis):
    best = jnp.max(rest, axis=axis, keepdims=True)
    first = jnp.min(jnp.where(rest == best, index, limit), axis=axis, keepdims=True)
    hit = index == first
    return hit, jnp.where(hit, -jnp.inf, rest)


def _router_gates_t(x1, xb, rw_hi, rw_lo, bias):
    tm = x1.shape[0]
    x_lo = (x1 - xb.astype(F32)).astype(BF16)
    logits = _dot_nt(rw_hi, xb) + (_dot_nt(rw_hi, x_lo) + _dot_nt(rw_lo, xb))
    scores = _sigmoid(logits)
    row = lax.broadcasted_iota(jnp.int32, (LANES, tm), 0)
    sel = jnp.where(row < N_EXPERTS, scores + bias, -jnp.inf)
    by_group = sel.reshape(LANES // EXPERTS_PER_GROUP, EXPERTS_PER_GROUP, tm)
    sub = lax.broadcasted_iota(jnp.int32, by_group.shape, 1)
    top1 = jnp.max(by_group, axis=1, keepdims=True)
    _, others = _take_first_max(by_group, sub, EXPERTS_PER_GROUP, 1)
    gscore = (top1 + jnp.max(others, axis=1, keepdims=True))[:N_EXPERT_GROUPS]
    gidx = lax.broadcasted_iota(jnp.int32, gscore.shape, 0)
    gkeep = jnp.zeros(gscore.shape, jnp.bool_)
    for _ in range(TOPK_GROUPS):
        hit, gscore = _take_first_max(gscore, gidx, N_EXPERT_GROUPS, 0)
        gkeep = gkeep | hit
    gkeep = jnp.broadcast_to(gkeep, (N_EXPERT_GROUPS, EXPERTS_PER_GROUP, tm)).reshape(N_EXPERTS, tm)
    rest = jnp.where(gkeep, sel[:N_EXPERTS], -jnp.inf)
    eidx = lax.broadcasted_iota(jnp.int32, (N_EXPERTS, tm), 0)
    keep = jnp.zeros((N_EXPERTS, tm), jnp.bool_)
    for _ in range(TOP_K):
        hit, rest = _take_first_max(rest, eidx, N_EXPERTS, 0)
        keep = keep | hit
    picked = jnp.where(keep, scores[:N_EXPERTS], 0.0)
    gates = picked / jnp.sum(picked, axis=0, keepdims=True) * ROUTED_SCALE
    return jnp.concatenate([gates, jnp.zeros((LANES - N_EXPERTS, tm), F32)], axis=0)


def _dense_router_kernel(x1_ref, x1b_ref, pp_ref, ps_ref, sw1_ref, sw3_ref, sw2_ref, pw_ref, pg_ref, rwh_ref,
                         rwl_ref, rb_ref, dense_ref, gates_ref, rank_ref, chosen_t_ref, rank_t_ref, count_ref,
                         *, prompt_tiles):
    xb = x1b_ref[...]
    hid = _silu(_dot(xb, sw1_ref[...])) * _dot(xb, sw3_ref[...])
    shared = _dot(hid.astype(BF16), sw2_ref[...])
    p = jnp.where(pl.program_id(0) < prompt_tiles, pp_ref[...], ps_ref[...])
    ple = _dot(p.astype(BF16), pw_ref[...]) * _sigmoid(_dot(xb, pg_ref[...]))
    x1 = x1_ref[...]
    dense_ref[...] = DEEPNORM_ALPHA * x1 + shared + ple
    ii = lax.broadcasted_iota(jnp.int32, (MOE_BLOCK, MOE_BLOCK), 0)
    jj = lax.broadcasted_iota(jnp.int32, (MOE_BLOCK, MOE_BLOCK), 1)
    earlier = (ii < jj).astype(BF16)
    for blk in range(x1.shape[0] // MOE_BLOCK):
        tok = slice(blk * MOE_BLOCK, (blk + 1) * MOE_BLOCK)
        exp = slice(blk * LANES, (blk + 1) * LANES)
        gates_t = _router_gates_t(x1[tok], xb[tok], rwh_ref[...], rwl_ref[...], rb_ref[...])
        chosen_t = (gates_t > 0.0).astype(BF16)
        rank_t = _dot(chosen_t, earlier)
        chosen_t_ref[exp, :] = chosen_t
        rank_t_ref[exp, :] = rank_t.astype(BF16)
        gates = gates_t.T
        rank = rank_t.T
        gates_ref[tok, :] = gates.astype(BF16)
        rank_ref[tok, :] = rank.astype(BF16)
        count = rank[MOE_BLOCK - 1:, :] + (gates[MOE_BLOCK - 1:, :] > 0.0).astype(F32)
        count_ref[blk] = jnp.broadcast_to(count, (SUBLANES, LANES))


def _dense_router(x1, x1b, p_prompt, p_sample, consts):
    m = x1.shape[0]
    tm = _row_tile(math.gcd(p_prompt.shape[0], p_sample.shape[0]), ROUTER_TM)
    assert tm % MOE_BLOCK == 0
    nblk = m // MOE_BLOCK
    per_step = tm // MOE_BLOCK
    prompt_tiles = p_prompt.shape[0] // tm
    in_prompt, in_sample = _group_maps(prompt_tiles)
    row = lambda i: (i, 0)
    const = lambda i: (0, 0)
    return pl.pallas_call(
        functools.partial(_dense_router_kernel, prompt_tiles=prompt_tiles),
        grid=(m // tm,),
        in_specs=[pl.BlockSpec((tm, D_MODEL), row), pl.BlockSpec((tm, D_MODEL), row),
                  pl.BlockSpec((tm, PLE_DIM), in_prompt), pl.BlockSpec((tm, PLE_DIM), in_sample)]
        + [pl.BlockSpec(a.shape, const) for a in consts],
        out_specs=[pl.BlockSpec((tm, D_MODEL), row), pl.BlockSpec((tm, LANES), row),
                   pl.BlockSpec((tm, LANES), row), pl.BlockSpec((per_step * LANES, MOE_BLOCK), row),
                   pl.BlockSpec((per_step * LANES, MOE_BLOCK), row),
                   pl.BlockSpec((per_step, SUBLANES, LANES), lambda i: (i, 0, 0))],
        out_shape=[jax.ShapeDtypeStruct((m, D_MODEL), F32), jax.ShapeDtypeStruct((m, LANES), BF16),
                   jax.ShapeDtypeStruct((m, LANES), BF16), jax.ShapeDtypeStruct((nblk * LANES, MOE_BLOCK), BF16),
                   jax.ShapeDtypeStruct((nblk * LANES, MOE_BLOCK), BF16),
                   jax.ShapeDtypeStruct((nblk, SUBLANES, LANES), F32)],
        compiler_params=_params("parallel"),
        name="dense_router",
    )(x1, x1b, p_prompt, p_sample, *consts)


def _moe_tables(counts, tm):
    nblk = counts.shape[0]
    nchunk_max = MOE_CAP // MOE_SEG
    padded = (counts + MOE_SEG - 1) // MOE_SEG * MOE_SEG
    start = jnp.cumsum(padded, axis=1) - padded
    used = jnp.sum(padded, axis=1)
    rows_e = jnp.sum(padded, axis=0)
    rows_e_t = (rows_e + tm - 1) // tm * tm
    base_e = jnp.cumsum(rows_e_t) - rows_e_t
    seg_row = base_e[None, :] + jnp.cumsum(padded, axis=0) - padded
    chunk0 = jnp.arange(nchunk_max, dtype=jnp.int32) * MOE_SEG
    owner = jnp.sum((start + padded)[:, None, :] <= chunk0[None, :, None], axis=-1)
    onehot = owner[..., None] == jnp.arange(N_EXPERTS, dtype=jnp.int32)
    dst = chunk0[None, :] + jnp.sum(jnp.where(onehot, (seg_row - start)[:, None, :], 0), axis=-1)
    n_tiles = _moe_max_tiles(nblk, tm)
    live = chunk0[None, :] < used[:, None]
    parity = (jnp.arange(nblk, dtype=jnp.int32) % 2)[:, None]
    spare = n_tiles * tm + parity * MOE_CAP + chunk0[None, :]
    src = jnp.where(live, dst, 0)
    dst = jnp.where(live, dst, spare)
    tile_row = jnp.arange(n_tiles, dtype=jnp.int32) * tm
    tiles_used = jnp.sum(rows_e_t) // tm
    tile_row = jnp.minimum(tile_row, (tiles_used - 1) * tm)
    tile_expert = jnp.minimum(jnp.sum((base_e + rows_e_t)[None, :] <= tile_row[:, None], axis=-1), N_EXPERTS - 1)
    in_use = rows_e > 0
    experts = jnp.arange(N_EXPERTS, dtype=jnp.int32)
    later = jnp.where(in_use[None, :] & (experts[None, :] > experts[:, None]), experts[None, :], N_EXPERTS)
    next_in_use = jnp.min(later, axis=1)
    next_in_use = jnp.where(next_in_use == N_EXPERTS, -1, next_in_use)
    order = jnp.sum(in_use[None, :] & (experts[None, :] < experts[:, None]), axis=1)
    of_tile = tile_expert[:, None] == experts[None, :]
    tile_slot = jnp.sum(jnp.where(of_tile, (order % 2)[None, :], 0), axis=1)
    tile_next = jnp.sum(jnp.where(of_tile, next_in_use[None, :], 0), axis=1)
    bounds = jnp.zeros((nblk, SUBLANES, LANES), F32)
    bounds = bounds.at[:, 0, :N_EXPERTS].set(start.astype(F32)).at[:, 1, :N_EXPERTS].set(padded.astype(F32))
    i32 = lambda a: a.astype(jnp.int32)
    return dict(dst=i32(dst.reshape(-1)), src=i32(src.reshape(-1)), nchunk=i32(used // MOE_SEG),
                pad_row=i32(base_e + rows_e),
                pad_n=i32((rows_e_t - rows_e) // MOE_SEG), tile_expert=i32(tile_expert),
                tile_slot=i32(tile_slot), tile_next=i32(tile_next),
                tiles_used=i32(tiles_used.reshape(1)), bounds=bounds)


def _moe_max_tiles(nblk, tm):
    return (nblk * (MOE_BLOCK * TOP_K + N_EXPERTS * (MOE_SEG - 1)) + N_EXPERTS * (tm - MOE_SEG) + tm - 1) // tm


def _slot_bounds(bounds):
    return bounds[0:1, :], bounds[1:2, :]


def _moe_gather_kernel(dst_ref, nchunk_ref, pad_row_ref, pad_n_ref,
                       x_ref, chosen_t_ref, rank_t_ref, bounds_ref, rows_ref, buf, zbuf, sem, zsem):
    j = pl.program_id(0)
    nblk = pl.num_programs(0)
    slot = j % 2
    nchunk_max = MOE_CAP // MOE_SEG

    def chunk_copy(blk, q, sl):
        src = buf.at[sl, pl.ds(pl.multiple_of(q * MOE_SEG, MOE_SEG), MOE_SEG), :]
        row = pl.multiple_of(dst_ref[blk * nchunk_max + q], MOE_SEG)
        return pltpu.make_async_copy(src, rows_ref.at[pl.ds(row, MOE_SEG), :], sem.at[sl])

    def pad_copy(e, i):
        row = pl.multiple_of(pad_row_ref[e] + i * MOE_SEG, MOE_SEG)
        return pltpu.make_async_copy(zbuf, rows_ref.at[pl.ds(row, MOE_SEG), :], zsem)

    def sub_chunks(blk, sub, sl, action):
        for q in range(sub * (MOE_SUB // MOE_SEG), (sub + 1) * (MOE_SUB // MOE_SEG)):
            action(chunk_copy(blk, q, sl))

    def sub_used(blk, sub):
        return True if sub < MOE_SUB_ALWAYS else sub * MOE_SUB < nchunk_ref[blk] * MOE_SEG

    def for_pads(action):
        def per_expert(e, carry):
            def body(i, c):
                action(pad_copy(e, i))
                return c
            return lax.fori_loop(0, pad_n_ref[e], body, carry)
        lax.fori_loop(0, N_EXPERTS, per_expert, 0)

    @pl.when(j == 0)
    def _():
        zbuf[...] = jnp.zeros_like(zbuf)
        for_pads(lambda c: c.start())

    start, length = _slot_bounds(bounds_ref[0])
    x = x_ref[...]
    chosen_rank_t = jnp.concatenate([chosen_t_ref[...], rank_t_ref[...]], axis=1)
    before = jnp.maximum(j - 2, 0)
    subs = range(MOE_CAP // MOE_SUB)

    @pl.when(j >= 2)
    def _():
        for sub in subs:
            @_when(sub_used(before, sub))
            def _():
                sub_chunks(before, sub, slot, lambda c: c.wait())

    def owners(sub):
        s = (sub * MOE_SUB + lax.broadcasted_iota(jnp.int32, (MOE_SUB, LANES), 0)).astype(F32)
        owner = (s >= start) & (s < start + length)
        within = s[:, 0:1] - jnp.sum(jnp.where(owner, start, 0.0), axis=1, keepdims=True)
        return owner.astype(BF16), within

    def picks(owner_within):
        owner, within = owner_within
        hit = _dot(owner, chosen_rank_t)
        return ((hit[:, :MOE_BLOCK] > 0.5) & (hit[:, MOE_BLOCK:] == within)).astype(BF16)

    def emit(sub, pick):
        buf[slot, sub * MOE_SUB:(sub + 1) * MOE_SUB, :] = _dot(pick, x).astype(BF16)
        sub_chunks(j, sub, slot, lambda c: c.start())

    staged = [picks(ow) for ow in [owners(sub) for sub in subs[:MOE_SUB_ALWAYS]]]
    for sub, pick in enumerate(staged):
        emit(sub, pick)
    for sub in subs[MOE_SUB_ALWAYS:]:
        @pl.when(sub_used(j, sub))
        def _():
            emit(sub, picks(owners(sub)))

    @pl.when(j == nblk - 1)
    def _():
        prev = jnp.maximum(j - 1, 0)
        for sub in subs:
            @_when(sub_used(j, sub))
            def _():
                sub_chunks(j, sub, slot, lambda c: c.wait())

            @_when((j >= 1) & sub_used(prev, sub))
            def _():
                sub_chunks(prev, sub, 1 - slot, lambda c: c.wait())

        for_pads(lambda c: c.wait())


def _moe_gather(x1b, chosen_t, rank_t, tables, tm):
    m = x1b.shape[0]
    nblk = m // MOE_BLOCK
    rows = _moe_max_tiles(nblk, tm) * tm + 2 * MOE_CAP
    blk = lambda j, *_: (j, 0)
    grid_spec = pltpu.PrefetchScalarGridSpec(
        num_scalar_prefetch=4,
        grid=(nblk,),
        in_specs=[pl.BlockSpec((MOE_BLOCK, D_MODEL), blk), pl.BlockSpec((LANES, MOE_BLOCK), blk),
                  pl.BlockSpec((LANES, MOE_BLOCK), blk),
                  pl.BlockSpec((1, SUBLANES, LANES), lambda j, *_: (j, 0, 0))],
        out_specs=pl.BlockSpec(memory_space=pl.ANY),
        scratch_shapes=[pltpu.VMEM((2, MOE_CAP, D_MODEL), BF16), pltpu.VMEM((MOE_SEG, D_MODEL), BF16),
                        pltpu.SemaphoreType.DMA((2,)), pltpu.SemaphoreType.DMA(())],
    )
    return pl.pallas_call(
        _moe_gather_kernel,
        grid_spec=grid_spec,
        out_shape=jax.ShapeDtypeStruct((rows, D_MODEL), BF16),
        compiler_params=_params("arbitrary"),
        name="moe_gather",
    )(tables["dst"], tables["nchunk"], tables["pad_row"], tables["pad_n"], x1b, chosen_t, rank_t, tables["bounds"])


def _moe_ffn_kernel(tile_expert_ref, tile_slot_ref, tile_next_ref, x_ref, w1_hbm, w3_hbm, w2_hbm, y_ref,
                    w1_buf, w3_buf, w2_buf, w13_scr, w2_scr, sem):
    i = pl.program_id(0)
    expert = tile_expert_ref[i]
    slot = tile_slot_ref[i]

    def weight_copies(e, s):
        pairs = ((w1_hbm, w1_buf), (w3_hbm, w3_buf), (w2_hbm, w2_buf))
        return [pltpu.make_async_copy(w.at[e], buf.at[s], sem.at[s, n]) for n, (w, buf) in enumerate(pairs)]

    @pl.when(i == 0)
    def _():
        for copy in weight_copies(expert, slot):
            copy.start()

    @pl.when((i == 0) | (expert != tile_expert_ref[jnp.maximum(i - 1, 0)]))
    def _():
        for copy in weight_copies(expert, slot):
            copy.wait()

        @pl.when(tile_next_ref[i] >= 0)
        def _():
            for copy in weight_copies(tile_next_ref[i], 1 - slot):
                copy.start()

        w13_scr[:, :EXPERT_FF] = w1_buf[slot].astype(BF16)
        w13_scr[:, EXPERT_FF:] = w3_buf[slot].astype(BF16)
        w2_scr[...] = w2_buf[slot].astype(BF16)

    up = _dot(x_ref[...], w13_scr[...])
    hid = _silu(up[:, :EXPERT_FF]) * up[:, EXPERT_FF:]
    y_ref[...] = _dot(hid.astype(BF16), w2_scr[...]).astype(BF16)


def _moe_ffn(rows, w1, w3, w2, tables, tm):
    tile = lambda i, *_: (i, 0)
    grid_spec = pltpu.PrefetchScalarGridSpec(
        num_scalar_prefetch=3,
        grid=(tables["tiles_used"][0],),
        in_specs=[pl.BlockSpec((tm, D_MODEL), tile)] + [pl.BlockSpec(memory_space=pl.ANY)] * 3,
        out_specs=pl.BlockSpec((tm, D_MODEL), tile),
        scratch_shapes=[pltpu.VMEM((2, D_MODEL, EXPERT_FF), F32), pltpu.VMEM((2, D_MODEL, EXPERT_FF), F32),
                        pltpu.VMEM((2, EXPERT_FF, D_MODEL), F32),
                        pltpu.VMEM((D_MODEL, 2 * EXPERT_FF), BF16), pltpu.VMEM((EXPERT_FF, D_MODEL), BF16),
                        pltpu.SemaphoreType.DMA((2, 3))],
    )
    return pl.pallas_call(
        _moe_ffn_kernel,
        grid_spec=grid_spec,
        out_shape=jax.ShapeDtypeStruct(rows.shape, BF16),
        compiler_params=_params("arbitrary"),
        name="moe_ffn",
    )(tables["tile_expert"], tables["tile_slot"], tables["tile_next"], rows, w1, w3, w2)


def _moe_combine_kernel(dst_ref, nchunk_ref, dense_ref, gates_ref, rank_ref, bounds_ref, g_ref, b_ref, rows_ref,
                        op_ref, os_ref, acc, buf, sem, *, prompt_tiles):
    j = pl.program_id(0)
    nblk = pl.num_programs(0)
    slot = j % 2
    nchunk_max = MOE_CAP // MOE_SEG

    def chunk_copy(blk, q, sl):
        row = pl.multiple_of(dst_ref[blk * nchunk_max + q], MOE_SEG)
        dst = buf.at[sl, pl.ds(pl.multiple_of(q * MOE_SEG, MOE_SEG), MOE_SEG), :]
        return pltpu.make_async_copy(rows_ref.at[pl.ds(row, MOE_SEG), :], dst, sem.at[sl])

    subs = range(MOE_CAP // MOE_SUB)
    always, rest = subs[:MOE_SUB_ALWAYS], subs[MOE_SUB_ALWAYS:]

    def sub_used(blk, sub):
        return True if sub < MOE_SUB_ALWAYS else sub * MOE_SUB < nchunk_ref[blk] * MOE_SEG

    def group(blk, sub, sl, action):
        for q in range(sub * (MOE_SUB // MOE_SEG), (sub + 1) * (MOE_SUB // MOE_SEG)):
            action(chunk_copy(blk, q, sl))

    @pl.when(j == 0)
    def _():
        buf[...] = jnp.zeros_like(buf)
        for sub in subs:
            @_when(sub_used(0, sub))
            def _():
                group(0, sub, 0, lambda c: c.start())

    for sub in subs:
        @_when(sub_used(j, sub))
        def _():
            group(j, sub, slot, lambda c: c.wait())

    bounds_t = jnp.concatenate([bounds_ref[0], jnp.zeros((LANES - SUBLANES, LANES), F32)], axis=0).T
    start, length = bounds_t[:, 0:1], bounds_t[:, 1:2]
    gates_rank = jnp.concatenate([gates_ref[...], rank_ref[...]], axis=0)

    def owners(sub):
        s = (sub * MOE_SUB + lax.broadcasted_iota(jnp.int32, (LANES, MOE_SUB), 1)).astype(F32)
        owner = (s >= start) & (s < start + length)
        within = s[0:1, :] - jnp.sum(jnp.where(owner, start, 0.0), axis=0, keepdims=True)
        return owner.astype(BF16), within

    def weights(owner_within):
        owner, within = owner_within
        hit = _dot(gates_rank, owner)
        return jnp.where(hit[MOE_BLOCK:] == within, hit[:MOE_BLOCK], 0.0).astype(BF16)

    def apply(sub, weight):
        return _dot(weight, buf[slot, sub * MOE_SUB:(sub + 1) * MOE_SUB, :])

    nxt = jnp.minimum(j + 1, nblk - 1)
    staged = [owners(sub) for sub in always]
    for sub in always:
        group(nxt, sub, 1 - slot, lambda c: c.start())
    staged = [weights(ow) for ow in staged]
    total = dense_ref[...]
    for sub, weight in zip(always, staged):
        total = total + apply(sub, weight)
    acc[...] = total
    for sub in rest:
        @pl.when((j + 1 < nblk) & sub_used(nxt, sub))
        def _():
            group(nxt, sub, 1 - slot, lambda c: c.start())

        @pl.when(sub_used(j, sub))
        def _():
            acc[...] += apply(sub, weights(owners(sub)))

    @pl.when(j == nblk - 1)
    def _():
        for sub in always:
            group(nxt, sub, 1 - slot, lambda c: c.wait())

    @pl.when(j < prompt_tiles)
    def _():
        op_ref[...] = _layer_norm(acc[...], g_ref[...], b_ref[...])

    @pl.when(j >= prompt_tiles)
    def _():
        os_ref[...] = _layer_norm(acc[...], g_ref[...], b_ref[...])


def _moe_combine(rows, dense, gates, rank, tables, g, b, n_prompt):
    m = dense.shape[0]
    nblk = m // MOE_BLOCK
    prompt_tiles = n_prompt // MOE_BLOCK
    in_prompt, in_sample = _group_maps(prompt_tiles)
    blk = lambda j, *_: (j, 0)
    const = lambda j, *_: (0, 0)
    grid_spec = pltpu.PrefetchScalarGridSpec(
        num_scalar_prefetch=2,
        grid=(nblk,),
        in_specs=[pl.BlockSpec((MOE_BLOCK, D_MODEL), blk), pl.BlockSpec((MOE_BLOCK, LANES), blk),
                  pl.BlockSpec((MOE_BLOCK, LANES), blk),
                  pl.BlockSpec((1, SUBLANES, LANES), lambda j, *_: (j, 0, 0)),
                  pl.BlockSpec((1, D_MODEL), const), pl.BlockSpec((1, D_MODEL), const),
                  pl.BlockSpec(memory_space=pl.ANY)],
        out_specs=[pl.BlockSpec((MOE_BLOCK, D_MODEL), in_prompt), pl.BlockSpec((MOE_BLOCK, D_MODEL), in_sample)],
        scratch_shapes=[pltpu.VMEM((MOE_BLOCK, D_MODEL), F32), pltpu.VMEM((2, MOE_CAP, D_MODEL), BF16),
                        pltpu.SemaphoreType.DMA((2,))],
    )
    return pl.pallas_call(
        functools.partial(_moe_combine_kernel, prompt_tiles=prompt_tiles),
        grid_spec=grid_spec,
        out_shape=[jax.ShapeDtypeStruct((n_prompt, D_MODEL), F32),
                   jax.ShapeDtypeStruct((m - n_prompt, D_MODEL), F32)],
        compiler_params=_params("arbitrary"),
        name="moe_combine",
    )(tables["src"], tables["nchunk"], dense, gates, rank, tables["bounds"], g, b, rows)


def _row(v):
    return v.reshape(1, -1).astype(F32)


def _scan_layout(batch, seq, row0):
    t = math.gcd(seq, CHUNK)
    nchunk = seq // t
    chunks = SCAN_CHUNKS if nchunk % SCAN_CHUNKS == 0 else 1
    seqs = SCAN_SEQS if (nchunk == 1 and batch % SCAN_SEQS == 0 and row0 % (SCAN_SEQS * t) == 0) else 1
    return dict(batch=batch, seq=seq, row0=row0, t=t, seqs=seqs, chunks=chunks)


def _lanes_at(v, start):
    return jnp.zeros((1, LANES), F32).at[0, start:start + v.shape[0]].set(v.astype(F32))


def kernel(x_prompt, x_sample, p_prompt, p_sample, state_ssm_conv, state_ssm, state_dn_conv, state_dn, emb_ln_g, emb_ln_b, w_in, conv_a_w, conv_a_b, ssm_dt_bias, ssm_a_log, ssm_d, ssm_norm_w, w_a, conv_b_w, dn_dt_bias, dn_a_log, dn_norm_w, w_b, w_o, ln1_g, ln1_b, router_w, router_bias, exp_w1, exp_w3, exp_w2, sh_w1, sh_w3, sh_w2, ple_w, ple_gate_w, ln2_g, ln2_b):
    bp, lp, _ = x_prompt.shape
    bs, ls, _ = x_sample.shape
    n_p = bp * lp
    n_s = bs * ls
    xp = x_prompt.reshape(n_p, D_MODEL)
    xs = x_sample.reshape(n_s, D_MODEL)

    w = w_in[0]
    o_z, o_xbc, o_dt = 0, SSM_INNER, SSM_INNER + SSM_CONV_CH
    o_qkv = o_dt + SSM_HEADS
    o_a = o_qkv + DN_CONV_CH
    o_b = o_a + DN_HEADS
    o_zb = o_b + DN_HEADS
    o_ga = o_zb + DN_V
    w_pa = w[:, o_z:o_dt].astype(BF16)
    w_pb = jnp.concatenate([w[:, o_qkv:o_a], w[:, o_zb:o_ga]], axis=1).astype(BF16)
    w_pg = w[:, o_ga:].astype(BF16)
    w_ps = jnp.zeros((D_MODEL, LANES), F32)
    w_ps = w_ps.at[:, SMALL_DT:SMALL_DT + SSM_HEADS].set(w[:, o_dt:o_qkv])
    w_ps = w_ps.at[:, SMALL_A:SMALL_A + 2 * DN_HEADS].set(w[:, o_a:o_zb]).astype(BF16)

    eg, eb = _row(emb_ln_g), _row(emb_ln_b)
    proj_a, small = _ln_matmul(xp, xs, eg, eb, w_pa, PROJ_TM, PROJ_TN_A, w_ps)
    proj_b = _ln_matmul(xp, xs, eg, eb, w_pb, PROJ_TM, PROJ_TN_B)

    ssd_consts = [conv_a_w[0], _row(conv_a_b[0]), _lanes_at(ssm_dt_bias[0], SMALL_DT),
                  _lanes_at(ssm_a_log[0], SMALL_DT), _row(jnp.repeat(ssm_d[0], SSM_HEAD_DIM)),
                  _row(ssm_norm_w[0])]
    gdn_consts = [conv_b_w[0], _lanes_at(dn_a_log[0], SMALL_A), _lanes_at(dn_dt_bias[0], SMALL_A),
                  _row(dn_norm_w[0])]
    zeros = lambda *s: jnp.zeros(s, F32)
    prompt = _scan_layout(bp, lp, 0)
    sample = _scan_layout(bs, ls, n_p)

    ya_p, pa_conv, pa_ssm = _ssd(proj_a, small, ssd_consts, zeros(bp, CONV_WIDTH - 1, SSM_CONV_CH),
                                 zeros(bp, SSM_HEADS, SSM_HEAD_DIM, SSM_STATE), **prompt)
    ya_s, sa_conv, sa_ssm = _ssd(proj_a, small, ssd_consts, state_ssm_conv[0], state_ssm[0], **sample)
    ob_p, pb_conv, pb_dn = _gdn(proj_b, small, gdn_consts, zeros(bp, CONV_WIDTH - 1, DN_CONV_CH),
                                zeros(bp, DN_HEADS, DN_HEAD, DN_HEAD), **prompt)
    ob_s, sb_conv, sb_dn = _gdn(proj_b, small, gdn_consts, state_dn_conv[0], state_dn[0], **sample)

    x1, x1b = _merge(
        xp, xs, ya_p, ya_s, ob_p, ob_s,
        [eg, eb, w_pg, w_a[0].astype(BF16), w_b[0].astype(BF16), w_o[0].astype(BF16), _row(ln1_g[0]), _row(ln1_b[0])],
        MERGE_TM)

    router_w_t = jnp.zeros((LANES, D_MODEL), F32).at[:N_EXPERTS].set(router_w[0].T)
    router_w_hi = router_w_t.astype(BF16)
    router_w_lo = (router_w_t - router_w_hi.astype(F32)).astype(BF16)
    router_b = jnp.broadcast_to(_lanes_at(router_bias[0], 0).reshape(LANES, 1), (LANES, MOE_BLOCK))
    dense, gates, rank, chosen_t, rank_t, counts = _dense_router(
        x1, x1b, p_prompt[0].reshape(n_p, PLE_DIM), p_sample[0].reshape(n_s, PLE_DIM),
        [sh_w1[0].astype(BF16), sh_w3[0].astype(BF16), sh_w2[0].astype(BF16), ple_w[0].astype(BF16),
         ple_gate_w[0].astype(BF16), router_w_hi, router_w_lo, router_b])

    tables = _moe_tables(counts[:, 0, :N_EXPERTS].astype(jnp.int32), MOE_TM)
    sorted_rows = _moe_gather(x1b, chosen_t, rank_t, tables, MOE_TM)
    expert_out = _moe_ffn(sorted_rows, exp_w1[0], exp_w3[0], exp_w2[0], tables, MOE_TM)
    out_p, out_s = _moe_combine(expert_out, dense, gates, rank, tables, _row(ln2_g[0]), _row(ln2_b[0]), n_p)

    return (out_p.reshape(bp, lp, D_MODEL), out_s.reshape(bs, ls, D_MODEL),
            pa_conv[None], pa_ssm[None], pb_conv[None], pb_dn[None],
            sa_conv[None], sa_ssm[None], sb_conv[None], sb_dn[None])
```

```python
import functools
import math

import jax
import jax.numpy as jnp
from jax import lax
from jax.experimental import pallas as pl
from jax.experimental.pallas import tpu as pltpu

F32 = jnp.float32
BF16 = jnp.bfloat16
HIGHEST = lax.Precision.HIGHEST

D_MODEL = 1024
SSM_INNER = 2048
SSM_HEAD_DIM = 64
SSM_HEADS = 32
SSM_GROUPS = 2
SSM_HEADS_PER_GROUP = 16
SSM_STATE = 128
SSM_CONV_CH = 2560
DN_HEADS = 8
DN_HEAD = 128
DN_QK = 1024
DN_V = 1024
DN_CONV_CH = 3072
CONV_WIDTH = 4
CHUNK = 64
N_EXPERTS = 64
TOP_K = 8
N_EXPERT_GROUPS = 8
EXPERTS_PER_GROUP = 8
TOPK_GROUPS = 4
EXPERT_FF = 256
SHARED_FF = 256
ROUTED_SCALE = 2.5
PLE_DIM = 256
LN_EPS = 1e-5
RMS_EPS = 1e-6
L2_EPS = 1e-6
DEEPNORM_ALPHA = 2.0 ** 0.25

LANES = 128
SUBLANES = 8
VMEM_LIMIT = 56 * 1024 * 1024
SMALL_DT = 0
SMALL_A = 32
SMALL_B = 40
PROJ_TM = 1024
PROJ_TN_A = 2304
PROJ_TN_B = 2048
MERGE_TM = 512
SCAN_CHUNKS = 8
SCAN_SEQS = 8
MOE_BLOCK = 256
MOE_SEG = 16
MOE_SUB = 512
MOE_CAP = -(-(MOE_BLOCK * TOP_K + N_EXPERTS * (MOE_SEG - 1)) // MOE_SUB) * MOE_SUB
MOE_TM = 512
MOE_ROW_SLOTS = 3
ROUTER_TM = 512
MOE_SUB_ALWAYS = -(-(MOE_BLOCK * TOP_K + N_EXPERTS * MOE_SEG // 2) // MOE_SUB)


def _sigmoid(x):
    return 1.0 / (1.0 + jnp.exp(-x))


def _silu(x):
    return x * _sigmoid(x)


def _softplus(x):
    return jnp.maximum(x, 0.0) + jnp.log(1.0 + jnp.exp(-jnp.abs(x)))


def _layer_norm(x, g, b):
    mu = jnp.mean(x, axis=-1, keepdims=True)
    xc = x - mu
    var = jnp.mean(xc * xc, axis=-1, keepdims=True)
    return xc * lax.rsqrt(var + LN_EPS) * g + b


def _dot(a, b):
    return jnp.dot(a, b, preferred_element_type=F32)


def _dot_nt(a, b):
    return lax.dot_general(a, b, (((1,), (1,)), ((), ())), preferred_element_type=F32)


def _dot_tn(a, b):
    return lax.dot_general(a, b, (((0,), (0,)), ((), ())), preferred_element_type=F32)


def _dot_f32(a, b):
    return jnp.dot(a, b, precision=HIGHEST, preferred_element_type=F32)


def _params(*sem):
    return pltpu.CompilerParams(dimension_semantics=sem, vmem_limit_bytes=VMEM_LIMIT)


def _when(cond):
    if cond is True:
        return lambda fn: fn()
    if cond is False:
        return lambda fn: None
    return pl.when(cond)


def _row_tile(m, preferred):
    return max(d for d in range(SUBLANES, min(m, preferred) + 1, SUBLANES) if m % d == 0)


def _group_maps(prompt_tiles):
    in_prompt = lambda i, *_: (jnp.minimum(i, prompt_tiles - 1), 0)
    in_sample = lambda i, *_: (jnp.maximum(i - prompt_tiles, 0), 0)
    return in_prompt, in_sample


def _ln_matmul_kernel(xp_ref, xs_ref, g_ref, b_ref, w_ref, *rest, prompt_tiles, narrow):
    if narrow:
        wn_ref, o_ref, on_ref, h_scr = rest
    else:
        o_ref, h_scr = rest

    @pl.when(pl.program_id(1) == 0)
    def _():
        x = jnp.where(pl.program_id(0) < prompt_tiles, xp_ref[...], xs_ref[...])
        h_scr[...] = _layer_norm(x, g_ref[...], b_ref[...]).astype(BF16)
        if narrow:
            on_ref[...] = _dot(h_scr[...], wn_ref[...])

    o_ref[...] = _dot(h_scr[...], w_ref[...]).astype(o_ref.dtype)


def _ln_matmul(xp, xs, g, b, w, tm, tn, w_narrow=None):
    k = xp.shape[1]
    m = xp.shape[0] + xs.shape[0]
    n = w.shape[1]
    tm = _row_tile(math.gcd(xp.shape[0], xs.shape[0]), tm)
    prompt_tiles = xp.shape[0] // tm
    in_prompt, in_sample = _group_maps(prompt_tiles)
    narrow = w_narrow is not None
    const = lambda i, j: (0, 0)
    row = lambda i, j: (i, 0)
    in_specs = [pl.BlockSpec((tm, k), in_prompt), pl.BlockSpec((tm, k), in_sample),
                pl.BlockSpec((1, k), const), pl.BlockSpec((1, k), const), pl.BlockSpec((k, tn), lambda i, j: (0, j))]
    out_specs = [pl.BlockSpec((tm, tn), lambda i, j: (i, j))]
    out_shape = [jax.ShapeDtypeStruct((m, n), BF16)]
    args = [xp, xs, g, b, w]
    if narrow:
        in_specs.append(pl.BlockSpec((k, LANES), const))
        out_specs.append(pl.BlockSpec((tm, LANES), row))
        out_shape.append(jax.ShapeDtypeStruct((m, LANES), F32))
        args.append(w_narrow)
    out = pl.pallas_call(
        functools.partial(_ln_matmul_kernel, prompt_tiles=prompt_tiles, narrow=narrow),
        grid=(m // tm, n // tn),
        in_specs=in_specs,
        out_specs=out_specs,
        out_shape=out_shape,
        scratch_shapes=[pltpu.VMEM((tm, k), BF16)],
        compiler_params=_params("parallel", "arbitrary"),
        name="ln_matmul",
    )(*args)
    return out if narrow else out[0]


def _load_conv_state(cbuf, state):
    taps = CONV_WIDTH - 1
    cbuf[0:SUBLANES - taps, :] = jnp.zeros((SUBLANES - taps, cbuf.shape[1]), F32)
    cbuf[SUBLANES - taps:SUBLANES, :] = state


def _causal_conv(cbuf, x16, cw, t):
    ch = x16.shape[1]
    prev = cbuf[...]
    taps = CONV_WIDTH - 1
    if t <= SUBLANES:
        x = x16.astype(F32)
        ext = jnp.concatenate([prev, x], axis=0)
        lo = SUBLANES - taps
        y = ext[lo:lo + t] * cw[0:1]
        y = y + ext[lo + 1:lo + 1 + t] * cw[1:2]
        y = y + ext[lo + 2:lo + 2 + t] * cw[2:3]
        y = y + x * cw[3:4]
        cbuf[...] = ext[t:t + SUBLANES]
        return y, ext[SUBLANES + t - taps:SUBLANES + t]
    hi = prev.astype(BF16).astype(F32)
    mid = (prev - hi).astype(BF16).astype(F32)
    low = ((prev - hi) - mid).astype(BF16).astype(F32)
    x = x16.astype(F32)
    pack = 2 * SUBLANES
    nhead = 4 * SUBLANES if t % pack == 0 else 5 * SUBLANES
    head = jnp.concatenate([hi, mid, low, jnp.zeros((nhead - 3 * SUBLANES, ch), F32)], axis=0)
    if t % pack == 0:
        ext = jnp.concatenate([head.astype(BF16), x16], axis=0)
    else:
        ext = jnp.concatenate([head, x], axis=0).astype(BF16)
    out_row = lax.broadcasted_iota(jnp.int32, (taps * t, nhead + t), 0)
    col = lax.broadcasted_iota(jnp.int32, (taps * t, nhead + t), 1)
    log_t = t.bit_length() - 1
    shift = lax.shift_right_logical(out_row, log_t) + 1
    src = (out_row & (t - 1)) - shift
    in_x = (src >= 0) & (col == nhead + src)
    in_prev = (src < 0) & (col < 3 * SUBLANES) & ((col & (SUBLANES - 1)) == SUBLANES + src)
    shifted = _dot((in_x | in_prev).astype(BF16), ext)
    y = shifted[2 * t:3 * t] * cw[0:1]
    y = y + shifted[t:2 * t] * cw[1:2]
    y = y + shifted[0:t] * cw[2:3]
    y = y + x * cw[3:4]
    cbuf[...] = x[t - SUBLANES:t]
    return y, x[t - taps:t]


def _lane_expand(v, h0, count, width):
    t = v.shape[0]
    n = count * width
    out = jnp.broadcast_to(v[:, h0:h0 + 1], (t, n))
    if count > 1:
        lane = lax.broadcasted_iota(jnp.int32, (t, n), 1)
        for i in range(1, count):
            out = jnp.where(lane >= i * width, jnp.broadcast_to(v[:, h0 + i:h0 + i + 1], (t, n)), out)
    return out


def _grouped_transpose(v, per_tile):
    t = v.shape[0]
    blocks = [v if r == 0 else pltpu.roll(v, LANES - r, axis=1) for r in range(per_tile)]
    if per_tile * t < LANES:
        blocks.append(jnp.zeros((LANES - per_tile * t, LANES), F32))
    return jnp.concatenate(blocks, axis=0).T


def _same_block(i, j, size):
    shift = size.bit_length() - 1
    return lax.shift_right_logical(i, shift) == lax.shift_right_logical(j, shift)


def _unit_lower_inverses(lmats, ii, jj, t):
    mm = lambda a, b: _dot(a.astype(BF16), b.astype(BF16))
    base = min(16, t)
    in_base = _same_block(ii, jj, base)
    eye = (ii == jj).astype(F32)
    power = [jnp.where(in_base, l, 0.0) for l in lmats]
    inv = [eye - p for p in power]
    span = 2
    while span < base:
        power = [mm(p, p) for p in power]
        inv = [a + mm(a, p) for a, p in zip(inv, power)]
        span *= 2
    size = base
    while size < t:
        link = _same_block(ii, jj, 2 * size) & jnp.logical_not(_same_block(ii, jj, size))
        cross = [mm(a, jnp.where(link, l, 0.0)) for a, l in zip(inv, lmats)]
        inv = [a - mm(c, a) for a, c in zip(inv, cross)]
        size *= 2
    return inv


def _row_sums_of_squares(blocks):
    t = blocks[0].shape[0]
    sq = jnp.concatenate([b * b for b in blocks], axis=0)
    hi = sq.astype(BF16)
    lo = (sq - hi.astype(F32)).astype(BF16)
    ones = jnp.ones((LANES, LANES), BF16)
    sums = _dot(hi, ones) + _dot(lo, ones)
    return [sums[i * t:(i + 1) * t] for i in range(len(blocks))]


def _cumsum_rows(v, t):
    ii = lax.broadcasted_iota(jnp.int32, (t, t), 0)
    jj = lax.broadcasted_iota(jnp.int32, (t, t), 1)
    return _dot_f32((jj <= ii).astype(F32), v)


def _ssd_kernel(pa_ref, sm_ref, cw_ref, cb_ref, dtb_ref, alog_ref, dskip_ref, nw_ref, cst_ref, h0_ref,
                y_ref, cnew_ref, hnew_ref, cbuf, s_scr, *, t, first, last):
    per_tile = LANES // t
    width = per_tile * SSM_HEAD_DIM

    @_when(first)
    def _():
        _load_conv_state(cbuf, cst_ref[0])
        s_scr[...] = h0_ref[0]

    z = pa_ref[:, :SSM_INNER].astype(F32)
    conv, tail = _causal_conv(cbuf, pa_ref[:, SSM_INNER:], cw_ref[...], t)

    @_when(last)
    def _():
        cnew_ref[0] = tail

    xbc = _silu(conv + cb_ref[...])
    xs = xbc[:, :SSM_INNER]
    bm = xbc[:, SSM_INNER:SSM_INNER + SSM_GROUPS * SSM_STATE]
    cm = xbc[:, SSM_INNER + SSM_GROUPS * SSM_STATE:]

    dt = _softplus(sm_ref[...] + dtb_ref[...])
    da = dt * (-jnp.exp(alog_ref[...]))
    cum = _cumsum_rows(da, t)
    cum_t = _grouped_transpose(cum, per_tile)
    dt_t = _grouped_transpose(dt, per_tile)

    row = lax.broadcasted_iota(jnp.int32, (t, LANES), 0)
    lane = lax.broadcasted_iota(jnp.int32, (t, LANES), 1)
    causal = (lane % t) <= row
    brow = lax.broadcasted_iota(jnp.int32, (LANES, width), 0)
    bcol = lax.broadcasted_iota(jnp.int32, (LANES, width), 1)
    blockdiag = (brow // t) == (bcol // SSM_HEAD_DIM)

    ys = []
    for g in range(SSM_GROUPS):
        bg = bm[:, g * SSM_STATE:(g + 1) * SSM_STATE]
        cg = cm[:, g * SSM_STATE:(g + 1) * SSM_STATE].astype(BF16)
        cb = _dot_nt(cg, jnp.concatenate([bg] * per_tile, axis=0).astype(BF16))
        bg = bg.astype(BF16)
        for q in range(SSM_HEADS_PER_GROUP // per_tile):
            h0 = g * SSM_HEADS_PER_GROUP + q * per_tile
            ccol = _lane_expand(cum, h0, per_tile, t)
            seg = ccol - cum_t[h0:h0 + 1, :]
            decay = jnp.exp(jnp.where(causal, seg, -jnp.inf))
            wts = (cb * decay * dt_t[h0:h0 + 1, :]).astype(BF16)
            xt = xs[:, h0 * SSM_HEAD_DIM:h0 * SSM_HEAD_DIM + width]
            xbd = jnp.where(blockdiag, jnp.concatenate([xt] * per_tile, axis=0), 0.0).astype(BF16)
            y_intra = _dot(wts, xbd)
            st = s_scr[h0:h0 + per_tile].reshape(width, SSM_STATE)
            cum_w = ccol if width == LANES else _lane_expand(cum, h0, per_tile, SSM_HEAD_DIM)
            y_state = _dot_nt(cg, st.astype(BF16)) * jnp.exp(cum_w)
            ys.append(y_intra + y_state)
            wend = jnp.exp(cum_w[t - 1:t, :] - cum_w) * _lane_expand(dt, h0, per_tile, SSM_HEAD_DIM)
            ds = _dot_tn((xt * wend).astype(BF16), bg)
            for i in range(per_tile):
                h = h0 + i
                s_scr[h] = jnp.exp(cum[t - 1:t, h:h + 1]) * s_scr[h] + ds[i * SSM_HEAD_DIM:(i + 1) * SSM_HEAD_DIM]

    y = jnp.concatenate(ys, axis=1) + dskip_ref[...] * xs
    y = y * _silu(z)
    half = SSM_INNER // SSM_GROUPS
    normed = []
    for g in range(SSM_GROUPS):
        yg = y[:, g * half:(g + 1) * half]
        normed.append(yg * lax.rsqrt(jnp.mean(yg * yg, axis=-1, keepdims=True) + RMS_EPS))
    y_ref[...] = (jnp.concatenate(normed, axis=1) * nw_ref[...]).astype(BF16)

    @_when(last)
    def _():
        hnew_ref[0] = s_scr[...]


def _scan_step_kernel(chunk_kernel, n_tok, n_const, n_state, seqs, chunks, t):
    def step(*refs):
        tok = refs[:n_tok]
        const = refs[n_tok:n_tok + n_const]
        st_in = refs[n_tok + n_const:n_tok + n_const + n_state]
        y_ref = refs[n_tok + n_const + n_state]
        st_out = refs[n_tok + n_const + n_state + 1:n_tok + n_const + 2 * n_state + 1]
        scratch = refs[n_tok + n_const + 2 * n_state + 1:]
        first_step = pl.program_id(1) == 0
        last_step = pl.program_id(1) == pl.num_programs(1) - 1
        for s in range(seqs):
            for c in range(chunks):
                rows = pl.ds((s * chunks + c) * t, t)
                chunk_kernel(*[r.at[rows, :] for r in tok], *const, *[r.at[pl.ds(s, 1)] for r in st_in],
                             y_ref.at[rows, :], *[r.at[pl.ds(s, 1)] for r in st_out], *scratch,
                             first=first_step if c == 0 else False,
                             last=last_step if c == chunks - 1 else False)
    return step


def _scan_call(kernel, name, tok_in, const_in, state_in, y_width, scratch, *, batch, seq, row0, t, seqs, chunks,
               step_kernel=None):
    nstep = seq // (t * chunks)
    rows = seqs * chunks * t
    blk0 = row0 // rows

    def per_batch(a):
        zeros = (0,) * (a.ndim - 1)
        return pl.BlockSpec((seqs,) + a.shape[1:], lambda b, c: (b,) + zeros)

    if step_kernel is None:
        step_kernel = _scan_step_kernel(kernel, len(tok_in), len(const_in), len(state_in), seqs, chunks, t)
    return pl.pallas_call(
        step_kernel,
        grid=(batch // seqs, nstep),
        in_specs=[pl.BlockSpec((rows, a.shape[1]), lambda b, c: (blk0 + b * nstep + c, 0)) for a in tok_in]
        + [pl.BlockSpec(a.shape, lambda b, c: (0, 0)) for a in const_in]
        + [per_batch(a) for a in state_in],
        out_specs=[pl.BlockSpec((rows, y_width), lambda b, c: (b * nstep + c, 0))]
        + [per_batch(a) for a in state_in],
        out_shape=[jax.ShapeDtypeStruct((batch * seq, y_width), BF16)]
        + [jax.ShapeDtypeStruct(a.shape, F32) for a in state_in],
        scratch_shapes=scratch,
        compiler_params=_params("parallel", "arbitrary"),
        name=name,
    )(*tok_in, *const_in, *state_in)


def _ssd(proj_a, small, consts, conv_state, h0, **where):
    t = where["t"]
    scratch = [pltpu.VMEM((SUBLANES, SSM_CONV_CH), F32),
               pltpu.VMEM((SSM_HEADS, SSM_HEAD_DIM, SSM_STATE), F32)]
    return _scan_call(functools.partial(_ssd_kernel, t=t), f"ssd_t{t}", [proj_a, small], consts,
                      [conv_state, h0], SSM_INNER, scratch, **where)


def _gdn_step_kernel(pb_ref, sm_ref, cw_ref, alog_ref, dtb_ref, nw_ref, cst_ref, s0_ref,
                     o_ref, cnew_ref, snew_ref, cbuf, s_scr, *, t, seqs, chunks):
    first_step = pl.program_id(1) == 0
    last_step = pl.program_id(1) == pl.num_programs(1) - 1
    units = [(s, c) for s in range(seqs) for c in range(chunks)]
    rows = lambda u: pl.ds(u * t, t)
    per_tile = min(LANES // t, DN_HEADS)
    n = per_tile * t
    ntile = DN_HEADS // per_tile
    heads = range(DN_HEADS)
    ii = lax.broadcasted_iota(jnp.int32, (n, n), 0)
    jj = lax.broadcasted_iota(jnp.int32, (n, n), 1)
    causal = _same_block(ii, jj, t) & (jj <= ii)
    diag = ii == jj
    stack = lambda xs, p: jnp.concatenate(xs[p * per_tile:(p + 1) * per_tile], axis=0)

    qkv, zb, gcum, beta = [], [], [], []
    for u, (s, c) in enumerate(units):
        @_when(first_step if c == 0 else False)
        def _():
            _load_conv_state(cbuf, cst_ref[s])

        conv, tail = _causal_conv(cbuf, pb_ref[rows(u), :DN_CONV_CH], cw_ref[...], t)

        @_when(last_step if c == chunks - 1 else False)
        def _():
            cnew_ref[s] = tail

        qkv.append(_silu(conv))
        zb.append(pb_ref[rows(u), DN_CONV_CH:].astype(F32))
        sm = sm_ref[rows(u), :]
        beta.append(_sigmoid(sm))
        gate = -jnp.exp(alog_ref[...]) * _softplus(sm + dtb_ref[...])
        gcum.append(_cumsum_rows(gate, t))
    gcum_t = [_grouped_transpose(g, per_tile) for g in gcum]

    sumsq = _row_sums_of_squares([x[:, i * DN_HEAD:(i + 1) * DN_HEAD] for x in qkv for i in range(2 * DN_HEADS)])
    q, k, v, gcol, bcol, kb, egcol = ([[] for _ in units] for _ in range(7))
    for u in range(len(units)):
        for h in heads:
            lo = h * DN_HEAD
            ss = sumsq[u * 2 * DN_HEADS:(u + 1) * 2 * DN_HEADS]
            q[u].append(qkv[u][:, lo:lo + DN_HEAD] * lax.rsqrt(ss[h] + L2_EPS) * (DN_HEAD ** -0.5))
            k[u].append(qkv[u][:, DN_QK + lo:DN_QK + lo + DN_HEAD] * lax.rsqrt(ss[DN_HEADS + h] + L2_EPS))
            v[u].append(qkv[u][:, 2 * DN_QK + lo:2 * DN_QK + lo + DN_HEAD])
            gcol[u].append(gcum[u][:, SMALL_A + h:SMALL_A + h + 1])
            bcol[u].append(beta[u][:, SMALL_B + h:SMALL_B + h + 1])
            kb[u].append(k[u][h] * bcol[u][h])
            egcol[u].append(jnp.exp(gcol[u][h]))

    tiles = [(u, p) for u in range(len(units)) for p in range(ntile)]
    k16 = [stack(k[u], p).astype(BF16) for u, p in tiles]
    decay = []
    for u, p in tiles:
        h0 = p * per_tile
        seg = stack(gcol[u], p) - gcum_t[u][SMALL_A + h0:SMALL_A + h0 + 1, :n]
        decay.append(jnp.exp(jnp.where(causal, seg, -jnp.inf)))
    lmat = [_dot_nt(stack(kb[u], p).astype(BF16), k16[i]) * jnp.where(diag, 0.0, decay[i])
            for i, (u, p) in enumerate(tiles)]
    qk = [(_dot_nt(stack(q[u], p).astype(BF16), k16[i]) * decay[i]).astype(BF16) for i, (u, p) in enumerate(tiles)]
    inv = [a.astype(BF16) for a in _unit_lower_inverses(lmat, ii, jj, t)]

    split = lambda x: (x[:, :DN_HEAD], x[:, DN_HEAD:])
    o_from, o_own, s_from, s_own, glast = ([[] for _ in units] for _ in range(5))
    aw = [_dot(inv[i], jnp.concatenate(
        [stack([kb[u][h] * egcol[u][h] for h in heads], p), stack([v[u][h] * bcol[u][h] for h in heads], p)],
        axis=1).astype(BF16)) for i, (u, p) in enumerate(tiles)]
    qaw = [_dot(qk[i], aw[i].astype(BF16)) for i in range(len(tiles))]
    pad_rows = -t % (2 * SUBLANES)
    pad = [jnp.zeros((pad_rows, DN_HEAD), F32)] if pad_rows else []
    for u in range(len(units)):
        for h in heads:
            p, j = divmod(h, per_tile)
            i = u * ntile + p
            sel = slice(j * t, (j + 1) * t)
            qkw, qku0 = split(qaw[i][sel])
            o_from[u].append(q[u][h] * egcol[u][h] - qkw)
            o_own[u].append(qku0)
            glast[u].append(gcum[u][t - 1:t, SMALL_A + h:SMALL_A + h + 1])
            kd = (k[u][h] * jnp.exp(glast[u][h] - gcol[u][h])).astype(BF16)
            kdw, kdu0 = split(_dot_tn(kd, aw[i][sel].astype(BF16)))
            s_from[u].append(kdw)
            s_own[u].append(kdu0)

    outs = []
    for u, (s, c) in enumerate(units):
        @_when(first_step if c == 0 else False)
        def _():
            s_scr[...] = s0_ref[s]

        for h in heads:
            state = s_scr[h]
            both = _dot(jnp.concatenate([s_from[u][h], o_from[u][h]] + pad, axis=0).astype(BF16), state.astype(BF16))
            outs.append(both[DN_HEAD:DN_HEAD + t] + o_own[u][h])
            s_scr[h] = jnp.exp(glast[u][h]) * state - both[:DN_HEAD] + s_own[u][h]

        @_when(last_step if c == chunks - 1 else False)
        def _():
            snew_ref[s] = s_scr[...]

    norm_w = nw_ref[...]
    osq = _row_sums_of_squares(outs)
    for u in range(len(units)):
        normed = []
        for h in heads:
            i = u * DN_HEADS + h
            normed.append(outs[i] * lax.rsqrt(osq[i] * (1.0 / DN_HEAD) + RMS_EPS) * norm_w
                          * _silu(zb[u][:, h * DN_HEAD:(h + 1) * DN_HEAD]))
        o_ref[rows(u), :] = jnp.concatenate(normed, axis=1).astype(BF16)


def _gdn(proj_b, small, consts, conv_state, s0, **where):
    t = where["t"]
    scratch = [pltpu.VMEM((SUBLANES, DN_CONV_CH), F32),
               pltpu.VMEM((DN_HEADS, DN_HEAD, DN_HEAD), F32)]
    step = functools.partial(_gdn_step_kernel, t=t, seqs=where["seqs"], chunks=where["chunks"])
    return _scan_call(None, f"gdn_t{t}", [proj_b, small], consts, [conv_state, s0], DN_V, scratch,
                      step_kernel=step, **where)


def _merge_kernel(xp_ref, xs_ref, yap_ref, yas_ref, obp_ref, obs_ref, eg_ref, eb_ref, wg_ref, wa_ref, wb_ref,
                  wo_ref, g_ref, b_ref, x1_ref, x1b_ref, *, prompt_tiles):
    is_prompt = pl.program_id(0) < prompt_tiles
    h = _layer_norm(jnp.where(is_prompt, xp_ref[...], xs_ref[...]), eg_ref[...], eb_ref[...])
    out_a = _dot(jnp.where(is_prompt, yap_ref[...], yas_ref[...]), wa_ref[...])
    out_b = _dot(jnp.where(is_prompt, obp_ref[...], obs_ref[...]), wb_ref[...])
    gt = _dot(h.astype(BF16), wg_ref[...])
    merged = _sigmoid(gt[:, :D_MODEL]) * out_a + _sigmoid(gt[:, D_MODEL:]) * out_b
    mix = _dot(merged.astype(BF16), wo_ref[...])
    x1 = _layer_norm(DEEPNORM_ALPHA * h + mix, g_ref[...], b_ref[...])
    x1_ref[...] = x1
    x1b_ref[...] = x1.astype(BF16)


def _merge(xp, xs, ya_p, ya_s, ob_p, ob_s, consts, tm):
    m = xp.shape[0] + xs.shape[0]
    n_p = xp.shape[0]
    tm = _row_tile(math.gcd(n_p, m - n_p), tm)
    in_prompt, in_sample = _group_maps(n_p // tm)
    row = lambda i: (i, 0)
    const = lambda i: (0, 0)
    return pl.pallas_call(
        functools.partial(_merge_kernel, prompt_tiles=n_p // tm),
        grid=(m // tm,),
        in_specs=[pl.BlockSpec((tm, D_MODEL), in_prompt), pl.BlockSpec((tm, D_MODEL), in_sample),
                  pl.BlockSpec((tm, SSM_INNER), in_prompt), pl.BlockSpec((tm, SSM_INNER), in_sample),
                  pl.BlockSpec((tm, DN_V), in_prompt), pl.BlockSpec((tm, DN_V), in_sample)]
        + [pl.BlockSpec(a.shape, const) for a in consts],
        out_specs=[pl.BlockSpec((tm, D_MODEL), row), pl.BlockSpec((tm, D_MODEL), row)],
        out_shape=[jax.ShapeDtypeStruct((m, D_MODEL), F32), jax.ShapeDtypeStruct((m, D_MODEL), BF16)],
        compiler_params=_params("parallel"),
        name="merge",
    )(xp, xs, ya_p, ya_s, ob_p, ob_s, *consts)


def _take_first_max(rest, index, limit, axis):
    best = jnp.max(rest, axis=axis, keepdims=True)
    first = jnp.min(jnp.where(rest == best, index, limit), axis=axis, keepdims=True)
    hit = index == first
    return hit, jnp.where(hit, -jnp.inf, rest)


def _router_gates_t(x1, xb, rw_hi, rw_lo, bias):
    tm = x1.shape[0]
    x_lo = (x1 - xb.astype(F32)).astype(BF16)
    logits = _dot_nt(rw_hi, xb) + (_dot_nt(rw_hi, x_lo) + _dot_nt(rw_lo, xb))
    scores = _sigmoid(logits)
    row = lax.broadcasted_iota(jnp.int32, (LANES, tm), 0)
    sel = jnp.where(row < N_EXPERTS, scores + bias, -jnp.inf)
    by_group = sel.reshape(LANES // EXPERTS_PER_GROUP, EXPERTS_PER_GROUP, tm)
    sub = lax.broadcasted_iota(jnp.int32, by_group.shape, 1)
    top1 = jnp.max(by_group, axis=1, keepdims=True)
    _, others = _take_first_max(by_group, sub, EXPERTS_PER_GROUP, 1)
    gscore = (top1 + jnp.max(others, axis=1, keepdims=True))[:N_EXPERT_GROUPS]
    gidx = lax.broadcasted_iota(jnp.int32, gscore.shape, 0)
    gkeep = jnp.zeros(gscore.shape, jnp.bool_)
    for _ in range(TOPK_GROUPS):
        hit, gscore = _take_first_max(gscore, gidx, N_EXPERT_GROUPS, 0)
        gkeep = gkeep | hit
    gkeep = jnp.broadcast_to(gkeep, (N_EXPERT_GROUPS, EXPERTS_PER_GROUP, tm)).reshape(N_EXPERTS, tm)
    rest = jnp.where(gkeep, sel[:N_EXPERTS], -jnp.inf)
    eidx = lax.broadcasted_iota(jnp.int32, (N_EXPERTS, tm), 0)
    keep = jnp.zeros((N_EXPERTS, tm), jnp.bool_)
    for _ in range(TOP_K):
        hit, rest = _take_first_max(rest, eidx, N_EXPERTS, 0)
        keep = keep | hit
    picked = jnp.where(keep, scores[:N_EXPERTS], 0.0)
    gates = picked / jnp.sum(picked, axis=0, keepdims=True) * ROUTED_SCALE
    return jnp.concatenate([gates, jnp.zeros((LANES - N_EXPERTS, tm), F32)], axis=0)


def _dense_router_kernel(x1_ref, x1b_ref, pp_ref, ps_ref, sw1_ref, sw3_ref, sw2_ref, pw_ref, pg_ref, rwh_ref,
                         rwl_ref, rb_ref, dense_ref, gates_ref, rank_ref, chosen_t_ref, rank_t_ref, count_ref,
                         *, prompt_tiles):
    xb = x1b_ref[...]
    hid = _silu(_dot(xb, sw1_ref[...])) * _dot(xb, sw3_ref[...])
    shared = _dot(hid.astype(BF16), sw2_ref[...])
    p = jnp.where(pl.program_id(0) < prompt_tiles, pp_ref[...], ps_ref[...])
    ple = _dot(p.astype(BF16), pw_ref[...]) * _sigmoid(_dot(xb, pg_ref[...]))
    x1 = x1_ref[...]
    dense_ref[...] = DEEPNORM_ALPHA * x1 + shared + ple
    ii = lax.broadcasted_iota(jnp.int32, (MOE_BLOCK, MOE_BLOCK), 0)
    jj = lax.broadcasted_iota(jnp.int32, (MOE_BLOCK, MOE_BLOCK), 1)
    earlier = (ii < jj).astype(BF16)
    for blk in range(x1.shape[0] // MOE_BLOCK):
        tok = slice(blk * MOE_BLOCK, (blk + 1) * MOE_BLOCK)
        exp = slice(blk * LANES, (blk + 1) * LANES)
        gates_t = _router_gates_t(x1[tok], xb[tok], rwh_ref[...], rwl_ref[...], rb_ref[...])
        chosen_t = (gates_t > 0.0).astype(BF16)
        rank_t = _dot(chosen_t, earlier)
        chosen_t_ref[exp, :] = chosen_t
        rank_t_ref[exp, :] = rank_t.astype(BF16)
        gates = gates_t.T
        rank = rank_t.T
        gates_ref[tok, :] = gates.astype(BF16)
        rank_ref[tok, :] = rank.astype(BF16)
        count = rank[MOE_BLOCK - 1:, :] + (gates[MOE_BLOCK - 1:, :] > 0.0).astype(F32)
        count_ref[blk] = jnp.broadcast_to(count, (SUBLANES, LANES))


def _dense_router(x1, x1b, p_prompt, p_sample, consts):
    m = x1.shape[0]
    tm = _row_tile(math.gcd(p_prompt.shape[0], p_sample.shape[0]), ROUTER_TM)
    assert tm % MOE_BLOCK == 0
    nblk = m // MOE_BLOCK
    per_step = tm // MOE_BLOCK
    prompt_tiles = p_prompt.shape[0] // tm
    in_prompt, in_sample = _group_maps(prompt_tiles)
    row = lambda i: (i, 0)
    const = lambda i: (0, 0)
    return pl.pallas_call(
        functools.partial(_dense_router_kernel, prompt_tiles=prompt_tiles),
        grid=(m // tm,),
        in_specs=[pl.BlockSpec((tm, D_MODEL), row), pl.BlockSpec((tm, D_MODEL), row),
                  pl.BlockSpec((tm, PLE_DIM), in_prompt), pl.BlockSpec((tm, PLE_DIM), in_sample)]
        + [pl.BlockSpec(a.shape, const) for a in consts],
        out_specs=[pl.BlockSpec((tm, D_MODEL), row), pl.BlockSpec((tm, LANES), row),
                   pl.BlockSpec((tm, LANES), row), pl.BlockSpec((per_step * LANES, MOE_BLOCK), row),
                   pl.BlockSpec((per_step * LANES, MOE_BLOCK), row),
                   pl.BlockSpec((per_step, SUBLANES, LANES), lambda i: (i, 0, 0))],
        out_shape=[jax.ShapeDtypeStruct((m, D_MODEL), F32), jax.ShapeDtypeStruct((m, LANES), BF16),
                   jax.ShapeDtypeStruct((m, LANES), BF16), jax.ShapeDtypeStruct((nblk * LANES, MOE_BLOCK), BF16),
                   jax.ShapeDtypeStruct((nblk * LANES, MOE_BLOCK), BF16),
                   jax.ShapeDtypeStruct((nblk, SUBLANES, LANES), F32)],
        compiler_params=_params("parallel"),
        name="dense_router",
    )(x1, x1b, p_prompt, p_sample, *consts)


def _moe_tables(counts, tm):
    nblk = counts.shape[0]
    nchunk_max = MOE_CAP // MOE_SEG
    padded = (counts + MOE_SEG - 1) // MOE_SEG * MOE_SEG
    start = jnp.cumsum(padded, axis=1) - padded
    used = jnp.sum(padded, axis=1)
    rows_e = jnp.sum(padded, axis=0)
    rows_e_t = (rows_e + tm - 1) // tm * tm
    base_e = jnp.cumsum(rows_e_t) - rows_e_t
    seg_row = base_e[None, :] + jnp.cumsum(padded, axis=0) - padded
    chunk0 = jnp.arange(nchunk_max, dtype=jnp.int32) * MOE_SEG
    owner = jnp.sum((start + padded)[:, None, :] <= chunk0[None, :, None], axis=-1)
    onehot = owner[..., None] == jnp.arange(N_EXPERTS, dtype=jnp.int32)
    dst = chunk0[None, :] + jnp.sum(jnp.where(onehot, (seg_row - start)[:, None, :], 0), axis=-1)
    n_tiles = _moe_max_tiles(nblk, tm)
    live = chunk0[None, :] < used[:, None]
    parity = (jnp.arange(nblk, dtype=jnp.int32) % 2)[:, None]
    spare = n_tiles * tm + parity * MOE_CAP + chunk0[None, :]
    src = jnp.where(live, dst, 0)
    dst = jnp.where(live, dst, spare)
    tile_row = jnp.arange(n_tiles, dtype=jnp.int32) * tm
    tiles_used = jnp.sum(rows_e_t) // tm
    tile_row = jnp.minimum(tile_row, (tiles_used - 1) * tm)
    tile_expert = jnp.minimum(jnp.sum((base_e + rows_e_t)[None, :] <= tile_row[:, None], axis=-1), N_EXPERTS - 1)
    in_use = rows_e > 0
    experts = jnp.arange(N_EXPERTS, dtype=jnp.int32)
    later = jnp.where(in_use[None, :] & (experts[None, :] > experts[:, None]), experts[None, :], N_EXPERTS)
    next_in_use = jnp.min(later, axis=1)
    next_in_use = jnp.where(next_in_use == N_EXPERTS, -1, next_in_use)
    order = jnp.sum(in_use[None, :] & (experts[None, :] < experts[:, None]), axis=1)
    of_tile = tile_expert[:, None] == experts[None, :]
    tile_slot = jnp.sum(jnp.where(of_tile, (order % 2)[None, :], 0), axis=1)
    tile_next = jnp.sum(jnp.where(of_tile, next_in_use[None, :], 0), axis=1)
    bounds = jnp.zeros((nblk, SUBLANES, LANES), F32)
    bounds = bounds.at[:, 0, :N_EXPERTS].set(start.astype(F32)).at[:, 1, :N_EXPERTS].set(padded.astype(F32))
    i32 = lambda a: a.astype(jnp.int32)
    return dict(dst=i32(dst.reshape(-1)), src=i32(src.reshape(-1)), nchunk=i32(used // MOE_SEG),
                pad_row=i32(base_e + rows_e),
                pad_n=i32((rows_e_t - rows_e) // MOE_SEG), tile_expert=i32(tile_expert),
                tile_slot=i32(tile_slot), tile_next=i32(tile_next),
                tiles_used=i32(tiles_used.reshape(1)), bounds=bounds)


def _moe_max_tiles(nblk, tm):
    return (nblk * (MOE_BLOCK * TOP_K + N_EXPERTS * (MOE_SEG - 1)) + N_EXPERTS * (tm - MOE_SEG) + tm - 1) // tm


def _slot_bounds(bounds):
    return bounds[0:1, :], bounds[1:2, :]


def _moe_gather_kernel(dst_ref, nchunk_ref, pad_row_ref, pad_n_ref,
                       x_ref, chosen_t_ref, rank_t_ref, bounds_ref, rows_ref, buf, zbuf, sem, zsem):
    j = pl.program_id(0)
    nblk = pl.num_programs(0)
    slot = j % 2
    nchunk_max = MOE_CAP // MOE_SEG

    def chunk_copy(blk, q, sl):
        src = buf.at[sl, pl.ds(pl.multiple_of(q * MOE_SEG, MOE_SEG), MOE_SEG), :]
        row = pl.multiple_of(dst_ref[blk * nchunk_max + q], MOE_SEG)
        return pltpu.make_async_copy(src, rows_ref.at[pl.ds(row, MOE_SEG), :], sem.at[sl])

    def pad_copy(e, i):
        row = pl.multiple_of(pad_row_ref[e] + i * MOE_SEG, MOE_SEG)
        return pltpu.make_async_copy(zbuf, rows_ref.at[pl.ds(row, MOE_SEG), :], zsem)

    def sub_chunks(blk, sub, sl, action):
        for q in range(sub * (MOE_SUB // MOE_SEG), (sub + 1) * (MOE_SUB // MOE_SEG)):
            action(chunk_copy(blk, q, sl))

    def sub_used(blk, sub):
        return True if sub < MOE_SUB_ALWAYS else sub * MOE_SUB < nchunk_ref[blk] * MOE_SEG

    def for_pads(action):
        def per_expert(e, carry):
            def body(i, c):
                action(pad_copy(e, i))
                return c
            return lax.fori_loop(0, pad_n_ref[e], body, carry)
        lax.fori_loop(0, N_EXPERTS, per_expert, 0)

    @pl.when(j == 0)
    def _():
        zbuf[...] = jnp.zeros_like(zbuf)
        for_pads(lambda c: c.start())

    start, length = _slot_bounds(bounds_ref[0])
    x = x_ref[...]
    chosen_rank_t = jnp.concatenate([chosen_t_ref[...], rank_t_ref[...]], axis=1)
    before = jnp.maximum(j - 2, 0)
    subs = range(MOE_CAP // MOE_SUB)

    @pl.when(j >= 2)
    def _():
        for sub in subs:
            @_when(sub_used(before, sub))
            def _():
                sub_chunks(before, sub, slot, lambda c: c.wait())

    def owners(sub):
        s = (sub * MOE_SUB + lax.broadcasted_iota(jnp.int32, (MOE_SUB, LANES), 0)).astype(F32)
        owner = (s >= start) & (s < start + length)
        within = s[:, 0:1] - jnp.sum(jnp.where(owner, start, 0.0), axis=1, keepdims=True)
        return owner.astype(BF16), within

    def picks(owner_within):
        owner, within = owner_within
        hit = _dot(owner, chosen_rank_t)
        return ((hit[:, :MOE_BLOCK] > 0.5) & (hit[:, MOE_BLOCK:] == within)).astype(BF16)

    def emit(sub, pick):
        buf[slot, sub * MOE_SUB:(sub + 1) * MOE_SUB, :] = _dot(pick, x).astype(BF16)
        sub_chunks(j, sub, slot, lambda c: c.start())

    staged = [picks(ow) for ow in [owners(sub) for sub in subs[:MOE_SUB_ALWAYS]]]
    for sub, pick in enumerate(staged):
        emit(sub, pick)
    for sub in subs[MOE_SUB_ALWAYS:]:
        @pl.when(sub_used(j, sub))
        def _():
            emit(sub, picks(owners(sub)))

    @pl.when(j == nblk - 1)
    def _():
        prev = jnp.maximum(j - 1, 0)
        for sub in subs:
            @_when(sub_used(j, sub))
            def _():
                sub_chunks(j, sub, slot, lambda c: c.wait())

            @_when((j >= 1) & sub_used(prev, sub))
            def _():
                sub_chunks(prev, sub, 1 - slot, lambda c: c.wait())

        for_pads(lambda c: c.wait())


def _moe_gather(x1b, chosen_t, rank_t, tables, tm):
    m = x1b.shape[0]
    nblk = m // MOE_BLOCK
    rows = _moe_max_tiles(nblk, tm) * tm + 2 * MOE_CAP
    blk = lambda j, *_: (j, 0)
    grid_spec = pltpu.PrefetchScalarGridSpec(
        num_scalar_prefetch=4,
        grid=(nblk,),
        in_specs=[pl.BlockSpec((MOE_BLOCK, D_MODEL), blk), pl.BlockSpec((LANES, MOE_BLOCK), blk),
                  pl.BlockSpec((LANES, MOE_BLOCK), blk),
                  pl.BlockSpec((1, SUBLANES, LANES), lambda j, *_: (j, 0, 0))],
        out_specs=pl.BlockSpec(memory_space=pl.ANY),
        scratch_shapes=[pltpu.VMEM((2, MOE_CAP, D_MODEL), BF16), pltpu.VMEM((MOE_SEG, D_MODEL), BF16),
                        pltpu.SemaphoreType.DMA((2,)), pltpu.SemaphoreType.DMA(())],
    )
    return pl.pallas_call(
        _moe_gather_kernel,
        grid_spec=grid_spec,
        out_shape=jax.ShapeDtypeStruct((rows, D_MODEL), BF16),
        compiler_params=_params("arbitrary"),
        name="moe_gather",
    )(tables["dst"], tables["nchunk"], tables["pad_row"], tables["pad_n"], x1b, chosen_t, rank_t, tables["bounds"])


def _moe_ffn_kernel(tile_expert_ref, tile_slot_ref, tile_next_ref, x_hbm, w1_hbm, w3_hbm, w2_hbm, y_ref,
                    x_buf, w1_buf, w3_buf, w2_buf, w13_scr, w2_scr, sem, x_sem):
    i = pl.program_id(0)
    n = pl.num_programs(0)
    expert = tile_expert_ref[i]
    slot = tile_slot_ref[i]
    ahead = MOE_ROW_SLOTS - 1
    tm = x_buf.shape[1]

    def row_copy(t):
        rows = pl.ds(pl.multiple_of(t * tm, tm), tm)
        return pltpu.make_async_copy(x_hbm.at[rows], x_buf.at[t % MOE_ROW_SLOTS], x_sem.at[t % MOE_ROW_SLOTS])

    @pl.when(i == 0)
    def _():
        for t in range(ahead):
            @pl.when(t < n)
            def _():
                row_copy(jnp.int32(t)).start()

    @pl.when(i + ahead < n)
    def _():
        row_copy(i + ahead).start()

    def weight_copies(e, s):
        pairs = ((w1_hbm, w1_buf), (w3_hbm, w3_buf), (w2_hbm, w2_buf))
        return [pltpu.make_async_copy(w.at[e], buf.at[s], sem.at[s, n]) for n, (w, buf) in enumerate(pairs)]

    @pl.when(i == 0)
    def _():
        for copy in weight_copies(expert, slot):
            copy.start()

    @pl.when((i == 0) | (expert != tile_expert_ref[jnp.maximum(i - 1, 0)]))
    def _():
        for copy in weight_copies(expert, slot):
            copy.wait()

        @pl.when(tile_next_ref[i] >= 0)
        def _():
            for copy in weight_copies(tile_next_ref[i], 1 - slot):
                copy.start()

        w13_scr[:, :EXPERT_FF] = w1_buf[slot].astype(BF16)
        w13_scr[:, EXPERT_FF:] = w3_buf[slot].astype(BF16)
        w2_scr[...] = w2_buf[slot].astype(BF16)

    row_copy(i).wait()
    up = _dot(x_buf[i % MOE_ROW_SLOTS], w13_scr[...])
    hid = _silu(up[:, :EXPERT_FF]) * up[:, EXPERT_FF:]
    y_ref[...] = _dot(hid.astype(BF16), w2_scr[...]).astype(BF16)


def _moe_ffn(rows, w1, w3, w2, tables, tm):
    grid_spec = pltpu.PrefetchScalarGridSpec(
        num_scalar_prefetch=3,
        grid=(tables["tiles_used"][0],),
        in_specs=[pl.BlockSpec(memory_space=pl.ANY)] * 4,
        out_specs=pl.BlockSpec((tm, D_MODEL), lambda i, *_: (i, 0)),
        scratch_shapes=[pltpu.VMEM((MOE_ROW_SLOTS, tm, D_MODEL), BF16),
                        pltpu.VMEM((2, D_MODEL, EXPERT_FF), F32), pltpu.VMEM((2, D_MODEL, EXPERT_FF), F32),
                        pltpu.VMEM((2, EXPERT_FF, D_MODEL), F32),
                        pltpu.VMEM((D_MODEL, 2 * EXPERT_FF), BF16), pltpu.VMEM((EXPERT_FF, D_MODEL), BF16),
                        pltpu.SemaphoreType.DMA((2, 3)), pltpu.SemaphoreType.DMA((MOE_ROW_SLOTS,))],
    )
    return pl.pallas_call(
        _moe_ffn_kernel,
        grid_spec=grid_spec,
        out_shape=jax.ShapeDtypeStruct(rows.shape, BF16),
        compiler_params=_params("arbitrary"),
        name="moe_ffn",
    )(tables["tile_expert"], tables["tile_slot"], tables["tile_next"], rows, w1, w3, w2)


def _moe_combine_kernel(dst_ref, nchunk_ref, dense_ref, gates_ref, rank_ref, bounds_ref, g_ref, b_ref, rows_ref,
                        op_ref, os_ref, acc, buf, sem, *, prompt_tiles):
    j = pl.program_id(0)
    nblk = pl.num_programs(0)
    slot = j % 2
    nchunk_max = MOE_CAP // MOE_SEG

    def chunk_copy(blk, q, sl):
        row = pl.multiple_of(dst_ref[blk * nchunk_max + q], MOE_SEG)
        dst = buf.at[sl, pl.ds(pl.multiple_of(q * MOE_SEG, MOE_SEG), MOE_SEG), :]
        return pltpu.make_async_copy(rows_ref.at[pl.ds(row, MOE_SEG), :], dst, sem.at[sl])

    subs = range(MOE_CAP // MOE_SUB)
    always, rest = subs[:MOE_SUB_ALWAYS], subs[MOE_SUB_ALWAYS:]

    def sub_used(blk, sub):
        return True if sub < MOE_SUB_ALWAYS else sub * MOE_SUB < nchunk_ref[blk] * MOE_SEG

    def group(blk, sub, sl, action):
        for q in range(sub * (MOE_SUB // MOE_SEG), (sub + 1) * (MOE_SUB // MOE_SEG)):
            action(chunk_copy(blk, q, sl))

    @pl.when(j == 0)
    def _():
        buf[...] = jnp.zeros_like(buf)
        for sub in subs:
            @_when(sub_used(0, sub))
            def _():
                group(0, sub, 0, lambda c: c.start())

    for sub in subs:
        @_when(sub_used(j, sub))
        def _():
            group(j, sub, slot, lambda c: c.wait())

    bounds_t = jnp.concatenate([bounds_ref[0], jnp.zeros((LANES - SUBLANES, LANES), F32)], axis=0).T
    start, length = bounds_t[:, 0:1], bounds_t[:, 1:2]
    gates_rank = jnp.concatenate([gates_ref[...], rank_ref[...]], axis=0)

    def owners(sub):
        s = (sub * MOE_SUB + lax.broadcasted_iota(jnp.int32, (LANES, MOE_SUB), 1)).astype(F32)
        owner = (s >= start) & (s < start + length)
        within = s[0:1, :] - jnp.sum(jnp.where(owner, start, 0.0), axis=0, keepdims=True)
        return owner.astype(BF16), within

    def weights(owner_within):
        owner, within = owner_within
        hit = _dot(gates_rank, owner)
        return jnp.where(hit[MOE_BLOCK:] == within, hit[:MOE_BLOCK], 0.0).astype(BF16)

    def apply(sub, weight):
        return _dot(weight, buf[slot, sub * MOE_SUB:(sub + 1) * MOE_SUB, :])

    nxt = jnp.minimum(j + 1, nblk - 1)
    staged = [owners(sub) for sub in always]
    for sub in always:
        group(nxt, sub, 1 - slot, lambda c: c.start())
    staged = [weights(ow) for ow in staged]
    total = dense_ref[...]
    for sub, weight in zip(always, staged):
        total = total + apply(sub, weight)
    acc[...] = total
    for sub in rest:
        @pl.when((j + 1 < nblk) & sub_used(nxt, sub))
        def _():
            group(nxt, sub, 1 - slot, lambda c: c.start())

        @pl.when(sub_used(j, sub))
        def _():
            acc[...] += apply(sub, weights(owners(sub)))

    @pl.when(j == nblk - 1)
    def _():
        for sub in always:
            group(nxt, sub, 1 - slot, lambda c: c.wait())

    @pl.when(j < prompt_tiles)
    def _():
        op_ref[...] = _layer_norm(acc[...], g_ref[...], b_ref[...])

    @pl.when(j >= prompt_tiles)
    def _():
        os_ref[...] = _layer_norm(acc[...], g_ref[...], b_ref[...])


def _moe_combine(rows, dense, gates, rank, tables, g, b, n_prompt):
    m = dense.shape[0]
    nblk = m // MOE_BLOCK
    prompt_tiles = n_prompt // MOE_BLOCK
    in_prompt, in_sample = _group_maps(prompt_tiles)
    blk = lambda j, *_: (j, 0)
    const = lambda j, *_: (0, 0)
    grid_spec = pltpu.PrefetchScalarGridSpec(
        num_scalar_prefetch=2,
        grid=(nblk,),
        in_specs=[pl.BlockSpec((MOE_BLOCK, D_MODEL), blk), pl.BlockSpec((MOE_BLOCK, LANES), blk),
                  pl.BlockSpec((MOE_BLOCK, LANES), blk),
                  pl.BlockSpec((1, SUBLANES, LANES), lambda j, *_: (j, 0, 0)),
                  pl.BlockSpec((1, D_MODEL), const), pl.BlockSpec((1, D_MODEL), const),
                  pl.BlockSpec(memory_space=pl.ANY)],
        out_specs=[pl.BlockSpec((MOE_BLOCK, D_MODEL), in_prompt), pl.BlockSpec((MOE_BLOCK, D_MODEL), in_sample)],
        scratch_shapes=[pltpu.VMEM((MOE_BLOCK, D_MODEL), F32), pltpu.VMEM((2, MOE_CAP, D_MODEL), BF16),
                        pltpu.SemaphoreType.DMA((2,))],
    )
    return pl.pallas_call(
        functools.partial(_moe_combine_kernel, prompt_tiles=prompt_tiles),
        grid_spec=grid_spec,
        out_shape=[jax.ShapeDtypeStruct((n_prompt, D_MODEL), F32),
                   jax.ShapeDtypeStruct((m - n_prompt, D_MODEL), F32)],
        compiler_params=_params("arbitrary"),
        name="moe_combine",
    )(tables["src"], tables["nchunk"], dense, gates, rank, tables["bounds"], g, b, rows)


def _row(v):
    return v.reshape(1, -1).astype(F32)


def _scan_layout(batch, seq, row0):
    t = math.gcd(seq, CHUNK)
    nchunk = seq // t
    chunks = SCAN_CHUNKS if nchunk % SCAN_CHUNKS == 0 else 1
    seqs = SCAN_SEQS if (nchunk == 1 and batch % SCAN_SEQS == 0 and row0 % (SCAN_SEQS * t) == 0) else 1
    return dict(batch=batch, seq=seq, row0=row0, t=t, seqs=seqs, chunks=chunks)


def _lanes_at(v, start):
    return jnp.zeros((1, LANES), F32).at[0, start:start + v.shape[0]].set(v.astype(F32))


def kernel(x_prompt, x_sample, p_prompt, p_sample, state_ssm_conv, state_ssm, state_dn_conv, state_dn, emb_ln_g, emb_ln_b, w_in, conv_a_w, conv_a_b, ssm_dt_bias, ssm_a_log, ssm_d, ssm_norm_w, w_a, conv_b_w, dn_dt_bias, dn_a_log, dn_norm_w, w_b, w_o, ln1_g, ln1_b, router_w, router_bias, exp_w1, exp_w3, exp_w2, sh_w1, sh_w3, sh_w2, ple_w, ple_gate_w, ln2_g, ln2_b):
    bp, lp, _ = x_prompt.shape
    bs, ls, _ = x_sample.shape
    n_p = bp * lp
    n_s = bs * ls
    xp = x_prompt.reshape(n_p, D_MODEL)
    xs = x_sample.reshape(n_s, D_MODEL)

    w = w_in[0]
    o_z, o_xbc, o_dt = 0, SSM_INNER, SSM_INNER + SSM_CONV_CH
    o_qkv = o_dt + SSM_HEADS
    o_a = o_qkv + DN_CONV_CH
    o_b = o_a + DN_HEADS
    o_zb = o_b + DN_HEADS
    o_ga = o_zb + DN_V
    w_pa = w[:, o_z:o_dt].astype(BF16)
    w_pb = jnp.concatenate([w[:, o_qkv:o_a], w[:, o_zb:o_ga]], axis=1).astype(BF16)
    w_pg = w[:, o_ga:].astype(BF16)
    w_ps = jnp.zeros((D_MODEL, LANES), F32)
    w_ps = w_ps.at[:, SMALL_DT:SMALL_DT + SSM_HEADS].set(w[:, o_dt:o_qkv])
    w_ps = w_ps.at[:, SMALL_A:SMALL_A + 2 * DN_HEADS].set(w[:, o_a:o_zb]).astype(BF16)

    eg, eb = _row(emb_ln_g), _row(emb_ln_b)
    proj_a, small = _ln_matmul(xp, xs, eg, eb, w_pa, PROJ_TM, PROJ_TN_A, w_ps)
    proj_b = _ln_matmul(xp, xs, eg, eb, w_pb, PROJ_TM, PROJ_TN_B)

    ssd_consts = [conv_a_w[0], _row(conv_a_b[0]), _lanes_at(ssm_dt_bias[0], SMALL_DT),
                  _lanes_at(ssm_a_log[0], SMALL_DT), _row(jnp.repeat(ssm_d[0], SSM_HEAD_DIM)),
                  _row(ssm_norm_w[0])]
    gdn_consts = [conv_b_w[0], _lanes_at(dn_a_log[0], SMALL_A), _lanes_at(dn_dt_bias[0], SMALL_A),
                  _row(dn_norm_w[0])]
    zeros = lambda *s: jnp.zeros(s, F32)
    prompt = _scan_layout(bp, lp, 0)
    sample = _scan_layout(bs, ls, n_p)

    ya_p, pa_conv, pa_ssm = _ssd(proj_a, small, ssd_consts, zeros(bp, CONV_WIDTH - 1, SSM_CONV_CH),
                                 zeros(bp, SSM_HEADS, SSM_HEAD_DIM, SSM_STATE), **prompt)
    ya_s, sa_conv, sa_ssm = _ssd(proj_a, small, ssd_consts, state_ssm_conv[0], state_ssm[0], **sample)
    ob_p, pb_conv, pb_dn = _gdn(proj_b, small, gdn_consts, zeros(bp, CONV_WIDTH - 1, DN_CONV_CH),
                                zeros(bp, DN_HEADS, DN_HEAD, DN_HEAD), **prompt)
    ob_s, sb_conv, sb_dn = _gdn(proj_b, small, gdn_consts, state_dn_conv[0], state_dn[0], **sample)

    x1, x1b = _merge(
        xp, xs, ya_p, ya_s, ob_p, ob_s,
        [eg, eb, w_pg, w_a[0].astype(BF16), w_b[0].astype(BF16), w_o[0].astype(BF16), _row(ln1_g[0]), _row(ln1_b[0])],
        MERGE_TM)

    router_w_t = jnp.zeros((LANES, D_MODEL), F32).at[:N_EXPERTS].set(router_w[0].T)
    router_w_hi = router_w_t.astype(BF16)
    router_w_lo = (router_w_t - router_w_hi.astype(F32)).astype(BF16)
    router_b = jnp.broadcast_to(_lanes_at(router_bias[0], 0).reshape(LANES, 1), (LANES, MOE_BLOCK))
    dense, gates, rank, chosen_t, rank_t, counts = _dense_router(
        x1, x1b, p_prompt[0].reshape(n_p, PLE_DIM), p_sample[0].reshape(n_s, PLE_DIM),
        [sh_w1[0].astype(BF16), sh_w3[0].astype(BF16), sh_w2[0].astype(BF16), ple_w[0].astype(BF16),
         ple_gate_w[0].astype(BF16), router_w_hi, router_w_lo, router_b])

    tables = _moe_tables(counts[:, 0, :N_EXPERTS].astype(jnp.int32), MOE_TM)
    sorted_rows = _moe_gather(x1b, chosen_t, rank_t, tables, MOE_TM)
    expert_out = _moe_ffn(sorted_rows, exp_w1[0], exp_w3[0], exp_w2[0], tables, MOE_TM)
    out_p, out_s = _moe_combine(expert_out, dense, gates, rank, tables, _row(ln2_g[0]), _row(ln2_b[0]), n_p)

    return (out_p.reshape(bp, lp, D_MODEL), out_s.reshape(bs, ls, D_MODEL),
            pa_conv[None], pa_ssm[None], pb_conv[None], pb_dn[None],
            sa_conv[None], sa_ssm[None], sb_conv[None], sb_dn[None])
```

```python
import functools
import math

import jax
import jax.numpy as jnp
from jax import lax
from jax.experimental import pallas as pl
from jax.experimental.pallas import tpu as pltpu

F32 = jnp.float32
BF16 = jnp.bfloat16
HIGHEST = lax.Precision.HIGHEST

D_MODEL = 1024
SSM_INNER = 2048
SSM_HEAD_DIM = 64
SSM_HEADS = 32
SSM_GROUPS = 2
SSM_HEADS_PER_GROUP = 16
SSM_STATE = 128
SSM_CONV_CH = 2560
DN_HEADS = 8
DN_HEAD = 128
DN_QK = 1024
DN_V = 1024
DN_CONV_CH = 3072
CONV_WIDTH = 4
CHUNK = 64
N_EXPERTS = 64
TOP_K = 8
N_EXPERT_GROUPS = 8
EXPERTS_PER_GROUP = 8
TOPK_GROUPS = 4
EXPERT_FF = 256
SHARED_FF = 256
ROUTED_SCALE = 2.5
PLE_DIM = 256
LN_EPS = 1e-5
RMS_EPS = 1e-6
L2_EPS = 1e-6
DEEPNORM_ALPHA = 2.0 ** 0.25

LANES = 128
SUBLANES = 8
VMEM_LIMIT = 56 * 1024 * 1024
SMALL_DT = 0
SMALL_A = 32
SMALL_B = 40
PROJ_TM = 1024
PROJ_TN_A = 2304
PROJ_TN_B = 2048
MERGE_TM = 512
SCAN_CHUNKS = 8
SCAN_SEQS = 8
MOE_BLOCK = 256
MOE_SEG = 16
MOE_SUB = 512
MOE_CAP = -(-(MOE_BLOCK * TOP_K + N_EXPERTS * (MOE_SEG - 1)) // MOE_SUB) * MOE_SUB
MOE_TM = 512
MOE_ROW_SLOTS = 3
ROUTER_TM = 512
MOE_SUB_ALWAYS = -(-(MOE_BLOCK * TOP_K + N_EXPERTS * MOE_SEG // 2) // MOE_SUB)


def _sigmoid(x):
    return 1.0 / (1.0 + jnp.exp(-x))


def _silu(x):
    return x * _sigmoid(x)


def _softplus(x):
    return jnp.maximum(x, 0.0) + jnp.log(1.0 + jnp.exp(-jnp.abs(x)))


def _layer_norm(x, g, b):
    mu = jnp.mean(x, axis=-1, keepdims=True)
    xc = x - mu
    var = jnp.mean(xc * xc, axis=-1, keepdims=True)
    return xc * lax.rsqrt(var + LN_EPS) * g + b


def _dot(a, b):
    return jnp.dot(a, b, preferred_element_type=F32)


def _dot_nt(a, b):
    return lax.dot_general(a, b, (((1,), (1,)), ((), ())), preferred_element_type=F32)


def _dot_tn(a, b):
    return lax.dot_general(a, b, (((0,), (0,)), ((), ())), preferred_element_type=F32)


def _dot_f32(a, b):
    return jnp.dot(a, b, precision=HIGHEST, preferred_element_type=F32)


def _params(*sem):
    return pltpu.CompilerParams(dimension_semantics=sem, vmem_limit_bytes=VMEM_LIMIT)


def _when(cond):
    if cond is True:
        return lambda fn: fn()
    if cond is False:
        return lambda fn: None
    return pl.when(cond)


def _row_tile(m, preferred):
    return max(d for d in range(SUBLANES, min(m, preferred) + 1, SUBLANES) if m % d == 0)


def _group_maps(prompt_tiles):
    in_prompt = lambda i, *_: (jnp.minimum(i, prompt_tiles - 1), 0)
    in_sample = lambda i, *_: (jnp.maximum(i - prompt_tiles, 0), 0)
    return in_prompt, in_sample


def _ln_matmul_kernel(xp_ref, xs_ref, g_ref, b_ref, w_ref, *rest, prompt_tiles, narrow):
    if narrow:
        wn_ref, o_ref, on_ref, h_scr = rest
    else:
        o_ref, h_scr = rest

    @pl.when(pl.program_id(1) == 0)
    def _():
        x = jnp.where(pl.program_id(0) < prompt_tiles, xp_ref[...], xs_ref[...])
        h_scr[...] = _layer_norm(x, g_ref[...], b_ref[...]).astype(BF16)
        if narrow:
            on_ref[...] = _dot(h_scr[...], wn_ref[...])

    o_ref[...] = _dot(h_scr[...], w_ref[...]).astype(o_ref.dtype)


def _ln_matmul(xp, xs, g, b, w, tm, tn, w_narrow=None):
    k = xp.shape[1]
    m = xp.shape[0] + xs.shape[0]
    n = w.shape[1]
    tm = _row_tile(math.gcd(xp.shape[0], xs.shape[0]), tm)
    prompt_tiles = xp.shape[0] // tm
    in_prompt, in_sample = _group_maps(prompt_tiles)
    narrow = w_narrow is not None
    const = lambda i, j: (0, 0)
    row = lambda i, j: (i, 0)
    in_specs = [pl.BlockSpec((tm, k), in_prompt), pl.BlockSpec((tm, k), in_sample),
                pl.BlockSpec((1, k), const), pl.BlockSpec((1, k), const), pl.BlockSpec((k, tn), lambda i, j: (0, j))]
    out_specs = [pl.BlockSpec((tm, tn), lambda i, j: (i, j))]
    out_shape = [jax.ShapeDtypeStruct((m, n), BF16)]
    args = [xp, xs, g, b, w]
    if narrow:
        in_specs.append(pl.BlockSpec((k, LANES), const))
        out_specs.append(pl.BlockSpec((tm, LANES), row))
        out_shape.append(jax.ShapeDtypeStruct((m, LANES), F32))
        args.append(w_narrow)
    out = pl.pallas_call(
        functools.partial(_ln_matmul_kernel, prompt_tiles=prompt_tiles, narrow=narrow),
        grid=(m // tm, n // tn),
        in_specs=in_specs,
        out_specs=out_specs,
        out_shape=out_shape,
        scratch_shapes=[pltpu.VMEM((tm, k), BF16)],
        compiler_params=_params("parallel", "arbitrary"),
        name="ln_matmul",
    )(*args)
    return out if narrow else out[0]


def _load_conv_state(cbuf, state):
    taps = CONV_WIDTH - 1
    cbuf[0:SUBLANES - taps, :] = jnp.zeros((SUBLANES - taps, cbuf.shape[1]), F32)
    cbuf[SUBLANES - taps:SUBLANES, :] = state


def _causal_conv(cbuf, x16, cw, t):
    ch = x16.shape[1]
    prev = cbuf[...]
    taps = CONV_WIDTH - 1
    if t <= SUBLANES:
        x = x16.astype(F32)
        ext = jnp.concatenate([prev, x], axis=0)
        lo = SUBLANES - taps
        y = ext[lo:lo + t] * cw[0:1]
        y = y + ext[lo + 1:lo + 1 + t] * cw[1:2]
        y = y + ext[lo + 2:lo + 2 + t] * cw[2:3]
        y = y + x * cw[3:4]
        cbuf[...] = ext[t:t + SUBLANES]
        return y, ext[SUBLANES + t - taps:SUBLANES + t]
    hi = prev.astype(BF16).astype(F32)
    mid = (prev - hi).astype(BF16).astype(F32)
    low = ((prev - hi) - mid).astype(BF16).astype(F32)
    x = x16.astype(F32)
    pack = 2 * SUBLANES
    nhead = 4 * SUBLANES if t % pack == 0 else 5 * SUBLANES
    head = jnp.concatenate([hi, mid, low, jnp.zeros((nhead - 3 * SUBLANES, ch), F32)], axis=0)
    if t % pack == 0:
        ext = jnp.concatenate([head.astype(BF16), x16], axis=0)
    else:
        ext = jnp.concatenate([head, x], axis=0).astype(BF16)
    out_row = lax.broadcasted_iota(jnp.int32, (taps * t, nhead + t), 0)
    col = lax.broadcasted_iota(jnp.int32, (taps * t, nhead + t), 1)
    log_t = t.bit_length() - 1
    shift = lax.shift_right_logical(out_row, log_t) + 1
    src = (out_row & (t - 1)) - shift
    in_x = (src >= 0) & (col == nhead + src)
    in_prev = (src < 0) & (col < 3 * SUBLANES) & ((col & (SUBLANES - 1)) == SUBLANES + src)
    shifted = _dot((in_x | in_prev).astype(BF16), ext)
    y = shifted[2 * t:3 * t] * cw[0:1]
    y = y + shifted[t:2 * t] * cw[1:2]
    y = y + shifted[0:t] * cw[2:3]
    y = y + x * cw[3:4]
    cbuf[...] = x[t - SUBLANES:t]
    return y, x[t - taps:t]


def _lane_expand(v, h0, count, width):
    t = v.shape[0]
    n = count * width
    out = jnp.broadcast_to(v[:, h0:h0 + 1], (t, n))
    if count > 1:
        lane = lax.broadcasted_iota(jnp.int32, (t, n), 1)
        for i in range(1, count):
            out = jnp.where(lane >= i * width, jnp.broadcast_to(v[:, h0 + i:h0 + i + 1], (t, n)), out)
    return out


def _grouped_transpose(v, per_tile):
    t = v.shape[0]
    blocks = [v if r == 0 else pltpu.roll(v, LANES - r, axis=1) for r in range(per_tile)]
    if per_tile * t < LANES:
        blocks.append(jnp.zeros((LANES - per_tile * t, LANES), F32))
    return jnp.concatenate(blocks, axis=0).T


def _same_block(i, j, size):
    shift = size.bit_length() - 1
    return lax.shift_right_logical(i, shift) == lax.shift_right_logical(j, shift)


def _unit_lower_inverses(lmats, ii, jj, t):
    mm = lambda a, b: _dot(a.astype(BF16), b.astype(BF16))
    base = min(16, t)
    in_base = _same_block(ii, jj, base)
    eye = (ii == jj).astype(F32)
    power = [jnp.where(in_base, l, 0.0) for l in lmats]
    inv = [eye - p for p in power]
    span = 2
    while span < base:
        power = [mm(p, p) for p in power]
        inv = [a + mm(a, p) for a, p in zip(inv, power)]
        span *= 2
    size = base
    while size < t:
        link = _same_block(ii, jj, 2 * size) & jnp.logical_not(_same_block(ii, jj, size))
        cross = [mm(a, jnp.where(link, l, 0.0)) for a, l in zip(inv, lmats)]
        inv = [a - mm(c, a) for a, c in zip(inv, cross)]
        size *= 2
    return inv


def _row_sums_of_squares(blocks):
    t = blocks[0].shape[0]
    sq = jnp.concatenate([b * b for b in blocks], axis=0)
    hi = sq.astype(BF16)
    lo = (sq - hi.astype(F32)).astype(BF16)
    ones = jnp.ones((LANES, LANES), BF16)
    sums = _dot(hi, ones) + _dot(lo, ones)
    return [sums[i * t:(i + 1) * t] for i in range(len(blocks))]


def _cumsum_rows(v, t):
    ii = lax.broadcasted_iota(jnp.int32, (t, t), 0)
    jj = lax.broadcasted_iota(jnp.int32, (t, t), 1)
    return _dot_f32((jj <= ii).astype(F32), v)


def _ssd_kernel(pa_ref, sm_ref, cw_ref, cb_ref, dtb_ref, alog_ref, dskip_ref, nw_ref, cst_ref, h0_ref,
                y_ref, cnew_ref, hnew_ref, cbuf, s_scr, *, t, first, last):
    per_tile = LANES // t
    width = per_tile * SSM_HEAD_DIM

    @_when(first)
    def _():
        _load_conv_state(cbuf, cst_ref[0])
        s_scr[...] = h0_ref[0]

    z = pa_ref[:, :SSM_INNER].astype(F32)
    conv, tail = _causal_conv(cbuf, pa_ref[:, SSM_INNER:], cw_ref[...], t)

    @_when(last)
    def _():
        cnew_ref[0] = tail

    xbc = _silu(conv + cb_ref[...])
    xs = xbc[:, :SSM_INNER]
    bm = xbc[:, SSM_INNER:SSM_INNER + SSM_GROUPS * SSM_STATE]
    cm = xbc[:, SSM_INNER + SSM_GROUPS * SSM_STATE:]

    dt = _softplus(sm_ref[...] + dtb_ref[...])
    da = dt * (-jnp.exp(alog_ref[...]))
    cum = _cumsum_rows(da, t)
    cum_t = _grouped_transpose(cum, per_tile)
    dt_t = _grouped_transpose(dt, per_tile)

    row = lax.broadcasted_iota(jnp.int32, (t, LANES), 0)
    lane = lax.broadcasted_iota(jnp.int32, (t, LANES), 1)
    causal = (lane % t) <= row
    brow = lax.broadcasted_iota(jnp.int32, (LANES, width), 0)
    bcol = lax.broadcasted_iota(jnp.int32, (LANES, width), 1)
    blockdiag = (brow // t) == (bcol // SSM_HEAD_DIM)

    ys = []
    for g in range(SSM_GROUPS):
        bg = bm[:, g * SSM_STATE:(g + 1) * SSM_STATE]
        cg = cm[:, g * SSM_STATE:(g + 1) * SSM_STATE].astype(BF16)
        cb = _dot_nt(cg, jnp.concatenate([bg] * per_tile, axis=0).astype(BF16))
        bg = bg.astype(BF16)
        for q in range(SSM_HEADS_PER_GROUP // per_tile):
            h0 = g * SSM_HEADS_PER_GROUP + q * per_tile
            ccol = _lane_expand(cum, h0, per_tile, t)
            seg = ccol - cum_t[h0:h0 + 1, :]
            decay = jnp.exp(jnp.where(causal, seg, -jnp.inf))
            wts = (cb * decay * dt_t[h0:h0 + 1, :]).astype(BF16)
            xt = xs[:, h0 * SSM_HEAD_DIM:h0 * SSM_HEAD_DIM + width]
            xbd = jnp.where(blockdiag, jnp.concatenate([xt] * per_tile, axis=0), 0.0).astype(BF16)
            y_intra = _dot(wts, xbd)
            st = s_scr[h0:h0 + per_tile].reshape(width, SSM_STATE)
            cum_w = ccol if width == LANES else _lane_expand(cum, h0, per_tile, SSM_HEAD_DIM)
            y_state = _dot_nt(cg, st.astype(BF16)) * jnp.exp(cum_w)
            ys.append(y_intra + y_state)
            wend = jnp.exp(cum_w[t - 1:t, :] - cum_w) * _lane_expand(dt, h0, per_tile, SSM_HEAD_DIM)
            ds = _dot_tn((xt * wend).astype(BF16), bg)
            for i in range(per_tile):
                h = h0 + i
                s_scr[h] = jnp.exp(cum[t - 1:t, h:h + 1]) * s_scr[h] + ds[i * SSM_HEAD_DIM:(i + 1) * SSM_HEAD_DIM]

    y = jnp.concatenate(ys, axis=1) + dskip_ref[...] * xs
    y = y * _silu(z)
    half = SSM_INNER // SSM_GROUPS
    normed = []
    for g in range(SSM_GROUPS):
        yg = y[:, g * half:(g + 1) * half]
        normed.append(yg * lax.rsqrt(jnp.mean(yg * yg, axis=-1, keepdims=True) + RMS_EPS))
    y_ref[...] = (jnp.concatenate(normed, axis=1) * nw_ref[...]).astype(BF16)

    @_when(last)
    def _():
        hnew_ref[0] = s_scr[...]


def _scan_step_kernel(chunk_kernel, n_tok, n_const, n_state, seqs, chunks, t):
    def step(*refs):
        tok = refs[:n_tok]
        const = refs[n_tok:n_tok + n_const]
        st_in = refs[n_tok + n_const:n_tok + n_const + n_state]
        y_ref = refs[n_tok + n_const + n_state]
        st_out = refs[n_tok + n_const + n_state + 1:n_tok + n_const + 2 * n_state + 1]
        scratch = refs[n_tok + n_const + 2 * n_state + 1:]
        first_step = pl.program_id(1) == 0
        last_step = pl.program_id(1) == pl.num_programs(1) - 1
        for s in range(seqs):
            for c in range(chunks):
                rows = pl.ds((s * chunks + c) * t, t)
                chunk_kernel(*[r.at[rows, :] for r in tok], *const, *[r.at[pl.ds(s, 1)] for r in st_in],
                             y_ref.at[rows, :], *[r.at[pl.ds(s, 1)] for r in st_out], *scratch,
                             first=first_step if c == 0 else False,
                             last=last_step if c == chunks - 1 else False)
    return step


def _scan_call(kernel, name, tok_in, const_in, state_in, y_width, scratch, *, batch, seq, row0, t, seqs, chunks,
               step_kernel=None):
    nstep = seq // (t * chunks)
    rows = seqs * chunks * t
    blk0 = row0 // rows

    def per_batch(a):
        zeros = (0,) * (a.ndim - 1)
        return pl.BlockSpec((seqs,) + a.shape[1:], lambda b, c: (b,) + zeros)

    if step_kernel is None:
        step_kernel = _scan_step_kernel(kernel, len(tok_in), len(const_in), len(state_in), seqs, chunks, t)
    return pl.pallas_call(
        step_kernel,
        grid=(batch // seqs, nstep),
        in_specs=[pl.BlockSpec((rows, a.shape[1]), lambda b, c: (blk0 + b * nstep + c, 0)) for a in tok_in]
        + [pl.BlockSpec(a.shape, lambda b, c: (0, 0)) for a in const_in]
        + [per_batch(a) for a in state_in],
        out_specs=[pl.BlockSpec((rows, y_width), lambda b, c: (b * nstep + c, 0))]
        + [per_batch(a) for a in state_in],
        out_shape=[jax.ShapeDtypeStruct((batch * seq, y_width), BF16)]
        + [jax.ShapeDtypeStruct(a.shape, F32) for a in state_in],
        scratch_shapes=scratch,
        compiler_params=_params("parallel", "arbitrary"),
        name=name,
    )(*tok_in, *const_in, *state_in)


def _ssd(proj_a, small, consts, conv_state, h0, **where):
    t = where["t"]
    scratch = [pltpu.VMEM((SUBLANES, SSM_CONV_CH), F32),
               pltpu.VMEM((SSM_HEADS, SSM_HEAD_DIM, SSM_STATE), F32)]
    return _scan_call(functools.partial(_ssd_kernel, t=t), f"ssd_t{t}", [proj_a, small], consts,
                      [conv_state, h0], SSM_INNER, scratch, **where)


def _gdn_step_kernel(pb_ref, sm_ref, cw_ref, alog_ref, dtb_ref, nw_ref, cst_ref, s0_ref,
                     o_ref, cnew_ref, snew_ref, cbuf, s_scr, *, t, seqs, chunks):
    first_step = pl.program_id(1) == 0
    last_step = pl.program_id(1) == pl.num_programs(1) - 1
    units = [(s, c) for s in range(seqs) for c in range(chunks)]
    rows = lambda u: pl.ds(u * t, t)
    per_tile = min(LANES // t, DN_HEADS)
    n = per_tile * t
    ntile = DN_HEADS // per_tile
    heads = range(DN_HEADS)
    ii = lax.broadcasted_iota(jnp.int32, (n, n), 0)
    jj = lax.broadcasted_iota(jnp.int32, (n, n), 1)
    causal = _same_block(ii, jj, t) & (jj <= ii)
    diag = ii == jj
    stack = lambda xs, p: jnp.concatenate(xs[p * per_tile:(p + 1) * per_tile], axis=0)

    qkv, zb, gcum, beta = [], [], [], []
    for u, (s, c) in enumerate(units):
        @_when(first_step if c == 0 else False)
        def _():
            _load_conv_state(cbuf, cst_ref[s])

        conv, tail = _causal_conv(cbuf, pb_ref[rows(u), :DN_CONV_CH], cw_ref[...], t)

        @_when(last_step if c == chunks - 1 else False)
        def _():
            cnew_ref[s] = tail

        qkv.append(_silu(conv))
        zb.append(pb_ref[rows(u), DN_CONV_CH:].astype(F32))
        sm = sm_ref[rows(u), :]
        beta.append(_sigmoid(sm))
        gate = -jnp.exp(alog_ref[...]) * _softplus(sm + dtb_ref[...])
        gcum.append(_cumsum_rows(gate, t))
    gcum_t = [_grouped_transpose(g, per_tile) for g in gcum]

    sumsq = _row_sums_of_squares([x[:, i * DN_HEAD:(i + 1) * DN_HEAD] for x in qkv for i in range(2 * DN_HEADS)])
    q, k, v, gcol, bcol, kb, egcol = ([[] for _ in units] for _ in range(7))
    for u in range(len(units)):
        for h in heads:
            lo = h * DN_HEAD
            ss = sumsq[u * 2 * DN_HEADS:(u + 1) * 2 * DN_HEADS]
            q[u].append(qkv[u][:, lo:lo + DN_HEAD] * lax.rsqrt(ss[h] + L2_EPS) * (DN_HEAD ** -0.5))
            k[u].append(qkv[u][:, DN_QK + lo:DN_QK + lo + DN_HEAD] * lax.rsqrt(ss[DN_HEADS + h] + L2_EPS))
            v[u].append(qkv[u][:, 2 * DN_QK + lo:2 * DN_QK + lo + DN_HEAD])
            gcol[u].append(gcum[u][:, SMALL_A + h:SMALL_A + h + 1])
            bcol[u].append(beta[u][:, SMALL_B + h:SMALL_B + h + 1])
            kb[u].append(k[u][h] * bcol[u][h])
            egcol[u].append(jnp.exp(gcol[u][h]))

    tiles = [(u, p) for u in range(len(units)) for p in range(ntile)]
    k16 = [stack(k[u], p).astype(BF16) for u, p in tiles]
    decay = []
    for u, p in tiles:
        h0 = p * per_tile
        seg = stack(gcol[u], p) - gcum_t[u][SMALL_A + h0:SMALL_A + h0 + 1, :n]
        decay.append(jnp.exp(jnp.where(causal, seg, -jnp.inf)))
    lmat = [_dot_nt(stack(kb[u], p).astype(BF16), k16[i]) * jnp.where(diag, 0.0, decay[i])
            for i, (u, p) in enumerate(tiles)]
    qk = [(_dot_nt(stack(q[u], p).astype(BF16), k16[i]) * decay[i]).astype(BF16) for i, (u, p) in enumerate(tiles)]
    inv = [a.astype(BF16) for a in _unit_lower_inverses(lmat, ii, jj, t)]

    split = lambda x: (x[:, :DN_HEAD], x[:, DN_HEAD:])
    o_from, o_own, s_from, s_own, glast = ([[] for _ in units] for _ in range(5))
    aw = [_dot(inv[i], jnp.concatenate(
        [stack([kb[u][h] * egcol[u][h] for h in heads], p), stack([v[u][h] * bcol[u][h] for h in heads], p)],
        axis=1).astype(BF16)) for i, (u, p) in enumerate(tiles)]
    qaw = [_dot(qk[i], aw[i].astype(BF16)) for i in range(len(tiles))]
    pad_rows = -t % (2 * SUBLANES)
    pad = [jnp.zeros((pad_rows, DN_HEAD), F32)] if pad_rows else []
    for u in range(len(units)):
        for h in heads:
            p, j = divmod(h, per_tile)
            i = u * ntile + p
            sel = slice(j * t, (j + 1) * t)
            qkw, qku0 = split(qaw[i][sel])
            o_from[u].append(q[u][h] * egcol[u][h] - qkw)
            o_own[u].append(qku0)
            glast[u].append(gcum[u][t - 1:t, SMALL_A + h:SMALL_A + h + 1])
            kd = (k[u][h] * jnp.exp(glast[u][h] - gcol[u][h])).astype(BF16)
            kdw, kdu0 = split(_dot_tn(kd, aw[i][sel].astype(BF16)))
            s_from[u].append(kdw)
            s_own[u].append(kdu0)

    outs = []
    for u, (s, c) in enumerate(units):
        @_when(first_step if c == 0 else False)
        def _():
            s_scr[...] = s0_ref[s]

        for h in heads:
            state = s_scr[h]
            both = _dot(jnp.concatenate([s_from[u][h], o_from[u][h]] + pad, axis=0).astype(BF16), state.astype(BF16))
            outs.append(both[DN_HEAD:DN_HEAD + t] + o_own[u][h])
            s_scr[h] = jnp.exp(glast[u][h]) * state - both[:DN_HEAD] + s_own[u][h]

        @_when(last_step if c == chunks - 1 else False)
        def _():
            snew_ref[s] = s_scr[...]

    norm_w = nw_ref[...]
    osq = _row_sums_of_squares(outs)
    for u in range(len(units)):
        normed = []
        for h in heads:
            i = u * DN_HEADS + h
            normed.append(outs[i] * lax.rsqrt(osq[i] * (1.0 / DN_HEAD) + RMS_EPS) * norm_w
                          * _silu(zb[u][:, h * DN_HEAD:(h + 1) * DN_HEAD]))
        o_ref[rows(u), :] = jnp.concatenate(normed, axis=1).astype(BF16)


def _gdn(proj_b, small, consts, conv_state, s0, **where):
    t = where["t"]
    scratch = [pltpu.VMEM((SUBLANES, DN_CONV_CH), F32),
               pltpu.VMEM((DN_HEADS, DN_HEAD, DN_HEAD), F32)]
    step = functools.partial(_gdn_step_kernel, t=t, seqs=where["seqs"], chunks=where["chunks"])
    return _scan_call(None, f"gdn_t{t}", [proj_b, small], consts, [conv_state, s0], DN_V, scratch,
                      step_kernel=step, **where)


def _merge_kernel(xp_ref, xs_ref, yap_ref, yas_ref, obp_ref, obs_ref, eg_ref, eb_ref, wg_ref, wa_ref, wb_ref,
                  wo_ref, g_ref, b_ref, x1_ref, x1b_ref, *, prompt_tiles):
    is_prompt = pl.program_id(0) < prompt_tiles
    h = _layer_norm(jnp.where(is_prompt, xp_ref[...], xs_ref[...]), eg_ref[...], eb_ref[...])
    out_a = _dot(jnp.where(is_prompt, yap_ref[...], yas_ref[...]), wa_ref[...])
    out_b = _dot(jnp.where(is_prompt, obp_ref[...], obs_ref[...]), wb_ref[...])
    gt = _dot(h.astype(BF16), wg_ref[...])
    merged = _sigmoid(gt[:, :D_MODEL]) * out_a + _sigmoid(gt[:, D_MODEL:]) * out_b
    mix = _dot(merged.astype(BF16), wo_ref[...])
    x1 = _layer_norm(DEEPNORM_ALPHA * h + mix, g_ref[...], b_ref[...])
    x1_ref[...] = x1
    x1b_ref[...] = x1.astype(BF16)


def _merge(xp, xs, ya_p, ya_s, ob_p, ob_s, consts, tm):
    m = xp.shape[0] + xs.shape[0]
    n_p = xp.shape[0]
    tm = _row_tile(math.gcd(n_p, m - n_p), tm)
    in_prompt, in_sample = _group_maps(n_p // tm)
    row = lambda i: (i, 0)
    const = lambda i: (0, 0)
    return pl.pallas_call(
        functools.partial(_merge_kernel, prompt_tiles=n_p // tm),
        grid=(m // tm,),
        in_specs=[pl.BlockSpec((tm, D_MODEL), in_prompt), pl.BlockSpec((tm, D_MODEL), in_sample),
                  pl.BlockSpec((tm, SSM_INNER), in_prompt), pl.BlockSpec((tm, SSM_INNER), in_sample),
                  pl.BlockSpec((tm, DN_V), in_prompt), pl.BlockSpec((tm, DN_V), in_sample)]
        + [pl.BlockSpec(a.shape, const) for a in consts],
        out_specs=[pl.BlockSpec((tm, D_MODEL), row), pl.BlockSpec((tm, D_MODEL), row)],
        out_shape=[jax.ShapeDtypeStruct((m, D_MODEL), F32), jax.ShapeDtypeStruct((m, D_MODEL), BF16)],
        compiler_params=_params("parallel"),
        name="merge",
    )(xp, xs, ya_p, ya_s, ob_p, ob_s, *consts)


def _take_first_max(rest, index, limit, axis):
    best = jnp.max(rest, axis=axis, keepdims=True)
    first = jnp.min(jnp.where(rest == best, index, limit), axis=axis, keepdims=True)
    hit = index == first
    return hit, jnp.where(hit, -jnp.inf, rest)


def _router_gates_t(x1, xb, rw_hi, rw_lo, bias):
    tm = x1.shape[0]
    x_lo = (x1 - xb.astype(F32)).astype(BF16)
    logits = _dot_nt(rw_hi, xb) + (_dot_nt(rw_hi, x_lo) + _dot_nt(rw_lo, xb))
    scores = _sigmoid(logits)
    row = lax.broadcasted_iota(jnp.int32, (LANES, tm), 0)
    sel = jnp.where(row < N_EXPERTS, scores + bias, -jnp.inf)
    by_group = sel.reshape(LANES // EXPERTS_PER_GROUP, EXPERTS_PER_GROUP, tm)
    sub = lax.broadcasted_iota(jnp.int32, by_group.shape, 1)
    top1 = jnp.max(by_group, axis=1, keepdims=True)
    _, others = _take_first_max(by_group, sub, EXPERTS_PER_GROUP, 1)
    gscore = (top1 + jnp.max(others, axis=1, keepdims=True))[:N_EXPERT_GROUPS]
    gidx = lax.broadcasted_iota(jnp.int32, gscore.shape, 0)
    gkeep = jnp.zeros(gscore.shape, jnp.bool_)
    for _ in range(TOPK_GROUPS):
        hit, gscore = _take_first_max(gscore, gidx, N_EXPERT_GROUPS, 0)
        gkeep = gkeep | hit
    gkeep = jnp.broadcast_to(gkeep, (N_EXPERT_GROUPS, EXPERTS_PER_GROUP, tm)).reshape(N_EXPERTS, tm)
    rest = jnp.where(gkeep, sel[:N_EXPERTS], -jnp.inf)
    eidx = lax.broadcasted_iota(jnp.int32, (N_EXPERTS, tm), 0)
    keep = jnp.zeros((N_EXPERTS, tm), jnp.bool_)
    for _ in range(TOP_K):
        hit, rest = _take_first_max(rest, eidx, N_EXPERTS, 0)
        keep = keep | hit
    picked = jnp.where(keep, scores[:N_EXPERTS], 0.0)
    gates = picked / jnp.sum(picked, axis=0, keepdims=True) * ROUTED_SCALE
    return jnp.concatenate([gates, jnp.zeros((LANES - N_EXPERTS, tm), F32)], axis=0)


def _dense_router_kernel(x1_ref, x1b_ref, pp_ref, ps_ref, sw1_ref, sw3_ref, sw2_ref, pw_ref, pg_ref, rwh_ref,
                         rwl_ref, rb_ref, dense_ref, gates_ref, rank_ref, chosen_t_ref, rank_t_ref, count_ref,
                         *, prompt_tiles):
    xb = x1b_ref[...]
    hid = _silu(_dot(xb, sw1_ref[...])) * _dot(xb, sw3_ref[...])
    shared = _dot(hid.astype(BF16), sw2_ref[...])
    p = jnp.where(pl.program_id(0) < prompt_tiles, pp_ref[...], ps_ref[...])
    ple = _dot(p.astype(BF16), pw_ref[...]) * _sigmoid(_dot(xb, pg_ref[...]))
    x1 = x1_ref[...]
    dense_ref[...] = DEEPNORM_ALPHA * x1 + shared + ple
    ii = lax.broadcasted_iota(jnp.int32, (MOE_BLOCK, MOE_BLOCK), 0)
    jj = lax.broadcasted_iota(jnp.int32, (MOE_BLOCK, MOE_BLOCK), 1)
    earlier = (ii < jj).astype(BF16)
    for blk in range(x1.shape[0] // MOE_BLOCK):
        tok = slice(blk * MOE_BLOCK, (blk + 1) * MOE_BLOCK)
        exp = slice(blk * LANES, (blk + 1) * LANES)
        gates_t = _router_gates_t(x1[tok], xb[tok], rwh_ref[...], rwl_ref[...], rb_ref[...])
        chosen_t = (gates_t > 0.0).astype(BF16)
        rank_t = _dot(chosen_t, earlier)
        chosen_t_ref[exp, :] = chosen_t
        rank_t_ref[exp, :] = rank_t.astype(BF16)
        gates = gates_t.T
        rank = rank_t.T
        gates_ref[tok, :] = gates.astype(BF16)
        rank_ref[tok, :] = rank.astype(BF16)
        count = rank[MOE_BLOCK - 1:, :] + (gates[MOE_BLOCK - 1:, :] > 0.0).astype(F32)
        count_ref[blk] = jnp.broadcast_to(count, (SUBLANES, LANES))


def _dense_router(x1, x1b, p_prompt, p_sample, consts):
    m = x1.shape[0]
    tm = _row_tile(math.gcd(p_prompt.shape[0], p_sample.shape[0]), ROUTER_TM)
    assert tm % MOE_BLOCK == 0
    nblk = m // MOE_BLOCK
    per_step = tm // MOE_BLOCK
    prompt_tiles = p_prompt.shape[0] // tm
    in_prompt, in_sample = _group_maps(prompt_tiles)
    row = lambda i: (i, 0)
    const = lambda i: (0, 0)
    return pl.pallas_call(
        functools.partial(_dense_router_kernel, prompt_tiles=prompt_tiles),
        grid=(m // tm,),
        in_specs=[pl.BlockSpec((tm, D_MODEL), row), pl.BlockSpec((tm, D_MODEL), row),
                  pl.BlockSpec((tm, PLE_DIM), in_prompt), pl.BlockSpec((tm, PLE_DIM), in_sample)]
        + [pl.BlockSpec(a.shape, const) for a in consts],
        out_specs=[pl.BlockSpec((tm, D_MODEL), row), pl.BlockSpec((tm, LANES), row),
                   pl.BlockSpec((tm, LANES), row), pl.BlockSpec((per_step * LANES, MOE_BLOCK), row),
                   pl.BlockSpec((per_step * LANES, MOE_BLOCK), row),
                   pl.BlockSpec((per_step, SUBLANES, LANES), lambda i: (i, 0, 0))],
        out_shape=[jax.ShapeDtypeStruct((m, D_MODEL), F32), jax.ShapeDtypeStruct((m, LANES), BF16),
                   jax.ShapeDtypeStruct((m, LANES), BF16), jax.ShapeDtypeStruct((nblk * LANES, MOE_BLOCK), BF16),
                   jax.ShapeDtypeStruct((nblk * LANES, MOE_BLOCK), BF16),
                   jax.ShapeDtypeStruct((nblk, SUBLANES, LANES), F32)],
        compiler_params=_params("parallel"),
        name="dense_router",
    )(x1, x1b, p_prompt, p_sample, *consts)


def _moe_tables(counts, tm):
    nblk = counts.shape[0]
    nchunk_max = MOE_CAP // MOE_SEG
    padded = (counts + MOE_SEG - 1) // MOE_SEG * MOE_SEG
    start = jnp.cumsum(padded, axis=1) - padded
    used = jnp.sum(padded, axis=1)
    rows_e = jnp.sum(padded, axis=0)
    rows_e_t = (rows_e + tm - 1) // tm * tm
    base_e = jnp.cumsum(rows_e_t) - rows_e_t
    seg_row = base_e[None, :] + jnp.cumsum(padded, axis=0) - padded
    chunk0 = jnp.arange(nchunk_max, dtype=jnp.int32) * MOE_SEG
    owner = jnp.sum((start + padded)[:, None, :] <= chunk0[None, :, None], axis=-1)
    onehot = owner[..., None] == jnp.arange(N_EXPERTS, dtype=jnp.int32)
    dst = chunk0[None, :] + jnp.sum(jnp.where(onehot, (seg_row - start)[:, None, :], 0), axis=-1)
    n_tiles = _moe_max_tiles(nblk, tm)
    live = chunk0[None, :] < used[:, None]
    parity = (jnp.arange(nblk, dtype=jnp.int32) % 2)[:, None]
    spare = n_tiles * tm + parity * MOE_CAP + chunk0[None, :]
    src = jnp.where(live, dst, 0)
    dst = jnp.where(live, dst, spare)
    tile_row = jnp.arange(n_tiles, dtype=jnp.int32) * tm
    tiles_used = jnp.sum(rows_e_t) // tm
    tile_row = jnp.minimum(tile_row, (tiles_used - 1) * tm)
    tile_expert = jnp.minimum(jnp.sum((base_e + rows_e_t)[None, :] <= tile_row[:, None], axis=-1), N_EXPERTS - 1)
    in_use = rows_e > 0
    experts = jnp.arange(N_EXPERTS, dtype=jnp.int32)
    later = jnp.where(in_use[None, :] & (experts[None, :] > experts[:, None]), experts[None, :], N_EXPERTS)
    next_in_use = jnp.min(later, axis=1)
    next_in_use = jnp.where(next_in_use == N_EXPERTS, -1, next_in_use)
    order = jnp.sum(in_use[None, :] & (experts[None, :] < experts[:, None]), axis=1)
    of_tile = tile_expert[:, None] == experts[None, :]
    tile_slot = jnp.sum(jnp.where(of_tile, (order % 2)[None, :], 0), axis=1)
    tile_next = jnp.sum(jnp.where(of_tile, next_in_use[None, :], 0), axis=1)
    bounds = jnp.zeros((nblk, SUBLANES, LANES), F32)
    bounds = bounds.at[:, 0, :N_EXPERTS].set(start.astype(F32)).at[:, 1, :N_EXPERTS].set(padded.astype(F32))
    i32 = lambda a: a.astype(jnp.int32)
    return dict(dst=i32(dst.reshape(-1)), src=i32(src.reshape(-1)), nchunk=i32(used // MOE_SEG),
                pad_row=i32(base_e + rows_e),
                pad_n=i32((rows_e_t - rows_e) // MOE_SEG), tile_expert=i32(tile_expert),
                tile_slot=i32(tile_slot), tile_next=i32(tile_next),
                tiles_used=i32(tiles_used.reshape(1)), bounds=bounds)


def _moe_max_tiles(nblk, tm):
    return (nblk * (MOE_BLOCK * TOP_K + N_EXPERTS * (MOE_SEG - 1)) + N_EXPERTS * (tm - MOE_SEG) + tm - 1) // tm


def _start_chunk(copy, q):
    copy.start(priority=q % 2)


def _wait_chunk(copy, q):
    copy.wait()


def _slot_bounds(bounds):
    return bounds[0:1, :], bounds[1:2, :]


def _moe_gather_kernel(dst_ref, nchunk_ref, pad_row_ref, pad_n_ref,
                       x_ref, chosen_t_ref, rank_t_ref, bounds_ref, rows_ref, buf, zbuf, sem, zsem):
    j = pl.program_id(0)
    nblk = pl.num_programs(0)
    slot = j % 2
    nchunk_max = MOE_CAP // MOE_SEG

    def chunk_copy(blk, q, sl):
        src = buf.at[sl, pl.ds(pl.multiple_of(q * MOE_SEG, MOE_SEG), MOE_SEG), :]
        row = pl.multiple_of(dst_ref[blk * nchunk_max + q], MOE_SEG)
        return pltpu.make_async_copy(src, rows_ref.at[pl.ds(row, MOE_SEG), :], sem.at[sl])

    def pad_copy(e, i):
        row = pl.multiple_of(pad_row_ref[e] + i * MOE_SEG, MOE_SEG)
        return pltpu.make_async_copy(zbuf, rows_ref.at[pl.ds(row, MOE_SEG), :], zsem)

    def sub_chunks(blk, sub, sl, action):
        for q in range(sub * (MOE_SUB // MOE_SEG), (sub + 1) * (MOE_SUB // MOE_SEG)):
            action(chunk_copy(blk, q, sl), q)

    def sub_used(blk, sub):
        return True if sub < MOE_SUB_ALWAYS else sub * MOE_SUB < nchunk_ref[blk] * MOE_SEG

    def for_pads(action):
        def per_expert(e, carry):
            def body(i, c):
                action(pad_copy(e, i))
                return c
            return lax.fori_loop(0, pad_n_ref[e], body, carry)
        lax.fori_loop(0, N_EXPERTS, per_expert, 0)

    @pl.when(j == 0)
    def _():
        zbuf[...] = jnp.zeros_like(zbuf)
        for_pads(lambda c: c.start())

    start, length = _slot_bounds(bounds_ref[0])
    x = x_ref[...]
    chosen_rank_t = jnp.concatenate([chosen_t_ref[...], rank_t_ref[...]], axis=1)
    before = jnp.maximum(j - 2, 0)
    subs = range(MOE_CAP // MOE_SUB)

    @pl.when(j >= 2)
    def _():
        for sub in subs:
            @_when(sub_used(before, sub))
            def _():
                sub_chunks(before, sub, slot, _wait_chunk)

    def owners(sub):
        s = (sub * MOE_SUB + lax.broadcasted_iota(jnp.int32, (MOE_SUB, LANES), 0)).astype(F32)
        owner = (s >= start) & (s < start + length)
        within = s[:, 0:1] - jnp.sum(jnp.where(owner, start, 0.0), axis=1, keepdims=True)
        return owner.astype(BF16), within

    def picks(owner_within):
        owner, within = owner_within
        hit = _dot(owner, chosen_rank_t)
        return ((hit[:, :MOE_BLOCK] > 0.5) & (hit[:, MOE_BLOCK:] == within)).astype(BF16)

    def emit(sub, pick):
        buf[slot, sub * MOE_SUB:(sub + 1) * MOE_SUB, :] = _dot(pick, x).astype(BF16)
        sub_chunks(j, sub, slot, _start_chunk)

    staged = [picks(ow) for ow in [owners(sub) for sub in subs[:MOE_SUB_ALWAYS]]]
    for sub, pick in enumerate(staged):
        emit(sub, pick)
    for sub in subs[MOE_SUB_ALWAYS:]:
        @pl.when(sub_used(j, sub))
        def _():
            emit(sub, picks(owners(sub)))

    @pl.when(j == nblk - 1)
    def _():
        prev = jnp.maximum(j - 1, 0)
        for sub in subs:
            @_when(sub_used(j, sub))
            def _():
                sub_chunks(j, sub, slot, _wait_chunk)

            @_when((j >= 1) & sub_used(prev, sub))
            def _():
                sub_chunks(prev, sub, 1 - slot, _wait_chunk)

        for_pads(lambda c: c.wait())


def _moe_gather(x1b, chosen_t, rank_t, tables, tm):
    m = x1b.shape[0]
    nblk = m // MOE_BLOCK
    rows = _moe_max_tiles(nblk, tm) * tm + 2 * MOE_CAP
    blk = lambda j, *_: (j, 0)
    grid_spec = pltpu.PrefetchScalarGridSpec(
        num_scalar_prefetch=4,
        grid=(nblk,),
        in_specs=[pl.BlockSpec((MOE_BLOCK, D_MODEL), blk), pl.BlockSpec((LANES, MOE_BLOCK), blk),
                  pl.BlockSpec((LANES, MOE_BLOCK), blk),
                  pl.BlockSpec((1, SUBLANES, LANES), lambda j, *_: (j, 0, 0))],
        out_specs=pl.BlockSpec(memory_space=pl.ANY),
        scratch_shapes=[pltpu.VMEM((2, MOE_CAP, D_MODEL), BF16), pltpu.VMEM((MOE_SEG, D_MODEL), BF16),
                        pltpu.SemaphoreType.DMA((2,)), pltpu.SemaphoreType.DMA(())],
    )
    return pl.pallas_call(
        _moe_gather_kernel,
        grid_spec=grid_spec,
        out_shape=jax.ShapeDtypeStruct((rows, D_MODEL), BF16),
        compiler_params=_params("arbitrary"),
        name="moe_gather",
    )(tables["dst"], tables["nchunk"], tables["pad_row"], tables["pad_n"], x1b, chosen_t, rank_t, tables["bounds"])


def _moe_ffn_kernel(tile_expert_ref, tile_slot_ref, tile_next_ref, x_hbm, w1_hbm, w3_hbm, w2_hbm, y_ref,
                    x_buf, w1_buf, w3_buf, w2_buf, w13_scr, w2_scr, sem, x_sem):
    i = pl.program_id(0)
    n = pl.num_programs(0)
    expert = tile_expert_ref[i]
    slot = tile_slot_ref[i]
    ahead = MOE_ROW_SLOTS - 1
    tm = x_buf.shape[1]

    def row_copy(t):
        rows = pl.ds(pl.multiple_of(t * tm, tm), tm)
        return pltpu.make_async_copy(x_hbm.at[rows], x_buf.at[t % MOE_ROW_SLOTS], x_sem.at[t % MOE_ROW_SLOTS])

    @pl.when(i == 0)
    def _():
        for t in range(ahead):
            @pl.when(t < n)
            def _():
                row_copy(jnp.int32(t)).start()

    @pl.when(i + ahead < n)
    def _():
        row_copy(i + ahead).start()

    def weight_copies(e, s):
        pairs = ((w1_hbm, w1_buf), (w3_hbm, w3_buf), (w2_hbm, w2_buf))
        return [pltpu.make_async_copy(w.at[e], buf.at[s], sem.at[s, n]) for n, (w, buf) in enumerate(pairs)]

    @pl.when(i == 0)
    def _():
        for copy in weight_copies(expert, slot):
            copy.start()

    @pl.when((i == 0) | (expert != tile_expert_ref[jnp.maximum(i - 1, 0)]))
    def _():
        for copy in weight_copies(expert, slot):
            copy.wait()

        @pl.when(tile_next_ref[i] >= 0)
        def _():
            for copy in weight_copies(tile_next_ref[i], 1 - slot):
                copy.start()

        w13_scr[:, :EXPERT_FF] = w1_buf[slot].astype(BF16)
        w13_scr[:, EXPERT_FF:] = w3_buf[slot].astype(BF16)
        w2_scr[...] = w2_buf[slot].astype(BF16)

    row_copy(i).wait()
    up = _dot(x_buf[i % MOE_ROW_SLOTS], w13_scr[...])
    hid = _silu(up[:, :EXPERT_FF]) * up[:, EXPERT_FF:]
    y_ref[...] = _dot(hid.astype(BF16), w2_scr[...]).astype(BF16)


def _moe_ffn(rows, w1, w3, w2, tables, tm):
    grid_spec = pltpu.PrefetchScalarGridSpec(
        num_scalar_prefetch=3,
        grid=(tables["tiles_used"][0],),
        in_specs=[pl.BlockSpec(memory_space=pl.ANY)] * 4,
        out_specs=pl.BlockSpec((tm, D_MODEL), lambda i, *_: (i, 0)),
        scratch_shapes=[pltpu.VMEM((MOE_ROW_SLOTS, tm, D_MODEL), BF16),
                        pltpu.VMEM((2, D_MODEL, EXPERT_FF), F32), pltpu.VMEM((2, D_MODEL, EXPERT_FF), F32),
                        pltpu.VMEM((2, EXPERT_FF, D_MODEL), F32),
                        pltpu.VMEM((D_MODEL, 2 * EXPERT_FF), BF16), pltpu.VMEM((EXPERT_FF, D_MODEL), BF16),
                        pltpu.SemaphoreType.DMA((2, 3)), pltpu.SemaphoreType.DMA((MOE_ROW_SLOTS,))],
    )
    return pl.pallas_call(
        _moe_ffn_kernel,
        grid_spec=grid_spec,
        out_shape=jax.ShapeDtypeStruct(rows.shape, BF16),
        compiler_params=_params("arbitrary"),
        name="moe_ffn",
    )(tables["tile_expert"], tables["tile_slot"], tables["tile_next"], rows, w1, w3, w2)


def _moe_combine_kernel(dst_ref, nchunk_ref, dense_ref, gates_ref, rank_ref, bounds_ref, g_ref, b_ref, rows_ref,
                        op_ref, os_ref, acc, buf, sem, *, prompt_tiles):
    j = pl.program_id(0)
    nblk = pl.num_programs(0)
    slot = j % 2
    nchunk_max = MOE_CAP // MOE_SEG

    def chunk_copy(blk, q, sl):
        row = pl.multiple_of(dst_ref[blk * nchunk_max + q], MOE_SEG)
        dst = buf.at[sl, pl.ds(pl.multiple_of(q * MOE_SEG, MOE_SEG), MOE_SEG), :]
        return pltpu.make_async_copy(rows_ref.at[pl.ds(row, MOE_SEG), :], dst, sem.at[sl])

    subs = range(MOE_CAP // MOE_SUB)
    always, rest = subs[:MOE_SUB_ALWAYS], subs[MOE_SUB_ALWAYS:]

    def sub_used(blk, sub):
        return True if sub < MOE_SUB_ALWAYS else sub * MOE_SUB < nchunk_ref[blk] * MOE_SEG

    def group(blk, sub, sl, action):
        for q in range(sub * (MOE_SUB // MOE_SEG), (sub + 1) * (MOE_SUB // MOE_SEG)):
            action(chunk_copy(blk, q, sl), q)

    @pl.when(j == 0)
    def _():
        buf[...] = jnp.zeros_like(buf)
        for sub in subs:
            @_when(sub_used(0, sub))
            def _():
                group(0, sub, 0, _start_chunk)

    for sub in subs:
        @_when(sub_used(j, sub))
        def _():
            group(j, sub, slot, _wait_chunk)

    bounds_t = jnp.concatenate([bounds_ref[0], jnp.zeros((LANES - SUBLANES, LANES), F32)], axis=0).T
    start, length = bounds_t[:, 0:1], bounds_t[:, 1:2]
    gates_rank = jnp.concatenate([gates_ref[...], rank_ref[...]], axis=0)

    def owners(sub):
        s = (sub * MOE_SUB + lax.broadcasted_iota(jnp.int32, (LANES, MOE_SUB), 1)).astype(F32)
        owner = (s >= start) & (s < start + length)
        within = s[0:1, :] - jnp.sum(jnp.where(owner, start, 0.0), axis=0, keepdims=True)
        return owner.astype(BF16), within

    def weights(owner_within):
        owner, within = owner_within
        hit = _dot(gates_rank, owner)
        return jnp.where(hit[MOE_BLOCK:] == within, hit[:MOE_BLOCK], 0.0).astype(BF16)

    def apply(sub, weight):
        return _dot(weight, buf[slot, sub * MOE_SUB:(sub + 1) * MOE_SUB, :])

    nxt = jnp.minimum(j + 1, nblk - 1)
    staged = [owners(sub) for sub in always]
    for sub in always:
        group(nxt, sub, 1 - slot, _start_chunk)
    staged = [weights(ow) for ow in staged]
    total = dense_ref[...]
    for sub, weight in zip(always, staged):
        total = total + apply(sub, weight)
    acc[...] = total
    for sub in rest:
        @pl.when((j + 1 < nblk) & sub_used(nxt, sub))
        def _():
            group(nxt, sub, 1 - slot, _start_chunk)

        @pl.when(sub_used(j, sub))
        def _():
            acc[...] += apply(sub, weights(owners(sub)))

    @pl.when(j == nblk - 1)
    def _():
        for sub in always:
            group(nxt, sub, 1 - slot, _wait_chunk)

    @pl.when(j < prompt_tiles)
    def _():
        op_ref[...] = _layer_norm(acc[...], g_ref[...], b_ref[...])

    @pl.when(j >= prompt_tiles)
    def _():
        os_ref[...] = _layer_norm(acc[...], g_ref[...], b_ref[...])


def _moe_combine(rows, dense, gates, rank, tables, g, b, n_prompt):
    m = dense.shape[0]
    nblk = m // MOE_BLOCK
    prompt_tiles = n_prompt // MOE_BLOCK
    in_prompt, in_sample = _group_maps(prompt_tiles)
    blk = lambda j, *_: (j, 0)
    const = lambda j, *_: (0, 0)
    grid_spec = pltpu.PrefetchScalarGridSpec(
        num_scalar_prefetch=2,
        grid=(nblk,),
        in_specs=[pl.BlockSpec((MOE_BLOCK, D_MODEL), blk), pl.BlockSpec((MOE_BLOCK, LANES), blk),
                  pl.BlockSpec((MOE_BLOCK, LANES), blk),
                  pl.BlockSpec((1, SUBLANES, LANES), lambda j, *_: (j, 0, 0)),
                  pl.BlockSpec((1, D_MODEL), const), pl.BlockSpec((1, D_MODEL), const),
                  pl.BlockSpec(memory_space=pl.ANY)],
        out_specs=[pl.BlockSpec((MOE_BLOCK, D_MODEL), in_prompt), pl.BlockSpec((MOE_BLOCK, D_MODEL), in_sample)],
        scratch_shapes=[pltpu.VMEM((MOE_BLOCK, D_MODEL), F32), pltpu.VMEM((2, MOE_CAP, D_MODEL), BF16),
                        pltpu.SemaphoreType.DMA((2,))],
    )
    return pl.pallas_call(
        functools.partial(_moe_combine_kernel, prompt_tiles=prompt_tiles),
        grid_spec=grid_spec,
        out_shape=[jax.ShapeDtypeStruct((n_prompt, D_MODEL), F32),
                   jax.ShapeDtypeStruct((m - n_prompt, D_MODEL), F32)],
        compiler_params=_params("arbitrary"),
        name="moe_combine",
    )(tables["src"], tables["nchunk"], dense, gates, rank, tables["bounds"], g, b, rows)


def _row(v):
    return v.reshape(1, -1).astype(F32)


def _scan_layout(batch, seq, row0):
    t = math.gcd(seq, CHUNK)
    nchunk = seq // t
    chunks = SCAN_CHUNKS if nchunk % SCAN_CHUNKS == 0 else 1
    seqs = SCAN_SEQS if (nchunk == 1 and batch % SCAN_SEQS == 0 and row0 % (SCAN_SEQS * t) == 0) else 1
    return dict(batch=batch, seq=seq, row0=row0, t=t, seqs=seqs, chunks=chunks)


def _lanes_at(v, start):
    return jnp.zeros((1, LANES), F32).at[0, start:start + v.shape[0]].set(v.astype(F32))


def kernel(x_prompt, x_sample, p_prompt, p_sample, state_ssm_conv, state_ssm, state_dn_conv, state_dn, emb_ln_g, emb_ln_b, w_in, conv_a_w, conv_a_b, ssm_dt_bias, ssm_a_log, ssm_d, ssm_norm_w, w_a, conv_b_w, dn_dt_bias, dn_a_log, dn_norm_w, w_b, w_o, ln1_g, ln1_b, router_w, router_bias, exp_w1, exp_w3, exp_w2, sh_w1, sh_w3, sh_w2, ple_w, ple_gate_w, ln2_g, ln2_b):
    bp, lp, _ = x_prompt.shape
    bs, ls, _ = x_sample.shape
    n_p = bp * lp
    n_s = bs * ls
    xp = x_prompt.reshape(n_p, D_MODEL)
    xs = x_sample.reshape(n_s, D_MODEL)

    w = w_in[0]
    o_z, o_xbc, o_dt = 0, SSM_INNER, SSM_INNER + SSM_CONV_CH
    o_qkv = o_dt + SSM_HEADS
    o_a = o_qkv + DN_CONV_CH
    o_b = o_a + DN_HEADS
    o_zb = o_b + DN_HEADS
    o_ga = o_zb + DN_V
    w_pa = w[:, o_z:o_dt].astype(BF16)
    w_pb = jnp.concatenate([w[:, o_qkv:o_a], w[:, o_zb:o_ga]], axis=1).astype(BF16)
    w_pg = w[:, o_ga:].astype(BF16)
    w_ps = jnp.zeros((D_MODEL, LANES), F32)
    w_ps = w_ps.at[:, SMALL_DT:SMALL_DT + SSM_HEADS].set(w[:, o_dt:o_qkv])
    w_ps = w_ps.at[:, SMALL_A:SMALL_A + 2 * DN_HEADS].set(w[:, o_a:o_zb]).astype(BF16)

    eg, eb = _row(emb_ln_g), _row(emb_ln_b)
    proj_a, small = _ln_matmul(xp, xs, eg, eb, w_pa, PROJ_TM, PROJ_TN_A, w_ps)
    proj_b = _ln_matmul(xp, xs, eg, eb, w_pb, PROJ_TM, PROJ_TN_B)

    ssd_consts = [conv_a_w[0], _row(conv_a_b[0]), _lanes_at(ssm_dt_bias[0], SMALL_DT),
                  _lanes_at(ssm_a_log[0], SMALL_DT), _row(jnp.repeat(ssm_d[0], SSM_HEAD_DIM)),
                  _row(ssm_norm_w[0])]
    gdn_consts = [conv_b_w[0], _lanes_at(dn_a_log[0], SMALL_A), _lanes_at(dn_dt_bias[0], SMALL_A),
                  _row(dn_norm_w[0])]
    zeros = lambda *s: jnp.zeros(s, F32)
    prompt = _scan_layout(bp, lp, 0)
    sample = _scan_layout(bs, ls, n_p)

    ya_p, pa_conv, pa_ssm = _ssd(proj_a, small, ssd_consts, zeros(bp, CONV_WIDTH - 1, SSM_CONV_CH),
                                 zeros(bp, SSM_HEADS, SSM_HEAD_DIM, SSM_STATE), **prompt)
    ya_s, sa_conv, sa_ssm = _ssd(proj_a, small, ssd_consts, state_ssm_conv[0], state_ssm[0], **sample)
    ob_p, pb_conv, pb_dn = _gdn(proj_b, small, gdn_consts, zeros(bp, CONV_WIDTH - 1, DN_CONV_CH),
                                zeros(bp, DN_HEADS, DN_HEAD, DN_HEAD), **prompt)
    ob_s, sb_conv, sb_dn = _gdn(proj_b, small, gdn_consts, state_dn_conv[0], state_dn[0], **sample)

    x1, x1b = _merge(
        xp, xs, ya_p, ya_s, ob_p, ob_s,
        [eg, eb, w_pg, w_a[0].astype(BF16), w_b[0].astype(BF16), w_o[0].astype(BF16), _row(ln1_g[0]), _row(ln1_b[0])],
        MERGE_TM)

    router_w_t = jnp.zeros((LANES, D_MODEL), F32).at[:N_EXPERTS].set(router_w[0].T)
    router_w_hi = router_w_t.astype(BF16)
    router_w_lo = (router_w_t - router_w_hi.astype(F32)).astype(BF16)
    router_b = jnp.broadcast_to(_lanes_at(router_bias[0], 0).reshape(LANES, 1), (LANES, MOE_BLOCK))
    dense, gates, rank, chosen_t, rank_t, counts = _dense_router(
        x1, x1b, p_prompt[0].reshape(n_p, PLE_DIM), p_sample[0].reshape(n_s, PLE_DIM),
        [sh_w1[0].astype(BF16), sh_w3[0].astype(BF16), sh_w2[0].astype(BF16), ple_w[0].astype(BF16),
         ple_gate_w[0].astype(BF16), router_w_hi, router_w_lo, router_b])

    tables = _moe_tables(counts[:, 0, :N_EXPERTS].astype(jnp.int32), MOE_TM)
    sorted_rows = _moe_gather(x1b, chosen_t, rank_t, tables, MOE_TM)
    expert_out = _moe_ffn(sorted_rows, exp_w1[0], exp_w3[0], exp_w2[0], tables, MOE_TM)
    out_p, out_s = _moe_combine(expert_out, dense, gates, rank, tables, _row(ln2_g[0]), _row(ln2_b[0]), n_p)

    return (out_p.reshape(bp, lp, D_MODEL), out_s.reshape(bs, ls, D_MODEL),
            pa_conv[None], pa_ssm[None], pb_conv[None], pb_dn[None],
            sa_conv[None], sa_ssm[None], sb_conv[None], sb_dn[None])
```

```python
import functools
import math

import jax
import jax.numpy as jnp
from jax import lax
from jax.experimental import pallas as pl
from jax.experimental.pallas import tpu as pltpu

F32 = jnp.float32
BF16 = jnp.bfloat16
HIGHEST = lax.Precision.HIGHEST

D_MODEL = 1024
SSM_INNER = 2048
SSM_HEAD_DIM = 64
SSM_HEADS = 32
SSM_GROUPS = 2
SSM_HEADS_PER_GROUP = 16
SSM_STATE = 128
SSM_CONV_CH = 2560
DN_HEADS = 8
DN_HEAD = 128
DN_QK = 1024
DN_V = 1024
DN_CONV_CH = 3072
CONV_WIDTH = 4
CHUNK = 64
N_EXPERTS = 64
TOP_K = 8
N_EXPERT_GROUPS = 8
EXPERTS_PER_GROUP = 8
TOPK_GROUPS = 4
EXPERT_FF = 256
SHARED_FF = 256
ROUTED_SCALE = 2.5
PLE_DIM = 256
LN_EPS = 1e-5
RMS_EPS = 1e-6
L2_EPS = 1e-6
DEEPNORM_ALPHA = 2.0 ** 0.25

LANES = 128
SUBLANES = 8
VMEM_LIMIT = 56 * 1024 * 1024
SMALL_DT = 0
SMALL_A = 32
SMALL_B = 40
PROJ_TM = 1024
PROJ_TN_A = 2304
PROJ_TN_B = 2048
MERGE_TM = 512
SCAN_CHUNKS = 8
SCAN_SEQS = 8
MOE_BLOCK = 256
MOE_SEG = 16
MOE_SUB = 512
MOE_CAP = -(-(MOE_BLOCK * TOP_K + N_EXPERTS * (MOE_SEG - 1)) // MOE_SUB) * MOE_SUB
MOE_TM = 512
MOE_ROW_SLOTS = 3
ROUTER_TM = 512
MOE_SUB_ALWAYS = -(-(MOE_BLOCK * TOP_K + N_EXPERTS * MOE_SEG // 2) // MOE_SUB)


def _sigmoid(x):
    return 1.0 / (1.0 + jnp.exp(-x))


def _silu(x):
    return x * _sigmoid(x)


def _softplus(x):
    return jnp.maximum(x, 0.0) + jnp.log(1.0 + jnp.exp(-jnp.abs(x)))


def _layer_norm(x, g, b):
    mu = jnp.mean(x, axis=-1, keepdims=True)
    xc = x - mu
    var = jnp.mean(xc * xc, axis=-1, keepdims=True)
    return xc * lax.rsqrt(var + LN_EPS) * g + b


def _dot(a, b):
    return jnp.dot(a, b, preferred_element_type=F32)


def _dot_nt(a, b):
    return lax.dot_general(a, b, (((1,), (1,)), ((), ())), preferred_element_type=F32)


def _dot_tn(a, b):
    return lax.dot_general(a, b, (((0,), (0,)), ((), ())), preferred_element_type=F32)


def _dot_f32(a, b):
    return jnp.dot(a, b, precision=HIGHEST, preferred_element_type=F32)


def _params(*sem):
    return pltpu.CompilerParams(dimension_semantics=sem, vmem_limit_bytes=VMEM_LIMIT)


def _when(cond):
    if cond is True:
        return lambda fn: fn()
    if cond is False:
        return lambda fn: None
    return pl.when(cond)


def _row_tile(m, preferred):
    return max(d for d in range(SUBLANES, min(m, preferred) + 1, SUBLANES) if m % d == 0)


def _group_maps(prompt_tiles):
    in_prompt = lambda i, *_: (jnp.minimum(i, prompt_tiles - 1), 0)
    in_sample = lambda i, *_: (jnp.maximum(i - prompt_tiles, 0), 0)
    return in_prompt, in_sample


def _ln_matmul_kernel(xp_ref, xs_ref, g_ref, b_ref, w_ref, *rest, prompt_tiles, narrow):
    if narrow:
        wn_ref, o_ref, on_ref, h_scr = rest
    else:
        o_ref, h_scr = rest

    @pl.when(pl.program_id(1) == 0)
    def _():
        x = jnp.where(pl.program_id(0) < prompt_tiles, xp_ref[...], xs_ref[...])
        h_scr[...] = _layer_norm(x, g_ref[...], b_ref[...]).astype(BF16)
        if narrow:
            on_ref[...] = _dot(h_scr[...], wn_ref[...])

    o_ref[...] = _dot(h_scr[...], w_ref[...]).astype(o_ref.dtype)


def _ln_matmul(xp, xs, g, b, w, tm, tn, w_narrow=None):
    k = xp.shape[1]
    m = xp.shape[0] + xs.shape[0]
    n = w.shape[1]
    tm = _row_tile(math.gcd(xp.shape[0], xs.shape[0]), tm)
    prompt_tiles = xp.shape[0] // tm
    in_prompt, in_sample = _group_maps(prompt_tiles)
    narrow = w_narrow is not None
    const = lambda i, j: (0, 0)
    row = lambda i, j: (i, 0)
    in_specs = [pl.BlockSpec((tm, k), in_prompt), pl.BlockSpec((tm, k), in_sample),
                pl.BlockSpec((1, k), const), pl.BlockSpec((1, k), const), pl.BlockSpec((k, tn), lambda i, j: (0, j))]
    out_specs = [pl.BlockSpec((tm, tn), lambda i, j: (i, j))]
    out_shape = [jax.ShapeDtypeStruct((m, n), BF16)]
    args = [xp, xs, g, b, w]
    if narrow:
        in_specs.append(pl.BlockSpec((k, LANES), const))
        out_specs.append(pl.BlockSpec((tm, LANES), row))
        out_shape.append(jax.ShapeDtypeStruct((m, LANES), F32))
        args.append(w_narrow)
    out = pl.pallas_call(
        functools.partial(_ln_matmul_kernel, prompt_tiles=prompt_tiles, narrow=narrow),
        grid=(m // tm, n // tn),
        in_specs=in_specs,
        out_specs=out_specs,
        out_shape=out_shape,
        scratch_shapes=[pltpu.VMEM((tm, k), BF16)],
        compiler_params=_params("parallel", "arbitrary"),
        name="ln_matmul",
    )(*args)
    return out if narrow else out[0]


def _load_conv_state(cbuf, state):
    taps = CONV_WIDTH - 1
    cbuf[0:SUBLANES - taps, :] = jnp.zeros((SUBLANES - taps, cbuf.shape[1]), F32)
    cbuf[SUBLANES - taps:SUBLANES, :] = state


def _causal_conv(cbuf, x16, cw, t):
    ch = x16.shape[1]
    prev = cbuf[...]
    taps = CONV_WIDTH - 1
    if t <= SUBLANES:
        x = x16.astype(F32)
        ext = jnp.concatenate([prev, x], axis=0)
        lo = SUBLANES - taps
        y = ext[lo:lo + t] * cw[0:1]
        y = y + ext[lo + 1:lo + 1 + t] * cw[1:2]
        y = y + ext[lo + 2:lo + 2 + t] * cw[2:3]
        y = y + x * cw[3:4]
        cbuf[...] = ext[t:t + SUBLANES]
        return y, ext[SUBLANES + t - taps:SUBLANES + t]
    hi = prev.astype(BF16).astype(F32)
    mid = (prev - hi).astype(BF16).astype(F32)
    low = ((prev - hi) - mid).astype(BF16).astype(F32)
    x = x16.astype(F32)
    pack = 2 * SUBLANES
    nhead = 4 * SUBLANES if t % pack == 0 else 5 * SUBLANES
    head = jnp.concatenate([hi, mid, low, jnp.zeros((nhead - 3 * SUBLANES, ch), F32)], axis=0)
    if t % pack == 0:
        ext = jnp.concatenate([head.astype(BF16), x16], axis=0)
    else:
        ext = jnp.concatenate([head, x], axis=0).astype(BF16)
    out_row = lax.broadcasted_iota(jnp.int32, (taps * t, nhead + t), 0)
    col = lax.broadcasted_iota(jnp.int32, (taps * t, nhead + t), 1)
    log_t = t.bit_length() - 1
    shift = lax.shift_right_logical(out_row, log_t) + 1
    src = (out_row & (t - 1)) - shift
    in_x = (src >= 0) & (col == nhead + src)
    in_prev = (src < 0) & (col < 3 * SUBLANES) & ((col & (SUBLANES - 1)) == SUBLANES + src)
    shifted = _dot((in_x | in_prev).astype(BF16), ext)
    y = shifted[2 * t:3 * t] * cw[0:1]
    y = y + shifted[t:2 * t] * cw[1:2]
    y = y + shifted[0:t] * cw[2:3]
    y = y + x * cw[3:4]
    cbuf[...] = x[t - SUBLANES:t]
    return y, x[t - taps:t]


def _lane_expand(v, h0, count, width):
    t = v.shape[0]
    n = count * width
    out = jnp.broadcast_to(v[:, h0:h0 + 1], (t, n))
    if count > 1:
        lane = lax.broadcasted_iota(jnp.int32, (t, n), 1)
        for i in range(1, count):
            out = jnp.where(lane >= i * width, jnp.broadcast_to(v[:, h0 + i:h0 + i + 1], (t, n)), out)
    return out


def _grouped_transpose(v, per_tile):
    t = v.shape[0]
    blocks = [v if r == 0 else pltpu.roll(v, LANES - r, axis=1) for r in range(per_tile)]
    if per_tile * t < LANES:
        blocks.append(jnp.zeros((LANES - per_tile * t, LANES), F32))
    return jnp.concatenate(blocks, axis=0).T


def _same_block(i, j, size):
    shift = size.bit_length() - 1
    return lax.shift_right_logical(i, shift) == lax.shift_right_logical(j, shift)


def _unit_lower_inverses(lmats, ii, jj, t):
    mm = lambda a, b: _dot(a.astype(BF16), b.astype(BF16))
    base = min(16, t)
    in_base = _same_block(ii, jj, base)
    eye = (ii == jj).astype(F32)
    power = [jnp.where(in_base, l, 0.0) for l in lmats]
    inv = [eye - p for p in power]
    span = 2
    while span < base:
        power = [mm(p, p) for p in power]
        inv = [a + mm(a, p) for a, p in zip(inv, power)]
        span *= 2
    size = base
    while size < t:
        link = _same_block(ii, jj, 2 * size) & jnp.logical_not(_same_block(ii, jj, size))
        cross = [mm(a, jnp.where(link, l, 0.0)) for a, l in zip(inv, lmats)]
        inv = [a - mm(c, a) for a, c in zip(inv, cross)]
        size *= 2
    return inv


def _row_sums_of_squares(blocks):
    t = blocks[0].shape[0]
    sq = jnp.concatenate([b * b for b in blocks], axis=0)
    hi = sq.astype(BF16)
    lo = (sq - hi.astype(F32)).astype(BF16)
    ones = jnp.ones((LANES, LANES), BF16)
    sums = _dot(hi, ones) + _dot(lo, ones)
    return [sums[i * t:(i + 1) * t] for i in range(len(blocks))]


def _cumsum_rows(v, t):
    ii = lax.broadcasted_iota(jnp.int32, (t, t), 0)
    jj = lax.broadcasted_iota(jnp.int32, (t, t), 1)
    return _dot_f32((jj <= ii).astype(F32), v)


def _ssd_kernel(pa_ref, sm_ref, cw_ref, cb_ref, dtb_ref, alog_ref, dskip_ref, nw_ref, cst_ref, h0_ref,
                y_ref, cnew_ref, hnew_ref, cbuf, s_scr, *, t, first, last):
    per_tile = LANES // t
    width = per_tile * SSM_HEAD_DIM

    @_when(first)
    def _():
        _load_conv_state(cbuf, cst_ref[0])
        s_scr[...] = h0_ref[0]

    z = pa_ref[:, :SSM_INNER].astype(F32)
    conv, tail = _causal_conv(cbuf, pa_ref[:, SSM_INNER:], cw_ref[...], t)

    @_when(last)
    def _():
        cnew_ref[0] = tail

    xbc = _silu(conv + cb_ref[...])
    xs = xbc[:, :SSM_INNER]
    bm = xbc[:, SSM_INNER:SSM_INNER + SSM_GROUPS * SSM_STATE]
    cm = xbc[:, SSM_INNER + SSM_GROUPS * SSM_STATE:]

    dt = _softplus(sm_ref[...] + dtb_ref[...])
    da = dt * (-jnp.exp(alog_ref[...]))
    cum = _cumsum_rows(da, t)
    cum_t = _grouped_transpose(cum, per_tile)
    dt_t = _grouped_transpose(dt, per_tile)

    row = lax.broadcasted_iota(jnp.int32, (t, LANES), 0)
    lane = lax.broadcasted_iota(jnp.int32, (t, LANES), 1)
    causal = (lane % t) <= row
    brow = lax.broadcasted_iota(jnp.int32, (LANES, width), 0)
    bcol = lax.broadcasted_iota(jnp.int32, (LANES, width), 1)
    blockdiag = (brow // t) == (bcol // SSM_HEAD_DIM)

    ys = []
    for g in range(SSM_GROUPS):
        bg = bm[:, g * SSM_STATE:(g + 1) * SSM_STATE]
        cg = cm[:, g * SSM_STATE:(g + 1) * SSM_STATE].astype(BF16)
        cb = _dot_nt(cg, jnp.concatenate([bg] * per_tile, axis=0).astype(BF16))
        bg = bg.astype(BF16)
        for q in range(SSM_HEADS_PER_GROUP // per_tile):
            h0 = g * SSM_HEADS_PER_GROUP + q * per_tile
            ccol = _lane_expand(cum, h0, per_tile, t)
            seg = ccol - cum_t[h0:h0 + 1, :]
            decay = jnp.exp(jnp.where(causal, seg, -jnp.inf))
            wts = (cb * decay * dt_t[h0:h0 + 1, :]).astype(BF16)
            xt = xs[:, h0 * SSM_HEAD_DIM:h0 * SSM_HEAD_DIM + width]
            xbd = jnp.where(blockdiag, jnp.concatenate([xt] * per_tile, axis=0), 0.0).astype(BF16)
            y_intra = _dot(wts, xbd)
            st = s_scr[h0:h0 + per_tile].reshape(width, SSM_STATE)
            cum_w = ccol if width == LANES else _lane_expand(cum, h0, per_tile, SSM_HEAD_DIM)
            y_state = _dot_nt(cg, st.astype(BF16)) * jnp.exp(cum_w)
            ys.append(y_intra + y_state)
            wend = jnp.exp(cum_w[t - 1:t, :] - cum_w) * _lane_expand(dt, h0, per_tile, SSM_HEAD_DIM)
            ds = _dot_tn((xt * wend).astype(BF16), bg)
            for i in range(per_tile):
                h = h0 + i
                s_scr[h] = jnp.exp(cum[t - 1:t, h:h + 1]) * s_scr[h] + ds[i * SSM_HEAD_DIM:(i + 1) * SSM_HEAD_DIM]

    y = jnp.concatenate(ys, axis=1) + dskip_ref[...] * xs
    y = y * _silu(z)
    half = SSM_INNER // SSM_GROUPS
    normed = []
    for g in range(SSM_GROUPS):
        yg = y[:, g * half:(g + 1) * half]
        normed.append(yg * lax.rsqrt(jnp.mean(yg * yg, axis=-1, keepdims=True) + RMS_EPS))
    y_ref[...] = (jnp.concatenate(normed, axis=1) * nw_ref[...]).astype(BF16)

    @_when(last)
    def _():
        hnew_ref[0] = s_scr[...]


def _scan_step_kernel(chunk_kernel, n_tok, n_const, n_state, seqs, chunks, t):
    def step(*refs):
        tok = refs[:n_tok]
        const = refs[n_tok:n_tok + n_const]
        st_in = refs[n_tok + n_const:n_tok + n_const + n_state]
        y_ref = refs[n_tok + n_const + n_state]
        st_out = refs[n_tok + n_const + n_state + 1:n_tok + n_const + 2 * n_state + 1]
        scratch = refs[n_tok + n_const + 2 * n_state + 1:]
        first_step = pl.program_id(1) == 0
        last_step = pl.program_id(1) == pl.num_programs(1) - 1
        for s in range(seqs):
            for c in range(chunks):
                rows = pl.ds((s * chunks + c) * t, t)
                chunk_kernel(*[r.at[rows, :] for r in tok], *const, *[r.at[pl.ds(s, 1)] for r in st_in],
                             y_ref.at[rows, :], *[r.at[pl.ds(s, 1)] for r in st_out], *scratch,
                             first=first_step if c == 0 else False,
                             last=last_step if c == chunks - 1 else False)
    return step


def _scan_call(kernel, name, tok_in, const_in, state_in, y_width, scratch, *, batch, seq, row0, t, seqs, chunks,
               step_kernel=None):
    nstep = seq // (t * chunks)
    rows = seqs * chunks * t
    blk0 = row0 // rows

    def per_batch(a):
        zeros = (0,) * (a.ndim - 1)
        return pl.BlockSpec((seqs,) + a.shape[1:], lambda b, c: (b,) + zeros)

    if step_kernel is None:
        step_kernel = _scan_step_kernel(kernel, len(tok_in), len(const_in), len(state_in), seqs, chunks, t)
    return pl.pallas_call(
        step_kernel,
        grid=(batch // seqs, nstep),
        in_specs=[pl.BlockSpec((rows, a.shape[1]), lambda b, c: (blk0 + b * nstep + c, 0)) for a in tok_in]
        + [pl.BlockSpec(a.shape, lambda b, c: (0, 0)) for a in const_in]
        + [per_batch(a) for a in state_in],
        out_specs=[pl.BlockSpec((rows, y_width), lambda b, c: (b * nstep + c, 0))]
        + [per_batch(a) for a in state_in],
        out_shape=[jax.ShapeDtypeStruct((batch * seq, y_width), BF16)]
        + [jax.ShapeDtypeStruct(a.shape, F32) for a in state_in],
        scratch_shapes=scratch,
        compiler_params=_params("parallel", "arbitrary"),
        name=name,
    )(*tok_in, *const_in, *state_in)


def _ssd(proj_a, small, consts, conv_state, h0, **where):
    t = where["t"]
    scratch = [pltpu.VMEM((SUBLANES, SSM_CONV_CH), F32),
               pltpu.VMEM((SSM_HEADS, SSM_HEAD_DIM, SSM_STATE), F32)]
    return _scan_call(functools.partial(_ssd_kernel, t=t), f"ssd_t{t}", [proj_a, small], consts,
                      [conv_state, h0], SSM_INNER, scratch, **where)


def _gdn_step_kernel(pb_ref, sm_ref, cw_ref, alog_ref, dtb_ref, nw_ref, cst_ref, s0_ref,
                     o_ref, cnew_ref, snew_ref, cbuf, s_scr, *, t, seqs, chunks):
    first_step = pl.program_id(1) == 0
    last_step = pl.program_id(1) == pl.num_programs(1) - 1
    units = [(s, c) for s in range(seqs) for c in range(chunks)]
    rows = lambda u: pl.ds(u * t, t)
    per_tile = min(LANES // t, DN_HEADS)
    n = per_tile * t
    ntile = DN_HEADS // per_tile
    heads = range(DN_HEADS)
    ii = lax.broadcasted_iota(jnp.int32, (n, n), 0)
    jj = lax.broadcasted_iota(jnp.int32, (n, n), 1)
    causal = _same_block(ii, jj, t) & (jj <= ii)
    diag = ii == jj
    stack = lambda xs, p: jnp.concatenate(xs[p * per_tile:(p + 1) * per_tile], axis=0)

    qkv, zb, gcum, beta = [], [], [], []
    for u, (s, c) in enumerate(units):
        @_when(first_step if c == 0 else False)
        def _():
            _load_conv_state(cbuf, cst_ref[s])

        conv, tail = _causal_conv(cbuf, pb_ref[rows(u), :DN_CONV_CH], cw_ref[...], t)

        @_when(last_step if c == chunks - 1 else False)
        def _():
            cnew_ref[s] = tail

        qkv.append(_silu(conv))
        zb.append(pb_ref[rows(u), DN_CONV_CH:].astype(F32))
        sm = sm_ref[rows(u), :]
        beta.append(_sigmoid(sm))
        gate = -jnp.exp(alog_ref[...]) * _softplus(sm + dtb_ref[...])
        gcum.append(_cumsum_rows(gate, t))
    gcum_t = [_grouped_transpose(g, per_tile) for g in gcum]

    sumsq = _row_sums_of_squares([x[:, i * DN_HEAD:(i + 1) * DN_HEAD] for x in qkv for i in range(2 * DN_HEADS)])
    q, k, v, gcol, bcol, kb, egcol = ([[] for _ in units] for _ in range(7))
    for u in range(len(units)):
        for h in heads:
            lo = h * DN_HEAD
            ss = sumsq[u * 2 * DN_HEADS:(u + 1) * 2 * DN_HEADS]
            q[u].append(qkv[u][:, lo:lo + DN_HEAD] * lax.rsqrt(ss[h] + L2_EPS) * (DN_HEAD ** -0.5))
            k[u].append(qkv[u][:, DN_QK + lo:DN_QK + lo + DN_HEAD] * lax.rsqrt(ss[DN_HEADS + h] + L2_EPS))
            v[u].append(qkv[u][:, 2 * DN_QK + lo:2 * DN_QK + lo + DN_HEAD])
            gcol[u].append(gcum[u][:, SMALL_A + h:SMALL_A + h + 1])
            bcol[u].append(beta[u][:, SMALL_B + h:SMALL_B + h + 1])
            kb[u].append(k[u][h] * bcol[u][h])
            egcol[u].append(jnp.exp(gcol[u][h]))

    tiles = [(u, p) for u in range(len(units)) for p in range(ntile)]
    k16 = [stack(k[u], p).astype(BF16) for u, p in tiles]
    decay = []
    for u, p in tiles:
        h0 = p * per_tile
        seg = stack(gcol[u], p) - gcum_t[u][SMALL_A + h0:SMALL_A + h0 + 1, :n]
        decay.append(jnp.exp(jnp.where(causal, seg, -jnp.inf)))
    lmat = [_dot_nt(stack(kb[u], p).astype(BF16), k16[i]) * jnp.where(diag, 0.0, decay[i])
            for i, (u, p) in enumerate(tiles)]
    qk = [(_dot_nt(stack(q[u], p).astype(BF16), k16[i]) * decay[i]).astype(BF16) for i, (u, p) in enumerate(tiles)]
    inv = [a.astype(BF16) for a in _unit_lower_inverses(lmat, ii, jj, t)]

    split = lambda x: (x[:, :DN_HEAD], x[:, DN_HEAD:])
    o_from, o_own, s_from, s_own, glast = ([[] for _ in units] for _ in range(5))
    aw = [_dot(inv[i], jnp.concatenate(
        [stack([kb[u][h] * egcol[u][h] for h in heads], p), stack([v[u][h] * bcol[u][h] for h in heads], p)],
        axis=1).astype(BF16)) for i, (u, p) in enumerate(tiles)]
    qaw = [_dot(qk[i], aw[i].astype(BF16)) for i in range(len(tiles))]
    pad_rows = -t % (2 * SUBLANES)
    pad = [jnp.zeros((pad_rows, DN_HEAD), F32)] if pad_rows else []
    for u in range(len(units)):
        for h in heads:
            p, j = divmod(h, per_tile)
            i = u * ntile + p
            sel = slice(j * t, (j + 1) * t)
            qkw, qku0 = split(qaw[i][sel])
            o_from[u].append(q[u][h] * egcol[u][h] - qkw)
            o_own[u].append(qku0)
            glast[u].append(gcum[u][t - 1:t, SMALL_A + h:SMALL_A + h + 1])
            kd = (k[u][h] * jnp.exp(glast[u][h] - gcol[u][h])).astype(BF16)
            kdw, kdu0 = split(_dot_tn(kd, aw[i][sel].astype(BF16)))
            s_from[u].append(kdw)
            s_own[u].append(kdu0)

    outs = []
    for u, (s, c) in enumerate(units):
        @_when(first_step if c == 0 else False)
        def _():
            s_scr[...] = s0_ref[s]

        for h in heads:
            state = s_scr[h]
            both = _dot(jnp.concatenate([s_from[u][h], o_from[u][h]] + pad, axis=0).astype(BF16), state.astype(BF16))
            outs.append(both[DN_HEAD:DN_HEAD + t] + o_own[u][h])
            s_scr[h] = jnp.exp(glast[u][h]) * state - both[:DN_HEAD] + s_own[u][h]

        @_when(last_step if c == chunks - 1 else False)
        def _():
            snew_ref[s] = s_scr[...]

    norm_w = nw_ref[...]
    osq = _row_sums_of_squares(outs)
    for u in range(len(units)):
        normed = []
        for h in heads:
            i = u * DN_HEADS + h
            normed.append(outs[i] * lax.rsqrt(osq[i] * (1.0 / DN_HEAD) + RMS_EPS) * norm_w
                          * _silu(zb[u][:, h * DN_HEAD:(h + 1) * DN_HEAD]))
        o_ref[rows(u), :] = jnp.concatenate(normed, axis=1).astype(BF16)


def _gdn(proj_b, small, consts, conv_state, s0, **where):
    t = where["t"]
    scratch = [pltpu.VMEM((SUBLANES, DN_CONV_CH), F32),
               pltpu.VMEM((DN_HEADS, DN_HEAD, DN_HEAD), F32)]
    step = functools.partial(_gdn_step_kernel, t=t, seqs=where["seqs"], chunks=where["chunks"])
    return _scan_call(None, f"gdn_t{t}", [proj_b, small], consts, [conv_state, s0], DN_V, scratch,
                      step_kernel=step, **where)


def _merge_kernel(xp_ref, xs_ref, yap_ref, yas_ref, obp_ref, obs_ref, eg_ref, eb_ref, wg_ref, wa_ref, wb_ref,
                  wo_ref, g_ref, b_ref, x1_ref, x1b_ref, *, prompt_tiles):
    is_prompt = pl.program_id(0) < prompt_tiles
    h = _layer_norm(jnp.where(is_prompt, xp_ref[...], xs_ref[...]), eg_ref[...], eb_ref[...])
    out_a = _dot(jnp.where(is_prompt, yap_ref[...], yas_ref[...]), wa_ref[...])
    out_b = _dot(jnp.where(is_prompt, obp_ref[...], obs_ref[...]), wb_ref[...])
    gt = _dot(h.astype(BF16), wg_ref[...])
    merged = _sigmoid(gt[:, :D_MODEL]) * out_a + _sigmoid(gt[:, D_MODEL:]) * out_b
    mix = _dot(merged.astype(BF16), wo_ref[...])
    x1 = _layer_norm(DEEPNORM_ALPHA * h + mix, g_ref[...], b_ref[...])
    x1_ref[...] = x1
    x1b_ref[...] = x1.astype(BF16)


def _merge(xp, xs, ya_p, ya_s, ob_p, ob_s, consts, tm):
    m = xp.shape[0] + xs.shape[0]
    n_p = xp.shape[0]
    tm = _row_tile(math.gcd(n_p, m - n_p), tm)
    in_prompt, in_sample = _group_maps(n_p // tm)
    row = lambda i: (i, 0)
    const = lambda i: (0, 0)
    return pl.pallas_call(
        functools.partial(_merge_kernel, prompt_tiles=n_p // tm),
        grid=(m // tm,),
        in_specs=[pl.BlockSpec((tm, D_MODEL), in_prompt), pl.BlockSpec((tm, D_MODEL), in_sample),
                  pl.BlockSpec((tm, SSM_INNER), in_prompt), pl.BlockSpec((tm, SSM_INNER), in_sample),
                  pl.BlockSpec((tm, DN_V), in_prompt), pl.BlockSpec((tm, DN_V), in_sample)]
        + [pl.BlockSpec(a.shape, const) for a in consts],
        out_specs=[pl.BlockSpec((tm, D_MODEL), row), pl.BlockSpec((tm, D_MODEL), row)],
        out_shape=[jax.ShapeDtypeStruct((m, D_MODEL), F32), jax.ShapeDtypeStruct((m, D_MODEL), BF16)],
        compiler_params=_params("parallel"),
        name="merge",
    )(xp, xs, ya_p, ya_s, ob_p, ob_s, *consts)


def _take_first_max(rest, index, limit, axis):
    best = jnp.max(rest, axis=axis, keepdims=True)
    first = jnp.min(jnp.where(rest == best, index, limit), axis=axis, keepdims=True)
    hit = index == first
    return hit, jnp.where(hit, -jnp.inf, rest)


def _router_gates_t(x1, xb, rw_hi, rw_lo, bias):
    tm = x1.shape[0]
    x_lo = (x1 - xb.astype(F32)).astype(BF16)
    logits = _dot_nt(rw_hi, xb) + (_dot_nt(rw_hi, x_lo) + _dot_nt(rw_lo, xb))
    scores = _sigmoid(logits)
    row = lax.broadcasted_iota(jnp.int32, (LANES, tm), 0)
    sel = jnp.where(row < N_EXPERTS, scores + bias, -jnp.inf)
    by_group = sel.reshape(LANES // EXPERTS_PER_GROUP, EXPERTS_PER_GROUP, tm)
    sub = lax.broadcasted_iota(jnp.int32, by_group.shape, 1)
    top1 = jnp.max(by_group, axis=1, keepdims=True)
    _, others = _take_first_max(by_group, sub, EXPERTS_PER_GROUP, 1)
    gscore = (top1 + jnp.max(others, axis=1, keepdims=True))[:N_EXPERT_GROUPS]
    gidx = lax.broadcasted_iota(jnp.int32, gscore.shape, 0)
    gkeep = jnp.zeros(gscore.shape, jnp.bool_)
    for _ in range(TOPK_GROUPS):
        hit, gscore = _take_first_max(gscore, gidx, N_EXPERT_GROUPS, 0)
        gkeep = gkeep | hit
    gkeep = jnp.broadcast_to(gkeep, (N_EXPERT_GROUPS, EXPERTS_PER_GROUP, tm)).reshape(N_EXPERTS, tm)
    rest = jnp.where(gkeep, sel[:N_EXPERTS], -jnp.inf)
    eidx = lax.broadcasted_iota(jnp.int32, (N_EXPERTS, tm), 0)
    keep = jnp.zeros((N_EXPERTS, tm), jnp.bool_)
    for _ in range(TOP_K):
        hit, rest = _take_first_max(rest, eidx, N_EXPERTS, 0)
        keep = keep | hit
    picked = jnp.where(keep, scores[:N_EXPERTS], 0.0)
    gates = picked / jnp.sum(picked, axis=0, keepdims=True) * ROUTED_SCALE
    return jnp.concatenate([gates, jnp.zeros((LANES - N_EXPERTS, tm), F32)], axis=0)


def _dense_router_kernel(x1_ref, x1b_ref, pp_ref, ps_ref, sw1_ref, sw3_ref, sw2_ref, pw_ref, pg_ref, rwh_ref,
                         rwl_ref, rb_ref, dense_ref, gates_ref, rank_ref, chosen_t_ref, rank_t_ref, count_ref,
                         *, prompt_tiles):
    xb = x1b_ref[...]
    hid = _silu(_dot(xb, sw1_ref[...])) * _dot(xb, sw3_ref[...])
    shared = _dot(hid.astype(BF16), sw2_ref[...])
    p = jnp.where(pl.program_id(0) < prompt_tiles, pp_ref[...], ps_ref[...])
    ple = _dot(p.astype(BF16), pw_ref[...]) * _sigmoid(_dot(xb, pg_ref[...]))
    x1 = x1_ref[...]
    dense_ref[...] = DEEPNORM_ALPHA * x1 + shared + ple
    ii = lax.broadcasted_iota(jnp.int32, (MOE_BLOCK, MOE_BLOCK), 0)
    jj = lax.broadcasted_iota(jnp.int32, (MOE_BLOCK, MOE_BLOCK), 1)
    earlier = (ii < jj).astype(BF16)
    for blk in range(x1.shape[0] // MOE_BLOCK):
        tok = slice(blk * MOE_BLOCK, (blk + 1) * MOE_BLOCK)
        exp = slice(blk * LANES, (blk + 1) * LANES)
        gates_t = _router_gates_t(x1[tok], xb[tok], rwh_ref[...], rwl_ref[...], rb_ref[...])
        chosen_t = (gates_t > 0.0).astype(BF16)
        rank_t = _dot(chosen_t, earlier)
        chosen_t_ref[exp, :] = chosen_t
        rank_t_ref[exp, :] = rank_t.astype(BF16)
        gates = gates_t.T
        rank = rank_t.T
        gates_ref[tok, :] = gates.astype(BF16)
        rank_ref[tok, :] = rank.astype(BF16)
        count = rank[MOE_BLOCK - 1:, :] + (gates[MOE_BLOCK - 1:, :] > 0.0).astype(F32)
        count_ref[blk] = jnp.broadcast_to(count, (SUBLANES, LANES))


def _dense_router(x1, x1b, p_prompt, p_sample, consts):
    m = x1.shape[0]
    tm = _row_tile(math.gcd(p_prompt.shape[0], p_sample.shape[0]), ROUTER_TM)
    assert tm % MOE_BLOCK == 0
    nblk = m // MOE_BLOCK
    per_step = tm // MOE_BLOCK
    prompt_tiles = p_prompt.shape[0] // tm
    in_prompt, in_sample = _group_maps(prompt_tiles)
    row = lambda i: (i, 0)
    const = lambda i: (0, 0)
    return pl.pallas_call(
        functools.partial(_dense_router_kernel, prompt_tiles=prompt_tiles),
        grid=(m // tm,),
        in_specs=[pl.BlockSpec((tm, D_MODEL), row), pl.BlockSpec((tm, D_MODEL), row),
                  pl.BlockSpec((tm, PLE_DIM), in_prompt), pl.BlockSpec((tm, PLE_DIM), in_sample)]
        + [pl.BlockSpec(a.shape, const) for a in consts],
        out_specs=[pl.BlockSpec((tm, D_MODEL), row), pl.BlockSpec((tm, LANES), row),
                   pl.BlockSpec((tm, LANES), row), pl.BlockSpec((per_step * LANES, MOE_BLOCK), row),
                   pl.BlockSpec((per_step * LANES, MOE_BLOCK), row),
                   pl.BlockSpec((per_step, SUBLANES, LANES), lambda i: (i, 0, 0))],
        out_shape=[jax.ShapeDtypeStruct((m, D_MODEL), F32), jax.ShapeDtypeStruct((m, LANES), BF16),
                   jax.ShapeDtypeStruct((m, LANES), BF16), jax.ShapeDtypeStruct((nblk * LANES, MOE_BLOCK), BF16),
                   jax.ShapeDtypeStruct((nblk * LANES, MOE_BLOCK), BF16),
                   jax.ShapeDtypeStruct((nblk, SUBLANES, LANES), F32)],
        compiler_params=_params("parallel"),
        name="dense_router",
    )(x1, x1b, p_prompt, p_sample, *consts)


def _moe_tables(counts, tm):
    nblk = counts.shape[0]
    nchunk_max = MOE_CAP // MOE_SEG
    padded = (counts + MOE_SEG - 1) // MOE_SEG * MOE_SEG
    start = jnp.cumsum(padded, axis=1) - padded
    used = jnp.sum(padded, axis=1)
    rows_e = jnp.sum(padded, axis=0)
    rows_e_t = (rows_e + tm - 1) // tm * tm
    base_e = jnp.cumsum(rows_e_t) - rows_e_t
    seg_row = base_e[None, :] + jnp.cumsum(padded, axis=0) - padded
    chunk0 = jnp.arange(nchunk_max, dtype=jnp.int32) * MOE_SEG
    owner = jnp.sum((start + padded)[:, None, :] <= chunk0[None, :, None], axis=-1)
    onehot = owner[..., None] == jnp.arange(N_EXPERTS, dtype=jnp.int32)
    dst = chunk0[None, :] + jnp.sum(jnp.where(onehot, (seg_row - start)[:, None, :], 0), axis=-1)
    n_tiles = _moe_max_tiles(nblk, tm)
    live = chunk0[None, :] < used[:, None]
    parity = (jnp.arange(nblk, dtype=jnp.int32) % 2)[:, None]
    spare = n_tiles * tm + parity * MOE_CAP + chunk0[None, :]
    src = jnp.where(live, dst, 0)
    dst = jnp.where(live, dst, spare)
    tile_row = jnp.arange(n_tiles, dtype=jnp.int32) * tm
    tiles_used = jnp.sum(rows_e_t) // tm
    tile_row = jnp.minimum(tile_row, (tiles_used - 1) * tm)
    tile_expert = jnp.minimum(jnp.sum((base_e + rows_e_t)[None, :] <= tile_row[:, None], axis=-1), N_EXPERTS - 1)
    in_use = rows_e > 0
    experts = jnp.arange(N_EXPERTS, dtype=jnp.int32)
    later = jnp.where(in_use[None, :] & (experts[None, :] > experts[:, None]), experts[None, :], N_EXPERTS)
    next_in_use = jnp.min(later, axis=1)
    next_in_use = jnp.where(next_in_use == N_EXPERTS, -1, next_in_use)
    order = jnp.sum(in_use[None, :] & (experts[None, :] < experts[:, None]), axis=1)
    of_tile = tile_expert[:, None] == experts[None, :]
    tile_slot = jnp.sum(jnp.where(of_tile, (order % 2)[None, :], 0), axis=1)
    tile_next = jnp.sum(jnp.where(of_tile, next_in_use[None, :], 0), axis=1)
    bounds = jnp.zeros((nblk, SUBLANES, LANES), F32)
    bounds = bounds.at[:, 0, :N_EXPERTS].set(start.astype(F32)).at[:, 1, :N_EXPERTS].set(padded.astype(F32))
    i32 = lambda a: a.astype(jnp.int32)
    return dict(dst=i32(dst.reshape(-1)), src=i32(src.reshape(-1)), nchunk=i32(used // MOE_SEG),
                pad_row=i32(base_e + rows_e),
                pad_n=i32((rows_e_t - rows_e) // MOE_SEG), tile_expert=i32(tile_expert),
                tile_slot=i32(tile_slot), tile_next=i32(tile_next),
                tiles_used=i32(tiles_used.reshape(1)), bounds=bounds)


def _moe_max_tiles(nblk, tm):
    return (nblk * (MOE_BLOCK * TOP_K + N_EXPERTS * (MOE_SEG - 1)) + N_EXPERTS * (tm - MOE_SEG) + tm - 1) // tm


def _slot_bounds(bounds):
    return bounds[0:1, :], bounds[1:2, :]


def _moe_gather_kernel(dst_ref, nchunk_ref, pad_row_ref, pad_n_ref,
                       x_ref, chosen_t_ref, rank_t_ref, bounds_ref, rows_ref, buf, zbuf, sem, zsem):
    j = pl.program_id(0)
    nblk = pl.num_programs(0)
    slot = j % 2
    nchunk_max = MOE_CAP // MOE_SEG

    def chunk_copy(blk, q, sl):
        src = buf.at[sl, pl.ds(pl.multiple_of(q * MOE_SEG, MOE_SEG), MOE_SEG), :]
        row = pl.multiple_of(dst_ref[blk * nchunk_max + q], MOE_SEG)
        return pltpu.make_async_copy(src, rows_ref.at[pl.ds(row, MOE_SEG), :], sem.at[sl])

    def pad_copy(e, i):
        row = pl.multiple_of(pad_row_ref[e] + i * MOE_SEG, MOE_SEG)
        return pltpu.make_async_copy(zbuf, rows_ref.at[pl.ds(row, MOE_SEG), :], zsem)

    def sub_chunks(blk, sub, sl, action):
        for q in range(sub * (MOE_SUB // MOE_SEG), (sub + 1) * (MOE_SUB // MOE_SEG)):
            action(chunk_copy(blk, q, sl))

    def sub_used(blk, sub):
        return True if sub < MOE_SUB_ALWAYS else sub * MOE_SUB < nchunk_ref[blk] * MOE_SEG

    def for_pads(action):
        def per_expert(e, carry):
            def body(i, c):
                action(pad_copy(e, i))
                return c
            return lax.fori_loop(0, pad_n_ref[e], body, carry)
        lax.fori_loop(0, N_EXPERTS, per_expert, 0)

    @pl.when(j == 0)
    def _():
        zbuf[...] = jnp.zeros_like(zbuf)
        for_pads(lambda c: c.start())

    start, length = _slot_bounds(bounds_ref[0])
    x = x_ref[...]
    chosen_rank_t = jnp.concatenate([chosen_t_ref[...], rank_t_ref[...]], axis=1)
    before = jnp.maximum(j - 2, 0)
    subs = range(MOE_CAP // MOE_SUB)

    @pl.when(j >= 2)
    def _():
        for sub in subs:
            @_when(sub_used(before, sub))
            def _():
                sub_chunks(before, sub, slot, lambda c: c.wait())

    def owners(sub):
        s = (sub * MOE_SUB + lax.broadcasted_iota(jnp.int32, (MOE_SUB, LANES), 0)).astype(F32)
        owner = (s >= start) & (s < start + length)
        within = s[:, 0:1] - jnp.sum(jnp.where(owner, start, 0.0), axis=1, keepdims=True)
        return owner.astype(BF16), within

    def picks(owner_within):
        owner, within = owner_within
        hit = _dot(owner, chosen_rank_t)
        return ((hit[:, :MOE_BLOCK] > 0.5) & (hit[:, MOE_BLOCK:] == within)).astype(BF16)

    def emit(sub, pick):
        buf[slot, sub * MOE_SUB:(sub + 1) * MOE_SUB, :] = _dot(pick, x).astype(BF16)
        sub_chunks(j, sub, slot, lambda c: c.start())

    staged = [picks(ow) for ow in [owners(sub) for sub in subs[:MOE_SUB_ALWAYS]]]
    for sub, pick in enumerate(staged):
        emit(sub, pick)
    for sub in subs[MOE_SUB_ALWAYS:]:
        @pl.when(sub_used(j, sub))
        def _():
            emit(sub, picks(owners(sub)))

    @pl.when(j == nblk - 1)
    def _():
        prev = jnp.maximum(j - 1, 0)
        for sub in subs:
            @_when(sub_used(j, sub))
            def _():
                sub_chunks(j, sub, slot, lambda c: c.wait())

            @_when((j >= 1) & sub_used(prev, sub))
            def _():
                sub_chunks(prev, sub, 1 - slot, lambda c: c.wait())

        for_pads(lambda c: c.wait())


def _moe_gather(x1b, chosen_t, rank_t, tables, tm):
    m = x1b.shape[0]
    nblk = m // MOE_BLOCK
    rows = _moe_max_tiles(nblk, tm) * tm + 2 * MOE_CAP
    blk = lambda j, *_: (j, 0)
    grid_spec = pltpu.PrefetchScalarGridSpec(
        num_scalar_prefetch=4,
        grid=(nblk,),
        in_specs=[pl.BlockSpec((MOE_BLOCK, D_MODEL), blk), pl.BlockSpec((LANES, MOE_BLOCK), blk),
                  pl.BlockSpec((LANES, MOE_BLOCK), blk),
                  pl.BlockSpec((1, SUBLANES, LANES), lambda j, *_: (j, 0, 0))],
        out_specs=pl.BlockSpec(memory_space=pl.ANY),
        scratch_shapes=[pltpu.VMEM((2, MOE_CAP, D_MODEL), BF16), pltpu.VMEM((MOE_SEG, D_MODEL), BF16),
                        pltpu.SemaphoreType.DMA((2,)), pltpu.SemaphoreType.DMA(())],
    )
    return pl.pallas_call(
        _moe_gather_kernel,
        grid_spec=grid_spec,
        out_shape=jax.ShapeDtypeStruct((rows, D_MODEL), BF16),
        compiler_params=_params("arbitrary"),
        name="moe_gather",
    )(tables["dst"], tables["nchunk"], tables["pad_row"], tables["pad_n"], x1b, chosen_t, rank_t, tables["bounds"])


def _moe_ffn_kernel(tile_expert_ref, tile_slot_ref, tile_next_ref, x_hbm, w1_hbm, w3_hbm, w2_hbm, y_ref,
                    x_buf, w1_buf, w3_buf, w2_buf, w13_scr, w2_scr, sem, x_sem):
    i = pl.program_id(0)
    n = pl.num_programs(0)
    expert = tile_expert_ref[i]
    slot = tile_slot_ref[i]
    ahead = MOE_ROW_SLOTS - 1
    tm = x_buf.shape[1]

    def row_copy(t):
        rows = pl.ds(pl.multiple_of(t * tm, tm), tm)
        return pltpu.make_async_copy(x_hbm.at[rows], x_buf.at[t % MOE_ROW_SLOTS], x_sem.at[t % MOE_ROW_SLOTS])

    @pl.when(i == 0)
    def _():
        for t in range(ahead):
            @pl.when(t < n)
            def _():
                row_copy(jnp.int32(t)).start()

    @pl.when(i + ahead < n)
    def _():
        row_copy(i + ahead).start()

    def weight_copies(e, s):
        pairs = ((w1_hbm, w1_buf), (w3_hbm, w3_buf), (w2_hbm, w2_buf))
        return [pltpu.make_async_copy(w.at[e], buf.at[s], sem.at[s, n]) for n, (w, buf) in enumerate(pairs)]

    @pl.when(i == 0)
    def _():
        for copy in weight_copies(expert, slot):
            copy.start()

    @pl.when((i == 0) | (expert != tile_expert_ref[jnp.maximum(i - 1, 0)]))
    def _():
        for copy in weight_copies(expert, slot):
            copy.wait()

        @pl.when(tile_next_ref[i] >= 0)
        def _():
            for copy in weight_copies(tile_next_ref[i], 1 - slot):
                copy.start(priority=1)

        w13_scr[:, :EXPERT_FF] = w1_buf[slot].astype(BF16)
        w13_scr[:, EXPERT_FF:] = w3_buf[slot].astype(BF16)
        w2_scr[...] = w2_buf[slot].astype(BF16)

    row_copy(i).wait()
    up = _dot(x_buf[i % MOE_ROW_SLOTS], w13_scr[...])
    hid = _silu(up[:, :EXPERT_FF]) * up[:, EXPERT_FF:]
    y_ref[...] = _dot(hid.astype(BF16), w2_scr[...]).astype(BF16)


def _moe_ffn(rows, w1, w3, w2, tables, tm):
    grid_spec = pltpu.PrefetchScalarGridSpec(
        num_scalar_prefetch=3,
        grid=(tables["tiles_used"][0],),
        in_specs=[pl.BlockSpec(memory_space=pl.ANY)] * 4,
        out_specs=pl.BlockSpec((tm, D_MODEL), lambda i, *_: (i, 0)),
        scratch_shapes=[pltpu.VMEM((MOE_ROW_SLOTS, tm, D_MODEL), BF16),
                        pltpu.VMEM((2, D_MODEL, EXPERT_FF), F32), pltpu.VMEM((2, D_MODEL, EXPERT_FF), F32),
                        pltpu.VMEM((2, EXPERT_FF, D_MODEL), F32),
                        pltpu.VMEM((D_MODEL, 2 * EXPERT_FF), BF16), pltpu.VMEM((EXPERT_FF, D_MODEL), BF16),
                        pltpu.SemaphoreType.DMA((2, 3)), pltpu.SemaphoreType.DMA((MOE_ROW_SLOTS,))],
    )
    return pl.pallas_call(
        _moe_ffn_kernel,
        grid_spec=grid_spec,
        out_shape=jax.ShapeDtypeStruct(rows.shape, BF16),
        compiler_params=_params("arbitrary"),
        name="moe_ffn",
    )(tables["tile_expert"], tables["tile_slot"], tables["tile_next"], rows, w1, w3, w2)


def _moe_combine_kernel(dst_ref, nchunk_ref, dense_ref, gates_ref, rank_ref, bounds_ref, g_ref, b_ref, rows_ref,
                        op_ref, os_ref, acc, buf, sem, *, prompt_tiles):
    j = pl.program_id(0)
    nblk = pl.num_programs(0)
    slot = j % 2
    nchunk_max = MOE_CAP // MOE_SEG

    def chunk_copy(blk, q, sl):
        row = pl.multiple_of(dst_ref[blk * nchunk_max + q], MOE_SEG)
        dst = buf.at[sl, pl.ds(pl.multiple_of(q * MOE_SEG, MOE_SEG), MOE_SEG), :]
        return pltpu.make_async_copy(rows_ref.at[pl.ds(row, MOE_SEG), :], dst, sem.at[sl])

    subs = range(MOE_CAP // MOE_SUB)
    always, rest = subs[:MOE_SUB_ALWAYS], subs[MOE_SUB_ALWAYS:]

    def sub_used(blk, sub):
        return True if sub < MOE_SUB_ALWAYS else sub * MOE_SUB < nchunk_ref[blk] * MOE_SEG

    def group(blk, sub, sl, action):
        for q in range(sub * (MOE_SUB // MOE_SEG), (sub + 1) * (MOE_SUB // MOE_SEG)):
            action(chunk_copy(blk, q, sl))

    @pl.when(j == 0)
    def _():
        buf[...] = jnp.zeros_like(buf)
        for sub in subs:
            @_when(sub_used(0, sub))
            def _():
                group(0, sub, 0, lambda c: c.start())

    for sub in subs:
        @_when(sub_used(j, sub))
        def _():
            group(j, sub, slot, lambda c: c.wait())

    bounds_t = jnp.concatenate([bounds_ref[0], jnp.zeros((LANES - SUBLANES, LANES), F32)], axis=0).T
    start, length = bounds_t[:, 0:1], bounds_t[:, 1:2]
    gates_rank = jnp.concatenate([gates_ref[...], rank_ref[...]], axis=0)

    def owners(sub):
        s = (sub * MOE_SUB + lax.broadcasted_iota(jnp.int32, (LANES, MOE_SUB), 1)).astype(F32)
        owner = (s >= start) & (s < start + length)
        within = s[0:1, :] - jnp.sum(jnp.where(owner, start, 0.0), axis=0, keepdims=True)
        return owner.astype(BF16), within

    def weights(owner_within):
        owner, within = owner_within
        hit = _dot(gates_rank, owner)
        return jnp.where(hit[MOE_BLOCK:] == within, hit[:MOE_BLOCK], 0.0).astype(BF16)

    def apply(sub, weight):
        return _dot(weight, buf[slot, sub * MOE_SUB:(sub + 1) * MOE_SUB, :])

    nxt = jnp.minimum(j + 1, nblk - 1)
    staged = [owners(sub) for sub in always]
    for sub in always:
        group(nxt, sub, 1 - slot, lambda c: c.start())
    staged = [weights(ow) for ow in staged]
    total = dense_ref[...]
    for sub, weight in zip(always, staged):
        total = total + apply(sub, weight)
    acc[...] = total
    for sub in rest:
        @pl.when((j + 1 < nblk) & sub_used(nxt, sub))
        def _():
            group(nxt, sub, 1 - slot, lambda c: c.start())

        @pl.when(sub_used(j, sub))
        def _():
            acc[...] += apply(sub, weights(owners(sub)))

    @pl.when(j == nblk - 1)
    def _():
        for sub in always:
            group(nxt, sub, 1 - slot, lambda c: c.wait())

    @pl.when(j < prompt_tiles)
    def _():
        op_ref[...] = _layer_norm(acc[...], g_ref[...], b_ref[...])

    @pl.when(j >= prompt_tiles)
    def _():
        os_ref[...] = _layer_norm(acc[...], g_ref[...], b_ref[...])


def _moe_combine(rows, dense, gates, rank, tables, g, b, n_prompt):
    m = dense.shape[0]
    nblk = m // MOE_BLOCK
    prompt_tiles = n_prompt // MOE_BLOCK
    in_prompt, in_sample = _group_maps(prompt_tiles)
    blk = lambda j, *_: (j, 0)
    const = lambda j, *_: (0, 0)
    grid_spec = pltpu.PrefetchScalarGridSpec(
        num_scalar_prefetch=2,
        grid=(nblk,),
        in_specs=[pl.BlockSpec((MOE_BLOCK, D_MODEL), blk), pl.BlockSpec((MOE_BLOCK, LANES), blk),
                  pl.BlockSpec((MOE_BLOCK, LANES), blk),
                  pl.BlockSpec((1, SUBLANES, LANES), lambda j, *_: (j, 0, 0)),
                  pl.BlockSpec((1, D_MODEL), const), pl.BlockSpec((1, D_MODEL), const),
                  pl.BlockSpec(memory_space=pl.ANY)],
        out_specs=[pl.BlockSpec((MOE_BLOCK, D_MODEL), in_prompt), pl.BlockSpec((MOE_BLOCK, D_MODEL), in_sample)],
        scratch_shapes=[pltpu.VMEM((MOE_BLOCK, D_MODEL), F32), pltpu.VMEM((2, MOE_CAP, D_MODEL), BF16),
                        pltpu.SemaphoreType.DMA((2,))],
    )
    return pl.pallas_call(
        functools.partial(_moe_combine_kernel, prompt_tiles=prompt_tiles),
        grid_spec=grid_spec,
        out_shape=[jax.ShapeDtypeStruct((n_prompt, D_MODEL), F32),
                   jax.ShapeDtypeStruct((m - n_prompt, D_MODEL), F32)],
        compiler_params=_params("arbitrary"),
        name="moe_combine",
    )(tables["src"], tables["nchunk"], dense, gates, rank, tables["bounds"], g, b, rows)


def _row(v):
    return v.reshape(1, -1).astype(F32)


def _scan_layout(batch, seq, row0):
    t = math.gcd(seq, CHUNK)
    nchunk = seq // t
    chunks = SCAN_CHUNKS if nchunk % SCAN_CHUNKS == 0 else 1
    seqs = SCAN_SEQS if (nchunk == 1 and batch % SCAN_SEQS == 0 and row0 % (SCAN_SEQS * t) == 0) else 1
    return dict(batch=batch, seq=seq, row0=row0, t=t, seqs=seqs, chunks=chunks)


def _lanes_at(v, start):
    return jnp.zeros((1, LANES), F32).at[0, start:start + v.shape[0]].set(v.astype(F32))


def kernel(x_prompt, x_sample, p_prompt, p_sample, state_ssm_conv, state_ssm, state_dn_conv, state_dn, emb_ln_g, emb_ln_b, w_in, conv_a_w, conv_a_b, ssm_dt_bias, ssm_a_log, ssm_d, ssm_norm_w, w_a, conv_b_w, dn_dt_bias, dn_a_log, dn_norm_w, w_b, w_o, ln1_g, ln1_b, router_w, router_bias, exp_w1, exp_w3, exp_w2, sh_w1, sh_w3, sh_w2, ple_w, ple_gate_w, ln2_g, ln2_b):
    bp, lp, _ = x_prompt.shape
    bs, ls, _ = x_sample.shape
    n_p = bp * lp
    n_s = bs * ls
    xp = x_prompt.reshape(n_p, D_MODEL)
    xs = x_sample.reshape(n_s, D_MODEL)

    w = w_in[0]
    o_z, o_xbc, o_dt = 0, SSM_INNER, SSM_INNER + SSM_CONV_CH
    o_qkv = o_dt + SSM_HEADS
    o_a = o_qkv + DN_CONV_CH
    o_b = o_a + DN_HEADS
    o_zb = o_b + DN_HEADS
    o_ga = o_zb + DN_V
    w_pa = w[:, o_z:o_dt].astype(BF16)
    w_pb = jnp.concatenate([w[:, o_qkv:o_a], w[:, o_zb:o_ga]], axis=1).astype(BF16)
    w_pg = w[:, o_ga:].astype(BF16)
    w_ps = jnp.zeros((D_MODEL, LANES), F32)
    w_ps = w_ps.at[:, SMALL_DT:SMALL_DT + SSM_HEADS].set(w[:, o_dt:o_qkv])
    w_ps = w_ps.at[:, SMALL_A:SMALL_A + 2 * DN_HEADS].set(w[:, o_a:o_zb]).astype(BF16)

    eg, eb = _row(emb_ln_g), _row(emb_ln_b)
    proj_a, small = _ln_matmul(xp, xs, eg, eb, w_pa, PROJ_TM, PROJ_TN_A, w_ps)
    proj_b = _ln_matmul(xp, xs, eg, eb, w_pb, PROJ_TM, PROJ_TN_B)

    ssd_consts = [conv_a_w[0], _row(conv_a_b[0]), _lanes_at(ssm_dt_bias[0], SMALL_DT),
                  _lanes_at(ssm_a_log[0], SMALL_DT), _row(jnp.repeat(ssm_d[0], SSM_HEAD_DIM)),
                  _row(ssm_norm_w[0])]
    gdn_consts = [conv_b_w[0], _lanes_at(dn_a_log[0], SMALL_A), _lanes_at(dn_dt_bias[0], SMALL_A),
                  _row(dn_norm_w[0])]
    zeros = lambda *s: jnp.zeros(s, F32)
    prompt = _scan_layout(bp, lp, 0)
    sample = _scan_layout(bs, ls, n_p)

    ya_p, pa_conv, pa_ssm = _ssd(proj_a, small, ssd_consts, zeros(bp, CONV_WIDTH - 1, SSM_CONV_CH),
                                 zeros(bp, SSM_HEADS, SSM_HEAD_DIM, SSM_STATE), **prompt)
    ya_s, sa_conv, sa_ssm = _ssd(proj_a, small, ssd_consts, state_ssm_conv[0], state_ssm[0], **sample)
    ob_p, pb_conv, pb_dn = _gdn(proj_b, small, gdn_consts, zeros(bp, CONV_WIDTH - 1, DN_CONV_CH),
                                zeros(bp, DN_HEADS, DN_HEAD, DN_HEAD), **prompt)
    ob_s, sb_conv, sb_dn = _gdn(proj_b, small, gdn_consts, state_dn_conv[0], state_dn[0], **sample)

    x1, x1b = _merge(
        xp, xs, ya_p, ya_s, ob_p, ob_s,
        [eg, eb, w_pg, w_a[0].astype(BF16), w_b[0].astype(BF16), w_o[0].astype(BF16), _row(ln1_g[0]), _row(ln1_b[0])],
        MERGE_TM)

    router_w_t = jnp.zeros((LANES, D_MODEL), F32).at[:N_EXPERTS].set(router_w[0].T)
    router_w_hi = router_w_t.astype(BF16)
    router_w_lo = (router_w_t - router_w_hi.astype(F32)).astype(BF16)
    router_b = jnp.broadcast_to(_lanes_at(router_bias[0], 0).reshape(LANES, 1), (LANES, MOE_BLOCK))
    dense, gates, rank, chosen_t, rank_t, counts = _dense_router(
        x1, x1b, p_prompt[0].reshape(n_p, PLE_DIM), p_sample[0].reshape(n_s, PLE_DIM),
        [sh_w1[0].astype(BF16), sh_w3[0].astype(BF16), sh_w2[0].astype(BF16), ple_w[0].astype(BF16),
         ple_gate_w[0].astype(BF16), router_w_hi, router_w_lo, router_b])

    tables = _moe_tables(counts[:, 0, :N_EXPERTS].astype(jnp.int32), MOE_TM)
    sorted_rows = _moe_gather(x1b, chosen_t, rank_t, tables, MOE_TM)
    expert_out = _moe_ffn(sorted_rows, exp_w1[0], exp_w3[0], exp_w2[0], tables, MOE_TM)
    out_p, out_s = _moe_combine(expert_out, dense, gates, rank, tables, _row(ln2_g[0]), _row(ln2_b[0]), n_p)

    return (out_p.reshape(bp, lp, D_MODEL), out_s.reshape(bs, ls, D_MODEL),
            pa_conv[None], pa_ssm[None], pb_conv[None], pb_dn[None],
            sa_conv[None], sa_ssm[None], sb_conv[None], sb_dn[None])
```

```python
import functools
import math

import jax
import jax.numpy as jnp
from jax import lax
from jax.experimental import pallas as pl
from jax.experimental.pallas import tpu as pltpu

F32 = jnp.float32
BF16 = jnp.bfloat16
HIGHEST = lax.Precision.HIGHEST

D_MODEL = 1024
SSM_INNER = 2048
SSM_HEAD_DIM = 64
SSM_HEADS = 32
SSM_GROUPS = 2
SSM_HEADS_PER_GROUP = 16
SSM_STATE = 128
SSM_CONV_CH = 2560
DN_HEADS = 8
DN_HEAD = 128
DN_QK = 1024
DN_V = 1024
DN_CONV_CH = 3072
CONV_WIDTH = 4
CHUNK = 64
N_EXPERTS = 64
TOP_K = 8
N_EXPERT_GROUPS = 8
EXPERTS_PER_GROUP = 8
TOPK_GROUPS = 4
EXPERT_FF = 256
SHARED_FF = 256
ROUTED_SCALE = 2.5
PLE_DIM = 256
LN_EPS = 1e-5
RMS_EPS = 1e-6
L2_EPS = 1e-6
DEEPNORM_ALPHA = 2.0 ** 0.25

LANES = 128
SUBLANES = 8
VMEM_LIMIT = 56 * 1024 * 1024
SMALL_DT = 0
SMALL_A = 32
SMALL_B = 40
PROJ_TM = 1024
PROJ_TN_A = 2304
PROJ_TN_B = 2048
MERGE_TM = 512
SCAN_CHUNKS = 8
SCAN_SEQS = 8
MOE_BLOCK = 256
MOE_SEG = 16
MOE_SUB = 512
MOE_CAP = -(-(MOE_BLOCK * TOP_K + N_EXPERTS * (MOE_SEG - 1)) // MOE_SUB) * MOE_SUB
MOE_TM = 512
MOE_ROW_SLOTS = 4
ROUTER_TM = 512
MOE_SUB_ALWAYS = -(-(MOE_BLOCK * TOP_K + N_EXPERTS * MOE_SEG // 2) // MOE_SUB)


def _sigmoid(x):
    return 1.0 / (1.0 + jnp.exp(-x))


def _silu(x):
    return x * _sigmoid(x)


def _softplus(x):
    return jnp.maximum(x, 0.0) + jnp.log(1.0 + jnp.exp(-jnp.abs(x)))


def _layer_norm(x, g, b):
    mu = jnp.mean(x, axis=-1, keepdims=True)
    xc = x - mu
    var = jnp.mean(xc * xc, axis=-1, keepdims=True)
    return xc * lax.rsqrt(var + LN_EPS) * g + b


def _dot(a, b):
    return jnp.dot(a, b, preferred_element_type=F32)


def _dot_nt(a, b):
    return lax.dot_general(a, b, (((1,), (1,)), ((), ())), preferred_element_type=F32)


def _dot_tn(a, b):
    return lax.dot_general(a, b, (((0,), (0,)), ((), ())), preferred_element_type=F32)


def _dot_f32(a, b):
    return jnp.dot(a, b, precision=HIGHEST, preferred_element_type=F32)


def _params(*sem):
    return pltpu.CompilerParams(dimension_semantics=sem, vmem_limit_bytes=VMEM_LIMIT)


def _when(cond):
    if cond is True:
        return lambda fn: fn()
    if cond is False:
        return lambda fn: None
    return pl.when(cond)


def _row_tile(m, preferred):
    return max(d for d in range(SUBLANES, min(m, preferred) + 1, SUBLANES) if m % d == 0)


def _group_maps(prompt_tiles):
    in_prompt = lambda i, *_: (jnp.minimum(i, prompt_tiles - 1), 0)
    in_sample = lambda i, *_: (jnp.maximum(i - prompt_tiles, 0), 0)
    return in_prompt, in_sample


def _ln_matmul_kernel(xp_ref, xs_ref, g_ref, b_ref, w_ref, *rest, prompt_tiles, narrow):
    if narrow:
        wn_ref, o_ref, on_ref, h_scr = rest
    else:
        o_ref, h_scr = rest

    @pl.when(pl.program_id(1) == 0)
    def _():
        x = jnp.where(pl.program_id(0) < prompt_tiles, xp_ref[...], xs_ref[...])
        h_scr[...] = _layer_norm(x, g_ref[...], b_ref[...]).astype(BF16)
        if narrow:
            on_ref[...] = _dot(h_scr[...], wn_ref[...])

    o_ref[...] = _dot(h_scr[...], w_ref[...]).astype(o_ref.dtype)


def _ln_matmul(xp, xs, g, b, w, tm, tn, w_narrow=None):
    k = xp.shape[1]
    m = xp.shape[0] + xs.shape[0]
    n = w.shape[1]
    tm = _row_tile(math.gcd(xp.shape[0], xs.shape[0]), tm)
    prompt_tiles = xp.shape[0] // tm
    in_prompt, in_sample = _group_maps(prompt_tiles)
    narrow = w_narrow is not None
    const = lambda i, j: (0, 0)
    row = lambda i, j: (i, 0)
    in_specs = [pl.BlockSpec((tm, k), in_prompt), pl.BlockSpec((tm, k), in_sample),
                pl.BlockSpec((1, k), const), pl.BlockSpec((1, k), const), pl.BlockSpec((k, tn), lambda i, j: (0, j))]
    out_specs = [pl.BlockSpec((tm, tn), lambda i, j: (i, j))]
    out_shape = [jax.ShapeDtypeStruct((m, n), BF16)]
    args = [xp, xs, g, b, w]
    if narrow:
        in_specs.append(pl.BlockSpec((k, LANES), const))
        out_specs.append(pl.BlockSpec((tm, LANES), row))
        out_shape.append(jax.ShapeDtypeStruct((m, LANES), F32))
        args.append(w_narrow)
    out = pl.pallas_call(
        functools.partial(_ln_matmul_kernel, prompt_tiles=prompt_tiles, narrow=narrow),
        grid=(m // tm, n // tn),
        in_specs=in_specs,
        out_specs=out_specs,
        out_shape=out_shape,
        scratch_shapes=[pltpu.VMEM((tm, k), BF16)],
        compiler_params=_params("parallel", "arbitrary"),
        name="ln_matmul",
    )(*args)
    return out if narrow else out[0]


def _load_conv_state(cbuf, state):
    taps = CONV_WIDTH - 1
    cbuf[0:SUBLANES - taps, :] = jnp.zeros((SUBLANES - taps, cbuf.shape[1]), F32)
    cbuf[SUBLANES - taps:SUBLANES, :] = state


def _causal_conv(cbuf, x16, cw, t):
    ch = x16.shape[1]
    prev = cbuf[...]
    taps = CONV_WIDTH - 1
    if t <= SUBLANES:
        x = x16.astype(F32)
        ext = jnp.concatenate([prev, x], axis=0)
        lo = SUBLANES - taps
        y = ext[lo:lo + t] * cw[0:1]
        y = y + ext[lo + 1:lo + 1 + t] * cw[1:2]
        y = y + ext[lo + 2:lo + 2 + t] * cw[2:3]
        y = y + x * cw[3:4]
        cbuf[...] = ext[t:t + SUBLANES]
        return y, ext[SUBLANES + t - taps:SUBLANES + t]
    hi = prev.astype(BF16).astype(F32)
    mid = (prev - hi).astype(BF16).astype(F32)
    low = ((prev - hi) - mid).astype(BF16).astype(F32)
    x = x16.astype(F32)
    pack = 2 * SUBLANES
    nhead = 4 * SUBLANES if t % pack == 0 else 5 * SUBLANES
    head = jnp.concatenate([hi, mid, low, jnp.zeros((nhead - 3 * SUBLANES, ch), F32)], axis=0)
    if t % pack == 0:
        ext = jnp.concatenate([head.astype(BF16), x16], axis=0)
    else:
        ext = jnp.concatenate([head, x], axis=0).astype(BF16)
    out_row = lax.broadcasted_iota(jnp.int32, (taps * t, nhead + t), 0)
    col = lax.broadcasted_iota(jnp.int32, (taps * t, nhead + t), 1)
    log_t = t.bit_length() - 1
    shift = lax.shift_right_logical(out_row, log_t) + 1
    src = (out_row & (t - 1)) - shift
    in_x = (src >= 0) & (col == nhead + src)
    in_prev = (src < 0) & (col < 3 * SUBLANES) & ((col & (SUBLANES - 1)) == SUBLANES + src)
    shifted = _dot((in_x | in_prev).astype(BF16), ext)
    y = shifted[2 * t:3 * t] * cw[0:1]
    y = y + shifted[t:2 * t] * cw[1:2]
    y = y + shifted[0:t] * cw[2:3]
    y = y + x * cw[3:4]
    cbuf[...] = x[t - SUBLANES:t]
    return y, x[t - taps:t]


def _lane_expand(v, h0, count, width):
    t = v.shape[0]
    n = count * width
    out = jnp.broadcast_to(v[:, h0:h0 + 1], (t, n))
    if count > 1:
        lane = lax.broadcasted_iota(jnp.int32, (t, n), 1)
        for i in range(1, count):
            out = jnp.where(lane >= i * width, jnp.broadcast_to(v[:, h0 + i:h0 + i + 1], (t, n)), out)
    return out


def _grouped_transpose(v, per_tile):
    t = v.shape[0]
    blocks = [v if r == 0 else pltpu.roll(v, LANES - r, axis=1) for r in range(per_tile)]
    if per_tile * t < LANES:
        blocks.append(jnp.zeros((LANES - per_tile * t, LANES), F32))
    return jnp.concatenate(blocks, axis=0).T


def _same_block(i, j, size):
    shift = size.bit_length() - 1
    return lax.shift_right_logical(i, shift) == lax.shift_right_logical(j, shift)


def _unit_lower_inverses(lmats, ii, jj, t):
    mm = lambda a, b: _dot(a.astype(BF16), b.astype(BF16))
    base = min(16, t)
    in_base = _same_block(ii, jj, base)
    eye = (ii == jj).astype(F32)
    power = [jnp.where(in_base, l, 0.0) for l in lmats]
    inv = [eye - p for p in power]
    span = 2
    while span < base:
        power = [mm(p, p) for p in power]
        inv = [a + mm(a, p) for a, p in zip(inv, power)]
        span *= 2
    size = base
    while size < t:
        link = _same_block(ii, jj, 2 * size) & jnp.logical_not(_same_block(ii, jj, size))
        cross = [mm(a, jnp.where(link, l, 0.0)) for a, l in zip(inv, lmats)]
        inv = [a - mm(c, a) for a, c in zip(inv, cross)]
        size *= 2
    return inv


def _row_sums_of_squares(blocks):
    t = blocks[0].shape[0]
    sq = jnp.concatenate([b * b for b in blocks], axis=0)
    hi = sq.astype(BF16)
    lo = (sq - hi.astype(F32)).astype(BF16)
    ones = jnp.ones((LANES, LANES), BF16)
    sums = _dot(hi, ones) + _dot(lo, ones)
    return [sums[i * t:(i + 1) * t] for i in range(len(blocks))]


def _cumsum_rows(v, t):
    ii = lax.broadcasted_iota(jnp.int32, (t, t), 0)
    jj = lax.broadcasted_iota(jnp.int32, (t, t), 1)
    return _dot_f32((jj <= ii).astype(F32), v)


def _ssd_kernel(pa_ref, sm_ref, cw_ref, cb_ref, dtb_ref, alog_ref, dskip_ref, nw_ref, cst_ref, h0_ref,
                y_ref, cnew_ref, hnew_ref, cbuf, s_scr, *, t, first, last):
    per_tile = LANES // t
    width = per_tile * SSM_HEAD_DIM

    @_when(first)
    def _():
        _load_conv_state(cbuf, cst_ref[0])
        s_scr[...] = h0_ref[0]

    z = pa_ref[:, :SSM_INNER].astype(F32)
    conv, tail = _causal_conv(cbuf, pa_ref[:, SSM_INNER:], cw_ref[...], t)

    @_when(last)
    def _():
        cnew_ref[0] = tail

    xbc = _silu(conv + cb_ref[...])
    xs = xbc[:, :SSM_INNER]
    bm = xbc[:, SSM_INNER:SSM_INNER + SSM_GROUPS * SSM_STATE]
    cm = xbc[:, SSM_INNER + SSM_GROUPS * SSM_STATE:]

    dt = _softplus(sm_ref[...] + dtb_ref[...])
    da = dt * (-jnp.exp(alog_ref[...]))
    cum = _cumsum_rows(da, t)
    cum_t = _grouped_transpose(cum, per_tile)
    dt_t = _grouped_transpose(dt, per_tile)

    row = lax.broadcasted_iota(jnp.int32, (t, LANES), 0)
    lane = lax.broadcasted_iota(jnp.int32, (t, LANES), 1)
    causal = (lane % t) <= row
    brow = lax.broadcasted_iota(jnp.int32, (LANES, width), 0)
    bcol = lax.broadcasted_iota(jnp.int32, (LANES, width), 1)
    blockdiag = (brow // t) == (bcol // SSM_HEAD_DIM)

    ys = []
    for g in range(SSM_GROUPS):
        bg = bm[:, g * SSM_STATE:(g + 1) * SSM_STATE]
        cg = cm[:, g * SSM_STATE:(g + 1) * SSM_STATE].astype(BF16)
        cb = _dot_nt(cg, jnp.concatenate([bg] * per_tile, axis=0).astype(BF16))
        bg = bg.astype(BF16)
        for q in range(SSM_HEADS_PER_GROUP // per_tile):
            h0 = g * SSM_HEADS_PER_GROUP + q * per_tile
            ccol = _lane_expand(cum, h0, per_tile, t)
            seg = ccol - cum_t[h0:h0 + 1, :]
            decay = jnp.exp(jnp.where(causal, seg, -jnp.inf))
            wts = (cb * decay * dt_t[h0:h0 + 1, :]).astype(BF16)
            xt = xs[:, h0 * SSM_HEAD_DIM:h0 * SSM_HEAD_DIM + width]
            xbd = jnp.where(blockdiag, jnp.concatenate([xt] * per_tile, axis=0), 0.0).astype(BF16)
            y_intra = _dot(wts, xbd)
            st = s_scr[h0:h0 + per_tile].reshape(width, SSM_STATE)
            cum_w = ccol if width == LANES else _lane_expand(cum, h0, per_tile, SSM_HEAD_DIM)
            y_state = _dot_nt(cg, st.astype(BF16)) * jnp.exp(cum_w)
            ys.append(y_intra + y_state)
            wend = jnp.exp(cum_w[t - 1:t, :] - cum_w) * _lane_expand(dt, h0, per_tile, SSM_HEAD_DIM)
            ds = _dot_tn((xt * wend).astype(BF16), bg)
            for i in range(per_tile):
                h = h0 + i
                s_scr[h] = jnp.exp(cum[t - 1:t, h:h + 1]) * s_scr[h] + ds[i * SSM_HEAD_DIM:(i + 1) * SSM_HEAD_DIM]

    y = jnp.concatenate(ys, axis=1) + dskip_ref[...] * xs
    y = y * _silu(z)
    half = SSM_INNER // SSM_GROUPS
    normed = []
    for g in range(SSM_GROUPS):
        yg = y[:, g * half:(g + 1) * half]
        normed.append(yg * lax.rsqrt(jnp.mean(yg * yg, axis=-1, keepdims=True) + RMS_EPS))
    y_ref[...] = (jnp.concatenate(normed, axis=1) * nw_ref[...]).astype(BF16)

    @_when(last)
    def _():
        hnew_ref[0] = s_scr[...]


def _scan_step_kernel(chunk_kernel, n_tok, n_const, n_state, seqs, chunks, t):
    def step(*refs):
        tok = refs[:n_tok]
        const = refs[n_tok:n_tok + n_const]
        st_in = refs[n_tok + n_const:n_tok + n_const + n_state]
        y_ref = refs[n_tok + n_const + n_state]
        st_out = refs[n_tok + n_const + n_state + 1:n_tok + n_const + 2 * n_state + 1]
        scratch = refs[n_tok + n_const + 2 * n_state + 1:]
        first_step = pl.program_id(1) == 0
        last_step = pl.program_id(1) == pl.num_programs(1) - 1
        for s in range(seqs):
            for c in range(chunks):
                rows = pl.ds((s * chunks + c) * t, t)
                chunk_kernel(*[r.at[rows, :] for r in tok], *const, *[r.at[pl.ds(s, 1)] for r in st_in],
                             y_ref.at[rows, :], *[r.at[pl.ds(s, 1)] for r in st_out], *scratch,
                             first=first_step if c == 0 else False,
                             last=last_step if c == chunks - 1 else False)
    return step


def _scan_call(kernel, name, tok_in, const_in, state_in, y_width, scratch, *, batch, seq, row0, t, seqs, chunks,
               step_kernel=None):
    nstep = seq // (t * chunks)
    rows = seqs * chunks * t
    blk0 = row0 // rows

    def per_batch(a):
        zeros = (0,) * (a.ndim - 1)
        return pl.BlockSpec((seqs,) + a.shape[1:], lambda b, c: (b,) + zeros)

    if step_kernel is None:
        step_kernel = _scan_step_kernel(kernel, len(tok_in), len(const_in), len(state_in), seqs, chunks, t)
    return pl.pallas_call(
        step_kernel,
        grid=(batch // seqs, nstep),
        in_specs=[pl.BlockSpec((rows, a.shape[1]), lambda b, c: (blk0 + b * nstep + c, 0)) for a in tok_in]
        + [pl.BlockSpec(a.shape, lambda b, c: (0, 0)) for a in const_in]
        + [per_batch(a) for a in state_in],
        out_specs=[pl.BlockSpec((rows, y_width), lambda b, c: (b * nstep + c, 0))]
        + [per_batch(a) for a in state_in],
        out_shape=[jax.ShapeDtypeStruct((batch * seq, y_width), BF16)]
        + [jax.ShapeDtypeStruct(a.shape, F32) for a in state_in],
        scratch_shapes=scratch,
        compiler_params=_params("parallel", "arbitrary"),
        name=name,
    )(*tok_in, *const_in, *state_in)


def _ssd(proj_a, small, consts, conv_state, h0, **where):
    t = where["t"]
    scratch = [pltpu.VMEM((SUBLANES, SSM_CONV_CH), F32),
               pltpu.VMEM((SSM_HEADS, SSM_HEAD_DIM, SSM_STATE), F32)]
    return _scan_call(functools.partial(_ssd_kernel, t=t), f"ssd_t{t}", [proj_a, small], consts,
                      [conv_state, h0], SSM_INNER, scratch, **where)


def _gdn_step_kernel(pb_ref, sm_ref, cw_ref, alog_ref, dtb_ref, nw_ref, cst_ref, s0_ref,
                     o_ref, cnew_ref, snew_ref, cbuf, s_scr, *, t, seqs, chunks):
    first_step = pl.program_id(1) == 0
    last_step = pl.program_id(1) == pl.num_programs(1) - 1
    units = [(s, c) for s in range(seqs) for c in range(chunks)]
    rows = lambda u: pl.ds(u * t, t)
    per_tile = min(LANES // t, DN_HEADS)
    n = per_tile * t
    ntile = DN_HEADS // per_tile
    heads = range(DN_HEADS)
    ii = lax.broadcasted_iota(jnp.int32, (n, n), 0)
    jj = lax.broadcasted_iota(jnp.int32, (n, n), 1)
    causal = _same_block(ii, jj, t) & (jj <= ii)
    diag = ii == jj
    stack = lambda xs, p: jnp.concatenate(xs[p * per_tile:(p + 1) * per_tile], axis=0)

    qkv, zb, gcum, beta = [], [], [], []
    for u, (s, c) in enumerate(units):
        @_when(first_step if c == 0 else False)
        def _():
            _load_conv_state(cbuf, cst_ref[s])

        conv, tail = _causal_conv(cbuf, pb_ref[rows(u), :DN_CONV_CH], cw_ref[...], t)

        @_when(last_step if c == chunks - 1 else False)
        def _():
            cnew_ref[s] = tail

        qkv.append(_silu(conv))
        zb.append(pb_ref[rows(u), DN_CONV_CH:].astype(F32))
        sm = sm_ref[rows(u), :]
        beta.append(_sigmoid(sm))
        gate = -jnp.exp(alog_ref[...]) * _softplus(sm + dtb_ref[...])
        gcum.append(_cumsum_rows(gate, t))
    gcum_t = [_grouped_transpose(g, per_tile) for g in gcum]

    sumsq = _row_sums_of_squares([x[:, i * DN_HEAD:(i + 1) * DN_HEAD] for x in qkv for i in range(2 * DN_HEADS)])
    q, k, v, gcol, bcol, kb, egcol = ([[] for _ in units] for _ in range(7))
    for u in range(len(units)):
        for h in heads:
            lo = h * DN_HEAD
            ss = sumsq[u * 2 * DN_HEADS:(u + 1) * 2 * DN_HEADS]
            q[u].append(qkv[u][:, lo:lo + DN_HEAD] * lax.rsqrt(ss[h] + L2_EPS) * (DN_HEAD ** -0.5))
            k[u].append(qkv[u][:, DN_QK + lo:DN_QK + lo + DN_HEAD] * lax.rsqrt(ss[DN_HEADS + h] + L2_EPS))
            v[u].append(qkv[u][:, 2 * DN_QK + lo:2 * DN_QK + lo + DN_HEAD])
            gcol[u].append(gcum[u][:, SMALL_A + h:SMALL_A + h + 1])
            bcol[u].append(beta[u][:, SMALL_B + h:SMALL_B + h + 1])
            kb[u].append(k[u][h] * bcol[u][h])
            egcol[u].append(jnp.exp(gcol[u][h]))

    tiles = [(u, p) for u in range(len(units)) for p in range(ntile)]
    k16 = [stack(k[u], p).astype(BF16) for u, p in tiles]
    decay = []
    for u, p in tiles:
        h0 = p * per_tile
        seg = stack(gcol[u], p) - gcum_t[u][SMALL_A + h0:SMALL_A + h0 + 1, :n]
        decay.append(jnp.exp(jnp.where(causal, seg, -jnp.inf)))
    lmat = [_dot_nt(stack(kb[u], p).astype(BF16), k16[i]) * jnp.where(diag, 0.0, decay[i])
            for i, (u, p) in enumerate(tiles)]
    qk = [(_dot_nt(stack(q[u], p).astype(BF16), k16[i]) * decay[i]).astype(BF16) for i, (u, p) in enumerate(tiles)]
    inv = [a.astype(BF16) for a in _unit_lower_inverses(lmat, ii, jj, t)]

    split = lambda x: (x[:, :DN_HEAD], x[:, DN_HEAD:])
    o_from, o_own, s_from, s_own, glast = ([[] for _ in units] for _ in range(5))
    aw = [_dot(inv[i], jnp.concatenate(
        [stack([kb[u][h] * egcol[u][h] for h in heads], p), stack([v[u][h] * bcol[u][h] for h in heads], p)],
        axis=1).astype(BF16)) for i, (u, p) in enumerate(tiles)]
    qaw = [_dot(qk[i], aw[i].astype(BF16)) for i in range(len(tiles))]
    pad_rows = -t % (2 * SUBLANES)
    pad = [jnp.zeros((pad_rows, DN_HEAD), F32)] if pad_rows else []
    for u in range(len(units)):
        for h in heads:
            p, j = divmod(h, per_tile)
            i = u * ntile + p
            sel = slice(j * t, (j + 1) * t)
            qkw, qku0 = split(qaw[i][sel])
            o_from[u].append(q[u][h] * egcol[u][h] - qkw)
            o_own[u].append(qku0)
            glast[u].append(gcum[u][t - 1:t, SMALL_A + h:SMALL_A + h + 1])
            kd = (k[u][h] * jnp.exp(glast[u][h] - gcol[u][h])).astype(BF16)
            kdw, kdu0 = split(_dot_tn(kd, aw[i][sel].astype(BF16)))
            s_from[u].append(kdw)
            s_own[u].append(kdu0)

    outs = []
    for u, (s, c) in enumerate(units):
        @_when(first_step if c == 0 else False)
        def _():
            s_scr[...] = s0_ref[s]

        for h in heads:
            state = s_scr[h]
            both = _dot(jnp.concatenate([s_from[u][h], o_from[u][h]] + pad, axis=0).astype(BF16), state.astype(BF16))
            outs.append(both[DN_HEAD:DN_HEAD + t] + o_own[u][h])
            s_scr[h] = jnp.exp(glast[u][h]) * state - both[:DN_HEAD] + s_own[u][h]

        @_when(last_step if c == chunks - 1 else False)
        def _():
            snew_ref[s] = s_scr[...]

    norm_w = nw_ref[...]
    osq = _row_sums_of_squares(outs)
    for u in range(len(units)):
        normed = []
        for h in heads:
            i = u * DN_HEADS + h
            normed.append(outs[i] * lax.rsqrt(osq[i] * (1.0 / DN_HEAD) + RMS_EPS) * norm_w
                          * _silu(zb[u][:, h * DN_HEAD:(h + 1) * DN_HEAD]))
        o_ref[rows(u), :] = jnp.concatenate(normed, axis=1).astype(BF16)


def _gdn(proj_b, small, consts, conv_state, s0, **where):
    t = where["t"]
    scratch = [pltpu.VMEM((SUBLANES, DN_CONV_CH), F32),
               pltpu.VMEM((DN_HEADS, DN_HEAD, DN_HEAD), F32)]
    step = functools.partial(_gdn_step_kernel, t=t, seqs=where["seqs"], chunks=where["chunks"])
    return _scan_call(None, f"gdn_t{t}", [proj_b, small], consts, [conv_state, s0], DN_V, scratch,
                      step_kernel=step, **where)


def _merge_kernel(xp_ref, xs_ref, yap_ref, yas_ref, obp_ref, obs_ref, eg_ref, eb_ref, wg_ref, wa_ref, wb_ref,
                  wo_ref, g_ref, b_ref, x1_ref, x1b_ref, *, prompt_tiles):
    is_prompt = pl.program_id(0) < prompt_tiles
    h = _layer_norm(jnp.where(is_prompt, xp_ref[...], xs_ref[...]), eg_ref[...], eb_ref[...])
    out_a = _dot(jnp.where(is_prompt, yap_ref[...], yas_ref[...]), wa_ref[...])
    out_b = _dot(jnp.where(is_prompt, obp_ref[...], obs_ref[...]), wb_ref[...])
    gt = _dot(h.astype(BF16), wg_ref[...])
    merged = _sigmoid(gt[:, :D_MODEL]) * out_a + _sigmoid(gt[:, D_MODEL:]) * out_b
    mix = _dot(merged.astype(BF16), wo_ref[...])
    x1 = _layer_norm(DEEPNORM_ALPHA * h + mix, g_ref[...], b_ref[...])
    x1_ref[...] = x1
    x1b_ref[...] = x1.astype(BF16)


def _merge(xp, xs, ya_p, ya_s, ob_p, ob_s, consts, tm):
    m = xp.shape[0] + xs.shape[0]
    n_p = xp.shape[0]
    tm = _row_tile(math.gcd(n_p, m - n_p), tm)
    in_prompt, in_sample = _group_maps(n_p // tm)
    row = lambda i: (i, 0)
    const = lambda i: (0, 0)
    return pl.pallas_call(
        functools.partial(_merge_kernel, prompt_tiles=n_p // tm),
        grid=(m // tm,),
        in_specs=[pl.BlockSpec((tm, D_MODEL), in_prompt), pl.BlockSpec((tm, D_MODEL), in_sample),
                  pl.BlockSpec((tm, SSM_INNER), in_prompt), pl.BlockSpec((tm, SSM_INNER), in_sample),
                  pl.BlockSpec((tm, DN_V), in_prompt), pl.BlockSpec((tm, DN_V), in_sample)]
        + [pl.BlockSpec(a.shape, const) for a in consts],
        out_specs=[pl.BlockSpec((tm, D_MODEL), row), pl.BlockSpec((tm, D_MODEL), row)],
        out_shape=[jax.ShapeDtypeStruct((m, D_MODEL), F32), jax.ShapeDtypeStruct((m, D_MODEL), BF16)],
        compiler_params=_params("parallel"),
        name="merge",
    )(xp, xs, ya_p, ya_s, ob_p, ob_s, *consts)


def _take_first_max(rest, index, limit, axis):
    best = jnp.max(rest, axis=axis, keepdims=True)
    first = jnp.min(jnp.where(rest == best, index, limit), axis=axis, keepdims=True)
    hit = index == first
    return hit, jnp.where(hit, -jnp.inf, rest)


def _router_gates_t(x1, xb, rw_hi, rw_lo, bias):
    tm = x1.shape[0]
    x_lo = (x1 - xb.astype(F32)).astype(BF16)
    logits = _dot_nt(rw_hi, xb) + (_dot_nt(rw_hi, x_lo) + _dot_nt(rw_lo, xb))
    scores = _sigmoid(logits)
    row = lax.broadcasted_iota(jnp.int32, (LANES, tm), 0)
    sel = jnp.where(row < N_EXPERTS, scores + bias, -jnp.inf)
    by_group = sel.reshape(LANES // EXPERTS_PER_GROUP, EXPERTS_PER_GROUP, tm)
    sub = lax.broadcasted_iota(jnp.int32, by_group.shape, 1)
    top1 = jnp.max(by_group, axis=1, keepdims=True)
    _, others = _take_first_max(by_group, sub, EXPERTS_PER_GROUP, 1)
    gscore = (top1 + jnp.max(others, axis=1, keepdims=True))[:N_EXPERT_GROUPS]
    gidx = lax.broadcasted_iota(jnp.int32, gscore.shape, 0)
    gkeep = jnp.zeros(gscore.shape, jnp.bool_)
    for _ in range(TOPK_GROUPS):
        hit, gscore = _take_first_max(gscore, gidx, N_EXPERT_GROUPS, 0)
        gkeep = gkeep | hit
    gkeep = jnp.broadcast_to(gkeep, (N_EXPERT_GROUPS, EXPERTS_PER_GROUP, tm)).reshape(N_EXPERTS, tm)
    rest = jnp.where(gkeep, sel[:N_EXPERTS], -jnp.inf)
    eidx = lax.broadcasted_iota(jnp.int32, (N_EXPERTS, tm), 0)
    keep = jnp.zeros((N_EXPERTS, tm), jnp.bool_)
    for _ in range(TOP_K):
        hit, rest = _take_first_max(rest, eidx, N_EXPERTS, 0)
        keep = keep | hit
    picked = jnp.where(keep, scores[:N_EXPERTS], 0.0)
    gates = picked / jnp.sum(picked, axis=0, keepdims=True) * ROUTED_SCALE
    return jnp.concatenate([gates, jnp.zeros((LANES - N_EXPERTS, tm), F32)], axis=0)


def _dense_router_kernel(x1_ref, x1b_ref, pp_ref, ps_ref, sw1_ref, sw3_ref, sw2_ref, pw_ref, pg_ref, rwh_ref,
                         rwl_ref, rb_ref, dense_ref, gates_ref, rank_ref, chosen_t_ref, rank_t_ref, count_ref,
                         *, prompt_tiles):
    xb = x1b_ref[...]
    hid = _silu(_dot(xb, sw1_ref[...])) * _dot(xb, sw3_ref[...])
    shared = _dot(hid.astype(BF16), sw2_ref[...])
    p = jnp.where(pl.program_id(0) < prompt_tiles, pp_ref[...], ps_ref[...])
    ple = _dot(p.astype(BF16), pw_ref[...]) * _sigmoid(_dot(xb, pg_ref[...]))
    x1 = x1_ref[...]
    dense_ref[...] = DEEPNORM_ALPHA * x1 + shared + ple
    ii = lax.broadcasted_iota(jnp.int32, (MOE_BLOCK, MOE_BLOCK), 0)
    jj = lax.broadcasted_iota(jnp.int32, (MOE_BLOCK, MOE_BLOCK), 1)
    earlier = (ii < jj).astype(BF16)
    for blk in range(x1.shape[0] // MOE_BLOCK):
        tok = slice(blk * MOE_BLOCK, (blk + 1) * MOE_BLOCK)
        exp = slice(blk * LANES, (blk + 1) * LANES)
        gates_t = _router_gates_t(x1[tok], xb[tok], rwh_ref[...], rwl_ref[...], rb_ref[...])
        chosen_t = (gates_t > 0.0).astype(BF16)
        rank_t = _dot(chosen_t, earlier)
        chosen_t_ref[exp, :] = chosen_t
        rank_t_ref[exp, :] = rank_t.astype(BF16)
        gates = gates_t.T
        rank = rank_t.T
        gates_ref[tok, :] = gates.astype(BF16)
        rank_ref[tok, :] = rank.astype(BF16)
        count = rank[MOE_BLOCK - 1:, :] + (gates[MOE_BLOCK - 1:, :] > 0.0).astype(F32)
        count_ref[blk] = jnp.broadcast_to(count, (SUBLANES, LANES))


def _dense_router(x1, x1b, p_prompt, p_sample, consts):
    m = x1.shape[0]
    tm = _row_tile(math.gcd(p_prompt.shape[0], p_sample.shape[0]), ROUTER_TM)
    assert tm % MOE_BLOCK == 0
    nblk = m // MOE_BLOCK
    per_step = tm // MOE_BLOCK
    prompt_tiles = p_prompt.shape[0] // tm
    in_prompt, in_sample = _group_maps(prompt_tiles)
    row = lambda i: (i, 0)
    const = lambda i: (0, 0)
    return pl.pallas_call(
        functools.partial(_dense_router_kernel, prompt_tiles=prompt_tiles),
        grid=(m // tm,),
        in_specs=[pl.BlockSpec((tm, D_MODEL), row), pl.BlockSpec((tm, D_MODEL), row),
                  pl.BlockSpec((tm, PLE_DIM), in_prompt), pl.BlockSpec((tm, PLE_DIM), in_sample)]
        + [pl.BlockSpec(a.shape, const) for a in consts],
        out_specs=[pl.BlockSpec((tm, D_MODEL), row), pl.BlockSpec((tm, LANES), row),
                   pl.BlockSpec((tm, LANES), row), pl.BlockSpec((per_step * LANES, MOE_BLOCK), row),
                   pl.BlockSpec((per_step * LANES, MOE_BLOCK), row),
                   pl.BlockSpec((per_step, SUBLANES, LANES), lambda i: (i, 0, 0))],
        out_shape=[jax.ShapeDtypeStruct((m, D_MODEL), F32), jax.ShapeDtypeStruct((m, LANES), BF16),
                   jax.ShapeDtypeStruct((m, LANES), BF16), jax.ShapeDtypeStruct((nblk * LANES, MOE_BLOCK), BF16),
                   jax.ShapeDtypeStruct((nblk * LANES, MOE_BLOCK), BF16),
                   jax.ShapeDtypeStruct((nblk, SUBLANES, LANES), F32)],
        compiler_params=_params("parallel"),
        name="dense_router",
    )(x1, x1b, p_prompt, p_sample, *consts)


def _moe_tables(counts, tm):
    nblk = counts.shape[0]
    nchunk_max = MOE_CAP // MOE_SEG
    padded = (counts + MOE_SEG - 1) // MOE_SEG * MOE_SEG
    start = jnp.cumsum(padded, axis=1) - padded
    used = jnp.sum(padded, axis=1)
    rows_e = jnp.sum(padded, axis=0)
    rows_e_t = (rows_e + tm - 1) // tm * tm
    base_e = jnp.cumsum(rows_e_t) - rows_e_t
    seg_row = base_e[None, :] + jnp.cumsum(padded, axis=0) - padded
    chunk0 = jnp.arange(nchunk_max, dtype=jnp.int32) * MOE_SEG
    owner = jnp.sum((start + padded)[:, None, :] <= chunk0[None, :, None], axis=-1)
    onehot = owner[..., None] == jnp.arange(N_EXPERTS, dtype=jnp.int32)
    dst = chunk0[None, :] + jnp.sum(jnp.where(onehot, (seg_row - start)[:, None, :], 0), axis=-1)
    n_tiles = _moe_max_tiles(nblk, tm)
    live = chunk0[None, :] < used[:, None]
    parity = (jnp.arange(nblk, dtype=jnp.int32) % 2)[:, None]
    spare = n_tiles * tm + parity * MOE_CAP + chunk0[None, :]
    src = jnp.where(live, dst, 0)
    dst = jnp.where(live, dst, spare)
    tile_row = jnp.arange(n_tiles, dtype=jnp.int32) * tm
    tiles_used = jnp.sum(rows_e_t) // tm
    tile_row = jnp.minimum(tile_row, (tiles_used - 1) * tm)
    tile_expert = jnp.minimum(jnp.sum((base_e + rows_e_t)[None, :] <= tile_row[:, None], axis=-1), N_EXPERTS - 1)
    in_use = rows_e > 0
    experts = jnp.arange(N_EXPERTS, dtype=jnp.int32)
    later = jnp.where(in_use[None, :] & (experts[None, :] > experts[:, None]), experts[None, :], N_EXPERTS)
    next_in_use = jnp.min(later, axis=1)
    next_in_use = jnp.where(next_in_use == N_EXPERTS, -1, next_in_use)
    order = jnp.sum(in_use[None, :] & (experts[None, :] < experts[:, None]), axis=1)
    of_tile = tile_expert[:, None] == experts[None, :]
    tile_slot = jnp.sum(jnp.where(of_tile, (order % 2)[None, :], 0), axis=1)
    tile_next = jnp.sum(jnp.where(of_tile, next_in_use[None, :], 0), axis=1)
    bounds = jnp.zeros((nblk, SUBLANES, LANES), F32)
    bounds = bounds.at[:, 0, :N_EXPERTS].set(start.astype(F32)).at[:, 1, :N_EXPERTS].set(padded.astype(F32))
    i32 = lambda a: a.astype(jnp.int32)
    return dict(dst=i32(dst.reshape(-1)), src=i32(src.reshape(-1)), nchunk=i32(used // MOE_SEG),
                pad_row=i32(base_e + rows_e),
                pad_n=i32((rows_e_t - rows_e) // MOE_SEG), tile_expert=i32(tile_expert),
                tile_slot=i32(tile_slot), tile_next=i32(tile_next),
                tiles_used=i32(tiles_used.reshape(1)), bounds=bounds)


def _moe_max_tiles(nblk, tm):
    return (nblk * (MOE_BLOCK * TOP_K + N_EXPERTS * (MOE_SEG - 1)) + N_EXPERTS * (tm - MOE_SEG) + tm - 1) // tm


def _slot_bounds(bounds):
    return bounds[0:1, :], bounds[1:2, :]


def _moe_gather_kernel(dst_ref, nchunk_ref, pad_row_ref, pad_n_ref,
                       x_ref, chosen_t_ref, rank_t_ref, bounds_ref, rows_ref, buf, zbuf, sem, zsem):
    j = pl.program_id(0)
    nblk = pl.num_programs(0)
    slot = j % 2
    nchunk_max = MOE_CAP // MOE_SEG

    def chunk_copy(blk, q, sl):
        src = buf.at[sl, pl.ds(pl.multiple_of(q * MOE_SEG, MOE_SEG), MOE_SEG), :]
        row = pl.multiple_of(dst_ref[blk * nchunk_max + q], MOE_SEG)
        return pltpu.make_async_copy(src, rows_ref.at[pl.ds(row, MOE_SEG), :], sem.at[sl])

    def pad_copy(e, i):
        row = pl.multiple_of(pad_row_ref[e] + i * MOE_SEG, MOE_SEG)
        return pltpu.make_async_copy(zbuf, rows_ref.at[pl.ds(row, MOE_SEG), :], zsem)

    def sub_chunks(blk, sub, sl, action):
        for q in range(sub * (MOE_SUB // MOE_SEG), (sub + 1) * (MOE_SUB // MOE_SEG)):
            action(chunk_copy(blk, q, sl))

    def sub_used(blk, sub):
        return True if sub < MOE_SUB_ALWAYS else sub * MOE_SUB < nchunk_ref[blk] * MOE_SEG

    def for_pads(action):
        def per_expert(e, carry):
            def body(i, c):
                action(pad_copy(e, i))
                return c
            return lax.fori_loop(0, pad_n_ref[e], body, carry)
        lax.fori_loop(0, N_EXPERTS, per_expert, 0)

    @pl.when(j == 0)
    def _():
        zbuf[...] = jnp.zeros_like(zbuf)
        for_pads(lambda c: c.start())

    start, length = _slot_bounds(bounds_ref[0])
    x = x_ref[...]
    chosen_rank_t = jnp.concatenate([chosen_t_ref[...], rank_t_ref[...]], axis=1)
    before = jnp.maximum(j - 2, 0)
    subs = range(MOE_CAP // MOE_SUB)

    @pl.when(j >= 2)
    def _():
        for sub in subs:
            @_when(sub_used(before, sub))
            def _():
                sub_chunks(before, sub, slot, lambda c: c.wait())

    def owners(sub):
        s = (sub * MOE_SUB + lax.broadcasted_iota(jnp.int32, (MOE_SUB, LANES), 0)).astype(F32)
        owner = (s >= start) & (s < start + length)
        within = s[:, 0:1] - jnp.sum(jnp.where(owner, start, 0.0), axis=1, keepdims=True)
        return owner.astype(BF16), within

    def picks(owner_within):
        owner, within = owner_within
        hit = _dot(owner, chosen_rank_t)
        return ((hit[:, :MOE_BLOCK] > 0.5) & (hit[:, MOE_BLOCK:] == within)).astype(BF16)

    def emit(sub, pick):
        buf[slot, sub * MOE_SUB:(sub + 1) * MOE_SUB, :] = _dot(pick, x).astype(BF16)
        sub_chunks(j, sub, slot, lambda c: c.start())

    staged = [picks(ow) for ow in [owners(sub) for sub in subs[:MOE_SUB_ALWAYS]]]
    for sub, pick in enumerate(staged):
        emit(sub, pick)
    for sub in subs[MOE_SUB_ALWAYS:]:
        @pl.when(sub_used(j, sub))
        def _():
            emit(sub, picks(owners(sub)))

    @pl.when(j == nblk - 1)
    def _():
        prev = jnp.maximum(j - 1, 0)
        for sub in subs:
            @_when(sub_used(j, sub))
            def _():
                sub_chunks(j, sub, slot, lambda c: c.wait())

            @_when((j >= 1) & sub_used(prev, sub))
            def _():
                sub_chunks(prev, sub, 1 - slot, lambda c: c.wait())

        for_pads(lambda c: c.wait())


def _moe_gather(x1b, chosen_t, rank_t, tables, tm):
    m = x1b.shape[0]
    nblk = m // MOE_BLOCK
    rows = _moe_max_tiles(nblk, tm) * tm + 2 * MOE_CAP
    blk = lambda j, *_: (j, 0)
    grid_spec = pltpu.PrefetchScalarGridSpec(
        num_scalar_prefetch=4,
        grid=(nblk,),
        in_specs=[pl.BlockSpec((MOE_BLOCK, D_MODEL), blk), pl.BlockSpec((LANES, MOE_BLOCK), blk),
                  pl.BlockSpec((LANES, MOE_BLOCK), blk),
                  pl.BlockSpec((1, SUBLANES, LANES), lambda j, *_: (j, 0, 0))],
        out_specs=pl.BlockSpec(memory_space=pl.ANY),
        scratch_shapes=[pltpu.VMEM((2, MOE_CAP, D_MODEL), BF16), pltpu.VMEM((MOE_SEG, D_MODEL), BF16),
                        pltpu.SemaphoreType.DMA((2,)), pltpu.SemaphoreType.DMA(())],
    )
    return pl.pallas_call(
        _moe_gather_kernel,
        grid_spec=grid_spec,
        out_shape=jax.ShapeDtypeStruct((rows, D_MODEL), BF16),
        compiler_params=_params("arbitrary"),
        name="moe_gather",
    )(tables["dst"], tables["nchunk"], tables["pad_row"], tables["pad_n"], x1b, chosen_t, rank_t, tables["bounds"])


def _moe_ffn_kernel(tile_expert_ref, tile_slot_ref, tile_next_ref, x_hbm, w1_hbm, w3_hbm, w2_hbm, y_ref,
                    x_buf, w1_buf, w3_buf, w2_buf, w13_scr, w2_scr, sem, x_sem):
    i = pl.program_id(0)
    n = pl.num_programs(0)
    expert = tile_expert_ref[i]
    slot = tile_slot_ref[i]
    ahead = MOE_ROW_SLOTS - 1
    tm = x_buf.shape[1]

    def row_copy(t):
        rows = pl.ds(pl.multiple_of(t * tm, tm), tm)
        return pltpu.make_async_copy(x_hbm.at[rows], x_buf.at[t % MOE_ROW_SLOTS], x_sem.at[t % MOE_ROW_SLOTS])

    @pl.when(i == 0)
    def _():
        for t in range(ahead):
            @pl.when(t < n)
            def _():
                row_copy(jnp.int32(t)).start()

    @pl.when(i + ahead < n)
    def _():
        row_copy(i + ahead).start()

    def weight_copies(e, s):
        pairs = ((w1_hbm, w1_buf), (w3_hbm, w3_buf), (w2_hbm, w2_buf))
        return [pltpu.make_async_copy(w.at[e], buf.at[s], sem.at[s, n]) for n, (w, buf) in enumerate(pairs)]

    @pl.when(i == 0)
    def _():
        for copy in weight_copies(expert, slot):
            copy.start()

    @pl.when((i == 0) | (expert != tile_expert_ref[jnp.maximum(i - 1, 0)]))
    def _():
        for copy in weight_copies(expert, slot):
            copy.wait()

        @pl.when(tile_next_ref[i] >= 0)
        def _():
            for copy in weight_copies(tile_next_ref[i], 1 - slot):
                copy.start()

        w13_scr[:, :EXPERT_FF] = w1_buf[slot].astype(BF16)
        w13_scr[:, EXPERT_FF:] = w3_buf[slot].astype(BF16)
        w2_scr[...] = w2_buf[slot].astype(BF16)

    row_copy(i).wait()
    up = _dot(x_buf[i % MOE_ROW_SLOTS], w13_scr[...])
    hid = _silu(up[:, :EXPERT_FF]) * up[:, EXPERT_FF:]
    y_ref[...] = _dot(hid.astype(BF16), w2_scr[...]).astype(BF16)


def _moe_ffn(rows, w1, w3, w2, tables, tm):
    grid_spec = pltpu.PrefetchScalarGridSpec(
        num_scalar_prefetch=3,
        grid=(tables["tiles_used"][0],),
        in_specs=[pl.BlockSpec(memory_space=pl.ANY)] * 4,
        out_specs=pl.BlockSpec((tm, D_MODEL), lambda i, *_: (i, 0)),
        scratch_shapes=[pltpu.VMEM((MOE_ROW_SLOTS, tm, D_MODEL), BF16),
                        pltpu.VMEM((2, D_MODEL, EXPERT_FF), F32), pltpu.VMEM((2, D_MODEL, EXPERT_FF), F32),
                        pltpu.VMEM((2, EXPERT_FF, D_MODEL), F32),
                        pltpu.VMEM((D_MODEL, 2 * EXPERT_FF), BF16), pltpu.VMEM((EXPERT_FF, D_MODEL), BF16),
                        pltpu.SemaphoreType.DMA((2, 3)), pltpu.SemaphoreType.DMA((MOE_ROW_SLOTS,))],
    )
    return pl.pallas_call(
        _moe_ffn_kernel,
        grid_spec=grid_spec,
        out_shape=jax.ShapeDtypeStruct(rows.shape, BF16),
        compiler_params=_params("arbitrary"),
        name="moe_ffn",
    )(tables["tile_expert"], tables["tile_slot"], tables["tile_next"], rows, w1, w3, w2)


def _moe_combine_kernel(dst_ref, nchunk_ref, dense_ref, gates_ref, rank_ref, bounds_ref, g_ref, b_ref, rows_ref,
                        op_ref, os_ref, acc, buf, sem, *, prompt_tiles):
    j = pl.program_id(0)
    nblk = pl.num_programs(0)
    slot = j % 2
    nchunk_max = MOE_CAP // MOE_SEG

    def chunk_copy(blk, q, sl):
        row = pl.multiple_of(dst_ref[blk * nchunk_max + q], MOE_SEG)
        dst = buf.at[sl, pl.ds(pl.multiple_of(q * MOE_SEG, MOE_SEG), MOE_SEG), :]
        return pltpu.make_async_copy(rows_ref.at[pl.ds(row, MOE_SEG), :], dst, sem.at[sl])

    subs = range(MOE_CAP // MOE_SUB)
    always, rest = subs[:MOE_SUB_ALWAYS], subs[MOE_SUB_ALWAYS:]

    def sub_used(blk, sub):
        return True if sub < MOE_SUB_ALWAYS else sub * MOE_SUB < nchunk_ref[blk] * MOE_SEG

    def group(blk, sub, sl, action):
        for q in range(sub * (MOE_SUB // MOE_SEG), (sub + 1) * (MOE_SUB // MOE_SEG)):
            action(chunk_copy(blk, q, sl))

    @pl.when(j == 0)
    def _():
        buf[...] = jnp.zeros_like(buf)
        for sub in subs:
            @_when(sub_used(0, sub))
            def _():
                group(0, sub, 0, lambda c: c.start())

    for sub in subs:
        @_when(sub_used(j, sub))
        def _():
            group(j, sub, slot, lambda c: c.wait())

    bounds_t = jnp.concatenate([bounds_ref[0], jnp.zeros((LANES - SUBLANES, LANES), F32)], axis=0).T
    start, length = bounds_t[:, 0:1], bounds_t[:, 1:2]
    gates_rank = jnp.concatenate([gates_ref[...], rank_ref[...]], axis=0)

    def owners(sub):
        s = (sub * MOE_SUB + lax.broadcasted_iota(jnp.int32, (LANES, MOE_SUB), 1)).astype(F32)
        owner = (s >= start) & (s < start + length)
        within = s[0:1, :] - jnp.sum(jnp.where(owner, start, 0.0), axis=0, keepdims=True)
        return owner.astype(BF16), within

    def weights(owner_within):
        owner, within = owner_within
        hit = _dot(gates_rank, owner)
        return jnp.where(hit[MOE_BLOCK:] == within, hit[:MOE_BLOCK], 0.0).astype(BF16)

    def apply(sub, weight):
        return _dot(weight, buf[slot, sub * MOE_SUB:(sub + 1) * MOE_SUB, :])

    nxt = jnp.minimum(j + 1, nblk - 1)
    staged = [owners(sub) for sub in always]
    for sub in always:
        group(nxt, sub, 1 - slot, lambda c: c.start())
    staged = [weights(ow) for ow in staged]
    total = dense_ref[...]
    for sub, weight in zip(always, staged):
        total = total + apply(sub, weight)
    acc[...] = total
    for sub in rest:
        @pl.when((j + 1 < nblk) & sub_used(nxt, sub))
        def _():
            group(nxt, sub, 1 - slot, lambda c: c.start())

        @pl.when(sub_used(j, sub))
        def _():
            acc[...] += apply(sub, weights(owners(sub)))

    @pl.when(j == nblk - 1)
    def _():
        for sub in always:
            group(nxt, sub, 1 - slot, lambda c: c.wait())

    @pl.when(j < prompt_tiles)
    def _():
        op_ref[...] = _layer_norm(acc[...], g_ref[...], b_ref[...])

    @pl.when(j >= prompt_tiles)
    def _():
        os_ref[...] = _layer_norm(acc[...], g_ref[...], b_ref[...])


def _moe_combine(rows, dense, gates, rank, tables, g, b, n_prompt):
    m = dense.shape[0]
    nblk = m // MOE_BLOCK
    prompt_tiles = n_prompt // MOE_BLOCK
    in_prompt, in_sample = _group_maps(prompt_tiles)
    blk = lambda j, *_: (j, 0)
    const = lambda j, *_: (0, 0)
    grid_spec = pltpu.PrefetchScalarGridSpec(
        num_scalar_prefetch=2,
        grid=(nblk,),
        in_specs=[pl.BlockSpec((MOE_BLOCK, D_MODEL), blk), pl.BlockSpec((MOE_BLOCK, LANES), blk),
                  pl.BlockSpec((MOE_BLOCK, LANES), blk),
                  pl.BlockSpec((1, SUBLANES, LANES), lambda j, *_: (j, 0, 0)),
                  pl.BlockSpec((1, D_MODEL), const), pl.BlockSpec((1, D_MODEL), const),
                  pl.BlockSpec(memory_space=pl.ANY)],
        out_specs=[pl.BlockSpec((MOE_BLOCK, D_MODEL), in_prompt), pl.BlockSpec((MOE_BLOCK, D_MODEL), in_sample)],
        scratch_shapes=[pltpu.VMEM((MOE_BLOCK, D_MODEL), F32), pltpu.VMEM((2, MOE_CAP, D_MODEL), BF16),
                        pltpu.SemaphoreType.DMA((2,))],
    )
    return pl.pallas_call(
        functools.partial(_moe_combine_kernel, prompt_tiles=prompt_tiles),
        grid_spec=grid_spec,
        out_shape=[jax.ShapeDtypeStruct((n_prompt, D_MODEL), F32),
                   jax.ShapeDtypeStruct((m - n_prompt, D_MODEL), F32)],
        compiler_params=_params("arbitrary"),
        name="moe_combine",
    )(tables["src"], tables["nchunk"], dense, gates, rank, tables["bounds"], g, b, rows)


def _row(v):
    return v.reshape(1, -1).astype(F32)


def _scan_layout(batch, seq, row0):
    t = math.gcd(seq, CHUNK)
    nchunk = seq // t
    chunks = SCAN_CHUNKS if nchunk % SCAN_CHUNKS == 0 else 1
    seqs = SCAN_SEQS if (nchunk == 1 and batch % SCAN_SEQS == 0 and row0 % (SCAN_SEQS * t) == 0) else 1
    return dict(batch=batch, seq=seq, row0=row0, t=t, seqs=seqs, chunks=chunks)


def _lanes_at(v, start):
    return jnp.zeros((1, LANES), F32).at[0, start:start + v.shape[0]].set(v.astype(F32))


def kernel(x_prompt, x_sample, p_prompt, p_sample, state_ssm_conv, state_ssm, state_dn_conv, state_dn, emb_ln_g, emb_ln_b, w_in, conv_a_w, conv_a_b, ssm_dt_bias, ssm_a_log, ssm_d, ssm_norm_w, w_a, conv_b_w, dn_dt_bias, dn_a_log, dn_norm_w, w_b, w_o, ln1_g, ln1_b, router_w, router_bias, exp_w1, exp_w3, exp_w2, sh_w1, sh_w3, sh_w2, ple_w, ple_gate_w, ln2_g, ln2_b):
    bp, lp, _ = x_prompt.shape
    bs, ls, _ = x_sample.shape
    n_p = bp * lp
    n_s = bs * ls
    xp = x_prompt.reshape(n_p, D_MODEL)
    xs = x_sample.reshape(n_s, D_MODEL)

    w = w_in[0]
    o_z, o_xbc, o_dt = 0, SSM_INNER, SSM_INNER + SSM_CONV_CH
    o_qkv = o_dt + SSM_HEADS
    o_a = o_qkv + DN_CONV_CH
    o_b = o_a + DN_HEADS
    o_zb = o_b + DN_HEADS
    o_ga = o_zb + DN_V
    w_pa = w[:, o_z:o_dt].astype(BF16)
    w_pb = jnp.concatenate([w[:, o_qkv:o_a], w[:, o_zb:o_ga]], axis=1).astype(BF16)
    w_pg = w[:, o_ga:].astype(BF16)
    w_ps = jnp.zeros((D_MODEL, LANES), F32)
    w_ps = w_ps.at[:, SMALL_DT:SMALL_DT + SSM_HEADS].set(w[:, o_dt:o_qkv])
    w_ps = w_ps.at[:, SMALL_A:SMALL_A + 2 * DN_HEADS].set(w[:, o_a:o_zb]).astype(BF16)

    eg, eb = _row(emb_ln_g), _row(emb_ln_b)
    proj_a, small = _ln_matmul(xp, xs, eg, eb, w_pa, PROJ_TM, PROJ_TN_A, w_ps)
    proj_b = _ln_matmul(xp, xs, eg, eb, w_pb, PROJ_TM, PROJ_TN_B)

    ssd_consts = [conv_a_w[0], _row(conv_a_b[0]), _lanes_at(ssm_dt_bias[0], SMALL_DT),
                  _lanes_at(ssm_a_log[0], SMALL_DT), _row(jnp.repeat(ssm_d[0], SSM_HEAD_DIM)),
                  _row(ssm_norm_w[0])]
    gdn_consts = [conv_b_w[0], _lanes_at(dn_a_log[0], SMALL_A), _lanes_at(dn_dt_bias[0], SMALL_A),
                  _row(dn_norm_w[0])]
    zeros = lambda *s: jnp.zeros(s, F32)
    prompt = _scan_layout(bp, lp, 0)
    sample = _scan_layout(bs, ls, n_p)

    ya_p, pa_conv, pa_ssm = _ssd(proj_a, small, ssd_consts, zeros(bp, CONV_WIDTH - 1, SSM_CONV_CH),
                                 zeros(bp, SSM_HEADS, SSM_HEAD_DIM, SSM_STATE), **prompt)
    ya_s, sa_conv, sa_ssm = _ssd(proj_a, small, ssd_consts, state_ssm_conv[0], state_ssm[0], **sample)
    ob_p, pb_conv, pb_dn = _gdn(proj_b, small, gdn_consts, zeros(bp, CONV_WIDTH - 1, DN_CONV_CH),
                                zeros(bp, DN_HEADS, DN_HEAD, DN_HEAD), **prompt)
    ob_s, sb_conv, sb_dn = _gdn(proj_b, small, gdn_consts, state_dn_conv[0], state_dn[0], **sample)

    x1, x1b = _merge(
        xp, xs, ya_p, ya_s, ob_p, ob_s,
        [eg, eb, w_pg, w_a[0].astype(BF16), w_b[0].astype(BF16), w_o[0].astype(BF16), _row(ln1_g[0]), _row(ln1_b[0])],
        MERGE_TM)

    router_w_t = jnp.zeros((LANES, D_MODEL), F32).at[:N_EXPERTS].set(router_w[0].T)
    router_w_hi = router_w_t.astype(BF16)
    router_w_lo = (router_w_t - router_w_hi.astype(F32)).astype(BF16)
    router_b = jnp.broadcast_to(_lanes_at(router_bias[0], 0).reshape(LANES, 1), (LANES, MOE_BLOCK))
    dense, gates, rank, chosen_t, rank_t, counts = _dense_router(
        x1, x1b, p_prompt[0].reshape(n_p, PLE_DIM), p_sample[0].reshape(n_s, PLE_DIM),
        [sh_w1[0].astype(BF16), sh_w3[0].astype(BF16), sh_w2[0].astype(BF16), ple_w[0].astype(BF16),
         ple_gate_w[0].astype(BF16), router_w_hi, router_w_lo, router_b])

    tables = _moe_tables(counts[:, 0, :N_EXPERTS].astype(jnp.int32), MOE_TM)
    sorted_rows = _moe_gather(x1b, chosen_t, rank_t, tables, MOE_TM)
    expert_out = _moe_ffn(sorted_rows, exp_w1[0], exp_w3[0], exp_w2[0], tables, MOE_TM)
    out_p, out_s = _moe_combine(expert_out, dense, gates, rank, tables, _row(ln2_g[0]), _row(ln2_b[0]), n_p)

    return (out_p.reshape(bp, lp, D_MODEL), out_s.reshape(bs, ls, D_MODEL),
            pa_conv[None], pa_ssm[None], pb_conv[None], pb_dn[None],
            sa_conv[None], sa_ssm[None], sb_conv[None], sb_dn[None])
```

```python
import functools
import math

import jax
import jax.numpy as jnp
from jax import lax
from jax.experimental import pallas as pl
from jax.experimental.pallas import tpu as pltpu

F32 = jnp.float32
BF16 = jnp.bfloat16
HIGHEST = lax.Precision.HIGHEST

D_MODEL = 1024
SSM_INNER = 2048
SSM_HEAD_DIM = 64
SSM_HEADS = 32
SSM_GROUPS = 2
SSM_HEADS_PER_GROUP = 16
SSM_STATE = 128
SSM_CONV_CH = 2560
DN_HEADS = 8
DN_HEAD = 128
DN_QK = 1024
DN_V = 1024
DN_CONV_CH = 3072
CONV_WIDTH = 4
CHUNK = 64
N_EXPERTS = 64
TOP_K = 8
N_EXPERT_GROUPS = 8
EXPERTS_PER_GROUP = 8
TOPK_GROUPS = 4
EXPERT_FF = 256
SHARED_FF = 256
ROUTED_SCALE = 2.5
PLE_DIM = 256
LN_EPS = 1e-5
RMS_EPS = 1e-6
L2_EPS = 1e-6
DEEPNORM_ALPHA = 2.0 ** 0.25

LANES = 128
SUBLANES = 8
VMEM_LIMIT = 56 * 1024 * 1024
SMALL_DT = 0
SMALL_A = 32
SMALL_B = 40
PROJ_TM = 1024
PROJ_TN_A = 2304
PROJ_TN_B = 2048
MERGE_TM = 512
SCAN_CHUNKS = 8
SCAN_SEQS = 8
MOE_BLOCK = 256
MOE_SEG = 16
MOE_SUB = 512
MOE_CAP = -(-(MOE_BLOCK * TOP_K + N_EXPERTS * (MOE_SEG - 1)) // MOE_SUB) * MOE_SUB
MOE_TM = 512
MOE_ROW_SLOTS = 3
ROUTER_TM = 1024
MOE_SUB_ALWAYS = -(-(MOE_BLOCK * TOP_K + N_EXPERTS * MOE_SEG // 2) // MOE_SUB)


def _sigmoid(x):
    return 1.0 / (1.0 + jnp.exp(-x))


def _silu(x):
    return x * _sigmoid(x)


def _softplus(x):
    return jnp.maximum(x, 0.0) + jnp.log(1.0 + jnp.exp(-jnp.abs(x)))


def _layer_norm(x, g, b):
    mu = jnp.mean(x, axis=-1, keepdims=True)
    xc = x - mu
    var = jnp.mean(xc * xc, axis=-1, keepdims=True)
    return xc * lax.rsqrt(var + LN_EPS) * g + b


def _dot(a, b):
    return jnp.dot(a, b, preferred_element_type=F32)


def _dot_nt(a, b):
    return lax.dot_general(a, b, (((1,), (1,)), ((), ())), preferred_element_type=F32)


def _dot_tn(a, b):
    return lax.dot_general(a, b, (((0,), (0,)), ((), ())), preferred_element_type=F32)


def _dot_f32(a, b):
    return jnp.dot(a, b, precision=HIGHEST, preferred_element_type=F32)


def _params(*sem):
    return pltpu.CompilerParams(dimension_semantics=sem, vmem_limit_bytes=VMEM_LIMIT)


def _when(cond):
    if cond is True:
        return lambda fn: fn()
    if cond is False:
        return lambda fn: None
    return pl.when(cond)


def _row_tile(m, preferred):
    return max(d for d in range(SUBLANES, min(m, preferred) + 1, SUBLANES) if m % d == 0)


def _group_maps(prompt_tiles):
    in_prompt = lambda i, *_: (jnp.minimum(i, prompt_tiles - 1), 0)
    in_sample = lambda i, *_: (jnp.maximum(i - prompt_tiles, 0), 0)
    return in_prompt, in_sample


def _ln_matmul_kernel(xp_ref, xs_ref, g_ref, b_ref, w_ref, *rest, prompt_tiles, narrow):
    if narrow:
        wn_ref, o_ref, on_ref, h_scr = rest
    else:
        o_ref, h_scr = rest

    @pl.when(pl.program_id(1) == 0)
    def _():
        x = jnp.where(pl.program_id(0) < prompt_tiles, xp_ref[...], xs_ref[...])
        h_scr[...] = _layer_norm(x, g_ref[...], b_ref[...]).astype(BF16)
        if narrow:
            on_ref[...] = _dot(h_scr[...], wn_ref[...])

    o_ref[...] = _dot(h_scr[...], w_ref[...]).astype(o_ref.dtype)


def _ln_matmul(xp, xs, g, b, w, tm, tn, w_narrow=None):
    k = xp.shape[1]
    m = xp.shape[0] + xs.shape[0]
    n = w.shape[1]
    tm = _row_tile(math.gcd(xp.shape[0], xs.shape[0]), tm)
    prompt_tiles = xp.shape[0] // tm
    in_prompt, in_sample = _group_maps(prompt_tiles)
    narrow = w_narrow is not None
    const = lambda i, j: (0, 0)
    row = lambda i, j: (i, 0)
    in_specs = [pl.BlockSpec((tm, k), in_prompt), pl.BlockSpec((tm, k), in_sample),
                pl.BlockSpec((1, k), const), pl.BlockSpec((1, k), const), pl.BlockSpec((k, tn), lambda i, j: (0, j))]
    out_specs = [pl.BlockSpec((tm, tn), lambda i, j: (i, j))]
    out_shape = [jax.ShapeDtypeStruct((m, n), BF16)]
    args = [xp, xs, g, b, w]
    if narrow:
        in_specs.append(pl.BlockSpec((k, LANES), const))
        out_specs.append(pl.BlockSpec((tm, LANES), row))
        out_shape.append(jax.ShapeDtypeStruct((m, LANES), F32))
        args.append(w_narrow)
    out = pl.pallas_call(
        functools.partial(_ln_matmul_kernel, prompt_tiles=prompt_tiles, narrow=narrow),
        grid=(m // tm, n // tn),
        in_specs=in_specs,
        out_specs=out_specs,
        out_shape=out_shape,
        scratch_shapes=[pltpu.VMEM((tm, k), BF16)],
        compiler_params=_params("parallel", "arbitrary"),
        name="ln_matmul",
    )(*args)
    return out if narrow else out[0]


def _load_conv_state(cbuf, state):
    taps = CONV_WIDTH - 1
    cbuf[0:SUBLANES - taps, :] = jnp.zeros((SUBLANES - taps, cbuf.shape[1]), F32)
    cbuf[SUBLANES - taps:SUBLANES, :] = state


def _causal_conv(cbuf, x16, cw, t):
    ch = x16.shape[1]
    prev = cbuf[...]
    taps = CONV_WIDTH - 1
    if t <= SUBLANES:
        x = x16.astype(F32)
        ext = jnp.concatenate([prev, x], axis=0)
        lo = SUBLANES - taps
        y = ext[lo:lo + t] * cw[0:1]
        y = y + ext[lo + 1:lo + 1 + t] * cw[1:2]
        y = y + ext[lo + 2:lo + 2 + t] * cw[2:3]
        y = y + x * cw[3:4]
        cbuf[...] = ext[t:t + SUBLANES]
        return y, ext[SUBLANES + t - taps:SUBLANES + t]
    hi = prev.astype(BF16).astype(F32)
    mid = (prev - hi).astype(BF16).astype(F32)
    low = ((prev - hi) - mid).astype(BF16).astype(F32)
    x = x16.astype(F32)
    pack = 2 * SUBLANES
    nhead = 4 * SUBLANES if t % pack == 0 else 5 * SUBLANES
    head = jnp.concatenate([hi, mid, low, jnp.zeros((nhead - 3 * SUBLANES, ch), F32)], axis=0)
    if t % pack == 0:
        ext = jnp.concatenate([head.astype(BF16), x16], axis=0)
    else:
        ext = jnp.concatenate([head, x], axis=0).astype(BF16)
    out_row = lax.broadcasted_iota(jnp.int32, (taps * t, nhead + t), 0)
    col = lax.broadcasted_iota(jnp.int32, (taps * t, nhead + t), 1)
    log_t = t.bit_length() - 1
    shift = lax.shift_right_logical(out_row, log_t) + 1
    src = (out_row & (t - 1)) - shift
    in_x = (src >= 0) & (col == nhead + src)
    in_prev = (src < 0) & (col < 3 * SUBLANES) & ((col & (SUBLANES - 1)) == SUBLANES + src)
    shifted = _dot((in_x | in_prev).astype(BF16), ext)
    y = shifted[2 * t:3 * t] * cw[0:1]
    y = y + shifted[t:2 * t] * cw[1:2]
    y = y + shifted[0:t] * cw[2:3]
    y = y + x * cw[3:4]
    cbuf[...] = x[t - SUBLANES:t]
    return y, x[t - taps:t]


def _lane_expand(v, h0, count, width):
    t = v.shape[0]
    n = count * width
    out = jnp.broadcast_to(v[:, h0:h0 + 1], (t, n))
    if count > 1:
        lane = lax.broadcasted_iota(jnp.int32, (t, n), 1)
        for i in range(1, count):
            out = jnp.where(lane >= i * width, jnp.broadcast_to(v[:, h0 + i:h0 + i + 1], (t, n)), out)
    return out


def _grouped_transpose(v, per_tile):
    t = v.shape[0]
    blocks = [v if r == 0 else pltpu.roll(v, LANES - r, axis=1) for r in range(per_tile)]
    if per_tile * t < LANES:
        blocks.append(jnp.zeros((LANES - per_tile * t, LANES), F32))
    return jnp.concatenate(blocks, axis=0).T


def _same_block(i, j, size):
    shift = size.bit_length() - 1
    return lax.shift_right_logical(i, shift) == lax.shift_right_logical(j, shift)


def _unit_lower_inverses(lmats, ii, jj, t):
    mm = lambda a, b: _dot(a.astype(BF16), b.astype(BF16))
    base = min(16, t)
    in_base = _same_block(ii, jj, base)
    eye = (ii == jj).astype(F32)
    power = [jnp.where(in_base, l, 0.0) for l in lmats]
    inv = [eye - p for p in power]
    span = 2
    while span < base:
        power = [mm(p, p) for p in power]
        inv = [a + mm(a, p) for a, p in zip(inv, power)]
        span *= 2
    size = base
    while size < t:
        link = _same_block(ii, jj, 2 * size) & jnp.logical_not(_same_block(ii, jj, size))
        cross = [mm(a, jnp.where(link, l, 0.0)) for a, l in zip(inv, lmats)]
        inv = [a - mm(c, a) for a, c in zip(inv, cross)]
        size *= 2
    return inv


def _row_sums_of_squares(blocks):
    t = blocks[0].shape[0]
    sq = jnp.concatenate([b * b for b in blocks], axis=0)
    hi = sq.astype(BF16)
    lo = (sq - hi.astype(F32)).astype(BF16)
    ones = jnp.ones((LANES, LANES), BF16)
    sums = _dot(hi, ones) + _dot(lo, ones)
    return [sums[i * t:(i + 1) * t] for i in range(len(blocks))]


def _cumsum_rows(v, t):
    ii = lax.broadcasted_iota(jnp.int32, (t, t), 0)
    jj = lax.broadcasted_iota(jnp.int32, (t, t), 1)
    return _dot_f32((jj <= ii).astype(F32), v)


def _ssd_kernel(pa_ref, sm_ref, cw_ref, cb_ref, dtb_ref, alog_ref, dskip_ref, nw_ref, cst_ref, h0_ref,
                y_ref, cnew_ref, hnew_ref, cbuf, s_scr, *, t, first, last):
    per_tile = LANES // t
    width = per_tile * SSM_HEAD_DIM

    @_when(first)
    def _():
        _load_conv_state(cbuf, cst_ref[0])
        s_scr[...] = h0_ref[0]

    z = pa_ref[:, :SSM_INNER].astype(F32)
    conv, tail = _causal_conv(cbuf, pa_ref[:, SSM_INNER:], cw_ref[...], t)

    @_when(last)
    def _():
        cnew_ref[0] = tail

    xbc = _silu(conv + cb_ref[...])
    xs = xbc[:, :SSM_INNER]
    bm = xbc[:, SSM_INNER:SSM_INNER + SSM_GROUPS * SSM_STATE]
    cm = xbc[:, SSM_INNER + SSM_GROUPS * SSM_STATE:]

    dt = _softplus(sm_ref[...] + dtb_ref[...])
    da = dt * (-jnp.exp(alog_ref[...]))
    cum = _cumsum_rows(da, t)
    cum_t = _grouped_transpose(cum, per_tile)
    dt_t = _grouped_transpose(dt, per_tile)

    row = lax.broadcasted_iota(jnp.int32, (t, LANES), 0)
    lane = lax.broadcasted_iota(jnp.int32, (t, LANES), 1)
    causal = (lane % t) <= row
    brow = lax.broadcasted_iota(jnp.int32, (LANES, width), 0)
    bcol = lax.broadcasted_iota(jnp.int32, (LANES, width), 1)
    blockdiag = (brow // t) == (bcol // SSM_HEAD_DIM)

    ys = []
    for g in range(SSM_GROUPS):
        bg = bm[:, g * SSM_STATE:(g + 1) * SSM_STATE]
        cg = cm[:, g * SSM_STATE:(g + 1) * SSM_STATE].astype(BF16)
        cb = _dot_nt(cg, jnp.concatenate([bg] * per_tile, axis=0).astype(BF16))
        bg = bg.astype(BF16)
        for q in range(SSM_HEADS_PER_GROUP // per_tile):
            h0 = g * SSM_HEADS_PER_GROUP + q * per_tile
            ccol = _lane_expand(cum, h0, per_tile, t)
            seg = ccol - cum_t[h0:h0 + 1, :]
            decay = jnp.exp(jnp.where(causal, seg, -jnp.inf))
            wts = (cb * decay * dt_t[h0:h0 + 1, :]).astype(BF16)
            xt = xs[:, h0 * SSM_HEAD_DIM:h0 * SSM_HEAD_DIM + width]
            xbd = jnp.where(blockdiag, jnp.concatenate([xt] * per_tile, axis=0), 0.0).astype(BF16)
            y_intra = _dot(wts, xbd)
            st = s_scr[h0:h0 + per_tile].reshape(width, SSM_STATE)
            cum_w = ccol if width == LANES else _lane_expand(cum, h0, per_tile, SSM_HEAD_DIM)
            y_state = _dot_nt(cg, st.astype(BF16)) * jnp.exp(cum_w)
            ys.append(y_intra + y_state)
            wend = jnp.exp(cum_w[t - 1:t, :] - cum_w) * _lane_expand(dt, h0, per_tile, SSM_HEAD_DIM)
            ds = _dot_tn((xt * wend).astype(BF16), bg)
            for i in range(per_tile):
                h = h0 + i
                s_scr[h] = jnp.exp(cum[t - 1:t, h:h + 1]) * s_scr[h] + ds[i * SSM_HEAD_DIM:(i + 1) * SSM_HEAD_DIM]

    y = jnp.concatenate(ys, axis=1) + dskip_ref[...] * xs
    y = y * _silu(z)
    half = SSM_INNER // SSM_GROUPS
    normed = []
    for g in range(SSM_GROUPS):
        yg = y[:, g * half:(g + 1) * half]
        normed.append(yg * lax.rsqrt(jnp.mean(yg * yg, axis=-1, keepdims=True) + RMS_EPS))
    y_ref[...] = (jnp.concatenate(normed, axis=1) * nw_ref[...]).astype(BF16)

    @_when(last)
    def _():
        hnew_ref[0] = s_scr[...]


def _scan_step_kernel(chunk_kernel, n_tok, n_const, n_state, seqs, chunks, t):
    def step(*refs):
        tok = refs[:n_tok]
        const = refs[n_tok:n_tok + n_const]
        st_in = refs[n_tok + n_const:n_tok + n_const + n_state]
        y_ref = refs[n_tok + n_const + n_state]
        st_out = refs[n_tok + n_const + n_state + 1:n_tok + n_const + 2 * n_state + 1]
        scratch = refs[n_tok + n_const + 2 * n_state + 1:]
        first_step = pl.program_id(1) == 0
        last_step = pl.program_id(1) == pl.num_programs(1) - 1
        for s in range(seqs):
            for c in range(chunks):
                rows = pl.ds((s * chunks + c) * t, t)
                chunk_kernel(*[r.at[rows, :] for r in tok], *const, *[r.at[pl.ds(s, 1)] for r in st_in],
                             y_ref.at[rows, :], *[r.at[pl.ds(s, 1)] for r in st_out], *scratch,
                             first=first_step if c == 0 else False,
                             last=last_step if c == chunks - 1 else False)
    return step


def _scan_call(kernel, name, tok_in, const_in, state_in, y_width, scratch, *, batch, seq, row0, t, seqs, chunks,
               step_kernel=None):
    nstep = seq // (t * chunks)
    rows = seqs * chunks * t
    blk0 = row0 // rows

    def per_batch(a):
        zeros = (0,) * (a.ndim - 1)
        return pl.BlockSpec((seqs,) + a.shape[1:], lambda b, c: (b,) + zeros)

    if step_kernel is None:
        step_kernel = _scan_step_kernel(kernel, len(tok_in), len(const_in), len(state_in), seqs, chunks, t)
    return pl.pallas_call(
        step_kernel,
        grid=(batch // seqs, nstep),
        in_specs=[pl.BlockSpec((rows, a.shape[1]), lambda b, c: (blk0 + b * nstep + c, 0)) for a in tok_in]
        + [pl.BlockSpec(a.shape, lambda b, c: (0, 0)) for a in const_in]
        + [per_batch(a) for a in state_in],
        out_specs=[pl.BlockSpec((rows, y_width), lambda b, c: (b * nstep + c, 0))]
        + [per_batch(a) for a in state_in],
        out_shape=[jax.ShapeDtypeStruct((batch * seq, y_width), BF16)]
        + [jax.ShapeDtypeStruct(a.shape, F32) for a in state_in],
        scratch_shapes=scratch,
        compiler_params=_params("parallel", "arbitrary"),
        name=name,
    )(*tok_in, *const_in, *state_in)


def _ssd(proj_a, small, consts, conv_state, h0, **where):
    t = where["t"]
    scratch = [pltpu.VMEM((SUBLANES, SSM_CONV_CH), F32),
               pltpu.VMEM((SSM_HEADS, SSM_HEAD_DIM, SSM_STATE), F32)]
    return _scan_call(functools.partial(_ssd_kernel, t=t), f"ssd_t{t}", [proj_a, small], consts,
                      [conv_state, h0], SSM_INNER, scratch, **where)


def _gdn_step_kernel(pb_ref, sm_ref, cw_ref, alog_ref, dtb_ref, nw_ref, cst_ref, s0_ref,
                     o_ref, cnew_ref, snew_ref, cbuf, s_scr, *, t, seqs, chunks):
    first_step = pl.program_id(1) == 0
    last_step = pl.program_id(1) == pl.num_programs(1) - 1
    units = [(s, c) for s in range(seqs) for c in range(chunks)]
    rows = lambda u: pl.ds(u * t, t)
    per_tile = min(LANES // t, DN_HEADS)
    n = per_tile * t
    ntile = DN_HEADS // per_tile
    heads = range(DN_HEADS)
    ii = lax.broadcasted_iota(jnp.int32, (n, n), 0)
    jj = lax.broadcasted_iota(jnp.int32, (n, n), 1)
    causal = _same_block(ii, jj, t) & (jj <= ii)
    diag = ii == jj
    stack = lambda xs, p: jnp.concatenate(xs[p * per_tile:(p + 1) * per_tile], axis=0)

    qkv, zb, gcum, beta = [], [], [], []
    for u, (s, c) in enumerate(units):
        @_when(first_step if c == 0 else False)
        def _():
            _load_conv_state(cbuf, cst_ref[s])

        conv, tail = _causal_conv(cbuf, pb_ref[rows(u), :DN_CONV_CH], cw_ref[...], t)

        @_when(last_step if c == chunks - 1 else False)
        def _():
            cnew_ref[s] = tail

        qkv.append(_silu(conv))
        zb.append(pb_ref[rows(u), DN_CONV_CH:].astype(F32))
        sm = sm_ref[rows(u), :]
        beta.append(_sigmoid(sm))
        gate = -jnp.exp(alog_ref[...]) * _softplus(sm + dtb_ref[...])
        gcum.append(_cumsum_rows(gate, t))
    gcum_t = [_grouped_transpose(g, per_tile) for g in gcum]

    sumsq = _row_sums_of_squares([x[:, i * DN_HEAD:(i + 1) * DN_HEAD] for x in qkv for i in range(2 * DN_HEADS)])
    q, k, v, gcol, bcol, kb, egcol = ([[] for _ in units] for _ in range(7))
    for u in range(len(units)):
        for h in heads:
            lo = h * DN_HEAD
            ss = sumsq[u * 2 * DN_HEADS:(u + 1) * 2 * DN_HEADS]
            q[u].append(qkv[u][:, lo:lo + DN_HEAD] * lax.rsqrt(ss[h] + L2_EPS) * (DN_HEAD ** -0.5))
            k[u].append(qkv[u][:, DN_QK + lo:DN_QK + lo + DN_HEAD] * lax.rsqrt(ss[DN_HEADS + h] + L2_EPS))
            v[u].append(qkv[u][:, 2 * DN_QK + lo:2 * DN_QK + lo + DN_HEAD])
            gcol[u].append(gcum[u][:, SMALL_A + h:SMALL_A + h + 1])
            bcol[u].append(beta[u][:, SMALL_B + h:SMALL_B + h + 1])
            kb[u].append(k[u][h] * bcol[u][h])
            egcol[u].append(jnp.exp(gcol[u][h]))

    tiles = [(u, p) for u in range(len(units)) for p in range(ntile)]
    k16 = [stack(k[u], p).astype(BF16) for u, p in tiles]
    decay = []
    for u, p in tiles:
        h0 = p * per_tile
        seg = stack(gcol[u], p) - gcum_t[u][SMALL_A + h0:SMALL_A + h0 + 1, :n]
        decay.append(jnp.exp(jnp.where(causal, seg, -jnp.inf)))
    lmat = [_dot_nt(stack(kb[u], p).astype(BF16), k16[i]) * jnp.where(diag, 0.0, decay[i])
            for i, (u, p) in enumerate(tiles)]
    qk = [(_dot_nt(stack(q[u], p).astype(BF16), k16[i]) * decay[i]).astype(BF16) for i, (u, p) in enumerate(tiles)]
    inv = [a.astype(BF16) for a in _unit_lower_inverses(lmat, ii, jj, t)]

    split = lambda x: (x[:, :DN_HEAD], x[:, DN_HEAD:])
    o_from, o_own, s_from, s_own, glast = ([[] for _ in units] for _ in range(5))
    aw = [_dot(inv[i], jnp.concatenate(
        [stack([kb[u][h] * egcol[u][h] for h in heads], p), stack([v[u][h] * bcol[u][h] for h in heads], p)],
        axis=1).astype(BF16)) for i, (u, p) in enumerate(tiles)]
    qaw = [_dot(qk[i], aw[i].astype(BF16)) for i in range(len(tiles))]
    pad_rows = -t % (2 * SUBLANES)
    pad = [jnp.zeros((pad_rows, DN_HEAD), F32)] if pad_rows else []
    for u in range(len(units)):
        for h in heads:
            p, j = divmod(h, per_tile)
            i = u * ntile + p
            sel = slice(j * t, (j + 1) * t)
            qkw, qku0 = split(qaw[i][sel])
            o_from[u].append(q[u][h] * egcol[u][h] - qkw)
            o_own[u].append(qku0)
            glast[u].append(gcum[u][t - 1:t, SMALL_A + h:SMALL_A + h + 1])
            kd = (k[u][h] * jnp.exp(glast[u][h] - gcol[u][h])).astype(BF16)
            kdw, kdu0 = split(_dot_tn(kd, aw[i][sel].astype(BF16)))
            s_from[u].append(kdw)
            s_own[u].append(kdu0)

    outs = []
    for u, (s, c) in enumerate(units):
        @_when(first_step if c == 0 else False)
        def _():
            s_scr[...] = s0_ref[s]

        for h in heads:
            state = s_scr[h]
            both = _dot(jnp.concatenate([s_from[u][h], o_from[u][h]] + pad, axis=0).astype(BF16), state.astype(BF16))
            outs.append(both[DN_HEAD:DN_HEAD + t] + o_own[u][h])
            s_scr[h] = jnp.exp(glast[u][h]) * state - both[:DN_HEAD] + s_own[u][h]

        @_when(last_step if c == chunks - 1 else False)
        def _():
            snew_ref[s] = s_scr[...]

    norm_w = nw_ref[...]
    osq = _row_sums_of_squares(outs)
    for u in range(len(units)):
        normed = []
        for h in heads:
            i = u * DN_HEADS + h
            normed.append(outs[i] * lax.rsqrt(osq[i] * (1.0 / DN_HEAD) + RMS_EPS) * norm_w
                          * _silu(zb[u][:, h * DN_HEAD:(h + 1) * DN_HEAD]))
        o_ref[rows(u), :] = jnp.concatenate(normed, axis=1).astype(BF16)


def _gdn(proj_b, small, consts, conv_state, s0, **where):
    t = where["t"]
    scratch = [pltpu.VMEM((SUBLANES, DN_CONV_CH), F32),
               pltpu.VMEM((DN_HEADS, DN_HEAD, DN_HEAD), F32)]
    step = functools.partial(_gdn_step_kernel, t=t, seqs=where["seqs"], chunks=where["chunks"])
    return _scan_call(None, f"gdn_t{t}", [proj_b, small], consts, [conv_state, s0], DN_V, scratch,
                      step_kernel=step, **where)


def _merge_kernel(xp_ref, xs_ref, yap_ref, yas_ref, obp_ref, obs_ref, eg_ref, eb_ref, wg_ref, wa_ref, wb_ref,
                  wo_ref, g_ref, b_ref, x1_ref, x1b_ref, *, prompt_tiles):
    is_prompt = pl.program_id(0) < prompt_tiles
    h = _layer_norm(jnp.where(is_prompt, xp_ref[...], xs_ref[...]), eg_ref[...], eb_ref[...])
    out_a = _dot(jnp.where(is_prompt, yap_ref[...], yas_ref[...]), wa_ref[...])
    out_b = _dot(jnp.where(is_prompt, obp_ref[...], obs_ref[...]), wb_ref[...])
    gt = _dot(h.astype(BF16), wg_ref[...])
    merged = _sigmoid(gt[:, :D_MODEL]) * out_a + _sigmoid(gt[:, D_MODEL:]) * out_b
    mix = _dot(merged.astype(BF16), wo_ref[...])
    x1 = _layer_norm(DEEPNORM_ALPHA * h + mix, g_ref[...], b_ref[...])
    x1_ref[...] = x1
    x1b_ref[...] = x1.astype(BF16)


def _merge(xp, xs, ya_p, ya_s, ob_p, ob_s, consts, tm):
    m = xp.shape[0] + xs.shape[0]
    n_p = xp.shape[0]
    tm = _row_tile(math.gcd(n_p, m - n_p), tm)
    in_prompt, in_sample = _group_maps(n_p // tm)
    row = lambda i: (i, 0)
    const = lambda i: (0, 0)
    return pl.pallas_call(
        functools.partial(_merge_kernel, prompt_tiles=n_p // tm),
        grid=(m // tm,),
        in_specs=[pl.BlockSpec((tm, D_MODEL), in_prompt), pl.BlockSpec((tm, D_MODEL), in_sample),
                  pl.BlockSpec((tm, SSM_INNER), in_prompt), pl.BlockSpec((tm, SSM_INNER), in_sample),
                  pl.BlockSpec((tm, DN_V), in_prompt), pl.BlockSpec((tm, DN_V), in_sample)]
        + [pl.BlockSpec(a.shape, const) for a in consts],
        out_specs=[pl.BlockSpec((tm, D_MODEL), row), pl.BlockSpec((tm, D_MODEL), row)],
        out_shape=[jax.ShapeDtypeStruct((m, D_MODEL), F32), jax.ShapeDtypeStruct((m, D_MODEL), BF16)],
        compiler_params=_params("parallel"),
        name="merge",
    )(xp, xs, ya_p, ya_s, ob_p, ob_s, *consts)


def _take_first_max(rest, index, limit, axis):
    best = jnp.max(rest, axis=axis, keepdims=True)
    first = jnp.min(jnp.where(rest == best, index, limit), axis=axis, keepdims=True)
    hit = index == first
    return hit, jnp.where(hit, -jnp.inf, rest)


def _router_gates_t(x1, xb, rw_hi, rw_lo, bias):
    tm = x1.shape[0]
    x_lo = (x1 - xb.astype(F32)).astype(BF16)
    logits = _dot_nt(rw_hi, xb) + (_dot_nt(rw_hi, x_lo) + _dot_nt(rw_lo, xb))
    scores = _sigmoid(logits)
    row = lax.broadcasted_iota(jnp.int32, (LANES, tm), 0)
    sel = jnp.where(row < N_EXPERTS, scores + bias, -jnp.inf)
    by_group = sel.reshape(LANES // EXPERTS_PER_GROUP, EXPERTS_PER_GROUP, tm)
    sub = lax.broadcasted_iota(jnp.int32, by_group.shape, 1)
    top1 = jnp.max(by_group, axis=1, keepdims=True)
    _, others = _take_first_max(by_group, sub, EXPERTS_PER_GROUP, 1)
    gscore = (top1 + jnp.max(others, axis=1, keepdims=True))[:N_EXPERT_GROUPS]
    gidx = lax.broadcasted_iota(jnp.int32, gscore.shape, 0)
    gkeep = jnp.zeros(gscore.shape, jnp.bool_)
    for _ in range(TOPK_GROUPS):
        hit, gscore = _take_first_max(gscore, gidx, N_EXPERT_GROUPS, 0)
        gkeep = gkeep | hit
    gkeep = jnp.broadcast_to(gkeep, (N_EXPERT_GROUPS, EXPERTS_PER_GROUP, tm)).reshape(N_EXPERTS, tm)
    rest = jnp.where(gkeep, sel[:N_EXPERTS], -jnp.inf)
    eidx = lax.broadcasted_iota(jnp.int32, (N_EXPERTS, tm), 0)
    keep = jnp.zeros((N_EXPERTS, tm), jnp.bool_)
    for _ in range(TOP_K):
        hit, rest = _take_first_max(rest, eidx, N_EXPERTS, 0)
        keep = keep | hit
    picked = jnp.where(keep, scores[:N_EXPERTS], 0.0)
    gates = picked / jnp.sum(picked, axis=0, keepdims=True) * ROUTED_SCALE
    return jnp.concatenate([gates, jnp.zeros((LANES - N_EXPERTS, tm), F32)], axis=0)


def _dense_router_kernel(x1_ref, x1b_ref, pp_ref, ps_ref, sw1_ref, sw3_ref, sw2_ref, pw_ref, pg_ref, rwh_ref,
                         rwl_ref, rb_ref, dense_ref, gates_ref, rank_ref, chosen_t_ref, rank_t_ref, count_ref,
                         *, prompt_tiles):
    xb = x1b_ref[...]
    hid = _silu(_dot(xb, sw1_ref[...])) * _dot(xb, sw3_ref[...])
    shared = _dot(hid.astype(BF16), sw2_ref[...])
    p = jnp.where(pl.program_id(0) < prompt_tiles, pp_ref[...], ps_ref[...])
    ple = _dot(p.astype(BF16), pw_ref[...]) * _sigmoid(_dot(xb, pg_ref[...]))
    x1 = x1_ref[...]
    dense_ref[...] = DEEPNORM_ALPHA * x1 + shared + ple
    ii = lax.broadcasted_iota(jnp.int32, (MOE_BLOCK, MOE_BLOCK), 0)
    jj = lax.broadcasted_iota(jnp.int32, (MOE_BLOCK, MOE_BLOCK), 1)
    earlier = (ii < jj).astype(BF16)
    for blk in range(x1.shape[0] // MOE_BLOCK):
        tok = slice(blk * MOE_BLOCK, (blk + 1) * MOE_BLOCK)
        exp = slice(blk * LANES, (blk + 1) * LANES)
        gates_t = _router_gates_t(x1[tok], xb[tok], rwh_ref[...], rwl_ref[...], rb_ref[...])
        chosen_t = (gates_t > 0.0).astype(BF16)
        rank_t = _dot(chosen_t, earlier)
        chosen_t_ref[exp, :] = chosen_t
        rank_t_ref[exp, :] = rank_t.astype(BF16)
        gates = gates_t.T
        rank = rank_t.T
        gates_ref[tok, :] = gates.astype(BF16)
        rank_ref[tok, :] = rank.astype(BF16)
        count = rank[MOE_BLOCK - 1:, :] + (gates[MOE_BLOCK - 1:, :] > 0.0).astype(F32)
        count_ref[blk] = jnp.broadcast_to(count, (SUBLANES, LANES))


def _dense_router(x1, x1b, p_prompt, p_sample, consts):
    m = x1.shape[0]
    tm = _row_tile(math.gcd(p_prompt.shape[0], p_sample.shape[0]), ROUTER_TM)
    assert tm % MOE_BLOCK == 0
    nblk = m // MOE_BLOCK
    per_step = tm // MOE_BLOCK
    prompt_tiles = p_prompt.shape[0] // tm
    in_prompt, in_sample = _group_maps(prompt_tiles)
    row = lambda i: (i, 0)
    const = lambda i: (0, 0)
    return pl.pallas_call(
        functools.partial(_dense_router_kernel, prompt_tiles=prompt_tiles),
        grid=(m // tm,),
        in_specs=[pl.BlockSpec((tm, D_MODEL), row), pl.BlockSpec((tm, D_MODEL), row),
                  pl.BlockSpec((tm, PLE_DIM), in_prompt), pl.BlockSpec((tm, PLE_DIM), in_sample)]
        + [pl.BlockSpec(a.shape, const) for a in consts],
        out_specs=[pl.BlockSpec((tm, D_MODEL), row), pl.BlockSpec((tm, LANES), row),
                   pl.BlockSpec((tm, LANES), row), pl.BlockSpec((per_step * LANES, MOE_BLOCK), row),
                   pl.BlockSpec((per_step * LANES, MOE_BLOCK), row),
                   pl.BlockSpec((per_step, SUBLANES, LANES), lambda i: (i, 0, 0))],
        out_shape=[jax.ShapeDtypeStruct((m, D_MODEL), F32), jax.ShapeDtypeStruct((m, LANES), BF16),
                   jax.ShapeDtypeStruct((m, LANES), BF16), jax.ShapeDtypeStruct((nblk * LANES, MOE_BLOCK), BF16),
                   jax.ShapeDtypeStruct((nblk * LANES, MOE_BLOCK), BF16),
                   jax.ShapeDtypeStruct((nblk, SUBLANES, LANES), F32)],
        compiler_params=_params("parallel"),
        name="dense_router",
    )(x1, x1b, p_prompt, p_sample, *consts)


def _moe_tables(counts, tm):
    nblk = counts.shape[0]
    nchunk_max = MOE_CAP // MOE_SEG
    padded = (counts + MOE_SEG - 1) // MOE_SEG * MOE_SEG
    start = jnp.cumsum(padded, axis=1) - padded
    used = jnp.sum(padded, axis=1)
    rows_e = jnp.sum(padded, axis=0)
    rows_e_t = (rows_e + tm - 1) // tm * tm
    base_e = jnp.cumsum(rows_e_t) - rows_e_t
    seg_row = base_e[None, :] + jnp.cumsum(padded, axis=0) - padded
    chunk0 = jnp.arange(nchunk_max, dtype=jnp.int32) * MOE_SEG
    owner = jnp.sum((start + padded)[:, None, :] <= chunk0[None, :, None], axis=-1)
    onehot = owner[..., None] == jnp.arange(N_EXPERTS, dtype=jnp.int32)
    dst = chunk0[None, :] + jnp.sum(jnp.where(onehot, (seg_row - start)[:, None, :], 0), axis=-1)
    n_tiles = _moe_max_tiles(nblk, tm)
    live = chunk0[None, :] < used[:, None]
    parity = (jnp.arange(nblk, dtype=jnp.int32) % 2)[:, None]
    spare = n_tiles * tm + parity * MOE_CAP + chunk0[None, :]
    src = jnp.where(live, dst, 0)
    dst = jnp.where(live, dst, spare)
    tile_row = jnp.arange(n_tiles, dtype=jnp.int32) * tm
    tiles_used = jnp.sum(rows_e_t) // tm
    tile_row = jnp.minimum(tile_row, (tiles_used - 1) * tm)
    tile_expert = jnp.minimum(jnp.sum((base_e + rows_e_t)[None, :] <= tile_row[:, None], axis=-1), N_EXPERTS - 1)
    in_use = rows_e > 0
    experts = jnp.arange(N_EXPERTS, dtype=jnp.int32)
    later = jnp.where(in_use[None, :] & (experts[None, :] > experts[:, None]), experts[None, :], N_EXPERTS)
    next_in_use = jnp.min(later, axis=1)
    next_in_use = jnp.where(next_in_use == N_EXPERTS, -1, next_in_use)
    order = jnp.sum(in_use[None, :] & (experts[None, :] < experts[:, None]), axis=1)
    of_tile = tile_expert[:, None] == experts[None, :]
    tile_slot = jnp.sum(jnp.where(of_tile, (order % 2)[None, :], 0), axis=1)
    tile_next = jnp.sum(jnp.where(of_tile, next_in_use[None, :], 0), axis=1)
    bounds = jnp.zeros((nblk, SUBLANES, LANES), F32)
    bounds = bounds.at[:, 0, :N_EXPERTS].set(start.astype(F32)).at[:, 1, :N_EXPERTS].set(padded.astype(F32))
    i32 = lambda a: a.astype(jnp.int32)
    return dict(dst=i32(dst.reshape(-1)), src=i32(src.reshape(-1)), nchunk=i32(used // MOE_SEG),
                pad_row=i32(base_e + rows_e),
                pad_n=i32((rows_e_t - rows_e) // MOE_SEG), tile_expert=i32(tile_expert),
                tile_slot=i32(tile_slot), tile_next=i32(tile_next),
                tiles_used=i32(tiles_used.reshape(1)), bounds=bounds)


def _moe_max_tiles(nblk, tm):
    return (nblk * (MOE_BLOCK * TOP_K + N_EXPERTS * (MOE_SEG - 1)) + N_EXPERTS * (tm - MOE_SEG) + tm - 1) // tm


def _slot_bounds(bounds):
    return bounds[0:1, :], bounds[1:2, :]


def _moe_gather_kernel(dst_ref, nchunk_ref, pad_row_ref, pad_n_ref,
                       x_ref, chosen_t_ref, rank_t_ref, bounds_ref, rows_ref, buf, zbuf, sem, zsem):
    j = pl.program_id(0)
    nblk = pl.num_programs(0)
    slot = j % 2
    nchunk_max = MOE_CAP // MOE_SEG

    def chunk_copy(blk, q, sl):
        src = buf.at[sl, pl.ds(pl.multiple_of(q * MOE_SEG, MOE_SEG), MOE_SEG), :]
        row = pl.multiple_of(dst_ref[blk * nchunk_max + q], MOE_SEG)
        return pltpu.make_async_copy(src, rows_ref.at[pl.ds(row, MOE_SEG), :], sem.at[sl])

    def pad_copy(e, i):
        row = pl.multiple_of(pad_row_ref[e] + i * MOE_SEG, MOE_SEG)
        return pltpu.make_async_copy(zbuf, rows_ref.at[pl.ds(row, MOE_SEG), :], zsem)

    def sub_chunks(blk, sub, sl, action):
        for q in range(sub * (MOE_SUB // MOE_SEG), (sub + 1) * (MOE_SUB // MOE_SEG)):
            action(chunk_copy(blk, q, sl))

    def sub_used(blk, sub):
        return True if sub < MOE_SUB_ALWAYS else sub * MOE_SUB < nchunk_ref[blk] * MOE_SEG

    def for_pads(action):
        def per_expert(e, carry):
            def body(i, c):
                action(pad_copy(e, i))
                return c
            return lax.fori_loop(0, pad_n_ref[e], body, carry)
        lax.fori_loop(0, N_EXPERTS, per_expert, 0)

    @pl.when(j == 0)
    def _():
        zbuf[...] = jnp.zeros_like(zbuf)
        for_pads(lambda c: c.start())

    start, length = _slot_bounds(bounds_ref[0])
    x = x_ref[...]
    chosen_rank_t = jnp.concatenate([chosen_t_ref[...], rank_t_ref[...]], axis=1)
    before = jnp.maximum(j - 2, 0)
    subs = range(MOE_CAP // MOE_SUB)

    @pl.when(j >= 2)
    def _():
        for sub in subs:
            @_when(sub_used(before, sub))
            def _():
                sub_chunks(before, sub, slot, lambda c: c.wait())

    def owners(sub):
        s = (sub * MOE_SUB + lax.broadcasted_iota(jnp.int32, (MOE_SUB, LANES), 0)).astype(F32)
        owner = (s >= start) & (s < start + length)
        within = s[:, 0:1] - jnp.sum(jnp.where(owner, start, 0.0), axis=1, keepdims=True)
        return owner.astype(BF16), within

    def picks(owner_within):
        owner, within = owner_within
        hit = _dot(owner, chosen_rank_t)
        return ((hit[:, :MOE_BLOCK] > 0.5) & (hit[:, MOE_BLOCK:] == within)).astype(BF16)

    def emit(sub, pick):
        buf[slot, sub * MOE_SUB:(sub + 1) * MOE_SUB, :] = _dot(pick, x).astype(BF16)
        sub_chunks(j, sub, slot, lambda c: c.start())

    staged = [picks(ow) for ow in [owners(sub) for sub in subs[:MOE_SUB_ALWAYS]]]
    for sub, pick in enumerate(staged):
        emit(sub, pick)
    for sub in subs[MOE_SUB_ALWAYS:]:
        @pl.when(sub_used(j, sub))
        def _():
            emit(sub, picks(owners(sub)))

    @pl.when(j == nblk - 1)
    def _():
        prev = jnp.maximum(j - 1, 0)
        for sub in subs:
            @_when(sub_used(j, sub))
            def _():
                sub_chunks(j, sub, slot, lambda c: c.wait())

            @_when((j >= 1) & sub_used(prev, sub))
            def _():
                sub_chunks(prev, sub, 1 - slot, lambda c: c.wait())

        for_pads(lambda c: c.wait())


def _moe_gather(x1b, chosen_t, rank_t, tables, tm):
    m = x1b.shape[0]
    nblk = m // MOE_BLOCK
    rows = _moe_max_tiles(nblk, tm) * tm + 2 * MOE_CAP
    blk = lambda j, *_: (j, 0)
    grid_spec = pltpu.PrefetchScalarGridSpec(
        num_scalar_prefetch=4,
        grid=(nblk,),
        in_specs=[pl.BlockSpec((MOE_BLOCK, D_MODEL), blk), pl.BlockSpec((LANES, MOE_BLOCK), blk),
                  pl.BlockSpec((LANES, MOE_BLOCK), blk),
                  pl.BlockSpec((1, SUBLANES, LANES), lambda j, *_: (j, 0, 0))],
        out_specs=pl.BlockSpec(memory_space=pl.ANY),
        scratch_shapes=[pltpu.VMEM((2, MOE_CAP, D_MODEL), BF16), pltpu.VMEM((MOE_SEG, D_MODEL), BF16),
                        pltpu.SemaphoreType.DMA((2,)), pltpu.SemaphoreType.DMA(())],
    )
    return pl.pallas_call(
        _moe_gather_kernel,
        grid_spec=grid_spec,
        out_shape=jax.ShapeDtypeStruct((rows, D_MODEL), BF16),
        compiler_params=_params("arbitrary"),
        name="moe_gather",
    )(tables["dst"], tables["nchunk"], tables["pad_row"], tables["pad_n"], x1b, chosen_t, rank_t, tables["bounds"])


def _moe_ffn_kernel(tile_expert_ref, tile_slot_ref, tile_next_ref, x_hbm, w1_hbm, w3_hbm, w2_hbm, y_ref,
                    x_buf, w1_buf, w3_buf, w2_buf, w13_scr, w2_scr, sem, x_sem):
    i = pl.program_id(0)
    n = pl.num_programs(0)
    expert = tile_expert_ref[i]
    slot = tile_slot_ref[i]
    ahead = MOE_ROW_SLOTS - 1
    tm = x_buf.shape[1]

    def row_copy(t):
        rows = pl.ds(pl.multiple_of(t * tm, tm), tm)
        return pltpu.make_async_copy(x_hbm.at[rows], x_buf.at[t % MOE_ROW_SLOTS], x_sem.at[t % MOE_ROW_SLOTS])

    @pl.when(i == 0)
    def _():
        for t in range(ahead):
            @pl.when(t < n)
            def _():
                row_copy(jnp.int32(t)).start()

    @pl.when(i + ahead < n)
    def _():
        row_copy(i + ahead).start()

    def weight_copies(e, s):
        pairs = ((w1_hbm, w1_buf), (w3_hbm, w3_buf), (w2_hbm, w2_buf))
        return [pltpu.make_async_copy(w.at[e], buf.at[s], sem.at[s, n]) for n, (w, buf) in enumerate(pairs)]

    @pl.when(i == 0)
    def _():
        for copy in weight_copies(expert, slot):
            copy.start()

    @pl.when((i == 0) | (expert != tile_expert_ref[jnp.maximum(i - 1, 0)]))
    def _():
        for copy in weight_copies(expert, slot):
            copy.wait()

        @pl.when(tile_next_ref[i] >= 0)
        def _():
            for copy in weight_copies(tile_next_ref[i], 1 - slot):
                copy.start()

        w13_scr[:, :EXPERT_FF] = w1_buf[slot].astype(BF16)
        w13_scr[:, EXPERT_FF:] = w3_buf[slot].astype(BF16)
        w2_scr[...] = w2_buf[slot].astype(BF16)

    row_copy(i).wait()
    up = _dot(x_buf[i % MOE_ROW_SLOTS], w13_scr[...])
    hid = _silu(up[:, :EXPERT_FF]) * up[:, EXPERT_FF:]
    y_ref[...] = _dot(hid.astype(BF16), w2_scr[...]).astype(BF16)


def _moe_ffn(rows, w1, w3, w2, tables, tm):
    grid_spec = pltpu.PrefetchScalarGridSpec(
        num_scalar_prefetch=3,
        grid=(tables["tiles_used"][0],),
        in_specs=[pl.BlockSpec(memory_space=pl.ANY)] * 4,
        out_specs=pl.BlockSpec((tm, D_MODEL), lambda i, *_: (i, 0)),
        scratch_shapes=[pltpu.VMEM((MOE_ROW_SLOTS, tm, D_MODEL), BF16),
                        pltpu.VMEM((2, D_MODEL, EXPERT_FF), F32), pltpu.VMEM((2, D_MODEL, EXPERT_FF), F32),
                        pltpu.VMEM((2, EXPERT_FF, D_MODEL), F32),
                        pltpu.VMEM((D_MODEL, 2 * EXPERT_FF), BF16), pltpu.VMEM((EXPERT_FF, D_MODEL), BF16),
                        pltpu.SemaphoreType.DMA((2, 3)), pltpu.SemaphoreType.DMA((MOE_ROW_SLOTS,))],
    )
    return pl.pallas_call(
        _moe_ffn_kernel,
        grid_spec=grid_spec,
        out_shape=jax.ShapeDtypeStruct(rows.shape, BF16),
        compiler_params=_params("arbitrary"),
        name="moe_ffn",
    )(tables["tile_expert"], tables["tile_slot"], tables["tile_next"], rows, w1, w3, w2)


def _moe_combine_kernel(dst_ref, nchunk_ref, dense_ref, gates_ref, rank_ref, bounds_ref, g_ref, b_ref, rows_ref,
                        op_ref, os_ref, acc, buf, sem, *, prompt_tiles):
    j = pl.program_id(0)
    nblk = pl.num_programs(0)
    slot = j % 2
    nchunk_max = MOE_CAP // MOE_SEG

    def chunk_copy(blk, q, sl):
        row = pl.multiple_of(dst_ref[blk * nchunk_max + q], MOE_SEG)
        dst = buf.at[sl, pl.ds(pl.multiple_of(q * MOE_SEG, MOE_SEG), MOE_SEG), :]
        return pltpu.make_async_copy(rows_ref.at[pl.ds(row, MOE_SEG), :], dst, sem.at[sl])

    subs = range(MOE_CAP // MOE_SUB)
    always, rest = subs[:MOE_SUB_ALWAYS], subs[MOE_SUB_ALWAYS:]

    def sub_used(blk, sub):
        return True if sub < MOE_SUB_ALWAYS else sub * MOE_SUB < nchunk_ref[blk] * MOE_SEG

    def group(blk, sub, sl, action):
        for q in range(sub * (MOE_SUB // MOE_SEG), (sub + 1) * (MOE_SUB // MOE_SEG)):
            action(chunk_copy(blk, q, sl))

    @pl.when(j == 0)
    def _():
        buf[...] = jnp.zeros_like(buf)
        for sub in subs:
            @_when(sub_used(0, sub))
            def _():
                group(0, sub, 0, lambda c: c.start())

    for sub in subs:
        @_when(sub_used(j, sub))
        def _():
            group(j, sub, slot, lambda c: c.wait())

    bounds_t = jnp.concatenate([bounds_ref[0], jnp.zeros((LANES - SUBLANES, LANES), F32)], axis=0).T
    start, length = bounds_t[:, 0:1], bounds_t[:, 1:2]
    gates_rank = jnp.concatenate([gates_ref[...], rank_ref[...]], axis=0)

    def owners(sub):
        s = (sub * MOE_SUB + lax.broadcasted_iota(jnp.int32, (LANES, MOE_SUB), 1)).astype(F32)
        owner = (s >= start) & (s < start + length)
        within = s[0:1, :] - jnp.sum(jnp.where(owner, start, 0.0), axis=0, keepdims=True)
        return owner.astype(BF16), within

    def weights(owner_within):
        owner, within = owner_within
        hit = _dot(gates_rank, owner)
        return jnp.where(hit[MOE_BLOCK:] == within, hit[:MOE_BLOCK], 0.0).astype(BF16)

    def apply(sub, weight):
        return _dot(weight, buf[slot, sub * MOE_SUB:(sub + 1) * MOE_SUB, :])

    nxt = jnp.minimum(j + 1, nblk - 1)
    staged = [owners(sub) for sub in always]
    for sub in always:
        group(nxt, sub, 1 - slot, lambda c: c.start())
    staged = [weights(ow) for ow in staged]
    total = dense_ref[...]
    for sub, weight in zip(always, staged):
        total = total + apply(sub, weight)
    acc[...] = total
    for sub in rest:
        @pl.when((j + 1 < nblk) & sub_used(nxt, sub))
        def _():
            group(nxt, sub, 1 - slot, lambda c: c.start())

        @pl.when(sub_used(j, sub))
        def _():
            acc[...] += apply(sub, weights(owners(sub)))

    @pl.when(j == nblk - 1)
    def _():
        for sub in always:
            group(nxt, sub, 1 - slot, lambda c: c.wait())

    @pl.when(j < prompt_tiles)
    def _():
        op_ref[...] = _layer_norm(acc[...], g_ref[...], b_ref[...])

    @pl.when(j >= prompt_tiles)
    def _():
        os_ref[...] = _layer_norm(acc[...], g_ref[...], b_ref[...])


def _moe_combine(rows, dense, gates, rank, tables, g, b, n_prompt):
    m = dense.shape[0]
    nblk = m // MOE_BLOCK
    prompt_tiles = n_prompt // MOE_BLOCK
    in_prompt, in_sample = _group_maps(prompt_tiles)
    blk = lambda j, *_: (j, 0)
    const = lambda j, *_: (0, 0)
    grid_spec = pltpu.PrefetchScalarGridSpec(
        num_scalar_prefetch=2,
        grid=(nblk,),
        in_specs=[pl.BlockSpec((MOE_BLOCK, D_MODEL), blk), pl.BlockSpec((MOE_BLOCK, LANES), blk),
                  pl.BlockSpec((MOE_BLOCK, LANES), blk),
                  pl.BlockSpec((1, SUBLANES, LANES), lambda j, *_: (j, 0, 0)),
                  pl.BlockSpec((1, D_MODEL), const), pl.BlockSpec((1, D_MODEL), const),
                  pl.BlockSpec(memory_space=pl.ANY)],
        out_specs=[pl.BlockSpec((MOE_BLOCK, D_MODEL), in_prompt), pl.BlockSpec((MOE_BLOCK, D_MODEL), in_sample)],
        scratch_shapes=[pltpu.VMEM((MOE_BLOCK, D_MODEL), F32), pltpu.VMEM((2, MOE_CAP, D_MODEL), BF16),
                        pltpu.SemaphoreType.DMA((2,))],
    )
    return pl.pallas_call(
        functools.partial(_moe_combine_kernel, prompt_tiles=prompt_tiles),
        grid_spec=grid_spec,
        out_shape=[jax.ShapeDtypeStruct((n_prompt, D_MODEL), F32),
                   jax.ShapeDtypeStruct((m - n_prompt, D_MODEL), F32)],
        compiler_params=_params("arbitrary"),
        name="moe_combine",
    )(tables["src"], tables["nchunk"], dense, gates, rank, tables["bounds"], g, b, rows)


def _row(v):
    return v.reshape(1, -1).astype(F32)


def _scan_layout(batch, seq, row0):
    t = math.gcd(seq, CHUNK)
    nchunk = seq // t
    chunks = SCAN_CHUNKS if nchunk % SCAN_CHUNKS == 0 else 1
    seqs = SCAN_SEQS if (nchunk == 1 and batch % SCAN_SEQS == 0 and row0 % (SCAN_SEQS * t) == 0) else 1
    return dict(batch=batch, seq=seq, row0=row0, t=t, seqs=seqs, chunks=chunks)


def _lanes_at(v, start):
    return jnp.zeros((1, LANES), F32).at[0, start:start + v.shape[0]].set(v.astype(F32))


def kernel(x_prompt, x_sample, p_prompt, p_sample, state_ssm_conv, state_ssm, state_dn_conv, state_dn, emb_ln_g, emb_ln_b, w_in, conv_a_w, conv_a_b, ssm_dt_bias, ssm_a_log, ssm_d, ssm_norm_w, w_a, conv_b_w, dn_dt_bias, dn_a_log, dn_norm_w, w_b, w_o, ln1_g, ln1_b, router_w, router_bias, exp_w1, exp_w3, exp_w2, sh_w1, sh_w3, sh_w2, ple_w, ple_gate_w, ln2_g, ln2_b):
    bp, lp, _ = x_prompt.shape
    bs, ls, _ = x_sample.shape
    n_p = bp * lp
    n_s = bs * ls
    xp = x_prompt.reshape(n_p, D_MODEL)
    xs = x_sample.reshape(n_s, D_MODEL)

    w = w_in[0]
    o_z, o_xbc, o_dt = 0, SSM_INNER, SSM_INNER + SSM_CONV_CH
    o_qkv = o_dt + SSM_HEADS
    o_a = o_qkv + DN_CONV_CH
    o_b = o_a + DN_HEADS
    o_zb = o_b + DN_HEADS
    o_ga = o_zb + DN_V
    w_pa = w[:, o_z:o_dt].astype(BF16)
    w_pb = jnp.concatenate([w[:, o_qkv:o_a], w[:, o_zb:o_ga]], axis=1).astype(BF16)
    w_pg = w[:, o_ga:].astype(BF16)
    w_ps = jnp.zeros((D_MODEL, LANES), F32)
    w_ps = w_ps.at[:, SMALL_DT:SMALL_DT + SSM_HEADS].set(w[:, o_dt:o_qkv])
    w_ps = w_ps.at[:, SMALL_A:SMALL_A + 2 * DN_HEADS].set(w[:, o_a:o_zb]).astype(BF16)

    eg, eb = _row(emb_ln_g), _row(emb_ln_b)
    proj_a, small = _ln_matmul(xp, xs, eg, eb, w_pa, PROJ_TM, PROJ_TN_A, w_ps)
    proj_b = _ln_matmul(xp, xs, eg, eb, w_pb, PROJ_TM, PROJ_TN_B)

    ssd_consts = [conv_a_w[0], _row(conv_a_b[0]), _lanes_at(ssm_dt_bias[0], SMALL_DT),
                  _lanes_at(ssm_a_log[0], SMALL_DT), _row(jnp.repeat(ssm_d[0], SSM_HEAD_DIM)),
                  _row(ssm_norm_w[0])]
    gdn_consts = [conv_b_w[0], _lanes_at(dn_a_log[0], SMALL_A), _lanes_at(dn_dt_bias[0], SMALL_A),
                  _row(dn_norm_w[0])]
    zeros = lambda *s: jnp.zeros(s, F32)
    prompt = _scan_layout(bp, lp, 0)
    sample = _scan_layout(bs, ls, n_p)

    ya_p, pa_conv, pa_ssm = _ssd(proj_a, small, ssd_consts, zeros(bp, CONV_WIDTH - 1, SSM_CONV_CH),
                                 zeros(bp, SSM_HEADS, SSM_HEAD_DIM, SSM_STATE), **prompt)
    ya_s, sa_conv, sa_ssm = _ssd(proj_a, small, ssd_consts, state_ssm_conv[0], state_ssm[0], **sample)
    ob_p, pb_conv, pb_dn = _gdn(proj_b, small, gdn_consts, zeros(bp, CONV_WIDTH - 1, DN_CONV_CH),
                                zeros(bp, DN_HEADS, DN_HEAD, DN_HEAD), **prompt)
    ob_s, sb_conv, sb_dn = _gdn(proj_b, small, gdn_consts, state_dn_conv[0], state_dn[0], **sample)

    x1, x1b = _merge(
        xp, xs, ya_p, ya_s, ob_p, ob_s,
        [eg, eb, w_pg, w_a[0].astype(BF16), w_b[0].astype(BF16), w_o[0].astype(BF16), _row(ln1_g[0]), _row(ln1_b[0])],
        MERGE_TM)

    router_w_t = jnp.zeros((LANES, D_MODEL), F32).at[:N_EXPERTS].set(router_w[0].T)
    router_w_hi = router_w_t.astype(BF16)
    router_w_lo = (router_w_t - router_w_hi.astype(F32)).astype(BF16)
    router_b = jnp.broadcast_to(_lanes_at(router_bias[0], 0).reshape(LANES, 1), (LANES, MOE_BLOCK))
    dense, gates, rank, chosen_t, rank_t, counts = _dense_router(
        x1, x1b, p_prompt[0].reshape(n_p, PLE_DIM), p_sample[0].reshape(n_s, PLE_DIM),
        [sh_w1[0].astype(BF16), sh_w3[0].astype(BF16), sh_w2[0].astype(BF16), ple_w[0].astype(BF16),
         ple_gate_w[0].astype(BF16), router_w_hi, router_w_lo, router_b])

    tables = _moe_tables(counts[:, 0, :N_EXPERTS].astype(jnp.int32), MOE_TM)
    sorted_rows = _moe_gather(x1b, chosen_t, rank_t, tables, MOE_TM)
    expert_out = _moe_ffn(sorted_rows, exp_w1[0], exp_w3[0], exp_w2[0], tables, MOE_TM)
    out_p, out_s = _moe_combine(expert_out, dense, gates, rank, tables, _row(ln2_g[0]), _row(ln2_b[0]), n_p)

    return (out_p.reshape(bp, lp, D_MODEL), out_s.reshape(bs, ls, D_MODEL),
            pa_conv[None], pa_ssm[None], pb_conv[None], pb_dn[None],
            sa_conv[None], sa_ssm[None], sb_conv[None], sb_dn[None])
```
